```python
import jax, jax.numpy as jnp
from jax import lax
import numpy as np

D_MODEL = 1024
BATCH = 8
SEQ = 8192
DEPTH = 2

CHUNK = 64
N_META = 16
D_MIX = D_MODEL
D_POOL = D_MIX // 4
D_CONV = D_MIX // 4
D_RET = D_MIX - D_POOL - D_CONV
POOL_WINDOWS = (2, 4, 8, 16)
N_POOL_GROUPS = len(POOL_WINDOWS)
POOL_GROUP = D_POOL // N_POOL_GROUPS
CONV_WIDTH = 31
RET_HEADS = 4
RET_HEAD_DIM = D_RET // RET_HEADS
ROPE_BASE = 10000.0
D_FF = ((8 * D_MODEL // 3 + 63) // 64) * 64
D_IN = D_POOL + 2 * D_CONV + 4 * D_RET
DEEPNORM_ALPHA = (2.0 * DEPTH) ** 0.25
DEEPNORM_BETA = (8.0 * DEPTH) ** -0.25
LN_EPS = 1e-5

kernel_name = "hybrid_pool_conv_retention_deepnorm_trunk"


def layer_norm(x, g, b):
    xf = x.astype(jnp.float32)
    mu = jnp.mean(xf, axis=-1, keepdims=True)
    var = jnp.mean(jnp.square(xf - mu), axis=-1, keepdims=True)
    return ((xf - mu) * lax.rsqrt(var + LN_EPS) * g + b).astype(x.dtype)


def swiglu_ffn(x, w13, w2):
    a, u = jnp.split(x @ w13, 2, axis=-1)
    return (jax.nn.silu(a) * u) @ w2


def pool_mixer(xp, w_pool, scale):
    B, L, _ = xp.shape
    xf = xp.astype(jnp.float32)
    cs = jnp.concatenate([jnp.zeros((B, 1, D_POOL), jnp.float32), jnp.cumsum(xf, axis=1)], axis=1)
    t = jnp.arange(L)
    outs = []
    for gi, w in enumerate(POOL_WINDOWS):
        lo, hi = gi * POOL_GROUP, (gi + 1) * POOL_GROUP
        csg = cs[..., lo:hi]
        start = jnp.maximum(t + 1 - w, 0)
        win_sum = csg[:, 1:] - csg[:, start]
        count = (t + 1 - start).astype(jnp.float32)
        outs.append(win_sum / count[None, :, None] - xf[..., lo:hi])
    y = jnp.stack(outs, axis=2)
    y = jnp.einsum('blgc,gcd->blgd', y, w_pool.astype(jnp.float32)).reshape(B, L, D_POOL)
    return (y * scale).astype(xp.dtype)


def conv_module(a, gate, w_dw, b_dw, ln_g, ln_b, w_pw):
    u = a * jax.nn.sigmoid(gate)
    y = lax.conv_general_dilated(u, w_dw[:, None, :], window_strides=(1,),
                                 padding=[(CONV_WIDTH - 1, 0)],
                                 dimension_numbers=('NWC', 'WIO', 'NWC'),
                                 feature_group_count=D_CONV) + b_dw
    y = jax.nn.silu(layer_norm(y, ln_g, ln_b))
    return y @ w_pw


def rope(x, cos, sin):
    x1, x2 = jnp.split(x, 2, axis=-1)
    return jnp.concatenate([x1 * cos - x2 * sin, x2 * cos + x1 * sin], axis=-1)


def retention(q, k, v, g, gn_g):
    B, L, _ = q.shape
    f32 = jnp.float32
    pos = jnp.arange(L, dtype=f32)
    inv_freq = ROPE_BASE ** (-jnp.arange(0, RET_HEAD_DIM, 2, dtype=f32) / RET_HEAD_DIM)
    ang = pos[:, None] * inv_freq[None, :]
    cos, sin = jnp.cos(ang), jnp.sin(ang)
    heads = lambda t: t.astype(f32).reshape(B, L, RET_HEADS, RET_HEAD_DIM).transpose(0, 2, 1, 3)
    qh = rope(heads(q), cos, sin)
    kh = rope(heads(k), cos, sin) * (RET_HEAD_DIM ** -0.5)
    vh = heads(v)
    P = (-N_META) % CHUNK
    NC = (L + P) // CHUNK
    chunk = lambda t: jnp.pad(t, ((0, 0), (0, 0), (P, 0), (0, 0))).reshape(B, RET_HEADS, NC, CHUNK, RET_HEAD_DIM)
    qc, kc, vc = chunk(qh), chunk(kh), chunk(vh)
    log_gamma = jnp.log(1.0 - 2.0 ** (-5.0 - jnp.arange(RET_HEADS, dtype=f32)))
    i = jnp.arange(CHUNK, dtype=f32)
    intra_decay = jnp.exp(log_gamma[:, None, None] * jnp.abs(i[:, None] - i[None, :]))
    s = jnp.einsum('bhnid,bhnjd->bhnij', qc, kc) * intra_decay[None, :, None]
    o_intra = jnp.einsum('bhnij,bhnjd->bhnid', s, vc)
    q_decay = jnp.exp(log_gamma[:, None] * (i + 1.0))[None, :, :, None]
    k_decay = jnp.exp(log_gamma[:, None] * (CHUNK - 1.0 - i))[None, :, :, None]
    chunk_decay = jnp.exp(log_gamma * CHUNK)[None, :, None, None]

    def step(state, inp):
        q_n, k_n, v_n = inp
        o = jnp.einsum('bhid,bhde->bhie', q_n * q_decay, state)
        state = state * chunk_decay + jnp.einsum('bhjd,bhje->bhde', k_n * k_decay, v_n)
        return state, o

    to_scan = lambda t: t.transpose(2, 0, 1, 3, 4)
    state0 = jnp.zeros((B, RET_HEADS, RET_HEAD_DIM, RET_HEAD_DIM), f32)
    _, o_cross = lax.scan(step, state0, (to_scan(qc), to_scan(kc), to_scan(vc)))
    o = o_intra + o_cross.transpose(1, 2, 0, 3, 4)
    o = o.reshape(B, RET_HEADS, NC * CHUNK, RET_HEAD_DIM)[:, :, P:]
    mu = jnp.mean(o, axis=-1, keepdims=True)
    var = jnp.mean(jnp.square(o - mu), axis=-1, keepdims=True)
    o = ((o - mu) * lax.rsqrt(var + LN_EPS)).transpose(0, 2, 1, 3).reshape(B, L, D_RET) * gn_g
    return (jax.nn.silu(g.astype(f32)) * o).astype(g.dtype)


def token_mix(h, w_in, pool_w, pool_scale, conv_dw, conv_db, conv_ln_g, conv_ln_b, conv_pw, ret_gn_g, w_out):
    z = h @ w_in
    splits = [D_POOL, D_POOL + D_CONV, D_POOL + 2 * D_CONV,
              D_POOL + 2 * D_CONV + D_RET, D_POOL + 2 * D_CONV + 2 * D_RET,
              D_POOL + 2 * D_CONV + 3 * D_RET]
    xp, ca, cg, q, k, v, g = jnp.split(z, splits, axis=-1)
    y_pool = pool_mixer(xp, pool_w, pool_scale)
    y_conv = conv_module(ca, cg, conv_dw, conv_db, conv_ln_g, conv_ln_b, conv_pw)
    y_ret = retention(q, k, v, g, ret_gn_g)
    return jnp.concatenate([y_pool, y_conv, y_ret], axis=-1) @ w_out


def _fwd_setup_inputs(seed: int = 0) -> dict:
    key = jax.random.key(seed)
    ks = jax.random.split(key, 24)
    nrm = lambda k, shape, s: jax.random.normal(k, shape, jnp.float32) * s
    ones_n = lambda k, shape: 1.0 + 0.05 * jax.random.normal(k, shape, jnp.float32)
    return {
        "x": nrm(ks[0], (BATCH, SEQ, D_MODEL), 1.0),
        "meta": nrm(ks[1], (N_META, D_MODEL), 1.0),
        "ln_in_g": ones_n(ks[2], (D_MODEL,)),
        "ln_in_b": nrm(ks[3], (D_MODEL,), 0.02),
        "ffn1_w13": nrm(ks[4], (DEPTH, D_MODEL, 2 * D_FF), D_MODEL ** -0.5),
        "ffn1_w2": nrm(ks[5], (DEPTH, D_FF, D_MODEL), DEEPNORM_BETA * D_FF ** -0.5),
        "w_in": nrm(ks[6], (DEPTH, D_MODEL, D_IN), D_MODEL ** -0.5),
        "pool_w": nrm(ks[7], (DEPTH, N_POOL_GROUPS, POOL_GROUP, POOL_GROUP), POOL_GROUP ** -0.5),
        "pool_scale": ones_n(ks[8], (DEPTH, D_POOL)),
        "conv_dw": nrm(ks[9], (DEPTH, CONV_WIDTH, D_CONV), CONV_WIDTH ** -0.5),
        "conv_db": nrm(ks[10], (DEPTH, D_CONV), 0.02),
        "conv_ln_g": ones_n(ks[11], (DEPTH, D_CONV)),
        "conv_ln_b": nrm(ks[12], (DEPTH, D_CONV), 0.02),
        "conv_pw": nrm(ks[13], (DEPTH, D_CONV, D_CONV), D_CONV ** -0.5),
        "ret_gn_g": ones_n(ks[14], (DEPTH, D_RET)),
        "w_out": nrm(ks[15], (DEPTH, D_MIX, D_MODEL), DEEPNORM_BETA * D_MIX ** -0.5),
        "ffn2_w13": nrm(ks[16], (DEPTH, D_MODEL, 2 * D_FF), D_MODEL ** -0.5),
        "ffn2_w2": nrm(ks[17], (DEPTH, D_FF, D_MODEL), DEEPNORM_BETA * D_FF ** -0.5),
        "ln_g": ones_n(ks[18], (DEPTH, 3, D_MODEL)),
        "ln_b": nrm(ks[19], (DEPTH, 3, D_MODEL), 0.02),
    }


def _fwd_reference(x, meta, ln_in_g, ln_in_b, ffn1_w13, ffn1_w2, w_in, pool_w, pool_scale,
              conv_dw, conv_db, conv_ln_g, conv_ln_b, conv_pw, ret_gn_g, w_out,
              ffn2_w13, ffn2_w2, ln_g, ln_b):
    B = x.shape[0]
    h = jnp.concatenate([jnp.broadcast_to(meta[None].astype(x.dtype), (B, N_META, D_MODEL)), x], axis=1)
    h = layer_norm(h, ln_in_g, ln_in_b)
    for l in range(DEPTH):
        h = layer_norm(DEEPNORM_ALPHA * h + 0.5 * swiglu_ffn(h, ffn1_w13[l], ffn1_w2[l]), ln_g[l, 0], ln_b[l, 0])
        mix = token_mix(h, w_in[l], pool_w[l], pool_scale[l], conv_dw[l], conv_db[l],
                        conv_ln_g[l], conv_ln_b[l], conv_pw[l], ret_gn_g[l], w_out[l])
        h = layer_norm(DEEPNORM_ALPHA * h + mix, ln_g[l, 1], ln_b[l, 1])
        h = layer_norm(DEEPNORM_ALPHA * h + 0.5 * swiglu_ffn(h, ffn2_w13[l], ffn2_w2[l]), ln_g[l, 2], ln_b[l, 2])
    return h[:, N_META:]


import jax as _jax
import jax.numpy as _jnp

TWIN_FORMAT = 'train_step'
FWD_PARAMS = ['x', 'meta', 'ln_in_g', 'ln_in_b', 'ffn1_w13', 'ffn1_w2', 'w_in', 'pool_w', 'pool_scale', 'conv_dw', 'conv_db', 'conv_ln_g', 'conv_ln_b', 'conv_pw', 'ret_gn_g', 'w_out', 'ffn2_w13', 'ffn2_w2', 'ln_g', 'ln_b']
TWIN_WEIGHTS = ['meta', 'ln_in_g', 'ln_in_b', 'ffn1_w13', 'ffn1_w2', 'w_in', 'pool_w', 'pool_scale', 'conv_dw', 'conv_db', 'conv_ln_g', 'conv_ln_b', 'conv_pw', 'ret_gn_g', 'w_out', 'ffn2_w13', 'ffn2_w2', 'ln_g', 'ln_b']
TWIN_DIFF_INPUT = 'x'
TWIN_INPUTS = ['x', 'meta', 'ln_in_g', 'ln_in_b', 'ffn1_w13', 'ffn1_w2', 'w_in', 'pool_w', 'pool_scale', 'conv_dw', 'conv_db', 'conv_ln_g', 'conv_ln_b', 'conv_pw', 'ret_gn_g', 'w_out', 'ffn2_w13', 'ffn2_w2', 'ln_g', 'ln_b', 'loss_target', 'm_meta', 'm_ln_in_g', 'm_ln_in_b', 'm_ffn1_w13', 'm_ffn1_w2', 'm_w_in', 'm_pool_w', 'm_pool_scale', 'm_conv_dw', 'm_conv_db', 'm_conv_ln_g', 'm_conv_ln_b', 'm_conv_pw', 'm_ret_gn_g', 'm_w_out', 'm_ffn2_w13', 'm_ffn2_w2', 'm_ln_g', 'm_ln_b', 'v_meta', 'v_ln_in_g', 'v_ln_in_b', 'v_ffn1_w13', 'v_ffn1_w2', 'v_w_in', 'v_pool_w', 'v_pool_scale', 'v_conv_dw', 'v_conv_db', 'v_conv_ln_g', 'v_conv_ln_b', 'v_conv_pw', 'v_ret_gn_g', 'v_w_out', 'v_ffn2_w13', 'v_ffn2_w2', 'v_ln_g', 'v_ln_b']
TWIN_OUTPUTS = ['loss', 'grad_x', 'grad_meta', 'grad_ln_in_g', 'grad_ln_in_b', 'grad_ffn1_w13', 'grad_ffn1_w2', 'grad_w_in', 'grad_pool_w', 'grad_pool_scale', 'grad_conv_dw', 'grad_conv_db', 'grad_conv_ln_g', 'grad_conv_ln_b', 'grad_conv_pw', 'grad_ret_gn_g', 'grad_w_out', 'grad_ffn2_w13', 'grad_ffn2_w2', 'grad_ln_g', 'grad_ln_b', 'delta_meta', 'delta_ln_in_g', 'delta_ln_in_b', 'delta_ffn1_w13', 'delta_ffn1_w2', 'delta_w_in', 'delta_pool_w', 'delta_pool_scale', 'delta_conv_dw', 'delta_conv_db', 'delta_conv_ln_g', 'delta_conv_ln_b', 'delta_conv_pw', 'delta_ret_gn_g', 'delta_w_out', 'delta_ffn2_w13', 'delta_ffn2_w2', 'delta_ln_g', 'delta_ln_b', 'new_m_meta', 'new_m_ln_in_g', 'new_m_ln_in_b', 'new_m_ffn1_w13', 'new_m_ffn1_w2', 'new_m_w_in', 'new_m_pool_w', 'new_m_pool_scale', 'new_m_conv_dw', 'new_m_conv_db', 'new_m_conv_ln_g', 'new_m_conv_ln_b', 'new_m_conv_pw', 'new_m_ret_gn_g', 'new_m_w_out', 'new_m_ffn2_w13', 'new_m_ffn2_w2', 'new_m_ln_g', 'new_m_ln_b', 'new_v_meta', 'new_v_ln_in_g', 'new_v_ln_in_b', 'new_v_ffn1_w13', 'new_v_ffn1_w2', 'new_v_w_in', 'new_v_pool_w', 'new_v_pool_scale', 'new_v_conv_dw', 'new_v_conv_db', 'new_v_conv_ln_g', 'new_v_conv_ln_b', 'new_v_conv_pw', 'new_v_ret_gn_g', 'new_v_w_out', 'new_v_ffn2_w13', 'new_v_ffn2_w2', 'new_v_ln_g', 'new_v_ln_b']
TWIN_LEAF_KINDS = {'loss': 'loss', 'grad_x': 'grad_x', 'grad_meta': 'grad_w', 'grad_ln_in_g': 'grad_w', 'grad_ln_in_b': 'grad_w', 'grad_ffn1_w13': 'grad_w', 'grad_ffn1_w2': 'grad_w', 'grad_w_in': 'grad_w', 'grad_pool_w': 'grad_w', 'grad_pool_scale': 'grad_w', 'grad_conv_dw': 'grad_w', 'grad_conv_db': 'grad_w', 'grad_conv_ln_g': 'grad_w', 'grad_conv_ln_b': 'grad_w', 'grad_conv_pw': 'grad_w', 'grad_ret_gn_g': 'grad_w', 'grad_w_out': 'grad_w', 'grad_ffn2_w13': 'grad_w', 'grad_ffn2_w2': 'grad_w', 'grad_ln_g': 'grad_w', 'grad_ln_b': 'grad_w', 'delta_meta': 'delta_w', 'delta_ln_in_g': 'delta_w', 'delta_ln_in_b': 'delta_w', 'delta_ffn1_w13': 'delta_w', 'delta_ffn1_w2': 'delta_w', 'delta_w_in': 'delta_w', 'delta_pool_w': 'delta_w', 'delta_pool_scale': 'delta_w', 'delta_conv_dw': 'delta_w', 'delta_conv_db': 'delta_w', 'delta_conv_ln_g': 'delta_w', 'delta_conv_ln_b': 'delta_w', 'delta_conv_pw': 'delta_w', 'delta_ret_gn_g': 'delta_w', 'delta_w_out': 'delta_w', 'delta_ffn2_w13': 'delta_w', 'delta_ffn2_w2': 'delta_w', 'delta_ln_g': 'delta_w', 'delta_ln_b': 'delta_w', 'new_m_meta': 'new_m', 'new_m_ln_in_g': 'new_m', 'new_m_ln_in_b': 'new_m', 'new_m_ffn1_w13': 'new_m', 'new_m_ffn1_w2': 'new_m', 'new_m_w_in': 'new_m', 'new_m_pool_w': 'new_m', 'new_m_pool_scale': 'new_m', 'new_m_conv_dw': 'new_m', 'new_m_conv_db': 'new_m', 'new_m_conv_ln_g': 'new_m', 'new_m_conv_ln_b': 'new_m', 'new_m_conv_pw': 'new_m', 'new_m_ret_gn_g': 'new_m', 'new_m_w_out': 'new_m', 'new_m_ffn2_w13': 'new_m', 'new_m_ffn2_w2': 'new_m', 'new_m_ln_g': 'new_m', 'new_m_ln_b': 'new_m', 'new_v_meta': 'new_v', 'new_v_ln_in_g': 'new_v', 'new_v_ln_in_b': 'new_v', 'new_v_ffn1_w13': 'new_v', 'new_v_ffn1_w2': 'new_v', 'new_v_w_in': 'new_v', 'new_v_pool_w': 'new_v', 'new_v_pool_scale': 'new_v', 'new_v_conv_dw': 'new_v', 'new_v_conv_db': 'new_v', 'new_v_conv_ln_g': 'new_v', 'new_v_conv_ln_b': 'new_v', 'new_v_conv_pw': 'new_v', 'new_v_ret_gn_g': 'new_v', 'new_v_w_out': 'new_v', 'new_v_ffn2_w13': 'new_v', 'new_v_ffn2_w2': 'new_v', 'new_v_ln_g': 'new_v', 'new_v_ln_b': 'new_v'}


def _forward(args):
    return _fwd_reference(*[args[k] for k in FWD_PARAMS])


def _output_shape():
    def fwd():
        inp = _fwd_setup_inputs(0)
        return _fwd_reference(*[inp[k] for k in FWD_PARAMS])
    out = _jax.eval_shape(fwd)
    return out.shape, out.dtype

N_MICROBATCH = 1
ADAM_LR = 0.001
ADAM_B1 = 0.9
ADAM_B2 = 0.999
ADAM_EPS = 1e-08
ADAM_WD = 0.01
ADAM_STEP = 10
PER_EXAMPLE_BATCH_AXIS = {'x': 0, 'loss_target': 0}
SHARED_INPUTS = []
_WEIGHT_DTYPES = {'meta': _jnp.float32, 'ln_in_g': _jnp.float32, 'ln_in_b': _jnp.float32, 'ffn1_w13': _jnp.float32, 'ffn1_w2': _jnp.float32, 'w_in': _jnp.float32, 'pool_w': _jnp.float32, 'pool_scale': _jnp.float32, 'conv_dw': _jnp.float32, 'conv_db': _jnp.float32, 'conv_ln_g': _jnp.float32, 'conv_ln_b': _jnp.float32, 'conv_pw': _jnp.float32, 'ret_gn_g': _jnp.float32, 'w_out': _jnp.float32, 'ffn2_w13': _jnp.float32, 'ffn2_w2': _jnp.float32, 'ln_g': _jnp.float32, 'ln_b': _jnp.float32}
MOMENT_SCALE = {'meta': 4.920497e-03, 'ln_in_g': 5.202072e+00, 'ln_in_b': 9.721490e-01, 'ffn1_w13': 1.746946e-02, 'ffn1_w2': 5.660636e-02, 'w_in': 5.537476e-02, 'pool_w': 8.313558e-02, 'pool_scale': 7.952892e-02, 'conv_dw': 5.760968e-02, 'conv_db': 3.250304e-01, 'conv_ln_g': 1.118537e-01, 'conv_ln_b': 1.754228e-01, 'conv_pw': 7.823217e-02, 'ret_gn_g': 5.546362e-02, 'w_out': 1.339340e-01, 'ffn2_w13': 1.684356e-02, 'ffn2_w2': 5.459309e-02, 'ln_g': 2.731847e+01, 'ln_b': 1.881064e+00}


def _to_microbatches(a, axis):
    t = _jnp.moveaxis(a, axis, 0)
    t = t.reshape((N_MICROBATCH, t.shape[0] // N_MICROBATCH) + t.shape[1:])
    return _jnp.moveaxis(t, 1, axis + 1)


def setup_inputs(seed: int = 0) -> dict:
    inp = _fwd_setup_inputs(seed)
    key = _jax.random.fold_in(_jax.random.key(seed), 7919)
    shape, _ = _output_shape()
    out = dict(inp)
    out["loss_target"] = _jax.random.normal(_jax.random.fold_in(key, 0), shape, _jnp.float32)
    for i, name in enumerate(TWIN_WEIGHTS):
        w = inp[name].astype(_jnp.float32)
        if MOMENT_SCALE is None:
            s = _jnp.sqrt(_jnp.mean(_jnp.square(w)) + 1e-30)
        else:
            s = MOMENT_SCALE[name]
        km, kv = _jax.random.split(_jax.random.fold_in(key, i + 1))
        out[name] = w
        out["m_" + name] = s * _jax.random.normal(km, w.shape, _jnp.float32)
        out["v_" + name] = (s * s) * _jax.random.uniform(kv, w.shape, _jnp.float32, 0.5, 1.5)
    if N_MICROBATCH > 1:
        for name, axis in PER_EXAMPLE_BATCH_AXIS.items():
            out[name] = _to_microbatches(out[name], axis)
    return {'x': out['x'], 'meta': out['meta'], 'ln_in_g': out['ln_in_g'], 'ln_in_b': out['ln_in_b'], 'ffn1_w13': out['ffn1_w13'], 'ffn1_w2': out['ffn1_w2'], 'w_in': out['w_in'], 'pool_w': out['pool_w'], 'pool_scale': out['pool_scale'], 'conv_dw': out['conv_dw'], 'conv_db': out['conv_db'], 'conv_ln_g': out['conv_ln_g'], 'conv_ln_b': out['conv_ln_b'], 'conv_pw': out['conv_pw'], 'ret_gn_g': out['ret_gn_g'], 'w_out': out['w_out'], 'ffn2_w13': out['ffn2_w13'], 'ffn2_w2': out['ffn2_w2'], 'ln_g': out['ln_g'], 'ln_b': out['ln_b'], 'loss_target': out['loss_target'], 'm_meta': out['m_meta'], 'm_ln_in_g': out['m_ln_in_g'], 'm_ln_in_b': out['m_ln_in_b'], 'm_ffn1_w13': out['m_ffn1_w13'], 'm_ffn1_w2': out['m_ffn1_w2'], 'm_w_in': out['m_w_in'], 'm_pool_w': out['m_pool_w'], 'm_pool_scale': out['m_pool_scale'], 'm_conv_dw': out['m_conv_dw'], 'm_conv_db': out['m_conv_db'], 'm_conv_ln_g': out['m_conv_ln_g'], 'm_conv_ln_b': out['m_conv_ln_b'], 'm_conv_pw': out['m_conv_pw'], 'm_ret_gn_g': out['m_ret_gn_g'], 'm_w_out': out['m_w_out'], 'm_ffn2_w13': out['m_ffn2_w13'], 'm_ffn2_w2': out['m_ffn2_w2'], 'm_ln_g': out['m_ln_g'], 'm_ln_b': out['m_ln_b'], 'v_meta': out['v_meta'], 'v_ln_in_g': out['v_ln_in_g'], 'v_ln_in_b': out['v_ln_in_b'], 'v_ffn1_w13': out['v_ffn1_w13'], 'v_ffn1_w2': out['v_ffn1_w2'], 'v_w_in': out['v_w_in'], 'v_pool_w': out['v_pool_w'], 'v_pool_scale': out['v_pool_scale'], 'v_conv_dw': out['v_conv_dw'], 'v_conv_db': out['v_conv_db'], 'v_conv_ln_g': out['v_conv_ln_g'], 'v_conv_ln_b': out['v_conv_ln_b'], 'v_conv_pw': out['v_conv_pw'], 'v_ret_gn_g': out['v_ret_gn_g'], 'v_w_out': out['v_w_out'], 'v_ffn2_w13': out['v_ffn2_w13'], 'v_ffn2_w2': out['v_ffn2_w2'], 'v_ln_g': out['v_ln_g'], 'v_ln_b': out['v_ln_b']}


def _loss(weights, diff, rest, loss_target):
    with _jax.named_scope("forward"):
        args = {**rest, TWIN_DIFF_INPUT: diff, **{k: w.astype(_WEIGHT_DTYPES[k]) for k, w in weights.items()}}
        y = _forward(args)
    with _jax.named_scope("loss_head"):
        err = _jnp.square(y.astype(_jnp.float32) - loss_target)
        return 0.5 * _jnp.sum(_jnp.mean(err, axis=-1)) if err.ndim else 0.5 * err


def _adamw(w, g, m, v):
    m = ADAM_B1 * m + (1.0 - ADAM_B1) * g
    v = ADAM_B2 * v + (1.0 - ADAM_B2) * _jnp.square(g)
    m_hat = m / (1.0 - ADAM_B1 ** ADAM_STEP)
    v_hat = v / (1.0 - ADAM_B2 ** ADAM_STEP)
    delta = -ADAM_LR * (m_hat / (_jnp.sqrt(v_hat) + ADAM_EPS) + ADAM_WD * w)
    return delta, m, v


def reference(x, meta, ln_in_g, ln_in_b, ffn1_w13, ffn1_w2, w_in, pool_w, pool_scale, conv_dw, conv_db, conv_ln_g, conv_ln_b, conv_pw, ret_gn_g, w_out, ffn2_w13, ffn2_w2, ln_g, ln_b, loss_target, m_meta, m_ln_in_g, m_ln_in_b, m_ffn1_w13, m_ffn1_w2, m_w_in, m_pool_w, m_pool_scale, m_conv_dw, m_conv_db, m_conv_ln_g, m_conv_ln_b, m_conv_pw, m_ret_gn_g, m_w_out, m_ffn2_w13, m_ffn2_w2, m_ln_g, m_ln_b, v_meta, v_ln_in_g, v_ln_in_b, v_ffn1_w13, v_ffn1_w2, v_w_in, v_pool_w, v_pool_scale, v_conv_dw, v_conv_db, v_conv_ln_g, v_conv_ln_b, v_conv_pw, v_ret_gn_g, v_w_out, v_ffn2_w13, v_ffn2_w2, v_ln_g, v_ln_b):
    given = dict(x=x, meta=meta, ln_in_g=ln_in_g, ln_in_b=ln_in_b, ffn1_w13=ffn1_w13, ffn1_w2=ffn1_w2, w_in=w_in, pool_w=pool_w, pool_scale=pool_scale, conv_dw=conv_dw, conv_db=conv_db, conv_ln_g=conv_ln_g, conv_ln_b=conv_ln_b, conv_pw=conv_pw, ret_gn_g=ret_gn_g, w_out=w_out, ffn2_w13=ffn2_w13, ffn2_w2=ffn2_w2, ln_g=ln_g, ln_b=ln_b, loss_target=loss_target, m_meta=m_meta, m_ln_in_g=m_ln_in_g, m_ln_in_b=m_ln_in_b, m_ffn1_w13=m_ffn1_w13, m_ffn1_w2=m_ffn1_w2, m_w_in=m_w_in, m_pool_w=m_pool_w, m_pool_scale=m_pool_scale, m_conv_dw=m_conv_dw, m_conv_db=m_conv_db, m_conv_ln_g=m_conv_ln_g, m_conv_ln_b=m_conv_ln_b, m_conv_pw=m_conv_pw, m_ret_gn_g=m_ret_gn_g, m_w_out=m_w_out, m_ffn2_w13=m_ffn2_w13, m_ffn2_w2=m_ffn2_w2, m_ln_g=m_ln_g, m_ln_b=m_ln_b, v_meta=v_meta, v_ln_in_g=v_ln_in_g, v_ln_in_b=v_ln_in_b, v_ffn1_w13=v_ffn1_w13, v_ffn1_w2=v_ffn1_w2, v_w_in=v_w_in, v_pool_w=v_pool_w, v_pool_scale=v_pool_scale, v_conv_dw=v_conv_dw, v_conv_db=v_conv_db, v_conv_ln_g=v_conv_ln_g, v_conv_ln_b=v_conv_ln_b, v_conv_pw=v_conv_pw, v_ret_gn_g=v_ret_gn_g, v_w_out=v_w_out, v_ffn2_w13=v_ffn2_w13, v_ffn2_w2=v_ffn2_w2, v_ln_g=v_ln_g, v_ln_b=v_ln_b)
    weights = {n: given[n] for n in TWIN_WEIGHTS}
    shared = {n: given[n] for n in SHARED_INPUTS}
    per_example = {n: given[n] for n in ['x']}
    grad_fn = _jax.value_and_grad(_loss, argnums=(0, 1))

    def one_microbatch(ex, loss_target):
        ex = dict(ex)
        diff = ex.pop(TWIN_DIFF_INPUT)
        return grad_fn(weights, diff, {**shared, **ex}, loss_target)

    if N_MICROBATCH == 1:
        loss, (grad_w, grad_x) = one_microbatch(per_example, given["loss_target"])
    else:
        def body(carry, xs):
            loss_sum, grad_sum = carry
            l_k, (gw_k, gx_k) = one_microbatch(xs[0], xs[1])
            with _jax.named_scope("update"):
                return (loss_sum + l_k, _jax.tree.map(_jnp.add, grad_sum, gw_k)), gx_k

        init = (_jnp.zeros((), _jnp.float32), _jax.tree.map(_jnp.zeros_like, weights))
        (loss, grad_w), grad_x = _jax.lax.scan(body, init, (per_example, given["loss_target"]))
    with _jax.named_scope("update"):
        delta_w, new_m, new_v = {}, {}, {}
        for n in TWIN_WEIGHTS:
            delta_w[n], new_m[n], new_v[n] = _adamw(weights[n], grad_w[n], given["m_" + n], given["v_" + n])
    return (loss, grad_x, *[grad_w[n] for n in TWIN_WEIGHTS], *[delta_w[n] for n in TWIN_WEIGHTS],
            *[new_m[n] for n in TWIN_WEIGHTS], *[new_v[n] for n in TWIN_WEIGHTS])
```

```python
import functools
import math

import jax
import jax.numpy as jnp
from jax import lax
from jax.experimental import pallas as pl
from jax.experimental.pallas import tpu as pltpu

D = 1024
DEPTH = 2
N_META = 16
PAD = 112
ROW0 = PAD + N_META
D_POOL = 256
D_CONV = 256
D_RET = 512
HEADS = 4
DH = 128
CONV_W = 31
D_FF = 2752
FF_SLOT = 1408
D_FFP = 2 * FF_SLOT
D_IN = 2816
N_SHARD = 4
ALPHA = (2.0 * DEPTH) ** 0.25
LN_EPS = 1e-5
ROPE_BASE = 10000.0
LOG_GAMMA = tuple(math.log(1.0 - 2.0 ** (-5.0 - h)) for h in range(HEADS))
ADAM_LR, ADAM_B1, ADAM_B2, ADAM_EPS, ADAM_WD, ADAM_STEP = 0.001, 0.9, 0.999, 1e-08, 0.01, 10

_MM = jnp.bfloat16
_WIRE = jnp.bfloat16
_VMEM_LIMIT = 56 * 1024 * 1024

MESH = pl.DeviceIdType.MESH
_ANY = pl.BlockSpec(memory_space=pl.ANY)


def _dot(a, b):
    return jnp.dot(a, b, preferred_element_type=jnp.float32)


def _dot_nt(a, b):
    return lax.dot_general(a, b, (((1,), (1,)), ((), ())), preferred_element_type=jnp.float32)


def _dot_tn(a, b):
    return lax.dot_general(a, b, (((0,), (0,)), ((), ())), preferred_element_type=jnp.float32)


def _params(sem=("arbitrary",)):
    return pltpu.CompilerParams(dimension_semantics=sem, vmem_limit_bytes=_VMEM_LIMIT)


def _row_block(t, cap=640):
    for rb in (640, 320, 128):
        if rb <= cap and t % rb == 0 and (t > 1024 or rb == 128):
            return rb
    raise ValueError(t)


def _rows(rb, n):
    return pl.BlockSpec((rb, n), lambda i: (i, 0))


def _full(shape):
    nd = len(shape)
    return pl.BlockSpec(tuple(shape), lambda i: (0,) * nd, pipeline_mode=pl.Buffered(1))


def _acc(shape):
    nd = len(shape)
    return pl.BlockSpec(tuple(shape), lambda i: (0,) * nd)


def _sigmoid(x):
    return 1.0 / (1.0 + jnp.exp(-x))


def _ln_fwd(s):
    mu = jnp.mean(s, axis=-1, keepdims=True)
    xc = s - mu
    var = jnp.mean(xc * xc, axis=-1, keepdims=True)
    rstd = lax.rsqrt(var + LN_EPS)
    return xc * rstd, rstd


def _ln_bwd(dxh, xh, rstd):
    m1 = jnp.mean(dxh, axis=-1, keepdims=True)
    m2 = jnp.mean(dxh * xh, axis=-1, keepdims=True)
    return rstd * (dxh - m1 - xh * m2)


def _ln_in_fwd(raw):
    t = raw.shape[0]
    rb = _row_block(t)

    def body(raw_ref, xh_ref, rstd_ref):
        xh, rstd = _ln_fwd(raw_ref[...])
        xh_ref[...] = xh
        rstd_ref[...] = rstd

    return pl.pallas_call(
        body, name="ln_in_fwd", grid=(t // rb,),
        in_specs=[_rows(rb, D)],
        out_specs=[_rows(rb, D), _rows(rb, 1)],
        out_shape=[jax.ShapeDtypeStruct((t, D), jnp.float32), jax.ShapeDtypeStruct((t, 1), jnp.float32)],
        compiler_params=_params(),
    )(raw)


def _ffn_fwd(name, xh, gb, w13, w2):
    t = xh.shape[0]
    rb = _row_block(t)

    def body(xh_ref, gb_ref, w13_ref, w2_ref, out_ref, rstd_ref):
        h = xh_ref[...] * gb_ref[0:1, :] + gb_ref[1:2, :]
        hb = h.astype(_MM)
        acc = jnp.zeros((rb, D), jnp.float32)
        for j in range(2):
            lo = j * FF_SLOT
            a = _dot(hb, w13_ref[:, lo:lo + FF_SLOT])
            u = _dot(hb, w13_ref[:, D_FFP + lo:D_FFP + lo + FF_SLOT])
            hid = (a * _sigmoid(a) * u).astype(_MM)
            acc = acc + _dot(hid, w2_ref[lo:lo + FF_SLOT, :])
        xo, rstd = _ln_fwd(ALPHA * h + 0.5 * acc)
        out_ref[...] = xo
        rstd_ref[...] = rstd

    return pl.pallas_call(
        body, name=name, grid=(t // rb,),
        in_specs=[_rows(rb, D), _full((2, D)), _full(w13.shape), _full(w2.shape)],
        out_specs=[_rows(rb, D), _rows(rb, 1)],
        out_shape=[jax.ShapeDtypeStruct((t, D), jnp.float32), jax.ShapeDtypeStruct((t, 1), jnp.float32)],
        compiler_params=_params(),
    )(xh, gb, w13, w2)


def _ffn_bwd(name, dy, xo, rstd, gb_out, xh, gb, w13, w2):
    t = xh.shape[0]
    rb = _row_block(t, 320)

    def body(dy_ref, xo_ref, rstd_ref, gbo_ref, xh_ref, gb_ref, w13_ref, w2_ref,
             dh_ref, hb_ref, hid_ref, dau_ref, dffn_ref, dgb_ref):
        i = pl.program_id(0)

        @pl.when(i == 0)
        def _():
            dgb_ref[...] = jnp.zeros_like(dgb_ref)

        dy = dy_ref[...]
        xo = xo_ref[...]
        dgb_ref[0:1, :] += jnp.sum(dy * xo, axis=0, keepdims=True)
        dgb_ref[1:2, :] += jnp.sum(dy, axis=0, keepdims=True)
        ds = _ln_bwd(dy * gbo_ref[0:1, :], xo, rstd_ref[...])
        dffn = (0.5 * ds).astype(_MM)
        dffn_ref[...] = dffn
        h = xh_ref[...] * gb_ref[0:1, :] + gb_ref[1:2, :]
        hb = h.astype(_MM)
        hb_ref[...] = hb
        dh = ALPHA * ds
        for j in range(2):
            lo = j * FF_SLOT
            w1 = w13_ref[:, lo:lo + FF_SLOT]
            w3 = w13_ref[:, D_FFP + lo:D_FFP + lo + FF_SLOT]
            a = _dot(hb, w1)
            u = _dot(hb, w3)
            sg = _sigmoid(a)
            si = a * sg
            hid_ref[:, lo:lo + FF_SLOT] = (si * u).astype(_MM)
            dhid = _dot_nt(dffn, w2_ref[lo:lo + FF_SLOT, :])
            da = (dhid * u * (sg * (1.0 + a * (1.0 - sg)))).astype(_MM)
            du = (dhid * si).astype(_MM)
            dau_ref[:, lo:lo + FF_SLOT] = da
            dau_ref[:, D_FFP + lo:D_FFP + lo + FF_SLOT] = du
            dh = dh + _dot_nt(da, w1) + _dot_nt(du, w3)
        dh_ref[...] = dh

    return pl.pallas_call(
        body, name=name, grid=(t // rb,),
        in_specs=[_rows(rb, D), _rows(rb, D), _rows(rb, 1), _full((2, D)), _rows(rb, D), _full((2, D)),
                  _full(w13.shape), _full(w2.shape)],
        out_specs=[_rows(rb, D), _rows(rb, D), _rows(rb, D_FFP), _rows(rb, 2 * D_FFP), _rows(rb, D), _acc((8, D))],
        out_shape=[jax.ShapeDtypeStruct((t, D), jnp.float32), jax.ShapeDtypeStruct((t, D), _MM),
                   jax.ShapeDtypeStruct((t, D_FFP), _MM), jax.ShapeDtypeStruct((t, 2 * D_FFP), _MM),
                   jax.ShapeDtypeStruct((t, D), _MM), jax.ShapeDtypeStruct((8, D), jnp.float32)],
        compiler_params=_params(),
    )(dy, xo, rstd, gb_out, xh, gb, w13, w2)


def _mix_in_fwd(name, xh, gb, w_in):
    t = xh.shape[0]
    rb = _row_block(t)

    def body(xh_ref, gb_ref, w_ref, z_ref):
        h = xh_ref[...] * gb_ref[0:1, :] + gb_ref[1:2, :]
        z = _dot(h.astype(_MM), w_ref[...])
        row = pl.program_id(0) * rb + lax.broadcasted_iota(jnp.int32, (rb, 1), 0)
        z_ref[...] = jnp.where(row >= PAD, z, 0.0)

    return pl.pallas_call(
        body, name=name, grid=(t // rb,),
        in_specs=[_rows(rb, D), _full((2, D)), _full(w_in.shape)],
        out_specs=_rows(rb, D_IN),
        out_shape=jax.ShapeDtypeStruct((t, D_IN), jnp.float32),
        compiler_params=_params(),
    )(xh, gb, w_in)


def _mix_in_bwd(name, dh_res, dz, xh, gb, w_in):
    t = xh.shape[0]
    rb = _row_block(t)

    def body(dhr_ref, dz_ref, xh_ref, gb_ref, w_ref, dh_ref, hb_ref):
        dh_ref[...] = dhr_ref[...] + _dot_nt(dz_ref[...], w_ref[...])
        hb_ref[...] = (xh_ref[...] * gb_ref[0:1, :] + gb_ref[1:2, :]).astype(_MM)

    return pl.pallas_call(
        body, name=name, grid=(t // rb,),
        in_specs=[_rows(rb, D), _rows(rb, D_IN), _rows(rb, D), _full((2, D)), _full(w_in.shape)],
        out_specs=[_rows(rb, D), _rows(rb, D)],
        out_shape=[jax.ShapeDtypeStruct((t, D), jnp.float32), jax.ShapeDtypeStruct((t, D), _MM)],
        compiler_params=_params(),
    )(dh_res, dz, xh, gb, w_in)


def _mix_out_fwd(name, xh, gb, ycat, w_out):
    t = xh.shape[0]
    rb = _row_block(t)

    def body(xh_ref, gb_ref, y_ref, w_ref, out_ref, rstd_ref):
        h = xh_ref[...] * gb_ref[0:1, :] + gb_ref[1:2, :]
        xo, rstd = _ln_fwd(ALPHA * h + _dot(y_ref[...], w_ref[...]))
        out_ref[...] = xo
        rstd_ref[...] = rstd

    return pl.pallas_call(
        body, name=name, grid=(t // rb,),
        in_specs=[_rows(rb, D), _full((2, D)), _rows(rb, D), _full(w_out.shape)],
        out_specs=[_rows(rb, D), _rows(rb, 1)],
        out_shape=[jax.ShapeDtypeStruct((t, D), jnp.float32), jax.ShapeDtypeStruct((t, 1), jnp.float32)],
        compiler_params=_params(),
    )(xh, gb, ycat, w_out)


def _mix_out_bwd(name, dy, xo, rstd, gb_out, w_out):
    t = xo.shape[0]
    rb = _row_block(t)

    def body(dy_ref, xo_ref, rstd_ref, gbo_ref, w_ref, dhr_ref, dyc_ref, dsb_ref, dgb_ref):
        @pl.when(pl.program_id(0) == 0)
        def _():
            dgb_ref[...] = jnp.zeros_like(dgb_ref)

        dy = dy_ref[...]
        xo = xo_ref[...]
        dgb_ref[0:1, :] += jnp.sum(dy * xo, axis=0, keepdims=True)
        dgb_ref[1:2, :] += jnp.sum(dy, axis=0, keepdims=True)
        ds = _ln_bwd(dy * gbo_ref[0:1, :], xo, rstd_ref[...])
        dsb = ds.astype(_MM)
        dsb_ref[...] = dsb
        dhr_ref[...] = ALPHA * ds
        dyc_ref[...] = _dot_nt(dsb, w_ref[...])

    return pl.pallas_call(
        body, name=name, grid=(t // rb,),
        in_specs=[_rows(rb, D), _rows(rb, D), _rows(rb, 1), _full((2, D)), _full(w_out.shape)],
        out_specs=[_rows(rb, D), _rows(rb, D), _rows(rb, D), _acc((8, D))],
        out_shape=[jax.ShapeDtypeStruct((t, D), jnp.float32), jax.ShapeDtypeStruct((t, D), jnp.float32),
                   jax.ShapeDtypeStruct((t, D), _MM), jax.ShapeDtypeStruct((8, D), jnp.float32)],
        compiler_params=_params(),
    )(dy, xo, rstd, gb_out, w_out)


def _loss_fwd_bwd(xh, gb, target):
    t = xh.shape[0]
    rb = _row_block(t)

    def body(xh_ref, gb_ref, tg_ref, dy_ref, loss_ref):
        @pl.when(pl.program_id(0) == 0)
        def _():
            loss_ref[...] = jnp.zeros_like(loss_ref)

        y = xh_ref[...] * gb_ref[0:1, :] + gb_ref[1:2, :]
        row = pl.program_id(0) * rb + lax.broadcasted_iota(jnp.int32, (rb, 1), 0)
        err = jnp.where(row >= ROW0, y - tg_ref[...], 0.0)
        dy_ref[...] = err * (1.0 / D)
        per_row = jnp.mean(err * err, axis=-1, keepdims=True)
        loss_ref[...] += 0.5 * jnp.sum(per_row, axis=0, keepdims=True)

    return pl.pallas_call(
        body, name="loss", grid=(t // rb,),
        in_specs=[_rows(rb, D), _full((2, D)), _rows(rb, D)],
        out_specs=[_rows(rb, D), _acc((1, 1))],
        out_shape=[jax.ShapeDtypeStruct((t, D), jnp.float32), jax.ShapeDtypeStruct((1, 1), jnp.float32)],
        compiler_params=_params(),
    )(xh, gb, target)


def _ln_in_bwd(dy, xh, rstd, gb):
    t = xh.shape[0]
    rb = _row_block(t)

    def body(dy_ref, xh_ref, rstd_ref, gb_ref, dx_ref, dgb_ref):
        @pl.when(pl.program_id(0) == 0)
        def _():
            dgb_ref[...] = jnp.zeros_like(dgb_ref)

        dy = dy_ref[...]
        xh = xh_ref[...]
        dgb_ref[0:1, :] += jnp.sum(dy * xh, axis=0, keepdims=True)
        dgb_ref[1:2, :] += jnp.sum(dy, axis=0, keepdims=True)
        dx_ref[...] = _ln_bwd(dy * gb_ref[0:1, :], xh, rstd_ref[...])

    return pl.pallas_call(
        body, name="ln_in_bwd", grid=(t // rb,),
        in_specs=[_rows(rb, D), _rows(rb, D), _rows(rb, 1), _full((2, D))],
        out_specs=[_rows(rb, D), _acc((8, D))],
        out_shape=[jax.ShapeDtypeStruct((t, D), jnp.float32), jax.ShapeDtypeStruct((8, D), jnp.float32)],
        compiler_params=_params(),
    )(dy, xh, rstd, gb)


def _matmul_tn(name, x, y, tn, out_dtype):
    t, k = x.shape
    n = y.shape[1]
    tt = _row_block(t)
    nt = t // tt

    def body(x_ref, y_ref, o_ref, acc_ref):
        s = pl.program_id(1)

        @pl.when(s == 0)
        def _():
            acc_ref[...] = jnp.zeros_like(acc_ref)

        acc_ref[...] += _dot_tn(x_ref[...], y_ref[...])

        @pl.when(s == nt - 1)
        def _():
            o_ref[...] = acc_ref[...].astype(out_dtype)

    return pl.pallas_call(
        body, name=name, grid=(n // tn, nt),
        in_specs=[pl.BlockSpec((tt, k), lambda j, s: (s, 0)), pl.BlockSpec((tt, tn), lambda j, s: (s, j))],
        out_specs=pl.BlockSpec((k, tn), lambda j, s: (0, j)),
        out_shape=jax.ShapeDtypeStruct((k, n), out_dtype),
        scratch_shapes=[pltpu.VMEM((k, tn), jnp.float32)],
        compiler_params=_params(("arbitrary", "arbitrary")),
    )(x, y)


_TAIL_U = 32
_TAIL_X = 16
_MIX_ROWS = 320


def _decay_mask(rb, h):
    ii = lax.broadcasted_iota(jnp.int32, (rb, rb), 0)
    jj = lax.broadcasted_iota(jnp.int32, (rb, rb), 1)
    dist = jnp.abs(ii - jj).astype(jnp.float32)
    vis = (jj >> 6) <= (ii >> 6)
    return jnp.where(vis, jnp.exp(LOG_GAMMA[h] * dist), 0.0)


def _row_decays(rb, h):
    r = lax.broadcasted_iota(jnp.int32, (rb, DH), 0).astype(jnp.float32)
    return jnp.exp(LOG_GAMMA[h] * (r + 1.0)), jnp.exp(LOG_GAMMA[h] * (rb - 1.0 - r))


def _rope(x, cs, sn):
    return x * cs + pltpu.roll(x, DH // 2, 1) * sn


def _rope_t(dx, cs, sn):
    return dx * cs + pltpu.roll(dx * sn, DH // 2, 1)


def _pool_count(blk, rb):
    row = blk * rb + lax.broadcasted_iota(jnp.int32, (rb, D_POOL), 0) - PAD
    lane = lax.broadcasted_iota(jnp.int32, (rb, D_POOL), 1)
    win = jnp.left_shift(2, lane >> 6)
    return jnp.clip(row + 1, 1, win).astype(jnp.float32)


def _pool_select(p2, p4, p8, p16):
    lane = lax.broadcasted_iota(jnp.int32, p2.shape, 1)
    return jnp.where(lane < 64, p2, jnp.where(lane < 128, p4, jnp.where(lane < 192, p8, p16)))


def _window_sums(ext_ref, base, rows, sign):
    acc = ext_ref[pl.ds(base, rows), :]
    outs = []
    for k in range(1, 16):
        acc = acc + ext_ref[pl.ds(base + sign * k, rows), :]
        if k in (1, 3, 7, 15):
            outs.append(acc)
    return _pool_select(*outs)


def _sub_rows(rb):
    return 128 if rb % 128 == 0 else 64


def _mix_core_fwd(name, z, cs, sn, wbd, pscale, cdw, cvec, wpw, gn):
    t = z.shape[0]
    rb = _row_block(t, _MIX_ROWS)
    nblk = t // rb
    sr = _sub_rows(rb)

    def body(z_ref, cs_ref, sn_ref, wbd_ref, ps_ref, cdw_ref, cvec_ref, wpw_ref, gn_ref,
             y_ref, st_ref, ut_ref, xt_ref,
             uext, xext, cv, ypre, state, wmask):
        i = pl.program_id(0)

        @pl.when(i == 0)
        def _():
            state[...] = jnp.zeros_like(state)
            uext[0:_TAIL_U, :] = jnp.zeros((_TAIL_U, D_CONV), jnp.float32)
            xext[0:_TAIL_X, :] = jnp.zeros((_TAIL_X, D_POOL), jnp.float32)
            for h in range(HEADS):
                wmask[h] = _decay_mask(rb, h)

        st_ref[0] = state[...]
        ut_ref[0] = uext[0:_TAIL_U, :]
        xt_ref[0] = xext[0:_TAIL_X, :]

        xp = z_ref[:, 0:256]
        uext[_TAIL_U:_TAIL_U + rb, :] = z_ref[:, 256:512] * _sigmoid(z_ref[:, 512:768])
        xext[_TAIL_X:_TAIL_X + rb, :] = xp

        for r in range(0, rb, sr):
            win = _window_sums(xext, _TAIL_X + r, sr, -1)
            ypre[r:r + sr, :] = win
            acc = jnp.zeros((sr, D_CONV), jnp.float32)
            for k in range(CONV_W):
                acc = acc + uext[pl.ds(_TAIL_U + r - k, sr), :] * cdw_ref[CONV_W - 1 - k:CONV_W - k, :]
            cv[r:r + sr, :] = acc

        yp = ypre[...] / _pool_count(i, rb) - xp
        y_ref[:, 0:256] = (_dot(yp.astype(_MM), wbd_ref[...]) * ps_ref[...]).astype(_MM)
        cn, _ = _ln_fwd(cv[...] + cvec_ref[0:1, :])
        ln = cn * cvec_ref[1:2, :] + cvec_ref[2:3, :]
        sw = ln * _sigmoid(ln)
        y_ref[:, 256:512] = _dot(sw.astype(_MM), wpw_ref[...]).astype(_MM)
        csv = cs_ref[...]
        snv = sn_ref[...]
        for h in range(HEADS):
            q = _rope(z_ref[:, 768 + h * DH:768 + (h + 1) * DH], csv, snv)
            k = _rope(z_ref[:, 1280 + h * DH:1280 + (h + 1) * DH], csv, snv) * (DH ** -0.5)
            vb = z_ref[:, 1792 + h * DH:1792 + (h + 1) * DH].astype(_MM)
            g = z_ref[:, 2304 + h * DH:2304 + (h + 1) * DH]
            a, b = _row_decays(rb, h)
            s = _dot_nt(q.astype(_MM), k.astype(_MM)) * wmask[h]
            o = _dot(s.astype(_MM), vb) + _dot((q * a).astype(_MM), state[h].astype(_MM))
            state[h] = math.exp(LOG_GAMMA[h] * rb) * state[h] + _dot_tn((k * b).astype(_MM), vb)
            on, _ = _ln_fwd(o)
            y_ref[:, 512 + h * DH:512 + (h + 1) * DH] = (
                g * _sigmoid(g) * on * gn_ref[:, h * DH:(h + 1) * DH]).astype(_MM)

        uext[0:_TAIL_U, :] = uext[rb:rb + _TAIL_U, :]
        xext[0:_TAIL_X, :] = xext[rb:rb + _TAIL_X, :]

    return pl.pallas_call(
        body, name=name, grid=(nblk,),
        in_specs=[_rows(rb, D_IN), _rows(rb, DH), _rows(rb, DH), _full((256, 256)), _full((1, 256)),
                  _full((32, 256)), _full((8, 256)), _full((256, 256)), _full((1, D_RET))],
        out_specs=[_rows(rb, D),
                   pl.BlockSpec((1, HEADS, DH, DH), lambda i: (i, 0, 0, 0)),
                   pl.BlockSpec((1, _TAIL_U, D_CONV), lambda i: (i, 0, 0)),
                   pl.BlockSpec((1, _TAIL_X, D_POOL), lambda i: (i, 0, 0))],
        out_shape=[jax.ShapeDtypeStruct((t, D), _MM),
                   jax.ShapeDtypeStruct((nblk, HEADS, DH, DH), jnp.float32),
                   jax.ShapeDtypeStruct((nblk, _TAIL_U, D_CONV), jnp.float32),
                   jax.ShapeDtypeStruct((nblk, _TAIL_X, D_POOL), jnp.float32)],
        scratch_shapes=[pltpu.VMEM((rb + _TAIL_U, D_CONV), jnp.float32),
                        pltpu.VMEM((rb + _TAIL_X, D_POOL), jnp.float32),
                        pltpu.VMEM((rb, D_CONV), jnp.float32),
                        pltpu.VMEM((rb, D_POOL), jnp.float32),
                        pltpu.VMEM((HEADS, DH, DH), jnp.float32),
                        pltpu.VMEM((HEADS, rb, rb), jnp.float32)],
        compiler_params=_params(),
    )(z, cs, sn, wbd, pscale, cdw, cvec, wpw, gn)


def _mix_core_bwd(name, z, dyc, cs, sn, st_in, ut_in, xt_in, wbd, pscale, cdw, cvec, wpw, gn):
    t = z.shape[0]
    rb = _row_block(t, _MIX_ROWS)
    nblk = t // rb
    sr = _sub_rows(rb)
    rev = lambda i: nblk - 1 - i

    def body(z_ref, dy_ref, cs_ref, sn_ref, st_ref, ut_ref, xt_ref,
             wbd_ref, ps_ref, cdw_ref, cvec_ref, wpw_ref, gn_ref,
             dz_ref, dwbd_ref, dwpw_ref, dcdw_ref, dsm_ref,
             uext, xext, cv, ypre, dcvext, eext, dstate, wmask):
        i = pl.program_id(0)
        blk = nblk - 1 - i

        @pl.when(i == 0)
        def _():
            dstate[...] = jnp.zeros_like(dstate)
            dcvext[rb:rb + _TAIL_U, :] = jnp.zeros((_TAIL_U, D_CONV), jnp.float32)
            eext[rb:rb + _TAIL_X, :] = jnp.zeros((_TAIL_X, D_POOL), jnp.float32)
            dwbd_ref[...] = jnp.zeros_like(dwbd_ref)
            dwpw_ref[...] = jnp.zeros_like(dwpw_ref)
            dcdw_ref[...] = jnp.zeros_like(dcdw_ref)
            dsm_ref[...] = jnp.zeros_like(dsm_ref)
            for h in range(HEADS):
                wmask[h] = _decay_mask(rb, h)

        row = blk * rb + lax.broadcasted_iota(jnp.int32, (rb, 1), 0)
        live = row >= PAD

        xp = z_ref[:, 0:256]
        ca = z_ref[:, 256:512]
        sg_c = _sigmoid(z_ref[:, 512:768])
        uext[0:_TAIL_U, :] = ut_ref[0]
        xext[0:_TAIL_X, :] = xt_ref[0]
        uext[_TAIL_U:_TAIL_U + rb, :] = ca * sg_c
        xext[_TAIL_X:_TAIL_X + rb, :] = xp
        for r in range(0, rb, sr):
            ypre[r:r + sr, :] = _window_sums(xext, _TAIL_X + r, sr, -1)
            acc = jnp.zeros((sr, D_CONV), jnp.float32)
            for k in range(CONV_W):
                acc = acc + uext[pl.ds(_TAIL_U + r - k, sr), :] * cdw_ref[CONV_W - 1 - k:CONV_W - k, :]
            cv[r:r + sr, :] = acc

        cnt = _pool_count(blk, rb)
        ypb = (ypre[...] / cnt - xp).astype(_MM)
        dyp = dy_ref[:, 0:256]
        pm = _dot(ypb, wbd_ref[...])
        dsm_ref[1:2, 0:256] += jnp.sum(dyp * pm, axis=0, keepdims=True)
        dpm = (dyp * ps_ref[...]).astype(_MM)
        dwbd_ref[...] += _dot_tn(ypb, dpm)
        dypre = _dot_nt(dpm, wbd_ref[...])
        eext[0:rb, :] = dypre / cnt
        for r in range(0, rb, sr):
            ypre[r:r + sr, :] = _window_sums(eext, r, sr, 1)
        dz_ref[:, 0:256] = jnp.where(live, ypre[...] - dypre, 0.0).astype(_MM)

        cn, rstd_c = _ln_fwd(cv[...] + cvec_ref[0:1, :])
        ln = cn * cvec_ref[1:2, :] + cvec_ref[2:3, :]
        sg_l = _sigmoid(ln)
        swb = (ln * sg_l).astype(_MM)
        dycb = dy_ref[:, 256:512].astype(_MM)
        dwpw_ref[...] += _dot_tn(swb, dycb)
        dln = _dot_nt(dycb, wpw_ref[...]) * (sg_l * (1.0 + ln * (1.0 - sg_l)))
        dsm_ref[3:4, 0:256] += jnp.sum(dln * cn, axis=0, keepdims=True)
        dsm_ref[4:5, 0:256] += jnp.sum(dln, axis=0, keepdims=True)
        dcv = _ln_bwd(dln * cvec_ref[1:2, :], cn, rstd_c)
        dsm_ref[2:3, 0:256] += jnp.sum(dcv, axis=0, keepdims=True)
        dcvext[0:rb, :] = dcv
        for k in range(CONV_W):
            prod = dcv * uext[pl.ds(_TAIL_U - k, rb), :]
            dcdw_ref[CONV_W - 1 - k:CONV_W - k, :] += jnp.sum(prod, axis=0, keepdims=True)
        for r in range(0, rb, sr):
            acc = jnp.zeros((sr, D_CONV), jnp.float32)
            for k in range(CONV_W):
                acc = acc + dcvext[pl.ds(r + k, sr), :] * cdw_ref[CONV_W - 1 - k:CONV_W - k, :]
            cv[r:r + sr, :] = acc
        du = cv[...]
        dz_ref[:, 256:512] = jnp.where(live, du * sg_c, 0.0).astype(_MM)
        dz_ref[:, 512:768] = jnp.where(live, du * ca * sg_c * (1.0 - sg_c), 0.0).astype(_MM)

        csv = cs_ref[...]
        snv = sn_ref[...]
        for h in range(HEADS):
            q = _rope(z_ref[:, 768 + h * DH:768 + (h + 1) * DH], csv, snv)
            k = _rope(z_ref[:, 1280 + h * DH:1280 + (h + 1) * DH], csv, snv) * (DH ** -0.5)
            vb = z_ref[:, 1792 + h * DH:1792 + (h + 1) * DH].astype(_MM)
            g = z_ref[:, 2304 + h * DH:2304 + (h + 1) * DH]
            a, b = _row_decays(rb, h)
            qb = q.astype(_MM)
            kb = k.astype(_MM)
            qab = (q * a).astype(_MM)
            kbb = (k * b).astype(_MM)
            stb = st_ref[0, h].astype(_MM)
            sb = (_dot_nt(qb, kb) * wmask[h]).astype(_MM)
            o = _dot(sb, vb) + _dot(qab, stb)
            on, rstd_o = _ln_fwd(o)
            gnv = gn_ref[:, h * DH:(h + 1) * DH]
            sg_g = _sigmoid(g)
            si_g = g * sg_g
            dyr = dy_ref[:, 512 + h * DH:512 + (h + 1) * DH]
            dsm_ref[0:1, h * DH:(h + 1) * DH] += jnp.sum(dyr * on * si_g, axis=0, keepdims=True)
            dgate = dyr * on * gnv * (sg_g * (1.0 + g * (1.0 - sg_g)))
            dob = _ln_bwd(dyr * gnv * si_g, on, rstd_o).astype(_MM)
            dstb = dstate[h].astype(_MM)
            dsb = (_dot_nt(dob, vb) * wmask[h]).astype(_MM)
            dq = _dot(dsb, kb) + _dot_nt(dob, stb) * a
            dk = _dot_tn(dsb, qb) + _dot_nt(vb, dstb) * b
            dv = _dot_tn(sb, dob) + _dot(kbb, dstb)
            dstate[h] = math.exp(LOG_GAMMA[h] * rb) * dstate[h] + _dot_tn(qab, dob)
            dz_ref[:, 768 + h * DH:768 + (h + 1) * DH] = jnp.where(live, _rope_t(dq, csv, snv), 0.0).astype(_MM)
            dz_ref[:, 1280 + h * DH:1280 + (h + 1) * DH] = jnp.where(
                live, _rope_t(dk * (DH ** -0.5), csv, snv), 0.0).astype(_MM)
            dz_ref[:, 1792 + h * DH:1792 + (h + 1) * DH] = jnp.where(live, dv, 0.0).astype(_MM)
            dz_ref[:, 2304 + h * DH:2304 + (h + 1) * DH] = jnp.where(live, dgate, 0.0).astype(_MM)

        dcvext[rb:rb + _TAIL_U, :] = dcvext[0:_TAIL_U, :]
        eext[rb:rb + _TAIL_X, :] = eext[0:_TAIL_X, :]

    rrows = lambda n: pl.BlockSpec((rb, n), lambda i: (rev(i), 0))
    return pl.pallas_call(
        body, name=name, grid=(nblk,),
        in_specs=[rrows(D_IN), rrows(D), rrows(DH), rrows(DH),
                  pl.BlockSpec((1, HEADS, DH, DH), lambda i: (rev(i), 0, 0, 0)),
                  pl.BlockSpec((1, _TAIL_U, D_CONV), lambda i: (rev(i), 0, 0)),
                  pl.BlockSpec((1, _TAIL_X, D_POOL), lambda i: (rev(i), 0, 0)),
                  _full((256, 256)), _full((1, 256)), _full((32, 256)), _full((8, 256)), _full((256, 256)),
                  _full((1, D_RET))],
        out_specs=[rrows(D_IN), _acc((256, 256)), _acc((256, 256)), _acc((32, 256)), _acc((8, 512))],
        out_shape=[jax.ShapeDtypeStruct((t, D_IN), _MM),
                   jax.ShapeDtypeStruct((256, 256), jnp.float32), jax.ShapeDtypeStruct((256, 256), jnp.float32),
                   jax.ShapeDtypeStruct((32, 256), jnp.float32), jax.ShapeDtypeStruct((8, 512), jnp.float32)],
        scratch_shapes=[pltpu.VMEM((rb + _TAIL_U, D_CONV), jnp.float32),
                        pltpu.VMEM((rb + _TAIL_X, D_POOL), jnp.float32),
                        pltpu.VMEM((rb, D_CONV), jnp.float32),
                        pltpu.VMEM((rb, D_POOL), jnp.float32),
                        pltpu.VMEM((rb + _TAIL_U, D_CONV), jnp.float32),
                        pltpu.VMEM((rb + _TAIL_X, D_POOL), jnp.float32),
                        pltpu.VMEM((HEADS, DH, DH), jnp.float32),
                        pltpu.VMEM((HEADS, rb, rb), jnp.float32)],
        compiler_params=_params(),
    )(z, dyc, cs, sn, st_in, ut_in, xt_in, wbd, pscale, cdw, cvec, wpw, gn)


def _me():
    return lax.axis_index("x"), lax.axis_index("y"), lax.axis_index("c")


def _flip(me, mask):
    return tuple(1 - m if f else m for m, f in zip(me, mask))


def _push(name, srcs, out_shapes, remote, local=()):
    n_in = len(srcs)
    n_out = len(out_shapes)
    n_rem = len(remote)
    n_loc = len(local)

    def body(*refs):
        ins = refs[:n_in]
        outs = refs[n_in:n_in + n_out]
        send_sems, recv_sems, loc_sems = refs[n_in + n_out:]
        me = _me()
        copies = []
        for k, (mask, src_fn, dst_fn) in enumerate(remote):
            cp = pltpu.make_async_remote_copy(
                src_ref=src_fn(ins, me), dst_ref=dst_fn(outs, me),
                send_sem=send_sems.at[k], recv_sem=recv_sems.at[k],
                device_id=_flip(me, mask), device_id_type=MESH)
            cp.start()
            copies.append(cp)
        locs = []
        for k, (src_fn, dst_fn) in enumerate(local):
            cp = pltpu.make_async_copy(src_fn(ins, me), dst_fn(outs, me), loc_sems.at[k])
            cp.start()
            locs.append(cp)
        for cp in copies:
            cp.wait()
        for cp in locs:
            cp.wait()

    return pl.pallas_call(
        body, name=name,
        in_specs=[_ANY] * n_in, out_specs=[_ANY] * n_out, out_shape=list(out_shapes),
        scratch_shapes=[pltpu.SemaphoreType.DMA((n_rem,)), pltpu.SemaphoreType.DMA((n_rem,)),
                        pltpu.SemaphoreType.DMA((max(n_loc, 1),))],
    )(*srcs)


_ICI_MASKS = ((0, 1, 0), (1, 0, 0), (1, 1, 0))
_D2D_MASK = (0, 0, 1)
_ALL_MASKS = tuple((a, b, c) for a in (0, 1) for b in (0, 1) for c in (0, 1))[1:]


def _chip(me):
    return 2 * me[0] + me[1]


def _gather_weights(wb, small):
    r = wb.shape[1]
    s = small.shape[0]
    remote = []
    for mask in _ICI_MASKS:
        remote.append((mask,
                       lambda ins, me: ins[0].at[me[2]],
                       lambda outs, me: outs[0].at[me[2], _chip(me)]))
        remote.append((mask,
                       lambda ins, me: ins[1],
                       lambda outs, me: outs[1].at[_chip(me)]))
    local = [(lambda ins, me: ins[0].at[0], lambda outs, me: outs[0].at[0, _chip(me)]),
             (lambda ins, me: ins[0].at[1], lambda outs, me: outs[0].at[1, _chip(me)]),
             (lambda ins, me: ins[1], lambda outs, me: outs[1].at[_chip(me)])]
    part, small_all = _push("gather_ici", [wb, small],
                            [jax.ShapeDtypeStruct((2, N_SHARD, r, D), wb.dtype),
                             jax.ShapeDtypeStruct((N_SHARD, s, D), small.dtype)], remote, local)
    remote = []
    for j in range(1, N_SHARD):
        remote.append((_D2D_MASK,
                       lambda ins, me, j=j: ins[0].at[me[2], (_chip(me) + j) % N_SHARD],
                       lambda outs, me, j=j: outs[0].at[me[2], (_chip(me) + j) % N_SHARD]))
    local = [(lambda ins, me: ins[0].at[me[2]], lambda outs, me: outs[0].at[me[2]]),
             (lambda ins, me: ins[0].at[1 - me[2], _chip(me)], lambda outs, me: outs[0].at[1 - me[2], _chip(me)])]
    (full,) = _push("gather_d2d", [part], [jax.ShapeDtypeStruct(part.shape, part.dtype)], remote, local)
    return full, small_all


def _sum_pair(name, g, recv):
    _, _, r, _ = g.shape
    rb = _row_block(r) if r % 128 == 0 else r // 11 if r % 11 == 0 else r
    c = lax.axis_index("c").astype(jnp.int32).reshape(1)

    def body(c_ref, g_ref, r_ref, o_ref):
        o_ref[...] = (g_ref[...].astype(jnp.float32) + r_ref[...].astype(jnp.float32)).astype(o_ref.dtype)

    return pl.pallas_call(
        body, name=name,
        grid_spec=pltpu.PrefetchScalarGridSpec(
            num_scalar_prefetch=1, grid=(N_SHARD, r // rb),
            in_specs=[pl.BlockSpec((None, None, rb, D), lambda s, i, c_ref: (c_ref[0], s, i, 0)),
                      pl.BlockSpec((None, rb, D), lambda s, i, c_ref: (s, i, 0))],
            out_specs=pl.BlockSpec((None, rb, D), lambda s, i, c_ref: (s, i, 0))),
        out_shape=jax.ShapeDtypeStruct((N_SHARD, r, D), g.dtype),
        compiler_params=_params(("arbitrary", "arbitrary")),
    )(c, g, recv)


def _sum_chips(name, p, recv):
    _, r, _ = p.shape
    rb = _row_block(r) if r % 128 == 0 else r // 11 if r % 11 == 0 else r
    s = (2 * lax.axis_index("x") + lax.axis_index("y")).astype(jnp.int32).reshape(1)

    def body(s_ref, p_ref, r_ref, o_ref):
        acc = p_ref[...].astype(jnp.float32)
        for j in range(3):
            acc = acc + r_ref[j].astype(jnp.float32)
        o_ref[...] = acc

    return pl.pallas_call(
        body, name=name,
        grid_spec=pltpu.PrefetchScalarGridSpec(
            num_scalar_prefetch=1, grid=(r // rb,),
            in_specs=[pl.BlockSpec((None, rb, D), lambda i, s_ref: (s_ref[0], i, 0)),
                      pl.BlockSpec((3, rb, D), lambda i, s_ref: (0, i, 0))],
            out_specs=pl.BlockSpec((rb, D), lambda i, s_ref: (i, 0))),
        out_shape=jax.ShapeDtypeStruct((r, D), jnp.float32),
        compiler_params=_params(),
    )(s, p, recv)


def _reduce_scatter(g):
    _, _, r, _ = g.shape
    (recv,) = _push("rs_d2d", [g], [jax.ShapeDtypeStruct((N_SHARD, r, D), g.dtype)],
                    [(_D2D_MASK, lambda ins, me: ins[0].at[1 - me[2]], lambda outs, me: outs[0])])
    p = _sum_pair("rs_sum_pair", g, recv)
    remote = []
    for j, mask in enumerate(_ICI_MASKS):
        remote.append((mask,
                       lambda ins, me, mask=mask: ins[0].at[_chip(_flip(me, mask))],
                       lambda outs, me, j=j: outs[0].at[j]))
    (recv3,) = _push("rs_ici", [p], [jax.ShapeDtypeStruct((3, r, D), g.dtype)], remote)
    mine = _sum_chips("rs_sum_chips", p, recv3)
    (both,) = _push("rs_share", [mine], [jax.ShapeDtypeStruct((2, r, D), jnp.float32)],
                    [(_D2D_MASK, lambda ins, me: ins[0], lambda outs, me: outs[0].at[me[2]])],
                    [(lambda ins, me: ins[0], lambda outs, me: outs[0].at[me[2]])])
    return both


def _all_reduce_small(v):
    s = v.shape[0]
    remote = [(mask, lambda ins, me: ins[0], lambda outs, me: outs[0].at[4 * me[0] + 2 * me[1] + me[2]])
              for mask in _ALL_MASKS]
    local = [(lambda ins, me: ins[0], lambda outs, me: outs[0].at[4 * me[0] + 2 * me[1] + me[2]])]
    (every,) = _push("small_all", [v], [jax.ShapeDtypeStruct((8, s, D), jnp.float32)], remote, local)

    def body(e_ref, o_ref):
        acc = e_ref[0]
        for j in range(1, 8):
            acc = acc + e_ref[j]
        o_ref[...] = acc

    return pl.pallas_call(
        body, name="small_sum", grid=(1,),
        in_specs=[pl.BlockSpec((8, s, D), lambda i: (0, 0, 0))],
        out_specs=pl.BlockSpec((s, D), lambda i: (0, 0)),
        out_shape=jax.ShapeDtypeStruct((s, D), jnp.float32),
        compiler_params=_params(),
    )(every)


def _adamw(name, w, g, m, v):
    r, c = w.shape
    rb = next(b for b in (256, 344, 128, 64, 32, 16, 8, r) if r % b == 0)

    def body(w_ref, g_ref, m_ref, v_ref, d_ref, mo_ref, vo_ref):
        g = g_ref[...]
        m = ADAM_B1 * m_ref[...] + (1.0 - ADAM_B1) * g
        v = ADAM_B2 * v_ref[...] + (1.0 - ADAM_B2) * (g * g)
        m_hat = m / (1.0 - ADAM_B1 ** ADAM_STEP)
        v_hat = v / (1.0 - ADAM_B2 ** ADAM_STEP)
        d_ref[...] = -ADAM_LR * (m_hat / (jnp.sqrt(v_hat) + ADAM_EPS) + ADAM_WD * w_ref[...])
        mo_ref[...] = m
        vo_ref[...] = v

    spec = pl.BlockSpec((rb, c), lambda i: (i, 0))
    return pl.pallas_call(
        body, name=name, grid=(r // rb,),
        in_specs=[spec] * 4, out_specs=[spec] * 3,
        out_shape=[jax.ShapeDtypeStruct((r, c), jnp.float32)] * 3,
        compiler_params=_params(),
    )(w, g, m, v)


_BIG = ("ffn1_w13", "ffn2_w13", "ffn1_w2", "ffn2_w2", "w_in", "w_out", "conv_pw")
_BIG_SHARD = {"ffn1_w13": (D, 1376), "ffn2_w13": (D, 1376), "ffn1_w2": (688, D), "ffn2_w2": (688, D),
              "w_in": (D, 704), "w_out": (256, D), "conv_pw": (64, 256)}


def _pack_rows(parts):
    flat = jnp.concatenate([p.reshape(-1) for p in parts])
    pad = (-flat.shape[0]) % (8 * D)
    if pad:
        flat = jnp.concatenate([flat, jnp.zeros((pad,), flat.dtype)])
    return flat.reshape(-1, D)


def _unpack_rows(buf, shapes):
    flat = buf.reshape(-1)
    out, off = [], 0
    for shp in shapes:
        n = math.prod(shp)
        out.append(flat[off:off + n].reshape(shp))
        off += n
    return out


def _pack_big_half(parts):
    return jnp.concatenate([parts[n].reshape(-1, D) for n in _BIG], axis=0)


def _unpack_big_half(buf):
    out, off = {}, 0
    for n in _BIG:
        shp = _BIG_SHARD[n]
        rows = math.prod(shp) // D
        out[n] = buf[off:off + rows].reshape(shp)
        off += rows
    return out


def _pad_cols(a, n):
    return jnp.pad(a, ((0, 0), (0, n - a.shape[1])))


def kernel(x, meta, ln_in_g, ln_in_b, ffn1_w13, ffn1_w2, w_in, pool_w, pool_scale, conv_dw, conv_db, conv_ln_g, conv_ln_b, conv_pw, ret_gn_g, w_out, ffn2_w13, ffn2_w2, ln_g, ln_b, loss_target, m_meta, m_ln_in_g, m_ln_in_b, m_ffn1_w13, m_ffn1_w2, m_w_in, m_pool_w, m_pool_scale, m_conv_dw, m_conv_db, m_conv_ln_g, m_conv_ln_b, m_conv_pw, m_ret_gn_g, m_w_out, m_ffn2_w13, m_ffn2_w2, m_ln_g, m_ln_b, v_meta, v_ln_in_g, v_ln_in_b, v_ffn1_w13, v_ffn1_w2, v_w_in, v_pool_w, v_pool_scale, v_conv_dw, v_conv_db, v_conv_ln_g, v_conv_ln_b, v_conv_pw, v_ret_gn_g, v_w_out, v_ffn2_w13, v_ffn2_w2, v_ln_g, v_ln_b):
    f32 = jnp.float32
    seq = x.shape[1]
    t = seq + ROW0
    me = _me()
    chip = _chip(me)
    big_w = {"ffn1_w13": ffn1_w13, "ffn2_w13": ffn2_w13, "ffn1_w2": ffn1_w2, "ffn2_w2": ffn2_w2,
             "w_in": w_in, "w_out": w_out, "conv_pw": conv_pw}

    wb = jnp.stack([_pack_big_half({n: big_w[n][l].astype(_WIRE) for n in _BIG}) for l in range(DEPTH)])
    small_shapes = [(N_META, 256), (DEPTH, CONV_W, 64), (DEPTH, 3, 256), (DEPTH, 3, 256)]
    small = _pack_rows([meta, conv_dw, ln_g, ln_b])
    wfull, small_all = _gather_weights(wb, small)

    sm = [_unpack_rows(small_all[s], small_shapes) for s in range(N_SHARD)]
    meta_f = jnp.concatenate([sm[s][0] for s in range(N_SHARD)], axis=1)
    cdw_f = jnp.concatenate([sm[s][1] for s in range(N_SHARD)], axis=2)
    lng_f = jnp.concatenate([sm[s][2] for s in range(N_SHARD)], axis=2)
    lnb_f = jnp.concatenate([sm[s][3] for s in range(N_SHARD)], axis=2)

    def layer_weights(l):
        sh = [_unpack_big_half(wfull[l, s]) for s in range(N_SHARD)]
        out = {}
        for n in ("ffn1", "ffn2"):
            out[n + "_w13"] = jnp.concatenate([_pad_cols(sh[s][n + "_w13"], FF_SLOT) for s in range(N_SHARD)], axis=1)
            zrows = jnp.zeros((FF_SLOT - 1376, D), _WIRE)
            out[n + "_w2"] = jnp.concatenate([sh[0][n + "_w2"], sh[1][n + "_w2"], zrows,
                                              sh[2][n + "_w2"], sh[3][n + "_w2"], zrows], axis=0)
        out["w_in"] = jnp.concatenate([sh[s]["w_in"] for s in range(N_SHARD)], axis=1)
        out["w_out"] = jnp.concatenate([sh[s]["w_out"] for s in range(N_SHARD)], axis=0)
        out["conv_pw"] = jnp.concatenate([sh[s]["conv_pw"] for s in range(N_SHARD)], axis=0)
        return out

    wl = [layer_weights(l) for l in range(DEPTH)]

    def mix_params(l):
        wbd = jnp.zeros((D_POOL, D_POOL), f32)
        for g in range(4):
            wbd = wbd.at[64 * g:64 * (g + 1), 64 * g:64 * (g + 1)].set(pool_w[l, g])
        cdw = jnp.pad(cdw_f[l], ((0, 1), (0, 0)))
        cvec = jnp.pad(jnp.stack([conv_db[l], conv_ln_g[l], conv_ln_b[l]]), ((0, 5), (0, 0)))
        return (wbd.astype(_MM), pool_scale[l][None], cdw, cvec, wl[l]["conv_pw"], ret_gn_g[l][None])

    gb_of = lambda l, i: jnp.stack([lng_f[l, i], lnb_f[l, i]])
    gb_in = jnp.stack([ln_in_g, ln_in_b])

    pos = jnp.arange(t, dtype=f32) - PAD
    inv_freq = ROPE_BASE ** (-jnp.arange(0, DH, 2, dtype=f32) / DH)
    ang = pos[:, None] * inv_freq[None, :]
    cs = jnp.concatenate([jnp.cos(ang), jnp.cos(ang)], axis=1)
    sn = jnp.concatenate([-jnp.sin(ang), jnp.sin(ang)], axis=1)

    raw = jnp.concatenate([jnp.zeros((PAD, D), f32), meta_f, x[0]], axis=0)
    target = jnp.concatenate([jnp.zeros((ROW0, D), f32), loss_target[0]], axis=0)
    acts = []
    xh, rstd = _ln_in_fwd(raw)
    cur = (xh, rstd, gb_in)
    saved = []
    for l in range(DEPTH):
        w = wl[l]
        a0 = cur
        xh1, r1 = _ffn_fwd(f"ffn1_fwd_{l}", a0[0], a0[2], w["ffn1_w13"], w["ffn1_w2"])
        a1 = (xh1, r1, gb_of(l, 0))
        z = _mix_in_fwd(f"mix_in_fwd_{l}", a1[0], a1[2], w["w_in"])
        mp = mix_params(l)
        ycat, st_in, ut_in, xt_in = _mix_core_fwd(f"mix_core_fwd_{l}", z, cs, sn, *mp)
        xh2, r2 = _mix_out_fwd(f"mix_out_fwd_{l}", a1[0], a1[2], ycat, w["w_out"])
        a2 = (xh2, r2, gb_of(l, 1))
        xh3, r3 = _ffn_fwd(f"ffn2_fwd_{l}", a2[0], a2[2], w["ffn2_w13"], w["ffn2_w2"])
        a3 = (xh3, r3, gb_of(l, 2))
        saved.append((a0, a1, a2, a3, z, ycat, st_in, ut_in, xt_in, mp))
        cur = a3

    dy, loss_part = _loss_fwd_bwd(cur[0], cur[2], target)
    loss = lax.psum(loss_part[0, 0], ("x", "y", "c"))

    gw = [dict() for _ in range(DEPTH)]
    g_ln_g = [[None] * 3 for _ in range(DEPTH)]
    g_ln_b = [[None] * 3 for _ in range(DEPTH)]
    g_small = [dict() for _ in range(DEPTH)]
    for l in reversed(range(DEPTH)):
        a0, a1, a2, a3, z, ycat, st_in, ut_in, xt_in, mp = saved[l]
        w = wl[l]
        dh, hb, hid, dau, dffn, dgb = _ffn_bwd(f"ffn2_bwd_{l}", dy, a3[0], a3[1], a3[2], a2[0], a2[2],
                                               w["ffn2_w13"], w["ffn2_w2"])
        g_ln_g[l][2], g_ln_b[l][2] = dgb[0], dgb[1]
        gw[l]["ffn2_w13"] = _matmul_tn(f"dw13_ffn2_{l}", hb, dau, FF_SLOT, _WIRE)
        gw[l]["ffn2_w2"] = _matmul_tn(f"dw2_ffn2_{l}", hid, dffn, D, _WIRE)
        dh_res, dycat, dsb, dgb = _mix_out_bwd(f"mix_out_bwd_{l}", dh, a2[0], a2[1], a2[2], w["w_out"])
        g_ln_g[l][1], g_ln_b[l][1] = dgb[0], dgb[1]
        gw[l]["w_out"] = _matmul_tn(f"dw_out_{l}", ycat, dsb, D, _WIRE)
        dz, dwbd, dwpw, dcdw, dsm = _mix_core_bwd(f"mix_core_bwd_{l}", z, dycat, cs, sn, st_in, ut_in, xt_in, *mp)
        gw[l]["conv_pw"] = dwpw.astype(_WIRE)
        g_small[l] = dict(
            pool_w=jnp.stack([dwbd[64 * g:64 * (g + 1), 64 * g:64 * (g + 1)] for g in range(4)]),
            pool_scale=dsm[1, :256], conv_db=dsm[2, :256], conv_ln_g=dsm[3, :256], conv_ln_b=dsm[4, :256],
            ret_gn_g=dsm[0], conv_dw=dcdw[:CONV_W])
        dh, hb = _mix_in_bwd(f"mix_in_bwd_{l}", dh_res, dz, a1[0], a1[2], w["w_in"])
        gw[l]["w_in"] = _matmul_tn(f"dw_in_{l}", hb, dz, D_IN, _WIRE)
        dh, hb, hid, dau, dffn, dgb = _ffn_bwd(f"ffn1_bwd_{l}", dh, a1[0], a1[1], a1[2], a0[0], a0[2],
                                               w["ffn1_w13"], w["ffn1_w2"])
        g_ln_g[l][0], g_ln_b[l][0] = dgb[0], dgb[1]
        gw[l]["ffn1_w13"] = _matmul_tn(f"dw13_ffn1_{l}", hb, dau, FF_SLOT, _WIRE)
        gw[l]["ffn1_w2"] = _matmul_tn(f"dw2_ffn1_{l}", hid, dffn, D, _WIRE)
        dy = dh
    d_raw, dgb_in = _ln_in_bwd(dy, saved[0][0][0], saved[0][0][1], gb_in)
    grad_x = d_raw[ROW0:][None]

    def shard_slices(l, s):
        g = gw[l]
        w2_off = 688 * s + (FF_SLOT - 1376) * (s // 2)
        return {
            "ffn1_w13": g["ffn1_w13"][:, FF_SLOT * s:FF_SLOT * s + 1376],
            "ffn2_w13": g["ffn2_w13"][:, FF_SLOT * s:FF_SLOT * s + 1376],
            "ffn1_w2": g["ffn1_w2"][w2_off:w2_off + 688],
            "ffn2_w2": g["ffn2_w2"][w2_off:w2_off + 688],
            "w_in": g["w_in"][:, 704 * s:704 * (s + 1)],
            "w_out": g["w_out"][256 * s:256 * (s + 1)],
            "conv_pw": g["conv_pw"][64 * s:64 * (s + 1)],
        }

    gbuf = jnp.stack([jnp.stack([_pack_big_half(shard_slices(l, s)) for s in range(N_SHARD)])
                      for l in range(DEPTH)])
    gsum = _reduce_scatter(gbuf)
    g_big = [_unpack_big_half(gsum[l]) for l in range(DEPTH)]
    grads = {n: jnp.stack([g_big[l][n] for l in range(DEPTH)]) for n in _BIG}

    small_parts = [
        d_raw[PAD:ROW0],
        jnp.stack([g_small[l]["conv_dw"] for l in range(DEPTH)]),
        jnp.stack([jnp.stack(g_ln_g[l]) for l in range(DEPTH)]),
        jnp.stack([jnp.stack(g_ln_b[l]) for l in range(DEPTH)]),
        dgb_in[0], dgb_in[1],
        jnp.stack([g_small[l]["pool_w"] for l in range(DEPTH)]),
        jnp.stack([g_small[l]["pool_scale"] for l in range(DEPTH)]),
        jnp.stack([g_small[l]["conv_db"] for l in range(DEPTH)]),
        jnp.stack([g_small[l]["conv_ln_g"] for l in range(DEPTH)]),
        jnp.stack([g_small[l]["conv_ln_b"] for l in range(DEPTH)]),
        jnp.stack([g_small[l]["ret_gn_g"] for l in range(DEPTH)]),
    ]
    red = _unpack_rows(_all_reduce_small(_pack_rows(small_parts)), [p.shape for p in small_parts])
    grads["meta"] = lax.dynamic_slice_in_dim(red[0], 256 * chip, 256, axis=1)
    grads["conv_dw"] = lax.dynamic_slice_in_dim(red[1], 64 * chip, 64, axis=2)
    grads["ln_g"] = lax.dynamic_slice_in_dim(red[2], 256 * chip, 256, axis=2)
    grads["ln_b"] = lax.dynamic_slice_in_dim(red[3], 256 * chip, 256, axis=2)
    for n, v in zip(("ln_in_g", "ln_in_b", "pool_w", "pool_scale", "conv_db", "conv_ln_g", "conv_ln_b", "ret_gn_g"),
                    red[4:]):
        grads[n] = v

    names = ['meta', 'ln_in_g', 'ln_in_b', 'ffn1_w13', 'ffn1_w2', 'w_in', 'pool_w', 'pool_scale', 'conv_dw',
             'conv_db', 'conv_ln_g', 'conv_ln_b', 'conv_pw', 'ret_gn_g', 'w_out', 'ffn2_w13', 'ffn2_w2', 'ln_g', 'ln_b']
    ws = dict(meta=meta, ln_in_g=ln_in_g, ln_in_b=ln_in_b, ffn1_w13=ffn1_w13, ffn1_w2=ffn1_w2, w_in=w_in,
              pool_w=pool_w, pool_scale=pool_scale, conv_dw=conv_dw, conv_db=conv_db, conv_ln_g=conv_ln_g,
              conv_ln_b=conv_ln_b, conv_pw=conv_pw, ret_gn_g=ret_gn_g, w_out=w_out, ffn2_w13=ffn2_w13,
              ffn2_w2=ffn2_w2, ln_g=ln_g, ln_b=ln_b)
    ms = dict(meta=m_meta, ln_in_g=m_ln_in_g, ln_in_b=m_ln_in_b, ffn1_w13=m_ffn1_w13, ffn1_w2=m_ffn1_w2,
              w_in=m_w_in, pool_w=m_pool_w, pool_scale=m_pool_scale, conv_dw=m_conv_dw, conv_db=m_conv_db,
              conv_ln_g=m_conv_ln_g, conv_ln_b=m_conv_ln_b, conv_pw=m_conv_pw, ret_gn_g=m_ret_gn_g,
              w_out=m_w_out, ffn2_w13=m_ffn2_w13, ffn2_w2=m_ffn2_w2, ln_g=m_ln_g, ln_b=m_ln_b)
    vs = dict(meta=v_meta, ln_in_g=v_ln_in_g, ln_in_b=v_ln_in_b, ffn1_w13=v_ffn1_w13, ffn1_w2=v_ffn1_w2,
              w_in=v_w_in, pool_w=v_pool_w, pool_scale=v_pool_scale, conv_dw=v_conv_dw, conv_db=v_conv_db,
              conv_ln_g=v_conv_ln_g, conv_ln_b=v_conv_ln_b, conv_pw=v_conv_pw, ret_gn_g=v_ret_gn_g,
              w_out=v_w_out, ffn2_w13=v_ffn2_w13, ffn2_w2=v_ffn2_w2, ln_g=v_ln_g, ln_b=v_ln_b)
    delta, new_m, new_v = {}, {}, {}
    for n in _BIG:
        shp = ws[n].shape
        two = lambda a: a.reshape(-1, shp[-1])
        d_, m_, v_ = _adamw("adamw_" + n, two(ws[n]), two(grads[n]), two(ms[n]), two(vs[n]))
        delta[n], new_m[n], new_v[n] = d_.reshape(shp), m_.reshape(shp), v_.reshape(shp)
    small_names = [n for n in names if n not in _BIG]
    pk = lambda d: _pack_rows([d[n] for n in small_names])
    d_, m_, v_ = _adamw("adamw_small", pk(ws), pk(grads), pk(ms), pk(vs))
    shapes = [ws[n].shape for n in small_names]
    for n, a, b, c in zip(small_names, _unpack_rows(d_, shapes), _unpack_rows(m_, shapes), _unpack_rows(v_, shapes)):
        delta[n], new_m[n], new_v[n] = a, b, c

    return (loss, grad_x, *[grads[n] for n in names], *[delta[n] for n in names],
            *[new_m[n] for n in names], *[new_v[n] for n in names])
```

```python
import functools
import math

import jax
import jax.numpy as jnp
from jax import lax
from jax.experimental import pallas as pl
from jax.experimental.pallas import tpu as pltpu

D = 1024
DEPTH = 2
N_META = 16
PAD = 112
ROW0 = PAD + N_META
D_POOL = 256
D_CONV = 256
D_RET = 512
HEADS = 4
DH = 128
CONV_W = 31
D_FF = 2752
FF_SLOT = 1408
D_FFP = 2 * FF_SLOT
D_IN = 2816
N_SHARD = 4
ALPHA = (2.0 * DEPTH) ** 0.25
LN_EPS = 1e-5
ROPE_BASE = 10000.0
LOG_GAMMA = tuple(math.log(1.0 - 2.0 ** (-5.0 - h)) for h in range(HEADS))
ADAM_LR, ADAM_B1, ADAM_B2, ADAM_EPS, ADAM_WD, ADAM_STEP = 0.001, 0.9, 0.999, 1e-08, 0.01, 10

_MM = jnp.bfloat16
_WIRE = jnp.bfloat16
_VMEM_LIMIT = 56 * 1024 * 1024

MESH = pl.DeviceIdType.MESH
_ANY = pl.BlockSpec(memory_space=pl.ANY)


def _dot(a, b):
    return jnp.dot(a, b, preferred_element_type=jnp.float32)


def _dot_nt(a, b):
    return lax.dot_general(a, b, (((1,), (1,)), ((), ())), preferred_element_type=jnp.float32)


def _dot_tn(a, b):
    return lax.dot_general(a, b, (((0,), (0,)), ((), ())), preferred_element_type=jnp.float32)


def _params(sem=("arbitrary",)):
    return pltpu.CompilerParams(dimension_semantics=sem, vmem_limit_bytes=_VMEM_LIMIT)


def _row_block(t, cap=640):
    for rb in (640, 320, 128):
        if rb <= cap and t % rb == 0 and (t > 1024 or rb == 128):
            return rb
    raise ValueError(t)


def _rows(rb, n):
    return pl.BlockSpec((rb, n), lambda i: (i, 0))


def _full(shape):
    nd = len(shape)
    return pl.BlockSpec(tuple(shape), lambda i: (0,) * nd, pipeline_mode=pl.Buffered(1))


def _acc(shape):
    nd = len(shape)
    return pl.BlockSpec(tuple(shape), lambda i: (0,) * nd)


def _sigmoid(x):
    return 1.0 / (1.0 + jnp.exp(-x))


def _ln_fwd(s):
    mu = jnp.mean(s, axis=-1, keepdims=True)
    xc = s - mu
    var = jnp.mean(xc * xc, axis=-1, keepdims=True)
    rstd = lax.rsqrt(var + LN_EPS)
    return xc * rstd, rstd


def _ln_bwd(dxh, xh, rstd):
    m1 = jnp.mean(dxh, axis=-1, keepdims=True)
    m2 = jnp.mean(dxh * xh, axis=-1, keepdims=True)
    return rstd * (dxh - m1 - xh * m2)


def _ln_in_fwd(raw):
    t = raw.shape[0]
    rb = _row_block(t)

    def body(raw_ref, xh_ref, rstd_ref):
        xh, rstd = _ln_fwd(raw_ref[...])
        xh_ref[...] = xh
        rstd_ref[...] = rstd

    return pl.pallas_call(
        body, name="ln_in_fwd", grid=(t // rb,),
        in_specs=[_rows(rb, D)],
        out_specs=[_rows(rb, D), _rows(rb, 1)],
        out_shape=[jax.ShapeDtypeStruct((t, D), jnp.float32), jax.ShapeDtypeStruct((t, 1), jnp.float32)],
        compiler_params=_params(),
    )(raw)


def _ffn_fwd(name, xh, gb, w13, w2):
    t = xh.shape[0]
    rb = _row_block(t)

    def body(xh_ref, gb_ref, w13_ref, w2_ref, out_ref, rstd_ref):
        h = xh_ref[...] * gb_ref[0:1, :] + gb_ref[1:2, :]
        hb = h.astype(_MM)
        acc = jnp.zeros((rb, D), jnp.float32)
        for j in range(2):
            lo = j * FF_SLOT
            a = _dot(hb, w13_ref[:, lo:lo + FF_SLOT])
            u = _dot(hb, w13_ref[:, D_FFP + lo:D_FFP + lo + FF_SLOT])
            hid = (a * _sigmoid(a) * u).astype(_MM)
            acc = acc + _dot(hid, w2_ref[lo:lo + FF_SLOT, :])
        xo, rstd = _ln_fwd(ALPHA * h + 0.5 * acc)
        out_ref[...] = xo
        rstd_ref[...] = rstd

    return pl.pallas_call(
        body, name=name, grid=(t // rb,),
        in_specs=[_rows(rb, D), _full((2, D)), _full(w13.shape), _full(w2.shape)],
        out_specs=[_rows(rb, D), _rows(rb, 1)],
        out_shape=[jax.ShapeDtypeStruct((t, D), jnp.float32), jax.ShapeDtypeStruct((t, 1), jnp.float32)],
        compiler_params=_params(),
    )(xh, gb, w13, w2)


def _ffn_bwd(name, dy, xo, rstd, gb_out, xh, gb, w13, w2):
    t = xh.shape[0]
    rb = _row_block(t, 320)

    def body(dy_ref, xo_ref, rstd_ref, gbo_ref, xh_ref, gb_ref, w13_ref, w2_ref,
             dh_ref, hb_ref, hid_ref, dau_ref, dffn_ref, dgb_ref):
        i = pl.program_id(0)

        @pl.when(i == 0)
        def _():
            dgb_ref[...] = jnp.zeros_like(dgb_ref)

        dy = dy_ref[...]
        xo = xo_ref[...]
        dgb_ref[0:1, :] += jnp.sum(dy * xo, axis=0, keepdims=True)
        dgb_ref[1:2, :] += jnp.sum(dy, axis=0, keepdims=True)
        ds = _ln_bwd(dy * gbo_ref[0:1, :], xo, rstd_ref[...])
        dffn = (0.5 * ds).astype(_MM)
        dffn_ref[...] = dffn
        h = xh_ref[...] * gb_ref[0:1, :] + gb_ref[1:2, :]
        hb = h.astype(_MM)
        hb_ref[...] = hb
        dh = ALPHA * ds
        for j in range(2):
            lo = j * FF_SLOT
            w1 = w13_ref[:, lo:lo + FF_SLOT]
            w3 = w13_ref[:, D_FFP + lo:D_FFP + lo + FF_SLOT]
            a = _dot(hb, w1)
            u = _dot(hb, w3)
            sg = _sigmoid(a)
            si = a * sg
            hid_ref[:, lo:lo + FF_SLOT] = (si * u).astype(_MM)
            dhid = _dot_nt(dffn, w2_ref[lo:lo + FF_SLOT, :])
            da = (dhid * u * (sg * (1.0 + a * (1.0 - sg)))).astype(_MM)
            du = (dhid * si).astype(_MM)
            dau_ref[:, lo:lo + FF_SLOT] = da
            dau_ref[:, D_FFP + lo:D_FFP + lo + FF_SLOT] = du
            dh = dh + _dot_nt(da, w1) + _dot_nt(du, w3)
        dh_ref[...] = dh

    return pl.pallas_call(
        body, name=name, grid=(t // rb,),
        in_specs=[_rows(rb, D), _rows(rb, D), _rows(rb, 1), _full((2, D)), _rows(rb, D), _full((2, D)),
                  _full(w13.shape), _full(w2.shape)],
        out_specs=[_rows(rb, D), _rows(rb, D), _rows(rb, D_FFP), _rows(rb, 2 * D_FFP), _rows(rb, D), _acc((8, D))],
        out_shape=[jax.ShapeDtypeStruct((t, D), jnp.float32), jax.ShapeDtypeStruct((t, D), _MM),
                   jax.ShapeDtypeStruct((t, D_FFP), _MM), jax.ShapeDtypeStruct((t, 2 * D_FFP), _MM),
                   jax.ShapeDtypeStruct((t, D), _MM), jax.ShapeDtypeStruct((8, D), jnp.float32)],
        compiler_params=_params(),
    )(dy, xo, rstd, gb_out, xh, gb, w13, w2)


def _mix_in_fwd(name, xh, gb, w_in):
    t = xh.shape[0]
    rb = _row_block(t)

    def body(xh_ref, gb_ref, w_ref, z_ref):
        h = xh_ref[...] * gb_ref[0:1, :] + gb_ref[1:2, :]
        z = _dot(h.astype(_MM), w_ref[...])
        row = pl.program_id(0) * rb + lax.broadcasted_iota(jnp.int32, (rb, 1), 0)
        z_ref[...] = jnp.where(row >= PAD, z, 0.0)

    return pl.pallas_call(
        body, name=name, grid=(t // rb,),
        in_specs=[_rows(rb, D), _full((2, D)), _full(w_in.shape)],
        out_specs=_rows(rb, D_IN),
        out_shape=jax.ShapeDtypeStruct((t, D_IN), jnp.float32),
        compiler_params=_params(),
    )(xh, gb, w_in)


def _mix_in_bwd(name, dh_res, dz, xh, gb, w_in):
    t = xh.shape[0]
    rb = _row_block(t)

    def body(dhr_ref, dz_ref, xh_ref, gb_ref, w_ref, dh_ref, hb_ref):
        dh_ref[...] = dhr_ref[...] + _dot_nt(dz_ref[...], w_ref[...])
        hb_ref[...] = (xh_ref[...] * gb_ref[0:1, :] + gb_ref[1:2, :]).astype(_MM)

    return pl.pallas_call(
        body, name=name, grid=(t // rb,),
        in_specs=[_rows(rb, D), _rows(rb, D_IN), _rows(rb, D), _full((2, D)), _full(w_in.shape)],
        out_specs=[_rows(rb, D), _rows(rb, D)],
        out_shape=[jax.ShapeDtypeStruct((t, D), jnp.float32), jax.ShapeDtypeStruct((t, D), _MM)],
        compiler_params=_params(),
    )(dh_res, dz, xh, gb, w_in)


def _mix_out_fwd(name, xh, gb, ycat, w_out):
    t = xh.shape[0]
    rb = _row_block(t)

    def body(xh_ref, gb_ref, y_ref, w_ref, out_ref, rstd_ref):
        h = xh_ref[...] * gb_ref[0:1, :] + gb_ref[1:2, :]
        xo, rstd = _ln_fwd(ALPHA * h + _dot(y_ref[...], w_ref[...]))
        out_ref[...] = xo
        rstd_ref[...] = rstd

    return pl.pallas_call(
        body, name=name, grid=(t // rb,),
        in_specs=[_rows(rb, D), _full((2, D)), _rows(rb, D), _full(w_out.shape)],
        out_specs=[_rows(rb, D), _rows(rb, 1)],
        out_shape=[jax.ShapeDtypeStruct((t, D), jnp.float32), jax.ShapeDtypeStruct((t, 1), jnp.float32)],
        compiler_params=_params(),
    )(xh, gb, ycat, w_out)


def _mix_out_bwd(name, dy, xo, rstd, gb_out, w_out):
    t = xo.shape[0]
    rb = _row_block(t)

    def body(dy_ref, xo_ref, rstd_ref, gbo_ref, w_ref, dhr_ref, dyc_ref, dsb_ref, dgb_ref):
        @pl.when(pl.program_id(0) == 0)
        def _():
            dgb_ref[...] = jnp.zeros_like(dgb_ref)

        dy = dy_ref[...]
        xo = xo_ref[...]
        dgb_ref[0:1, :] += jnp.sum(dy * xo, axis=0, keepdims=True)
        dgb_ref[1:2, :] += jnp.sum(dy, axis=0, keepdims=True)
        ds = _ln_bwd(dy * gbo_ref[0:1, :], xo, rstd_ref[...])
        dsb = ds.astype(_MM)
        dsb_ref[...] = dsb
        dhr_ref[...] = ALPHA * ds
        dyc_ref[...] = _dot_nt(dsb, w_ref[...])

    return pl.pallas_call(
        body, name=name, grid=(t // rb,),
        in_specs=[_rows(rb, D), _rows(rb, D), _rows(rb, 1), _full((2, D)), _full(w_out.shape)],
        out_specs=[_rows(rb, D), _rows(rb, D), _rows(rb, D), _acc((8, D))],
        out_shape=[jax.ShapeDtypeStruct((t, D), jnp.float32), jax.ShapeDtypeStruct((t, D), jnp.float32),
                   jax.ShapeDtypeStruct((t, D), _MM), jax.ShapeDtypeStruct((8, D), jnp.float32)],
        compiler_params=_params(),
    )(dy, xo, rstd, gb_out, w_out)


def _loss_fwd_bwd(xh, gb, target):
    t = xh.shape[0]
    rb = _row_block(t)

    def body(xh_ref, gb_ref, tg_ref, dy_ref, loss_ref):
        @pl.when(pl.program_id(0) == 0)
        def _():
            loss_ref[...] = jnp.zeros_like(loss_ref)

        y = xh_ref[...] * gb_ref[0:1, :] + gb_ref[1:2, :]
        row = pl.program_id(0) * rb + lax.broadcasted_iota(jnp.int32, (rb, 1), 0)
        err = jnp.where(row >= ROW0, y - tg_ref[...], 0.0)
        dy_ref[...] = err * (1.0 / D)
        per_row = jnp.mean(err * err, axis=-1, keepdims=True)
        loss_ref[...] += 0.5 * jnp.sum(per_row, axis=0, keepdims=True)

    return pl.pallas_call(
        body, name="loss", grid=(t // rb,),
        in_specs=[_rows(rb, D), _full((2, D)), _rows(rb, D)],
        out_specs=[_rows(rb, D), _acc((1, 1))],
        out_shape=[jax.ShapeDtypeStruct((t, D), jnp.float32), jax.ShapeDtypeStruct((1, 1), jnp.float32)],
        compiler_params=_params(),
    )(xh, gb, target)


def _ln_in_bwd(dy, xh, rstd, gb):
    t = xh.shape[0]
    rb = _row_block(t)

    def body(dy_ref, xh_ref, rstd_ref, gb_ref, dx_ref, dgb_ref):
        @pl.when(pl.program_id(0) == 0)
        def _():
            dgb_ref[...] = jnp.zeros_like(dgb_ref)

        dy = dy_ref[...]
        xh = xh_ref[...]
        dgb_ref[0:1, :] += jnp.sum(dy * xh, axis=0, keepdims=True)
        dgb_ref[1:2, :] += jnp.sum(dy, axis=0, keepdims=True)
        dx_ref[...] = _ln_bwd(dy * gb_ref[0:1, :], xh, rstd_ref[...])

    return pl.pallas_call(
        body, name="ln_in_bwd", grid=(t // rb,),
        in_specs=[_rows(rb, D), _rows(rb, D), _rows(rb, 1), _full((2, D))],
        out_specs=[_rows(rb, D), _acc((8, D))],
        out_shape=[jax.ShapeDtypeStruct((t, D), jnp.float32), jax.ShapeDtypeStruct((8, D), jnp.float32)],
        compiler_params=_params(),
    )(dy, xh, rstd, gb)


def _matmul_tn(name, x, y, tn, out_dtype):
    t, k = x.shape
    n = y.shape[1]
    tt = _row_block(t)
    nt = t // tt

    def body(x_ref, y_ref, o_ref, acc_ref):
        s = pl.program_id(1)

        @pl.when(s == 0)
        def _():
            acc_ref[...] = jnp.zeros_like(acc_ref)

        acc_ref[...] += _dot_tn(x_ref[...], y_ref[...])

        @pl.when(s == nt - 1)
        def _():
            o_ref[...] = acc_ref[...].astype(out_dtype)

    return pl.pallas_call(
        body, name=name, grid=(n // tn, nt),
        in_specs=[pl.BlockSpec((tt, k), lambda j, s: (s, 0)), pl.BlockSpec((tt, tn), lambda j, s: (s, j))],
        out_specs=pl.BlockSpec((k, tn), lambda j, s: (0, j)),
        out_shape=jax.ShapeDtypeStruct((k, n), out_dtype),
        scratch_shapes=[pltpu.VMEM((k, tn), jnp.float32)],
        compiler_params=_params(("arbitrary", "arbitrary")),
    )(x, y)


_TAIL_U = 32
_TAIL_X = 16
_MIX_ROWS = 320


def _decay_mask(rb, h):
    ii = lax.broadcasted_iota(jnp.int32, (rb, rb), 0)
    jj = lax.broadcasted_iota(jnp.int32, (rb, rb), 1)
    dist = jnp.abs(ii - jj).astype(jnp.float32)
    vis = (jj >> 6) <= (ii >> 6)
    return jnp.where(vis, jnp.exp(LOG_GAMMA[h] * dist), 0.0)


def _row_decays(rb, h):
    r = lax.broadcasted_iota(jnp.int32, (rb, DH), 0).astype(jnp.float32)
    return jnp.exp(LOG_GAMMA[h] * (r + 1.0)), jnp.exp(LOG_GAMMA[h] * (rb - 1.0 - r))


def _rope(x, cs, sn):
    return x * cs + pltpu.roll(x, DH // 2, 1) * sn


def _rope_t(dx, cs, sn):
    return dx * cs + pltpu.roll(dx * sn, DH // 2, 1)


def _pool_count(blk, rb):
    row = blk * rb + lax.broadcasted_iota(jnp.int32, (rb, D_POOL), 0) - PAD
    lane = lax.broadcasted_iota(jnp.int32, (rb, D_POOL), 1)
    win = jnp.left_shift(2, lane >> 6)
    return jnp.clip(row + 1, 1, win).astype(jnp.float32)


def _pool_select(p2, p4, p8, p16):
    lane = lax.broadcasted_iota(jnp.int32, p2.shape, 1)
    return jnp.where(lane < 64, p2, jnp.where(lane < 128, p4, jnp.where(lane < 192, p8, p16)))


def _window_sums(ext_ref, base, rows, sign):
    acc = ext_ref[pl.ds(base, rows), :]
    outs = []
    for k in range(1, 16):
        acc = acc + ext_ref[pl.ds(base + sign * k, rows), :]
        if k in (1, 3, 7, 15):
            outs.append(acc)
    return _pool_select(*outs)


def _sub_rows(rb):
    return 128 if rb % 128 == 0 else 64


def _mix_core_fwd(name, z, cs, sn, wbd, pscale, cdw, cvec, wpw, gn):
    t = z.shape[0]
    rb = _row_block(t, _MIX_ROWS)
    nblk = t // rb
    sr = _sub_rows(rb)

    def body(z_ref, cs_ref, sn_ref, wbd_ref, ps_ref, cdw_ref, cvec_ref, wpw_ref, gn_ref,
             y_ref, st_ref, ut_ref, xt_ref,
             uext, xext, cv, ypre, state, wmask):
        i = pl.program_id(0)

        @pl.when(i == 0)
        def _():
            state[...] = jnp.zeros_like(state)
            uext[0:_TAIL_U, :] = jnp.zeros((_TAIL_U, D_CONV), jnp.float32)
            xext[0:_TAIL_X, :] = jnp.zeros((_TAIL_X, D_POOL), jnp.float32)
            for h in range(HEADS):
                wmask[h] = _decay_mask(rb, h)

        st_ref[0] = state[...]
        ut_ref[0] = uext[0:_TAIL_U, :]
        xt_ref[0] = xext[0:_TAIL_X, :]

        xp = z_ref[:, 0:256]
        uext[_TAIL_U:_TAIL_U + rb, :] = z_ref[:, 256:512] * _sigmoid(z_ref[:, 512:768])
        xext[_TAIL_X:_TAIL_X + rb, :] = xp

        for r in range(0, rb, sr):
            win = _window_sums(xext, _TAIL_X + r, sr, -1)
            ypre[r:r + sr, :] = win
            acc = jnp.zeros((sr, D_CONV), jnp.float32)
            for k in range(CONV_W):
                acc = acc + uext[pl.ds(_TAIL_U + r - k, sr), :] * cdw_ref[CONV_W - 1 - k:CONV_W - k, :]
            cv[r:r + sr, :] = acc

        yp = ypre[...] / _pool_count(i, rb) - xp
        y_ref[:, 0:256] = (_dot(yp.astype(_MM), wbd_ref[...]) * ps_ref[...]).astype(_MM)
        cn, _ = _ln_fwd(cv[...] + cvec_ref[0:1, :])
        ln = cn * cvec_ref[1:2, :] + cvec_ref[2:3, :]
        sw = ln * _sigmoid(ln)
        y_ref[:, 256:512] = _dot(sw.astype(_MM), wpw_ref[...]).astype(_MM)
        csv = cs_ref[...]
        snv = sn_ref[...]
        for h in range(HEADS):
            q = _rope(z_ref[:, 768 + h * DH:768 + (h + 1) * DH], csv, snv)
            k = _rope(z_ref[:, 1280 + h * DH:1280 + (h + 1) * DH], csv, snv) * (DH ** -0.5)
            vb = z_ref[:, 1792 + h * DH:1792 + (h + 1) * DH].astype(_MM)
            g = z_ref[:, 2304 + h * DH:2304 + (h + 1) * DH]
            a, b = _row_decays(rb, h)
            s = _dot_nt(q.astype(_MM), k.astype(_MM)) * wmask[h]
            o = _dot(s.astype(_MM), vb) + _dot((q * a).astype(_MM), state[h].astype(_MM))
            state[h] = math.exp(LOG_GAMMA[h] * rb) * state[h] + _dot_tn((k * b).astype(_MM), vb)
            on, _ = _ln_fwd(o)
            y_ref[:, 512 + h * DH:512 + (h + 1) * DH] = (
                g * _sigmoid(g) * on * gn_ref[:, h * DH:(h + 1) * DH]).astype(_MM)

        uext[0:_TAIL_U, :] = uext[rb:rb + _TAIL_U, :]
        xext[0:_TAIL_X, :] = xext[rb:rb + _TAIL_X, :]

    return pl.pallas_call(
        body, name=name, grid=(nblk,),
        in_specs=[_rows(rb, D_IN), _rows(rb, DH), _rows(rb, DH), _full((256, 256)), _full((1, 256)),
                  _full((32, 256)), _full((8, 256)), _full((256, 256)), _full((1, D_RET))],
        out_specs=[_rows(rb, D),
                   pl.BlockSpec((1, HEADS, DH, DH), lambda i: (i, 0, 0, 0)),
                   pl.BlockSpec((1, _TAIL_U, D_CONV), lambda i: (i, 0, 0)),
                   pl.BlockSpec((1, _TAIL_X, D_POOL), lambda i: (i, 0, 0))],
        out_shape=[jax.ShapeDtypeStruct((t, D), _MM),
                   jax.ShapeDtypeStruct((nblk, HEADS, DH, DH), jnp.float32),
                   jax.ShapeDtypeStruct((nblk, _TAIL_U, D_CONV), jnp.float32),
                   jax.ShapeDtypeStruct((nblk, _TAIL_X, D_POOL), jnp.float32)],
        scratch_shapes=[pltpu.VMEM((rb + _TAIL_U, D_CONV), jnp.float32),
                        pltpu.VMEM((rb + _TAIL_X, D_POOL), jnp.float32),
                        pltpu.VMEM((rb, D_CONV), jnp.float32),
                        pltpu.VMEM((rb, D_POOL), jnp.float32),
                        pltpu.VMEM((HEADS, DH, DH), jnp.float32),
                        pltpu.VMEM((HEADS, rb, rb), jnp.float32)],
        compiler_params=_params(),
    )(z, cs, sn, wbd, pscale, cdw, cvec, wpw, gn)


def _mix_core_bwd(name, z, dyc, cs, sn, st_in, ut_in, xt_in, wbd, pscale, cdw, cvec, wpw, gn):
    t = z.shape[0]
    rb = _row_block(t, _MIX_ROWS)
    nblk = t // rb
    sr = _sub_rows(rb)
    rev = lambda i: nblk - 1 - i

    def body(z_ref, dy_ref, cs_ref, sn_ref, st_ref, ut_ref, xt_ref,
             wbd_ref, ps_ref, cdw_ref, cvec_ref, wpw_ref, gn_ref,
             dz_ref, dwbd_ref, dwpw_ref, dcdw_ref, dsm_ref,
             uext, xext, cv, ypre, dcvext, eext, dstate, wmask):
        i = pl.program_id(0)
        blk = nblk - 1 - i

        @pl.when(i == 0)
        def _():
            dstate[...] = jnp.zeros_like(dstate)
            dcvext[rb:rb + _TAIL_U, :] = jnp.zeros((_TAIL_U, D_CONV), jnp.float32)
            eext[rb:rb + _TAIL_X, :] = jnp.zeros((_TAIL_X, D_POOL), jnp.float32)
            dwbd_ref[...] = jnp.zeros_like(dwbd_ref)
            dwpw_ref[...] = jnp.zeros_like(dwpw_ref)
            dcdw_ref[...] = jnp.zeros_like(dcdw_ref)
            dsm_ref[...] = jnp.zeros_like(dsm_ref)
            for h in range(HEADS):
                wmask[h] = _decay_mask(rb, h)

        row = blk * rb + lax.broadcasted_iota(jnp.int32, (rb, 1), 0)
        live = row >= PAD

        xp = z_ref[:, 0:256]
        ca = z_ref[:, 256:512]
        sg_c = _sigmoid(z_ref[:, 512:768])
        uext[0:_TAIL_U, :] = ut_ref[0]
        xext[0:_TAIL_X, :] = xt_ref[0]
        uext[_TAIL_U:_TAIL_U + rb, :] = ca * sg_c
        xext[_TAIL_X:_TAIL_X + rb, :] = xp
        for r in range(0, rb, sr):
            ypre[r:r + sr, :] = _window_sums(xext, _TAIL_X + r, sr, -1)
            acc = jnp.zeros((sr, D_CONV), jnp.float32)
            for k in range(CONV_W):
                acc = acc + uext[pl.ds(_TAIL_U + r - k, sr), :] * cdw_ref[CONV_W - 1 - k:CONV_W - k, :]
            cv[r:r + sr, :] = acc

        cnt = _pool_count(blk, rb)
        ypb = (ypre[...] / cnt - xp).astype(_MM)
        dyp = dy_ref[:, 0:256]
        pm = _dot(ypb, wbd_ref[...])
        dsm_ref[1:2, 0:256] += jnp.sum(dyp * pm, axis=0, keepdims=True)
        dpm = (dyp * ps_ref[...]).astype(_MM)
        dwbd_ref[...] += _dot_tn(ypb, dpm)
        dypre = _dot_nt(dpm, wbd_ref[...])
        eext[0:rb, :] = dypre / cnt
        for r in range(0, rb, sr):
            ypre[r:r + sr, :] = _window_sums(eext, r, sr, 1)
        dz_ref[:, 0:256] = jnp.where(live, ypre[...] - dypre, 0.0).astype(_MM)

        cn, rstd_c = _ln_fwd(cv[...] + cvec_ref[0:1, :])
        ln = cn * cvec_ref[1:2, :] + cvec_ref[2:3, :]
        sg_l = _sigmoid(ln)
        swb = (ln * sg_l).astype(_MM)
        dycb = dy_ref[:, 256:512].astype(_MM)
        dwpw_ref[...] += _dot_tn(swb, dycb)
        dln = _dot_nt(dycb, wpw_ref[...]) * (sg_l * (1.0 + ln * (1.0 - sg_l)))
        dsm_ref[3:4, 0:256] += jnp.sum(dln * cn, axis=0, keepdims=True)
        dsm_ref[4:5, 0:256] += jnp.sum(dln, axis=0, keepdims=True)
        dcv = _ln_bwd(dln * cvec_ref[1:2, :], cn, rstd_c)
        dsm_ref[2:3, 0:256] += jnp.sum(dcv, axis=0, keepdims=True)
        dcvext[0:rb, :] = dcv
        for k in range(CONV_W):
            prod = dcv * uext[pl.ds(_TAIL_U - k, rb), :]
            dcdw_ref[CONV_W - 1 - k:CONV_W - k, :] += jnp.sum(prod, axis=0, keepdims=True)
        for r in range(0, rb, sr):
            acc = jnp.zeros((sr, D_CONV), jnp.float32)
            for k in range(CONV_W):
                acc = acc + dcvext[pl.ds(r + k, sr), :] * cdw_ref[CONV_W - 1 - k:CONV_W - k, :]
            cv[r:r + sr, :] = acc
        du = cv[...]
        dz_ref[:, 256:512] = jnp.where(live, du * sg_c, 0.0).astype(_MM)
        dz_ref[:, 512:768] = jnp.where(live, du * ca * sg_c * (1.0 - sg_c), 0.0).astype(_MM)

        csv = cs_ref[...]
        snv = sn_ref[...]
        for h in range(HEADS):
            q = _rope(z_ref[:, 768 + h * DH:768 + (h + 1) * DH], csv, snv)
            k = _rope(z_ref[:, 1280 + h * DH:1280 + (h + 1) * DH], csv, snv) * (DH ** -0.5)
            vb = z_ref[:, 1792 + h * DH:1792 + (h + 1) * DH].astype(_MM)
            g = z_ref[:, 2304 + h * DH:2304 + (h + 1) * DH]
            a, b = _row_decays(rb, h)
            qb = q.astype(_MM)
            kb = k.astype(_MM)
            qab = (q * a).astype(_MM)
            kbb = (k * b).astype(_MM)
            stb = st_ref[0, h].astype(_MM)
            sb = (_dot_nt(qb, kb) * wmask[h]).astype(_MM)
            o = _dot(sb, vb) + _dot(qab, stb)
            on, rstd_o = _ln_fwd(o)
            gnv = gn_ref[:, h * DH:(h + 1) * DH]
            sg_g = _sigmoid(g)
            si_g = g * sg_g
            dyr = dy_ref[:, 512 + h * DH:512 + (h + 1) * DH]
            dsm_ref[0:1, h * DH:(h + 1) * DH] += jnp.sum(dyr * on * si_g, axis=0, keepdims=True)
            dgate = dyr * on * gnv * (sg_g * (1.0 + g * (1.0 - sg_g)))
            dob = _ln_bwd(dyr * gnv * si_g, on, rstd_o).astype(_MM)
            dstb = dstate[h].astype(_MM)
            dsb = (_dot_nt(dob, vb) * wmask[h]).astype(_MM)
            dq = _dot(dsb, kb) + _dot_nt(dob, stb) * a
            dk = _dot_tn(dsb, qb) + _dot_nt(vb, dstb) * b
            dv = _dot_tn(sb, dob) + _dot(kbb, dstb)
            dstate[h] = math.exp(LOG_GAMMA[h] * rb) * dstate[h] + _dot_tn(qab, dob)
            dz_ref[:, 768 + h * DH:768 + (h + 1) * DH] = jnp.where(live, _rope_t(dq, csv, snv), 0.0).astype(_MM)
            dz_ref[:, 1280 + h * DH:1280 + (h + 1) * DH] = jnp.where(
                live, _rope_t(dk * (DH ** -0.5), csv, snv), 0.0).astype(_MM)
            dz_ref[:, 1792 + h * DH:1792 + (h + 1) * DH] = jnp.where(live, dv, 0.0).astype(_MM)
            dz_ref[:, 2304 + h * DH:2304 + (h + 1) * DH] = jnp.where(live, dgate, 0.0).astype(_MM)

        dcvext[rb:rb + _TAIL_U, :] = dcvext[0:_TAIL_U, :]
        eext[rb:rb + _TAIL_X, :] = eext[0:_TAIL_X, :]

    rrows = lambda n: pl.BlockSpec((rb, n), lambda i: (rev(i), 0))
    return pl.pallas_call(
        body, name=name, grid=(nblk,),
        in_specs=[rrows(D_IN), rrows(D), rrows(DH), rrows(DH),
                  pl.BlockSpec((1, HEADS, DH, DH), lambda i: (rev(i), 0, 0, 0)),
                  pl.BlockSpec((1, _TAIL_U, D_CONV), lambda i: (rev(i), 0, 0)),
                  pl.BlockSpec((1, _TAIL_X, D_POOL), lambda i: (rev(i), 0, 0)),
                  _full((256, 256)), _full((1, 256)), _full((32, 256)), _full((8, 256)), _full((256, 256)),
                  _full((1, D_RET))],
        out_specs=[rrows(D_IN), _acc((256, 256)), _acc((256, 256)), _acc((32, 256)), _acc((8, 512))],
        out_shape=[jax.ShapeDtypeStruct((t, D_IN), _MM),
                   jax.ShapeDtypeStruct((256, 256), jnp.float32), jax.ShapeDtypeStruct((256, 256), jnp.float32),
                   jax.ShapeDtypeStruct((32, 256), jnp.float32), jax.ShapeDtypeStruct((8, 512), jnp.float32)],
        scratch_shapes=[pltpu.VMEM((rb + _TAIL_U, D_CONV), jnp.float32),
                        pltpu.VMEM((rb + _TAIL_X, D_POOL), jnp.float32),
                        pltpu.VMEM((rb, D_CONV), jnp.float32),
                        pltpu.VMEM((rb, D_POOL), jnp.float32),
                        pltpu.VMEM((rb + _TAIL_U, D_CONV), jnp.float32),
                        pltpu.VMEM((rb + _TAIL_X, D_POOL), jnp.float32),
                        pltpu.VMEM((HEADS, DH, DH), jnp.float32),
                        pltpu.VMEM((HEADS, rb, rb), jnp.float32)],
        compiler_params=_params(),
    )(z, dyc, cs, sn, st_in, ut_in, xt_in, wbd, pscale, cdw, cvec, wpw, gn)


def _me():
    return lax.axis_index("x"), lax.axis_index("y"), lax.axis_index("c")


def _flip(me, mask):
    return tuple(1 - m if f else m for m, f in zip(me, mask))


def _push(name, aliased, inputs, fresh, remote):
    n_al, n_in, n_out, n_rem = len(aliased), len(inputs), len(fresh), len(remote)

    def body(*refs):
        ins = refs[n_al:n_al + n_in]
        al = refs[n_al + n_in:2 * n_al + n_in]
        outs = refs[2 * n_al + n_in:2 * n_al + n_in + n_out]
        send_sems, recv_sems = refs[2 * n_al + n_in + n_out:]
        me = _me()
        copies = []
        for k, (mask, src_fn, dst_fn) in enumerate(remote):
            cp = pltpu.make_async_remote_copy(
                src_ref=src_fn(al, ins, outs, me), dst_ref=dst_fn(al, ins, outs, me),
                send_sem=send_sems.at[k], recv_sem=recv_sems.at[k],
                device_id=_flip(me, mask), device_id_type=MESH)
            cp.start()
            copies.append(cp)
        for cp in copies:
            cp.wait()

    return pl.pallas_call(
        body, name=name,
        in_specs=[_ANY] * (n_al + n_in), out_specs=[_ANY] * (n_al + n_out),
        out_shape=[jax.ShapeDtypeStruct(a.shape, a.dtype) for a in aliased] + list(fresh),
        input_output_aliases={i: i for i in range(n_al)},
        scratch_shapes=[pltpu.SemaphoreType.DMA((n_rem,)), pltpu.SemaphoreType.DMA((n_rem,))],
    )(*aliased, *inputs)


_ICI_MASKS = ((0, 1, 0), (1, 0, 0), (1, 1, 0))
_D2D_MASK = (0, 0, 1)
_ALL_MASKS = tuple((a, b, c) for a in (0, 1) for b in (0, 1) for c in (0, 1))[1:]


def _chip(me):
    return 2 * me[0] + me[1]


def _gather_weights(wb, small):
    r = wb.shape[1]
    s = small.shape[0]
    chip = _chip(_me())
    part = lax.dynamic_update_slice(lax.empty((2, N_SHARD, r, D), wb.dtype), wb[:, None], (0, chip, 0, 0))
    small_all = lax.dynamic_update_slice(lax.empty((N_SHARD, s, D), small.dtype), small[None], (chip, 0, 0))
    remote = []
    for mask in _ICI_MASKS:
        mine = lambda al, ins, outs, me: al[0].at[me[2], _chip(me)]
        remote.append((mask, mine, mine))
        mine_small = lambda al, ins, outs, me: al[1].at[_chip(me)]
        remote.append((mask, mine_small, mine_small))
    part, small_all = _push("gather_ici", [part, small_all], [], [], remote)
    remote = []
    for j in range(1, N_SHARD):
        theirs = lambda al, ins, outs, me, j=j: al[0].at[me[2], (_chip(me) + j) % N_SHARD]
        remote.append((_D2D_MASK, theirs, theirs))
    (full,) = _push("gather_d2d", [part], [], [], remote)
    return full, small_all


def _sum_rows_block(r):
    return _row_block(r) if r % 128 == 0 else r // 11 if r % 11 == 0 else r


def _sum_pair(name, g, recv):
    _, _, r, _ = g.shape
    rb = _sum_rows_block(r)
    c = lax.axis_index("c").astype(jnp.int32).reshape(1)

    def body(c_ref, g_ref, r_ref, o_ref):
        o_ref[...] = (g_ref[...].astype(jnp.float32) + r_ref[...].astype(jnp.float32)).astype(o_ref.dtype)

    return pl.pallas_call(
        body, name=name,
        grid_spec=pltpu.PrefetchScalarGridSpec(
            num_scalar_prefetch=1, grid=(N_SHARD, r // rb),
            in_specs=[pl.BlockSpec((None, None, rb, D), lambda s, i, c_ref: (c_ref[0], s, i, 0)),
                      pl.BlockSpec((None, rb, D), lambda s, i, c_ref: (s, i, 0))],
            out_specs=pl.BlockSpec((None, rb, D), lambda s, i, c_ref: (s, i, 0))),
        out_shape=jax.ShapeDtypeStruct((N_SHARD, r, D), g.dtype),
        compiler_params=_params(("arbitrary", "arbitrary")),
    )(c, g, recv)


def _sum_chips(name, p, recv):
    _, r, _ = p.shape
    rb = _sum_rows_block(r)
    s = jnp.stack([2 * lax.axis_index("x") + lax.axis_index("y"), lax.axis_index("c")]).astype(jnp.int32)

    def body(s_ref, p_ref, r_ref, o_ref):
        acc = p_ref[...].astype(jnp.float32)
        for j in range(3):
            acc = acc + r_ref[j].astype(jnp.float32)
        o_ref[...] = acc

    return pl.pallas_call(
        body, name=name,
        grid_spec=pltpu.PrefetchScalarGridSpec(
            num_scalar_prefetch=1, grid=(r // rb,),
            in_specs=[pl.BlockSpec((None, rb, D), lambda i, s_ref: (s_ref[0], i, 0)),
                      pl.BlockSpec((3, rb, D), lambda i, s_ref: (0, i, 0))],
            out_specs=pl.BlockSpec((None, rb, D), lambda i, s_ref: (s_ref[1], i, 0))),
        out_shape=jax.ShapeDtypeStruct((2, r, D), jnp.float32),
        compiler_params=_params(),
    )(s, p, recv)


def _reduce_scatter(g):
    _, _, r, _ = g.shape
    (recv,) = _push("rs_d2d", [], [g], [jax.ShapeDtypeStruct((N_SHARD, r, D), g.dtype)],
                    [(_D2D_MASK, lambda al, ins, outs, me: ins[0].at[1 - me[2]], lambda al, ins, outs, me: outs[0])])
    p = _sum_pair("rs_sum_pair", g, recv)
    remote = []
    for j, mask in enumerate(_ICI_MASKS):
        remote.append((mask,
                       lambda al, ins, outs, me, mask=mask: ins[0].at[_chip(_flip(me, mask))],
                       lambda al, ins, outs, me, j=j: outs[0].at[j]))
    (recv3,) = _push("rs_ici", [], [p], [jax.ShapeDtypeStruct((3, r, D), g.dtype)], remote)
    mine = _sum_chips("rs_sum_chips", p, recv3)
    half = lambda al, ins, outs, me: al[0].at[me[2]]
    (both,) = _push("rs_share", [mine], [], [], [(_D2D_MASK, half, half)])
    return both


def _all_reduce_small(v):
    s = v.shape[0]
    me = _me()
    every = lax.dynamic_update_slice(lax.empty((8, s, D), jnp.float32), v[None], (4 * me[0] + 2 * me[1] + me[2], 0, 0))
    slot = lambda al, ins, outs, me: al[0].at[4 * me[0] + 2 * me[1] + me[2]]
    (every,) = _push("small_all", [every], [], [], [(mask, slot, slot) for mask in _ALL_MASKS])

    def body(e_ref, o_ref):
        acc = e_ref[0]
        for j in range(1, 8):
            acc = acc + e_ref[j]
        o_ref[...] = acc

    return pl.pallas_call(
        body, name="small_sum", grid=(1,),
        in_specs=[pl.BlockSpec((8, s, D), lambda i: (0, 0, 0))],
        out_specs=pl.BlockSpec((s, D), lambda i: (0, 0)),
        out_shape=jax.ShapeDtypeStruct((s, D), jnp.float32),
        compiler_params=_params(),
    )(every)


def _adamw(name, w, g, m, v):
    r, c = w.shape
    rb = next(b for b in (256, 344, 128, 64, 32, 16, 8, r) if r % b == 0)

    def body(w_ref, g_ref, m_ref, v_ref, d_ref, mo_ref, vo_ref):
        g = g_ref[...]
        m = ADAM_B1 * m_ref[...] + (1.0 - ADAM_B1) * g
        v = ADAM_B2 * v_ref[...] + (1.0 - ADAM_B2) * (g * g)
        m_hat = m / (1.0 - ADAM_B1 ** ADAM_STEP)
        v_hat = v / (1.0 - ADAM_B2 ** ADAM_STEP)
        d_ref[...] = -ADAM_LR * (m_hat / (jnp.sqrt(v_hat) + ADAM_EPS) + ADAM_WD * w_ref[...])
        mo_ref[...] = m
        vo_ref[...] = v

    spec = pl.BlockSpec((rb, c), lambda i: (i, 0))
    return pl.pallas_call(
        body, name=name, grid=(r // rb,),
        in_specs=[spec] * 4, out_specs=[spec] * 3,
        out_shape=[jax.ShapeDtypeStruct((r, c), jnp.float32)] * 3,
        compiler_params=_params(),
    )(w, g, m, v)


_BIG = ("ffn1_w13", "ffn2_w13", "ffn1_w2", "ffn2_w2", "w_in", "w_out", "conv_pw")
_BIG_SHARD = {"ffn1_w13": (D, 1376), "ffn2_w13": (D, 1376), "ffn1_w2": (688, D), "ffn2_w2": (688, D),
              "w_in": (D, 704), "w_out": (256, D), "conv_pw": (64, 256)}


def _pack_rows(parts):
    flat = jnp.concatenate([p.reshape(-1) for p in parts])
    pad = (-flat.shape[0]) % (8 * D)
    if pad:
        flat = jnp.concatenate([flat, jnp.zeros((pad,), flat.dtype)])
    return flat.reshape(-1, D)


def _unpack_rows(buf, shapes):
    flat = buf.reshape(-1)
    out, off = [], 0
    for shp in shapes:
        n = math.prod(shp)
        out.append(flat[off:off + n].reshape(shp))
        off += n
    return out


def _pack_big_half(parts):
    return jnp.concatenate([parts[n].reshape(-1, D) for n in _BIG], axis=0)


def _unpack_big_half(buf):
    out, off = {}, 0
    for n in _BIG:
        shp = _BIG_SHARD[n]
        rows = math.prod(shp) // D
        out[n] = buf[off:off + rows].reshape(shp)
        off += rows
    return out


def _pad_cols(a, n):
    return jnp.pad(a, ((0, 0), (0, n - a.shape[1])))


def kernel(x, meta, ln_in_g, ln_in_b, ffn1_w13, ffn1_w2, w_in, pool_w, pool_scale, conv_dw, conv_db, conv_ln_g, conv_ln_b, conv_pw, ret_gn_g, w_out, ffn2_w13, ffn2_w2, ln_g, ln_b, loss_target, m_meta, m_ln_in_g, m_ln_in_b, m_ffn1_w13, m_ffn1_w2, m_w_in, m_pool_w, m_pool_scale, m_conv_dw, m_conv_db, m_conv_ln_g, m_conv_ln_b, m_conv_pw, m_ret_gn_g, m_w_out, m_ffn2_w13, m_ffn2_w2, m_ln_g, m_ln_b, v_meta, v_ln_in_g, v_ln_in_b, v_ffn1_w13, v_ffn1_w2, v_w_in, v_pool_w, v_pool_scale, v_conv_dw, v_conv_db, v_conv_ln_g, v_conv_ln_b, v_conv_pw, v_ret_gn_g, v_w_out, v_ffn2_w13, v_ffn2_w2, v_ln_g, v_ln_b):
    f32 = jnp.float32
    seq = x.shape[1]
    t = seq + ROW0
    me = _me()
    chip = _chip(me)
    big_w = {"ffn1_w13": ffn1_w13, "ffn2_w13": ffn2_w13, "ffn1_w2": ffn1_w2, "ffn2_w2": ffn2_w2,
             "w_in": w_in, "w_out": w_out, "conv_pw": conv_pw}

    wb = jnp.stack([_pack_big_half({n: big_w[n][l].astype(_WIRE) for n in _BIG}) for l in range(DEPTH)])
    small_shapes = [(N_META, 256), (DEPTH, CONV_W, 64), (DEPTH, 3, 256), (DEPTH, 3, 256)]
    small = _pack_rows([meta, conv_dw, ln_g, ln_b])
    wfull, small_all = _gather_weights(wb, small)

    sm = [_unpack_rows(small_all[s], small_shapes) for s in range(N_SHARD)]
    meta_f = jnp.concatenate([sm[s][0] for s in range(N_SHARD)], axis=1)
    cdw_f = jnp.concatenate([sm[s][1] for s in range(N_SHARD)], axis=2)
    lng_f = jnp.concatenate([sm[s][2] for s in range(N_SHARD)], axis=2)
    lnb_f = jnp.concatenate([sm[s][3] for s in range(N_SHARD)], axis=2)

    def layer_weights(l):
        sh = [_unpack_big_half(wfull[l, s]) for s in range(N_SHARD)]
        out = {}
        for n in ("ffn1", "ffn2"):
            out[n + "_w13"] = jnp.concatenate([_pad_cols(sh[s][n + "_w13"], FF_SLOT) for s in range(N_SHARD)], axis=1)
            zrows = jnp.zeros((FF_SLOT - 1376, D), _WIRE)
            out[n + "_w2"] = jnp.concatenate([sh[0][n + "_w2"], sh[1][n + "_w2"], zrows,
                                              sh[2][n + "_w2"], sh[3][n + "_w2"], zrows], axis=0)
        out["w_in"] = jnp.concatenate([sh[s]["w_in"] for s in range(N_SHARD)], axis=1)
        out["w_out"] = jnp.concatenate([sh[s]["w_out"] for s in range(N_SHARD)], axis=0)
        out["conv_pw"] = jnp.concatenate([sh[s]["conv_pw"] for s in range(N_SHARD)], axis=0)
        return out

    wl = [layer_weights(l) for l in range(DEPTH)]

    def mix_params(l):
        wbd = jnp.zeros((D_POOL, D_POOL), f32)
        for g in range(4):
            wbd = wbd.at[64 * g:64 * (g + 1), 64 * g:64 * (g + 1)].set(pool_w[l, g])
        cdw = jnp.pad(cdw_f[l], ((0, 1), (0, 0)))
        cvec = jnp.pad(jnp.stack([conv_db[l], conv_ln_g[l], conv_ln_b[l]]), ((0, 5), (0, 0)))
        return (wbd.astype(_MM), pool_scale[l][None], cdw, cvec, wl[l]["conv_pw"], ret_gn_g[l][None])

    gb_of = lambda l, i: jnp.stack([lng_f[l, i], lnb_f[l, i]])
    gb_in = jnp.stack([ln_in_g, ln_in_b])

    pos = jnp.arange(t, dtype=f32) - PAD
    inv_freq = ROPE_BASE ** (-jnp.arange(0, DH, 2, dtype=f32) / DH)
    ang = pos[:, None] * inv_freq[None, :]
    cs = jnp.concatenate([jnp.cos(ang), jnp.cos(ang)], axis=1)
    sn = jnp.concatenate([-jnp.sin(ang), jnp.sin(ang)], axis=1)

    raw = jnp.concatenate([jnp.zeros((PAD, D), f32), meta_f, x[0]], axis=0)
    target = jnp.concatenate([jnp.zeros((ROW0, D), f32), loss_target[0]], axis=0)
    acts = []
    xh, rstd = _ln_in_fwd(raw)
    cur = (xh, rstd, gb_in)
    saved = []
    for l in range(DEPTH):
        w = wl[l]
        a0 = cur
        xh1, r1 = _ffn_fwd(f"ffn1_fwd_{l}", a0[0], a0[2], w["ffn1_w13"], w["ffn1_w2"])
        a1 = (xh1, r1, gb_of(l, 0))
        z = _mix_in_fwd(f"mix_in_fwd_{l}", a1[0], a1[2], w["w_in"])
        mp = mix_params(l)
        ycat, st_in, ut_in, xt_in = _mix_core_fwd(f"mix_core_fwd_{l}", z, cs, sn, *mp)
        xh2, r2 = _mix_out_fwd(f"mix_out_fwd_{l}", a1[0], a1[2], ycat, w["w_out"])
        a2 = (xh2, r2, gb_of(l, 1))
        xh3, r3 = _ffn_fwd(f"ffn2_fwd_{l}", a2[0], a2[2], w["ffn2_w13"], w["ffn2_w2"])
        a3 = (xh3, r3, gb_of(l, 2))
        saved.append((a0, a1, a2, a3, z, ycat, st_in, ut_in, xt_in, mp))
        cur = a3

    dy, loss_part = _loss_fwd_bwd(cur[0], cur[2], target)
    loss = lax.psum(loss_part[0, 0], ("x", "y", "c"))

    gw = [dict() for _ in range(DEPTH)]
    g_ln_g = [[None] * 3 for _ in range(DEPTH)]
    g_ln_b = [[None] * 3 for _ in range(DEPTH)]
    g_small = [dict() for _ in range(DEPTH)]
    for l in reversed(range(DEPTH)):
        a0, a1, a2, a3, z, ycat, st_in, ut_in, xt_in, mp = saved[l]
        w = wl[l]
        dh, hb, hid, dau, dffn, dgb = _ffn_bwd(f"ffn2_bwd_{l}", dy, a3[0], a3[1], a3[2], a2[0], a2[2],
                                               w["ffn2_w13"], w["ffn2_w2"])
        g_ln_g[l][2], g_ln_b[l][2] = dgb[0], dgb[1]
        gw[l]["ffn2_w13"] = _matmul_tn(f"dw13_ffn2_{l}", hb, dau, FF_SLOT, _WIRE)
        gw[l]["ffn2_w2"] = _matmul_tn(f"dw2_ffn2_{l}", hid, dffn, D, _WIRE)
        dh_res, dycat, dsb, dgb = _mix_out_bwd(f"mix_out_bwd_{l}", dh, a2[0], a2[1], a2[2], w["w_out"])
        g_ln_g[l][1], g_ln_b[l][1] = dgb[0], dgb[1]
        gw[l]["w_out"] = _matmul_tn(f"dw_out_{l}", ycat, dsb, D, _WIRE)
        dz, dwbd, dwpw, dcdw, dsm = _mix_core_bwd(f"mix_core_bwd_{l}", z, dycat, cs, sn, st_in, ut_in, xt_in, *mp)
        gw[l]["conv_pw"] = dwpw.astype(_WIRE)
        g_small[l] = dict(
            pool_w=jnp.stack([dwbd[64 * g:64 * (g + 1), 64 * g:64 * (g + 1)] for g in range(4)]),
            pool_scale=dsm[1, :256], conv_db=dsm[2, :256], conv_ln_g=dsm[3, :256], conv_ln_b=dsm[4, :256],
            ret_gn_g=dsm[0], conv_dw=dcdw[:CONV_W])
        dh, hb = _mix_in_bwd(f"mix_in_bwd_{l}", dh_res, dz, a1[0], a1[2], w["w_in"])
        gw[l]["w_in"] = _matmul_tn(f"dw_in_{l}", hb, dz, D_IN, _WIRE)
        dh, hb, hid, dau, dffn, dgb = _ffn_bwd(f"ffn1_bwd_{l}", dh, a1[0], a1[1], a1[2], a0[0], a0[2],
                                               w["ffn1_w13"], w["ffn1_w2"])
        g_ln_g[l][0], g_ln_b[l][0] = dgb[0], dgb[1]
        gw[l]["ffn1_w13"] = _matmul_tn(f"dw13_ffn1_{l}", hb, dau, FF_SLOT, _WIRE)
        gw[l]["ffn1_w2"] = _matmul_tn(f"dw2_ffn1_{l}", hid, dffn, D, _WIRE)
        dy = dh
    d_raw, dgb_in = _ln_in_bwd(dy, saved[0][0][0], saved[0][0][1], gb_in)
    grad_x = d_raw[ROW0:][None]

    def shard_slices(l, s):
        g = gw[l]
        w2_off = 688 * s + (FF_SLOT - 1376) * (s // 2)
        return {
            "ffn1_w13": g["ffn1_w13"][:, FF_SLOT * s:FF_SLOT * s + 1376],
            "ffn2_w13": g["ffn2_w13"][:, FF_SLOT * s:FF_SLOT * s + 1376],
            "ffn1_w2": g["ffn1_w2"][w2_off:w2_off + 688],
            "ffn2_w2": g["ffn2_w2"][w2_off:w2_off + 688],
            "w_in": g["w_in"][:, 704 * s:704 * (s + 1)],
            "w_out": g["w_out"][256 * s:256 * (s + 1)],
            "conv_pw": g["conv_pw"][64 * s:64 * (s + 1)],
        }

    gbuf = jnp.stack([jnp.stack([_pack_big_half(shard_slices(l, s)) for s in range(N_SHARD)])
                      for l in range(DEPTH)])
    gsum = _reduce_scatter(gbuf)
    g_big = [_unpack_big_half(gsum[l]) for l in range(DEPTH)]
    grads = {n: jnp.stack([g_big[l][n] for l in range(DEPTH)]) for n in _BIG}

    small_parts = [
        d_raw[PAD:ROW0],
        jnp.stack([g_small[l]["conv_dw"] for l in range(DEPTH)]),
        jnp.stack([jnp.stack(g_ln_g[l]) for l in range(DEPTH)]),
        jnp.stack([jnp.stack(g_ln_b[l]) for l in range(DEPTH)]),
        dgb_in[0], dgb_in[1],
        jnp.stack([g_small[l]["pool_w"] for l in range(DEPTH)]),
        jnp.stack([g_small[l]["pool_scale"] for l in range(DEPTH)]),
        jnp.stack([g_small[l]["conv_db"] for l in range(DEPTH)]),
        jnp.stack([g_small[l]["conv_ln_g"] for l in range(DEPTH)]),
        jnp.stack([g_small[l]["conv_ln_b"] for l in range(DEPTH)]),
        jnp.stack([g_small[l]["ret_gn_g"] for l in range(DEPTH)]),
    ]
    red = _unpack_rows(_all_reduce_small(_pack_rows(small_parts)), [p.shape for p in small_parts])
    grads["meta"] = lax.dynamic_slice_in_dim(red[0], 256 * chip, 256, axis=1)
    grads["conv_dw"] = lax.dynamic_slice_in_dim(red[1], 64 * chip, 64, axis=2)
    grads["ln_g"] = lax.dynamic_slice_in_dim(red[2], 256 * chip, 256, axis=2)
    grads["ln_b"] = lax.dynamic_slice_in_dim(red[3], 256 * chip, 256, axis=2)
    for n, v in zip(("ln_in_g", "ln_in_b", "pool_w", "pool_scale", "conv_db", "conv_ln_g", "conv_ln_b", "ret_gn_g"),
                    red[4:]):
        grads[n] = v

    names = ['meta', 'ln_in_g', 'ln_in_b', 'ffn1_w13', 'ffn1_w2', 'w_in', 'pool_w', 'pool_scale', 'conv_dw',
             'conv_db', 'conv_ln_g', 'conv_ln_b', 'conv_pw', 'ret_gn_g', 'w_out', 'ffn2_w13', 'ffn2_w2', 'ln_g', 'ln_b']
    ws = dict(meta=meta, ln_in_g=ln_in_g, ln_in_b=ln_in_b, ffn1_w13=ffn1_w13, ffn1_w2=ffn1_w2, w_in=w_in,
              pool_w=pool_w, pool_scale=pool_scale, conv_dw=conv_dw, conv_db=conv_db, conv_ln_g=conv_ln_g,
              conv_ln_b=conv_ln_b, conv_pw=conv_pw, ret_gn_g=ret_gn_g, w_out=w_out, ffn2_w13=ffn2_w13,
              ffn2_w2=ffn2_w2, ln_g=ln_g, ln_b=ln_b)
    ms = dict(meta=m_meta, ln_in_g=m_ln_in_g, ln_in_b=m_ln_in_b, ffn1_w13=m_ffn1_w13, ffn1_w2=m_ffn1_w2,
              w_in=m_w_in, pool_w=m_pool_w, pool_scale=m_pool_scale, conv_dw=m_conv_dw, conv_db=m_conv_db,
              conv_ln_g=m_conv_ln_g, conv_ln_b=m_conv_ln_b, conv_pw=m_conv_pw, ret_gn_g=m_ret_gn_g,
              w_out=m_w_out, ffn2_w13=m_ffn2_w13, ffn2_w2=m_ffn2_w2, ln_g=m_ln_g, ln_b=m_ln_b)
    vs = dict(meta=v_meta, ln_in_g=v_ln_in_g, ln_in_b=v_ln_in_b, ffn1_w13=v_ffn1_w13, ffn1_w2=v_ffn1_w2,
              w_in=v_w_in, pool_w=v_pool_w, pool_scale=v_pool_scale, conv_dw=v_conv_dw, conv_db=v_conv_db,
              conv_ln_g=v_conv_ln_g, conv_ln_b=v_conv_ln_b, conv_pw=v_conv_pw, ret_gn_g=v_ret_gn_g,
              w_out=v_w_out, ffn2_w13=v_ffn2_w13, ffn2_w2=v_ffn2_w2, ln_g=v_ln_g, ln_b=v_ln_b)
    delta, new_m, new_v = {}, {}, {}
    for n in _BIG:
        shp = ws[n].shape
        two = lambda a: a.reshape(-1, shp[-1])
        d_, m_, v_ = _adamw("adamw_" + n, two(ws[n]), two(grads[n]), two(ms[n]), two(vs[n]))
        delta[n], new_m[n], new_v[n] = d_.reshape(shp), m_.reshape(shp), v_.reshape(shp)
    small_names = [n for n in names if n not in _BIG]
    pk = lambda d: _pack_rows([d[n] for n in small_names])
    d_, m_, v_ = _adamw("adamw_small", pk(ws), pk(grads), pk(ms), pk(vs))
    shapes = [ws[n].shape for n in small_names]
    for n, a, b, c in zip(small_names, _unpack_rows(d_, shapes), _unpack_rows(m_, shapes), _unpack_rows(v_, shapes)):
        delta[n], new_m[n], new_v[n] = a, b, c

    return (loss, grad_x, *[grads[n] for n in names], *[delta[n] for n in names],
            *[new_m[n] for n in names], *[new_v[n] for n in names])
```

```python
import functools
import math

import jax
import jax.numpy as jnp
from jax import lax
from jax.experimental import pallas as pl
from jax.experimental.pallas import tpu as pltpu

D = 1024
DEPTH = 2
N_META = 16
PAD = 112
ROW0 = PAD + N_META
D_POOL = 256
D_CONV = 256
D_RET = 512
HEADS = 4
DH = 128
CONV_W = 31
D_FF = 2752
FF_SLOT = 1408
D_FFP = 2 * FF_SLOT
D_IN = 2816
N_SHARD = 4
ALPHA = (2.0 * DEPTH) ** 0.25
LN_EPS = 1e-5
ROPE_BASE = 10000.0
LOG_GAMMA = tuple(math.log(1.0 - 2.0 ** (-5.0 - h)) for h in range(HEADS))
ADAM_LR, ADAM_B1, ADAM_B2, ADAM_EPS, ADAM_WD, ADAM_STEP = 0.001, 0.9, 0.999, 1e-08, 0.01, 10

_MM = jnp.bfloat16
_WIRE = jnp.bfloat16
_VMEM_LIMIT = 56 * 1024 * 1024

MESH = pl.DeviceIdType.MESH
_ANY = pl.BlockSpec(memory_space=pl.ANY)

W13_ROWS = 1376
W2_ROWS = 688
WIN_ROWS = 704
WOUT_ROWS = 256
PW_ROWS = 16
OFF_W13 = (0, W13_ROWS)
OFF_W2 = (2 * W13_ROWS, 2 * W13_ROWS + W2_ROWS)
OFF_WIN = 2 * W13_ROWS + 2 * W2_ROWS
OFF_WOUT = OFF_WIN + WIN_ROWS
OFF_PW = OFF_WOUT + WOUT_ROWS
SHARD_ROWS = OFF_PW + PW_ROWS
W2_SLOT_OFF = (0, W2_ROWS, FF_SLOT, FF_SLOT + W2_ROWS)


def _start_rows(w_ref, layer, off, n, dst_of, sems, k0):
    cps = []
    for s in range(N_SHARD):
        cp = pltpu.make_async_copy(w_ref.at[layer, s, pl.ds(off, n)], dst_of(s), sems.at[k0 + s])
        cp.start()
        cps.append(cp)
    return cps


def _load_ffn_weights(w_ref, layer, f, w13, w2, sems):
    cps = _start_rows(w_ref, layer, OFF_W13[f], W13_ROWS, lambda s: w13.at[s, pl.ds(0, W13_ROWS)], sems, 0)
    cps += _start_rows(w_ref, layer, OFF_W2[f], W2_ROWS, lambda s: w2.at[pl.ds(W2_SLOT_OFF[s], W2_ROWS)], sems, 4)
    zpad = jnp.zeros((FF_SLOT - W13_ROWS, D), w13.dtype)
    for s in range(N_SHARD):
        w13[s, W13_ROWS:FF_SLOT, :] = zpad
    w2[W13_ROWS:FF_SLOT, :] = zpad
    w2[FF_SLOT + W13_ROWS:D_FFP, :] = zpad
    for cp in cps:
        cp.wait()


def _load_rows(w_ref, layer, off, n, dst, sems):
    for cp in _start_rows(w_ref, layer, off, n, lambda s: dst.at[pl.ds(s * n, n)], sems, 0):
        cp.wait()


def _dot(a, b):
    return jnp.dot(a, b, preferred_element_type=jnp.float32)


def _dot_nt(a, b):
    return lax.dot_general(a, b, (((1,), (1,)), ((), ())), preferred_element_type=jnp.float32)


def _dot_tn(a, b):
    return lax.dot_general(a, b, (((0,), (0,)), ((), ())), preferred_element_type=jnp.float32)


def _params(sem=("arbitrary",)):
    return pltpu.CompilerParams(dimension_semantics=sem, vmem_limit_bytes=_VMEM_LIMIT)


def _row_block(t, cap=640):
    for rb in (640, 320, 128):
        if rb <= cap and t % rb == 0 and (t > 1024 or rb == 128):
            return rb
    raise ValueError(t)


def _rows(rb, n):
    return pl.BlockSpec((rb, n), lambda i: (i, 0))


def _full(shape):
    nd = len(shape)
    return pl.BlockSpec(tuple(shape), lambda i: (0,) * nd, pipeline_mode=pl.Buffered(1))


def _acc(shape):
    nd = len(shape)
    return pl.BlockSpec(tuple(shape), lambda i: (0,) * nd)


def _sigmoid(x):
    return 1.0 / (1.0 + jnp.exp(-x))


def _ln_fwd(s):
    mu = jnp.mean(s, axis=-1, keepdims=True)
    xc = s - mu
    var = jnp.mean(xc * xc, axis=-1, keepdims=True)
    rstd = lax.rsqrt(var + LN_EPS)
    return xc * rstd, rstd


def _ln_bwd(dxh, xh, rstd):
    m1 = jnp.mean(dxh, axis=-1, keepdims=True)
    m2 = jnp.mean(dxh * xh, axis=-1, keepdims=True)
    return rstd * (dxh - m1 - xh * m2)


def _ln_in_fwd(raw):
    t = raw.shape[0]
    rb = _row_block(t)

    def body(raw_ref, xh_ref, rstd_ref):
        xh, rstd = _ln_fwd(raw_ref[...])
        xh_ref[...] = xh
        rstd_ref[...] = rstd

    return pl.pallas_call(
        body, name="ln_in_fwd", grid=(t // rb,),
        in_specs=[_rows(rb, D)],
        out_specs=[_rows(rb, D), _rows(rb, 1)],
        out_shape=[jax.ShapeDtypeStruct((t, D), jnp.float32), jax.ShapeDtypeStruct((t, 1), jnp.float32)],
        compiler_params=_params(),
    )(raw)


def _ffn_fwd(name, xh, gb, wfull, layer, f):
    t = xh.shape[0]
    rb = _row_block(t)

    def body(xh_ref, gb_ref, w_ref, out_ref, rstd_ref, w13, w2, sems):
        @pl.when(pl.program_id(0) == 0)
        def _():
            _load_ffn_weights(w_ref, layer, f, w13, w2, sems)

        h = xh_ref[...] * gb_ref[0:1, :] + gb_ref[1:2, :]
        hb = h.astype(_MM)
        acc = jnp.zeros((rb, D), jnp.float32)
        for j in range(2):
            a = _dot_nt(hb, w13[j])
            u = _dot_nt(hb, w13[2 + j])
            hid = (a * _sigmoid(a) * u).astype(_MM)
            acc = acc + _dot(hid, w2[j * FF_SLOT:(j + 1) * FF_SLOT, :])
        xo, rstd = _ln_fwd(ALPHA * h + 0.5 * acc)
        out_ref[...] = xo
        rstd_ref[...] = rstd

    return pl.pallas_call(
        body, name=name, grid=(t // rb,),
        in_specs=[_rows(rb, D), _full((2, D)), _ANY],
        out_specs=[_rows(rb, D), _rows(rb, 1)],
        out_shape=[jax.ShapeDtypeStruct((t, D), jnp.float32), jax.ShapeDtypeStruct((t, 1), jnp.float32)],
        scratch_shapes=[pltpu.VMEM((N_SHARD, FF_SLOT, D), _MM), pltpu.VMEM((D_FFP, D), _MM),
                        pltpu.SemaphoreType.DMA((8,))],
        compiler_params=_params(),
    )(xh, gb, wfull)


def _ffn_bwd(name, dy, xo, rstd, gb_out, xh, gb, wfull, layer, f):
    t = xh.shape[0]
    rb = _row_block(t, 320)

    def body(dy_ref, xo_ref, rstd_ref, gbo_ref, xh_ref, gb_ref, w_ref,
             dh_ref, hb_ref, hid_ref, dau_ref, dffn_ref, dgb_ref, w13, w2, sems):
        i = pl.program_id(0)

        @pl.when(i == 0)
        def _():
            dgb_ref[...] = jnp.zeros_like(dgb_ref)
            _load_ffn_weights(w_ref, layer, f, w13, w2, sems)

        dy = dy_ref[...]
        xo = xo_ref[...]
        dgb_ref[0:1, :] += jnp.sum(dy * xo, axis=0, keepdims=True)
        dgb_ref[1:2, :] += jnp.sum(dy, axis=0, keepdims=True)
        ds = _ln_bwd(dy * gbo_ref[0:1, :], xo, rstd_ref[...])
        dffn = (0.5 * ds).astype(_MM)
        dffn_ref[...] = dffn
        h = xh_ref[...] * gb_ref[0:1, :] + gb_ref[1:2, :]
        hb = h.astype(_MM)
        hb_ref[...] = hb
        dh = ALPHA * ds
        for j in range(2):
            lo = j * FF_SLOT
            a = _dot_nt(hb, w13[j])
            u = _dot_nt(hb, w13[2 + j])
            sg = _sigmoid(a)
            si = a * sg
            hid_ref[:, lo:lo + FF_SLOT] = (si * u).astype(_MM)
            dhid = _dot_nt(dffn, w2[lo:lo + FF_SLOT, :])
            da = (dhid * u * (sg * (1.0 + a * (1.0 - sg)))).astype(_MM)
            du = (dhid * si).astype(_MM)
            dau_ref[:, lo:lo + FF_SLOT] = da
            dau_ref[:, D_FFP + lo:D_FFP + lo + FF_SLOT] = du
            dh = dh + _dot(da, w13[j]) + _dot(du, w13[2 + j])
        dh_ref[...] = dh

    return pl.pallas_call(
        body, name=name, grid=(t // rb,),
        in_specs=[_rows(rb, D), _rows(rb, D), _rows(rb, 1), _full((2, D)), _rows(rb, D), _full((2, D)), _ANY],
        out_specs=[_rows(rb, D), _rows(rb, D), _rows(rb, D_FFP), _rows(rb, 2 * D_FFP), _rows(rb, D), _acc((8, D))],
        out_shape=[jax.ShapeDtypeStruct((t, D), jnp.float32), jax.ShapeDtypeStruct((t, D), _MM),
                   jax.ShapeDtypeStruct((t, D_FFP), _MM), jax.ShapeDtypeStruct((t, 2 * D_FFP), _MM),
                   jax.ShapeDtypeStruct((t, D), _MM), jax.ShapeDtypeStruct((8, D), jnp.float32)],
        scratch_shapes=[pltpu.VMEM((N_SHARD, FF_SLOT, D), _MM), pltpu.VMEM((D_FFP, D), _MM),
                        pltpu.SemaphoreType.DMA((8,))],
        compiler_params=_params(),
    )(dy, xo, rstd, gb_out, xh, gb, wfull)


def _mix_in_fwd(name, xh, gb, wfull, layer):
    t = xh.shape[0]
    rb = _row_block(t)

    def body(xh_ref, gb_ref, w_ref, z_ref, wt, sems):
        @pl.when(pl.program_id(0) == 0)
        def _():
            _load_rows(w_ref, layer, OFF_WIN, WIN_ROWS, wt, sems)

        h = xh_ref[...] * gb_ref[0:1, :] + gb_ref[1:2, :]
        z = _dot_nt(h.astype(_MM), wt[...])
        row = pl.program_id(0) * rb + lax.broadcasted_iota(jnp.int32, (rb, 1), 0)
        z_ref[...] = jnp.where(row >= PAD, z, 0.0)

    return pl.pallas_call(
        body, name=name, grid=(t // rb,),
        in_specs=[_rows(rb, D), _full((2, D)), _ANY],
        out_specs=_rows(rb, D_IN),
        out_shape=jax.ShapeDtypeStruct((t, D_IN), jnp.float32),
        scratch_shapes=[pltpu.VMEM((D_IN, D), _MM), pltpu.SemaphoreType.DMA((4,))],
        compiler_params=_params(),
    )(xh, gb, wfull)


def _mix_in_bwd(name, dh_res, dz, xh, gb, wfull, layer):
    t = xh.shape[0]
    rb = _row_block(t)

    def body(dhr_ref, dz_ref, xh_ref, gb_ref, w_ref, dh_ref, hb_ref, wt, sems):
        @pl.when(pl.program_id(0) == 0)
        def _():
            _load_rows(w_ref, layer, OFF_WIN, WIN_ROWS, wt, sems)

        dh_ref[...] = dhr_ref[...] + _dot(dz_ref[...], wt[...])
        hb_ref[...] = (xh_ref[...] * gb_ref[0:1, :] + gb_ref[1:2, :]).astype(_MM)

    return pl.pallas_call(
        body, name=name, grid=(t // rb,),
        in_specs=[_rows(rb, D), _rows(rb, D_IN), _rows(rb, D), _full((2, D)), _ANY],
        out_specs=[_rows(rb, D), _rows(rb, D)],
        out_shape=[jax.ShapeDtypeStruct((t, D), jnp.float32), jax.ShapeDtypeStruct((t, D), _MM)],
        scratch_shapes=[pltpu.VMEM((D_IN, D), _MM), pltpu.SemaphoreType.DMA((4,))],
        compiler_params=_params(),
    )(dh_res, dz, xh, gb, wfull)


def _mix_out_fwd(name, xh, gb, ycat, wfull, layer):
    t = xh.shape[0]
    rb = _row_block(t)

    def body(xh_ref, gb_ref, y_ref, w_ref, out_ref, rstd_ref, wo, sems):
        @pl.when(pl.program_id(0) == 0)
        def _():
            _load_rows(w_ref, layer, OFF_WOUT, WOUT_ROWS, wo, sems)

        h = xh_ref[...] * gb_ref[0:1, :] + gb_ref[1:2, :]
        xo, rstd = _ln_fwd(ALPHA * h + _dot(y_ref[...], wo[...]))
        out_ref[...] = xo
        rstd_ref[...] = rstd

    return pl.pallas_call(
        body, name=name, grid=(t // rb,),
        in_specs=[_rows(rb, D), _full((2, D)), _rows(rb, D), _ANY],
        out_specs=[_rows(rb, D), _rows(rb, 1)],
        out_shape=[jax.ShapeDtypeStruct((t, D), jnp.float32), jax.ShapeDtypeStruct((t, 1), jnp.float32)],
        scratch_shapes=[pltpu.VMEM((D, D), _MM), pltpu.SemaphoreType.DMA((4,))],
        compiler_params=_params(),
    )(xh, gb, ycat, wfull)


def _mix_out_bwd(name, dy, xo, rstd, gb_out, wfull, layer):
    t = xo.shape[0]
    rb = _row_block(t)

    def body(dy_ref, xo_ref, rstd_ref, gbo_ref, w_ref, dhr_ref, dyc_ref, dsb_ref, dgb_ref, wo, sems):
        @pl.when(pl.program_id(0) == 0)
        def _():
            dgb_ref[...] = jnp.zeros_like(dgb_ref)
            _load_rows(w_ref, layer, OFF_WOUT, WOUT_ROWS, wo, sems)

        dy = dy_ref[...]
        xo = xo_ref[...]
        dgb_ref[0:1, :] += jnp.sum(dy * xo, axis=0, keepdims=True)
        dgb_ref[1:2, :] += jnp.sum(dy, axis=0, keepdims=True)
        ds = _ln_bwd(dy * gbo_ref[0:1, :], xo, rstd_ref[...])
        dsb = ds.astype(_MM)
        dsb_ref[...] = dsb
        dhr_ref[...] = ALPHA * ds
        dyc_ref[...] = _dot_nt(dsb, wo[...])

    return pl.pallas_call(
        body, name=name, grid=(t // rb,),
        in_specs=[_rows(rb, D), _rows(rb, D), _rows(rb, 1), _full((2, D)), _ANY],
        out_specs=[_rows(rb, D), _rows(rb, D), _rows(rb, D), _acc((8, D))],
        out_shape=[jax.ShapeDtypeStruct((t, D), jnp.float32), jax.ShapeDtypeStruct((t, D), jnp.float32),
                   jax.ShapeDtypeStruct((t, D), _MM), jax.ShapeDtypeStruct((8, D), jnp.float32)],
        scratch_shapes=[pltpu.VMEM((D, D), _MM), pltpu.SemaphoreType.DMA((4,))],
        compiler_params=_params(),
    )(dy, xo, rstd, gb_out, wfull)


def _loss_fwd_bwd(xh, gb, target):
    t = xh.shape[0]
    rb = _row_block(t)

    def body(xh_ref, gb_ref, tg_ref, dy_ref, loss_ref):
        @pl.when(pl.program_id(0) == 0)
        def _():
            loss_ref[...] = jnp.zeros_like(loss_ref)

        y = xh_ref[...] * gb_ref[0:1, :] + gb_ref[1:2, :]
        row = pl.program_id(0) * rb + lax.broadcasted_iota(jnp.int32, (rb, 1), 0)
        err = jnp.where(row >= ROW0, y - tg_ref[...], 0.0)
        dy_ref[...] = err * (1.0 / D)
        per_row = jnp.mean(err * err, axis=-1, keepdims=True)
        loss_ref[...] += 0.5 * jnp.sum(per_row, axis=0, keepdims=True)

    return pl.pallas_call(
        body, name="loss", grid=(t // rb,),
        in_specs=[_rows(rb, D), _full((2, D)), _rows(rb, D)],
        out_specs=[_rows(rb, D), _acc((1, 1))],
        out_shape=[jax.ShapeDtypeStruct((t, D), jnp.float32), jax.ShapeDtypeStruct((1, 1), jnp.float32)],
        compiler_params=_params(),
    )(xh, gb, target)


def _ln_in_bwd(dy, xh, rstd, gb):
    t = xh.shape[0]
    rb = _row_block(t)

    def body(dy_ref, xh_ref, rstd_ref, gb_ref, dx_ref, dgb_ref):
        @pl.when(pl.program_id(0) == 0)
        def _():
            dgb_ref[...] = jnp.zeros_like(dgb_ref)

        dy = dy_ref[...]
        xh = xh_ref[...]
        dgb_ref[0:1, :] += jnp.sum(dy * xh, axis=0, keepdims=True)
        dgb_ref[1:2, :] += jnp.sum(dy, axis=0, keepdims=True)
        dx_ref[...] = _ln_bwd(dy * gb_ref[0:1, :], xh, rstd_ref[...])

    return pl.pallas_call(
        body, name="ln_in_bwd", grid=(t // rb,),
        in_specs=[_rows(rb, D), _rows(rb, D), _rows(rb, 1), _full((2, D))],
        out_specs=[_rows(rb, D), _acc((8, D))],
        out_shape=[jax.ShapeDtypeStruct((t, D), jnp.float32), jax.ShapeDtypeStruct((8, D), jnp.float32)],
        compiler_params=_params(),
    )(dy, xh, rstd, gb)


def _dw_into(name, gpack, layer, x, y, cols, pieces):
    t, k = x.shape
    tt = _row_block(t)
    nt = t // tt

    def body(x_ref, y_ref, g_in, g_out, acc, stage, sems):
        j = pl.program_id(0)
        s = pl.program_id(1)

        @pl.when(s == 0)
        def _():
            acc[...] = jnp.zeros_like(acc)

        acc[...] += _dot_tn(x_ref[...], y_ref[...])

        @pl.when(s == nt - 1)
        def _():
            stage[...] = acc[...].astype(stage.dtype)
            cps = []
            for q, (lo, n, chip_of, off) in enumerate(pieces):
                cp = pltpu.make_async_copy(stage.at[pl.ds(lo, n)], g_out.at[layer, chip_of(j), pl.ds(off, n)],
                                           sems.at[q])
                cp.start()
                cps.append(cp)
            for cp in cps:
                cp.wait()

    return pl.pallas_call(
        body, name=name, grid=(k // cols, nt),
        in_specs=[pl.BlockSpec((tt, cols), lambda j, s: (s, j)), pl.BlockSpec((tt, D), lambda j, s: (s, 0)), _ANY],
        out_specs=_ANY,
        out_shape=jax.ShapeDtypeStruct(gpack.shape, gpack.dtype),
        input_output_aliases={2: 0},
        scratch_shapes=[pltpu.VMEM((cols, D), jnp.float32), pltpu.VMEM((cols, D), gpack.dtype),
                        pltpu.SemaphoreType.DMA((len(pieces),))],
        compiler_params=_params(("arbitrary", "arbitrary")),
    )(x, y, gpack)


_TAIL_U = 32
_TAIL_X = 16
_MIX_ROWS = 320


def _decay_mask(rb, h):
    ii = lax.broadcasted_iota(jnp.int32, (rb, rb), 0)
    jj = lax.broadcasted_iota(jnp.int32, (rb, rb), 1)
    dist = jnp.abs(ii - jj).astype(jnp.float32)
    vis = (jj >> 6) <= (ii >> 6)
    return jnp.where(vis, jnp.exp(LOG_GAMMA[h] * dist), 0.0)


def _row_decays(rb, h):
    r = lax.broadcasted_iota(jnp.int32, (rb, DH), 0).astype(jnp.float32)
    return jnp.exp(LOG_GAMMA[h] * (r + 1.0)), jnp.exp(LOG_GAMMA[h] * (rb - 1.0 - r))


def _rope(x, cs, sn):
    return x * cs + pltpu.roll(x, DH // 2, 1) * sn


def _rope_t(dx, cs, sn):
    return dx * cs + pltpu.roll(dx * sn, DH // 2, 1)


def _pool_count(blk, rb):
    row = blk * rb + lax.broadcasted_iota(jnp.int32, (rb, D_POOL), 0) - PAD
    lane = lax.broadcasted_iota(jnp.int32, (rb, D_POOL), 1)
    win = jnp.left_shift(2, lane >> 6)
    return jnp.clip(row + 1, 1, win).astype(jnp.float32)


def _pool_select(p2, p4, p8, p16):
    lane = lax.broadcasted_iota(jnp.int32, p2.shape, 1)
    return jnp.where(lane < 64, p2, jnp.where(lane < 128, p4, jnp.where(lane < 192, p8, p16)))


def _window_sums(ext_ref, base, rows, sign):
    acc = ext_ref[pl.ds(base, rows), :]
    outs = []
    for k in range(1, 16):
        acc = acc + ext_ref[pl.ds(base + sign * k, rows), :]
        if k in (1, 3, 7, 15):
            outs.append(acc)
    return _pool_select(*outs)


def _sub_rows(rb):
    return 128 if rb % 128 == 0 else 64


def _mix_core_fwd(name, z, cs, sn, wbd, pscale, cdw, cvec, wpw, gn):
    t = z.shape[0]
    rb = _row_block(t, _MIX_ROWS)
    nblk = t // rb
    sr = _sub_rows(rb)

    def body(z_ref, cs_ref, sn_ref, wbd_ref, ps_ref, cdw_ref, cvec_ref, wpw_ref, gn_ref,
             y_ref, st_ref, ut_ref, xt_ref,
             uext, xext, cv, ypre, state, wmask):
        i = pl.program_id(0)

        @pl.when(i == 0)
        def _():
            state[...] = jnp.zeros_like(state)
            uext[0:_TAIL_U, :] = jnp.zeros((_TAIL_U, D_CONV), jnp.float32)
            xext[0:_TAIL_X, :] = jnp.zeros((_TAIL_X, D_POOL), jnp.float32)
            for h in range(HEADS):
                wmask[h] = _decay_mask(rb, h)

        st_ref[0] = state[...]
        ut_ref[0] = uext[0:_TAIL_U, :]
        xt_ref[0] = xext[0:_TAIL_X, :]

        xp = z_ref[:, 0:256]
        uext[_TAIL_U:_TAIL_U + rb, :] = z_ref[:, 256:512] * _sigmoid(z_ref[:, 512:768])
        xext[_TAIL_X:_TAIL_X + rb, :] = xp

        for r in range(0, rb, sr):
            win = _window_sums(xext, _TAIL_X + r, sr, -1)
            ypre[r:r + sr, :] = win
            acc = jnp.zeros((sr, D_CONV), jnp.float32)
            for k in range(CONV_W):
                acc = acc + uext[pl.ds(_TAIL_U + r - k, sr), :] * cdw_ref[CONV_W - 1 - k:CONV_W - k, :]
            cv[r:r + sr, :] = acc

        yp = ypre[...] / _pool_count(i, rb) - xp
        y_ref[:, 0:256] = (_dot(yp.astype(_MM), wbd_ref[...]) * ps_ref[...]).astype(_MM)
        cn, _ = _ln_fwd(cv[...] + cvec_ref[0:1, :])
        ln = cn * cvec_ref[1:2, :] + cvec_ref[2:3, :]
        sw = ln * _sigmoid(ln)
        y_ref[:, 256:512] = _dot(sw.astype(_MM), wpw_ref[...]).astype(_MM)
        csv = cs_ref[...]
        snv = sn_ref[...]
        for h in range(HEADS):
            q = _rope(z_ref[:, 768 + h * DH:768 + (h + 1) * DH], csv, snv)
            k = _rope(z_ref[:, 1280 + h * DH:1280 + (h + 1) * DH], csv, snv) * (DH ** -0.5)
            vb = z_ref[:, 1792 + h * DH:1792 + (h + 1) * DH].astype(_MM)
            g = z_ref[:, 2304 + h * DH:2304 + (h + 1) * DH]
            a, b = _row_decays(rb, h)
            s = _dot_nt(q.astype(_MM), k.astype(_MM)) * wmask[h]
            o = _dot(s.astype(_MM), vb) + _dot((q * a).astype(_MM), state[h].astype(_MM))
            state[h] = math.exp(LOG_GAMMA[h] * rb) * state[h] + _dot_tn((k * b).astype(_MM), vb)
            on, _ = _ln_fwd(o)
            y_ref[:, 512 + h * DH:512 + (h + 1) * DH] = (
                g * _sigmoid(g) * on * gn_ref[:, h * DH:(h + 1) * DH]).astype(_MM)

        uext[0:_TAIL_U, :] = uext[rb:rb + _TAIL_U, :]
        xext[0:_TAIL_X, :] = xext[rb:rb + _TAIL_X, :]

    return pl.pallas_call(
        body, name=name, grid=(nblk,),
        in_specs=[_rows(rb, D_IN), _rows(rb, DH), _rows(rb, DH), _full((256, 256)), _full((1, 256)),
                  _full((32, 256)), _full((8, 256)), _full((256, 256)), _full((1, D_RET))],
        out_specs=[_rows(rb, D),
                   pl.BlockSpec((1, HEADS, DH, DH), lambda i: (i, 0, 0, 0)),
                   pl.BlockSpec((1, _TAIL_U, D_CONV), lambda i: (i, 0, 0)),
                   pl.BlockSpec((1, _TAIL_X, D_POOL), lambda i: (i, 0, 0))],
        out_shape=[jax.ShapeDtypeStruct((t, D), _MM),
                   jax.ShapeDtypeStruct((nblk, HEADS, DH, DH), jnp.float32),
                   jax.ShapeDtypeStruct((nblk, _TAIL_U, D_CONV), jnp.float32),
                   jax.ShapeDtypeStruct((nblk, _TAIL_X, D_POOL), jnp.float32)],
        scratch_shapes=[pltpu.VMEM((rb + _TAIL_U, D_CONV), jnp.float32),
                        pltpu.VMEM((rb + _TAIL_X, D_POOL), jnp.float32),
                        pltpu.VMEM((rb, D_CONV), jnp.float32),
                        pltpu.VMEM((rb, D_POOL), jnp.float32),
                        pltpu.VMEM((HEADS, DH, DH), jnp.float32),
                        pltpu.VMEM((HEADS, rb, rb), jnp.float32)],
        compiler_params=_params(),
    )(z, cs, sn, wbd, pscale, cdw, cvec, wpw, gn)


def _mix_core_bwd(name, z, dyc, cs, sn, st_in, ut_in, xt_in, wbd, pscale, cdw, cvec, wpw, gn):
    t = z.shape[0]
    rb = _row_block(t, _MIX_ROWS)
    nblk = t // rb
    sr = _sub_rows(rb)
    rev = lambda i: nblk - 1 - i

    def body(z_ref, dy_ref, cs_ref, sn_ref, st_ref, ut_ref, xt_ref,
             wbd_ref, ps_ref, cdw_ref, cvec_ref, wpw_ref, gn_ref,
             dz_ref, dwbd_ref, dwpw_ref, dcdw_ref, dsm_ref,
             uext, xext, cv, ypre, dcvext, eext, dstate, wmask):
        i = pl.program_id(0)
        blk = nblk - 1 - i

        @pl.when(i == 0)
        def _():
            dstate[...] = jnp.zeros_like(dstate)
            dcvext[rb:rb + _TAIL_U, :] = jnp.zeros((_TAIL_U, D_CONV), jnp.float32)
            eext[rb:rb + _TAIL_X, :] = jnp.zeros((_TAIL_X, D_POOL), jnp.float32)
            dwbd_ref[...] = jnp.zeros_like(dwbd_ref)
            dwpw_ref[...] = jnp.zeros_like(dwpw_ref)
            dcdw_ref[...] = jnp.zeros_like(dcdw_ref)
            dsm_ref[...] = jnp.zeros_like(dsm_ref)
            for h in range(HEADS):
                wmask[h] = _decay_mask(rb, h)

        row = blk * rb + lax.broadcasted_iota(jnp.int32, (rb, 1), 0)
        live = row >= PAD

        xp = z_ref[:, 0:256]
        ca = z_ref[:, 256:512]
        sg_c = _sigmoid(z_ref[:, 512:768])
        uext[0:_TAIL_U, :] = ut_ref[0]
        xext[0:_TAIL_X, :] = xt_ref[0]
        uext[_TAIL_U:_TAIL_U + rb, :] = ca * sg_c
        xext[_TAIL_X:_TAIL_X + rb, :] = xp
        for r in range(0, rb, sr):
            ypre[r:r + sr, :] = _window_sums(xext, _TAIL_X + r, sr, -1)
            acc = jnp.zeros((sr, D_CONV), jnp.float32)
            for k in range(CONV_W):
                acc = acc + uext[pl.ds(_TAIL_U + r - k, sr), :] * cdw_ref[CONV_W - 1 - k:CONV_W - k, :]
            cv[r:r + sr, :] = acc

        cnt = _pool_count(blk, rb)
        ypb = (ypre[...] / cnt - xp).astype(_MM)
        dyp = dy_ref[:, 0:256]
        pm = _dot(ypb, wbd_ref[...])
        dsm_ref[1:2, 0:256] += jnp.sum(dyp * pm, axis=0, keepdims=True)
        dpm = (dyp * ps_ref[...]).astype(_MM)
        dwbd_ref[...] += _dot_tn(ypb, dpm)
        dypre = _dot_nt(dpm, wbd_ref[...])
        eext[0:rb, :] = dypre / cnt
        for r in range(0, rb, sr):
            ypre[r:r + sr, :] = _window_sums(eext, r, sr, 1)
        dz_ref[:, 0:256] = jnp.where(live, ypre[...] - dypre, 0.0).astype(_MM)

        cn, rstd_c = _ln_fwd(cv[...] + cvec_ref[0:1, :])
        ln = cn * cvec_ref[1:2, :] + cvec_ref[2:3, :]
        sg_l = _sigmoid(ln)
        swb = (ln * sg_l).astype(_MM)
        dycb = dy_ref[:, 256:512].astype(_MM)
        dwpw_ref[...] += _dot_tn(swb, dycb)
        dln = _dot_nt(dycb, wpw_ref[...]) * (sg_l * (1.0 + ln * (1.0 - sg_l)))
        dsm_ref[3:4, 0:256] += jnp.sum(dln * cn, axis=0, keepdims=True)
        dsm_ref[4:5, 0:256] += jnp.sum(dln, axis=0, keepdims=True)
        dcv = _ln_bwd(dln * cvec_ref[1:2, :], cn, rstd_c)
        dsm_ref[2:3, 0:256] += jnp.sum(dcv, axis=0, keepdims=True)
        dcvext[0:rb, :] = dcv
        for k in range(CONV_W):
            prod = dcv * uext[pl.ds(_TAIL_U - k, rb), :]
            dcdw_ref[CONV_W - 1 - k:CONV_W - k, :] += jnp.sum(prod, axis=0, keepdims=True)
        for r in range(0, rb, sr):
            acc = jnp.zeros((sr, D_CONV), jnp.float32)
            for k in range(CONV_W):
                acc = acc + dcvext[pl.ds(r + k, sr), :] * cdw_ref[CONV_W - 1 - k:CONV_W - k, :]
            cv[r:r + sr, :] = acc
        du = cv[...]
        dz_ref[:, 256:512] = jnp.where(live, du * sg_c, 0.0).astype(_MM)
        dz_ref[:, 512:768] = jnp.where(live, du * ca * sg_c * (1.0 - sg_c), 0.0).astype(_MM)

        csv = cs_ref[...]
        snv = sn_ref[...]
        for h in range(HEADS):
            q = _rope(z_ref[:, 768 + h * DH:768 + (h + 1) * DH], csv, snv)
            k = _rope(z_ref[:, 1280 + h * DH:1280 + (h + 1) * DH], csv, snv) * (DH ** -0.5)
            vb = z_ref[:, 1792 + h * DH:1792 + (h + 1) * DH].astype(_MM)
            g = z_ref[:, 2304 + h * DH:2304 + (h + 1) * DH]
            a, b = _row_decays(rb, h)
            qb = q.astype(_MM)
            kb = k.astype(_MM)
            qab = (q * a).astype(_MM)
            kbb = (k * b).astype(_MM)
            stb = st_ref[0, h].astype(_MM)
            sb = (_dot_nt(qb, kb) * wmask[h]).astype(_MM)
            o = _dot(sb, vb) + _dot(qab, stb)
            on, rstd_o = _ln_fwd(o)
            gnv = gn_ref[:, h * DH:(h + 1) * DH]
            sg_g = _sigmoid(g)
            si_g = g * sg_g
            dyr = dy_ref[:, 512 + h * DH:512 + (h + 1) * DH]
            dsm_ref[0:1, h * DH:(h + 1) * DH] += jnp.sum(dyr * on * si_g, axis=0, keepdims=True)
            dgate = dyr * on * gnv * (sg_g * (1.0 + g * (1.0 - sg_g)))
            dob = _ln_bwd(dyr * gnv * si_g, on, rstd_o).astype(_MM)
            dstb = dstate[h].astype(_MM)
            dsb = (_dot_nt(dob, vb) * wmask[h]).astype(_MM)
            dq = _dot(dsb, kb) + _dot_nt(dob, stb) * a
            dk = _dot_tn(dsb, qb) + _dot_nt(vb, dstb) * b
            dv = _dot_tn(sb, dob) + _dot(kbb, dstb)
            dstate[h] = math.exp(LOG_GAMMA[h] * rb) * dstate[h] + _dot_tn(qab, dob)
            dz_ref[:, 768 + h * DH:768 + (h + 1) * DH] = jnp.where(live, _rope_t(dq, csv, snv), 0.0).astype(_MM)
            dz_ref[:, 1280 + h * DH:1280 + (h + 1) * DH] = jnp.where(
                live, _rope_t(dk * (DH ** -0.5), csv, snv), 0.0).astype(_MM)
            dz_ref[:, 1792 + h * DH:1792 + (h + 1) * DH] = jnp.where(live, dv, 0.0).astype(_MM)
            dz_ref[:, 2304 + h * DH:2304 + (h + 1) * DH] = jnp.where(live, dgate, 0.0).astype(_MM)

        dcvext[rb:rb + _TAIL_U, :] = dcvext[0:_TAIL_U, :]
        eext[rb:rb + _TAIL_X, :] = eext[0:_TAIL_X, :]

    rrows = lambda n: pl.BlockSpec((rb, n), lambda i: (rev(i), 0))
    return pl.pallas_call(
        body, name=name, grid=(nblk,),
        in_specs=[rrows(D_IN), rrows(D), rrows(DH), rrows(DH),
                  pl.BlockSpec((1, HEADS, DH, DH), lambda i: (rev(i), 0, 0, 0)),
                  pl.BlockSpec((1, _TAIL_U, D_CONV), lambda i: (rev(i), 0, 0)),
                  pl.BlockSpec((1, _TAIL_X, D_POOL), lambda i: (rev(i), 0, 0)),
                  _full((256, 256)), _full((1, 256)), _full((32, 256)), _full((8, 256)), _full((256, 256)),
                  _full((1, D_RET))],
        out_specs=[rrows(D_IN), _acc((256, 256)), _acc((256, 256)), _acc((32, 256)), _acc((8, 512))],
        out_shape=[jax.ShapeDtypeStruct((t, D_IN), _MM),
                   jax.ShapeDtypeStruct((256, 256), jnp.float32), jax.ShapeDtypeStruct((256, 256), jnp.float32),
                   jax.ShapeDtypeStruct((32, 256), jnp.float32), jax.ShapeDtypeStruct((8, 512), jnp.float32)],
        scratch_shapes=[pltpu.VMEM((rb + _TAIL_U, D_CONV), jnp.float32),
                        pltpu.VMEM((rb + _TAIL_X, D_POOL), jnp.float32),
                        pltpu.VMEM((rb, D_CONV), jnp.float32),
                        pltpu.VMEM((rb, D_POOL), jnp.float32),
                        pltpu.VMEM((rb + _TAIL_U, D_CONV), jnp.float32),
                        pltpu.VMEM((rb + _TAIL_X, D_POOL), jnp.float32),
                        pltpu.VMEM((HEADS, DH, DH), jnp.float32),
                        pltpu.VMEM((HEADS, rb, rb), jnp.float32)],
        compiler_params=_params(),
    )(z, dyc, cs, sn, st_in, ut_in, xt_in, wbd, pscale, cdw, cvec, wpw, gn)


def _me():
    return lax.axis_index("x"), lax.axis_index("y"), lax.axis_index("c")


def _flip(me, mask):
    return tuple(1 - m if f else m for m, f in zip(me, mask))


def _push(name, aliased, inputs, fresh, remote):
    n_al, n_in, n_out, n_rem = len(aliased), len(inputs), len(fresh), len(remote)

    def body(*refs):
        ins = refs[n_al:n_al + n_in]
        al = refs[n_al + n_in:2 * n_al + n_in]
        outs = refs[2 * n_al + n_in:2 * n_al + n_in + n_out]
        send_sems, recv_sems = refs[2 * n_al + n_in + n_out:]
        me = _me()
        copies = []
        for k, (mask, src_fn, dst_fn) in enumerate(remote):
            cp = pltpu.make_async_remote_copy(
                src_ref=src_fn(al, ins, outs, me), dst_ref=dst_fn(al, ins, outs, me),
                send_sem=send_sems.at[k], recv_sem=recv_sems.at[k],
                device_id=_flip(me, mask), device_id_type=MESH)
            cp.start()
            copies.append(cp)
        for cp in copies:
            cp.wait()

    return pl.pallas_call(
        body, name=name,
        in_specs=[_ANY] * (n_al + n_in), out_specs=[_ANY] * (n_al + n_out),
        out_shape=[jax.ShapeDtypeStruct(a.shape, a.dtype) for a in aliased] + list(fresh),
        input_output_aliases={i: i for i in range(n_al)},
        scratch_shapes=[pltpu.SemaphoreType.DMA((n_rem,)), pltpu.SemaphoreType.DMA((n_rem,))],
    )(*aliased, *inputs)


_ICI_MASKS = ((0, 1, 0), (1, 0, 0), (1, 1, 0))
_D2D_MASK = (0, 0, 1)
_ALL_MASKS = tuple((a, b, c) for a in (0, 1) for b in (0, 1) for c in (0, 1))[1:]


def _chip(me):
    return 2 * me[0] + me[1]


def _gather_weights(wb, small):
    r = wb.shape[1]
    s = small.shape[0]
    chip = _chip(_me())
    part = lax.dynamic_update_slice(lax.empty((2, N_SHARD, r, D), wb.dtype), wb[:, None], (0, chip, 0, 0))
    small_all = lax.dynamic_update_slice(lax.empty((N_SHARD, s, D), small.dtype), small[None], (chip, 0, 0))
    remote = []
    for mask in _ICI_MASKS:
        mine = lambda al, ins, outs, me: al[0].at[me[2], _chip(me)]
        remote.append((mask, mine, mine))
        mine_small = lambda al, ins, outs, me: al[1].at[_chip(me)]
        remote.append((mask, mine_small, mine_small))
    part, small_all = _push("gather_ici", [part, small_all], [], [], remote)
    remote = []
    for j in range(1, N_SHARD):
        theirs = lambda al, ins, outs, me, j=j: al[0].at[me[2], (_chip(me) + j) % N_SHARD]
        remote.append((_D2D_MASK, theirs, theirs))
    (full,) = _push("gather_d2d", [part], [], [], remote)
    return full, small_all


def _sum_rows_block(r):
    return _row_block(r) if r % 128 == 0 else r // 11 if r % 11 == 0 else r


def _sum_pair(name, g, recv):
    _, _, r, _ = g.shape
    rb = _sum_rows_block(r)
    c = lax.axis_index("c").astype(jnp.int32).reshape(1)

    def body(c_ref, g_ref, r_ref, o_ref):
        o_ref[...] = (g_ref[...].astype(jnp.float32) + r_ref[...].astype(jnp.float32)).astype(o_ref.dtype)

    return pl.pallas_call(
        body, name=name,
        grid_spec=pltpu.PrefetchScalarGridSpec(
            num_scalar_prefetch=1, grid=(N_SHARD, r // rb),
            in_specs=[pl.BlockSpec((None, None, rb, D), lambda s, i, c_ref: (c_ref[0], s, i, 0)),
                      pl.BlockSpec((None, rb, D), lambda s, i, c_ref: (s, i, 0))],
            out_specs=pl.BlockSpec((None, rb, D), lambda s, i, c_ref: (s, i, 0))),
        out_shape=jax.ShapeDtypeStruct((N_SHARD, r, D), g.dtype),
        compiler_params=_params(("arbitrary", "arbitrary")),
    )(c, g, recv)


def _sum_chips(name, p, recv):
    _, r, _ = p.shape
    rb = _sum_rows_block(r)
    s = jnp.stack([2 * lax.axis_index("x") + lax.axis_index("y"), lax.axis_index("c")]).astype(jnp.int32)

    def body(s_ref, p_ref, r_ref, o_ref):
        acc = p_ref[...].astype(jnp.float32)
        for j in range(3):
            acc = acc + r_ref[j].astype(jnp.float32)
        o_ref[...] = acc

    return pl.pallas_call(
        body, name=name,
        grid_spec=pltpu.PrefetchScalarGridSpec(
            num_scalar_prefetch=1, grid=(r // rb,),
            in_specs=[pl.BlockSpec((None, rb, D), lambda i, s_ref: (s_ref[0], i, 0)),
                      pl.BlockSpec((3, rb, D), lambda i, s_ref: (0, i, 0))],
            out_specs=pl.BlockSpec((None, rb, D), lambda i, s_ref: (s_ref[1], i, 0))),
        out_shape=jax.ShapeDtypeStruct((2, r, D), jnp.float32),
        compiler_params=_params(),
    )(s, p, recv)


def _reduce_scatter(g):
    _, _, r, _ = g.shape
    (recv,) = _push("rs_d2d", [], [g], [jax.ShapeDtypeStruct((N_SHARD, r, D), g.dtype)],
                    [(_D2D_MASK, lambda al, ins, outs, me: ins[0].at[1 - me[2]], lambda al, ins, outs, me: outs[0])])
    p = _sum_pair("rs_sum_pair", g, recv)
    remote = []
    for j, mask in enumerate(_ICI_MASKS):
        remote.append((mask,
                       lambda al, ins, outs, me, mask=mask: ins[0].at[_chip(_flip(me, mask))],
                       lambda al, ins, outs, me, j=j: outs[0].at[j]))
    (recv3,) = _push("rs_ici", [], [p], [jax.ShapeDtypeStruct((3, r, D), g.dtype)], remote)
    mine = _sum_chips("rs_sum_chips", p, recv3)
    half = lambda al, ins, outs, me: al[0].at[me[2]]
    (both,) = _push("rs_share", [mine], [], [], [(_D2D_MASK, half, half)])
    return both


def _all_reduce_small(v):
    s = v.shape[0]
    me = _me()
    every = lax.dynamic_update_slice(lax.empty((8, s, D), jnp.float32), v[None], (4 * me[0] + 2 * me[1] + me[2], 0, 0))
    slot = lambda al, ins, outs, me: al[0].at[4 * me[0] + 2 * me[1] + me[2]]
    (every,) = _push("small_all", [every], [], [], [(mask, slot, slot) for mask in _ALL_MASKS])

    def body(e_ref, o_ref):
        acc = e_ref[0]
        for j in range(1, 8):
            acc = acc + e_ref[j]
        o_ref[...] = acc

    return pl.pallas_call(
        body, name="small_sum", grid=(1,),
        in_specs=[pl.BlockSpec((8, s, D), lambda i: (0, 0, 0))],
        out_specs=pl.BlockSpec((s, D), lambda i: (0, 0)),
        out_shape=jax.ShapeDtypeStruct((s, D), jnp.float32),
        compiler_params=_params(),
    )(every)


def _adamw(name, w, g, m, v):
    r, c = w.shape
    rb = next(b for b in (256, 344, 128, 64, 32, 16, 8, r) if r % b == 0)

    def body(w_ref, g_ref, m_ref, v_ref, d_ref, mo_ref, vo_ref):
        g = g_ref[...]
        m = ADAM_B1 * m_ref[...] + (1.0 - ADAM_B1) * g
        v = ADAM_B2 * v_ref[...] + (1.0 - ADAM_B2) * (g * g)
        m_hat = m / (1.0 - ADAM_B1 ** ADAM_STEP)
        v_hat = v / (1.0 - ADAM_B2 ** ADAM_STEP)
        d_ref[...] = -ADAM_LR * (m_hat / (jnp.sqrt(v_hat) + ADAM_EPS) + ADAM_WD * w_ref[...])
        mo_ref[...] = m
        vo_ref[...] = v

    spec = pl.BlockSpec((rb, c), lambda i: (i, 0))
    return pl.pallas_call(
        body, name=name, grid=(r // rb,),
        in_specs=[spec] * 4, out_specs=[spec] * 3,
        out_shape=[jax.ShapeDtypeStruct((r, c), jnp.float32)] * 3,
        compiler_params=_params(),
    )(w, g, m, v)


_BIG = ("ffn1_w13", "ffn2_w13", "ffn1_w2", "ffn2_w2", "w_in", "w_out", "conv_pw")
_BIG_SHARD = {"ffn1_w13": (D, 1376), "ffn2_w13": (D, 1376), "ffn1_w2": (688, D), "ffn2_w2": (688, D),
              "w_in": (D, 704), "w_out": (256, D), "conv_pw": (64, 256)}


def _pack_rows(parts):
    flat = jnp.concatenate([p.reshape(-1) for p in parts])
    pad = (-flat.shape[0]) % (8 * D)
    if pad:
        flat = jnp.concatenate([flat, jnp.zeros((pad,), flat.dtype)])
    return flat.reshape(-1, D)


def _unpack_rows(buf, shapes):
    flat = buf.reshape(-1)
    out, off = [], 0
    for shp in shapes:
        n = math.prod(shp)
        out.append(flat[off:off + n].reshape(shp))
        off += n
    return out


def _pack_big_half(parts):
    return jnp.concatenate([parts["ffn1_w13"].T, parts["ffn2_w13"].T, parts["ffn1_w2"], parts["ffn2_w2"],
                            parts["w_in"].T, parts["w_out"], parts["conv_pw"].reshape(PW_ROWS, D)], axis=0)


def _unpack_big_half(buf):
    return {"ffn1_w13": buf[OFF_W13[0]:OFF_W13[0] + W13_ROWS].T, "ffn2_w13": buf[OFF_W13[1]:OFF_W13[1] + W13_ROWS].T,
            "ffn1_w2": buf[OFF_W2[0]:OFF_W2[0] + W2_ROWS], "ffn2_w2": buf[OFF_W2[1]:OFF_W2[1] + W2_ROWS],
            "w_in": buf[OFF_WIN:OFF_WIN + WIN_ROWS].T, "w_out": buf[OFF_WOUT:OFF_WOUT + WOUT_ROWS],
            "conv_pw": buf[OFF_PW:OFF_PW + PW_ROWS].reshape(64, 256)}


def kernel(x, meta, ln_in_g, ln_in_b, ffn1_w13, ffn1_w2, w_in, pool_w, pool_scale, conv_dw, conv_db, conv_ln_g, conv_ln_b, conv_pw, ret_gn_g, w_out, ffn2_w13, ffn2_w2, ln_g, ln_b, loss_target, m_meta, m_ln_in_g, m_ln_in_b, m_ffn1_w13, m_ffn1_w2, m_w_in, m_pool_w, m_pool_scale, m_conv_dw, m_conv_db, m_conv_ln_g, m_conv_ln_b, m_conv_pw, m_ret_gn_g, m_w_out, m_ffn2_w13, m_ffn2_w2, m_ln_g, m_ln_b, v_meta, v_ln_in_g, v_ln_in_b, v_ffn1_w13, v_ffn1_w2, v_w_in, v_pool_w, v_pool_scale, v_conv_dw, v_conv_db, v_conv_ln_g, v_conv_ln_b, v_conv_pw, v_ret_gn_g, v_w_out, v_ffn2_w13, v_ffn2_w2, v_ln_g, v_ln_b):
    f32 = jnp.float32
    seq = x.shape[1]
    t = seq + ROW0
    me = _me()
    chip = _chip(me)
    big_w = {"ffn1_w13": ffn1_w13, "ffn2_w13": ffn2_w13, "ffn1_w2": ffn1_w2, "ffn2_w2": ffn2_w2,
             "w_in": w_in, "w_out": w_out, "conv_pw": conv_pw}

    wb = jnp.stack([_pack_big_half({n: big_w[n][l].astype(_WIRE) for n in _BIG}) for l in range(DEPTH)])
    small_shapes = [(N_META, 256), (DEPTH, CONV_W, 64), (DEPTH, 3, 256), (DEPTH, 3, 256)]
    small = _pack_rows([meta, conv_dw, ln_g, ln_b])
    wfull, small_all = _gather_weights(wb, small)

    sm = [_unpack_rows(small_all[s], small_shapes) for s in range(N_SHARD)]
    meta_f = jnp.concatenate([sm[s][0] for s in range(N_SHARD)], axis=1)
    cdw_f = jnp.concatenate([sm[s][1] for s in range(N_SHARD)], axis=2)
    lng_f = jnp.concatenate([sm[s][2] for s in range(N_SHARD)], axis=2)
    lnb_f = jnp.concatenate([sm[s][3] for s in range(N_SHARD)], axis=2)

    def mix_params(l):
        wbd = jnp.zeros((D_POOL, D_POOL), f32)
        for g in range(4):
            wbd = wbd.at[64 * g:64 * (g + 1), 64 * g:64 * (g + 1)].set(pool_w[l, g])
        cdw = jnp.pad(cdw_f[l], ((0, 1), (0, 0)))
        cvec = jnp.pad(jnp.stack([conv_db[l], conv_ln_g[l], conv_ln_b[l]]), ((0, 5), (0, 0)))
        wpw = wfull[l, :, OFF_PW:OFF_PW + PW_ROWS].reshape(D_CONV, D_CONV)
        return (wbd.astype(_MM), pool_scale[l][None], cdw, cvec, wpw, ret_gn_g[l][None])

    gb_of = lambda l, i: jnp.stack([lng_f[l, i], lnb_f[l, i]])
    gb_in = jnp.stack([ln_in_g, ln_in_b])

    pos = jnp.arange(t, dtype=f32) - PAD
    inv_freq = ROPE_BASE ** (-jnp.arange(0, DH, 2, dtype=f32) / DH)
    ang = pos[:, None] * inv_freq[None, :]
    cs = jnp.concatenate([jnp.cos(ang), jnp.cos(ang)], axis=1)
    sn = jnp.concatenate([-jnp.sin(ang), jnp.sin(ang)], axis=1)

    raw = jnp.concatenate([jnp.zeros((PAD, D), f32), meta_f, x[0]], axis=0)
    target = jnp.concatenate([jnp.zeros((ROW0, D), f32), loss_target[0]], axis=0)
    acts = []
    xh, rstd = _ln_in_fwd(raw)
    cur = (xh, rstd, gb_in)
    saved = []
    for l in range(DEPTH):
        a0 = cur
        xh1, r1 = _ffn_fwd(f"ffn1_fwd_{l}", a0[0], a0[2], wfull, l, 0)
        a1 = (xh1, r1, gb_of(l, 0))
        z = _mix_in_fwd(f"mix_in_fwd_{l}", a1[0], a1[2], wfull, l)
        mp = mix_params(l)
        ycat, st_in, ut_in, xt_in = _mix_core_fwd(f"mix_core_fwd_{l}", z, cs, sn, *mp)
        xh2, r2 = _mix_out_fwd(f"mix_out_fwd_{l}", a1[0], a1[2], ycat, wfull, l)
        a2 = (xh2, r2, gb_of(l, 1))
        xh3, r3 = _ffn_fwd(f"ffn2_fwd_{l}", a2[0], a2[2], wfull, l, 1)
        a3 = (xh3, r3, gb_of(l, 2))
        saved.append((a0, a1, a2, a3, z, ycat, st_in, ut_in, xt_in, mp))
        cur = a3

    dy, loss_part = _loss_fwd_bwd(cur[0], cur[2], target)
    loss = lax.psum(loss_part[0, 0], ("x", "y", "c"))

    gbuf = lax.empty((DEPTH, N_SHARD, SHARD_ROWS, D), _WIRE)
    g_ln_g = [[None] * 3 for _ in range(DEPTH)]
    g_ln_b = [[None] * 3 for _ in range(DEPTH)]
    g_small = [dict() for _ in range(DEPTH)]
    slot = lambda j: j

    def ffn_grads(gbuf, tag, l, f, hb, hid, dau, dffn):
        gbuf = _dw_into(f"dw13_{tag}_{l}", gbuf, l, dau, hb, FF_SLOT, [(0, W13_ROWS, slot, OFF_W13[f])])
        return _dw_into(f"dw2_{tag}_{l}", gbuf, l, hid, dffn, FF_SLOT,
                        [(0, W2_ROWS, lambda j: 2 * j, OFF_W2[f]), (W2_ROWS, W2_ROWS, lambda j: 2 * j + 1, OFF_W2[f])])

    for l in reversed(range(DEPTH)):
        a0, a1, a2, a3, z, ycat, st_in, ut_in, xt_in, mp = saved[l]
        dh, hb, hid, dau, dffn, dgb = _ffn_bwd(f"ffn2_bwd_{l}", dy, a3[0], a3[1], a3[2], a2[0], a2[2], wfull, l, 1)
        g_ln_g[l][2], g_ln_b[l][2] = dgb[0], dgb[1]
        gbuf = ffn_grads(gbuf, "ffn2", l, 1, hb, hid, dau, dffn)
        dh_res, dycat, dsb, dgb = _mix_out_bwd(f"mix_out_bwd_{l}", dh, a2[0], a2[1], a2[2], wfull, l)
        g_ln_g[l][1], g_ln_b[l][1] = dgb[0], dgb[1]
        gbuf = _dw_into(f"dw_out_{l}", gbuf, l, ycat, dsb, D,
                        [(WOUT_ROWS * s, WOUT_ROWS, lambda j, s=s: s, OFF_WOUT) for s in range(N_SHARD)])
        dz, dwbd, dwpw, dcdw, dsm = _mix_core_bwd(f"mix_core_bwd_{l}", z, dycat, cs, sn, st_in, ut_in, xt_in, *mp)
        gbuf = lax.dynamic_update_slice(gbuf, dwpw.astype(_WIRE).reshape(1, N_SHARD, PW_ROWS, D), (l, 0, OFF_PW, 0))
        g_small[l] = dict(
            pool_w=jnp.stack([dwbd[64 * g:64 * (g + 1), 64 * g:64 * (g + 1)] for g in range(4)]),
            pool_scale=dsm[1, :256], conv_db=dsm[2, :256], conv_ln_g=dsm[3, :256], conv_ln_b=dsm[4, :256],
            ret_gn_g=dsm[0], conv_dw=dcdw[:CONV_W])
        dh, hb = _mix_in_bwd(f"mix_in_bwd_{l}", dh_res, dz, a1[0], a1[2], wfull, l)
        gbuf = _dw_into(f"dw_in_{l}", gbuf, l, dz, hb, D_IN,
                        [(WIN_ROWS * s, WIN_ROWS, lambda j, s=s: s, OFF_WIN) for s in range(N_SHARD)])
        dh, hb, hid, dau, dffn, dgb = _ffn_bwd(f"ffn1_bwd_{l}", dh, a1[0], a1[1], a1[2], a0[0], a0[2], wfull, l, 0)
        g_ln_g[l][0], g_ln_b[l][0] = dgb[0], dgb[1]
        gbuf = ffn_grads(gbuf, "ffn1", l, 0, hb, hid, dau, dffn)
        dy = dh
    d_raw, dgb_in = _ln_in_bwd(dy, saved[0][0][0], saved[0][0][1], gb_in)
    grad_x = d_raw[ROW0:][None]

    gsum = _reduce_scatter(gbuf)
    g_big = [_unpack_big_half(gsum[l]) for l in range(DEPTH)]
    grads = {n: jnp.stack([g_big[l][n] for l in range(DEPTH)]) for n in _BIG}

    small_parts = [
        d_raw[PAD:ROW0],
        jnp.stack([g_small[l]["conv_dw"] for l in range(DEPTH)]),
        jnp.stack([jnp.stack(g_ln_g[l]) for l in range(DEPTH)]),
        jnp.stack([jnp.stack(g_ln_b[l]) for l in range(DEPTH)]),
        dgb_in[0], dgb_in[1],
        jnp.stack([g_small[l]["pool_w"] for l in range(DEPTH)]),
        jnp.stack([g_small[l]["pool_scale"] for l in range(DEPTH)]),
        jnp.stack([g_small[l]["conv_db"] for l in range(DEPTH)]),
        jnp.stack([g_small[l]["conv_ln_g"] for l in range(DEPTH)]),
        jnp.stack([g_small[l]["conv_ln_b"] for l in range(DEPTH)]),
        jnp.stack([g_small[l]["ret_gn_g"] for l in range(DEPTH)]),
    ]
    red = _unpack_rows(_all_reduce_small(_pack_rows(small_parts)), [p.shape for p in small_parts])
    grads["meta"] = lax.dynamic_slice_in_dim(red[0], 256 * chip, 256, axis=1)
    grads["conv_dw"] = lax.dynamic_slice_in_dim(red[1], 64 * chip, 64, axis=2)
    grads["ln_g"] = lax.dynamic_slice_in_dim(red[2], 256 * chip, 256, axis=2)
    grads["ln_b"] = lax.dynamic_slice_in_dim(red[3], 256 * chip, 256, axis=2)
    for n, v in zip(("ln_in_g", "ln_in_b", "pool_w", "pool_scale", "conv_db", "conv_ln_g", "conv_ln_b", "ret_gn_g"),
                    red[4:]):
        grads[n] = v

    names = ['meta', 'ln_in_g', 'ln_in_b', 'ffn1_w13', 'ffn1_w2', 'w_in', 'pool_w', 'pool_scale', 'conv_dw',
             'conv_db', 'conv_ln_g', 'conv_ln_b', 'conv_pw', 'ret_gn_g', 'w_out', 'ffn2_w13', 'ffn2_w2', 'ln_g', 'ln_b']
    ws = dict(meta=meta, ln_in_g=ln_in_g, ln_in_b=ln_in_b, ffn1_w13=ffn1_w13, ffn1_w2=ffn1_w2, w_in=w_in,
              pool_w=pool_w, pool_scale=pool_scale, conv_dw=conv_dw, conv_db=conv_db, conv_ln_g=conv_ln_g,
              conv_ln_b=conv_ln_b, conv_pw=conv_pw, ret_gn_g=ret_gn_g, w_out=w_out, ffn2_w13=ffn2_w13,
              ffn2_w2=ffn2_w2, ln_g=ln_g, ln_b=ln_b)
    ms = dict(meta=m_meta, ln_in_g=m_ln_in_g, ln_in_b=m_ln_in_b, ffn1_w13=m_ffn1_w13, ffn1_w2=m_ffn1_w2,
              w_in=m_w_in, pool_w=m_pool_w, pool_scale=m_pool_scale, conv_dw=m_conv_dw, conv_db=m_conv_db,
              conv_ln_g=m_conv_ln_g, conv_ln_b=m_conv_ln_b, conv_pw=m_conv_pw, ret_gn_g=m_ret_gn_g,
              w_out=m_w_out, ffn2_w13=m_ffn2_w13, ffn2_w2=m_ffn2_w2, ln_g=m_ln_g, ln_b=m_ln_b)
    vs = dict(meta=v_meta, ln_in_g=v_ln_in_g, ln_in_b=v_ln_in_b, ffn1_w13=v_ffn1_w13, ffn1_w2=v_ffn1_w2,
              w_in=v_w_in, pool_w=v_pool_w, pool_scale=v_pool_scale, conv_dw=v_conv_dw, conv_db=v_conv_db,
              conv_ln_g=v_conv_ln_g, conv_ln_b=v_conv_ln_b, conv_pw=v_conv_pw, ret_gn_g=v_ret_gn_g,
              w_out=v_w_out, ffn2_w13=v_ffn2_w13, ffn2_w2=v_ffn2_w2, ln_g=v_ln_g, ln_b=v_ln_b)
    delta, new_m, new_v = {}, {}, {}
    for n in _BIG:
        shp = ws[n].shape
        two = lambda a: a.reshape(-1, shp[-1])
        d_, m_, v_ = _adamw("adamw_" + n, two(ws[n]), two(grads[n]), two(ms[n]), two(vs[n]))
        delta[n], new_m[n], new_v[n] = d_.reshape(shp), m_.reshape(shp), v_.reshape(shp)
    small_names = [n for n in names if n not in _BIG]
    pk = lambda d: _pack_rows([d[n] for n in small_names])
    d_, m_, v_ = _adamw("adamw_small", pk(ws), pk(grads), pk(ms), pk(vs))
    shapes = [ws[n].shape for n in small_names]
    for n, a, b, c in zip(small_names, _unpack_rows(d_, shapes), _unpack_rows(m_, shapes), _unpack_rows(v_, shapes)):
        delta[n], new_m[n], new_v[n] = a, b, c

    return (loss, grad_x, *[grads[n] for n in names], *[delta[n] for n in names],
            *[new_m[n] for n in names], *[new_v[n] for n in names])
```

```python
import functools
import math

import jax
import jax.numpy as jnp
from jax import lax
from jax.experimental import pallas as pl
from jax.experimental.pallas import tpu as pltpu

D = 1024
DEPTH = 2
N_META = 16
PAD = 112
ROW0 = PAD + N_META
D_POOL = 256
D_CONV = 256
D_RET = 512
HEADS = 4
DH = 128
CONV_W = 31
D_FF = 2752
FF_SLOT = 1408
D_FFP = 2 * FF_SLOT
D_IN = 2816
N_SHARD = 4
ALPHA = (2.0 * DEPTH) ** 0.25
LN_EPS = 1e-5
ROPE_BASE = 10000.0
LOG_GAMMA = tuple(math.log(1.0 - 2.0 ** (-5.0 - h)) for h in range(HEADS))
ADAM_LR, ADAM_B1, ADAM_B2, ADAM_EPS, ADAM_WD, ADAM_STEP = 0.001, 0.9, 0.999, 1e-08, 0.01, 10

_MM = jnp.bfloat16
_WIRE = jnp.bfloat16
_VMEM_LIMIT = 56 * 1024 * 1024

MESH = pl.DeviceIdType.MESH
_ANY = pl.BlockSpec(memory_space=pl.ANY)

W13_ROWS = 1376
W2_ROWS = 688
WIN_ROWS = 704
WOUT_ROWS = 256
PW_ROWS = 32
OFF_W13 = (0, W13_ROWS)
OFF_W2 = (2 * W13_ROWS, 2 * W13_ROWS + W2_ROWS)
OFF_WIN = 2 * W13_ROWS + 2 * W2_ROWS
OFF_WOUT = OFF_WIN + WIN_ROWS
OFF_PW = OFF_WOUT + WOUT_ROWS
SHARD_ROWS = OFF_PW + PW_ROWS
HALF_ROWS = SHARD_ROWS // 2
W2_SLOT_OFF = (0, W2_ROWS, FF_SLOT, FF_SLOT + W2_ROWS)


def _start_rows(w_ref, off, n, dst_of, sems, k0):
    cps = []
    for s in range(N_SHARD):
        cp = pltpu.make_async_copy(w_ref.at[s, pl.ds(off, n)], dst_of(s), sems.at[k0 + s])
        cp.start()
        cps.append(cp)
    return cps


def _load_ffn_weights(w_ref, f, w13, w2, sems):
    cps = _start_rows(w_ref, OFF_W13[f], W13_ROWS, lambda s: w13.at[s, pl.ds(0, W13_ROWS)], sems, 0)
    cps += _start_rows(w_ref, OFF_W2[f], W2_ROWS, lambda s: w2.at[pl.ds(W2_SLOT_OFF[s], W2_ROWS)], sems, 4)
    zpad = jnp.zeros((FF_SLOT - W13_ROWS, D), w13.dtype)
    for s in range(N_SHARD):
        w13[s, W13_ROWS:FF_SLOT, :] = zpad
    w2[W13_ROWS:FF_SLOT, :] = zpad
    w2[FF_SLOT + W13_ROWS:D_FFP, :] = zpad
    for cp in cps:
        cp.wait()


def _load_rows(w_ref, off, n, dst, sems):
    for cp in _start_rows(w_ref, off, n, lambda s: dst.at[pl.ds(s * n, n)], sems, 0):
        cp.wait()


def _dot(a, b):
    return jnp.dot(a, b, preferred_element_type=jnp.float32)


def _dot_nt(a, b):
    return lax.dot_general(a, b, (((1,), (1,)), ((), ())), preferred_element_type=jnp.float32)


def _dot_tn(a, b):
    return lax.dot_general(a, b, (((0,), (0,)), ((), ())), preferred_element_type=jnp.float32)


def _params(sem=("arbitrary",)):
    return pltpu.CompilerParams(dimension_semantics=sem, vmem_limit_bytes=_VMEM_LIMIT)


def _row_block(t, cap=640):
    for rb in (640, 320, 128):
        if rb <= cap and t % rb == 0 and (t > 1024 or rb == 128):
            return rb
    raise ValueError(t)


def _rows(rb, n):
    return pl.BlockSpec((rb, n), lambda i: (i, 0))


def _full(shape):
    nd = len(shape)
    return pl.BlockSpec(tuple(shape), lambda i: (0,) * nd, pipeline_mode=pl.Buffered(1))


def _acc(shape):
    nd = len(shape)
    return pl.BlockSpec(tuple(shape), lambda i: (0,) * nd)


def _sigmoid(x):
    return 1.0 / (1.0 + jnp.exp(-x))


def _ln_fwd(s):
    mu = jnp.mean(s, axis=-1, keepdims=True)
    xc = s - mu
    var = jnp.mean(xc * xc, axis=-1, keepdims=True)
    rstd = lax.rsqrt(var + LN_EPS)
    return xc * rstd, rstd


def _ln_bwd(dxh, xh, rstd):
    m1 = jnp.mean(dxh, axis=-1, keepdims=True)
    m2 = jnp.mean(dxh * xh, axis=-1, keepdims=True)
    return rstd * (dxh - m1 - xh * m2)


def _ln_in_fwd(raw):
    t = raw.shape[0]
    rb = _row_block(t)

    def body(raw_ref, xh_ref, rstd_ref):
        xh, rstd = _ln_fwd(raw_ref[...])
        xh_ref[...] = xh
        rstd_ref[...] = rstd

    return pl.pallas_call(
        body, name="ln_in_fwd", grid=(t // rb,),
        in_specs=[_rows(rb, D)],
        out_specs=[_rows(rb, D), _rows(rb, 1)],
        out_shape=[jax.ShapeDtypeStruct((t, D), jnp.float32), jax.ShapeDtypeStruct((t, 1), jnp.float32)],
        compiler_params=_params(),
    )(raw)


def _ffn_fwd(name, xh, gb, wfull, f):
    t = xh.shape[0]
    rb = _row_block(t)

    def body(xh_ref, gb_ref, w_ref, out_ref, rstd_ref, w13, w2, sems):
        @pl.when(pl.program_id(0) == 0)
        def _():
            _load_ffn_weights(w_ref, f, w13, w2, sems)

        h = xh_ref[...] * gb_ref[0:1, :] + gb_ref[1:2, :]
        hb = h.astype(_MM)
        acc = jnp.zeros((rb, D), jnp.float32)
        for j in range(2):
            a = _dot_nt(hb, w13[j])
            u = _dot_nt(hb, w13[2 + j])
            hid = (a * _sigmoid(a) * u).astype(_MM)
            acc = acc + _dot(hid, w2[j * FF_SLOT:(j + 1) * FF_SLOT, :])
        xo, rstd = _ln_fwd(ALPHA * h + 0.5 * acc)
        out_ref[...] = xo
        rstd_ref[...] = rstd

    return pl.pallas_call(
        body, name=name, grid=(t // rb,),
        in_specs=[_rows(rb, D), _full((2, D)), _ANY],
        out_specs=[_rows(rb, D), _rows(rb, 1)],
        out_shape=[jax.ShapeDtypeStruct((t, D), jnp.float32), jax.ShapeDtypeStruct((t, 1), jnp.float32)],
        scratch_shapes=[pltpu.VMEM((N_SHARD, FF_SLOT, D), _MM), pltpu.VMEM((D_FFP, D), _MM),
                        pltpu.SemaphoreType.DMA((8,))],
        compiler_params=_params(),
    )(xh, gb, wfull)


def _ffn_bwd(name, dy, xo, rstd, gb_out, xh, gb, wfull, f):
    t = xh.shape[0]
    rb = _row_block(t, 320)

    def body(dy_ref, xo_ref, rstd_ref, gbo_ref, xh_ref, gb_ref, w_ref,
             dh_ref, hb_ref, hid_ref, dau_ref, dffn_ref, dgb_ref, w13, w2, sems):
        i = pl.program_id(0)

        @pl.when(i == 0)
        def _():
            dgb_ref[...] = jnp.zeros_like(dgb_ref)
            _load_ffn_weights(w_ref, f, w13, w2, sems)

        dy = dy_ref[...]
        xo = xo_ref[...]
        dgb_ref[0:1, :] += jnp.sum(dy * xo, axis=0, keepdims=True)
        dgb_ref[1:2, :] += jnp.sum(dy, axis=0, keepdims=True)
        ds = _ln_bwd(dy * gbo_ref[0:1, :], xo, rstd_ref[...])
        dffn = (0.5 * ds).astype(_MM)
        dffn_ref[...] = dffn
        h = xh_ref[...] * gb_ref[0:1, :] + gb_ref[1:2, :]
        hb = h.astype(_MM)
        hb_ref[...] = hb
        dh = ALPHA * ds
        for j in range(2):
            lo = j * FF_SLOT
            a = _dot_nt(hb, w13[j])
            u = _dot_nt(hb, w13[2 + j])
            sg = _sigmoid(a)
            si = a * sg
            hid_ref[:, lo:lo + FF_SLOT] = (si * u).astype(_MM)
            dhid = _dot_nt(dffn, w2[lo:lo + FF_SLOT, :])
            da = (dhid * u * (sg * (1.0 + a * (1.0 - sg)))).astype(_MM)
            du = (dhid * si).astype(_MM)
            dau_ref[:, lo:lo + FF_SLOT] = da
            dau_ref[:, D_FFP + lo:D_FFP + lo + FF_SLOT] = du
            dh = dh + _dot(da, w13[j]) + _dot(du, w13[2 + j])
        dh_ref[...] = dh

    return pl.pallas_call(
        body, name=name, grid=(t // rb,),
        in_specs=[_rows(rb, D), _rows(rb, D), _rows(rb, 1), _full((2, D)), _rows(rb, D), _full((2, D)), _ANY],
        out_specs=[_rows(rb, D), _rows(rb, D), _rows(rb, D_FFP), _rows(rb, 2 * D_FFP), _rows(rb, D), _acc((8, D))],
        out_shape=[jax.ShapeDtypeStruct((t, D), jnp.float32), jax.ShapeDtypeStruct((t, D), _MM),
                   jax.ShapeDtypeStruct((t, D_FFP), _MM), jax.ShapeDtypeStruct((t, 2 * D_FFP), _MM),
                   jax.ShapeDtypeStruct((t, D), _MM), jax.ShapeDtypeStruct((8, D), jnp.float32)],
        scratch_shapes=[pltpu.VMEM((N_SHARD, FF_SLOT, D), _MM), pltpu.VMEM((D_FFP, D), _MM),
                        pltpu.SemaphoreType.DMA((8,))],
        compiler_params=_params(),
    )(dy, xo, rstd, gb_out, xh, gb, wfull)


def _mix_in_fwd(name, xh, gb, wfull):
    t = xh.shape[0]
    rb = _row_block(t)

    def body(xh_ref, gb_ref, w_ref, z_ref, wt, sems):
        @pl.when(pl.program_id(0) == 0)
        def _():
            _load_rows(w_ref, OFF_WIN, WIN_ROWS, wt, sems)

        h = xh_ref[...] * gb_ref[0:1, :] + gb_ref[1:2, :]
        z = _dot_nt(h.astype(_MM), wt[...])
        row = pl.program_id(0) * rb + lax.broadcasted_iota(jnp.int32, (rb, 1), 0)
        z_ref[...] = jnp.where(row >= PAD, z, 0.0)

    return pl.pallas_call(
        body, name=name, grid=(t // rb,),
        in_specs=[_rows(rb, D), _full((2, D)), _ANY],
        out_specs=_rows(rb, D_IN),
        out_shape=jax.ShapeDtypeStruct((t, D_IN), jnp.float32),
        scratch_shapes=[pltpu.VMEM((D_IN, D), _MM), pltpu.SemaphoreType.DMA((4,))],
        compiler_params=_params(),
    )(xh, gb, wfull)


def _mix_in_bwd(name, dh_res, dz, xh, gb, wfull):
    t = xh.shape[0]
    rb = _row_block(t)

    def body(dhr_ref, dz_ref, xh_ref, gb_ref, w_ref, dh_ref, hb_ref, wt, sems):
        @pl.when(pl.program_id(0) == 0)
        def _():
            _load_rows(w_ref, OFF_WIN, WIN_ROWS, wt, sems)

        dh_ref[...] = dhr_ref[...] + _dot(dz_ref[...], wt[...])
        hb_ref[...] = (xh_ref[...] * gb_ref[0:1, :] + gb_ref[1:2, :]).astype(_MM)

    return pl.pallas_call(
        body, name=name, grid=(t // rb,),
        in_specs=[_rows(rb, D), _rows(rb, D_IN), _rows(rb, D), _full((2, D)), _ANY],
        out_specs=[_rows(rb, D), _rows(rb, D)],
        out_shape=[jax.ShapeDtypeStruct((t, D), jnp.float32), jax.ShapeDtypeStruct((t, D), _MM)],
        scratch_shapes=[pltpu.VMEM((D_IN, D), _MM), pltpu.SemaphoreType.DMA((4,))],
        compiler_params=_params(),
    )(dh_res, dz, xh, gb, wfull)


def _mix_out_fwd(name, xh, gb, ycat, wfull):
    t = xh.shape[0]
    rb = _row_block(t)

    def body(xh_ref, gb_ref, y_ref, w_ref, out_ref, rstd_ref, wo, sems):
        @pl.when(pl.program_id(0) == 0)
        def _():
            _load_rows(w_ref, OFF_WOUT, WOUT_ROWS, wo, sems)

        h = xh_ref[...] * gb_ref[0:1, :] + gb_ref[1:2, :]
        xo, rstd = _ln_fwd(ALPHA * h + _dot(y_ref[...], wo[...]))
        out_ref[...] = xo
        rstd_ref[...] = rstd

    return pl.pallas_call(
        body, name=name, grid=(t // rb,),
        in_specs=[_rows(rb, D), _full((2, D)), _rows(rb, D), _ANY],
        out_specs=[_rows(rb, D), _rows(rb, 1)],
        out_shape=[jax.ShapeDtypeStruct((t, D), jnp.float32), jax.ShapeDtypeStruct((t, 1), jnp.float32)],
        scratch_shapes=[pltpu.VMEM((D, D), _MM), pltpu.SemaphoreType.DMA((4,))],
        compiler_params=_params(),
    )(xh, gb, ycat, wfull)


def _mix_out_bwd(name, dy, xo, rstd, gb_out, wfull):
    t = xo.shape[0]
    rb = _row_block(t)

    def body(dy_ref, xo_ref, rstd_ref, gbo_ref, w_ref, dhr_ref, dyc_ref, dsb_ref, dgb_ref, wo, sems):
        @pl.when(pl.program_id(0) == 0)
        def _():
            dgb_ref[...] = jnp.zeros_like(dgb_ref)
            _load_rows(w_ref, OFF_WOUT, WOUT_ROWS, wo, sems)

        dy = dy_ref[...]
        xo = xo_ref[...]
        dgb_ref[0:1, :] += jnp.sum(dy * xo, axis=0, keepdims=True)
        dgb_ref[1:2, :] += jnp.sum(dy, axis=0, keepdims=True)
        ds = _ln_bwd(dy * gbo_ref[0:1, :], xo, rstd_ref[...])
        dsb = ds.astype(_MM)
        dsb_ref[...] = dsb
        dhr_ref[...] = ALPHA * ds
        dyc_ref[...] = _dot_nt(dsb, wo[...])

    return pl.pallas_call(
        body, name=name, grid=(t // rb,),
        in_specs=[_rows(rb, D), _rows(rb, D), _rows(rb, 1), _full((2, D)), _ANY],
        out_specs=[_rows(rb, D), _rows(rb, D), _rows(rb, D), _acc((8, D))],
        out_shape=[jax.ShapeDtypeStruct((t, D), jnp.float32), jax.ShapeDtypeStruct((t, D), jnp.float32),
                   jax.ShapeDtypeStruct((t, D), _MM), jax.ShapeDtypeStruct((8, D), jnp.float32)],
        scratch_shapes=[pltpu.VMEM((D, D), _MM), pltpu.SemaphoreType.DMA((4,))],
        compiler_params=_params(),
    )(dy, xo, rstd, gb_out, wfull)


def _loss_fwd_bwd(xh, gb, target):
    t = xh.shape[0]
    rb = _row_block(t)

    def body(xh_ref, gb_ref, tg_ref, dy_ref, loss_ref):
        @pl.when(pl.program_id(0) == 0)
        def _():
            loss_ref[...] = jnp.zeros_like(loss_ref)

        y = xh_ref[...] * gb_ref[0:1, :] + gb_ref[1:2, :]
        row = pl.program_id(0) * rb + lax.broadcasted_iota(jnp.int32, (rb, 1), 0)
        err = jnp.where(row >= ROW0, y - tg_ref[...], 0.0)
        dy_ref[...] = err * (1.0 / D)
        per_row = jnp.mean(err * err, axis=-1, keepdims=True)
        loss_ref[...] += 0.5 * jnp.sum(per_row, axis=0, keepdims=True)

    return pl.pallas_call(
        body, name="loss", grid=(t // rb,),
        in_specs=[_rows(rb, D), _full((2, D)), _rows(rb, D)],
        out_specs=[_rows(rb, D), _acc((1, 1))],
        out_shape=[jax.ShapeDtypeStruct((t, D), jnp.float32), jax.ShapeDtypeStruct((1, 1), jnp.float32)],
        compiler_params=_params(),
    )(xh, gb, target)


def _ln_in_bwd(dy, xh, rstd, gb):
    t = xh.shape[0]
    rb = _row_block(t)

    def body(dy_ref, xh_ref, rstd_ref, gb_ref, dx_ref, dgb_ref):
        @pl.when(pl.program_id(0) == 0)
        def _():
            dgb_ref[...] = jnp.zeros_like(dgb_ref)

        dy = dy_ref[...]
        xh = xh_ref[...]
        dgb_ref[0:1, :] += jnp.sum(dy * xh, axis=0, keepdims=True)
        dgb_ref[1:2, :] += jnp.sum(dy, axis=0, keepdims=True)
        dx_ref[...] = _ln_bwd(dy * gb_ref[0:1, :], xh, rstd_ref[...])

    return pl.pallas_call(
        body, name="ln_in_bwd", grid=(t // rb,),
        in_specs=[_rows(rb, D), _rows(rb, D), _rows(rb, 1), _full((2, D))],
        out_specs=[_rows(rb, D), _acc((8, D))],
        out_shape=[jax.ShapeDtypeStruct((t, D), jnp.float32), jax.ShapeDtypeStruct((8, D), jnp.float32)],
        compiler_params=_params(),
    )(dy, xh, rstd, gb)


def _dw_into(name, gpack, x, y, cols, pieces):
    t, k = x.shape
    tt = _row_block(t)
    nt = t // tt

    def body(x_ref, y_ref, g_in, g_out, acc, stage, sems):
        j = pl.program_id(0)
        s = pl.program_id(1)

        @pl.when(s == 0)
        def _():
            acc[...] = jnp.zeros_like(acc)

        acc[...] += _dot_tn(x_ref[...], y_ref[...])

        @pl.when(s == nt - 1)
        def _():
            stage[...] = acc[...].astype(stage.dtype)
            cps = []
            for q, (lo, n, chip_of, off) in enumerate(pieces):
                cp = pltpu.make_async_copy(stage.at[pl.ds(lo, n)], g_out.at[chip_of(j), pl.ds(off, n)],
                                           sems.at[q])
                cp.start()
                cps.append(cp)
            for cp in cps:
                cp.wait()

    return pl.pallas_call(
        body, name=name, grid=(k // cols, nt),
        in_specs=[pl.BlockSpec((tt, cols), lambda j, s: (s, j)), pl.BlockSpec((tt, D), lambda j, s: (s, 0)), _ANY],
        out_specs=_ANY,
        out_shape=jax.ShapeDtypeStruct(gpack.shape, gpack.dtype),
        input_output_aliases={2: 0},
        scratch_shapes=[pltpu.VMEM((cols, D), jnp.float32), pltpu.VMEM((cols, D), gpack.dtype),
                        pltpu.SemaphoreType.DMA((len(pieces),))],
        compiler_params=_params(("arbitrary", "arbitrary")),
    )(x, y, gpack)


_TAIL_U = 32
_TAIL_X = 16
_MIX_ROWS = 320


def _decay_mask(rb, h):
    ii = lax.broadcasted_iota(jnp.int32, (rb, rb), 0)
    jj = lax.broadcasted_iota(jnp.int32, (rb, rb), 1)
    dist = jnp.abs(ii - jj).astype(jnp.float32)
    vis = (jj >> 6) <= (ii >> 6)
    return jnp.where(vis, jnp.exp(LOG_GAMMA[h] * dist), 0.0)


def _row_decays(rb, h):
    r = lax.broadcasted_iota(jnp.int32, (rb, DH), 0).astype(jnp.float32)
    return jnp.exp(LOG_GAMMA[h] * (r + 1.0)), jnp.exp(LOG_GAMMA[h] * (rb - 1.0 - r))


def _rope(x, cs, sn):
    return x * cs + pltpu.roll(x, DH // 2, 1) * sn


def _rope_t(dx, cs, sn):
    return dx * cs + pltpu.roll(dx * sn, DH // 2, 1)


def _pool_count(blk, rb):
    row = blk * rb + lax.broadcasted_iota(jnp.int32, (rb, D_POOL), 0) - PAD
    lane = lax.broadcasted_iota(jnp.int32, (rb, D_POOL), 1)
    win = jnp.left_shift(2, lane >> 6)
    return jnp.clip(row + 1, 1, win).astype(jnp.float32)


def _pool_select(p2, p4, p8, p16):
    lane = lax.broadcasted_iota(jnp.int32, p2.shape, 1)
    return jnp.where(lane < 64, p2, jnp.where(lane < 128, p4, jnp.where(lane < 192, p8, p16)))


def _window_sums(ext_ref, base, rows, sign):
    acc = ext_ref[pl.ds(base, rows), :]
    outs = []
    for k in range(1, 16):
        acc = acc + ext_ref[pl.ds(base + sign * k, rows), :]
        if k in (1, 3, 7, 15):
            outs.append(acc)
    return _pool_select(*outs)


def _sub_rows(rb):
    return 128 if rb % 128 == 0 else 64


def _mix_core_fwd(name, z, cs, sn, wbd, pscale, cdw, cvec, wpw, gn):
    t = z.shape[0]
    rb = _row_block(t, _MIX_ROWS)
    nblk = t // rb
    sr = _sub_rows(rb)

    def body(z_ref, cs_ref, sn_ref, wbd_ref, ps_ref, cdw_ref, cvec_ref, wpw_ref, gn_ref,
             y_ref, st_ref, ut_ref, xt_ref,
             uext, xext, cv, ypre, state, wmask):
        i = pl.program_id(0)

        @pl.when(i == 0)
        def _():
            state[...] = jnp.zeros_like(state)
            uext[0:_TAIL_U, :] = jnp.zeros((_TAIL_U, D_CONV), jnp.float32)
            xext[0:_TAIL_X, :] = jnp.zeros((_TAIL_X, D_POOL), jnp.float32)
            for h in range(HEADS):
                wmask[h] = _decay_mask(rb, h)

        st_ref[0] = state[...]
        ut_ref[0] = uext[0:_TAIL_U, :]
        xt_ref[0] = xext[0:_TAIL_X, :]

        xp = z_ref[:, 0:256]
        uext[_TAIL_U:_TAIL_U + rb, :] = z_ref[:, 256:512] * _sigmoid(z_ref[:, 512:768])
        xext[_TAIL_X:_TAIL_X + rb, :] = xp

        for r in range(0, rb, sr):
            win = _window_sums(xext, _TAIL_X + r, sr, -1)
            ypre[r:r + sr, :] = win
            acc = jnp.zeros((sr, D_CONV), jnp.float32)
            for k in range(CONV_W):
                acc = acc + uext[pl.ds(_TAIL_U + r - k, sr), :] * cdw_ref[CONV_W - 1 - k:CONV_W - k, :]
            cv[r:r + sr, :] = acc

        yp = ypre[...] / _pool_count(i, rb) - xp
        y_ref[:, 0:256] = (_dot(yp.astype(_MM), wbd_ref[...]) * ps_ref[...]).astype(_MM)
        cn, _ = _ln_fwd(cv[...] + cvec_ref[0:1, :])
        ln = cn * cvec_ref[1:2, :] + cvec_ref[2:3, :]
        sw = ln * _sigmoid(ln)
        y_ref[:, 256:512] = _dot(sw.astype(_MM), wpw_ref[...]).astype(_MM)
        csv = cs_ref[...]
        snv = sn_ref[...]
        for h in range(HEADS):
            q = _rope(z_ref[:, 768 + h * DH:768 + (h + 1) * DH], csv, snv)
            k = _rope(z_ref[:, 1280 + h * DH:1280 + (h + 1) * DH], csv, snv) * (DH ** -0.5)
            vb = z_ref[:, 1792 + h * DH:1792 + (h + 1) * DH].astype(_MM)
            g = z_ref[:, 2304 + h * DH:2304 + (h + 1) * DH]
            a, b = _row_decays(rb, h)
            s = _dot_nt(q.astype(_MM), k.astype(_MM)) * wmask[h]
            o = _dot(s.astype(_MM), vb) + _dot((q * a).astype(_MM), state[h].astype(_MM))
            state[h] = math.exp(LOG_GAMMA[h] * rb) * state[h] + _dot_tn((k * b).astype(_MM), vb)
            on, _ = _ln_fwd(o)
            y_ref[:, 512 + h * DH:512 + (h + 1) * DH] = (
                g * _sigmoid(g) * on * gn_ref[:, h * DH:(h + 1) * DH]).astype(_MM)

        uext[0:_TAIL_U, :] = uext[rb:rb + _TAIL_U, :]
        xext[0:_TAIL_X, :] = xext[rb:rb + _TAIL_X, :]

    return pl.pallas_call(
        body, name=name, grid=(nblk,),
        in_specs=[_rows(rb, D_IN), _rows(rb, DH), _rows(rb, DH), _full((256, 256)), _full((1, 256)),
                  _full((32, 256)), _full((8, 256)), _full((256, 256)), _full((1, D_RET))],
        out_specs=[_rows(rb, D),
                   pl.BlockSpec((1, HEADS, DH, DH), lambda i: (i, 0, 0, 0)),
                   pl.BlockSpec((1, _TAIL_U, D_CONV), lambda i: (i, 0, 0)),
                   pl.BlockSpec((1, _TAIL_X, D_POOL), lambda i: (i, 0, 0))],
        out_shape=[jax.ShapeDtypeStruct((t, D), _MM),
                   jax.ShapeDtypeStruct((nblk, HEADS, DH, DH), jnp.float32),
                   jax.ShapeDtypeStruct((nblk, _TAIL_U, D_CONV), jnp.float32),
                   jax.ShapeDtypeStruct((nblk, _TAIL_X, D_POOL), jnp.float32)],
        scratch_shapes=[pltpu.VMEM((rb + _TAIL_U, D_CONV), jnp.float32),
                        pltpu.VMEM((rb + _TAIL_X, D_POOL), jnp.float32),
                        pltpu.VMEM((rb, D_CONV), jnp.float32),
                        pltpu.VMEM((rb, D_POOL), jnp.float32),
                        pltpu.VMEM((HEADS, DH, DH), jnp.float32),
                        pltpu.VMEM((HEADS, rb, rb), jnp.float32)],
        compiler_params=_params(),
    )(z, cs, sn, wbd, pscale, cdw, cvec, wpw, gn)


def _mix_core_bwd(name, z, dyc, cs, sn, st_in, ut_in, xt_in, wbd, pscale, cdw, cvec, wpw, gn):
    t = z.shape[0]
    rb = _row_block(t, _MIX_ROWS)
    nblk = t // rb
    sr = _sub_rows(rb)
    rev = lambda i: nblk - 1 - i

    def body(z_ref, dy_ref, cs_ref, sn_ref, st_ref, ut_ref, xt_ref,
             wbd_ref, ps_ref, cdw_ref, cvec_ref, wpw_ref, gn_ref,
             dz_ref, dwbd_ref, dwpw_ref, dcdw_ref, dsm_ref,
             uext, xext, cv, ypre, dcvext, eext, dstate, wmask):
        i = pl.program_id(0)
        blk = nblk - 1 - i

        @pl.when(i == 0)
        def _():
            dstate[...] = jnp.zeros_like(dstate)
            dcvext[rb:rb + _TAIL_U, :] = jnp.zeros((_TAIL_U, D_CONV), jnp.float32)
            eext[rb:rb + _TAIL_X, :] = jnp.zeros((_TAIL_X, D_POOL), jnp.float32)
            dwbd_ref[...] = jnp.zeros_like(dwbd_ref)
            dwpw_ref[...] = jnp.zeros_like(dwpw_ref)
            dcdw_ref[...] = jnp.zeros_like(dcdw_ref)
            dsm_ref[...] = jnp.zeros_like(dsm_ref)
            for h in range(HEADS):
                wmask[h] = _decay_mask(rb, h)

        row = blk * rb + lax.broadcasted_iota(jnp.int32, (rb, 1), 0)
        live = row >= PAD

        xp = z_ref[:, 0:256]
        ca = z_ref[:, 256:512]
        sg_c = _sigmoid(z_ref[:, 512:768])
        uext[0:_TAIL_U, :] = ut_ref[0]
        xext[0:_TAIL_X, :] = xt_ref[0]
        uext[_TAIL_U:_TAIL_U + rb, :] = ca * sg_c
        xext[_TAIL_X:_TAIL_X + rb, :] = xp
        for r in range(0, rb, sr):
            ypre[r:r + sr, :] = _window_sums(xext, _TAIL_X + r, sr, -1)
            acc = jnp.zeros((sr, D_CONV), jnp.float32)
            for k in range(CONV_W):
                acc = acc + uext[pl.ds(_TAIL_U + r - k, sr), :] * cdw_ref[CONV_W - 1 - k:CONV_W - k, :]
            cv[r:r + sr, :] = acc

        cnt = _pool_count(blk, rb)
        ypb = (ypre[...] / cnt - xp).astype(_MM)
        dyp = dy_ref[:, 0:256]
        pm = _dot(ypb, wbd_ref[...])
        dsm_ref[1:2, 0:256] += jnp.sum(dyp * pm, axis=0, keepdims=True)
        dpm = (dyp * ps_ref[...]).astype(_MM)
        dwbd_ref[...] += _dot_tn(ypb, dpm)
        dypre = _dot_nt(dpm, wbd_ref[...])
        eext[0:rb, :] = dypre / cnt
        for r in range(0, rb, sr):
            ypre[r:r + sr, :] = _window_sums(eext, r, sr, 1)
        dz_ref[:, 0:256] = jnp.where(live, ypre[...] - dypre, 0.0).astype(_MM)

        cn, rstd_c = _ln_fwd(cv[...] + cvec_ref[0:1, :])
        ln = cn * cvec_ref[1:2, :] + cvec_ref[2:3, :]
        sg_l = _sigmoid(ln)
        swb = (ln * sg_l).astype(_MM)
        dycb = dy_ref[:, 256:512].astype(_MM)
        dwpw_ref[...] += _dot_tn(swb, dycb)
        dln = _dot_nt(dycb, wpw_ref[...]) * (sg_l * (1.0 + ln * (1.0 - sg_l)))
        dsm_ref[3:4, 0:256] += jnp.sum(dln * cn, axis=0, keepdims=True)
        dsm_ref[4:5, 0:256] += jnp.sum(dln, axis=0, keepdims=True)
        dcv = _ln_bwd(dln * cvec_ref[1:2, :], cn, rstd_c)
        dsm_ref[2:3, 0:256] += jnp.sum(dcv, axis=0, keepdims=True)
        dcvext[0:rb, :] = dcv
        for k in range(CONV_W):
            prod = dcv * uext[pl.ds(_TAIL_U - k, rb), :]
            dcdw_ref[CONV_W - 1 - k:CONV_W - k, :] += jnp.sum(prod, axis=0, keepdims=True)
        for r in range(0, rb, sr):
            acc = jnp.zeros((sr, D_CONV), jnp.float32)
            for k in range(CONV_W):
                acc = acc + dcvext[pl.ds(r + k, sr), :] * cdw_ref[CONV_W - 1 - k:CONV_W - k, :]
            cv[r:r + sr, :] = acc
        du = cv[...]
        dz_ref[:, 256:512] = jnp.where(live, du * sg_c, 0.0).astype(_MM)
        dz_ref[:, 512:768] = jnp.where(live, du * ca * sg_c * (1.0 - sg_c), 0.0).astype(_MM)

        csv = cs_ref[...]
        snv = sn_ref[...]
        for h in range(HEADS):
            q = _rope(z_ref[:, 768 + h * DH:768 + (h + 1) * DH], csv, snv)
            k = _rope(z_ref[:, 1280 + h * DH:1280 + (h + 1) * DH], csv, snv) * (DH ** -0.5)
            vb = z_ref[:, 1792 + h * DH:1792 + (h + 1) * DH].astype(_MM)
            g = z_ref[:, 2304 + h * DH:2304 + (h + 1) * DH]
            a, b = _row_decays(rb, h)
            qb = q.astype(_MM)
            kb = k.astype(_MM)
            qab = (q * a).astype(_MM)
            kbb = (k * b).astype(_MM)
            stb = st_ref[0, h].astype(_MM)
            sb = (_dot_nt(qb, kb) * wmask[h]).astype(_MM)
            o = _dot(sb, vb) + _dot(qab, stb)
            on, rstd_o = _ln_fwd(o)
            gnv = gn_ref[:, h * DH:(h + 1) * DH]
            sg_g = _sigmoid(g)
            si_g = g * sg_g
            dyr = dy_ref[:, 512 + h * DH:512 + (h + 1) * DH]
            dsm_ref[0:1, h * DH:(h + 1) * DH] += jnp.sum(dyr * on * si_g, axis=0, keepdims=True)
            dgate = dyr * on * gnv * (sg_g * (1.0 + g * (1.0 - sg_g)))
            dob = _ln_bwd(dyr * gnv * si_g, on, rstd_o).astype(_MM)
            dstb = dstate[h].astype(_MM)
            dsb = (_dot_nt(dob, vb) * wmask[h]).astype(_MM)
            dq = _dot(dsb, kb) + _dot_nt(dob, stb) * a
            dk = _dot_tn(dsb, qb) + _dot_nt(vb, dstb) * b
            dv = _dot_tn(sb, dob) + _dot(kbb, dstb)
            dstate[h] = math.exp(LOG_GAMMA[h] * rb) * dstate[h] + _dot_tn(qab, dob)
            dz_ref[:, 768 + h * DH:768 + (h + 1) * DH] = jnp.where(live, _rope_t(dq, csv, snv), 0.0).astype(_MM)
            dz_ref[:, 1280 + h * DH:1280 + (h + 1) * DH] = jnp.where(
                live, _rope_t(dk * (DH ** -0.5), csv, snv), 0.0).astype(_MM)
            dz_ref[:, 1792 + h * DH:1792 + (h + 1) * DH] = jnp.where(live, dv, 0.0).astype(_MM)
            dz_ref[:, 2304 + h * DH:2304 + (h + 1) * DH] = jnp.where(live, dgate, 0.0).astype(_MM)

        dcvext[rb:rb + _TAIL_U, :] = dcvext[0:_TAIL_U, :]
        eext[rb:rb + _TAIL_X, :] = eext[0:_TAIL_X, :]

    rrows = lambda n: pl.BlockSpec((rb, n), lambda i: (rev(i), 0))
    return pl.pallas_call(
        body, name=name, grid=(nblk,),
        in_specs=[rrows(D_IN), rrows(D), rrows(DH), rrows(DH),
                  pl.BlockSpec((1, HEADS, DH, DH), lambda i: (rev(i), 0, 0, 0)),
                  pl.BlockSpec((1, _TAIL_U, D_CONV), lambda i: (rev(i), 0, 0)),
                  pl.BlockSpec((1, _TAIL_X, D_POOL), lambda i: (rev(i), 0, 0)),
                  _full((256, 256)), _full((1, 256)), _full((32, 256)), _full((8, 256)), _full((256, 256)),
                  _full((1, D_RET))],
        out_specs=[rrows(D_IN), _acc((256, 256)), _acc((256, 256)), _acc((32, 256)), _acc((8, 512))],
        out_shape=[jax.ShapeDtypeStruct((t, D_IN), _MM),
                   jax.ShapeDtypeStruct((256, 256), jnp.float32), jax.ShapeDtypeStruct((256, 256), jnp.float32),
                   jax.ShapeDtypeStruct((32, 256), jnp.float32), jax.ShapeDtypeStruct((8, 512), jnp.float32)],
        scratch_shapes=[pltpu.VMEM((rb + _TAIL_U, D_CONV), jnp.float32),
                        pltpu.VMEM((rb + _TAIL_X, D_POOL), jnp.float32),
                        pltpu.VMEM((rb, D_CONV), jnp.float32),
                        pltpu.VMEM((rb, D_POOL), jnp.float32),
                        pltpu.VMEM((rb + _TAIL_U, D_CONV), jnp.float32),
                        pltpu.VMEM((rb + _TAIL_X, D_POOL), jnp.float32),
                        pltpu.VMEM((HEADS, DH, DH), jnp.float32),
                        pltpu.VMEM((HEADS, rb, rb), jnp.float32)],
        compiler_params=_params(),
    )(z, dyc, cs, sn, st_in, ut_in, xt_in, wbd, pscale, cdw, cvec, wpw, gn)


def _me():
    return lax.axis_index("x"), lax.axis_index("y"), lax.axis_index("c")


def _flip(me, mask):
    return tuple(1 - m if f else m for m, f in zip(me, mask))


def _push(name, aliased, inputs, fresh, remote):
    n_al, n_in, n_out, n_rem = len(aliased), len(inputs), len(fresh), len(remote)

    def body(*refs):
        ins = refs[n_al:n_al + n_in]
        al = refs[n_al + n_in:2 * n_al + n_in]
        outs = refs[2 * n_al + n_in:2 * n_al + n_in + n_out]
        send_sems, recv_sems = refs[2 * n_al + n_in + n_out:]
        me = _me()
        copies = []
        for k, (mask, src_fn, dst_fn) in enumerate(remote):
            cp = pltpu.make_async_remote_copy(
                src_ref=src_fn(al, ins, outs, me), dst_ref=dst_fn(al, ins, outs, me),
                send_sem=send_sems.at[k], recv_sem=recv_sems.at[k],
                device_id=_flip(me, mask), device_id_type=MESH)
            cp.start()
            copies.append(cp)
        for cp in copies:
            cp.wait()

    return pl.pallas_call(
        body, name=name,
        in_specs=[_ANY] * (n_al + n_in), out_specs=[_ANY] * (n_al + n_out),
        out_shape=[jax.ShapeDtypeStruct(a.shape, a.dtype) for a in aliased] + list(fresh),
        input_output_aliases={i: i for i in range(n_al)},
        scratch_shapes=[pltpu.SemaphoreType.DMA((n_rem,)), pltpu.SemaphoreType.DMA((n_rem,))],
    )(*aliased, *inputs)


_HBM = pl.BlockSpec(memory_space=pltpu.HBM)
_SEM = pl.BlockSpec(memory_space=pltpu.SEMAPHORE)
_EFFECT = pltpu.SideEffectType.DATAFLOW_SIDE_EFFECTING


def _push_start(name, bufs, remote):
    n, n_rem = len(bufs), len(remote)

    def body(*refs):
        ins = refs[:n]
        send_sems, recv_sems = refs[n], refs[n + 1]
        token = refs[2 * n + 2]
        me = _me()
        for k, (mask, src_fn, dst_fn) in enumerate(remote):
            pltpu.make_async_remote_copy(
                src_ref=src_fn(ins, me), dst_ref=dst_fn(ins, me),
                send_sem=send_sems.at[k], recv_sem=recv_sems.at[k],
                device_id=_flip(me, mask), device_id_type=MESH).start()
        token[...] = jnp.zeros_like(token)

    out = pl.pallas_call(
        body, name=name,
        out_shape=(pltpu.SemaphoreType.DMA((n_rem,)), pltpu.SemaphoreType.DMA((n_rem,)),
                   *[pltpu.HBM(b.shape, b.dtype) for b in bufs], jax.ShapeDtypeStruct((8, 128), jnp.float32)),
        in_specs=[_HBM] * n,
        out_specs=(_SEM, _SEM, *[_HBM] * n, pl.BlockSpec(memory_space=pltpu.VMEM)),
        input_output_aliases={i: i + 2 for i in range(n)},
        compiler_params=pltpu.CompilerParams(has_side_effects=_EFFECT),
    )(*[pltpu.with_memory_space_constraint(b, pltpu.HBM) for b in bufs])
    return out[0], out[1], list(out[2:2 + n]), out[2 + n]


def _push_wait(name, send_sems, recv_sems, bufs, after, remote):
    n = len(bufs)

    def body(*refs):
        ins = refs[:n]
        s_sems, r_sems = refs[n], refs[n + 1]
        me = _me()
        for k, (mask, src_fn, dst_fn) in enumerate(remote):
            cp = pltpu.make_async_remote_copy(
                src_ref=src_fn(ins, me), dst_ref=dst_fn(ins, me),
                send_sem=s_sems.at[k], recv_sem=r_sems.at[k],
                device_id=_flip(me, mask), device_id_type=MESH)
            cp.wait_send()
            cp.wait_recv()

    out = pl.pallas_call(
        body, name=name,
        out_shape=tuple(pltpu.HBM(b.shape, b.dtype) for b in bufs),
        in_specs=[_HBM] * n + [_SEM, _SEM, _ANY], out_specs=tuple([_HBM] * n),
        input_output_aliases={i: i for i in range(n)},
        compiler_params=pltpu.CompilerParams(has_side_effects=_EFFECT),
    )(*bufs, send_sems, recv_sems, after)
    return list(out)


_ICI_MASKS = ((0, 1, 0), (1, 0, 0), (1, 1, 0))
_D2D_MASK = (0, 0, 1)
_ALL_MASKS = tuple((a, b, c) for a in (0, 1) for b in (0, 1) for c in (0, 1))[1:]


def _chip(me):
    return 2 * me[0] + me[1]


def _half(me):
    return pl.ds(me[2] * HALF_ROWS, HALF_ROWS)


def _other_half(me):
    return pl.ds((1 - me[2]) * HALF_ROWS, HALF_ROWS)


def _own_slot(mine):
    chip = _chip(_me())
    return lax.dynamic_update_slice(lax.empty((N_SHARD,) + mine.shape, mine.dtype), mine[None],
                                    (chip,) + (0,) * mine.ndim)


def _gather_ici_plan(with_small):
    remote = []
    for mask in _ICI_MASKS:
        mine = lambda bufs, me: bufs[0].at[_chip(me), _half(me)]
        remote.append((mask, mine, mine))
        if with_small:
            mine_small = lambda bufs, me: bufs[1].at[_chip(me)]
            remote.append((mask, mine_small, mine_small))
    return remote


def _gather_d2d(name, w):
    remote = []
    for j in range(1, N_SHARD):
        theirs = lambda al, ins, outs, me, j=j: al[0].at[(_chip(me) + j) % N_SHARD, _half(me)]
        remote.append((_D2D_MASK, theirs, theirs))
    (w,) = _push(name, [w], [], [], remote)
    return w


def _sum_pair(name, g, recv):
    _, r, _ = g.shape
    rb = _row_block(r // 2)
    nb = r // 2 // rb
    c = lax.axis_index("c").astype(jnp.int32).reshape(1)

    def body(c_ref, g_ref, r_ref, o_ref):
        o_ref[...] = (g_ref[...].astype(jnp.float32) + r_ref[...].astype(jnp.float32)).astype(o_ref.dtype)

    return pl.pallas_call(
        body, name=name,
        grid_spec=pltpu.PrefetchScalarGridSpec(
            num_scalar_prefetch=1, grid=(N_SHARD, nb),
            in_specs=[pl.BlockSpec((None, rb, D), lambda s, i, c_ref: (s, c_ref[0] * nb + i, 0)),
                      pl.BlockSpec((None, rb, D), lambda s, i, c_ref: (s, i, 0))],
            out_specs=pl.BlockSpec((None, rb, D), lambda s, i, c_ref: (s, i, 0))),
        out_shape=jax.ShapeDtypeStruct((N_SHARD, r // 2, D), g.dtype),
        compiler_params=_params(("arbitrary", "arbitrary")),
    )(c, g, recv)


def _sum_chips(name, p, recv):
    _, rh, _ = p.shape
    rb = _row_block(rh)
    nb = rh // rb
    s = jnp.stack([2 * lax.axis_index("x") + lax.axis_index("y"), lax.axis_index("c")]).astype(jnp.int32)

    def body(s_ref, p_ref, r_ref, o_ref):
        acc = p_ref[...].astype(jnp.float32)
        for j in range(3):
            acc = acc + r_ref[j].astype(jnp.float32)
        o_ref[...] = acc

    return pl.pallas_call(
        body, name=name,
        grid_spec=pltpu.PrefetchScalarGridSpec(
            num_scalar_prefetch=1, grid=(nb,),
            in_specs=[pl.BlockSpec((None, rb, D), lambda i, s_ref: (s_ref[0], i, 0)),
                      pl.BlockSpec((3, rb, D), lambda i, s_ref: (0, i, 0))],
            out_specs=pl.BlockSpec((rb, D), lambda i, s_ref: (s_ref[1] * nb + i, 0))),
        out_shape=jax.ShapeDtypeStruct((2 * rh, D), jnp.float32),
        compiler_params=_params(),
    )(s, p, recv)


def _rs_ici_plan():
    remote = []
    for j, mask in enumerate(_ICI_MASKS):
        remote.append((mask,
                       lambda bufs, me, mask=mask: bufs[0].at[_chip(_flip(me, mask))],
                       lambda bufs, me, j=j: bufs[1].at[j]))
    return remote


def _rs_pair(tag, g):
    _, r, _ = g.shape
    remote = [(_D2D_MASK,
               lambda al, ins, outs, me, s=s: ins[0].at[s, _other_half(me)],
               lambda al, ins, outs, me, s=s: outs[0].at[s]) for s in range(N_SHARD)]
    (recv,) = _push("rs_d2d_" + tag, [], [g], [jax.ShapeDtypeStruct((N_SHARD, r // 2, D), g.dtype)], remote)
    return _sum_pair("rs_sum_pair_" + tag, g, recv)


def _rs_finish(tag, p, recv3):
    mine = _sum_chips("rs_sum_chips_" + tag, p, recv3)
    half = lambda al, ins, outs, me: al[0].at[_half(me)]
    (both,) = _push("rs_share_" + tag, [mine], [], [], [(_D2D_MASK, half, half)])
    return both


def _all_reduce_small(v):
    s = v.shape[0]
    me = _me()
    every = lax.dynamic_update_slice(lax.empty((8, s, D), jnp.float32), v[None], (4 * me[0] + 2 * me[1] + me[2], 0, 0))
    slot = lambda al, ins, outs, me: al[0].at[4 * me[0] + 2 * me[1] + me[2]]
    (every,) = _push("small_all", [every], [], [], [(mask, slot, slot) for mask in _ALL_MASKS])

    def body(e_ref, o_ref):
        acc = e_ref[0]
        for j in range(1, 8):
            acc = acc + e_ref[j]
        o_ref[...] = acc

    return pl.pallas_call(
        body, name="small_sum", grid=(1,),
        in_specs=[pl.BlockSpec((8, s, D), lambda i: (0, 0, 0))],
        out_specs=pl.BlockSpec((s, D), lambda i: (0, 0)),
        out_shape=jax.ShapeDtypeStruct((s, D), jnp.float32),
        compiler_params=_params(),
    )(every)


def _adamw(name, w, g, m, v):
    r, c = w.shape
    rb = next(b for b in (256, 344, 128, 64, 32, 16, 8, r) if r % b == 0)

    def body(w_ref, g_ref, m_ref, v_ref, d_ref, mo_ref, vo_ref):
        g = g_ref[...]
        m = ADAM_B1 * m_ref[...] + (1.0 - ADAM_B1) * g
        v = ADAM_B2 * v_ref[...] + (1.0 - ADAM_B2) * (g * g)
        m_hat = m / (1.0 - ADAM_B1 ** ADAM_STEP)
        v_hat = v / (1.0 - ADAM_B2 ** ADAM_STEP)
        d_ref[...] = -ADAM_LR * (m_hat / (jnp.sqrt(v_hat) + ADAM_EPS) + ADAM_WD * w_ref[...])
        mo_ref[...] = m
        vo_ref[...] = v

    spec = pl.BlockSpec((rb, c), lambda i: (i, 0))
    return pl.pallas_call(
        body, name=name, grid=(r // rb,),
        in_specs=[spec] * 4, out_specs=[spec] * 3,
        out_shape=[jax.ShapeDtypeStruct((r, c), jnp.float32)] * 3,
        compiler_params=_params(),
    )(w, g, m, v)


_BIG = ("ffn1_w13", "ffn2_w13", "ffn1_w2", "ffn2_w2", "w_in", "w_out", "conv_pw")
_BIG_SHARD = {"ffn1_w13": (D, 1376), "ffn2_w13": (D, 1376), "ffn1_w2": (688, D), "ffn2_w2": (688, D),
              "w_in": (D, 704), "w_out": (256, D), "conv_pw": (64, 256)}


def _pack_rows(parts):
    flat = jnp.concatenate([p.reshape(-1) for p in parts])
    pad = (-flat.shape[0]) % (8 * D)
    if pad:
        flat = jnp.concatenate([flat, jnp.zeros((pad,), flat.dtype)])
    return flat.reshape(-1, D)


def _unpack_rows(buf, shapes):
    flat = buf.reshape(-1)
    out, off = [], 0
    for shp in shapes:
        n = math.prod(shp)
        out.append(flat[off:off + n].reshape(shp))
        off += n
    return out


def _pack_big_half(parts):
    pw = parts["conv_pw"].reshape(PW_ROWS // 2, D)
    return jnp.concatenate([parts["ffn1_w13"].T, parts["ffn2_w13"].T, parts["ffn1_w2"], parts["ffn2_w2"],
                            parts["w_in"].T, parts["w_out"], pw, jnp.zeros_like(pw)], axis=0)


def _unpack_big_half(buf):
    return {"ffn1_w13": buf[OFF_W13[0]:OFF_W13[0] + W13_ROWS].T, "ffn2_w13": buf[OFF_W13[1]:OFF_W13[1] + W13_ROWS].T,
            "ffn1_w2": buf[OFF_W2[0]:OFF_W2[0] + W2_ROWS], "ffn2_w2": buf[OFF_W2[1]:OFF_W2[1] + W2_ROWS],
            "w_in": buf[OFF_WIN:OFF_WIN + WIN_ROWS].T, "w_out": buf[OFF_WOUT:OFF_WOUT + WOUT_ROWS],
            "conv_pw": buf[OFF_PW:OFF_PW + PW_ROWS // 2].reshape(64, 256)}


def kernel(x, meta, ln_in_g, ln_in_b, ffn1_w13, ffn1_w2, w_in, pool_w, pool_scale, conv_dw, conv_db, conv_ln_g, conv_ln_b, conv_pw, ret_gn_g, w_out, ffn2_w13, ffn2_w2, ln_g, ln_b, loss_target, m_meta, m_ln_in_g, m_ln_in_b, m_ffn1_w13, m_ffn1_w2, m_w_in, m_pool_w, m_pool_scale, m_conv_dw, m_conv_db, m_conv_ln_g, m_conv_ln_b, m_conv_pw, m_ret_gn_g, m_w_out, m_ffn2_w13, m_ffn2_w2, m_ln_g, m_ln_b, v_meta, v_ln_in_g, v_ln_in_b, v_ffn1_w13, v_ffn1_w2, v_w_in, v_pool_w, v_pool_scale, v_conv_dw, v_conv_db, v_conv_ln_g, v_conv_ln_b, v_conv_pw, v_ret_gn_g, v_w_out, v_ffn2_w13, v_ffn2_w2, v_ln_g, v_ln_b):
    f32 = jnp.float32
    seq = x.shape[1]
    t = seq + ROW0
    me = _me()
    chip = _chip(me)
    big_w = {"ffn1_w13": ffn1_w13, "ffn2_w13": ffn2_w13, "ffn1_w2": ffn1_w2, "ffn2_w2": ffn2_w2,
             "w_in": w_in, "w_out": w_out, "conv_pw": conv_pw}

    wl = [_own_slot(_pack_big_half({n: big_w[n][l].astype(_WIRE) for n in _BIG})) for l in range(DEPTH)]
    small_shapes = [(N_META, 256), (DEPTH, CONV_W, 64), (DEPTH, 3, 256), (DEPTH, 3, 256)]
    small_all = _own_slot(_pack_rows([meta, conv_dw, ln_g, ln_b]))
    wrap = lambda f: (lambda al, ins, outs, me: f(al, me))
    wl[0], small_all = _push("gather_ici_0", [wl[0], small_all], [], [],
                             [(m, wrap(s), wrap(d)) for m, s, d in _gather_ici_plan(True)])
    wl[0] = _gather_d2d("gather_d2d_0", wl[0])
    g1_send, g1_recv, (w1_flying,), g1_token = _push_start("gather_ici_1_start", [wl[1]], _gather_ici_plan(False))

    sm = [_unpack_rows(small_all[s], small_shapes) for s in range(N_SHARD)]
    meta_f = jnp.concatenate([sm[s][0] for s in range(N_SHARD)], axis=1)
    cdw_f = jnp.concatenate([sm[s][1] for s in range(N_SHARD)], axis=2)
    lng_f = jnp.concatenate([sm[s][2] for s in range(N_SHARD)], axis=2)
    lnb_f = jnp.concatenate([sm[s][3] for s in range(N_SHARD)], axis=2)

    def mix_params(l):
        wbd = jnp.zeros((D_POOL, D_POOL), f32)
        for g in range(4):
            wbd = wbd.at[64 * g:64 * (g + 1), 64 * g:64 * (g + 1)].set(pool_w[l, g])
        cdw = jnp.pad(cdw_f[l], ((0, 1), (0, 0)))
        cvec = jnp.pad(jnp.stack([conv_db[l], conv_ln_g[l], conv_ln_b[l]]), ((0, 5), (0, 0)))
        wpw = wl[l][:, OFF_PW:OFF_PW + PW_ROWS // 2].reshape(D_CONV, D_CONV)
        return (wbd.astype(_MM), pool_scale[l][None], cdw, cvec, wpw, ret_gn_g[l][None])

    gb_of = lambda l, i: jnp.stack([lng_f[l, i], lnb_f[l, i]])
    gb_in = jnp.stack([ln_in_g, ln_in_b])

    pos = jnp.arange(t, dtype=f32) - PAD
    inv_freq = ROPE_BASE ** (-jnp.arange(0, DH, 2, dtype=f32) / DH)
    ang = pos[:, None] * inv_freq[None, :]
    cs = jnp.concatenate([jnp.cos(ang), jnp.cos(ang)], axis=1)
    sn = jnp.concatenate([-jnp.sin(ang), jnp.sin(ang)], axis=1)

    raw = jnp.concatenate([jnp.zeros((PAD, D), f32), meta_f, x[0]], axis=0)
    target = jnp.concatenate([jnp.zeros((ROW0, D), f32), loss_target[0]], axis=0)
    xh, rstd = _ln_in_fwd(raw)
    cur = (xh, rstd, gb_in + g1_token[0, 0])
    saved = []
    for l in range(DEPTH):
        if l == 1:
            (w1_landed,) = _push_wait("gather_ici_1_wait", g1_send, g1_recv, [w1_flying], cur[0],
                                      _gather_ici_plan(False))
            wl[1] = _gather_d2d("gather_d2d_1", w1_landed)
        a0 = cur
        xh1, r1 = _ffn_fwd(f"ffn1_fwd_{l}", a0[0], a0[2], wl[l], 0)
        a1 = (xh1, r1, gb_of(l, 0))
        z = _mix_in_fwd(f"mix_in_fwd_{l}", a1[0], a1[2], wl[l])
        mp = mix_params(l)
        ycat, st_in, ut_in, xt_in = _mix_core_fwd(f"mix_core_fwd_{l}", z, cs, sn, *mp)
        xh2, r2 = _mix_out_fwd(f"mix_out_fwd_{l}", a1[0], a1[2], ycat, wl[l])
        a2 = (xh2, r2, gb_of(l, 1))
        xh3, r3 = _ffn_fwd(f"ffn2_fwd_{l}", a2[0], a2[2], wl[l], 1)
        a3 = (xh3, r3, gb_of(l, 2))
        saved.append((a0, a1, a2, a3, z, ycat, st_in, ut_in, xt_in, mp))
        cur = a3

    dy, loss_part = _loss_fwd_bwd(cur[0], cur[2], target)
    loss = lax.psum(loss_part[0, 0], ("x", "y", "c"))

    g_ln_g = [[None] * 3 for _ in range(DEPTH)]
    g_ln_b = [[None] * 3 for _ in range(DEPTH)]
    g_small = [dict() for _ in range(DEPTH)]
    slot = lambda j: j

    def ffn_grads(gbuf, tag, l, f, hb, hid, dau, dffn):
        gbuf = _dw_into(f"dw13_{tag}_{l}", gbuf, dau, hb, FF_SLOT, [(0, W13_ROWS, slot, OFF_W13[f])])
        return _dw_into(f"dw2_{tag}_{l}", gbuf, hid, dffn, FF_SLOT,
                        [(0, W2_ROWS, lambda j: 2 * j, OFF_W2[f]), (W2_ROWS, W2_ROWS, lambda j: 2 * j + 1, OFF_W2[f])])

    rs_plan = _rs_ici_plan()
    token = jnp.zeros((), f32)
    for l in reversed(range(DEPTH)):
        a0, a1, a2, a3, z, ycat, st_in, ut_in, xt_in, mp = saved[l]
        gbuf = lax.empty((N_SHARD, SHARD_ROWS, D), _WIRE)
        dh, hb, hid, dau, dffn, dgb = _ffn_bwd(f"ffn2_bwd_{l}", dy, a3[0], a3[1], a3[2] + token, a2[0], a2[2],
                                               wl[l], 1)
        g_ln_g[l][2], g_ln_b[l][2] = dgb[0], dgb[1]
        gbuf = ffn_grads(gbuf, "ffn2", l, 1, hb, hid, dau, dffn)
        dh_res, dycat, dsb, dgb = _mix_out_bwd(f"mix_out_bwd_{l}", dh, a2[0], a2[1], a2[2], wl[l])
        g_ln_g[l][1], g_ln_b[l][1] = dgb[0], dgb[1]
        gbuf = _dw_into(f"dw_out_{l}", gbuf, ycat, dsb, D,
                        [(WOUT_ROWS * s, WOUT_ROWS, lambda j, s=s: s, OFF_WOUT) for s in range(N_SHARD)])
        dz, dwbd, dwpw, dcdw, dsm = _mix_core_bwd(f"mix_core_bwd_{l}", z, dycat, cs, sn, st_in, ut_in, xt_in, *mp)
        pw = dwpw.astype(_WIRE).reshape(N_SHARD, PW_ROWS // 2, D)
        gbuf = lax.dynamic_update_slice(gbuf, jnp.concatenate([pw, jnp.zeros_like(pw)], axis=1), (0, OFF_PW, 0))
        g_small[l] = dict(
            pool_w=jnp.stack([dwbd[64 * g:64 * (g + 1), 64 * g:64 * (g + 1)] for g in range(4)]),
            pool_scale=dsm[1, :256], conv_db=dsm[2, :256], conv_ln_g=dsm[3, :256], conv_ln_b=dsm[4, :256],
            ret_gn_g=dsm[0], conv_dw=dcdw[:CONV_W])
        dh, hb = _mix_in_bwd(f"mix_in_bwd_{l}", dh_res, dz, a1[0], a1[2], wl[l])
        gbuf = _dw_into(f"dw_in_{l}", gbuf, dz, hb, D_IN,
                        [(WIN_ROWS * s, WIN_ROWS, lambda j, s=s: s, OFF_WIN) for s in range(N_SHARD)])
        dh, hb, hid, dau, dffn, dgb = _ffn_bwd(f"ffn1_bwd_{l}", dh, a1[0], a1[1], a1[2], a0[0], a0[2], wl[l], 0)
        g_ln_g[l][0], g_ln_b[l][0] = dgb[0], dgb[1]
        gbuf = ffn_grads(gbuf, "ffn1", l, 0, hb, hid, dau, dffn)
        dy = dh
        p = _rs_pair(str(l), gbuf)
        landing = lax.empty((3, HALF_ROWS, D), _WIRE)
        if l == DEPTH - 1:
            rs_send, rs_recv, flying, rs_token = _push_start("rs_ici_1_start", [p, landing], rs_plan)
            token = rs_token[0, 0]
        else:
            (recv0,) = _push("rs_ici_0", [], [p], [jax.ShapeDtypeStruct(landing.shape, landing.dtype)],
                             [(m, lambda al, ins, outs, me, s=s: s([ins[0], outs[0]], me),
                               lambda al, ins, outs, me, d=d: d([ins[0], outs[0]], me)) for m, s, d in rs_plan])
            g_layer0 = _rs_finish("0", p, recv0)
    p1, recv1 = _push_wait("rs_ici_1_wait", rs_send, rs_recv, flying, dy, rs_plan)
    gsum = [g_layer0, _rs_finish("1", p1, recv1)]
    d_raw, dgb_in = _ln_in_bwd(dy, saved[0][0][0], saved[0][0][1], gb_in)
    grad_x = d_raw[ROW0:][None]

    g_big = [_unpack_big_half(gsum[l]) for l in range(DEPTH)]
    grads = {n: jnp.stack([g_big[l][n] for l in range(DEPTH)]) for n in _BIG}

    small_parts = [
        d_raw[PAD:ROW0],
        jnp.stack([g_small[l]["conv_dw"] for l in range(DEPTH)]),
        jnp.stack([jnp.stack(g_ln_g[l]) for l in range(DEPTH)]),
        jnp.stack([jnp.stack(g_ln_b[l]) for l in range(DEPTH)]),
        dgb_in[0], dgb_in[1],
        jnp.stack([g_small[l]["pool_w"] for l in range(DEPTH)]),
        jnp.stack([g_small[l]["pool_scale"] for l in range(DEPTH)]),
        jnp.stack([g_small[l]["conv_db"] for l in range(DEPTH)]),
        jnp.stack([g_small[l]["conv_ln_g"] for l in range(DEPTH)]),
        jnp.stack([g_small[l]["conv_ln_b"] for l in range(DEPTH)]),
        jnp.stack([g_small[l]["ret_gn_g"] for l in range(DEPTH)]),
    ]
    red = _unpack_rows(_all_reduce_small(_pack_rows(small_parts)), [p.shape for p in small_parts])
    grads["meta"] = lax.dynamic_slice_in_dim(red[0], 256 * chip, 256, axis=1)
    grads["conv_dw"] = lax.dynamic_slice_in_dim(red[1], 64 * chip, 64, axis=2)
    grads["ln_g"] = lax.dynamic_slice_in_dim(red[2], 256 * chip, 256, axis=2)
    grads["ln_b"] = lax.dynamic_slice_in_dim(red[3], 256 * chip, 256, axis=2)
    for n, v in zip(("ln_in_g", "ln_in_b", "pool_w", "pool_scale", "conv_db", "conv_ln_g", "conv_ln_b", "ret_gn_g"),
                    red[4:]):
        grads[n] = v

    names = ['meta', 'ln_in_g', 'ln_in_b', 'ffn1_w13', 'ffn1_w2', 'w_in', 'pool_w', 'pool_scale', 'conv_dw',
             'conv_db', 'conv_ln_g', 'conv_ln_b', 'conv_pw', 'ret_gn_g', 'w_out', 'ffn2_w13', 'ffn2_w2', 'ln_g', 'ln_b']
    ws = dict(meta=meta, ln_in_g=ln_in_g, ln_in_b=ln_in_b, ffn1_w13=ffn1_w13, ffn1_w2=ffn1_w2, w_in=w_in,
              pool_w=pool_w, pool_scale=pool_scale, conv_dw=conv_dw, conv_db=conv_db, conv_ln_g=conv_ln_g,
              conv_ln_b=conv_ln_b, conv_pw=conv_pw, ret_gn_g=ret_gn_g, w_out=w_out, ffn2_w13=ffn2_w13,
              ffn2_w2=ffn2_w2, ln_g=ln_g, ln_b=ln_b)
    ms = dict(meta=m_meta, ln_in_g=m_ln_in_g, ln_in_b=m_ln_in_b, ffn1_w13=m_ffn1_w13, ffn1_w2=m_ffn1_w2,
              w_in=m_w_in, pool_w=m_pool_w, pool_scale=m_pool_scale, conv_dw=m_conv_dw, conv_db=m_conv_db,
              conv_ln_g=m_conv_ln_g, conv_ln_b=m_conv_ln_b, conv_pw=m_conv_pw, ret_gn_g=m_ret_gn_g,
              w_out=m_w_out, ffn2_w13=m_ffn2_w13, ffn2_w2=m_ffn2_w2, ln_g=m_ln_g, ln_b=m_ln_b)
    vs = dict(meta=v_meta, ln_in_g=v_ln_in_g, ln_in_b=v_ln_in_b, ffn1_w13=v_ffn1_w13, ffn1_w2=v_ffn1_w2,
              w_in=v_w_in, pool_w=v_pool_w, pool_scale=v_pool_scale, conv_dw=v_conv_dw, conv_db=v_conv_db,
              conv_ln_g=v_conv_ln_g, conv_ln_b=v_conv_ln_b, conv_pw=v_conv_pw, ret_gn_g=v_ret_gn_g,
              w_out=v_w_out, ffn2_w13=v_ffn2_w13, ffn2_w2=v_ffn2_w2, ln_g=v_ln_g, ln_b=v_ln_b)
    delta, new_m, new_v = {}, {}, {}
    for n in _BIG:
        shp = ws[n].shape
        two = lambda a: a.reshape(-1, shp[-1])
        d_, m_, v_ = _adamw("adamw_" + n, two(ws[n]), two(grads[n]), two(ms[n]), two(vs[n]))
        delta[n], new_m[n], new_v[n] = d_.reshape(shp), m_.reshape(shp), v_.reshape(shp)
    small_names = [n for n in names if n not in _BIG]
    pk = lambda d: _pack_rows([d[n] for n in small_names])
    d_, m_, v_ = _adamw("adamw_small", pk(ws), pk(grads), pk(ms), pk(vs))
    shapes = [ws[n].shape for n in small_names]
    for n, a, b, c in zip(small_names, _unpack_rows(d_, shapes), _unpack_rows(m_, shapes), _unpack_rows(v_, shapes)):
        delta[n], new_m[n], new_v[n] = a, b, c

    return (loss, grad_x, *[grads[n] for n in names], *[delta[n] for n in names],
            *[new_m[n] for n in names], *[new_v[n] for n in names])
```

```python
import functools
import math

import jax
import jax.numpy as jnp
from jax import lax
from jax.experimental import pallas as pl
from jax.experimental.pallas import tpu as pltpu

D = 1024
DEPTH = 2
N_META = 16
PAD = 112
ROW0 = PAD + N_META
D_POOL = 256
D_CONV = 256
D_RET = 512
HEADS = 4
DH = 128
CONV_W = 31
D_FF = 2752
FF_SLOT = 1408
D_FFP = 2 * FF_SLOT
D_IN = 2816
N_SHARD = 4
ALPHA = (2.0 * DEPTH) ** 0.25
LN_EPS = 1e-5
ROPE_BASE = 10000.0
LOG_GAMMA = tuple(math.log(1.0 - 2.0 ** (-5.0 - h)) for h in range(HEADS))
ADAM_LR, ADAM_B1, ADAM_B2, ADAM_EPS, ADAM_WD, ADAM_STEP = 0.001, 0.9, 0.999, 1e-08, 0.01, 10

_MM = jnp.bfloat16
_WIRE = jnp.bfloat16
_VMEM_LIMIT = 56 * 1024 * 1024
_FFN_ROWS = 640

MESH = pl.DeviceIdType.MESH
_ANY = pl.BlockSpec(memory_space=pl.ANY)

W13_ROWS = 1376
W2_ROWS = 688
WIN_ROWS = 704
WOUT_ROWS = 256
PW_ROWS = 32
OFF_W13 = (0, W13_ROWS)
OFF_W2 = (2 * W13_ROWS, 2 * W13_ROWS + W2_ROWS)
OFF_WIN = 2 * W13_ROWS + 2 * W2_ROWS
OFF_WOUT = OFF_WIN + WIN_ROWS
OFF_PW = OFF_WOUT + WOUT_ROWS
SHARD_ROWS = OFF_PW + PW_ROWS
HALF_ROWS = SHARD_ROWS // 2
W2_SLOT_OFF = (0, W2_ROWS, FF_SLOT, FF_SLOT + W2_ROWS)


def _start_rows(w_ref, off, n, dst_of, sems, k0):
    cps = []
    for s in range(N_SHARD):
        cp = pltpu.make_async_copy(w_ref.at[s, pl.ds(off, n)], dst_of(s), sems.at[k0 + s])
        cp.start()
        cps.append(cp)
    return cps


def _load_ffn_weights(w_ref, f, w13, w2, sems):
    cps = _start_rows(w_ref, OFF_W13[f], W13_ROWS, lambda s: w13.at[s, pl.ds(0, W13_ROWS)], sems, 0)
    cps += _start_rows(w_ref, OFF_W2[f], W2_ROWS, lambda s: w2.at[pl.ds(W2_SLOT_OFF[s], W2_ROWS)], sems, 4)
    zpad = jnp.zeros((FF_SLOT - W13_ROWS, D), w13.dtype)
    for s in range(N_SHARD):
        w13[s, W13_ROWS:FF_SLOT, :] = zpad
    w2[W13_ROWS:FF_SLOT, :] = zpad
    w2[FF_SLOT + W13_ROWS:D_FFP, :] = zpad
    for cp in cps:
        cp.wait()


def _load_rows(w_ref, off, n, dst, sems):
    for cp in _start_rows(w_ref, off, n, lambda s: dst.at[pl.ds(s * n, n)], sems, 0):
        cp.wait()


def _dot(a, b):
    return jnp.dot(a, b, preferred_element_type=jnp.float32)


def _dot_nt(a, b):
    return lax.dot_general(a, b, (((1,), (1,)), ((), ())), preferred_element_type=jnp.float32)


def _dot_tn(a, b):
    return lax.dot_general(a, b, (((0,), (0,)), ((), ())), preferred_element_type=jnp.float32)


def _params(sem=("arbitrary",)):
    return pltpu.CompilerParams(dimension_semantics=sem, vmem_limit_bytes=_VMEM_LIMIT)


def _row_block(t, cap=640):
    for rb in (640, 320, 128):
        if rb <= cap and t % rb == 0 and (t > 1024 or rb == 128):
            return rb
    raise ValueError(t)


def _rows(rb, n):
    return pl.BlockSpec((rb, n), lambda i: (i, 0))


def _full(shape):
    nd = len(shape)
    return pl.BlockSpec(tuple(shape), lambda i: (0,) * nd, pipeline_mode=pl.Buffered(1))


def _acc(shape):
    nd = len(shape)
    return pl.BlockSpec(tuple(shape), lambda i: (0,) * nd)


def _sigmoid(x):
    return 1.0 / (1.0 + jnp.exp(-x))


def _ln_fwd(s):
    mu = jnp.mean(s, axis=-1, keepdims=True)
    xc = s - mu
    var = jnp.mean(xc * xc, axis=-1, keepdims=True)
    rstd = lax.rsqrt(var + LN_EPS)
    return xc * rstd, rstd


def _ln_bwd(dxh, xh, rstd):
    m1 = jnp.mean(dxh, axis=-1, keepdims=True)
    m2 = jnp.mean(dxh * xh, axis=-1, keepdims=True)
    return rstd * (dxh - m1 - xh * m2)


def _ln_in_fwd(raw):
    t = raw.shape[0]
    rb = _row_block(t)

    def body(raw_ref, xh_ref, rstd_ref):
        xh, rstd = _ln_fwd(raw_ref[...])
        xh_ref[...] = xh
        rstd_ref[...] = rstd

    return pl.pallas_call(
        body, name="ln_in_fwd", grid=(t // rb,),
        in_specs=[_rows(rb, D)],
        out_specs=[_rows(rb, D), _rows(rb, 1)],
        out_shape=[jax.ShapeDtypeStruct((t, D), jnp.float32), jax.ShapeDtypeStruct((t, 1), jnp.float32)],
        compiler_params=_params(),
    )(raw)


def _ffn_fwd(name, xh, gb, wfull, f):
    t = xh.shape[0]
    rb = _row_block(t, _FFN_ROWS)

    def body(xh_ref, gb_ref, w_ref, out_ref, rstd_ref, au_ref, hb_ref, w13, w2, sems):
        @pl.when(pl.program_id(0) == 0)
        def _():
            _load_ffn_weights(w_ref, f, w13, w2, sems)

        h = xh_ref[...] * gb_ref[0:1, :] + gb_ref[1:2, :]
        hb = h.astype(_MM)
        hb_ref[...] = hb
        acc = jnp.zeros((rb, D), jnp.float32)
        for j in range(2):
            lo = j * FF_SLOT
            a = _dot_nt(hb, w13[j])
            u = _dot_nt(hb, w13[2 + j])
            au_ref[:, lo:lo + FF_SLOT] = a.astype(_MM)
            au_ref[:, D_FFP + lo:D_FFP + lo + FF_SLOT] = u.astype(_MM)
            hid = (a * _sigmoid(a) * u).astype(_MM)
            acc = acc + _dot(hid, w2[lo:lo + FF_SLOT, :])
        xo, rstd = _ln_fwd(ALPHA * h + 0.5 * acc)
        out_ref[...] = xo
        rstd_ref[...] = rstd

    return pl.pallas_call(
        body, name=name, grid=(t // rb,),
        in_specs=[_rows(rb, D), _full((2, D)), _ANY],
        out_specs=[_rows(rb, D), _rows(rb, 1), _rows(rb, 2 * D_FFP), _rows(rb, D)],
        out_shape=[jax.ShapeDtypeStruct((t, D), jnp.float32), jax.ShapeDtypeStruct((t, 1), jnp.float32),
                   jax.ShapeDtypeStruct((t, 2 * D_FFP), _MM), jax.ShapeDtypeStruct((t, D), _MM)],
        scratch_shapes=[pltpu.VMEM((N_SHARD, FF_SLOT, D), _MM), pltpu.VMEM((D_FFP, D), _MM),
                        pltpu.SemaphoreType.DMA((8,))],
        compiler_params=_params(),
    )(xh, gb, wfull)


def _ffn_bwd(name, dy, xo, rstd, gb_out, au, wfull, f):
    t = xo.shape[0]
    rb = _row_block(t, 320)

    def body(dy_ref, xo_ref, rstd_ref, gbo_ref, au_ref, w_ref,
             dh_ref, hid_ref, dau_ref, dffn_ref, dgb_ref, w13, w2, sems):
        i = pl.program_id(0)

        @pl.when(i == 0)
        def _():
            dgb_ref[...] = jnp.zeros_like(dgb_ref)
            _load_ffn_weights(w_ref, f, w13, w2, sems)

        dy = dy_ref[...]
        xo = xo_ref[...]
        dgb_ref[0:1, :] += jnp.sum(dy * xo, axis=0, keepdims=True)
        dgb_ref[1:2, :] += jnp.sum(dy, axis=0, keepdims=True)
        ds = _ln_bwd(dy * gbo_ref[0:1, :], xo, rstd_ref[...])
        dffn = (0.5 * ds).astype(_MM)
        dffn_ref[...] = dffn
        dh = ALPHA * ds
        for j in range(2):
            lo = j * FF_SLOT
            a = au_ref[:, lo:lo + FF_SLOT].astype(jnp.float32)
            u = au_ref[:, D_FFP + lo:D_FFP + lo + FF_SLOT].astype(jnp.float32)
            sg = _sigmoid(a)
            si = a * sg
            hid_ref[:, lo:lo + FF_SLOT] = (si * u).astype(_MM)
            dhid = _dot_nt(dffn, w2[lo:lo + FF_SLOT, :])
            da = (dhid * u * (sg * (1.0 + a * (1.0 - sg)))).astype(_MM)
            du = (dhid * si).astype(_MM)
            dau_ref[:, lo:lo + FF_SLOT] = da
            dau_ref[:, D_FFP + lo:D_FFP + lo + FF_SLOT] = du
            dh = dh + _dot(da, w13[j]) + _dot(du, w13[2 + j])
        dh_ref[...] = dh

    return pl.pallas_call(
        body, name=name, grid=(t // rb,),
        in_specs=[_rows(rb, D), _rows(rb, D), _rows(rb, 1), _full((2, D)), _rows(rb, 2 * D_FFP), _ANY],
        out_specs=[_rows(rb, D), _rows(rb, D_FFP), _rows(rb, 2 * D_FFP), _rows(rb, D), _acc((8, D))],
        out_shape=[jax.ShapeDtypeStruct((t, D), jnp.float32),
                   jax.ShapeDtypeStruct((t, D_FFP), _MM), jax.ShapeDtypeStruct((t, 2 * D_FFP), _MM),
                   jax.ShapeDtypeStruct((t, D), _MM), jax.ShapeDtypeStruct((8, D), jnp.float32)],
        scratch_shapes=[pltpu.VMEM((N_SHARD, FF_SLOT, D), _MM), pltpu.VMEM((D_FFP, D), _MM),
                        pltpu.SemaphoreType.DMA((8,))],
        compiler_params=_params(),
    )(dy, xo, rstd, gb_out, au, wfull)


def _mix_in_fwd(name, xh, gb, wfull):
    t = xh.shape[0]
    rb = _row_block(t)

    def body(xh_ref, gb_ref, w_ref, z_ref, wt, sems):
        @pl.when(pl.program_id(0) == 0)
        def _():
            _load_rows(w_ref, OFF_WIN, WIN_ROWS, wt, sems)

        h = xh_ref[...] * gb_ref[0:1, :] + gb_ref[1:2, :]
        z = _dot_nt(h.astype(_MM), wt[...])
        row = pl.program_id(0) * rb + lax.broadcasted_iota(jnp.int32, (rb, 1), 0)
        z_ref[...] = jnp.where(row >= PAD, z, 0.0)

    return pl.pallas_call(
        body, name=name, grid=(t // rb,),
        in_specs=[_rows(rb, D), _full((2, D)), _ANY],
        out_specs=_rows(rb, D_IN),
        out_shape=jax.ShapeDtypeStruct((t, D_IN), jnp.float32),
        scratch_shapes=[pltpu.VMEM((D_IN, D), _MM), pltpu.SemaphoreType.DMA((4,))],
        compiler_params=_params(),
    )(xh, gb, wfull)


def _mix_in_bwd(name, dh_res, dz, xh, gb, wfull):
    t = xh.shape[0]
    rb = _row_block(t)

    def body(dhr_ref, dz_ref, xh_ref, gb_ref, w_ref, dh_ref, hb_ref, wt, sems):
        @pl.when(pl.program_id(0) == 0)
        def _():
            _load_rows(w_ref, OFF_WIN, WIN_ROWS, wt, sems)

        dh_ref[...] = dhr_ref[...] + _dot(dz_ref[...], wt[...])
        hb_ref[...] = (xh_ref[...] * gb_ref[0:1, :] + gb_ref[1:2, :]).astype(_MM)

    return pl.pallas_call(
        body, name=name, grid=(t // rb,),
        in_specs=[_rows(rb, D), _rows(rb, D_IN), _rows(rb, D), _full((2, D)), _ANY],
        out_specs=[_rows(rb, D), _rows(rb, D)],
        out_shape=[jax.ShapeDtypeStruct((t, D), jnp.float32), jax.ShapeDtypeStruct((t, D), _MM)],
        scratch_shapes=[pltpu.VMEM((D_IN, D), _MM), pltpu.SemaphoreType.DMA((4,))],
        compiler_params=_params(),
    )(dh_res, dz, xh, gb, wfull)


def _mix_out_fwd(name, xh, gb, ycat, wfull):
    t = xh.shape[0]
    rb = _row_block(t)

    def body(xh_ref, gb_ref, y_ref, w_ref, out_ref, rstd_ref, wo, sems):
        @pl.when(pl.program_id(0) == 0)
        def _():
            _load_rows(w_ref, OFF_WOUT, WOUT_ROWS, wo, sems)

        h = xh_ref[...] * gb_ref[0:1, :] + gb_ref[1:2, :]
        xo, rstd = _ln_fwd(ALPHA * h + _dot(y_ref[...], wo[...]))
        out_ref[...] = xo
        rstd_ref[...] = rstd

    return pl.pallas_call(
        body, name=name, grid=(t // rb,),
        in_specs=[_rows(rb, D), _full((2, D)), _rows(rb, D), _ANY],
        out_specs=[_rows(rb, D), _rows(rb, 1)],
        out_shape=[jax.ShapeDtypeStruct((t, D), jnp.float32), jax.ShapeDtypeStruct((t, 1), jnp.float32)],
        scratch_shapes=[pltpu.VMEM((D, D), _MM), pltpu.SemaphoreType.DMA((4,))],
        compiler_params=_params(),
    )(xh, gb, ycat, wfull)


def _mix_out_bwd(name, dy, xo, rstd, gb_out, wfull):
    t = xo.shape[0]
    rb = _row_block(t)

    def body(dy_ref, xo_ref, rstd_ref, gbo_ref, w_ref, dhr_ref, dyc_ref, dsb_ref, dgb_ref, wo, sems):
        @pl.when(pl.program_id(0) == 0)
        def _():
            dgb_ref[...] = jnp.zeros_like(dgb_ref)
            _load_rows(w_ref, OFF_WOUT, WOUT_ROWS, wo, sems)

        dy = dy_ref[...]
        xo = xo_ref[...]
        dgb_ref[0:1, :] += jnp.sum(dy * xo, axis=0, keepdims=True)
        dgb_ref[1:2, :] += jnp.sum(dy, axis=0, keepdims=True)
        ds = _ln_bwd(dy * gbo_ref[0:1, :], xo, rstd_ref[...])
        dsb = ds.astype(_MM)
        dsb_ref[...] = dsb
        dhr_ref[...] = ALPHA * ds
        dyc_ref[...] = _dot_nt(dsb, wo[...])

    return pl.pallas_call(
        body, name=name, grid=(t // rb,),
        in_specs=[_rows(rb, D), _rows(rb, D), _rows(rb, 1), _full((2, D)), _ANY],
        out_specs=[_rows(rb, D), _rows(rb, D), _rows(rb, D), _acc((8, D))],
        out_shape=[jax.ShapeDtypeStruct((t, D), jnp.float32), jax.ShapeDtypeStruct((t, D), jnp.float32),
                   jax.ShapeDtypeStruct((t, D), _MM), jax.ShapeDtypeStruct((8, D), jnp.float32)],
        scratch_shapes=[pltpu.VMEM((D, D), _MM), pltpu.SemaphoreType.DMA((4,))],
        compiler_params=_params(),
    )(dy, xo, rstd, gb_out, wfull)


def _loss_fwd_bwd(xh, gb, target):
    t = xh.shape[0]
    rb = _row_block(t)

    def body(xh_ref, gb_ref, tg_ref, dy_ref, loss_ref):
        @pl.when(pl.program_id(0) == 0)
        def _():
            loss_ref[...] = jnp.zeros_like(loss_ref)

        y = xh_ref[...] * gb_ref[0:1, :] + gb_ref[1:2, :]
        row = pl.program_id(0) * rb + lax.broadcasted_iota(jnp.int32, (rb, 1), 0)
        err = jnp.where(row >= ROW0, y - tg_ref[...], 0.0)
        dy_ref[...] = err * (1.0 / D)
        per_row = jnp.mean(err * err, axis=-1, keepdims=True)
        loss_ref[...] += 0.5 * jnp.sum(per_row, axis=0, keepdims=True)

    return pl.pallas_call(
        body, name="loss", grid=(t // rb,),
        in_specs=[_rows(rb, D), _full((2, D)), _rows(rb, D)],
        out_specs=[_rows(rb, D), _acc((1, 1))],
        out_shape=[jax.ShapeDtypeStruct((t, D), jnp.float32), jax.ShapeDtypeStruct((1, 1), jnp.float32)],
        compiler_params=_params(),
    )(xh, gb, target)


def _ln_in_bwd(dy, xh, rstd, gb):
    t = xh.shape[0]
    rb = _row_block(t)

    def body(dy_ref, xh_ref, rstd_ref, gb_ref, dx_ref, dgb_ref):
        @pl.when(pl.program_id(0) == 0)
        def _():
            dgb_ref[...] = jnp.zeros_like(dgb_ref)

        dy = dy_ref[...]
        xh = xh_ref[...]
        dgb_ref[0:1, :] += jnp.sum(dy * xh, axis=0, keepdims=True)
        dgb_ref[1:2, :] += jnp.sum(dy, axis=0, keepdims=True)
        dx_ref[...] = _ln_bwd(dy * gb_ref[0:1, :], xh, rstd_ref[...])

    return pl.pallas_call(
        body, name="ln_in_bwd", grid=(t // rb,),
        in_specs=[_rows(rb, D), _rows(rb, D), _rows(rb, 1), _full((2, D))],
        out_specs=[_rows(rb, D), _acc((8, D))],
        out_shape=[jax.ShapeDtypeStruct((t, D), jnp.float32), jax.ShapeDtypeStruct((8, D), jnp.float32)],
        compiler_params=_params(),
    )(dy, xh, rstd, gb)


def _dw_into(name, gpack, x, y, cols, pieces):
    t, k = x.shape
    tt = _row_block(t)
    nt = t // tt

    def body(x_ref, y_ref, g_in, g_out, acc, stage, sems):
        j = pl.program_id(0)
        s = pl.program_id(1)

        @pl.when(s == 0)
        def _():
            acc[...] = jnp.zeros_like(acc)

        acc[...] += _dot_tn(x_ref[...], y_ref[...])

        @pl.when(s == nt - 1)
        def _():
            stage[...] = acc[...].astype(stage.dtype)
            cps = []
            for q, (lo, n, chip_of, off) in enumerate(pieces):
                cp = pltpu.make_async_copy(stage.at[pl.ds(lo, n)], g_out.at[chip_of(j), pl.ds(off, n)],
                                           sems.at[q])
                cp.start()
                cps.append(cp)
            for cp in cps:
                cp.wait()

    return pl.pallas_call(
        body, name=name, grid=(k // cols, nt),
        in_specs=[pl.BlockSpec((tt, cols), lambda j, s: (s, j)), pl.BlockSpec((tt, D), lambda j, s: (s, 0)), _ANY],
        out_specs=_ANY,
        out_shape=jax.ShapeDtypeStruct(gpack.shape, gpack.dtype),
        input_output_aliases={2: 0},
        scratch_shapes=[pltpu.VMEM((cols, D), jnp.float32), pltpu.VMEM((cols, D), gpack.dtype),
                        pltpu.SemaphoreType.DMA((len(pieces),))],
        compiler_params=_params(("arbitrary", "arbitrary")),
    )(x, y, gpack)


_TAIL_U = 32
_TAIL_X = 16
_MIX_ROWS = 320


def _decay_mask(rb, h):
    ii = lax.broadcasted_iota(jnp.int32, (rb, rb), 0)
    jj = lax.broadcasted_iota(jnp.int32, (rb, rb), 1)
    dist = jnp.abs(ii - jj).astype(jnp.float32)
    vis = (jj >> 6) <= (ii >> 6)
    return jnp.where(vis, jnp.exp(LOG_GAMMA[h] * dist), 0.0)


def _row_decays(rb, h):
    r = lax.broadcasted_iota(jnp.int32, (rb, DH), 0).astype(jnp.float32)
    return jnp.exp(LOG_GAMMA[h] * (r + 1.0)), jnp.exp(LOG_GAMMA[h] * (rb - 1.0 - r))


def _rope(x, cs, sn):
    return x * cs + pltpu.roll(x, DH // 2, 1) * sn


def _rope_t(dx, cs, sn):
    return dx * cs + pltpu.roll(dx * sn, DH // 2, 1)


def _pool_count(blk, rb):
    row = blk * rb + lax.broadcasted_iota(jnp.int32, (rb, D_POOL), 0) - PAD
    lane = lax.broadcasted_iota(jnp.int32, (rb, D_POOL), 1)
    win = jnp.left_shift(2, lane >> 6)
    return jnp.clip(row + 1, 1, win).astype(jnp.float32)


def _pool_select(p2, p4, p8, p16):
    lane = lax.broadcasted_iota(jnp.int32, p2.shape, 1)
    return jnp.where(lane < 64, p2, jnp.where(lane < 128, p4, jnp.where(lane < 192, p8, p16)))


def _window_sums(ext_ref, base, rows, sign):
    acc = ext_ref[pl.ds(base, rows), :]
    outs = []
    for k in range(1, 16):
        acc = acc + ext_ref[pl.ds(base + sign * k, rows), :]
        if k in (1, 3, 7, 15):
            outs.append(acc)
    return _pool_select(*outs)


def _sub_rows(rb):
    return 128 if rb % 128 == 0 else 64


def _mix_core_fwd(name, z, cs, sn, wbd, pscale, cdw, cvec, wpw, gn):
    t = z.shape[0]
    rb = _row_block(t, _MIX_ROWS)
    nblk = t // rb
    sr = _sub_rows(rb)

    def body(z_ref, cs_ref, sn_ref, wbd_ref, ps_ref, cdw_ref, cvec_ref, wpw_ref, gn_ref,
             y_ref, st_ref, ut_ref, xt_ref,
             uext, xext, cv, ypre, state, wmask):
        i = pl.program_id(0)

        @pl.when(i == 0)
        def _():
            state[...] = jnp.zeros_like(state)
            uext[0:_TAIL_U, :] = jnp.zeros((_TAIL_U, D_CONV), jnp.float32)
            xext[0:_TAIL_X, :] = jnp.zeros((_TAIL_X, D_POOL), jnp.float32)
            for h in range(HEADS):
                wmask[h] = _decay_mask(rb, h)

        st_ref[0] = state[...]
        ut_ref[0] = uext[0:_TAIL_U, :]
        xt_ref[0] = xext[0:_TAIL_X, :]

        xp = z_ref[:, 0:256]
        uext[_TAIL_U:_TAIL_U + rb, :] = z_ref[:, 256:512] * _sigmoid(z_ref[:, 512:768])
        xext[_TAIL_X:_TAIL_X + rb, :] = xp

        for r in range(0, rb, sr):
            win = _window_sums(xext, _TAIL_X + r, sr, -1)
            ypre[r:r + sr, :] = win
            acc = jnp.zeros((sr, D_CONV), jnp.float32)
            for k in range(CONV_W):
                acc = acc + uext[pl.ds(_TAIL_U + r - k, sr), :] * cdw_ref[CONV_W - 1 - k:CONV_W - k, :]
            cv[r:r + sr, :] = acc

        yp = ypre[...] / _pool_count(i, rb) - xp
        y_ref[:, 0:256] = (_dot(yp.astype(_MM), wbd_ref[...]) * ps_ref[...]).astype(_MM)
        cn, _ = _ln_fwd(cv[...] + cvec_ref[0:1, :])
        ln = cn * cvec_ref[1:2, :] + cvec_ref[2:3, :]
        sw = ln * _sigmoid(ln)
        y_ref[:, 256:512] = _dot(sw.astype(_MM), wpw_ref[...]).astype(_MM)
        csv = cs_ref[...]
        snv = sn_ref[...]
        for h in range(HEADS):
            q = _rope(z_ref[:, 768 + h * DH:768 + (h + 1) * DH], csv, snv)
            k = _rope(z_ref[:, 1280 + h * DH:1280 + (h + 1) * DH], csv, snv) * (DH ** -0.5)
            vb = z_ref[:, 1792 + h * DH:1792 + (h + 1) * DH].astype(_MM)
            g = z_ref[:, 2304 + h * DH:2304 + (h + 1) * DH]
            a, b = _row_decays(rb, h)
            s = _dot_nt(q.astype(_MM), k.astype(_MM)) * wmask[h]
            o = _dot(s.astype(_MM), vb) + _dot((q * a).astype(_MM), state[h].astype(_MM))
            state[h] = math.exp(LOG_GAMMA[h] * rb) * state[h] + _dot_tn((k * b).astype(_MM), vb)
            on, _ = _ln_fwd(o)
            y_ref[:, 512 + h * DH:512 + (h + 1) * DH] = (
                g * _sigmoid(g) * on * gn_ref[:, h * DH:(h + 1) * DH]).astype(_MM)

        uext[0:_TAIL_U, :] = uext[rb:rb + _TAIL_U, :]
        xext[0:_TAIL_X, :] = xext[rb:rb + _TAIL_X, :]

    return pl.pallas_call(
        body, name=name, grid=(nblk,),
        in_specs=[_rows(rb, D_IN), _rows(rb, DH), _rows(rb, DH), _full((256, 256)), _full((1, 256)),
                  _full((32, 256)), _full((8, 256)), _full((256, 256)), _full((1, D_RET))],
        out_specs=[_rows(rb, D),
                   pl.BlockSpec((1, HEADS, DH, DH), lambda i: (i, 0, 0, 0)),
                   pl.BlockSpec((1, _TAIL_U, D_CONV), lambda i: (i, 0, 0)),
                   pl.BlockSpec((1, _TAIL_X, D_POOL), lambda i: (i, 0, 0))],
        out_shape=[jax.ShapeDtypeStruct((t, D), _MM),
                   jax.ShapeDtypeStruct((nblk, HEADS, DH, DH), jnp.float32),
                   jax.ShapeDtypeStruct((nblk, _TAIL_U, D_CONV), jnp.float32),
                   jax.ShapeDtypeStruct((nblk, _TAIL_X, D_POOL), jnp.float32)],
        scratch_shapes=[pltpu.VMEM((rb + _TAIL_U, D_CONV), jnp.float32),
                        pltpu.VMEM((rb + _TAIL_X, D_POOL), jnp.float32),
                        pltpu.VMEM((rb, D_CONV), jnp.float32),
                        pltpu.VMEM((rb, D_POOL), jnp.float32),
                        pltpu.VMEM((HEADS, DH, DH), jnp.float32),
                        pltpu.VMEM((HEADS, rb, rb), jnp.float32)],
        compiler_params=_params(),
    )(z, cs, sn, wbd, pscale, cdw, cvec, wpw, gn)


def _mix_core_bwd(name, z, dyc, cs, sn, st_in, ut_in, xt_in, wbd, pscale, cdw, cvec, wpw, gn):
    t = z.shape[0]
    rb = _row_block(t, _MIX_ROWS)
    nblk = t // rb
    sr = _sub_rows(rb)
    rev = lambda i: nblk - 1 - i

    def body(z_ref, dy_ref, cs_ref, sn_ref, st_ref, ut_ref, xt_ref,
             wbd_ref, ps_ref, cdw_ref, cvec_ref, wpw_ref, gn_ref,
             dz_ref, dwbd_ref, dwpw_ref, dcdw_ref, dsm_ref,
             uext, xext, cv, ypre, dcvext, eext, dstate, wmask):
        i = pl.program_id(0)
        blk = nblk - 1 - i

        @pl.when(i == 0)
        def _():
            dstate[...] = jnp.zeros_like(dstate)
            dcvext[rb:rb + _TAIL_U, :] = jnp.zeros((_TAIL_U, D_CONV), jnp.float32)
            eext[rb:rb + _TAIL_X, :] = jnp.zeros((_TAIL_X, D_POOL), jnp.float32)
            dwbd_ref[...] = jnp.zeros_like(dwbd_ref)
            dwpw_ref[...] = jnp.zeros_like(dwpw_ref)
            dcdw_ref[...] = jnp.zeros_like(dcdw_ref)
            dsm_ref[...] = jnp.zeros_like(dsm_ref)
            for h in range(HEADS):
                wmask[h] = _decay_mask(rb, h)

        row = blk * rb + lax.broadcasted_iota(jnp.int32, (rb, 1), 0)
        live = row >= PAD

        xp = z_ref[:, 0:256]
        ca = z_ref[:, 256:512]
        sg_c = _sigmoid(z_ref[:, 512:768])
        uext[0:_TAIL_U, :] = ut_ref[0]
        xext[0:_TAIL_X, :] = xt_ref[0]
        uext[_TAIL_U:_TAIL_U + rb, :] = ca * sg_c
        xext[_TAIL_X:_TAIL_X + rb, :] = xp
        for r in range(0, rb, sr):
            ypre[r:r + sr, :] = _window_sums(xext, _TAIL_X + r, sr, -1)
            acc = jnp.zeros((sr, D_CONV), jnp.float32)
            for k in range(CONV_W):
                acc = acc + uext[pl.ds(_TAIL_U + r - k, sr), :] * cdw_ref[CONV_W - 1 - k:CONV_W - k, :]
            cv[r:r + sr, :] = acc

        cnt = _pool_count(blk, rb)
        ypb = (ypre[...] / cnt - xp).astype(_MM)
        dyp = dy_ref[:, 0:256]
        pm = _dot(ypb, wbd_ref[...])
        dsm_ref[1:2, 0:256] += jnp.sum(dyp * pm, axis=0, keepdims=True)
        dpm = (dyp * ps_ref[...]).astype(_MM)
        dwbd_ref[...] += _dot_tn(ypb, dpm)
        dypre = _dot_nt(dpm, wbd_ref[...])
        eext[0:rb, :] = dypre / cnt
        for r in range(0, rb, sr):
            ypre[r:r + sr, :] = _window_sums(eext, r, sr, 1)
        dz_ref[:, 0:256] = jnp.where(live, ypre[...] - dypre, 0.0).astype(_MM)

        cn, rstd_c = _ln_fwd(cv[...] + cvec_ref[0:1, :])
        ln = cn * cvec_ref[1:2, :] + cvec_ref[2:3, :]
        sg_l = _sigmoid(ln)
        swb = (ln * sg_l).astype(_MM)
        dycb = dy_ref[:, 256:512].astype(_MM)
        dwpw_ref[...] += _dot_tn(swb, dycb)
        dln = _dot_nt(dycb, wpw_ref[...]) * (sg_l * (1.0 + ln * (1.0 - sg_l)))
        dsm_ref[3:4, 0:256] += jnp.sum(dln * cn, axis=0, keepdims=True)
        dsm_ref[4:5, 0:256] += jnp.sum(dln, axis=0, keepdims=True)
        dcv = _ln_bwd(dln * cvec_ref[1:2, :], cn, rstd_c)
        dsm_ref[2:3, 0:256] += jnp.sum(dcv, axis=0, keepdims=True)
        dcvext[0:rb, :] = dcv
        for k in range(CONV_W):
            prod = dcv * uext[pl.ds(_TAIL_U - k, rb), :]
            dcdw_ref[CONV_W - 1 - k:CONV_W - k, :] += jnp.sum(prod, axis=0, keepdims=True)
        for r in range(0, rb, sr):
            acc = jnp.zeros((sr, D_CONV), jnp.float32)
            for k in range(CONV_W):
                acc = acc + dcvext[pl.ds(r + k, sr), :] * cdw_ref[CONV_W - 1 - k:CONV_W - k, :]
            cv[r:r + sr, :] = acc
        du = cv[...]
        dz_ref[:, 256:512] = jnp.where(live, du * sg_c, 0.0).astype(_MM)
        dz_ref[:, 512:768] = jnp.where(live, du * ca * sg_c * (1.0 - sg_c), 0.0).astype(_MM)

        csv = cs_ref[...]
        snv = sn_ref[...]
        for h in range(HEADS):
            q = _rope(z_ref[:, 768 + h * DH:768 + (h + 1) * DH], csv, snv)
            k = _rope(z_ref[:, 1280 + h * DH:1280 + (h + 1) * DH], csv, snv) * (DH ** -0.5)
            vb = z_ref[:, 1792 + h * DH:1792 + (h + 1) * DH].astype(_MM)
            g = z_ref[:, 2304 + h * DH:2304 + (h + 1) * DH]
            a, b = _row_decays(rb, h)
            qb = q.astype(_MM)
            kb = k.astype(_MM)
            qab = (q * a).astype(_MM)
            kbb = (k * b).astype(_MM)
            stb = st_ref[0, h].astype(_MM)
            sb = (_dot_nt(qb, kb) * wmask[h]).astype(_MM)
            o = _dot(sb, vb) + _dot(qab, stb)
            on, rstd_o = _ln_fwd(o)
            gnv = gn_ref[:, h * DH:(h + 1) * DH]
            sg_g = _sigmoid(g)
            si_g = g * sg_g
            dyr = dy_ref[:, 512 + h * DH:512 + (h + 1) * DH]
            dsm_ref[0:1, h * DH:(h + 1) * DH] += jnp.sum(dyr * on * si_g, axis=0, keepdims=True)
            dgate = dyr * on * gnv * (sg_g * (1.0 + g * (1.0 - sg_g)))
            dob = _ln_bwd(dyr * gnv * si_g, on, rstd_o).astype(_MM)
            dstb = dstate[h].astype(_MM)
            dsb = (_dot_nt(dob, vb) * wmask[h]).astype(_MM)
            dq = _dot(dsb, kb) + _dot_nt(dob, stb) * a
            dk = _dot_tn(dsb, qb) + _dot_nt(vb, dstb) * b
            dv = _dot_tn(sb, dob) + _dot(kbb, dstb)
            dstate[h] = math.exp(LOG_GAMMA[h] * rb) * dstate[h] + _dot_tn(qab, dob)
            dz_ref[:, 768 + h * DH:768 + (h + 1) * DH] = jnp.where(live, _rope_t(dq, csv, snv), 0.0).astype(_MM)
            dz_ref[:, 1280 + h * DH:1280 + (h + 1) * DH] = jnp.where(
                live, _rope_t(dk * (DH ** -0.5), csv, snv), 0.0).astype(_MM)
            dz_ref[:, 1792 + h * DH:1792 + (h + 1) * DH] = jnp.where(live, dv, 0.0).astype(_MM)
            dz_ref[:, 2304 + h * DH:2304 + (h + 1) * DH] = jnp.where(live, dgate, 0.0).astype(_MM)

        dcvext[rb:rb + _TAIL_U, :] = dcvext[0:_TAIL_U, :]
        eext[rb:rb + _TAIL_X, :] = eext[0:_TAIL_X, :]

    rrows = lambda n: pl.BlockSpec((rb, n), lambda i: (rev(i), 0))
    return pl.pallas_call(
        body, name=name, grid=(nblk,),
        in_specs=[rrows(D_IN), rrows(D), rrows(DH), rrows(DH),
                  pl.BlockSpec((1, HEADS, DH, DH), lambda i: (rev(i), 0, 0, 0)),
                  pl.BlockSpec((1, _TAIL_U, D_CONV), lambda i: (rev(i), 0, 0)),
                  pl.BlockSpec((1, _TAIL_X, D_POOL), lambda i: (rev(i), 0, 0)),
                  _full((256, 256)), _full((1, 256)), _full((32, 256)), _full((8, 256)), _full((256, 256)),
                  _full((1, D_RET))],
        out_specs=[rrows(D_IN), _acc((256, 256)), _acc((256, 256)), _acc((32, 256)), _acc((8, 512))],
        out_shape=[jax.ShapeDtypeStruct((t, D_IN), _MM),
                   jax.ShapeDtypeStruct((256, 256), jnp.float32), jax.ShapeDtypeStruct((256, 256), jnp.float32),
                   jax.ShapeDtypeStruct((32, 256), jnp.float32), jax.ShapeDtypeStruct((8, 512), jnp.float32)],
        scratch_shapes=[pltpu.VMEM((rb + _TAIL_U, D_CONV), jnp.float32),
                        pltpu.VMEM((rb + _TAIL_X, D_POOL), jnp.float32),
                        pltpu.VMEM((rb, D_CONV), jnp.float32),
                        pltpu.VMEM((rb, D_POOL), jnp.float32),
                        pltpu.VMEM((rb + _TAIL_U, D_CONV), jnp.float32),
                        pltpu.VMEM((rb + _TAIL_X, D_POOL), jnp.float32),
                        pltpu.VMEM((HEADS, DH, DH), jnp.float32),
                        pltpu.VMEM((HEADS, rb, rb), jnp.float32)],
        compiler_params=_params(),
    )(z, dyc, cs, sn, st_in, ut_in, xt_in, wbd, pscale, cdw, cvec, wpw, gn)


def _me():
    return lax.axis_index("x"), lax.axis_index("y"), lax.axis_index("c")


def _flip(me, mask):
    return tuple(1 - m if f else m for m, f in zip(me, mask))


def _push(name, aliased, inputs, fresh, remote):
    n_al, n_in, n_out, n_rem = len(aliased), len(inputs), len(fresh), len(remote)

    def body(*refs):
        ins = refs[n_al:n_al + n_in]
        al = refs[n_al + n_in:2 * n_al + n_in]
        outs = refs[2 * n_al + n_in:2 * n_al + n_in + n_out]
        send_sems, recv_sems = refs[2 * n_al + n_in + n_out:]
        me = _me()
        copies = []
        for k, (mask, src_fn, dst_fn) in enumerate(remote):
            cp = pltpu.make_async_remote_copy(
                src_ref=src_fn(al, ins, outs, me), dst_ref=dst_fn(al, ins, outs, me),
                send_sem=send_sems.at[k], recv_sem=recv_sems.at[k],
                device_id=_flip(me, mask), device_id_type=MESH)
            cp.start()
            copies.append(cp)
        for cp in copies:
            cp.wait()

    return pl.pallas_call(
        body, name=name,
        in_specs=[_ANY] * (n_al + n_in), out_specs=[_ANY] * (n_al + n_out),
        out_shape=[jax.ShapeDtypeStruct(a.shape, a.dtype) for a in aliased] + list(fresh),
        input_output_aliases={i: i for i in range(n_al)},
        scratch_shapes=[pltpu.SemaphoreType.DMA((n_rem,)), pltpu.SemaphoreType.DMA((n_rem,))],
    )(*aliased, *inputs)


_HBM = pl.BlockSpec(memory_space=pltpu.HBM)
_SEM = pl.BlockSpec(memory_space=pltpu.SEMAPHORE)
_EFFECT = pltpu.SideEffectType.DATAFLOW_SIDE_EFFECTING


def _push_start(name, bufs, remote):
    n, n_rem = len(bufs), len(remote)

    def body(*refs):
        ins = refs[:n]
        send_sems, recv_sems = refs[n], refs[n + 1]
        token = refs[2 * n + 2]
        me = _me()
        for k, (mask, src_fn, dst_fn) in enumerate(remote):
            pltpu.make_async_remote_copy(
                src_ref=src_fn(ins, me), dst_ref=dst_fn(ins, me),
                send_sem=send_sems.at[k], recv_sem=recv_sems.at[k],
                device_id=_flip(me, mask), device_id_type=MESH).start()
        token[...] = jnp.zeros_like(token)

    out = pl.pallas_call(
        body, name=name,
        out_shape=(pltpu.SemaphoreType.DMA((n_rem,)), pltpu.SemaphoreType.DMA((n_rem,)),
                   *[pltpu.HBM(b.shape, b.dtype) for b in bufs], jax.ShapeDtypeStruct((8, 128), jnp.float32)),
        in_specs=[_HBM] * n,
        out_specs=(_SEM, _SEM, *[_HBM] * n, pl.BlockSpec(memory_space=pltpu.VMEM)),
        input_output_aliases={i: i + 2 for i in range(n)},
        compiler_params=pltpu.CompilerParams(has_side_effects=_EFFECT),
    )(*[pltpu.with_memory_space_constraint(b, pltpu.HBM) for b in bufs])
    return out[0], out[1], list(out[2:2 + n]), out[2 + n]


def _push_wait(name, send_sems, recv_sems, bufs, after, remote):
    n = len(bufs)

    def body(*refs):
        ins = refs[:n]
        s_sems, r_sems = refs[n], refs[n + 1]
        me = _me()
        for k, (mask, src_fn, dst_fn) in enumerate(remote):
            cp = pltpu.make_async_remote_copy(
                src_ref=src_fn(ins, me), dst_ref=dst_fn(ins, me),
                send_sem=s_sems.at[k], recv_sem=r_sems.at[k],
                device_id=_flip(me, mask), device_id_type=MESH)
            cp.wait_send()
            cp.wait_recv()

    out = pl.pallas_call(
        body, name=name,
        out_shape=tuple(pltpu.HBM(b.shape, b.dtype) for b in bufs),
        in_specs=[_HBM] * n + [_SEM, _SEM, _ANY], out_specs=tuple([_HBM] * n),
        input_output_aliases={i: i for i in range(n)},
        compiler_params=pltpu.CompilerParams(has_side_effects=_EFFECT),
    )(*bufs, send_sems, recv_sems, after)
    return list(out)


_ICI_MASKS = ((0, 1, 0), (1, 0, 0), (1, 1, 0))
_D2D_MASK = (0, 0, 1)
_ALL_MASKS = tuple((a, b, c) for a in (0, 1) for b in (0, 1) for c in (0, 1))[1:]


def _chip(me):
    return 2 * me[0] + me[1]


def _half(me):
    return pl.ds(me[2] * HALF_ROWS, HALF_ROWS)


def _other_half(me):
    return pl.ds((1 - me[2]) * HALF_ROWS, HALF_ROWS)


def _own_slot(mine):
    chip = _chip(_me())
    return lax.dynamic_update_slice(lax.empty((N_SHARD,) + mine.shape, mine.dtype), mine[None],
                                    (chip,) + (0,) * mine.ndim)


def _gather_ici_plan(with_small):
    remote = []
    for mask in _ICI_MASKS:
        mine = lambda bufs, me: bufs[0].at[_chip(me), _half(me)]
        remote.append((mask, mine, mine))
        if with_small:
            mine_small = lambda bufs, me: bufs[1].at[_chip(me)]
            remote.append((mask, mine_small, mine_small))
    return remote


def _gather_d2d(name, w):
    remote = []
    for j in range(1, N_SHARD):
        theirs = lambda al, ins, outs, me, j=j: al[0].at[(_chip(me) + j) % N_SHARD, _half(me)]
        remote.append((_D2D_MASK, theirs, theirs))
    (w,) = _push(name, [w], [], [], remote)
    return w


def _sum_pair(name, g, recv):
    _, r, _ = g.shape
    rb = _row_block(r // 2)
    nb = r // 2 // rb
    c = lax.axis_index("c").astype(jnp.int32).reshape(1)

    def body(c_ref, g_ref, r_ref, o_ref):
        o_ref[...] = (g_ref[...].astype(jnp.float32) + r_ref[...].astype(jnp.float32)).astype(o_ref.dtype)

    return pl.pallas_call(
        body, name=name,
        grid_spec=pltpu.PrefetchScalarGridSpec(
            num_scalar_prefetch=1, grid=(N_SHARD, nb),
            in_specs=[pl.BlockSpec((None, rb, D), lambda s, i, c_ref: (s, c_ref[0] * nb + i, 0)),
                      pl.BlockSpec((None, rb, D), lambda s, i, c_ref: (s, i, 0))],
            out_specs=pl.BlockSpec((None, rb, D), lambda s, i, c_ref: (s, i, 0))),
        out_shape=jax.ShapeDtypeStruct((N_SHARD, r // 2, D), g.dtype),
        compiler_params=_params(("arbitrary", "arbitrary")),
    )(c, g, recv)


def _sum_chips(name, p, recv):
    _, rh, _ = p.shape
    rb = _row_block(rh)
    nb = rh // rb
    s = jnp.stack([2 * lax.axis_index("x") + lax.axis_index("y"), lax.axis_index("c")]).astype(jnp.int32)

    def body(s_ref, p_ref, r_ref, o_ref):
        acc = p_ref[...].astype(jnp.float32)
        for j in range(3):
            acc = acc + r_ref[j].astype(jnp.float32)
        o_ref[...] = acc

    return pl.pallas_call(
        body, name=name,
        grid_spec=pltpu.PrefetchScalarGridSpec(
            num_scalar_prefetch=1, grid=(nb,),
            in_specs=[pl.BlockSpec((None, rb, D), lambda i, s_ref: (s_ref[0], i, 0)),
                      pl.BlockSpec((3, rb, D), lambda i, s_ref: (0, i, 0))],
            out_specs=pl.BlockSpec((rb, D), lambda i, s_ref: (s_ref[1] * nb + i, 0))),
        out_shape=jax.ShapeDtypeStruct((2 * rh, D), jnp.float32),
        compiler_params=_params(),
    )(s, p, recv)


def _rs_ici_plan():
    remote = []
    for j, mask in enumerate(_ICI_MASKS):
        remote.append((mask,
                       lambda bufs, me, mask=mask: bufs[0].at[_chip(_flip(me, mask))],
                       lambda bufs, me, j=j: bufs[1].at[j]))
    return remote


def _rs_pair(tag, g):
    _, r, _ = g.shape
    remote = [(_D2D_MASK,
               lambda al, ins, outs, me, s=s: ins[0].at[s, _other_half(me)],
               lambda al, ins, outs, me, s=s: outs[0].at[s]) for s in range(N_SHARD)]
    (recv,) = _push("rs_d2d_" + tag, [], [g], [jax.ShapeDtypeStruct((N_SHARD, r // 2, D), g.dtype)], remote)
    return _sum_pair("rs_sum_pair_" + tag, g, recv)


def _rs_finish(tag, p, recv3):
    mine = _sum_chips("rs_sum_chips_" + tag, p, recv3)
    half = lambda al, ins, outs, me: al[0].at[_half(me)]
    (both,) = _push("rs_share_" + tag, [mine], [], [], [(_D2D_MASK, half, half)])
    return both


def _all_reduce_small(v):
    s = v.shape[0]
    me = _me()
    every = lax.dynamic_update_slice(lax.empty((8, s, D), jnp.float32), v[None], (4 * me[0] + 2 * me[1] + me[2], 0, 0))
    slot = lambda al, ins, outs, me: al[0].at[4 * me[0] + 2 * me[1] + me[2]]
    (every,) = _push("small_all", [every], [], [], [(mask, slot, slot) for mask in _ALL_MASKS])

    def body(e_ref, o_ref):
        acc = e_ref[0]
        for j in range(1, 8):
            acc = acc + e_ref[j]
        o_ref[...] = acc

    return pl.pallas_call(
        body, name="small_sum", grid=(1,),
        in_specs=[pl.BlockSpec((8, s, D), lambda i: (0, 0, 0))],
        out_specs=pl.BlockSpec((s, D), lambda i: (0, 0)),
        out_shape=jax.ShapeDtypeStruct((s, D), jnp.float32),
        compiler_params=_params(),
    )(every)


def _adamw(name, w, g, m, v):
    r, c = w.shape
    rb = next(b for b in (256, 344, 128, 64, 32, 16, 8, r) if r % b == 0)

    def body(w_ref, g_ref, m_ref, v_ref, d_ref, mo_ref, vo_ref):
        g = g_ref[...]
        m = ADAM_B1 * m_ref[...] + (1.0 - ADAM_B1) * g
        v = ADAM_B2 * v_ref[...] + (1.0 - ADAM_B2) * (g * g)
        m_hat = m / (1.0 - ADAM_B1 ** ADAM_STEP)
        v_hat = v / (1.0 - ADAM_B2 ** ADAM_STEP)
        d_ref[...] = -ADAM_LR * (m_hat / (jnp.sqrt(v_hat) + ADAM_EPS) + ADAM_WD * w_ref[...])
        mo_ref[...] = m
        vo_ref[...] = v

    spec = pl.BlockSpec((rb, c), lambda i: (i, 0))
    return pl.pallas_call(
        body, name=name, grid=(r // rb,),
        in_specs=[spec] * 4, out_specs=[spec] * 3,
        out_shape=[jax.ShapeDtypeStruct((r, c), jnp.float32)] * 3,
        compiler_params=_params(),
    )(w, g, m, v)


_BIG = ("ffn1_w13", "ffn2_w13", "ffn1_w2", "ffn2_w2", "w_in", "w_out", "conv_pw")
_BIG_SHARD = {"ffn1_w13": (D, 1376), "ffn2_w13": (D, 1376), "ffn1_w2": (688, D), "ffn2_w2": (688, D),
              "w_in": (D, 704), "w_out": (256, D), "conv_pw": (64, 256)}


def _pack_rows(parts):
    flat = jnp.concatenate([p.reshape(-1) for p in parts])
    pad = (-flat.shape[0]) % (8 * D)
    if pad:
        flat = jnp.concatenate([flat, jnp.zeros((pad,), flat.dtype)])
    return flat.reshape(-1, D)


def _unpack_rows(buf, shapes):
    flat = buf.reshape(-1)
    out, off = [], 0
    for shp in shapes:
        n = math.prod(shp)
        out.append(flat[off:off + n].reshape(shp))
        off += n
    return out


def _pack_big_half(parts):
    pw = parts["conv_pw"].reshape(PW_ROWS // 2, D)
    return jnp.concatenate([parts["ffn1_w13"].T, parts["ffn2_w13"].T, parts["ffn1_w2"], parts["ffn2_w2"],
                            parts["w_in"].T, parts["w_out"], pw, jnp.zeros_like(pw)], axis=0)


def _unpack_big_half(buf):
    return {"ffn1_w13": buf[OFF_W13[0]:OFF_W13[0] + W13_ROWS].T, "ffn2_w13": buf[OFF_W13[1]:OFF_W13[1] + W13_ROWS].T,
            "ffn1_w2": buf[OFF_W2[0]:OFF_W2[0] + W2_ROWS], "ffn2_w2": buf[OFF_W2[1]:OFF_W2[1] + W2_ROWS],
            "w_in": buf[OFF_WIN:OFF_WIN + WIN_ROWS].T, "w_out": buf[OFF_WOUT:OFF_WOUT + WOUT_ROWS],
            "conv_pw": buf[OFF_PW:OFF_PW + PW_ROWS // 2].reshape(64, 256)}


def kernel(x, meta, ln_in_g, ln_in_b, ffn1_w13, ffn1_w2, w_in, pool_w, pool_scale, conv_dw, conv_db, conv_ln_g, conv_ln_b, conv_pw, ret_gn_g, w_out, ffn2_w13, ffn2_w2, ln_g, ln_b, loss_target, m_meta, m_ln_in_g, m_ln_in_b, m_ffn1_w13, m_ffn1_w2, m_w_in, m_pool_w, m_pool_scale, m_conv_dw, m_conv_db, m_conv_ln_g, m_conv_ln_b, m_conv_pw, m_ret_gn_g, m_w_out, m_ffn2_w13, m_ffn2_w2, m_ln_g, m_ln_b, v_meta, v_ln_in_g, v_ln_in_b, v_ffn1_w13, v_ffn1_w2, v_w_in, v_pool_w, v_pool_scale, v_conv_dw, v_conv_db, v_conv_ln_g, v_conv_ln_b, v_conv_pw, v_ret_gn_g, v_w_out, v_ffn2_w13, v_ffn2_w2, v_ln_g, v_ln_b):
    f32 = jnp.float32
    seq = x.shape[1]
    t = seq + ROW0
    me = _me()
    chip = _chip(me)
    big_w = {"ffn1_w13": ffn1_w13, "ffn2_w13": ffn2_w13, "ffn1_w2": ffn1_w2, "ffn2_w2": ffn2_w2,
             "w_in": w_in, "w_out": w_out, "conv_pw": conv_pw}

    wl = [_own_slot(_pack_big_half({n: big_w[n][l].astype(_WIRE) for n in _BIG})) for l in range(DEPTH)]
    small_shapes = [(N_META, 256), (DEPTH, CONV_W, 64), (DEPTH, 3, 256), (DEPTH, 3, 256)]
    small_all = _own_slot(_pack_rows([meta, conv_dw, ln_g, ln_b]))
    wrap = lambda f: (lambda al, ins, outs, me: f(al, me))
    wl[0], small_all = _push("gather_ici_0", [wl[0], small_all], [], [],
                             [(m, wrap(s), wrap(d)) for m, s, d in _gather_ici_plan(True)])
    wl[0] = _gather_d2d("gather_d2d_0", wl[0])
    g1_send, g1_recv, (w1_flying,), g1_token = _push_start("gather_ici_1_start", [wl[1]], _gather_ici_plan(False))

    sm = [_unpack_rows(small_all[s], small_shapes) for s in range(N_SHARD)]
    meta_f = jnp.concatenate([sm[s][0] for s in range(N_SHARD)], axis=1)
    cdw_f = jnp.concatenate([sm[s][1] for s in range(N_SHARD)], axis=2)
    lng_f = jnp.concatenate([sm[s][2] for s in range(N_SHARD)], axis=2)
    lnb_f = jnp.concatenate([sm[s][3] for s in range(N_SHARD)], axis=2)

    def mix_params(l):
        wbd = jnp.zeros((D_POOL, D_POOL), f32)
        for g in range(4):
            wbd = wbd.at[64 * g:64 * (g + 1), 64 * g:64 * (g + 1)].set(pool_w[l, g])
        cdw = jnp.pad(cdw_f[l], ((0, 1), (0, 0)))
        cvec = jnp.pad(jnp.stack([conv_db[l], conv_ln_g[l], conv_ln_b[l]]), ((0, 5), (0, 0)))
        wpw = wl[l][:, OFF_PW:OFF_PW + PW_ROWS // 2].reshape(D_CONV, D_CONV)
        return (wbd.astype(_MM), pool_scale[l][None], cdw, cvec, wpw, ret_gn_g[l][None])

    gb_of = lambda l, i: jnp.stack([lng_f[l, i], lnb_f[l, i]])
    gb_in = jnp.stack([ln_in_g, ln_in_b])

    pos = jnp.arange(t, dtype=f32) - PAD
    inv_freq = ROPE_BASE ** (-jnp.arange(0, DH, 2, dtype=f32) / DH)
    ang = pos[:, None] * inv_freq[None, :]
    cs = jnp.concatenate([jnp.cos(ang), jnp.cos(ang)], axis=1)
    sn = jnp.concatenate([-jnp.sin(ang), jnp.sin(ang)], axis=1)

    raw = jnp.concatenate([jnp.zeros((PAD, D), f32), meta_f, x[0]], axis=0)
    target = jnp.concatenate([jnp.zeros((ROW0, D), f32), loss_target[0]], axis=0)
    xh, rstd = _ln_in_fwd(raw)
    cur = (xh, rstd, gb_in + g1_token[0, 0])
    saved = []
    for l in range(DEPTH):
        if l == 1:
            (w1_landed,) = _push_wait("gather_ici_1_wait", g1_send, g1_recv, [w1_flying], cur[0],
                                      _gather_ici_plan(False))
            wl[1] = _gather_d2d("gather_d2d_1", w1_landed)
        a0 = cur
        xh1, r1, au1, hb1 = _ffn_fwd(f"ffn1_fwd_{l}", a0[0], a0[2], wl[l], 0)
        a1 = (xh1, r1, gb_of(l, 0))
        z = _mix_in_fwd(f"mix_in_fwd_{l}", a1[0], a1[2], wl[l])
        mp = mix_params(l)
        ycat, st_in, ut_in, xt_in = _mix_core_fwd(f"mix_core_fwd_{l}", z, cs, sn, *mp)
        xh2, r2 = _mix_out_fwd(f"mix_out_fwd_{l}", a1[0], a1[2], ycat, wl[l])
        a2 = (xh2, r2, gb_of(l, 1))
        xh3, r3, au2, hb2 = _ffn_fwd(f"ffn2_fwd_{l}", a2[0], a2[2], wl[l], 1)
        a3 = (xh3, r3, gb_of(l, 2))
        saved.append((a0, a1, a2, a3, z, ycat, st_in, ut_in, xt_in, mp, au1, hb1, au2, hb2))
        cur = a3

    dy, loss_part = _loss_fwd_bwd(cur[0], cur[2], target)
    loss = lax.psum(loss_part[0, 0], ("x", "y", "c"))

    g_ln_g = [[None] * 3 for _ in range(DEPTH)]
    g_ln_b = [[None] * 3 for _ in range(DEPTH)]
    g_small = [dict() for _ in range(DEPTH)]
    slot = lambda j: j

    def ffn_grads(gbuf, tag, l, f, hb, hid, dau, dffn):
        gbuf = _dw_into(f"dw13_{tag}_{l}", gbuf, dau, hb, FF_SLOT, [(0, W13_ROWS, slot, OFF_W13[f])])
        return _dw_into(f"dw2_{tag}_{l}", gbuf, hid, dffn, FF_SLOT,
                        [(0, W2_ROWS, lambda j: 2 * j, OFF_W2[f]), (W2_ROWS, W2_ROWS, lambda j: 2 * j + 1, OFF_W2[f])])

    rs_plan = _rs_ici_plan()
    token = jnp.zeros((), f32)
    for l in reversed(range(DEPTH)):
        a0, a1, a2, a3, z, ycat, st_in, ut_in, xt_in, mp, au1, hb1, au2, hb2 = saved[l]
        gbuf = lax.empty((N_SHARD, SHARD_ROWS, D), _WIRE)
        dh, hid, dau, dffn, dgb = _ffn_bwd(f"ffn2_bwd_{l}", dy, a3[0], a3[1], a3[2] + token, au2, wl[l], 1)
        g_ln_g[l][2], g_ln_b[l][2] = dgb[0], dgb[1]
        gbuf = ffn_grads(gbuf, "ffn2", l, 1, hb2, hid, dau, dffn)
        dh_res, dycat, dsb, dgb = _mix_out_bwd(f"mix_out_bwd_{l}", dh, a2[0], a2[1], a2[2], wl[l])
        g_ln_g[l][1], g_ln_b[l][1] = dgb[0], dgb[1]
        gbuf = _dw_into(f"dw_out_{l}", gbuf, ycat, dsb, D,
                        [(WOUT_ROWS * s, WOUT_ROWS, lambda j, s=s: s, OFF_WOUT) for s in range(N_SHARD)])
        dz, dwbd, dwpw, dcdw, dsm = _mix_core_bwd(f"mix_core_bwd_{l}", z, dycat, cs, sn, st_in, ut_in, xt_in, *mp)
        pw = dwpw.astype(_WIRE).reshape(N_SHARD, PW_ROWS // 2, D)
        gbuf = lax.dynamic_update_slice(gbuf, jnp.concatenate([pw, jnp.zeros_like(pw)], axis=1), (0, OFF_PW, 0))
        g_small[l] = dict(
            pool_w=jnp.stack([dwbd[64 * g:64 * (g + 1), 64 * g:64 * (g + 1)] for g in range(4)]),
            pool_scale=dsm[1, :256], conv_db=dsm[2, :256], conv_ln_g=dsm[3, :256], conv_ln_b=dsm[4, :256],
            ret_gn_g=dsm[0], conv_dw=dcdw[:CONV_W])
        dh, hb = _mix_in_bwd(f"mix_in_bwd_{l}", dh_res, dz, a1[0], a1[2], wl[l])
        gbuf = _dw_into(f"dw_in_{l}", gbuf, dz, hb, D_IN,
                        [(WIN_ROWS * s, WIN_ROWS, lambda j, s=s: s, OFF_WIN) for s in range(N_SHARD)])
        dh, hid, dau, dffn, dgb = _ffn_bwd(f"ffn1_bwd_{l}", dh, a1[0], a1[1], a1[2], au1, wl[l], 0)
        g_ln_g[l][0], g_ln_b[l][0] = dgb[0], dgb[1]
        gbuf = ffn_grads(gbuf, "ffn1", l, 0, hb1, hid, dau, dffn)
        dy = dh
        p = _rs_pair(str(l), gbuf)
        landing = lax.empty((3, HALF_ROWS, D), _WIRE)
        if l == DEPTH - 1:
            rs_send, rs_recv, flying, rs_token = _push_start("rs_ici_1_start", [p, landing], rs_plan)
            token = rs_token[0, 0]
        else:
            (recv0,) = _push("rs_ici_0", [], [p], [jax.ShapeDtypeStruct(landing.shape, landing.dtype)],
                             [(m, lambda al, ins, outs, me, s=s: s([ins[0], outs[0]], me),
                               lambda al, ins, outs, me, d=d: d([ins[0], outs[0]], me)) for m, s, d in rs_plan])
            g_layer0 = _rs_finish("0", p, recv0)
    p1, recv1 = _push_wait("rs_ici_1_wait", rs_send, rs_recv, flying, dy, rs_plan)
    gsum = [g_layer0, _rs_finish("1", p1, recv1)]
    d_raw, dgb_in = _ln_in_bwd(dy, saved[0][0][0], saved[0][0][1], gb_in)
    grad_x = d_raw[ROW0:][None]

    g_big = [_unpack_big_half(gsum[l]) for l in range(DEPTH)]
    grads = {n: jnp.stack([g_big[l][n] for l in range(DEPTH)]) for n in _BIG}

    small_parts = [
        d_raw[PAD:ROW0],
        jnp.stack([g_small[l]["conv_dw"] for l in range(DEPTH)]),
        jnp.stack([jnp.stack(g_ln_g[l]) for l in range(DEPTH)]),
        jnp.stack([jnp.stack(g_ln_b[l]) for l in range(DEPTH)]),
        dgb_in[0], dgb_in[1],
        jnp.stack([g_small[l]["pool_w"] for l in range(DEPTH)]),
        jnp.stack([g_small[l]["pool_scale"] for l in range(DEPTH)]),
        jnp.stack([g_small[l]["conv_db"] for l in range(DEPTH)]),
        jnp.stack([g_small[l]["conv_ln_g"] for l in range(DEPTH)]),
        jnp.stack([g_small[l]["conv_ln_b"] for l in range(DEPTH)]),
        jnp.stack([g_small[l]["ret_gn_g"] for l in range(DEPTH)]),
    ]
    red = _unpack_rows(_all_reduce_small(_pack_rows(small_parts)), [p.shape for p in small_parts])
    grads["meta"] = lax.dynamic_slice_in_dim(red[0], 256 * chip, 256, axis=1)
    grads["conv_dw"] = lax.dynamic_slice_in_dim(red[1], 64 * chip, 64, axis=2)
    grads["ln_g"] = lax.dynamic_slice_in_dim(red[2], 256 * chip, 256, axis=2)
    grads["ln_b"] = lax.dynamic_slice_in_dim(red[3], 256 * chip, 256, axis=2)
    for n, v in zip(("ln_in_g", "ln_in_b", "pool_w", "pool_scale", "conv_db", "conv_ln_g", "conv_ln_b", "ret_gn_g"),
                    red[4:]):
        grads[n] = v

    names = ['meta', 'ln_in_g', 'ln_in_b', 'ffn1_w13', 'ffn1_w2', 'w_in', 'pool_w', 'pool_scale', 'conv_dw',
             'conv_db', 'conv_ln_g', 'conv_ln_b', 'conv_pw', 'ret_gn_g', 'w_out', 'ffn2_w13', 'ffn2_w2', 'ln_g', 'ln_b']
    ws = dict(meta=meta, ln_in_g=ln_in_g, ln_in_b=ln_in_b, ffn1_w13=ffn1_w13, ffn1_w2=ffn1_w2, w_in=w_in,
              pool_w=pool_w, pool_scale=pool_scale, conv_dw=conv_dw, conv_db=conv_db, conv_ln_g=conv_ln_g,
              conv_ln_b=conv_ln_b, conv_pw=conv_pw, ret_gn_g=ret_gn_g, w_out=w_out, ffn2_w13=ffn2_w13,
              ffn2_w2=ffn2_w2, ln_g=ln_g, ln_b=ln_b)
    ms = dict(meta=m_meta, ln_in_g=m_ln_in_g, ln_in_b=m_ln_in_b, ffn1_w13=m_ffn1_w13, ffn1_w2=m_ffn1_w2,
              w_in=m_w_in, pool_w=m_pool_w, pool_scale=m_pool_scale, conv_dw=m_conv_dw, conv_db=m_conv_db,
              conv_ln_g=m_conv_ln_g, conv_ln_b=m_conv_ln_b, conv_pw=m_conv_pw, ret_gn_g=m_ret_gn_g,
              w_out=m_w_out, ffn2_w13=m_ffn2_w13, ffn2_w2=m_ffn2_w2, ln_g=m_ln_g, ln_b=m_ln_b)
    vs = dict(meta=v_meta, ln_in_g=v_ln_in_g, ln_in_b=v_ln_in_b, ffn1_w13=v_ffn1_w13, ffn1_w2=v_ffn1_w2,
              w_in=v_w_in, pool_w=v_pool_w, pool_scale=v_pool_scale, conv_dw=v_conv_dw, conv_db=v_conv_db,
              conv_ln_g=v_conv_ln_g, conv_ln_b=v_conv_ln_b, conv_pw=v_conv_pw, ret_gn_g=v_ret_gn_g,
              w_out=v_w_out, ffn2_w13=v_ffn2_w13, ffn2_w2=v_ffn2_w2, ln_g=v_ln_g, ln_b=v_ln_b)
    delta, new_m, new_v = {}, {}, {}
    for n in _BIG:
        shp = ws[n].shape
        two = lambda a: a.reshape(-1, shp[-1])
        d_, m_, v_ = _adamw("adamw_" + n, two(ws[n]), two(grads[n]), two(ms[n]), two(vs[n]))
        delta[n], new_m[n], new_v[n] = d_.reshape(shp), m_.reshape(shp), v_.reshape(shp)
    small_names = [n for n in names if n not in _BIG]
    pk = lambda d: _pack_rows([d[n] for n in small_names])
    d_, m_, v_ = _adamw("adamw_small", pk(ws), pk(grads), pk(ms), pk(vs))
    shapes = [ws[n].shape for n in small_names]
    for n, a, b, c in zip(small_names, _unpack_rows(d_, shapes), _unpack_rows(m_, shapes), _unpack_rows(v_, shapes)):
        delta[n], new_m[n], new_v[n] = a, b, c

    return (loss, grad_x, *[grads[n] for n in names], *[delta[n] for n in names],
            *[new_m[n] for n in names], *[new_v[n] for n in names])
```

```python
import functools
import math

import jax
import jax.numpy as jnp
from jax import lax
from jax.experimental import pallas as pl
from jax.experimental.pallas import tpu as pltpu

D = 1024
DEPTH = 2
N_META = 16
PAD = 112
ROW0 = PAD + N_META
D_POOL = 256
D_CONV = 256
D_RET = 512
HEADS = 4
DH = 128
CONV_W = 31
D_FF = 2752
FF_SLOT = 1408
D_FFP = 2 * FF_SLOT
D_IN = 2816
N_SHARD = 4
ALPHA = (2.0 * DEPTH) ** 0.25
LN_EPS = 1e-5
ROPE_BASE = 10000.0
LOG_GAMMA = tuple(math.log(1.0 - 2.0 ** (-5.0 - h)) for h in range(HEADS))
ADAM_LR, ADAM_B1, ADAM_B2, ADAM_EPS, ADAM_WD, ADAM_STEP = 0.001, 0.9, 0.999, 1e-08, 0.01, 10

_MM = jnp.bfloat16
_WIRE = jnp.bfloat16
_VMEM_LIMIT = 56 * 1024 * 1024
_FFN_ROWS = 640

MESH = pl.DeviceIdType.MESH
_ANY = pl.BlockSpec(memory_space=pl.ANY)

W13_ROWS = 1376
W2_ROWS = 688
WIN_ROWS = 704
WOUT_ROWS = 256
PW_ROWS = 32
OFF_W13 = (0, W13_ROWS)
OFF_W2 = (2 * W13_ROWS, 2 * W13_ROWS + W2_ROWS)
OFF_WIN = 2 * W13_ROWS + 2 * W2_ROWS
OFF_WOUT = OFF_WIN + WIN_ROWS
OFF_PW = OFF_WOUT + WOUT_ROWS
SHARD_ROWS = OFF_PW + PW_ROWS
HALF_ROWS = SHARD_ROWS // 2
W2_SLOT_OFF = (0, W2_ROWS, FF_SLOT, FF_SLOT + W2_ROWS)


def _start_rows(w_ref, off, n, dst_of, sems, k0):
    cps = []
    for s in range(N_SHARD):
        cp = pltpu.make_async_copy(w_ref.at[s, pl.ds(off, n)], dst_of(s), sems.at[k0 + s])
        cp.start()
        cps.append(cp)
    return cps


def _load_ffn_weights(w_ref, f, w13, w2, sems):
    cps = _start_rows(w_ref, OFF_W13[f], W13_ROWS, lambda s: w13.at[s, pl.ds(0, W13_ROWS)], sems, 0)
    cps += _start_rows(w_ref, OFF_W2[f], W2_ROWS, lambda s: w2.at[pl.ds(W2_SLOT_OFF[s], W2_ROWS)], sems, 4)
    zpad = jnp.zeros((FF_SLOT - W13_ROWS, D), w13.dtype)
    for s in range(N_SHARD):
        w13[s, W13_ROWS:FF_SLOT, :] = zpad
    w2[W13_ROWS:FF_SLOT, :] = zpad
    w2[FF_SLOT + W13_ROWS:D_FFP, :] = zpad
    for cp in cps:
        cp.wait()


def _load_rows(w_ref, off, n, dst, sems):
    for cp in _start_rows(w_ref, off, n, lambda s: dst.at[pl.ds(s * n, n)], sems, 0):
        cp.wait()


def _dot(a, b):
    return jnp.dot(a, b, preferred_element_type=jnp.float32)


def _dot_nt(a, b):
    return lax.dot_general(a, b, (((1,), (1,)), ((), ())), preferred_element_type=jnp.float32)


def _dot_tn(a, b):
    return lax.dot_general(a, b, (((0,), (0,)), ((), ())), preferred_element_type=jnp.float32)


def _params(sem=("arbitrary",)):
    return pltpu.CompilerParams(dimension_semantics=sem, vmem_limit_bytes=_VMEM_LIMIT)


def _row_block(t, cap=640):
    for rb in (640, 320, 128):
        if rb <= cap and t % rb == 0 and (t > 1024 or rb == 128):
            return rb
    raise ValueError(t)


def _rows(rb, n):
    return pl.BlockSpec((rb, n), lambda i: (i, 0))


def _full(shape):
    nd = len(shape)
    return pl.BlockSpec(tuple(shape), lambda i: (0,) * nd, pipeline_mode=pl.Buffered(1))


def _acc(shape):
    nd = len(shape)
    return pl.BlockSpec(tuple(shape), lambda i: (0,) * nd)


def _sigmoid(x):
    return 1.0 / (1.0 + jnp.exp(-x))


def _ln_fwd(s):
    mu = jnp.mean(s, axis=-1, keepdims=True)
    xc = s - mu
    var = jnp.mean(xc * xc, axis=-1, keepdims=True)
    rstd = lax.rsqrt(var + LN_EPS)
    return xc * rstd, rstd


def _ln_bwd(dxh, xh, rstd):
    m1 = jnp.mean(dxh, axis=-1, keepdims=True)
    m2 = jnp.mean(dxh * xh, axis=-1, keepdims=True)
    return rstd * (dxh - m1 - xh * m2)


def _ln_in_fwd(raw):
    t = raw.shape[0]
    rb = _row_block(t)

    def body(raw_ref, xh_ref, rstd_ref):
        xh, rstd = _ln_fwd(raw_ref[...])
        xh_ref[...] = xh
        rstd_ref[...] = rstd

    return pl.pallas_call(
        body, name="ln_in_fwd", grid=(t // rb,),
        in_specs=[_rows(rb, D)],
        out_specs=[_rows(rb, D), _rows(rb, 1)],
        out_shape=[jax.ShapeDtypeStruct((t, D), jnp.float32), jax.ShapeDtypeStruct((t, 1), jnp.float32)],
        compiler_params=_params(),
    )(raw)


def _ffn_fwd(name, xh, gb, wfull, f):
    t = xh.shape[0]
    rb = _row_block(t, _FFN_ROWS)

    def body(xh_ref, gb_ref, w_ref, out_ref, rstd_ref, au_ref, hb_ref, w13, w2, sems):
        @pl.when(pl.program_id(0) == 0)
        def _():
            _load_ffn_weights(w_ref, f, w13, w2, sems)

        h = xh_ref[...] * gb_ref[0:1, :] + gb_ref[1:2, :]
        hb = h.astype(_MM)
        hb_ref[...] = hb
        acc = jnp.zeros((rb, D), jnp.float32)
        for j in range(2):
            lo = j * FF_SLOT
            a = _dot_nt(hb, w13[j])
            u = _dot_nt(hb, w13[2 + j])
            au_ref[:, lo:lo + FF_SLOT] = a.astype(_MM)
            au_ref[:, D_FFP + lo:D_FFP + lo + FF_SLOT] = u.astype(_MM)
            hid = (a * _sigmoid(a) * u).astype(_MM)
            acc = acc + _dot(hid, w2[lo:lo + FF_SLOT, :])
        xo, rstd = _ln_fwd(ALPHA * h + 0.5 * acc)
        out_ref[...] = xo
        rstd_ref[...] = rstd

    return pl.pallas_call(
        body, name=name, grid=(t // rb,),
        in_specs=[_rows(rb, D), _full((2, D)), _ANY],
        out_specs=[_rows(rb, D), _rows(rb, 1), _rows(rb, 2 * D_FFP), _rows(rb, D)],
        out_shape=[jax.ShapeDtypeStruct((t, D), jnp.float32), jax.ShapeDtypeStruct((t, 1), jnp.float32),
                   jax.ShapeDtypeStruct((t, 2 * D_FFP), _MM), jax.ShapeDtypeStruct((t, D), _MM)],
        scratch_shapes=[pltpu.VMEM((N_SHARD, FF_SLOT, D), _MM), pltpu.VMEM((D_FFP, D), _MM),
                        pltpu.SemaphoreType.DMA((8,))],
        compiler_params=_params(),
    )(xh, gb, wfull)


def _ffn_bwd(name, dy, xo, rstd, gb_out, au, wfull, f):
    t = xo.shape[0]
    rb = _row_block(t, 320)

    def body(dy_ref, xo_ref, rstd_ref, gbo_ref, au_ref, w_ref,
             dh_ref, hid_ref, dau_ref, dffn_ref, dgb_ref, w13, w2, sems):
        i = pl.program_id(0)

        @pl.when(i == 0)
        def _():
            dgb_ref[...] = jnp.zeros_like(dgb_ref)
            _load_ffn_weights(w_ref, f, w13, w2, sems)

        dy = dy_ref[...]
        xo = xo_ref[...]
        dgb_ref[0:1, :] += jnp.sum(dy * xo, axis=0, keepdims=True)
        dgb_ref[1:2, :] += jnp.sum(dy, axis=0, keepdims=True)
        ds = _ln_bwd(dy * gbo_ref[0:1, :], xo, rstd_ref[...])
        dffn = (0.5 * ds).astype(_MM)
        dffn_ref[...] = dffn
        dh = ALPHA * ds
        for j in range(2):
            lo = j * FF_SLOT
            a = au_ref[:, lo:lo + FF_SLOT].astype(jnp.float32)
            u = au_ref[:, D_FFP + lo:D_FFP + lo + FF_SLOT].astype(jnp.float32)
            sg = _sigmoid(a)
            si = a * sg
            hid_ref[:, lo:lo + FF_SLOT] = (si * u).astype(_MM)
            dhid = _dot_nt(dffn, w2[lo:lo + FF_SLOT, :])
            da = (dhid * u * (sg * (1.0 + a * (1.0 - sg)))).astype(_MM)
            du = (dhid * si).astype(_MM)
            dau_ref[:, lo:lo + FF_SLOT] = da
            dau_ref[:, D_FFP + lo:D_FFP + lo + FF_SLOT] = du
            dh = dh + _dot(da, w13[j]) + _dot(du, w13[2 + j])
        dh_ref[...] = dh

    return pl.pallas_call(
        body, name=name, grid=(t // rb,),
        in_specs=[_rows(rb, D), _rows(rb, D), _rows(rb, 1), _full((2, D)), _rows(rb, 2 * D_FFP), _ANY],
        out_specs=[_rows(rb, D), _rows(rb, D_FFP), _rows(rb, 2 * D_FFP), _rows(rb, D), _acc((8, D))],
        out_shape=[jax.ShapeDtypeStruct((t, D), jnp.float32),
                   jax.ShapeDtypeStruct((t, D_FFP), _MM), jax.ShapeDtypeStruct((t, 2 * D_FFP), _MM),
                   jax.ShapeDtypeStruct((t, D), _MM), jax.ShapeDtypeStruct((8, D), jnp.float32)],
        scratch_shapes=[pltpu.VMEM((N_SHARD, FF_SLOT, D), _MM), pltpu.VMEM((D_FFP, D), _MM),
                        pltpu.SemaphoreType.DMA((8,))],
        compiler_params=_params(),
    )(dy, xo, rstd, gb_out, au, wfull)


def _mix_in_fwd(name, xh, gb, wfull):
    t = xh.shape[0]
    rb = _row_block(t)

    def body(xh_ref, gb_ref, w_ref, z_ref, wt, sems):
        @pl.when(pl.program_id(0) == 0)
        def _():
            _load_rows(w_ref, OFF_WIN, WIN_ROWS, wt, sems)

        h = xh_ref[...] * gb_ref[0:1, :] + gb_ref[1:2, :]
        z = _dot_nt(h.astype(_MM), wt[...])
        row = pl.program_id(0) * rb + lax.broadcasted_iota(jnp.int32, (rb, 1), 0)
        z_ref[...] = jnp.where(row >= PAD, z, 0.0)

    return pl.pallas_call(
        body, name=name, grid=(t // rb,),
        in_specs=[_rows(rb, D), _full((2, D)), _ANY],
        out_specs=_rows(rb, D_IN),
        out_shape=jax.ShapeDtypeStruct((t, D_IN), jnp.float32),
        scratch_shapes=[pltpu.VMEM((D_IN, D), _MM), pltpu.SemaphoreType.DMA((4,))],
        compiler_params=_params(),
    )(xh, gb, wfull)


def _mix_in_bwd(name, dh_res, dz, xh, gb, wfull):
    t = xh.shape[0]
    rb = _row_block(t)

    def body(dhr_ref, dz_ref, xh_ref, gb_ref, w_ref, dh_ref, hb_ref, wt, sems):
        @pl.when(pl.program_id(0) == 0)
        def _():
            _load_rows(w_ref, OFF_WIN, WIN_ROWS, wt, sems)

        dh_ref[...] = dhr_ref[...] + _dot(dz_ref[...], wt[...])
        hb_ref[...] = (xh_ref[...] * gb_ref[0:1, :] + gb_ref[1:2, :]).astype(_MM)

    return pl.pallas_call(
        body, name=name, grid=(t // rb,),
        in_specs=[_rows(rb, D), _rows(rb, D_IN), _rows(rb, D), _full((2, D)), _ANY],
        out_specs=[_rows(rb, D), _rows(rb, D)],
        out_shape=[jax.ShapeDtypeStruct((t, D), jnp.float32), jax.ShapeDtypeStruct((t, D), _MM)],
        scratch_shapes=[pltpu.VMEM((D_IN, D), _MM), pltpu.SemaphoreType.DMA((4,))],
        compiler_params=_params(),
    )(dh_res, dz, xh, gb, wfull)


def _mix_out_fwd(name, xh, gb, ycat, wfull):
    t = xh.shape[0]
    rb = _row_block(t)

    def body(xh_ref, gb_ref, y_ref, w_ref, out_ref, rstd_ref, wo, sems):
        @pl.when(pl.program_id(0) == 0)
        def _():
            _load_rows(w_ref, OFF_WOUT, WOUT_ROWS, wo, sems)

        h = xh_ref[...] * gb_ref[0:1, :] + gb_ref[1:2, :]
        xo, rstd = _ln_fwd(ALPHA * h + _dot(y_ref[...], wo[...]))
        out_ref[...] = xo
        rstd_ref[...] = rstd

    return pl.pallas_call(
        body, name=name, grid=(t // rb,),
        in_specs=[_rows(rb, D), _full((2, D)), _rows(rb, D), _ANY],
        out_specs=[_rows(rb, D), _rows(rb, 1)],
        out_shape=[jax.ShapeDtypeStruct((t, D), jnp.float32), jax.ShapeDtypeStruct((t, 1), jnp.float32)],
        scratch_shapes=[pltpu.VMEM((D, D), _MM), pltpu.SemaphoreType.DMA((4,))],
        compiler_params=_params(),
    )(xh, gb, ycat, wfull)


def _mix_out_bwd(name, dy, xo, rstd, gb_out, wfull):
    t = xo.shape[0]
    rb = _row_block(t)

    def body(dy_ref, xo_ref, rstd_ref, gbo_ref, w_ref, dhr_ref, dyc_ref, dsb_ref, dgb_ref, wo, sems):
        @pl.when(pl.program_id(0) == 0)
        def _():
            dgb_ref[...] = jnp.zeros_like(dgb_ref)
            _load_rows(w_ref, OFF_WOUT, WOUT_ROWS, wo, sems)

        dy = dy_ref[...]
        xo = xo_ref[...]
        dgb_ref[0:1, :] += jnp.sum(dy * xo, axis=0, keepdims=True)
        dgb_ref[1:2, :] += jnp.sum(dy, axis=0, keepdims=True)
        ds = _ln_bwd(dy * gbo_ref[0:1, :], xo, rstd_ref[...])
        dsb = ds.astype(_MM)
        dsb_ref[...] = dsb
        dhr_ref[...] = ALPHA * ds
        dyc_ref[...] = _dot_nt(dsb, wo[...])

    return pl.pallas_call(
        body, name=name, grid=(t // rb,),
        in_specs=[_rows(rb, D), _rows(rb, D), _rows(rb, 1), _full((2, D)), _ANY],
        out_specs=[_rows(rb, D), _rows(rb, D), _rows(rb, D), _acc((8, D))],
        out_shape=[jax.ShapeDtypeStruct((t, D), jnp.float32), jax.ShapeDtypeStruct((t, D), jnp.float32),
                   jax.ShapeDtypeStruct((t, D), _MM), jax.ShapeDtypeStruct((8, D), jnp.float32)],
        scratch_shapes=[pltpu.VMEM((D, D), _MM), pltpu.SemaphoreType.DMA((4,))],
        compiler_params=_params(),
    )(dy, xo, rstd, gb_out, wfull)


def _loss_fwd_bwd(xh, gb, target):
    t = xh.shape[0]
    rb = _row_block(t)

    def body(xh_ref, gb_ref, tg_ref, dy_ref, loss_ref):
        @pl.when(pl.program_id(0) == 0)
        def _():
            loss_ref[...] = jnp.zeros_like(loss_ref)

        y = xh_ref[...] * gb_ref[0:1, :] + gb_ref[1:2, :]
        row = pl.program_id(0) * rb + lax.broadcasted_iota(jnp.int32, (rb, 1), 0)
        err = jnp.where(row >= ROW0, y - tg_ref[...], 0.0)
        dy_ref[...] = err * (1.0 / D)
        per_row = jnp.mean(err * err, axis=-1, keepdims=True)
        loss_ref[...] += 0.5 * jnp.sum(per_row, axis=0, keepdims=True)

    return pl.pallas_call(
        body, name="loss", grid=(t // rb,),
        in_specs=[_rows(rb, D), _full((2, D)), _rows(rb, D)],
        out_specs=[_rows(rb, D), _acc((1, 1))],
        out_shape=[jax.ShapeDtypeStruct((t, D), jnp.float32), jax.ShapeDtypeStruct((1, 1), jnp.float32)],
        compiler_params=_params(),
    )(xh, gb, target)


def _ln_in_bwd(dy, xh, rstd, gb):
    t = xh.shape[0]
    rb = _row_block(t)

    def body(dy_ref, xh_ref, rstd_ref, gb_ref, dx_ref, dgb_ref):
        @pl.when(pl.program_id(0) == 0)
        def _():
            dgb_ref[...] = jnp.zeros_like(dgb_ref)

        dy = dy_ref[...]
        xh = xh_ref[...]
        dgb_ref[0:1, :] += jnp.sum(dy * xh, axis=0, keepdims=True)
        dgb_ref[1:2, :] += jnp.sum(dy, axis=0, keepdims=True)
        dx_ref[...] = _ln_bwd(dy * gb_ref[0:1, :], xh, rstd_ref[...])

    return pl.pallas_call(
        body, name="ln_in_bwd", grid=(t // rb,),
        in_specs=[_rows(rb, D), _rows(rb, D), _rows(rb, 1), _full((2, D))],
        out_specs=[_rows(rb, D), _acc((8, D))],
        out_shape=[jax.ShapeDtypeStruct((t, D), jnp.float32), jax.ShapeDtypeStruct((8, D), jnp.float32)],
        compiler_params=_params(),
    )(dy, xh, rstd, gb)


def _dw_rows(t, cols):
    for tt in (2080, 1664, 640, 128):
        vmem = 2 * tt * (cols + D) * 2 + cols * D * 6
        if t % tt == 0 and (t > 1024 or tt == 128) and vmem <= 44 * 1024 * 1024:
            return tt
    raise ValueError((t, cols))


def _dw_into(name, gpack, x, y, cols, pieces):
    t, k = x.shape
    tt = _dw_rows(t, cols)
    nt = t // tt

    def body(x_ref, y_ref, g_in, g_out, acc, stage, sems):
        j = pl.program_id(0)
        s = pl.program_id(1)

        @pl.when(s == 0)
        def _():
            acc[...] = jnp.zeros_like(acc)

        acc[...] += _dot_tn(x_ref[...], y_ref[...])

        @pl.when(s == nt - 1)
        def _():
            stage[...] = acc[...].astype(stage.dtype)
            cps = []
            for q, (lo, n, chip_of, off) in enumerate(pieces):
                cp = pltpu.make_async_copy(stage.at[pl.ds(lo, n)], g_out.at[chip_of(j), pl.ds(off, n)],
                                           sems.at[q])
                cp.start()
                cps.append(cp)
            for cp in cps:
                cp.wait()

    return pl.pallas_call(
        body, name=name, grid=(k // cols, nt),
        in_specs=[pl.BlockSpec((tt, cols), lambda j, s: (s, j)), pl.BlockSpec((tt, D), lambda j, s: (s, 0)), _ANY],
        out_specs=_ANY,
        out_shape=jax.ShapeDtypeStruct(gpack.shape, gpack.dtype),
        input_output_aliases={2: 0},
        scratch_shapes=[pltpu.VMEM((cols, D), jnp.float32), pltpu.VMEM((cols, D), gpack.dtype),
                        pltpu.SemaphoreType.DMA((len(pieces),))],
        compiler_params=_params(("arbitrary", "arbitrary")),
    )(x, y, gpack)


_TAIL_U = 32
_TAIL_X = 16
_MIX_ROWS = 320


def _decay_mask(rb, h):
    ii = lax.broadcasted_iota(jnp.int32, (rb, rb), 0)
    jj = lax.broadcasted_iota(jnp.int32, (rb, rb), 1)
    dist = jnp.abs(ii - jj).astype(jnp.float32)
    vis = (jj >> 6) <= (ii >> 6)
    return jnp.where(vis, jnp.exp(LOG_GAMMA[h] * dist), 0.0)


def _row_decays(rb, h):
    r = lax.broadcasted_iota(jnp.int32, (rb, DH), 0).astype(jnp.float32)
    return jnp.exp(LOG_GAMMA[h] * (r + 1.0)), jnp.exp(LOG_GAMMA[h] * (rb - 1.0 - r))


def _rope(x, cs, sn):
    return x * cs + pltpu.roll(x, DH // 2, 1) * sn


def _rope_t(dx, cs, sn):
    return dx * cs + pltpu.roll(dx * sn, DH // 2, 1)


def _pool_count(blk, rb):
    row = blk * rb + lax.broadcasted_iota(jnp.int32, (rb, D_POOL), 0) - PAD
    lane = lax.broadcasted_iota(jnp.int32, (rb, D_POOL), 1)
    win = jnp.left_shift(2, lane >> 6)
    return jnp.clip(row + 1, 1, win).astype(jnp.float32)


def _pool_select(p2, p4, p8, p16):
    lane = lax.broadcasted_iota(jnp.int32, p2.shape, 1)
    return jnp.where(lane < 64, p2, jnp.where(lane < 128, p4, jnp.where(lane < 192, p8, p16)))


def _window_sums(ext_ref, base, rows, sign):
    acc = ext_ref[pl.ds(base, rows), :]
    outs = []
    for k in range(1, 16):
        acc = acc + ext_ref[pl.ds(base + sign * k, rows), :]
        if k in (1, 3, 7, 15):
            outs.append(acc)
    return _pool_select(*outs)


def _sub_rows(rb):
    return 128 if rb % 128 == 0 else 64


def _mix_core_fwd(name, z, cs, sn, wbd, pscale, cdw, cvec, wpw, gn):
    t = z.shape[0]
    rb = _row_block(t, _MIX_ROWS)
    nblk = t // rb
    sr = _sub_rows(rb)

    def body(z_ref, cs_ref, sn_ref, wbd_ref, ps_ref, cdw_ref, cvec_ref, wpw_ref, gn_ref,
             y_ref, st_ref, ut_ref, cv_ref, yp_ref,
             uext, xext, cv, ypre, state, wmask):
        i = pl.program_id(0)

        @pl.when(i == 0)
        def _():
            state[...] = jnp.zeros_like(state)
            uext[0:_TAIL_U, :] = jnp.zeros((_TAIL_U, D_CONV), jnp.float32)
            xext[0:_TAIL_X, :] = jnp.zeros((_TAIL_X, D_POOL), jnp.float32)
            for h in range(HEADS):
                wmask[h] = _decay_mask(rb, h)

        st_ref[0] = state[...]
        ut_ref[0] = uext[0:_TAIL_U, :]

        xp = z_ref[:, 0:256]
        uext[_TAIL_U:_TAIL_U + rb, :] = z_ref[:, 256:512] * _sigmoid(z_ref[:, 512:768])
        xext[_TAIL_X:_TAIL_X + rb, :] = xp

        for r in range(0, rb, sr):
            win = _window_sums(xext, _TAIL_X + r, sr, -1)
            ypre[r:r + sr, :] = win
            acc = jnp.zeros((sr, D_CONV), jnp.float32)
            for k in range(CONV_W):
                acc = acc + uext[pl.ds(_TAIL_U + r - k, sr), :] * cdw_ref[CONV_W - 1 - k:CONV_W - k, :]
            cv[r:r + sr, :] = acc

        ypb = (ypre[...] / _pool_count(i, rb) - xp).astype(_MM)
        yp_ref[...] = ypb
        y_ref[:, 0:256] = (_dot(ypb, wbd_ref[...]) * ps_ref[...]).astype(_MM)
        cv_ref[...] = cv[...]
        cn, _ = _ln_fwd(cv[...] + cvec_ref[0:1, :])
        ln = cn * cvec_ref[1:2, :] + cvec_ref[2:3, :]
        sw = ln * _sigmoid(ln)
        y_ref[:, 256:512] = _dot(sw.astype(_MM), wpw_ref[...]).astype(_MM)
        csv = cs_ref[...]
        snv = sn_ref[...]
        for h in range(HEADS):
            q = _rope(z_ref[:, 768 + h * DH:768 + (h + 1) * DH], csv, snv)
            k = _rope(z_ref[:, 1280 + h * DH:1280 + (h + 1) * DH], csv, snv) * (DH ** -0.5)
            vb = z_ref[:, 1792 + h * DH:1792 + (h + 1) * DH].astype(_MM)
            g = z_ref[:, 2304 + h * DH:2304 + (h + 1) * DH]
            a, b = _row_decays(rb, h)
            s = _dot_nt(q.astype(_MM), k.astype(_MM)) * wmask[h]
            o = _dot(s.astype(_MM), vb) + _dot((q * a).astype(_MM), state[h].astype(_MM))
            state[h] = math.exp(LOG_GAMMA[h] * rb) * state[h] + _dot_tn((k * b).astype(_MM), vb)
            on, _ = _ln_fwd(o)
            y_ref[:, 512 + h * DH:512 + (h + 1) * DH] = (
                g * _sigmoid(g) * on * gn_ref[:, h * DH:(h + 1) * DH]).astype(_MM)

        uext[0:_TAIL_U, :] = uext[rb:rb + _TAIL_U, :]
        xext[0:_TAIL_X, :] = xext[rb:rb + _TAIL_X, :]

    return pl.pallas_call(
        body, name=name, grid=(nblk,),
        in_specs=[_rows(rb, D_IN), _rows(rb, DH), _rows(rb, DH), _full((256, 256)), _full((1, 256)),
                  _full((32, 256)), _full((8, 256)), _full((256, 256)), _full((1, D_RET))],
        out_specs=[_rows(rb, D),
                   pl.BlockSpec((1, HEADS, DH, DH), lambda i: (i, 0, 0, 0)),
                   pl.BlockSpec((1, _TAIL_U, D_CONV), lambda i: (i, 0, 0)),
                   _rows(rb, D_CONV), _rows(rb, D_POOL)],
        out_shape=[jax.ShapeDtypeStruct((t, D), _MM),
                   jax.ShapeDtypeStruct((nblk, HEADS, DH, DH), jnp.float32),
                   jax.ShapeDtypeStruct((nblk, _TAIL_U, D_CONV), jnp.float32),
                   jax.ShapeDtypeStruct((t, D_CONV), jnp.float32),
                   jax.ShapeDtypeStruct((t, D_POOL), _MM)],
        scratch_shapes=[pltpu.VMEM((rb + _TAIL_U, D_CONV), jnp.float32),
                        pltpu.VMEM((rb + _TAIL_X, D_POOL), jnp.float32),
                        pltpu.VMEM((rb, D_CONV), jnp.float32),
                        pltpu.VMEM((rb, D_POOL), jnp.float32),
                        pltpu.VMEM((HEADS, DH, DH), jnp.float32),
                        pltpu.VMEM((HEADS, rb, rb), jnp.float32)],
        compiler_params=_params(),
    )(z, cs, sn, wbd, pscale, cdw, cvec, wpw, gn)


def _mix_core_bwd(name, z, dyc, cs, sn, st_in, ut_in, cv_in, yp_in, wbd, pscale, cdw, cvec, wpw, gn):
    t = z.shape[0]
    rb = _row_block(t, _MIX_ROWS)
    nblk = t // rb
    sr = _sub_rows(rb)
    rev = lambda i: nblk - 1 - i

    def body(z_ref, dy_ref, cs_ref, sn_ref, st_ref, ut_ref, cv_ref, yp_ref,
             wbd_ref, ps_ref, cdw_ref, cvec_ref, wpw_ref, gn_ref,
             dz_ref, dwbd_ref, dwpw_ref, dcdw_ref, dsm_ref,
             uext, cv, ypre, dcvext, eext, dstate, wmask):
        i = pl.program_id(0)
        blk = nblk - 1 - i

        @pl.when(i == 0)
        def _():
            dstate[...] = jnp.zeros_like(dstate)
            dcvext[rb:rb + _TAIL_U, :] = jnp.zeros((_TAIL_U, D_CONV), jnp.float32)
            eext[rb:rb + _TAIL_X, :] = jnp.zeros((_TAIL_X, D_POOL), jnp.float32)
            dwbd_ref[...] = jnp.zeros_like(dwbd_ref)
            dwpw_ref[...] = jnp.zeros_like(dwpw_ref)
            dcdw_ref[...] = jnp.zeros_like(dcdw_ref)
            dsm_ref[...] = jnp.zeros_like(dsm_ref)
            for h in range(HEADS):
                wmask[h] = _decay_mask(rb, h)

        row = blk * rb + lax.broadcasted_iota(jnp.int32, (rb, 1), 0)
        live = row >= PAD

        ca = z_ref[:, 256:512]
        sg_c = _sigmoid(z_ref[:, 512:768])
        uext[0:_TAIL_U, :] = ut_ref[0]
        uext[_TAIL_U:_TAIL_U + rb, :] = ca * sg_c

        cnt = _pool_count(blk, rb)
        ypb = yp_ref[...]
        dyp = dy_ref[:, 0:256]
        pm = _dot(ypb, wbd_ref[...])
        dsm_ref[1:2, 0:256] += jnp.sum(dyp * pm, axis=0, keepdims=True)
        dpm = (dyp * ps_ref[...]).astype(_MM)
        dwbd_ref[...] += _dot_tn(ypb, dpm)
        dypre = _dot_nt(dpm, wbd_ref[...])
        eext[0:rb, :] = dypre / cnt
        for r in range(0, rb, sr):
            ypre[r:r + sr, :] = _window_sums(eext, r, sr, 1)
        dz_ref[:, 0:256] = jnp.where(live, ypre[...] - dypre, 0.0).astype(_MM)

        cn, rstd_c = _ln_fwd(cv_ref[...] + cvec_ref[0:1, :])
        ln = cn * cvec_ref[1:2, :] + cvec_ref[2:3, :]
        sg_l = _sigmoid(ln)
        swb = (ln * sg_l).astype(_MM)
        dycb = dy_ref[:, 256:512].astype(_MM)
        dwpw_ref[...] += _dot_tn(swb, dycb)
        dln = _dot_nt(dycb, wpw_ref[...]) * (sg_l * (1.0 + ln * (1.0 - sg_l)))
        dsm_ref[3:4, 0:256] += jnp.sum(dln * cn, axis=0, keepdims=True)
        dsm_ref[4:5, 0:256] += jnp.sum(dln, axis=0, keepdims=True)
        dcv = _ln_bwd(dln * cvec_ref[1:2, :], cn, rstd_c)
        dsm_ref[2:3, 0:256] += jnp.sum(dcv, axis=0, keepdims=True)
        dcvext[0:rb, :] = dcv
        for k in range(CONV_W):
            prod = dcv * uext[pl.ds(_TAIL_U - k, rb), :]
            dcdw_ref[CONV_W - 1 - k:CONV_W - k, :] += jnp.sum(prod, axis=0, keepdims=True)
        for r in range(0, rb, sr):
            acc = jnp.zeros((sr, D_CONV), jnp.float32)
            for k in range(CONV_W):
                acc = acc + dcvext[pl.ds(r + k, sr), :] * cdw_ref[CONV_W - 1 - k:CONV_W - k, :]
            cv[r:r + sr, :] = acc
        du = cv[...]
        dz_ref[:, 256:512] = jnp.where(live, du * sg_c, 0.0).astype(_MM)
        dz_ref[:, 512:768] = jnp.where(live, du * ca * sg_c * (1.0 - sg_c), 0.0).astype(_MM)

        csv = cs_ref[...]
        snv = sn_ref[...]
        for h in range(HEADS):
            q = _rope(z_ref[:, 768 + h * DH:768 + (h + 1) * DH], csv, snv)
            k = _rope(z_ref[:, 1280 + h * DH:1280 + (h + 1) * DH], csv, snv) * (DH ** -0.5)
            vb = z_ref[:, 1792 + h * DH:1792 + (h + 1) * DH].astype(_MM)
            g = z_ref[:, 2304 + h * DH:2304 + (h + 1) * DH]
            a, b = _row_decays(rb, h)
            qb = q.astype(_MM)
            kb = k.astype(_MM)
            qab = (q * a).astype(_MM)
            kbb = (k * b).astype(_MM)
            stb = st_ref[0, h].astype(_MM)
            sb = (_dot_nt(qb, kb) * wmask[h]).astype(_MM)
            o = _dot(sb, vb) + _dot(qab, stb)
            on, rstd_o = _ln_fwd(o)
            gnv = gn_ref[:, h * DH:(h + 1) * DH]
            sg_g = _sigmoid(g)
            si_g = g * sg_g
            dyr = dy_ref[:, 512 + h * DH:512 + (h + 1) * DH]
            dsm_ref[0:1, h * DH:(h + 1) * DH] += jnp.sum(dyr * on * si_g, axis=0, keepdims=True)
            dgate = dyr * on * gnv * (sg_g * (1.0 + g * (1.0 - sg_g)))
            dob = _ln_bwd(dyr * gnv * si_g, on, rstd_o).astype(_MM)
            dstb = dstate[h].astype(_MM)
            dsb = (_dot_nt(dob, vb) * wmask[h]).astype(_MM)
            dq = _dot(dsb, kb) + _dot_nt(dob, stb) * a
            dk = _dot_tn(dsb, qb) + _dot_nt(vb, dstb) * b
            dv = _dot_tn(sb, dob) + _dot(kbb, dstb)
            dstate[h] = math.exp(LOG_GAMMA[h] * rb) * dstate[h] + _dot_tn(qab, dob)
            dz_ref[:, 768 + h * DH:768 + (h + 1) * DH] = jnp.where(live, _rope_t(dq, csv, snv), 0.0).astype(_MM)
            dz_ref[:, 1280 + h * DH:1280 + (h + 1) * DH] = jnp.where(
                live, _rope_t(dk * (DH ** -0.5), csv, snv), 0.0).astype(_MM)
            dz_ref[:, 1792 + h * DH:1792 + (h + 1) * DH] = jnp.where(live, dv, 0.0).astype(_MM)
            dz_ref[:, 2304 + h * DH:2304 + (h + 1) * DH] = jnp.where(live, dgate, 0.0).astype(_MM)

        dcvext[rb:rb + _TAIL_U, :] = dcvext[0:_TAIL_U, :]
        eext[rb:rb + _TAIL_X, :] = eext[0:_TAIL_X, :]

    rrows = lambda n: pl.BlockSpec((rb, n), lambda i: (rev(i), 0))
    return pl.pallas_call(
        body, name=name, grid=(nblk,),
        in_specs=[rrows(D_IN), rrows(D), rrows(DH), rrows(DH),
                  pl.BlockSpec((1, HEADS, DH, DH), lambda i: (rev(i), 0, 0, 0)),
                  pl.BlockSpec((1, _TAIL_U, D_CONV), lambda i: (rev(i), 0, 0)),
                  rrows(D_CONV), rrows(D_POOL),
                  _full((256, 256)), _full((1, 256)), _full((32, 256)), _full((8, 256)), _full((256, 256)),
                  _full((1, D_RET))],
        out_specs=[rrows(D_IN), _acc((256, 256)), _acc((256, 256)), _acc((32, 256)), _acc((8, 512))],
        out_shape=[jax.ShapeDtypeStruct((t, D_IN), _MM),
                   jax.ShapeDtypeStruct((256, 256), jnp.float32), jax.ShapeDtypeStruct((256, 256), jnp.float32),
                   jax.ShapeDtypeStruct((32, 256), jnp.float32), jax.ShapeDtypeStruct((8, 512), jnp.float32)],
        scratch_shapes=[pltpu.VMEM((rb + _TAIL_U, D_CONV), jnp.float32),
                        pltpu.VMEM((rb, D_CONV), jnp.float32),
                        pltpu.VMEM((rb, D_POOL), jnp.float32),
                        pltpu.VMEM((rb + _TAIL_U, D_CONV), jnp.float32),
                        pltpu.VMEM((rb + _TAIL_X, D_POOL), jnp.float32),
                        pltpu.VMEM((HEADS, DH, DH), jnp.float32),
                        pltpu.VMEM((HEADS, rb, rb), jnp.float32)],
        compiler_params=_params(),
    )(z, dyc, cs, sn, st_in, ut_in, cv_in, yp_in, wbd, pscale, cdw, cvec, wpw, gn)


def _me():
    return lax.axis_index("x"), lax.axis_index("y"), lax.axis_index("c")


def _flip(me, mask):
    return tuple(1 - m if f else m for m, f in zip(me, mask))


def _push(name, aliased, inputs, fresh, remote):
    n_al, n_in, n_out, n_rem = len(aliased), len(inputs), len(fresh), len(remote)

    def body(*refs):
        ins = refs[n_al:n_al + n_in]
        al = refs[n_al + n_in:2 * n_al + n_in]
        outs = refs[2 * n_al + n_in:2 * n_al + n_in + n_out]
        send_sems, recv_sems = refs[2 * n_al + n_in + n_out:]
        me = _me()
        copies = []
        for k, (mask, src_fn, dst_fn) in enumerate(remote):
            cp = pltpu.make_async_remote_copy(
                src_ref=src_fn(al, ins, outs, me), dst_ref=dst_fn(al, ins, outs, me),
                send_sem=send_sems.at[k], recv_sem=recv_sems.at[k],
                device_id=_flip(me, mask), device_id_type=MESH)
            cp.start()
            copies.append(cp)
        for cp in copies:
            cp.wait()

    return pl.pallas_call(
        body, name=name,
        in_specs=[_ANY] * (n_al + n_in), out_specs=[_ANY] * (n_al + n_out),
        out_shape=[jax.ShapeDtypeStruct(a.shape, a.dtype) for a in aliased] + list(fresh),
        input_output_aliases={i: i for i in range(n_al)},
        scratch_shapes=[pltpu.SemaphoreType.DMA((n_rem,)), pltpu.SemaphoreType.DMA((n_rem,))],
    )(*aliased, *inputs)


_HBM = pl.BlockSpec(memory_space=pltpu.HBM)
_SEM = pl.BlockSpec(memory_space=pltpu.SEMAPHORE)
_EFFECT = pltpu.SideEffectType.DATAFLOW_SIDE_EFFECTING


def _push_start(name, bufs, remote):
    n, n_rem = len(bufs), len(remote)

    def body(*refs):
        ins = refs[:n]
        send_sems, recv_sems = refs[n], refs[n + 1]
        token = refs[2 * n + 2]
        me = _me()
        for k, (mask, src_fn, dst_fn) in enumerate(remote):
            pltpu.make_async_remote_copy(
                src_ref=src_fn(ins, me), dst_ref=dst_fn(ins, me),
                send_sem=send_sems.at[k], recv_sem=recv_sems.at[k],
                device_id=_flip(me, mask), device_id_type=MESH).start()
        token[...] = jnp.zeros_like(token)

    out = pl.pallas_call(
        body, name=name,
        out_shape=(pltpu.SemaphoreType.DMA((n_rem,)), pltpu.SemaphoreType.DMA((n_rem,)),
                   *[pltpu.HBM(b.shape, b.dtype) for b in bufs], jax.ShapeDtypeStruct((8, 128), jnp.float32)),
        in_specs=[_HBM] * n,
        out_specs=(_SEM, _SEM, *[_HBM] * n, pl.BlockSpec(memory_space=pltpu.VMEM)),
        input_output_aliases={i: i + 2 for i in range(n)},
        compiler_params=pltpu.CompilerParams(has_side_effects=_EFFECT),
    )(*[pltpu.with_memory_space_constraint(b, pltpu.HBM) for b in bufs])
    return out[0], out[1], list(out[2:2 + n]), out[2 + n]


def _push_wait(name, send_sems, recv_sems, bufs, after, remote):
    n = len(bufs)

    def body(*refs):
        ins = refs[:n]
        s_sems, r_sems = refs[n], refs[n + 1]
        me = _me()
        for k, (mask, src_fn, dst_fn) in enumerate(remote):
            cp = pltpu.make_async_remote_copy(
                src_ref=src_fn(ins, me), dst_ref=dst_fn(ins, me),
                send_sem=s_sems.at[k], recv_sem=r_sems.at[k],
                device_id=_flip(me, mask), device_id_type=MESH)
            cp.wait_send()
            cp.wait_recv()

    out = pl.pallas_call(
        body, name=name,
        out_shape=tuple(pltpu.HBM(b.shape, b.dtype) for b in bufs),
        in_specs=[_HBM] * n + [_SEM, _SEM, _ANY], out_specs=tuple([_HBM] * n),
        input_output_aliases={i: i for i in range(n)},
        compiler_params=pltpu.CompilerParams(has_side_effects=_EFFECT),
    )(*bufs, send_sems, recv_sems, after)
    return list(out)


_ICI_MASKS = ((0, 1, 0), (1, 0, 0), (1, 1, 0))
_D2D_MASK = (0, 0, 1)
_ALL_MASKS = tuple((a, b, c) for a in (0, 1) for b in (0, 1) for c in (0, 1))[1:]


def _chip(me):
    return 2 * me[0] + me[1]


def _half(me):
    return pl.ds(me[2] * HALF_ROWS, HALF_ROWS)


def _other_half(me):
    return pl.ds((1 - me[2]) * HALF_ROWS, HALF_ROWS)


def _own_slot(mine):
    chip = _chip(_me())
    return lax.dynamic_update_slice(lax.empty((N_SHARD,) + mine.shape, mine.dtype), mine[None],
                                    (chip,) + (0,) * mine.ndim)


def _gather_ici_plan(with_small):
    remote = []
    for mask in _ICI_MASKS:
        mine = lambda bufs, me: bufs[0].at[_chip(me), _half(me)]
        remote.append((mask, mine, mine))
        if with_small:
            mine_small = lambda bufs, me: bufs[1].at[_chip(me)]
            remote.append((mask, mine_small, mine_small))
    return remote


def _gather_d2d(name, w):
    remote = []
    for j in range(1, N_SHARD):
        theirs = lambda al, ins, outs, me, j=j: al[0].at[(_chip(me) + j) % N_SHARD, _half(me)]
        remote.append((_D2D_MASK, theirs, theirs))
    (w,) = _push(name, [w], [], [], remote)
    return w


def _sum_pair(name, g, recv):
    _, r, _ = g.shape
    rb = _row_block(r // 2)
    nb = r // 2 // rb
    c = lax.axis_index("c").astype(jnp.int32).reshape(1)

    def body(c_ref, g_ref, r_ref, o_ref):
        o_ref[...] = (g_ref[...].astype(jnp.float32) + r_ref[...].astype(jnp.float32)).astype(o_ref.dtype)

    return pl.pallas_call(
        body, name=name,
        grid_spec=pltpu.PrefetchScalarGridSpec(
            num_scalar_prefetch=1, grid=(N_SHARD, nb),
            in_specs=[pl.BlockSpec((None, rb, D), lambda s, i, c_ref: (s, c_ref[0] * nb + i, 0)),
                      pl.BlockSpec((None, rb, D), lambda s, i, c_ref: (s, i, 0))],
            out_specs=pl.BlockSpec((None, rb, D), lambda s, i, c_ref: (s, i, 0))),
        out_shape=jax.ShapeDtypeStruct((N_SHARD, r // 2, D), g.dtype),
        compiler_params=_params(("arbitrary", "arbitrary")),
    )(c, g, recv)


def _sum_chips(name, p, recv):
    _, rh, _ = p.shape
    rb = _row_block(rh)
    nb = rh // rb
    s = jnp.stack([2 * lax.axis_index("x") + lax.axis_index("y"), lax.axis_index("c")]).astype(jnp.int32)

    def body(s_ref, p_ref, r_ref, o_ref):
        acc = p_ref[...].astype(jnp.float32)
        for j in range(3):
            acc = acc + r_ref[j].astype(jnp.float32)
        o_ref[...] = acc

    return pl.pallas_call(
        body, name=name,
        grid_spec=pltpu.PrefetchScalarGridSpec(
            num_scalar_prefetch=1, grid=(nb,),
            in_specs=[pl.BlockSpec((None, rb, D), lambda i, s_ref: (s_ref[0], i, 0)),
                      pl.BlockSpec((3, rb, D), lambda i, s_ref: (0, i, 0))],
            out_specs=pl.BlockSpec((rb, D), lambda i, s_ref: (s_ref[1] * nb + i, 0))),
        out_shape=jax.ShapeDtypeStruct((2 * rh, D), jnp.float32),
        compiler_params=_params(),
    )(s, p, recv)


def _rs_ici_plan():
    remote = []
    for j, mask in enumerate(_ICI_MASKS):
        remote.append((mask,
                       lambda bufs, me, mask=mask: bufs[0].at[_chip(_flip(me, mask))],
                       lambda bufs, me, j=j: bufs[1].at[j]))
    return remote


def _rs_pair(tag, g):
    _, r, _ = g.shape
    remote = [(_D2D_MASK,
               lambda al, ins, outs, me, s=s: ins[0].at[s, _other_half(me)],
               lambda al, ins, outs, me, s=s: outs[0].at[s]) for s in range(N_SHARD)]
    (recv,) = _push("rs_d2d_" + tag, [], [g], [jax.ShapeDtypeStruct((N_SHARD, r // 2, D), g.dtype)], remote)
    return _sum_pair("rs_sum_pair_" + tag, g, recv)


def _rs_finish(tag, p, recv3):
    mine = _sum_chips("rs_sum_chips_" + tag, p, recv3)
    half = lambda al, ins, outs, me: al[0].at[_half(me)]
    (both,) = _push("rs_share_" + tag, [mine], [], [], [(_D2D_MASK, half, half)])
    return both


def _all_reduce_small(v):
    s = v.shape[0]
    me = _me()
    every = lax.dynamic_update_slice(lax.empty((8, s, D), jnp.float32), v[None], (4 * me[0] + 2 * me[1] + me[2], 0, 0))
    slot = lambda al, ins, outs, me: al[0].at[4 * me[0] + 2 * me[1] + me[2]]
    (every,) = _push("small_all", [every], [], [], [(mask, slot, slot) for mask in _ALL_MASKS])

    def body(e_ref, o_ref):
        acc = e_ref[0]
        for j in range(1, 8):
            acc = acc + e_ref[j]
        o_ref[...] = acc

    return pl.pallas_call(
        body, name="small_sum", grid=(1,),
        in_specs=[pl.BlockSpec((8, s, D), lambda i: (0, 0, 0))],
        out_specs=pl.BlockSpec((s, D), lambda i: (0, 0)),
        out_shape=jax.ShapeDtypeStruct((s, D), jnp.float32),
        compiler_params=_params(),
    )(every)


def _adamw(name, w, g, m, v):
    r, c = w.shape
    rb = next(b for b in (256, 344, 128, 64, 32, 16, 8, r) if r % b == 0)

    def body(w_ref, g_ref, m_ref, v_ref, d_ref, mo_ref, vo_ref):
        g = g_ref[...]
        m = ADAM_B1 * m_ref[...] + (1.0 - ADAM_B1) * g
        v = ADAM_B2 * v_ref[...] + (1.0 - ADAM_B2) * (g * g)
        m_hat = m / (1.0 - ADAM_B1 ** ADAM_STEP)
        v_hat = v / (1.0 - ADAM_B2 ** ADAM_STEP)
        d_ref[...] = -ADAM_LR * (m_hat / (jnp.sqrt(v_hat) + ADAM_EPS) + ADAM_WD * w_ref[...])
        mo_ref[...] = m
        vo_ref[...] = v

    spec = pl.BlockSpec((rb, c), lambda i: (i, 0))
    return pl.pallas_call(
        body, name=name, grid=(r // rb,),
        in_specs=[spec] * 4, out_specs=[spec] * 3,
        out_shape=[jax.ShapeDtypeStruct((r, c), jnp.float32)] * 3,
        compiler_params=_params(),
    )(w, g, m, v)


_BIG = ("ffn1_w13", "ffn2_w13", "ffn1_w2", "ffn2_w2", "w_in", "w_out", "conv_pw")
_BIG_SHARD = {"ffn1_w13": (D, 1376), "ffn2_w13": (D, 1376), "ffn1_w2": (688, D), "ffn2_w2": (688, D),
              "w_in": (D, 704), "w_out": (256, D), "conv_pw": (64, 256)}


def _pack_rows(parts):
    flat = jnp.concatenate([p.reshape(-1) for p in parts])
    pad = (-flat.shape[0]) % (8 * D)
    if pad:
        flat = jnp.concatenate([flat, jnp.zeros((pad,), flat.dtype)])
    return flat.reshape(-1, D)


def _unpack_rows(buf, shapes):
    flat = buf.reshape(-1)
    out, off = [], 0
    for shp in shapes:
        n = math.prod(shp)
        out.append(flat[off:off + n].reshape(shp))
        off += n
    return out


def _pack_big_half(parts):
    pw = parts["conv_pw"].reshape(PW_ROWS // 2, D)
    return jnp.concatenate([parts["ffn1_w13"].T, parts["ffn2_w13"].T, parts["ffn1_w2"], parts["ffn2_w2"],
                            parts["w_in"].T, parts["w_out"], pw, jnp.zeros_like(pw)], axis=0)


def _unpack_big_half(buf):
    return {"ffn1_w13": buf[OFF_W13[0]:OFF_W13[0] + W13_ROWS].T, "ffn2_w13": buf[OFF_W13[1]:OFF_W13[1] + W13_ROWS].T,
            "ffn1_w2": buf[OFF_W2[0]:OFF_W2[0] + W2_ROWS], "ffn2_w2": buf[OFF_W2[1]:OFF_W2[1] + W2_ROWS],
            "w_in": buf[OFF_WIN:OFF_WIN + WIN_ROWS].T, "w_out": buf[OFF_WOUT:OFF_WOUT + WOUT_ROWS],
            "conv_pw": buf[OFF_PW:OFF_PW + PW_ROWS // 2].reshape(64, 256)}


def kernel(x, meta, ln_in_g, ln_in_b, ffn1_w13, ffn1_w2, w_in, pool_w, pool_scale, conv_dw, conv_db, conv_ln_g, conv_ln_b, conv_pw, ret_gn_g, w_out, ffn2_w13, ffn2_w2, ln_g, ln_b, loss_target, m_meta, m_ln_in_g, m_ln_in_b, m_ffn1_w13, m_ffn1_w2, m_w_in, m_pool_w, m_pool_scale, m_conv_dw, m_conv_db, m_conv_ln_g, m_conv_ln_b, m_conv_pw, m_ret_gn_g, m_w_out, m_ffn2_w13, m_ffn2_w2, m_ln_g, m_ln_b, v_meta, v_ln_in_g, v_ln_in_b, v_ffn1_w13, v_ffn1_w2, v_w_in, v_pool_w, v_pool_scale, v_conv_dw, v_conv_db, v_conv_ln_g, v_conv_ln_b, v_conv_pw, v_ret_gn_g, v_w_out, v_ffn2_w13, v_ffn2_w2, v_ln_g, v_ln_b):
    f32 = jnp.float32
    seq = x.shape[1]
    t = seq + ROW0
    me = _me()
    chip = _chip(me)
    big_w = {"ffn1_w13": ffn1_w13, "ffn2_w13": ffn2_w13, "ffn1_w2": ffn1_w2, "ffn2_w2": ffn2_w2,
             "w_in": w_in, "w_out": w_out, "conv_pw": conv_pw}

    wl = [_own_slot(_pack_big_half({n: big_w[n][l].astype(_WIRE) for n in _BIG})) for l in range(DEPTH)]
    small_shapes = [(N_META, 256), (DEPTH, CONV_W, 64), (DEPTH, 3, 256), (DEPTH, 3, 256)]
    small_all = _own_slot(_pack_rows([meta, conv_dw, ln_g, ln_b]))
    wrap = lambda f: (lambda al, ins, outs, me: f(al, me))
    wl[0], small_all = _push("gather_ici_0", [wl[0], small_all], [], [],
                             [(m, wrap(s), wrap(d)) for m, s, d in _gather_ici_plan(True)])
    wl[0] = _gather_d2d("gather_d2d_0", wl[0])
    g1_send, g1_recv, (w1_flying,), g1_token = _push_start("gather_ici_1_start", [wl[1]], _gather_ici_plan(False))

    sm = [_unpack_rows(small_all[s], small_shapes) for s in range(N_SHARD)]
    meta_f = jnp.concatenate([sm[s][0] for s in range(N_SHARD)], axis=1)
    cdw_f = jnp.concatenate([sm[s][1] for s in range(N_SHARD)], axis=2)
    lng_f = jnp.concatenate([sm[s][2] for s in range(N_SHARD)], axis=2)
    lnb_f = jnp.concatenate([sm[s][3] for s in range(N_SHARD)], axis=2)

    def mix_params(l):
        wbd = jnp.zeros((D_POOL, D_POOL), f32)
        for g in range(4):
            wbd = wbd.at[64 * g:64 * (g + 1), 64 * g:64 * (g + 1)].set(pool_w[l, g])
        cdw = jnp.pad(cdw_f[l], ((0, 1), (0, 0)))
        cvec = jnp.pad(jnp.stack([conv_db[l], conv_ln_g[l], conv_ln_b[l]]), ((0, 5), (0, 0)))
        wpw = wl[l][:, OFF_PW:OFF_PW + PW_ROWS // 2].reshape(D_CONV, D_CONV)
        return (wbd.astype(_MM), pool_scale[l][None], cdw, cvec, wpw, ret_gn_g[l][None])

    gb_of = lambda l, i: jnp.stack([lng_f[l, i], lnb_f[l, i]])
    gb_in = jnp.stack([ln_in_g, ln_in_b])

    pos = jnp.arange(t, dtype=f32) - PAD
    inv_freq = ROPE_BASE ** (-jnp.arange(0, DH, 2, dtype=f32) / DH)
    ang = pos[:, None] * inv_freq[None, :]
    cs = jnp.concatenate([jnp.cos(ang), jnp.cos(ang)], axis=1)
    sn = jnp.concatenate([-jnp.sin(ang), jnp.sin(ang)], axis=1)

    raw = jnp.concatenate([jnp.zeros((PAD, D), f32), meta_f, x[0]], axis=0)
    target = jnp.concatenate([jnp.zeros((ROW0, D), f32), loss_target[0]], axis=0)
    xh, rstd = _ln_in_fwd(raw)
    cur = (xh, rstd, gb_in + g1_token[0, 0])
    saved = []
    for l in range(DEPTH):
        if l == 1:
            (w1_landed,) = _push_wait("gather_ici_1_wait", g1_send, g1_recv, [w1_flying], cur[0],
                                      _gather_ici_plan(False))
            wl[1] = _gather_d2d("gather_d2d_1", w1_landed)
        a0 = cur
        xh1, r1, au1, hb1 = _ffn_fwd(f"ffn1_fwd_{l}", a0[0], a0[2], wl[l], 0)
        a1 = (xh1, r1, gb_of(l, 0))
        z = _mix_in_fwd(f"mix_in_fwd_{l}", a1[0], a1[2], wl[l])
        mp = mix_params(l)
        ycat, st_in, ut_in, cv_in, yp_in = _mix_core_fwd(f"mix_core_fwd_{l}", z, cs, sn, *mp)
        xt_in = (cv_in, yp_in)
        xh2, r2 = _mix_out_fwd(f"mix_out_fwd_{l}", a1[0], a1[2], ycat, wl[l])
        a2 = (xh2, r2, gb_of(l, 1))
        xh3, r3, au2, hb2 = _ffn_fwd(f"ffn2_fwd_{l}", a2[0], a2[2], wl[l], 1)
        a3 = (xh3, r3, gb_of(l, 2))
        saved.append((a0, a1, a2, a3, z, ycat, st_in, ut_in, xt_in, mp, au1, hb1, au2, hb2))
        cur = a3

    dy, loss_part = _loss_fwd_bwd(cur[0], cur[2], target)
    loss = lax.psum(loss_part[0, 0], ("x", "y", "c"))

    g_ln_g = [[None] * 3 for _ in range(DEPTH)]
    g_ln_b = [[None] * 3 for _ in range(DEPTH)]
    g_small = [dict() for _ in range(DEPTH)]
    slot = lambda j: j

    def ffn_grads(gbuf, tag, l, f, hb, hid, dau, dffn):
        gbuf = _dw_into(f"dw13_{tag}_{l}", gbuf, dau, hb, FF_SLOT, [(0, W13_ROWS, slot, OFF_W13[f])])
        return _dw_into(f"dw2_{tag}_{l}", gbuf, hid, dffn, FF_SLOT,
                        [(0, W2_ROWS, lambda j: 2 * j, OFF_W2[f]), (W2_ROWS, W2_ROWS, lambda j: 2 * j + 1, OFF_W2[f])])

    rs_plan = _rs_ici_plan()
    token = jnp.zeros((), f32)
    for l in reversed(range(DEPTH)):
        a0, a1, a2, a3, z, ycat, st_in, ut_in, xt_in, mp, au1, hb1, au2, hb2 = saved[l]
        gbuf = lax.empty((N_SHARD, SHARD_ROWS, D), _WIRE)
        dh, hid, dau, dffn, dgb = _ffn_bwd(f"ffn2_bwd_{l}", dy, a3[0], a3[1], a3[2] + token, au2, wl[l], 1)
        g_ln_g[l][2], g_ln_b[l][2] = dgb[0], dgb[1]
        gbuf = ffn_grads(gbuf, "ffn2", l, 1, hb2, hid, dau, dffn)
        dh_res, dycat, dsb, dgb = _mix_out_bwd(f"mix_out_bwd_{l}", dh, a2[0], a2[1], a2[2], wl[l])
        g_ln_g[l][1], g_ln_b[l][1] = dgb[0], dgb[1]
        gbuf = _dw_into(f"dw_out_{l}", gbuf, ycat, dsb, D,
                        [(WOUT_ROWS * s, WOUT_ROWS, lambda j, s=s: s, OFF_WOUT) for s in range(N_SHARD)])
        dz, dwbd, dwpw, dcdw, dsm = _mix_core_bwd(f"mix_core_bwd_{l}", z, dycat, cs, sn, st_in, ut_in, *xt_in, *mp)
        pw = dwpw.astype(_WIRE).reshape(N_SHARD, PW_ROWS // 2, D)
        gbuf = lax.dynamic_update_slice(gbuf, jnp.concatenate([pw, jnp.zeros_like(pw)], axis=1), (0, OFF_PW, 0))
        g_small[l] = dict(
            pool_w=jnp.stack([dwbd[64 * g:64 * (g + 1), 64 * g:64 * (g + 1)] for g in range(4)]),
            pool_scale=dsm[1, :256], conv_db=dsm[2, :256], conv_ln_g=dsm[3, :256], conv_ln_b=dsm[4, :256],
            ret_gn_g=dsm[0], conv_dw=dcdw[:CONV_W])
        dh, hb = _mix_in_bwd(f"mix_in_bwd_{l}", dh_res, dz, a1[0], a1[2], wl[l])
        gbuf = _dw_into(f"dw_in_{l}", gbuf, dz, hb, D_IN,
                        [(WIN_ROWS * s, WIN_ROWS, lambda j, s=s: s, OFF_WIN) for s in range(N_SHARD)])
        dh, hid, dau, dffn, dgb = _ffn_bwd(f"ffn1_bwd_{l}", dh, a1[0], a1[1], a1[2], au1, wl[l], 0)
        g_ln_g[l][0], g_ln_b[l][0] = dgb[0], dgb[1]
        gbuf = ffn_grads(gbuf, "ffn1", l, 0, hb1, hid, dau, dffn)
        dy = dh
        p = _rs_pair(str(l), gbuf)
        landing = lax.empty((3, HALF_ROWS, D), _WIRE)
        if l == DEPTH - 1:
            rs_send, rs_recv, flying, rs_token = _push_start("rs_ici_1_start", [p, landing], rs_plan)
            token = rs_token[0, 0]
        else:
            (recv0,) = _push("rs_ici_0", [], [p], [jax.ShapeDtypeStruct(landing.shape, landing.dtype)],
                             [(m, lambda al, ins, outs, me, s=s: s([ins[0], outs[0]], me),
                               lambda al, ins, outs, me, d=d: d([ins[0], outs[0]], me)) for m, s, d in rs_plan])
            g_layer0 = _rs_finish("0", p, recv0)
    p1, recv1 = _push_wait("rs_ici_1_wait", rs_send, rs_recv, flying, dy, rs_plan)
    gsum = [g_layer0, _rs_finish("1", p1, recv1)]
    d_raw, dgb_in = _ln_in_bwd(dy, saved[0][0][0], saved[0][0][1], gb_in)
    grad_x = d_raw[ROW0:][None]

    g_big = [_unpack_big_half(gsum[l]) for l in range(DEPTH)]
    grads = {n: jnp.stack([g_big[l][n] for l in range(DEPTH)]) for n in _BIG}

    small_parts = [
        d_raw[PAD:ROW0],
        jnp.stack([g_small[l]["conv_dw"] for l in range(DEPTH)]),
        jnp.stack([jnp.stack(g_ln_g[l]) for l in range(DEPTH)]),
        jnp.stack([jnp.stack(g_ln_b[l]) for l in range(DEPTH)]),
        dgb_in[0], dgb_in[1],
        jnp.stack([g_small[l]["pool_w"] for l in range(DEPTH)]),
        jnp.stack([g_small[l]["pool_scale"] for l in range(DEPTH)]),
        jnp.stack([g_small[l]["conv_db"] for l in range(DEPTH)]),
        jnp.stack([g_small[l]["conv_ln_g"] for l in range(DEPTH)]),
        jnp.stack([g_small[l]["conv_ln_b"] for l in range(DEPTH)]),
        jnp.stack([g_small[l]["ret_gn_g"] for l in range(DEPTH)]),
    ]
    red = _unpack_rows(_all_reduce_small(_pack_rows(small_parts)), [p.shape for p in small_parts])
    grads["meta"] = lax.dynamic_slice_in_dim(red[0], 256 * chip, 256, axis=1)
    grads["conv_dw"] = lax.dynamic_slice_in_dim(red[1], 64 * chip, 64, axis=2)
    grads["ln_g"] = lax.dynamic_slice_in_dim(red[2], 256 * chip, 256, axis=2)
    grads["ln_b"] = lax.dynamic_slice_in_dim(red[3], 256 * chip, 256, axis=2)
    for n, v in zip(("ln_in_g", "ln_in_b", "pool_w", "pool_scale", "conv_db", "conv_ln_g", "conv_ln_b", "ret_gn_g"),
                    red[4:]):
        grads[n] = v

    names = ['meta', 'ln_in_g', 'ln_in_b', 'ffn1_w13', 'ffn1_w2', 'w_in', 'pool_w', 'pool_scale', 'conv_dw',
             'conv_db', 'conv_ln_g', 'conv_ln_b', 'conv_pw', 'ret_gn_g', 'w_out', 'ffn2_w13', 'ffn2_w2', 'ln_g', 'ln_b']
    ws = dict(meta=meta, ln_in_g=ln_in_g, ln_in_b=ln_in_b, ffn1_w13=ffn1_w13, ffn1_w2=ffn1_w2, w_in=w_in,
              pool_w=pool_w, pool_scale=pool_scale, conv_dw=conv_dw, conv_db=conv_db, conv_ln_g=conv_ln_g,
              conv_ln_b=conv_ln_b, conv_pw=conv_pw, ret_gn_g=ret_gn_g, w_out=w_out, ffn2_w13=ffn2_w13,
              ffn2_w2=ffn2_w2, ln_g=ln_g, ln_b=ln_b)
    ms = dict(meta=m_meta, ln_in_g=m_ln_in_g, ln_in_b=m_ln_in_b, ffn1_w13=m_ffn1_w13, ffn1_w2=m_ffn1_w2,
              w_in=m_w_in, pool_w=m_pool_w, pool_scale=m_pool_scale, conv_dw=m_conv_dw, conv_db=m_conv_db,
              conv_ln_g=m_conv_ln_g, conv_ln_b=m_conv_ln_b, conv_pw=m_conv_pw, ret_gn_g=m_ret_gn_g,
              w_out=m_w_out, ffn2_w13=m_ffn2_w13, ffn2_w2=m_ffn2_w2, ln_g=m_ln_g, ln_b=m_ln_b)
    vs = dict(meta=v_meta, ln_in_g=v_ln_in_g, ln_in_b=v_ln_in_b, ffn1_w13=v_ffn1_w13, ffn1_w2=v_ffn1_w2,
              w_in=v_w_in, pool_w=v_pool_w, pool_scale=v_pool_scale, conv_dw=v_conv_dw, conv_db=v_conv_db,
              conv_ln_g=v_conv_ln_g, conv_ln_b=v_conv_ln_b, conv_pw=v_conv_pw, ret_gn_g=v_ret_gn_g,
              w_out=v_w_out, ffn2_w13=v_ffn2_w13, ffn2_w2=v_ffn2_w2, ln_g=v_ln_g, ln_b=v_ln_b)
    delta, new_m, new_v = {}, {}, {}
    for n in _BIG:
        shp = ws[n].shape
        two = lambda a: a.reshape(-1, shp[-1])
        d_, m_, v_ = _adamw("adamw_" + n, two(ws[n]), two(grads[n]), two(ms[n]), two(vs[n]))
        delta[n], new_m[n], new_v[n] = d_.reshape(shp), m_.reshape(shp), v_.reshape(shp)
    small_names = [n for n in names if n not in _BIG]
    pk = lambda d: _pack_rows([d[n] for n in small_names])
    d_, m_, v_ = _adamw("adamw_small", pk(ws), pk(grads), pk(ms), pk(vs))
    shapes = [ws[n].shape for n in small_names]
    for n, a, b, c in zip(small_names, _unpack_rows(d_, shapes), _unpack_rows(m_, shapes), _unpack_rows(v_, shapes)):
        delta[n], new_m[n], new_v[n] = a, b, c

    return (loss, grad_x, *[grads[n] for n in names], *[delta[n] for n in names],
            *[new_m[n] for n in names], *[new_v[n] for n in names])
```

```python
import functools
import math

import jax
import jax.numpy as jnp
from jax import lax
from jax.experimental import pallas as pl
from jax.experimental.pallas import tpu as pltpu

D = 1024
DEPTH = 2
N_META = 16
PAD = 112
ROW0 = PAD + N_META
D_POOL = 256
D_CONV = 256
D_RET = 512
HEADS = 4
DH = 128
CONV_W = 31
D_FF = 2752
FF_SLOT = 1408
D_FFP = 2 * FF_SLOT
D_IN = 2816
N_SHARD = 4
ALPHA = (2.0 * DEPTH) ** 0.25
LN_EPS = 1e-5
ROPE_BASE = 10000.0
LOG_GAMMA = tuple(math.log(1.0 - 2.0 ** (-5.0 - h)) for h in range(HEADS))
ADAM_LR, ADAM_B1, ADAM_B2, ADAM_EPS, ADAM_WD, ADAM_STEP = 0.001, 0.9, 0.999, 1e-08, 0.01, 10

_MM = jnp.bfloat16
_WIRE = jnp.bfloat16
_VMEM_LIMIT = 56 * 1024 * 1024
_FFN_ROWS = 640

MESH = pl.DeviceIdType.MESH
_ANY = pl.BlockSpec(memory_space=pl.ANY)

W13_ROWS = 1376
W2_ROWS = 688
WIN_ROWS = 704
WOUT_ROWS = 256
PW_ROWS = 16
OFF_W13 = 0
OFF_W2 = W13_ROWS
OFF_WIN = W13_ROWS + W2_ROWS
OFF_WOUT = OFF_WIN + WIN_ROWS
OFF_PW = OFF_WOUT + WOUT_ROWS
A_ROWS = 2080
B_ROWS = 3072
W2_SLOT_OFF = (0, W2_ROWS, FF_SLOT, FF_SLOT + W2_ROWS)


def _start_rows(w_ref, off, n, dst_of, sems, k0):
    cps = []
    for s in range(N_SHARD):
        cp = pltpu.make_async_copy(w_ref.at[s, pl.ds(off, n)], dst_of(s), sems.at[k0 + s])
        cp.start()
        cps.append(cp)
    return cps


def _load_ffn_weights(w_ref, w13, w2, sems):
    cps = _start_rows(w_ref, OFF_W13, W13_ROWS, lambda s: w13.at[s, pl.ds(0, W13_ROWS)], sems, 0)
    cps += _start_rows(w_ref, OFF_W2, W2_ROWS, lambda s: w2.at[pl.ds(W2_SLOT_OFF[s], W2_ROWS)], sems, 4)
    zpad = jnp.zeros((FF_SLOT - W13_ROWS, D), w13.dtype)
    for s in range(N_SHARD):
        w13[s, W13_ROWS:FF_SLOT, :] = zpad
    w2[W13_ROWS:FF_SLOT, :] = zpad
    w2[FF_SLOT + W13_ROWS:D_FFP, :] = zpad
    for cp in cps:
        cp.wait()


def _load_rows(w_ref, off, n, dst, sems):
    for cp in _start_rows(w_ref, off, n, lambda s: dst.at[pl.ds(s * n, n)], sems, 0):
        cp.wait()


def _dot(a, b):
    return jnp.dot(a, b, preferred_element_type=jnp.float32)


def _dot_nt(a, b):
    return lax.dot_general(a, b, (((1,), (1,)), ((), ())), preferred_element_type=jnp.float32)


def _dot_tn(a, b):
    return lax.dot_general(a, b, (((0,), (0,)), ((), ())), preferred_element_type=jnp.float32)


def _params(sem=("arbitrary",)):
    return pltpu.CompilerParams(dimension_semantics=sem, vmem_limit_bytes=_VMEM_LIMIT)


def _row_block(t, cap=640):
    for rb in (640, 320, 128):
        if rb <= cap and t % rb == 0 and (t > 1024 or rb == 128):
            return rb
    raise ValueError(t)


def _rows(rb, n):
    return pl.BlockSpec((rb, n), lambda i: (i, 0))


def _full(shape):
    nd = len(shape)
    return pl.BlockSpec(tuple(shape), lambda i: (0,) * nd, pipeline_mode=pl.Buffered(1))


def _acc(shape):
    nd = len(shape)
    return pl.BlockSpec(tuple(shape), lambda i: (0,) * nd)


def _sigmoid(x):
    return 1.0 / (1.0 + jnp.exp(-x))


def _ln_fwd(s):
    mu = jnp.mean(s, axis=-1, keepdims=True)
    xc = s - mu
    var = jnp.mean(xc * xc, axis=-1, keepdims=True)
    rstd = lax.rsqrt(var + LN_EPS)
    return xc * rstd, rstd


def _ln_bwd(dxh, xh, rstd):
    m1 = jnp.mean(dxh, axis=-1, keepdims=True)
    m2 = jnp.mean(dxh * xh, axis=-1, keepdims=True)
    return rstd * (dxh - m1 - xh * m2)


def _ln_in_fwd(raw):
    t = raw.shape[0]
    rb = _row_block(t)

    def body(raw_ref, xh_ref, rstd_ref):
        xh, rstd = _ln_fwd(raw_ref[...])
        xh_ref[...] = xh
        rstd_ref[...] = rstd

    return pl.pallas_call(
        body, name="ln_in_fwd", grid=(t // rb,),
        in_specs=[_rows(rb, D)],
        out_specs=[_rows(rb, D), _rows(rb, 1)],
        out_shape=[jax.ShapeDtypeStruct((t, D), jnp.float32), jax.ShapeDtypeStruct((t, 1), jnp.float32)],
        compiler_params=_params(),
    )(raw)


def _ffn_fwd(name, xh, gb, wfull):
    t = xh.shape[0]
    rb = _row_block(t, _FFN_ROWS)

    def body(xh_ref, gb_ref, w_ref, out_ref, rstd_ref, au_ref, hb_ref, w13, w2, sems):
        @pl.when(pl.program_id(0) == 0)
        def _():
            _load_ffn_weights(w_ref, w13, w2, sems)

        h = xh_ref[...] * gb_ref[0:1, :] + gb_ref[1:2, :]
        hb = h.astype(_MM)
        hb_ref[...] = hb
        acc = jnp.zeros((rb, D), jnp.float32)
        for j in range(2):
            lo = j * FF_SLOT
            a = _dot_nt(hb, w13[j])
            u = _dot_nt(hb, w13[2 + j])
            au_ref[:, lo:lo + FF_SLOT] = a.astype(_MM)
            au_ref[:, D_FFP + lo:D_FFP + lo + FF_SLOT] = u.astype(_MM)
            hid = (a * _sigmoid(a) * u).astype(_MM)
            acc = acc + _dot(hid, w2[lo:lo + FF_SLOT, :])
        xo, rstd = _ln_fwd(ALPHA * h + 0.5 * acc)
        out_ref[...] = xo
        rstd_ref[...] = rstd

    return pl.pallas_call(
        body, name=name, grid=(t // rb,),
        in_specs=[_rows(rb, D), _full((2, D)), _ANY],
        out_specs=[_rows(rb, D), _rows(rb, 1), _rows(rb, 2 * D_FFP), _rows(rb, D)],
        out_shape=[jax.ShapeDtypeStruct((t, D), jnp.float32), jax.ShapeDtypeStruct((t, 1), jnp.float32),
                   jax.ShapeDtypeStruct((t, 2 * D_FFP), _MM), jax.ShapeDtypeStruct((t, D), _MM)],
        scratch_shapes=[pltpu.VMEM((N_SHARD, FF_SLOT, D), _MM), pltpu.VMEM((D_FFP, D), _MM),
                        pltpu.SemaphoreType.DMA((8,))],
        compiler_params=_params(),
    )(xh, gb, wfull)


def _ffn_bwd(name, dy, xo, rstd, gb_out, au, wfull):
    t = xo.shape[0]
    rb = _row_block(t, 320)

    def body(dy_ref, xo_ref, rstd_ref, gbo_ref, au_ref, w_ref,
             dh_ref, hid_ref, dau_ref, dffn_ref, dgb_ref, w13, w2, sems):
        i = pl.program_id(0)

        @pl.when(i == 0)
        def _():
            dgb_ref[...] = jnp.zeros_like(dgb_ref)
            _load_ffn_weights(w_ref, w13, w2, sems)

        dy = dy_ref[...]
        xo = xo_ref[...]
        dgb_ref[0:1, :] += jnp.sum(dy * xo, axis=0, keepdims=True)
        dgb_ref[1:2, :] += jnp.sum(dy, axis=0, keepdims=True)
        ds = _ln_bwd(dy * gbo_ref[0:1, :], xo, rstd_ref[...])
        dffn = (0.5 * ds).astype(_MM)
        dffn_ref[...] = dffn
        dh = ALPHA * ds
        for j in range(2):
            lo = j * FF_SLOT
            a = au_ref[:, lo:lo + FF_SLOT].astype(jnp.float32)
            u = au_ref[:, D_FFP + lo:D_FFP + lo + FF_SLOT].astype(jnp.float32)
            sg = _sigmoid(a)
            si = a * sg
            hid_ref[:, lo:lo + FF_SLOT] = (si * u).astype(_MM)
            dhid = _dot_nt(dffn, w2[lo:lo + FF_SLOT, :])
            da = (dhid * u * (sg * (1.0 + a * (1.0 - sg)))).astype(_MM)
            du = (dhid * si).astype(_MM)
            dau_ref[:, lo:lo + FF_SLOT] = da
            dau_ref[:, D_FFP + lo:D_FFP + lo + FF_SLOT] = du
            dh = dh + _dot(da, w13[j]) + _dot(du, w13[2 + j])
        dh_ref[...] = dh

    return pl.pallas_call(
        body, name=name, grid=(t // rb,),
        in_specs=[_rows(rb, D), _rows(rb, D), _rows(rb, 1), _full((2, D)), _rows(rb, 2 * D_FFP), _ANY],
        out_specs=[_rows(rb, D), _rows(rb, D_FFP), _rows(rb, 2 * D_FFP), _rows(rb, D), _acc((8, D))],
        out_shape=[jax.ShapeDtypeStruct((t, D), jnp.float32),
                   jax.ShapeDtypeStruct((t, D_FFP), _MM), jax.ShapeDtypeStruct((t, 2 * D_FFP), _MM),
                   jax.ShapeDtypeStruct((t, D), _MM), jax.ShapeDtypeStruct((8, D), jnp.float32)],
        scratch_shapes=[pltpu.VMEM((N_SHARD, FF_SLOT, D), _MM), pltpu.VMEM((D_FFP, D), _MM),
                        pltpu.SemaphoreType.DMA((8,))],
        compiler_params=_params(),
    )(dy, xo, rstd, gb_out, au, wfull)


def _mix_in_fwd(name, xh, gb, wfull):
    t = xh.shape[0]
    rb = _row_block(t)

    def body(xh_ref, gb_ref, w_ref, z_ref, wt, sems):
        @pl.when(pl.program_id(0) == 0)
        def _():
            _load_rows(w_ref, OFF_WIN, WIN_ROWS, wt, sems)

        h = xh_ref[...] * gb_ref[0:1, :] + gb_ref[1:2, :]
        z = _dot_nt(h.astype(_MM), wt[...])
        row = pl.program_id(0) * rb + lax.broadcasted_iota(jnp.int32, (rb, 1), 0)
        z_ref[...] = jnp.where(row >= PAD, z, 0.0)

    return pl.pallas_call(
        body, name=name, grid=(t // rb,),
        in_specs=[_rows(rb, D), _full((2, D)), _ANY],
        out_specs=_rows(rb, D_IN),
        out_shape=jax.ShapeDtypeStruct((t, D_IN), jnp.float32),
        scratch_shapes=[pltpu.VMEM((D_IN, D), _MM), pltpu.SemaphoreType.DMA((4,))],
        compiler_params=_params(),
    )(xh, gb, wfull)


def _mix_in_bwd(name, dh_res, dz, xh, gb, wfull):
    t = xh.shape[0]
    rb = _row_block(t)

    def body(dhr_ref, dz_ref, xh_ref, gb_ref, w_ref, dh_ref, hb_ref, wt, sems):
        @pl.when(pl.program_id(0) == 0)
        def _():
            _load_rows(w_ref, OFF_WIN, WIN_ROWS, wt, sems)

        dh_ref[...] = dhr_ref[...] + _dot(dz_ref[...], wt[...])
        hb_ref[...] = (xh_ref[...] * gb_ref[0:1, :] + gb_ref[1:2, :]).astype(_MM)

    return pl.pallas_call(
        body, name=name, grid=(t // rb,),
        in_specs=[_rows(rb, D), _rows(rb, D_IN), _rows(rb, D), _full((2, D)), _ANY],
        out_specs=[_rows(rb, D), _rows(rb, D)],
        out_shape=[jax.ShapeDtypeStruct((t, D), jnp.float32), jax.ShapeDtypeStruct((t, D), _MM)],
        scratch_shapes=[pltpu.VMEM((D_IN, D), _MM), pltpu.SemaphoreType.DMA((4,))],
        compiler_params=_params(),
    )(dh_res, dz, xh, gb, wfull)


def _mix_out_fwd(name, xh, gb, ycat, wfull):
    t = xh.shape[0]
    rb = _row_block(t)

    def body(xh_ref, gb_ref, y_ref, w_ref, out_ref, rstd_ref, wo, sems):
        @pl.when(pl.program_id(0) == 0)
        def _():
            _load_rows(w_ref, OFF_WOUT, WOUT_ROWS, wo, sems)

        h = xh_ref[...] * gb_ref[0:1, :] + gb_ref[1:2, :]
        xo, rstd = _ln_fwd(ALPHA * h + _dot(y_ref[...], wo[...]))
        out_ref[...] = xo
        rstd_ref[...] = rstd

    return pl.pallas_call(
        body, name=name, grid=(t // rb,),
        in_specs=[_rows(rb, D), _full((2, D)), _rows(rb, D), _ANY],
        out_specs=[_rows(rb, D), _rows(rb, 1)],
        out_shape=[jax.ShapeDtypeStruct((t, D), jnp.float32), jax.ShapeDtypeStruct((t, 1), jnp.float32)],
        scratch_shapes=[pltpu.VMEM((D, D), _MM), pltpu.SemaphoreType.DMA((4,))],
        compiler_params=_params(),
    )(xh, gb, ycat, wfull)


def _mix_out_bwd(name, dy, xo, rstd, gb_out, wfull):
    t = xo.shape[0]
    rb = _row_block(t)

    def body(dy_ref, xo_ref, rstd_ref, gbo_ref, w_ref, dhr_ref, dyc_ref, dsb_ref, dgb_ref, wo, sems):
        @pl.when(pl.program_id(0) == 0)
        def _():
            dgb_ref[...] = jnp.zeros_like(dgb_ref)
            _load_rows(w_ref, OFF_WOUT, WOUT_ROWS, wo, sems)

        dy = dy_ref[...]
        xo = xo_ref[...]
        dgb_ref[0:1, :] += jnp.sum(dy * xo, axis=0, keepdims=True)
        dgb_ref[1:2, :] += jnp.sum(dy, axis=0, keepdims=True)
        ds = _ln_bwd(dy * gbo_ref[0:1, :], xo, rstd_ref[...])
        dsb = ds.astype(_MM)
        dsb_ref[...] = dsb
        dhr_ref[...] = ALPHA * ds
        dyc_ref[...] = _dot_nt(dsb, wo[...])

    return pl.pallas_call(
        body, name=name, grid=(t // rb,),
        in_specs=[_rows(rb, D), _rows(rb, D), _rows(rb, 1), _full((2, D)), _ANY],
        out_specs=[_rows(rb, D), _rows(rb, D), _rows(rb, D), _acc((8, D))],
        out_shape=[jax.ShapeDtypeStruct((t, D), jnp.float32), jax.ShapeDtypeStruct((t, D), jnp.float32),
                   jax.ShapeDtypeStruct((t, D), _MM), jax.ShapeDtypeStruct((8, D), jnp.float32)],
        scratch_shapes=[pltpu.VMEM((D, D), _MM), pltpu.SemaphoreType.DMA((4,))],
        compiler_params=_params(),
    )(dy, xo, rstd, gb_out, wfull)


def _loss_fwd_bwd(xh, gb, target):
    t = xh.shape[0]
    rb = _row_block(t)

    def body(xh_ref, gb_ref, tg_ref, dy_ref, loss_ref):
        @pl.when(pl.program_id(0) == 0)
        def _():
            loss_ref[...] = jnp.zeros_like(loss_ref)

        y = xh_ref[...] * gb_ref[0:1, :] + gb_ref[1:2, :]
        row = pl.program_id(0) * rb + lax.broadcasted_iota(jnp.int32, (rb, 1), 0)
        err = jnp.where(row >= ROW0, y - tg_ref[...], 0.0)
        dy_ref[...] = err * (1.0 / D)
        per_row = jnp.mean(err * err, axis=-1, keepdims=True)
        loss_ref[...] += 0.5 * jnp.sum(per_row, axis=0, keepdims=True)

    return pl.pallas_call(
        body, name="loss", grid=(t // rb,),
        in_specs=[_rows(rb, D), _full((2, D)), _rows(rb, D)],
        out_specs=[_rows(rb, D), _acc((1, 1))],
        out_shape=[jax.ShapeDtypeStruct((t, D), jnp.float32), jax.ShapeDtypeStruct((1, 1), jnp.float32)],
        compiler_params=_params(),
    )(xh, gb, target)


def _ln_in_bwd(dy, xh, rstd, gb):
    t = xh.shape[0]
    rb = _row_block(t)

    def body(dy_ref, xh_ref, rstd_ref, gb_ref, dx_ref, dgb_ref):
        @pl.when(pl.program_id(0) == 0)
        def _():
            dgb_ref[...] = jnp.zeros_like(dgb_ref)

        dy = dy_ref[...]
        xh = xh_ref[...]
        dgb_ref[0:1, :] += jnp.sum(dy * xh, axis=0, keepdims=True)
        dgb_ref[1:2, :] += jnp.sum(dy, axis=0, keepdims=True)
        dx_ref[...] = _ln_bwd(dy * gb_ref[0:1, :], xh, rstd_ref[...])

    return pl.pallas_call(
        body, name="ln_in_bwd", grid=(t // rb,),
        in_specs=[_rows(rb, D), _rows(rb, D), _rows(rb, 1), _full((2, D))],
        out_specs=[_rows(rb, D), _acc((8, D))],
        out_shape=[jax.ShapeDtypeStruct((t, D), jnp.float32), jax.ShapeDtypeStruct((8, D), jnp.float32)],
        compiler_params=_params(),
    )(dy, xh, rstd, gb)


def _dw_rows(t, cols):
    for tt in (2080, 1664, 640, 128):
        vmem = 2 * tt * (cols + D) * 2 + cols * D * 6
        if t % tt == 0 and (t > 1024 or tt == 128) and vmem <= 44 * 1024 * 1024:
            return tt
    raise ValueError((t, cols))


def _dw_into(name, gpack, x, y, cols, pieces):
    t, k = x.shape
    tt = _dw_rows(t, cols)
    nt = t // tt

    def body(x_ref, y_ref, g_in, g_out, acc, stage, sems):
        j = pl.program_id(0)
        s = pl.program_id(1)

        @pl.when(s == 0)
        def _():
            acc[...] = jnp.zeros_like(acc)

        acc[...] += _dot_tn(x_ref[...], y_ref[...])

        @pl.when(s == nt - 1)
        def _():
            stage[...] = acc[...].astype(stage.dtype)
            cps = []
            for q, (lo, n, chip_of, off) in enumerate(pieces):
                cp = pltpu.make_async_copy(stage.at[pl.ds(lo, n)], g_out.at[chip_of(j), pl.ds(off, n)],
                                           sems.at[q])
                cp.start()
                cps.append(cp)
            for cp in cps:
                cp.wait()

    return pl.pallas_call(
        body, name=name, grid=(k // cols, nt),
        in_specs=[pl.BlockSpec((tt, cols), lambda j, s: (s, j)), pl.BlockSpec((tt, D), lambda j, s: (s, 0)), _ANY],
        out_specs=_ANY,
        out_shape=jax.ShapeDtypeStruct(gpack.shape, gpack.dtype),
        input_output_aliases={2: 0},
        scratch_shapes=[pltpu.VMEM((cols, D), jnp.float32), pltpu.VMEM((cols, D), gpack.dtype),
                        pltpu.SemaphoreType.DMA((len(pieces),))],
        compiler_params=_params(("arbitrary", "arbitrary")),
    )(x, y, gpack)


_TAIL_U = 32
_TAIL_X = 16
_MIX_ROWS = 320


def _decay_mask(rb, h):
    ii = lax.broadcasted_iota(jnp.int32, (rb, rb), 0)
    jj = lax.broadcasted_iota(jnp.int32, (rb, rb), 1)
    dist = jnp.abs(ii - jj).astype(jnp.float32)
    vis = (jj >> 6) <= (ii >> 6)
    return jnp.where(vis, jnp.exp(LOG_GAMMA[h] * dist), 0.0)


def _row_decays(rb, h):
    r = lax.broadcasted_iota(jnp.int32, (rb, DH), 0).astype(jnp.float32)
    return jnp.exp(LOG_GAMMA[h] * (r + 1.0)), jnp.exp(LOG_GAMMA[h] * (rb - 1.0 - r))


def _rope(x, cs, sn):
    return x * cs + pltpu.roll(x, DH // 2, 1) * sn


def _rope_t(dx, cs, sn):
    return dx * cs + pltpu.roll(dx * sn, DH // 2, 1)


def _pool_count(blk, rb):
    row = blk * rb + lax.broadcasted_iota(jnp.int32, (rb, D_POOL), 0) - PAD
    lane = lax.broadcasted_iota(jnp.int32, (rb, D_POOL), 1)
    win = jnp.left_shift(2, lane >> 6)
    return jnp.clip(row + 1, 1, win).astype(jnp.float32)


def _pool_select(p2, p4, p8, p16):
    lane = lax.broadcasted_iota(jnp.int32, p2.shape, 1)
    return jnp.where(lane < 64, p2, jnp.where(lane < 128, p4, jnp.where(lane < 192, p8, p16)))


def _window_sums(ext_ref, base, rows, sign):
    acc = ext_ref[pl.ds(base, rows), :]
    outs = []
    for k in range(1, 16):
        acc = acc + ext_ref[pl.ds(base + sign * k, rows), :]
        if k in (1, 3, 7, 15):
            outs.append(acc)
    return _pool_select(*outs)


def _sub_rows(rb):
    return 128 if rb % 128 == 0 else 64


def _mix_core_fwd(name, z, cs, sn, wbd, pscale, cdw, cvec, wpw, gn):
    t = z.shape[0]
    rb = _row_block(t, _MIX_ROWS)
    nblk = t // rb
    sr = _sub_rows(rb)

    def body(z_ref, cs_ref, sn_ref, wbd_ref, ps_ref, cdw_ref, cvec_ref, wpw_ref, gn_ref,
             y_ref, st_ref, ut_ref, cv_ref, yp_ref,
             uext, xext, cv, ypre, state, wmask):
        i = pl.program_id(0)

        @pl.when(i == 0)
        def _():
            state[...] = jnp.zeros_like(state)
            uext[0:_TAIL_U, :] = jnp.zeros((_TAIL_U, D_CONV), jnp.float32)
            xext[0:_TAIL_X, :] = jnp.zeros((_TAIL_X, D_POOL), jnp.float32)
            for h in range(HEADS):
                wmask[h] = _decay_mask(rb, h)

        st_ref[0] = state[...]
        ut_ref[0] = uext[0:_TAIL_U, :]

        xp = z_ref[:, 0:256]
        uext[_TAIL_U:_TAIL_U + rb, :] = z_ref[:, 256:512] * _sigmoid(z_ref[:, 512:768])
        xext[_TAIL_X:_TAIL_X + rb, :] = xp

        for r in range(0, rb, sr):
            win = _window_sums(xext, _TAIL_X + r, sr, -1)
            ypre[r:r + sr, :] = win
            acc = jnp.zeros((sr, D_CONV), jnp.float32)
            for k in range(CONV_W):
                acc = acc + uext[pl.ds(_TAIL_U + r - k, sr), :] * cdw_ref[CONV_W - 1 - k:CONV_W - k, :]
            cv[r:r + sr, :] = acc

        ypb = (ypre[...] / _pool_count(i, rb) - xp).astype(_MM)
        yp_ref[...] = ypb
        y_ref[:, 0:256] = (_dot(ypb, wbd_ref[...]) * ps_ref[...]).astype(_MM)
        cv_ref[...] = cv[...]
        cn, _ = _ln_fwd(cv[...] + cvec_ref[0:1, :])
        ln = cn * cvec_ref[1:2, :] + cvec_ref[2:3, :]
        sw = ln * _sigmoid(ln)
        y_ref[:, 256:512] = _dot(sw.astype(_MM), wpw_ref[...]).astype(_MM)
        csv = cs_ref[...]
        snv = sn_ref[...]
        for h in range(HEADS):
            q = _rope(z_ref[:, 768 + h * DH:768 + (h + 1) * DH], csv, snv)
            k = _rope(z_ref[:, 1280 + h * DH:1280 + (h + 1) * DH], csv, snv) * (DH ** -0.5)
            vb = z_ref[:, 1792 + h * DH:1792 + (h + 1) * DH].astype(_MM)
            g = z_ref[:, 2304 + h * DH:2304 + (h + 1) * DH]
            a, b = _row_decays(rb, h)
            s = _dot_nt(q.astype(_MM), k.astype(_MM)) * wmask[h]
            o = _dot(s.astype(_MM), vb) + _dot((q * a).astype(_MM), state[h].astype(_MM))
            state[h] = math.exp(LOG_GAMMA[h] * rb) * state[h] + _dot_tn((k * b).astype(_MM), vb)
            on, _ = _ln_fwd(o)
            y_ref[:, 512 + h * DH:512 + (h + 1) * DH] = (
                g * _sigmoid(g) * on * gn_ref[:, h * DH:(h + 1) * DH]).astype(_MM)

        uext[0:_TAIL_U, :] = uext[rb:rb + _TAIL_U, :]
        xext[0:_TAIL_X, :] = xext[rb:rb + _TAIL_X, :]

    return pl.pallas_call(
        body, name=name, grid=(nblk,),
        in_specs=[_rows(rb, D_IN), _rows(rb, DH), _rows(rb, DH), _full((256, 256)), _full((1, 256)),
                  _full((32, 256)), _full((8, 256)), _full((256, 256)), _full((1, D_RET))],
        out_specs=[_rows(rb, D),
                   pl.BlockSpec((1, HEADS, DH, DH), lambda i: (i, 0, 0, 0)),
                   pl.BlockSpec((1, _TAIL_U, D_CONV), lambda i: (i, 0, 0)),
                   _rows(rb, D_CONV), _rows(rb, D_POOL)],
        out_shape=[jax.ShapeDtypeStruct((t, D), _MM),
                   jax.ShapeDtypeStruct((nblk, HEADS, DH, DH), jnp.float32),
                   jax.ShapeDtypeStruct((nblk, _TAIL_U, D_CONV), jnp.float32),
                   jax.ShapeDtypeStruct((t, D_CONV), jnp.float32),
                   jax.ShapeDtypeStruct((t, D_POOL), _MM)],
        scratch_shapes=[pltpu.VMEM((rb + _TAIL_U, D_CONV), jnp.float32),
                        pltpu.VMEM((rb + _TAIL_X, D_POOL), jnp.float32),
                        pltpu.VMEM((rb, D_CONV), jnp.float32),
                        pltpu.VMEM((rb, D_POOL), jnp.float32),
                        pltpu.VMEM((HEADS, DH, DH), jnp.float32),
                        pltpu.VMEM((HEADS, rb, rb), jnp.float32)],
        compiler_params=_params(),
    )(z, cs, sn, wbd, pscale, cdw, cvec, wpw, gn)


def _mix_core_bwd(name, z, dyc, cs, sn, st_in, ut_in, cv_in, yp_in, wbd, pscale, cdw, cvec, wpw, gn):
    t = z.shape[0]
    rb = _row_block(t, _MIX_ROWS)
    nblk = t // rb
    sr = _sub_rows(rb)
    rev = lambda i: nblk - 1 - i

    def body(z_ref, dy_ref, cs_ref, sn_ref, st_ref, ut_ref, cv_ref, yp_ref,
             wbd_ref, ps_ref, cdw_ref, cvec_ref, wpw_ref, gn_ref,
             dz_ref, dwbd_ref, dwpw_ref, dcdw_ref, dsm_ref,
             uext, cv, ypre, dcvext, eext, dstate, wmask):
        i = pl.program_id(0)
        blk = nblk - 1 - i

        @pl.when(i == 0)
        def _():
            dstate[...] = jnp.zeros_like(dstate)
            dcvext[rb:rb + _TAIL_U, :] = jnp.zeros((_TAIL_U, D_CONV), jnp.float32)
            eext[rb:rb + _TAIL_X, :] = jnp.zeros((_TAIL_X, D_POOL), jnp.float32)
            dwbd_ref[...] = jnp.zeros_like(dwbd_ref)
            dwpw_ref[...] = jnp.zeros_like(dwpw_ref)
            dcdw_ref[...] = jnp.zeros_like(dcdw_ref)
            dsm_ref[...] = jnp.zeros_like(dsm_ref)
            for h in range(HEADS):
                wmask[h] = _decay_mask(rb, h)

        row = blk * rb + lax.broadcasted_iota(jnp.int32, (rb, 1), 0)
        live = row >= PAD

        ca = z_ref[:, 256:512]
        sg_c = _sigmoid(z_ref[:, 512:768])
        uext[0:_TAIL_U, :] = ut_ref[0]
        uext[_TAIL_U:_TAIL_U + rb, :] = ca * sg_c

        cnt = _pool_count(blk, rb)
        ypb = yp_ref[...]
        dyp = dy_ref[:, 0:256]
        pm = _dot(ypb, wbd_ref[...])
        dsm_ref[1:2, 0:256] += jnp.sum(dyp * pm, axis=0, keepdims=True)
        dpm = (dyp * ps_ref[...]).astype(_MM)
        dwbd_ref[...] += _dot_tn(ypb, dpm)
        dypre = _dot_nt(dpm, wbd_ref[...])
        eext[0:rb, :] = dypre / cnt
        for r in range(0, rb, sr):
            ypre[r:r + sr, :] = _window_sums(eext, r, sr, 1)
        dz_ref[:, 0:256] = jnp.where(live, ypre[...] - dypre, 0.0).astype(_MM)

        cn, rstd_c = _ln_fwd(cv_ref[...] + cvec_ref[0:1, :])
        ln = cn * cvec_ref[1:2, :] + cvec_ref[2:3, :]
        sg_l = _sigmoid(ln)
        swb = (ln * sg_l).astype(_MM)
        dycb = dy_ref[:, 256:512].astype(_MM)
        dwpw_ref[...] += _dot_tn(swb, dycb)
        dln = _dot_nt(dycb, wpw_ref[...]) * (sg_l * (1.0 + ln * (1.0 - sg_l)))
        dsm_ref[3:4, 0:256] += jnp.sum(dln * cn, axis=0, keepdims=True)
        dsm_ref[4:5, 0:256] += jnp.sum(dln, axis=0, keepdims=True)
        dcv = _ln_bwd(dln * cvec_ref[1:2, :], cn, rstd_c)
        dsm_ref[2:3, 0:256] += jnp.sum(dcv, axis=0, keepdims=True)
        dcvext[0:rb, :] = dcv
        for k in range(CONV_W):
            prod = dcv * uext[pl.ds(_TAIL_U - k, rb), :]
            dcdw_ref[CONV_W - 1 - k:CONV_W - k, :] += jnp.sum(prod, axis=0, keepdims=True)
        for r in range(0, rb, sr):
            acc = jnp.zeros((sr, D_CONV), jnp.float32)
            for k in range(CONV_W):
                acc = acc + dcvext[pl.ds(r + k, sr), :] * cdw_ref[CONV_W - 1 - k:CONV_W - k, :]
            cv[r:r + sr, :] = acc
        du = cv[...]
        dz_ref[:, 256:512] = jnp.where(live, du * sg_c, 0.0).astype(_MM)
        dz_ref[:, 512:768] = jnp.where(live, du * ca * sg_c * (1.0 - sg_c), 0.0).astype(_MM)

        csv = cs_ref[...]
        snv = sn_ref[...]
        for h in range(HEADS):
            q = _rope(z_ref[:, 768 + h * DH:768 + (h + 1) * DH], csv, snv)
            k = _rope(z_ref[:, 1280 + h * DH:1280 + (h + 1) * DH], csv, snv) * (DH ** -0.5)
            vb = z_ref[:, 1792 + h * DH:1792 + (h + 1) * DH].astype(_MM)
            g = z_ref[:, 2304 + h * DH:2304 + (h + 1) * DH]
            a, b = _row_decays(rb, h)
            qb = q.astype(_MM)
            kb = k.astype(_MM)
            qab = (q * a).astype(_MM)
            kbb = (k * b).astype(_MM)
            stb = st_ref[0, h].astype(_MM)
            sb = (_dot_nt(qb, kb) * wmask[h]).astype(_MM)
            o = _dot(sb, vb) + _dot(qab, stb)
            on, rstd_o = _ln_fwd(o)
            gnv = gn_ref[:, h * DH:(h + 1) * DH]
            sg_g = _sigmoid(g)
            si_g = g * sg_g
            dyr = dy_ref[:, 512 + h * DH:512 + (h + 1) * DH]
            dsm_ref[0:1, h * DH:(h + 1) * DH] += jnp.sum(dyr * on * si_g, axis=0, keepdims=True)
            dgate = dyr * on * gnv * (sg_g * (1.0 + g * (1.0 - sg_g)))
            dob = _ln_bwd(dyr * gnv * si_g, on, rstd_o).astype(_MM)
            dstb = dstate[h].astype(_MM)
            dsb = (_dot_nt(dob, vb) * wmask[h]).astype(_MM)
            dq = _dot(dsb, kb) + _dot_nt(dob, stb) * a
            dk = _dot_tn(dsb, qb) + _dot_nt(vb, dstb) * b
            dv = _dot_tn(sb, dob) + _dot(kbb, dstb)
            dstate[h] = math.exp(LOG_GAMMA[h] * rb) * dstate[h] + _dot_tn(qab, dob)
            dz_ref[:, 768 + h * DH:768 + (h + 1) * DH] = jnp.where(live, _rope_t(dq, csv, snv), 0.0).astype(_MM)
            dz_ref[:, 1280 + h * DH:1280 + (h + 1) * DH] = jnp.where(
                live, _rope_t(dk * (DH ** -0.5), csv, snv), 0.0).astype(_MM)
            dz_ref[:, 1792 + h * DH:1792 + (h + 1) * DH] = jnp.where(live, dv, 0.0).astype(_MM)
            dz_ref[:, 2304 + h * DH:2304 + (h + 1) * DH] = jnp.where(live, dgate, 0.0).astype(_MM)

        dcvext[rb:rb + _TAIL_U, :] = dcvext[0:_TAIL_U, :]
        eext[rb:rb + _TAIL_X, :] = eext[0:_TAIL_X, :]

    rrows = lambda n: pl.BlockSpec((rb, n), lambda i: (rev(i), 0))
    return pl.pallas_call(
        body, name=name, grid=(nblk,),
        in_specs=[rrows(D_IN), rrows(D), rrows(DH), rrows(DH),
                  pl.BlockSpec((1, HEADS, DH, DH), lambda i: (rev(i), 0, 0, 0)),
                  pl.BlockSpec((1, _TAIL_U, D_CONV), lambda i: (rev(i), 0, 0)),
                  rrows(D_CONV), rrows(D_POOL),
                  _full((256, 256)), _full((1, 256)), _full((32, 256)), _full((8, 256)), _full((256, 256)),
                  _full((1, D_RET))],
        out_specs=[rrows(D_IN), _acc((256, 256)), _acc((256, 256)), _acc((32, 256)), _acc((8, 512))],
        out_shape=[jax.ShapeDtypeStruct((t, D_IN), _MM),
                   jax.ShapeDtypeStruct((256, 256), jnp.float32), jax.ShapeDtypeStruct((256, 256), jnp.float32),
                   jax.ShapeDtypeStruct((32, 256), jnp.float32), jax.ShapeDtypeStruct((8, 512), jnp.float32)],
        scratch_shapes=[pltpu.VMEM((rb + _TAIL_U, D_CONV), jnp.float32),
                        pltpu.VMEM((rb, D_CONV), jnp.float32),
                        pltpu.VMEM((rb, D_POOL), jnp.float32),
                        pltpu.VMEM((rb + _TAIL_U, D_CONV), jnp.float32),
                        pltpu.VMEM((rb + _TAIL_X, D_POOL), jnp.float32),
                        pltpu.VMEM((HEADS, DH, DH), jnp.float32),
                        pltpu.VMEM((HEADS, rb, rb), jnp.float32)],
        compiler_params=_params(),
    )(z, dyc, cs, sn, st_in, ut_in, cv_in, yp_in, wbd, pscale, cdw, cvec, wpw, gn)


def _me():
    return lax.axis_index("x"), lax.axis_index("y"), lax.axis_index("c")


def _flip(me, mask):
    return tuple(1 - m if f else m for m, f in zip(me, mask))


def _push(name, aliased, inputs, fresh, remote):
    n_al, n_in, n_out, n_rem = len(aliased), len(inputs), len(fresh), len(remote)

    def body(*refs):
        ins = refs[n_al:n_al + n_in]
        al = refs[n_al + n_in:2 * n_al + n_in]
        outs = refs[2 * n_al + n_in:2 * n_al + n_in + n_out]
        send_sems, recv_sems = refs[2 * n_al + n_in + n_out:]
        me = _me()
        copies = []
        for k, (mask, src_fn, dst_fn) in enumerate(remote):
            cp = pltpu.make_async_remote_copy(
                src_ref=src_fn(al, ins, outs, me), dst_ref=dst_fn(al, ins, outs, me),
                send_sem=send_sems.at[k], recv_sem=recv_sems.at[k],
                device_id=_flip(me, mask), device_id_type=MESH)
            cp.start()
            copies.append(cp)
        for cp in copies:
            cp.wait()

    return pl.pallas_call(
        body, name=name,
        in_specs=[_ANY] * (n_al + n_in), out_specs=[_ANY] * (n_al + n_out),
        out_shape=[jax.ShapeDtypeStruct(a.shape, a.dtype) for a in aliased] + list(fresh),
        input_output_aliases={i: i for i in range(n_al)},
        scratch_shapes=[pltpu.SemaphoreType.DMA((n_rem,)), pltpu.SemaphoreType.DMA((n_rem,))],
    )(*aliased, *inputs)


_HBM = pl.BlockSpec(memory_space=pltpu.HBM)
_SEM = pl.BlockSpec(memory_space=pltpu.SEMAPHORE)
_EFFECT = pltpu.SideEffectType.DATAFLOW_SIDE_EFFECTING


def _push_start(name, bufs, remote):
    n, n_rem = len(bufs), len(remote)

    def body(*refs):
        ins = refs[:n]
        send_sems, recv_sems = refs[n], refs[n + 1]
        token = refs[2 * n + 2]
        me = _me()
        for k, (mask, src_fn, dst_fn) in enumerate(remote):
            pltpu.make_async_remote_copy(
                src_ref=src_fn(ins, me), dst_ref=dst_fn(ins, me),
                send_sem=send_sems.at[k], recv_sem=recv_sems.at[k],
                device_id=_flip(me, mask), device_id_type=MESH).start()
        token[...] = jnp.zeros_like(token)

    out = pl.pallas_call(
        body, name=name,
        out_shape=(pltpu.SemaphoreType.DMA((n_rem,)), pltpu.SemaphoreType.DMA((n_rem,)),
                   *[pltpu.HBM(b.shape, b.dtype) for b in bufs], jax.ShapeDtypeStruct((8, 128), jnp.float32)),
        in_specs=[_HBM] * n,
        out_specs=(_SEM, _SEM, *[_HBM] * n, pl.BlockSpec(memory_space=pltpu.VMEM)),
        input_output_aliases={i: i + 2 for i in range(n)},
        compiler_params=pltpu.CompilerParams(has_side_effects=_EFFECT),
    )(*[pltpu.with_memory_space_constraint(b, pltpu.HBM) for b in bufs])
    return out[0], out[1], list(out[2:2 + n]), out[2 + n]


def _push_wait(name, send_sems, recv_sems, bufs, after, remote):
    n = len(bufs)

    def body(*refs):
        ins = refs[:n]
        s_sems, r_sems = refs[n], refs[n + 1]
        me = _me()
        for k, (mask, src_fn, dst_fn) in enumerate(remote):
            cp = pltpu.make_async_remote_copy(
                src_ref=src_fn(ins, me), dst_ref=dst_fn(ins, me),
                send_sem=s_sems.at[k], recv_sem=r_sems.at[k],
                device_id=_flip(me, mask), device_id_type=MESH)
            cp.wait_send()
            cp.wait_recv()

    out = pl.pallas_call(
        body, name=name,
        out_shape=tuple(pltpu.HBM(b.shape, b.dtype) for b in bufs),
        in_specs=[_HBM] * n + [_SEM, _SEM, _ANY], out_specs=tuple([_HBM] * n),
        input_output_aliases={i: i for i in range(n)},
        compiler_params=pltpu.CompilerParams(has_side_effects=_EFFECT),
    )(*bufs, send_sems, recv_sems, after)
    return list(out)


_ICI_MASKS = ((0, 1, 0), (1, 0, 0), (1, 1, 0))
_D2D_MASK = (0, 0, 1)
_ALL_MASKS = tuple((a, b, c) for a in (0, 1) for b in (0, 1) for c in (0, 1))[1:]


def _chip(me):
    return 2 * me[0] + me[1]


def _half(me, rows):
    return pl.ds(me[2] * (rows // 2), rows // 2)


def _other_half(me, rows):
    return pl.ds((1 - me[2]) * (rows // 2), rows // 2)


def _sum_block(rh):
    return next((b for b in (768, 640, 512, 128) if rh % b == 0), rh)


def _own_slot(mine):
    chip = _chip(_me())
    return lax.dynamic_update_slice(lax.empty((N_SHARD,) + mine.shape, mine.dtype), mine[None],
                                    (chip,) + (0,) * mine.ndim)


def _gather_ici_plan(rows, with_small):
    remote = []
    for mask in _ICI_MASKS:
        for b, r in enumerate(rows):
            mine = lambda bufs, me, b=b, r=r: bufs[b].at[_chip(me), _half(me, r)]
            remote.append((mask, mine, mine))
        if with_small:
            mine_small = lambda bufs, me: bufs[len(rows)].at[_chip(me)]
            remote.append((mask, mine_small, mine_small))
    return remote


def _gather_d2d(name, ws):
    remote = []
    for b, w in enumerate(ws):
        for j in range(1, N_SHARD):
            theirs = lambda al, ins, outs, me, j=j, b=b, r=w.shape[1]: al[b].at[(_chip(me) + j) % N_SHARD, _half(me, r)]
            remote.append((_D2D_MASK, theirs, theirs))
    return _push(name, list(ws), [], [], remote)


def _sum_pair(name, g, recv):
    _, r, _ = g.shape
    rb = _sum_block(r // 2)
    nb = r // 2 // rb
    c = lax.axis_index("c").astype(jnp.int32).reshape(1)

    def body(c_ref, g_ref, r_ref, o_ref):
        o_ref[...] = (g_ref[...].astype(jnp.float32) + r_ref[...].astype(jnp.float32)).astype(o_ref.dtype)

    return pl.pallas_call(
        body, name=name,
        grid_spec=pltpu.PrefetchScalarGridSpec(
            num_scalar_prefetch=1, grid=(N_SHARD, nb),
            in_specs=[pl.BlockSpec((None, rb, D), lambda s, i, c_ref: (s, c_ref[0] * nb + i, 0)),
                      pl.BlockSpec((None, rb, D), lambda s, i, c_ref: (s, i, 0))],
            out_specs=pl.BlockSpec((None, rb, D), lambda s, i, c_ref: (s, i, 0))),
        out_shape=jax.ShapeDtypeStruct((N_SHARD, r // 2, D), g.dtype),
        compiler_params=_params(("arbitrary", "arbitrary")),
    )(c, g, recv)


def _sum_chips(name, p, recv):
    _, rh, _ = p.shape
    rb = _sum_block(rh)
    nb = rh // rb
    s = jnp.stack([2 * lax.axis_index("x") + lax.axis_index("y"), lax.axis_index("c")]).astype(jnp.int32)

    def body(s_ref, p_ref, r_ref, o_ref):
        acc = p_ref[...].astype(jnp.float32)
        for j in range(3):
            acc = acc + r_ref[j].astype(jnp.float32)
        o_ref[...] = acc

    return pl.pallas_call(
        body, name=name,
        grid_spec=pltpu.PrefetchScalarGridSpec(
            num_scalar_prefetch=1, grid=(nb,),
            in_specs=[pl.BlockSpec((None, rb, D), lambda i, s_ref: (s_ref[0], i, 0)),
                      pl.BlockSpec((3, rb, D), lambda i, s_ref: (0, i, 0))],
            out_specs=pl.BlockSpec((rb, D), lambda i, s_ref: (s_ref[1] * nb + i, 0))),
        out_shape=jax.ShapeDtypeStruct((2 * rh, D), jnp.float32),
        compiler_params=_params(),
    )(s, p, recv)


def _rs_ici_plan():
    remote = []
    for j, mask in enumerate(_ICI_MASKS):
        remote.append((mask,
                       lambda bufs, me, mask=mask: bufs[0].at[_chip(_flip(me, mask))],
                       lambda bufs, me, j=j: bufs[1].at[j]))
    return remote


def _rs_pair(tag, g):
    _, r, _ = g.shape
    remote = [(_D2D_MASK,
               lambda al, ins, outs, me, s=s: ins[0].at[s, _other_half(me, r)],
               lambda al, ins, outs, me, s=s: outs[0].at[s]) for s in range(N_SHARD)]
    (recv,) = _push("rs_d2d_" + tag, [], [g], [jax.ShapeDtypeStruct((N_SHARD, r // 2, D), g.dtype)], remote)
    return _sum_pair("rs_sum_pair_" + tag, g, recv)


def _rs_start(tag, g, rs_plan):
    p = _rs_pair(tag, g)
    landing = lax.empty((3, p.shape[1], D), p.dtype)
    send, recv, flying, token = _push_start("rs_ici_" + tag + "_start", [p, landing], rs_plan)
    return (tag, send, recv, flying), token[0, 0]


def _rs_end(started, after, rs_plan):
    tag, send, recv, flying = started
    p, recv3 = _push_wait("rs_ici_" + tag + "_wait", send, recv, flying, after, rs_plan)
    mine = _sum_chips("rs_sum_chips_" + tag, p, recv3)
    r = mine.shape[0]
    half = lambda al, ins, outs, me: al[0].at[_half(me, r)]
    (both,) = _push("rs_share_" + tag, [mine], [], [], [(_D2D_MASK, half, half)])
    return both


def _all_reduce_small(v):
    s = v.shape[0]
    me = _me()
    every = lax.dynamic_update_slice(lax.empty((8, s, D), jnp.float32), v[None], (4 * me[0] + 2 * me[1] + me[2], 0, 0))
    slot = lambda al, ins, outs, me: al[0].at[4 * me[0] + 2 * me[1] + me[2]]
    (every,) = _push("small_all", [every], [], [], [(mask, slot, slot) for mask in _ALL_MASKS])

    def body(e_ref, o_ref):
        acc = e_ref[0]
        for j in range(1, 8):
            acc = acc + e_ref[j]
        o_ref[...] = acc

    return pl.pallas_call(
        body, name="small_sum", grid=(1,),
        in_specs=[pl.BlockSpec((8, s, D), lambda i: (0, 0, 0))],
        out_specs=pl.BlockSpec((s, D), lambda i: (0, 0)),
        out_shape=jax.ShapeDtypeStruct((s, D), jnp.float32),
        compiler_params=_params(),
    )(every)


def _adamw(name, w, g, m, v):
    r, c = w.shape
    rb = next(b for b in (256, 344, 128, 64, 32, 16, 8, r) if r % b == 0)

    def body(w_ref, g_ref, m_ref, v_ref, d_ref, mo_ref, vo_ref):
        g = g_ref[...]
        m = ADAM_B1 * m_ref[...] + (1.0 - ADAM_B1) * g
        v = ADAM_B2 * v_ref[...] + (1.0 - ADAM_B2) * (g * g)
        m_hat = m / (1.0 - ADAM_B1 ** ADAM_STEP)
        v_hat = v / (1.0 - ADAM_B2 ** ADAM_STEP)
        d_ref[...] = -ADAM_LR * (m_hat / (jnp.sqrt(v_hat) + ADAM_EPS) + ADAM_WD * w_ref[...])
        mo_ref[...] = m
        vo_ref[...] = v

    spec = pl.BlockSpec((rb, c), lambda i: (i, 0))
    return pl.pallas_call(
        body, name=name, grid=(r // rb,),
        in_specs=[spec] * 4, out_specs=[spec] * 3,
        out_shape=[jax.ShapeDtypeStruct((r, c), jnp.float32)] * 3,
        compiler_params=_params(),
    )(w, g, m, v)


_BIG = ("ffn1_w13", "ffn2_w13", "ffn1_w2", "ffn2_w2", "w_in", "w_out", "conv_pw")
_BIG_SHARD = {"ffn1_w13": (D, 1376), "ffn2_w13": (D, 1376), "ffn1_w2": (688, D), "ffn2_w2": (688, D),
              "w_in": (D, 704), "w_out": (256, D), "conv_pw": (64, 256)}


def _pack_rows(parts):
    flat = jnp.concatenate([p.reshape(-1) for p in parts])
    pad = (-flat.shape[0]) % (8 * D)
    if pad:
        flat = jnp.concatenate([flat, jnp.zeros((pad,), flat.dtype)])
    return flat.reshape(-1, D)


def _unpack_rows(buf, shapes):
    flat = buf.reshape(-1)
    out, off = [], 0
    for shp in shapes:
        n = math.prod(shp)
        out.append(flat[off:off + n].reshape(shp))
        off += n
    return out


def _pack_shard(parts):
    zeros = lambda n: jnp.zeros((n, D), parts["w_out"].dtype)
    a = jnp.concatenate([parts["ffn1_w13"].T, parts["ffn1_w2"], zeros(A_ROWS - OFF_WIN)], axis=0)
    b = jnp.concatenate([parts["ffn2_w13"].T, parts["ffn2_w2"], parts["w_in"].T, parts["w_out"],
                         parts["conv_pw"].reshape(PW_ROWS, D), zeros(B_ROWS - OFF_PW - PW_ROWS)], axis=0)
    return a, b


def _unpack_shard(a, b):
    return {"ffn1_w13": a[OFF_W13:OFF_W13 + W13_ROWS].T, "ffn1_w2": a[OFF_W2:OFF_W2 + W2_ROWS],
            "ffn2_w13": b[OFF_W13:OFF_W13 + W13_ROWS].T, "ffn2_w2": b[OFF_W2:OFF_W2 + W2_ROWS],
            "w_in": b[OFF_WIN:OFF_WIN + WIN_ROWS].T, "w_out": b[OFF_WOUT:OFF_WOUT + WOUT_ROWS],
            "conv_pw": b[OFF_PW:OFF_PW + PW_ROWS].reshape(64, 256)}


def kernel(x, meta, ln_in_g, ln_in_b, ffn1_w13, ffn1_w2, w_in, pool_w, pool_scale, conv_dw, conv_db, conv_ln_g, conv_ln_b, conv_pw, ret_gn_g, w_out, ffn2_w13, ffn2_w2, ln_g, ln_b, loss_target, m_meta, m_ln_in_g, m_ln_in_b, m_ffn1_w13, m_ffn1_w2, m_w_in, m_pool_w, m_pool_scale, m_conv_dw, m_conv_db, m_conv_ln_g, m_conv_ln_b, m_conv_pw, m_ret_gn_g, m_w_out, m_ffn2_w13, m_ffn2_w2, m_ln_g, m_ln_b, v_meta, v_ln_in_g, v_ln_in_b, v_ffn1_w13, v_ffn1_w2, v_w_in, v_pool_w, v_pool_scale, v_conv_dw, v_conv_db, v_conv_ln_g, v_conv_ln_b, v_conv_pw, v_ret_gn_g, v_w_out, v_ffn2_w13, v_ffn2_w2, v_ln_g, v_ln_b):
    f32 = jnp.float32
    seq = x.shape[1]
    t = seq + ROW0
    me = _me()
    chip = _chip(me)
    big_w = {"ffn1_w13": ffn1_w13, "ffn2_w13": ffn2_w13, "ffn1_w2": ffn1_w2, "ffn2_w2": ffn2_w2,
             "w_in": w_in, "w_out": w_out, "conv_pw": conv_pw}

    wa, wb = [], []
    for l in range(DEPTH):
        a, b = _pack_shard({n: big_w[n][l].astype(_WIRE) for n in _BIG})
        wa.append(_own_slot(a))
        wb.append(_own_slot(b))
    small_shapes = [(N_META, 256), (DEPTH, CONV_W, 64), (DEPTH, 3, 256), (DEPTH, 3, 256)]
    small_all = _own_slot(_pack_rows([meta, conv_dw, ln_g, ln_b]))
    wrap = lambda f: (lambda al, ins, outs, me: f(al, me))
    wa[0], small_all = _push("gather_ici_a0", [wa[0], small_all], [], [],
                             [(m, wrap(s), wrap(d)) for m, s, d in _gather_ici_plan([A_ROWS], True)])
    (wa[0],) = _gather_d2d("gather_d2d_a0", [wa[0]])
    plan_b0 = _gather_ici_plan([B_ROWS], False)
    b0_send, b0_recv, b0_flying, b0_token = _push_start("gather_ici_b0_start", [wb[0]], plan_b0)
    plan_l1 = _gather_ici_plan([A_ROWS, B_ROWS], False)

    sm = [_unpack_rows(small_all[s], small_shapes) for s in range(N_SHARD)]
    meta_f = jnp.concatenate([sm[s][0] for s in range(N_SHARD)], axis=1)
    cdw_f = jnp.concatenate([sm[s][1] for s in range(N_SHARD)], axis=2)
    lng_f = jnp.concatenate([sm[s][2] for s in range(N_SHARD)], axis=2)
    lnb_f = jnp.concatenate([sm[s][3] for s in range(N_SHARD)], axis=2)

    def mix_params(l):
        wbd = jnp.zeros((D_POOL, D_POOL), f32)
        for g in range(4):
            wbd = wbd.at[64 * g:64 * (g + 1), 64 * g:64 * (g + 1)].set(pool_w[l, g])
        cdw = jnp.pad(cdw_f[l], ((0, 1), (0, 0)))
        cvec = jnp.pad(jnp.stack([conv_db[l], conv_ln_g[l], conv_ln_b[l]]), ((0, 5), (0, 0)))
        wpw = wb[l][:, OFF_PW:OFF_PW + PW_ROWS].reshape(D_CONV, D_CONV)
        return (wbd.astype(_MM), pool_scale[l][None], cdw, cvec, wpw, ret_gn_g[l][None])

    gb_of = lambda l, i: jnp.stack([lng_f[l, i], lnb_f[l, i]])
    gb_in = jnp.stack([ln_in_g, ln_in_b])

    pos = jnp.arange(t, dtype=f32) - PAD
    inv_freq = ROPE_BASE ** (-jnp.arange(0, DH, 2, dtype=f32) / DH)
    ang = pos[:, None] * inv_freq[None, :]
    cs = jnp.concatenate([jnp.cos(ang), jnp.cos(ang)], axis=1)
    sn = jnp.concatenate([-jnp.sin(ang), jnp.sin(ang)], axis=1)

    raw = jnp.concatenate([jnp.zeros((PAD, D), f32), meta_f, x[0]], axis=0)
    target = jnp.concatenate([jnp.zeros((ROW0, D), f32), loss_target[0]], axis=0)
    xh, rstd = _ln_in_fwd(raw)
    cur = (xh, rstd, gb_in + b0_token[0, 0])
    saved = []
    for l in range(DEPTH):
        if l == 1:
            wa[1], wb[1] = _push_wait("gather_ici_l1_wait", l1_send, l1_recv, l1_flying, cur[0], plan_l1)
            wa[1], wb[1] = _gather_d2d("gather_d2d_l1", [wa[1], wb[1]])
        a0 = cur
        xh1, r1, au1, hb1 = _ffn_fwd(f"ffn1_fwd_{l}", a0[0], a0[2], wa[l])
        a1 = (xh1, r1, gb_of(l, 0))
        if l == 0:
            (wb[0],) = _push_wait("gather_ici_b0_wait", b0_send, b0_recv, b0_flying, xh1, plan_b0)
            (wb[0],) = _gather_d2d("gather_d2d_b0", [wb[0]])
            l1_send, l1_recv, l1_flying, l1_token = _push_start("gather_ici_l1_start", [wa[1], wb[1]], plan_l1)
            a1 = (xh1, r1, a1[2] + l1_token[0, 0])
        z = _mix_in_fwd(f"mix_in_fwd_{l}", a1[0], a1[2], wb[l])
        mp = mix_params(l)
        ycat, st_in, ut_in, cv_in, yp_in = _mix_core_fwd(f"mix_core_fwd_{l}", z, cs, sn, *mp)
        xt_in = (cv_in, yp_in)
        xh2, r2 = _mix_out_fwd(f"mix_out_fwd_{l}", a1[0], a1[2], ycat, wb[l])
        a2 = (xh2, r2, gb_of(l, 1))
        xh3, r3, au2, hb2 = _ffn_fwd(f"ffn2_fwd_{l}", a2[0], a2[2], wb[l])
        a3 = (xh3, r3, gb_of(l, 2))
        saved.append((a0, a1, a2, a3, z, ycat, st_in, ut_in, xt_in, mp, au1, hb1, au2, hb2))
        cur = a3

    dy, loss_part = _loss_fwd_bwd(cur[0], cur[2], target)
    loss = lax.psum(loss_part[0, 0], ("x", "y", "c"))

    g_ln_g = [[None] * 3 for _ in range(DEPTH)]
    g_ln_b = [[None] * 3 for _ in range(DEPTH)]
    g_small = [dict() for _ in range(DEPTH)]
    slot = lambda j: j

    def ffn_grads(gbuf, tag, l, hb, hid, dau, dffn):
        gbuf = _dw_into(f"dw13_{tag}_{l}", gbuf, dau, hb, FF_SLOT, [(0, W13_ROWS, slot, OFF_W13)])
        return _dw_into(f"dw2_{tag}_{l}", gbuf, hid, dffn, FF_SLOT,
                        [(0, W2_ROWS, lambda j: 2 * j, OFF_W2), (W2_ROWS, W2_ROWS, lambda j: 2 * j + 1, OFF_W2)])

    rs_plan = _rs_ici_plan()
    token = jnp.zeros((), f32)
    started = {}
    for l in reversed(range(DEPTH)):
        a0, a1, a2, a3, z, ycat, st_in, ut_in, xt_in, mp, au1, hb1, au2, hb2 = saved[l]
        g_a = lax.empty((N_SHARD, A_ROWS, D), _WIRE)
        g_b = lax.empty((N_SHARD, B_ROWS, D), _WIRE)
        dh, hid, dau, dffn, dgb = _ffn_bwd(f"ffn2_bwd_{l}", dy, a3[0], a3[1], a3[2] + token, au2, wb[l])
        g_ln_g[l][2], g_ln_b[l][2] = dgb[0], dgb[1]
        g_b = ffn_grads(g_b, "ffn2", l, hb2, hid, dau, dffn)
        dh_res, dycat, dsb, dgb = _mix_out_bwd(f"mix_out_bwd_{l}", dh, a2[0], a2[1], a2[2], wb[l])
        g_ln_g[l][1], g_ln_b[l][1] = dgb[0], dgb[1]
        g_b = _dw_into(f"dw_out_{l}", g_b, ycat, dsb, D,
                       [(WOUT_ROWS * s, WOUT_ROWS, lambda j, s=s: s, OFF_WOUT) for s in range(N_SHARD)])
        dz, dwbd, dwpw, dcdw, dsm = _mix_core_bwd(f"mix_core_bwd_{l}", z, dycat, cs, sn, st_in, ut_in, *xt_in, *mp)
        pw = jnp.concatenate([dwpw.astype(_WIRE).reshape(N_SHARD, PW_ROWS, D),
                              jnp.zeros((N_SHARD, B_ROWS - OFF_PW - PW_ROWS, D), _WIRE)], axis=1)
        g_b = lax.dynamic_update_slice(g_b, pw, (0, OFF_PW, 0))
        g_small[l] = dict(
            pool_w=jnp.stack([dwbd[64 * g:64 * (g + 1), 64 * g:64 * (g + 1)] for g in range(4)]),
            pool_scale=dsm[1, :256], conv_db=dsm[2, :256], conv_ln_g=dsm[3, :256], conv_ln_b=dsm[4, :256],
            ret_gn_g=dsm[0], conv_dw=dcdw[:CONV_W])
        dh, hb = _mix_in_bwd(f"mix_in_bwd_{l}", dh_res, dz, a1[0], a1[2], wb[l])
        g_b = _dw_into(f"dw_in_{l}", g_b, dz, hb, D_IN,
                       [(WIN_ROWS * s, WIN_ROWS, lambda j, s=s: s, OFF_WIN) for s in range(N_SHARD)])
        started["b", l], token = _rs_start(f"b{l}", g_b, rs_plan)
        dh, hid, dau, dffn, dgb = _ffn_bwd(f"ffn1_bwd_{l}", dh, a1[0], a1[1], a1[2] + token, au1, wa[l])
        g_ln_g[l][0], g_ln_b[l][0] = dgb[0], dgb[1]
        g_a = ffn_grads(g_a, "ffn1", l, hb1, hid, dau, dffn)
        g_a = lax.dynamic_update_slice(g_a, jnp.zeros((N_SHARD, A_ROWS - OFF_WIN, D), _WIRE), (0, OFF_WIN, 0))
        dy = dh
        started["a", l], token = _rs_start(f"a{l}", g_a, rs_plan)
    d_raw, dgb_in = _ln_in_bwd(dy, saved[0][0][0], saved[0][0][1], gb_in + token)
    grad_x = d_raw[ROW0:][None]

    gsum, after = {}, d_raw
    for key in (("b", 1), ("a", 1), ("b", 0), ("a", 0)):
        gsum[key] = after = _rs_end(started[key], after, rs_plan)
    g_big = [_unpack_shard(gsum["a", l], gsum["b", l]) for l in range(DEPTH)]
    grads = {n: jnp.stack([g_big[l][n] for l in range(DEPTH)]) for n in _BIG}

    small_parts = [
        d_raw[PAD:ROW0],
        jnp.stack([g_small[l]["conv_dw"] for l in range(DEPTH)]),
        jnp.stack([jnp.stack(g_ln_g[l]) for l in range(DEPTH)]),
        jnp.stack([jnp.stack(g_ln_b[l]) for l in range(DEPTH)]),
        dgb_in[0], dgb_in[1],
        jnp.stack([g_small[l]["pool_w"] for l in range(DEPTH)]),
        jnp.stack([g_small[l]["pool_scale"] for l in range(DEPTH)]),
        jnp.stack([g_small[l]["conv_db"] for l in range(DEPTH)]),
        jnp.stack([g_small[l]["conv_ln_g"] for l in range(DEPTH)]),
        jnp.stack([g_small[l]["conv_ln_b"] for l in range(DEPTH)]),
        jnp.stack([g_small[l]["ret_gn_g"] for l in range(DEPTH)]),
    ]
    red = _unpack_rows(_all_reduce_small(_pack_rows(small_parts)), [p.shape for p in small_parts])
    grads["meta"] = lax.dynamic_slice_in_dim(red[0], 256 * chip, 256, axis=1)
    grads["conv_dw"] = lax.dynamic_slice_in_dim(red[1], 64 * chip, 64, axis=2)
    grads["ln_g"] = lax.dynamic_slice_in_dim(red[2], 256 * chip, 256, axis=2)
    grads["ln_b"] = lax.dynamic_slice_in_dim(red[3], 256 * chip, 256, axis=2)
    for n, v in zip(("ln_in_g", "ln_in_b", "pool_w", "pool_scale", "conv_db", "conv_ln_g", "conv_ln_b", "ret_gn_g"),
                    red[4:]):
        grads[n] = v

    names = ['meta', 'ln_in_g', 'ln_in_b', 'ffn1_w13', 'ffn1_w2', 'w_in', 'pool_w', 'pool_scale', 'conv_dw',
             'conv_db', 'conv_ln_g', 'conv_ln_b', 'conv_pw', 'ret_gn_g', 'w_out', 'ffn2_w13', 'ffn2_w2', 'ln_g', 'ln_b']
    ws = dict(meta=meta, ln_in_g=ln_in_g, ln_in_b=ln_in_b, ffn1_w13=ffn1_w13, ffn1_w2=ffn1_w2, w_in=w_in,
              pool_w=pool_w, pool_scale=pool_scale, conv_dw=conv_dw, conv_db=conv_db, conv_ln_g=conv_ln_g,
              conv_ln_b=conv_ln_b, conv_pw=conv_pw, ret_gn_g=ret_gn_g, w_out=w_out, ffn2_w13=ffn2_w13,
              ffn2_w2=ffn2_w2, ln_g=ln_g, ln_b=ln_b)
    ms = dict(meta=m_meta, ln_in_g=m_ln_in_g, ln_in_b=m_ln_in_b, ffn1_w13=m_ffn1_w13, ffn1_w2=m_ffn1_w2,
              w_in=m_w_in, pool_w=m_pool_w, pool_scale=m_pool_scale, conv_dw=m_conv_dw, conv_db=m_conv_db,
              conv_ln_g=m_conv_ln_g, conv_ln_b=m_conv_ln_b, conv_pw=m_conv_pw, ret_gn_g=m_ret_gn_g,
              w_out=m_w_out, ffn2_w13=m_ffn2_w13, ffn2_w2=m_ffn2_w2, ln_g=m_ln_g, ln_b=m_ln_b)
    vs = dict(meta=v_meta, ln_in_g=v_ln_in_g, ln_in_b=v_ln_in_b, ffn1_w13=v_ffn1_w13, ffn1_w2=v_ffn1_w2,
              w_in=v_w_in, pool_w=v_pool_w, pool_scale=v_pool_scale, conv_dw=v_conv_dw, conv_db=v_conv_db,
              conv_ln_g=v_conv_ln_g, conv_ln_b=v_conv_ln_b, conv_pw=v_conv_pw, ret_gn_g=v_ret_gn_g,
              w_out=v_w_out, ffn2_w13=v_ffn2_w13, ffn2_w2=v_ffn2_w2, ln_g=v_ln_g, ln_b=v_ln_b)
    delta, new_m, new_v = {}, {}, {}
    for n in _BIG:
        shp = ws[n].shape
        two = lambda a: a.reshape(-1, shp[-1])
        d_, m_, v_ = _adamw("adamw_" + n, two(ws[n]), two(grads[n]), two(ms[n]), two(vs[n]))
        delta[n], new_m[n], new_v[n] = d_.reshape(shp), m_.reshape(shp), v_.reshape(shp)
    small_names = [n for n in names if n not in _BIG]
    pk = lambda d: _pack_rows([d[n] for n in small_names])
    d_, m_, v_ = _adamw("adamw_small", pk(ws), pk(grads), pk(ms), pk(vs))
    shapes = [ws[n].shape for n in small_names]
    for n, a, b, c in zip(small_names, _unpack_rows(d_, shapes), _unpack_rows(m_, shapes), _unpack_rows(v_, shapes)):
        delta[n], new_m[n], new_v[n] = a, b, c

    return (loss, grad_x, *[grads[n] for n in names], *[delta[n] for n in names],
            *[new_m[n] for n in names], *[new_v[n] for n in names])
```

```python
import functools
import math

import jax
import jax.numpy as jnp
from jax import lax
from jax.experimental import pallas as pl
from jax.experimental.pallas import tpu as pltpu

D = 1024
DEPTH = 2
N_META = 16
PAD = 112
ROW0 = PAD + N_META
D_POOL = 256
D_CONV = 256
D_RET = 512
HEADS = 4
DH = 128
CONV_W = 31
D_FF = 2752
FF_SLOT = 1408
D_FFP = 2 * FF_SLOT
D_IN = 2816
N_SHARD = 4
ALPHA = (2.0 * DEPTH) ** 0.25
LN_EPS = 1e-5
ROPE_BASE = 10000.0
LOG_GAMMA = tuple(math.log(1.0 - 2.0 ** (-5.0 - h)) for h in range(HEADS))
ADAM_LR, ADAM_B1, ADAM_B2, ADAM_EPS, ADAM_WD, ADAM_STEP = 0.001, 0.9, 0.999, 1e-08, 0.01, 10

_MM = jnp.bfloat16
_WIRE = jnp.bfloat16
_VMEM_LIMIT = 56 * 1024 * 1024
_FFN_ROWS = 640

MESH = pl.DeviceIdType.MESH
_ANY = pl.BlockSpec(memory_space=pl.ANY)

W13_ROWS = 1376
W2_ROWS = 688
WIN_ROWS = 704
WOUT_ROWS = 256
PW_ROWS = 16
OFF_W13 = 0
OFF_W2 = W13_ROWS
OFF_WIN = W13_ROWS + W2_ROWS
OFF_WOUT = OFF_WIN + WIN_ROWS
OFF_PW = OFF_WOUT + WOUT_ROWS
A_ROWS = 2080
B_ROWS = 3072
W2_SLOT_OFF = (0, W2_ROWS, FF_SLOT, FF_SLOT + W2_ROWS)


def _start_rows(w_ref, off, n, dst_of, sems, k0):
    cps = []
    for s in range(N_SHARD):
        cp = pltpu.make_async_copy(w_ref.at[s, pl.ds(off, n)], dst_of(s), sems.at[k0 + s])
        cp.start()
        cps.append(cp)
    return cps


def _load_ffn_weights(w_ref, w13, w2, sems):
    cps = _start_rows(w_ref, OFF_W13, W13_ROWS, lambda s: w13.at[s, pl.ds(0, W13_ROWS)], sems, 0)
    cps += _start_rows(w_ref, OFF_W2, W2_ROWS, lambda s: w2.at[pl.ds(W2_SLOT_OFF[s], W2_ROWS)], sems, 4)
    zpad = jnp.zeros((FF_SLOT - W13_ROWS, D), w13.dtype)
    for s in range(N_SHARD):
        w13[s, W13_ROWS:FF_SLOT, :] = zpad
    w2[W13_ROWS:FF_SLOT, :] = zpad
    w2[FF_SLOT + W13_ROWS:D_FFP, :] = zpad
    for cp in cps:
        cp.wait()


def _load_rows(w_ref, off, n, dst, sems):
    for cp in _start_rows(w_ref, off, n, lambda s: dst.at[pl.ds(s * n, n)], sems, 0):
        cp.wait()


def _dot(a, b):
    return jnp.dot(a, b, preferred_element_type=jnp.float32)


def _dot_nt(a, b):
    return lax.dot_general(a, b, (((1,), (1,)), ((), ())), preferred_element_type=jnp.float32)


def _dot_tn(a, b):
    return lax.dot_general(a, b, (((0,), (0,)), ((), ())), preferred_element_type=jnp.float32)


def _params(sem=("arbitrary",)):
    return pltpu.CompilerParams(dimension_semantics=sem, vmem_limit_bytes=_VMEM_LIMIT)


def _row_block(t, cap=640):
    for rb in (640, 320, 128):
        if rb <= cap and t % rb == 0 and (t > 1024 or rb == 128):
            return rb
    raise ValueError(t)


def _rows(rb, n):
    return pl.BlockSpec((rb, n), lambda i: (i, 0))


def _full(shape):
    nd = len(shape)
    return pl.BlockSpec(tuple(shape), lambda i: (0,) * nd, pipeline_mode=pl.Buffered(1))


def _acc(shape):
    nd = len(shape)
    return pl.BlockSpec(tuple(shape), lambda i: (0,) * nd)


def _sigmoid(x):
    return 1.0 / (1.0 + jnp.exp(-x))


def _ln_fwd(s):
    mu = jnp.mean(s, axis=-1, keepdims=True)
    xc = s - mu
    var = jnp.mean(xc * xc, axis=-1, keepdims=True)
    rstd = lax.rsqrt(var + LN_EPS)
    return xc * rstd, rstd


def _ln_bwd(dxh, xh, rstd):
    m1 = jnp.mean(dxh, axis=-1, keepdims=True)
    m2 = jnp.mean(dxh * xh, axis=-1, keepdims=True)
    return rstd * (dxh - m1 - xh * m2)


def _ln_in_fwd(raw):
    t = raw.shape[0]
    rb = _row_block(t)

    def body(raw_ref, xh_ref, rstd_ref):
        xh, rstd = _ln_fwd(raw_ref[...])
        xh_ref[...] = xh
        rstd_ref[...] = rstd

    return pl.pallas_call(
        body, name="ln_in_fwd", grid=(t // rb,),
        in_specs=[_rows(rb, D)],
        out_specs=[_rows(rb, D), _rows(rb, 1)],
        out_shape=[jax.ShapeDtypeStruct((t, D), jnp.float32), jax.ShapeDtypeStruct((t, 1), jnp.float32)],
        compiler_params=_params(),
    )(raw)


def _ffn_fwd(name, xh, gb, wfull):
    t = xh.shape[0]
    rb = _row_block(t, _FFN_ROWS)

    def body(xh_ref, gb_ref, w_ref, out_ref, rstd_ref, au_ref, hb_ref, w13, w2, sems):
        @pl.when(pl.program_id(0) == 0)
        def _():
            _load_ffn_weights(w_ref, w13, w2, sems)

        h = xh_ref[...] * gb_ref[0:1, :] + gb_ref[1:2, :]
        hb = h.astype(_MM)
        hb_ref[...] = hb
        acc = jnp.zeros((rb, D), jnp.float32)
        for j in range(2):
            lo = j * FF_SLOT
            a = _dot_nt(hb, w13[j])
            u = _dot_nt(hb, w13[2 + j])
            au_ref[:, lo:lo + FF_SLOT] = a.astype(_MM)
            au_ref[:, D_FFP + lo:D_FFP + lo + FF_SLOT] = u.astype(_MM)
            hid = (a * _sigmoid(a) * u).astype(_MM)
            acc = acc + _dot(hid, w2[lo:lo + FF_SLOT, :])
        xo, rstd = _ln_fwd(ALPHA * h + 0.5 * acc)
        out_ref[...] = xo
        rstd_ref[...] = rstd

    return pl.pallas_call(
        body, name=name, grid=(t // rb,),
        in_specs=[_rows(rb, D), _full((2, D)), _ANY],
        out_specs=[_rows(rb, D), _rows(rb, 1), _rows(rb, 2 * D_FFP), _rows(rb, D)],
        out_shape=[jax.ShapeDtypeStruct((t, D), jnp.float32), jax.ShapeDtypeStruct((t, 1), jnp.float32),
                   jax.ShapeDtypeStruct((t, 2 * D_FFP), _MM), jax.ShapeDtypeStruct((t, D), _MM)],
        scratch_shapes=[pltpu.VMEM((N_SHARD, FF_SLOT, D), _MM), pltpu.VMEM((D_FFP, D), _MM),
                        pltpu.SemaphoreType.DMA((8,))],
        compiler_params=_params(),
    )(xh, gb, wfull)


def _ffn_bwd(name, dy, xo, rstd, gb_out, au, wfull):
    t = xo.shape[0]
    rb = _row_block(t, 320)

    def body(dy_ref, xo_ref, rstd_ref, gbo_ref, au_ref, w_ref,
             dh_ref, hid_ref, dau_ref, dffn_ref, dgb_ref, w13, w2, sems):
        i = pl.program_id(0)

        @pl.when(i == 0)
        def _():
            dgb_ref[...] = jnp.zeros_like(dgb_ref)
            _load_ffn_weights(w_ref, w13, w2, sems)

        dy = dy_ref[...]
        xo = xo_ref[...]
        dgb_ref[0:1, :] += jnp.sum(dy * xo, axis=0, keepdims=True)
        dgb_ref[1:2, :] += jnp.sum(dy, axis=0, keepdims=True)
        ds = _ln_bwd(dy * gbo_ref[0:1, :], xo, rstd_ref[...])
        dffn = (0.5 * ds).astype(_MM)
        dffn_ref[...] = dffn
        dh = ALPHA * ds
        for j in range(2):
            lo = j * FF_SLOT
            a = au_ref[:, lo:lo + FF_SLOT].astype(jnp.float32)
            u = au_ref[:, D_FFP + lo:D_FFP + lo + FF_SLOT].astype(jnp.float32)
            sg = _sigmoid(a)
            si = a * sg
            hid_ref[:, lo:lo + FF_SLOT] = (si * u).astype(_MM)
            dhid = _dot_nt(dffn, w2[lo:lo + FF_SLOT, :])
            da = (dhid * u * (sg * (1.0 + a * (1.0 - sg)))).astype(_MM)
            du = (dhid * si).astype(_MM)
            dau_ref[:, lo:lo + FF_SLOT] = da
            dau_ref[:, D_FFP + lo:D_FFP + lo + FF_SLOT] = du
            dh = dh + _dot(da, w13[j]) + _dot(du, w13[2 + j])
        dh_ref[...] = dh

    return pl.pallas_call(
        body, name=name, grid=(t // rb,),
        in_specs=[_rows(rb, D), _rows(rb, D), _rows(rb, 1), _full((2, D)), _rows(rb, 2 * D_FFP), _ANY],
        out_specs=[_rows(rb, D), _rows(rb, D_FFP), _rows(rb, 2 * D_FFP), _rows(rb, D), _acc((8, D))],
        out_shape=[jax.ShapeDtypeStruct((t, D), jnp.float32),
                   jax.ShapeDtypeStruct((t, D_FFP), _MM), jax.ShapeDtypeStruct((t, 2 * D_FFP), _MM),
                   jax.ShapeDtypeStruct((t, D), _MM), jax.ShapeDtypeStruct((8, D), jnp.float32)],
        scratch_shapes=[pltpu.VMEM((N_SHARD, FF_SLOT, D), _MM), pltpu.VMEM((D_FFP, D), _MM),
                        pltpu.SemaphoreType.DMA((8,))],
        compiler_params=_params(),
    )(dy, xo, rstd, gb_out, au, wfull)


def _mix_in_fwd(name, xh, gb, wfull):
    t = xh.shape[0]
    rb = _row_block(t)

    def body(xh_ref, gb_ref, w_ref, z_ref, wt, sems):
        @pl.when(pl.program_id(0) == 0)
        def _():
            _load_rows(w_ref, OFF_WIN, WIN_ROWS, wt, sems)

        h = xh_ref[...] * gb_ref[0:1, :] + gb_ref[1:2, :]
        z = _dot_nt(h.astype(_MM), wt[...])
        row = pl.program_id(0) * rb + lax.broadcasted_iota(jnp.int32, (rb, 1), 0)
        z_ref[...] = jnp.where(row >= PAD, z, 0.0)

    return pl.pallas_call(
        body, name=name, grid=(t // rb,),
        in_specs=[_rows(rb, D), _full((2, D)), _ANY],
        out_specs=_rows(rb, D_IN),
        out_shape=jax.ShapeDtypeStruct((t, D_IN), jnp.float32),
        scratch_shapes=[pltpu.VMEM((D_IN, D), _MM), pltpu.SemaphoreType.DMA((4,))],
        compiler_params=_params(),
    )(xh, gb, wfull)


def _mix_in_bwd(name, dh_res, dz, xh, gb, wfull):
    t = xh.shape[0]
    rb = _row_block(t)

    def body(dhr_ref, dz_ref, xh_ref, gb_ref, w_ref, dh_ref, hb_ref, wt, sems):
        @pl.when(pl.program_id(0) == 0)
        def _():
            _load_rows(w_ref, OFF_WIN, WIN_ROWS, wt, sems)

        dh_ref[...] = dhr_ref[...] + _dot(dz_ref[...], wt[...])
        hb_ref[...] = (xh_ref[...] * gb_ref[0:1, :] + gb_ref[1:2, :]).astype(_MM)

    return pl.pallas_call(
        body, name=name, grid=(t // rb,),
        in_specs=[_rows(rb, D), _rows(rb, D_IN), _rows(rb, D), _full((2, D)), _ANY],
        out_specs=[_rows(rb, D), _rows(rb, D)],
        out_shape=[jax.ShapeDtypeStruct((t, D), jnp.float32), jax.ShapeDtypeStruct((t, D), _MM)],
        scratch_shapes=[pltpu.VMEM((D_IN, D), _MM), pltpu.SemaphoreType.DMA((4,))],
        compiler_params=_params(),
    )(dh_res, dz, xh, gb, wfull)


def _mix_out_fwd(name, xh, gb, ycat, wfull):
    t = xh.shape[0]
    rb = _row_block(t)

    def body(xh_ref, gb_ref, y_ref, w_ref, out_ref, rstd_ref, wo, sems):
        @pl.when(pl.program_id(0) == 0)
        def _():
            _load_rows(w_ref, OFF_WOUT, WOUT_ROWS, wo, sems)

        h = xh_ref[...] * gb_ref[0:1, :] + gb_ref[1:2, :]
        xo, rstd = _ln_fwd(ALPHA * h + _dot(y_ref[...], wo[...]))
        out_ref[...] = xo
        rstd_ref[...] = rstd

    return pl.pallas_call(
        body, name=name, grid=(t // rb,),
        in_specs=[_rows(rb, D), _full((2, D)), _rows(rb, D), _ANY],
        out_specs=[_rows(rb, D), _rows(rb, 1)],
        out_shape=[jax.ShapeDtypeStruct((t, D), jnp.float32), jax.ShapeDtypeStruct((t, 1), jnp.float32)],
        scratch_shapes=[pltpu.VMEM((D, D), _MM), pltpu.SemaphoreType.DMA((4,))],
        compiler_params=_params(),
    )(xh, gb, ycat, wfull)


def _mix_out_bwd(name, dy, xo, rstd, gb_out, wfull):
    t = xo.shape[0]
    rb = _row_block(t)

    def body(dy_ref, xo_ref, rstd_ref, gbo_ref, w_ref, dhr_ref, dyc_ref, dsb_ref, dgb_ref, wo, sems):
        @pl.when(pl.program_id(0) == 0)
        def _():
            dgb_ref[...] = jnp.zeros_like(dgb_ref)
            _load_rows(w_ref, OFF_WOUT, WOUT_ROWS, wo, sems)

        dy = dy_ref[...]
        xo = xo_ref[...]
        dgb_ref[0:1, :] += jnp.sum(dy * xo, axis=0, keepdims=True)
        dgb_ref[1:2, :] += jnp.sum(dy, axis=0, keepdims=True)
        ds = _ln_bwd(dy * gbo_ref[0:1, :], xo, rstd_ref[...])
        dsb = ds.astype(_MM)
        dsb_ref[...] = dsb
        dhr_ref[...] = ALPHA * ds
        dyc_ref[...] = _dot_nt(dsb, wo[...])

    return pl.pallas_call(
        body, name=name, grid=(t // rb,),
        in_specs=[_rows(rb, D), _rows(rb, D), _rows(rb, 1), _full((2, D)), _ANY],
        out_specs=[_rows(rb, D), _rows(rb, D), _rows(rb, D), _acc((8, D))],
        out_shape=[jax.ShapeDtypeStruct((t, D), jnp.float32), jax.ShapeDtypeStruct((t, D), jnp.float32),
                   jax.ShapeDtypeStruct((t, D), _MM), jax.ShapeDtypeStruct((8, D), jnp.float32)],
        scratch_shapes=[pltpu.VMEM((D, D), _MM), pltpu.SemaphoreType.DMA((4,))],
        compiler_params=_params(),
    )(dy, xo, rstd, gb_out, wfull)


def _loss_fwd_bwd(xh, gb, target):
    t = xh.shape[0]
    rb = _row_block(t)

    def body(xh_ref, gb_ref, tg_ref, dy_ref, loss_ref):
        @pl.when(pl.program_id(0) == 0)
        def _():
            loss_ref[...] = jnp.zeros_like(loss_ref)

        y = xh_ref[...] * gb_ref[0:1, :] + gb_ref[1:2, :]
        row = pl.program_id(0) * rb + lax.broadcasted_iota(jnp.int32, (rb, 1), 0)
        err = jnp.where(row >= ROW0, y - tg_ref[...], 0.0)
        dy_ref[...] = err * (1.0 / D)
        per_row = jnp.mean(err * err, axis=-1, keepdims=True)
        loss_ref[...] += 0.5 * jnp.sum(per_row, axis=0, keepdims=True)

    return pl.pallas_call(
        body, name="loss", grid=(t // rb,),
        in_specs=[_rows(rb, D), _full((2, D)), _rows(rb, D)],
        out_specs=[_rows(rb, D), _acc((1, 1))],
        out_shape=[jax.ShapeDtypeStruct((t, D), jnp.float32), jax.ShapeDtypeStruct((1, 1), jnp.float32)],
        compiler_params=_params(),
    )(xh, gb, target)


def _ln_in_bwd(dy, xh, rstd, gb):
    t = xh.shape[0]
    rb = _row_block(t)

    def body(dy_ref, xh_ref, rstd_ref, gb_ref, dx_ref, dgb_ref):
        @pl.when(pl.program_id(0) == 0)
        def _():
            dgb_ref[...] = jnp.zeros_like(dgb_ref)

        dy = dy_ref[...]
        xh = xh_ref[...]
        dgb_ref[0:1, :] += jnp.sum(dy * xh, axis=0, keepdims=True)
        dgb_ref[1:2, :] += jnp.sum(dy, axis=0, keepdims=True)
        dx_ref[...] = _ln_bwd(dy * gb_ref[0:1, :], xh, rstd_ref[...])

    return pl.pallas_call(
        body, name="ln_in_bwd", grid=(t // rb,),
        in_specs=[_rows(rb, D), _rows(rb, D), _rows(rb, 1), _full((2, D))],
        out_specs=[_rows(rb, D), _acc((8, D))],
        out_shape=[jax.ShapeDtypeStruct((t, D), jnp.float32), jax.ShapeDtypeStruct((8, D), jnp.float32)],
        compiler_params=_params(),
    )(dy, xh, rstd, gb)


def _dw_rows(t, cols):
    for tt in (2080, 1664, 640, 128):
        vmem = 2 * tt * (cols + D) * 2 + cols * D * 6
        if t % tt == 0 and (t > 1024 or tt == 128) and vmem <= 44 * 1024 * 1024:
            return tt
    raise ValueError((t, cols))


def _dw_into(name, gpack, x, y, cols, pieces):
    t, k = x.shape
    tt = _dw_rows(t, cols)
    nt = t // tt

    def body(x_ref, y_ref, g_in, g_out, acc, stage, sems):
        j = pl.program_id(0)
        s = pl.program_id(1)

        @pl.when(s == 0)
        def _():
            acc[...] = jnp.zeros_like(acc)

        acc[...] += _dot_tn(x_ref[...], y_ref[...])

        @pl.when(s == nt - 1)
        def _():
            stage[...] = acc[...].astype(stage.dtype)
            cps = []
            for q, (lo, n, chip_of, off) in enumerate(pieces):
                cp = pltpu.make_async_copy(stage.at[pl.ds(lo, n)], g_out.at[chip_of(j), pl.ds(off, n)],
                                           sems.at[q])
                cp.start()
                cps.append(cp)
            for cp in cps:
                cp.wait()

    return pl.pallas_call(
        body, name=name, grid=(k // cols, nt),
        in_specs=[pl.BlockSpec((tt, cols), lambda j, s: (s, j)), pl.BlockSpec((tt, D), lambda j, s: (s, 0)), _ANY],
        out_specs=_ANY,
        out_shape=jax.ShapeDtypeStruct(gpack.shape, gpack.dtype),
        input_output_aliases={2: 0},
        scratch_shapes=[pltpu.VMEM((cols, D), jnp.float32), pltpu.VMEM((cols, D), gpack.dtype),
                        pltpu.SemaphoreType.DMA((len(pieces),))],
        compiler_params=_params(("arbitrary", "arbitrary")),
    )(x, y, gpack)


_TAIL_U = 32
_TAIL_X = 32
_MIX_ROWS = 320


def _decay_mask(rb, h):
    ii = lax.broadcasted_iota(jnp.int32, (rb, rb), 0)
    jj = lax.broadcasted_iota(jnp.int32, (rb, rb), 1)
    dist = jnp.abs(ii - jj).astype(jnp.float32)
    vis = (jj >> 6) <= (ii >> 6)
    return jnp.where(vis, jnp.exp(LOG_GAMMA[h] * dist), 0.0)


def _row_decays(rb, h):
    r = lax.broadcasted_iota(jnp.int32, (rb, DH), 0).astype(jnp.float32)
    return jnp.exp(LOG_GAMMA[h] * (r + 1.0)), jnp.exp(LOG_GAMMA[h] * (rb - 1.0 - r))


def _rope(x, cs, sn):
    return x * cs + pltpu.roll(x, DH // 2, 1) * sn


def _rope_t(dx, cs, sn):
    return dx * cs + pltpu.roll(dx * sn, DH // 2, 1)


def _pool_count(blk, rb):
    row = blk * rb + lax.broadcasted_iota(jnp.int32, (rb, D_POOL), 0) - PAD
    lane = lax.broadcasted_iota(jnp.int32, (rb, D_POOL), 1)
    win = jnp.left_shift(2, lane >> 6)
    return jnp.clip(row + 1, 1, win).astype(jnp.float32)


def _pool_select(p2, p4, p8, p16):
    lane = lax.broadcasted_iota(jnp.int32, p2.shape, 1)
    return jnp.where(lane < 64, p2, jnp.where(lane < 128, p4, jnp.where(lane < 192, p8, p16)))


def _trailing_windows(ext, p2, p4, p8, rb):
    n = _TAIL_X + rb
    p2[8:n, :] = ext[8:n, :] + ext[pl.ds(7, n - 8), :]
    p4[16:n, :] = p2[16:n, :] + p2[pl.ds(14, n - 16), :]
    p8[24:n, :] = p4[24:n, :] + p4[pl.ds(20, n - 24), :]
    lo = _TAIL_X
    p16 = p8[lo:n, :] + p8[lo - 8:n - 8, :]
    return _pool_select(p2[lo:n, :], p4[lo:n, :], p8[lo:n, :], p16)


def _leading_windows(ext, p2, p4, p8, rb):
    p2[0:rb + 24, :] = ext[0:rb + 24, :] + ext[pl.ds(1, rb + 24), :]
    p4[0:rb + 16, :] = p2[0:rb + 16, :] + p2[pl.ds(2, rb + 16), :]
    p8[0:rb + 8, :] = p4[0:rb + 8, :] + p4[pl.ds(4, rb + 8), :]
    p16 = p8[0:rb, :] + p8[8:rb + 8, :]
    return _pool_select(p2[0:rb, :], p4[0:rb, :], p8[0:rb, :], p16)


def _shifted_copies(ext, copies, first, n, sign):
    for b in range(1, 8):
        copies[b - 1, first:first + n, :] = ext[pl.ds(first - sign * b, n), :]


def _tap(ext, copies, k, start, rows, sign):
    a, b = divmod(k, 8)
    src = ext if b == 0 else copies.at[b - 1]
    return src[pl.ds(start - sign * 8 * a, rows), :]


def _sub_rows(rb):
    return 128 if rb % 128 == 0 else 64


def _mix_core_fwd(name, z, cs, sn, wbd, pscale, cdw, cvec, wpw, gn):
    t = z.shape[0]
    rb = _row_block(t, _MIX_ROWS)
    nblk = t // rb
    sr = _sub_rows(rb)

    def body(z_ref, cs_ref, sn_ref, wbd_ref, ps_ref, cdw_ref, cvec_ref, wpw_ref, gn_ref,
             y_ref, st_ref, ut_ref, cv_ref, yp_ref,
             uext, xext, cv, p2, p4, p8, ucopies, state, wmask):
        i = pl.program_id(0)

        @pl.when(i == 0)
        def _():
            state[...] = jnp.zeros_like(state)
            uext[0:_TAIL_U, :] = jnp.zeros((_TAIL_U, D_CONV), jnp.float32)
            xext[0:_TAIL_X, :] = jnp.zeros((_TAIL_X, D_POOL), jnp.float32)
            for h in range(HEADS):
                wmask[h] = _decay_mask(rb, h)

        st_ref[0] = state[...]
        ut_ref[0] = uext[0:_TAIL_U, :]

        xp = z_ref[:, 0:256]
        uext[_TAIL_U:_TAIL_U + rb, :] = z_ref[:, 256:512] * _sigmoid(z_ref[:, 512:768])
        xext[_TAIL_X:_TAIL_X + rb, :] = xp

        _shifted_copies(uext, ucopies, 8, rb + _TAIL_U - 8, 1)
        for r in range(0, rb, sr):
            acc = jnp.zeros((sr, D_CONV), jnp.float32)
            for k in range(CONV_W):
                acc = acc + _tap(uext, ucopies, k, _TAIL_U + r, sr, 1) * cdw_ref[CONV_W - 1 - k:CONV_W - k, :]
            cv[r:r + sr, :] = acc

        win = _trailing_windows(xext, p2, p4, p8, rb)
        ypb = (win / _pool_count(i, rb) - xp).astype(_MM)
        yp_ref[...] = ypb
        y_ref[:, 0:256] = (_dot(ypb, wbd_ref[...]) * ps_ref[...]).astype(_MM)
        cv_ref[...] = cv[...]
        cn, _ = _ln_fwd(cv[...] + cvec_ref[0:1, :])
        ln = cn * cvec_ref[1:2, :] + cvec_ref[2:3, :]
        sw = ln * _sigmoid(ln)
        y_ref[:, 256:512] = _dot(sw.astype(_MM), wpw_ref[...]).astype(_MM)
        csv = cs_ref[...]
        snv = sn_ref[...]
        for h in range(HEADS):
            q = _rope(z_ref[:, 768 + h * DH:768 + (h + 1) * DH], csv, snv)
            k = _rope(z_ref[:, 1280 + h * DH:1280 + (h + 1) * DH], csv, snv) * (DH ** -0.5)
            vb = z_ref[:, 1792 + h * DH:1792 + (h + 1) * DH].astype(_MM)
            g = z_ref[:, 2304 + h * DH:2304 + (h + 1) * DH]
            a, b = _row_decays(rb, h)
            s = _dot_nt(q.astype(_MM), k.astype(_MM)) * wmask[h]
            o = _dot(s.astype(_MM), vb) + _dot((q * a).astype(_MM), state[h].astype(_MM))
            state[h] = math.exp(LOG_GAMMA[h] * rb) * state[h] + _dot_tn((k * b).astype(_MM), vb)
            on, _ = _ln_fwd(o)
            y_ref[:, 512 + h * DH:512 + (h + 1) * DH] = (
                g * _sigmoid(g) * on * gn_ref[:, h * DH:(h + 1) * DH]).astype(_MM)

        uext[0:_TAIL_U, :] = uext[rb:rb + _TAIL_U, :]
        xext[0:_TAIL_X, :] = xext[rb:rb + _TAIL_X, :]

    return pl.pallas_call(
        body, name=name, grid=(nblk,),
        in_specs=[_rows(rb, D_IN), _rows(rb, DH), _rows(rb, DH), _full((256, 256)), _full((1, 256)),
                  _full((32, 256)), _full((8, 256)), _full((256, 256)), _full((1, D_RET))],
        out_specs=[_rows(rb, D),
                   pl.BlockSpec((1, HEADS, DH, DH), lambda i: (i, 0, 0, 0)),
                   pl.BlockSpec((1, _TAIL_U, D_CONV), lambda i: (i, 0, 0)),
                   _rows(rb, D_CONV), _rows(rb, D_POOL)],
        out_shape=[jax.ShapeDtypeStruct((t, D), _MM),
                   jax.ShapeDtypeStruct((nblk, HEADS, DH, DH), jnp.float32),
                   jax.ShapeDtypeStruct((nblk, _TAIL_U, D_CONV), jnp.float32),
                   jax.ShapeDtypeStruct((t, D_CONV), jnp.float32),
                   jax.ShapeDtypeStruct((t, D_POOL), _MM)],
        scratch_shapes=[pltpu.VMEM((rb + _TAIL_U, D_CONV), jnp.float32),
                        pltpu.VMEM((rb + _TAIL_X, D_POOL), jnp.float32),
                        pltpu.VMEM((rb, D_CONV), jnp.float32),
                        pltpu.VMEM((rb + _TAIL_X, D_POOL), jnp.float32),
                        pltpu.VMEM((rb + _TAIL_X, D_POOL), jnp.float32),
                        pltpu.VMEM((rb + _TAIL_X, D_POOL), jnp.float32),
                        pltpu.VMEM((7, rb + _TAIL_U, D_CONV), jnp.float32),
                        pltpu.VMEM((HEADS, DH, DH), jnp.float32),
                        pltpu.VMEM((HEADS, rb, rb), jnp.float32)],
        compiler_params=_params(),
    )(z, cs, sn, wbd, pscale, cdw, cvec, wpw, gn)


def _mix_core_bwd(name, z, dyc, cs, sn, st_in, ut_in, cv_in, yp_in, wbd, pscale, cdw, cvec, wpw, gn):
    t = z.shape[0]
    rb = _row_block(t, _MIX_ROWS)
    nblk = t // rb
    sr = _sub_rows(rb)
    rev = lambda i: nblk - 1 - i

    def body(z_ref, dy_ref, cs_ref, sn_ref, st_ref, ut_ref, cv_ref, yp_ref,
             wbd_ref, ps_ref, cdw_ref, cvec_ref, wpw_ref, gn_ref,
             dz_ref, dwbd_ref, dwpw_ref, dcdw_ref, dsm_ref,
             uext, cv, dcvext, eext, p2, p4, p8, ucopies, dcopies, dstate, wmask):
        i = pl.program_id(0)
        blk = nblk - 1 - i

        @pl.when(i == 0)
        def _():
            dstate[...] = jnp.zeros_like(dstate)
            dcvext[rb:rb + _TAIL_U, :] = jnp.zeros((_TAIL_U, D_CONV), jnp.float32)
            eext[rb:rb + _TAIL_X, :] = jnp.zeros((_TAIL_X, D_POOL), jnp.float32)
            dwbd_ref[...] = jnp.zeros_like(dwbd_ref)
            dwpw_ref[...] = jnp.zeros_like(dwpw_ref)
            dcdw_ref[...] = jnp.zeros_like(dcdw_ref)
            dsm_ref[...] = jnp.zeros_like(dsm_ref)
            for h in range(HEADS):
                wmask[h] = _decay_mask(rb, h)

        row = blk * rb + lax.broadcasted_iota(jnp.int32, (rb, 1), 0)
        live = row >= PAD

        ca = z_ref[:, 256:512]
        sg_c = _sigmoid(z_ref[:, 512:768])
        uext[0:_TAIL_U, :] = ut_ref[0]
        uext[_TAIL_U:_TAIL_U + rb, :] = ca * sg_c

        cnt = _pool_count(blk, rb)
        ypb = yp_ref[...]
        dyp = dy_ref[:, 0:256]
        pm = _dot(ypb, wbd_ref[...])
        dsm_ref[1:2, 0:256] += jnp.sum(dyp * pm, axis=0, keepdims=True)
        dpm = (dyp * ps_ref[...]).astype(_MM)
        dwbd_ref[...] += _dot_tn(ypb, dpm)
        dypre = _dot_nt(dpm, wbd_ref[...])
        eext[0:rb, :] = dypre / cnt
        win = _leading_windows(eext, p2, p4, p8, rb)
        dz_ref[:, 0:256] = jnp.where(live, win - dypre, 0.0).astype(_MM)

        cn, rstd_c = _ln_fwd(cv_ref[...] + cvec_ref[0:1, :])
        ln = cn * cvec_ref[1:2, :] + cvec_ref[2:3, :]
        sg_l = _sigmoid(ln)
        swb = (ln * sg_l).astype(_MM)
        dycb = dy_ref[:, 256:512].astype(_MM)
        dwpw_ref[...] += _dot_tn(swb, dycb)
        dln = _dot_nt(dycb, wpw_ref[...]) * (sg_l * (1.0 + ln * (1.0 - sg_l)))
        dsm_ref[3:4, 0:256] += jnp.sum(dln * cn, axis=0, keepdims=True)
        dsm_ref[4:5, 0:256] += jnp.sum(dln, axis=0, keepdims=True)
        dcv = _ln_bwd(dln * cvec_ref[1:2, :], cn, rstd_c)
        dsm_ref[2:3, 0:256] += jnp.sum(dcv, axis=0, keepdims=True)
        dcvext[0:rb, :] = dcv
        _shifted_copies(uext, ucopies, 8, rb + _TAIL_U - 8, 1)
        _shifted_copies(dcvext, dcopies, 0, rb + _TAIL_U - 8, -1)
        for k in range(CONV_W):
            prod = dcv * _tap(uext, ucopies, k, _TAIL_U, rb, 1)
            dcdw_ref[CONV_W - 1 - k:CONV_W - k, :] += jnp.sum(prod, axis=0, keepdims=True)
        for r in range(0, rb, sr):
            acc = jnp.zeros((sr, D_CONV), jnp.float32)
            for k in range(CONV_W):
                acc = acc + _tap(dcvext, dcopies, k, r, sr, -1) * cdw_ref[CONV_W - 1 - k:CONV_W - k, :]
            cv[r:r + sr, :] = acc
        du = cv[...]
        dz_ref[:, 256:512] = jnp.where(live, du * sg_c, 0.0).astype(_MM)
        dz_ref[:, 512:768] = jnp.where(live, du * ca * sg_c * (1.0 - sg_c), 0.0).astype(_MM)

        csv = cs_ref[...]
        snv = sn_ref[...]
        for h in range(HEADS):
            q = _rope(z_ref[:, 768 + h * DH:768 + (h + 1) * DH], csv, snv)
            k = _rope(z_ref[:, 1280 + h * DH:1280 + (h + 1) * DH], csv, snv) * (DH ** -0.5)
            vb = z_ref[:, 1792 + h * DH:1792 + (h + 1) * DH].astype(_MM)
            g = z_ref[:, 2304 + h * DH:2304 + (h + 1) * DH]
            a, b = _row_decays(rb, h)
            qb = q.astype(_MM)
            kb = k.astype(_MM)
            qab = (q * a).astype(_MM)
            kbb = (k * b).astype(_MM)
            stb = st_ref[0, h].astype(_MM)
            sb = (_dot_nt(qb, kb) * wmask[h]).astype(_MM)
            o = _dot(sb, vb) + _dot(qab, stb)
            on, rstd_o = _ln_fwd(o)
            gnv = gn_ref[:, h * DH:(h + 1) * DH]
            sg_g = _sigmoid(g)
            si_g = g * sg_g
            dyr = dy_ref[:, 512 + h * DH:512 + (h + 1) * DH]
            dsm_ref[0:1, h * DH:(h + 1) * DH] += jnp.sum(dyr * on * si_g, axis=0, keepdims=True)
            dgate = dyr * on * gnv * (sg_g * (1.0 + g * (1.0 - sg_g)))
            dob = _ln_bwd(dyr * gnv * si_g, on, rstd_o).astype(_MM)
            dstb = dstate[h].astype(_MM)
            dsb = (_dot_nt(dob, vb) * wmask[h]).astype(_MM)
            dq = _dot(dsb, kb) + _dot_nt(dob, stb) * a
            dk = _dot_tn(dsb, qb) + _dot_nt(vb, dstb) * b
            dv = _dot_tn(sb, dob) + _dot(kbb, dstb)
            dstate[h] = math.exp(LOG_GAMMA[h] * rb) * dstate[h] + _dot_tn(qab, dob)
            dz_ref[:, 768 + h * DH:768 + (h + 1) * DH] = jnp.where(live, _rope_t(dq, csv, snv), 0.0).astype(_MM)
            dz_ref[:, 1280 + h * DH:1280 + (h + 1) * DH] = jnp.where(
                live, _rope_t(dk * (DH ** -0.5), csv, snv), 0.0).astype(_MM)
            dz_ref[:, 1792 + h * DH:1792 + (h + 1) * DH] = jnp.where(live, dv, 0.0).astype(_MM)
            dz_ref[:, 2304 + h * DH:2304 + (h + 1) * DH] = jnp.where(live, dgate, 0.0).astype(_MM)

        dcvext[rb:rb + _TAIL_U, :] = dcvext[0:_TAIL_U, :]
        eext[rb:rb + _TAIL_X, :] = eext[0:_TAIL_X, :]

    rrows = lambda n: pl.BlockSpec((rb, n), lambda i: (rev(i), 0))
    return pl.pallas_call(
        body, name=name, grid=(nblk,),
        in_specs=[rrows(D_IN), rrows(D), rrows(DH), rrows(DH),
                  pl.BlockSpec((1, HEADS, DH, DH), lambda i: (rev(i), 0, 0, 0)),
                  pl.BlockSpec((1, _TAIL_U, D_CONV), lambda i: (rev(i), 0, 0)),
                  rrows(D_CONV), rrows(D_POOL),
                  _full((256, 256)), _full((1, 256)), _full((32, 256)), _full((8, 256)), _full((256, 256)),
                  _full((1, D_RET))],
        out_specs=[rrows(D_IN), _acc((256, 256)), _acc((256, 256)), _acc((32, 256)), _acc((8, 512))],
        out_shape=[jax.ShapeDtypeStruct((t, D_IN), _MM),
                   jax.ShapeDtypeStruct((256, 256), jnp.float32), jax.ShapeDtypeStruct((256, 256), jnp.float32),
                   jax.ShapeDtypeStruct((32, 256), jnp.float32), jax.ShapeDtypeStruct((8, 512), jnp.float32)],
        scratch_shapes=[pltpu.VMEM((rb + _TAIL_U, D_CONV), jnp.float32),
                        pltpu.VMEM((rb, D_CONV), jnp.float32),
                        pltpu.VMEM((rb + _TAIL_U, D_CONV), jnp.float32),
                        pltpu.VMEM((rb + _TAIL_X, D_POOL), jnp.float32),
                        pltpu.VMEM((rb + _TAIL_X, D_POOL), jnp.float32),
                        pltpu.VMEM((rb + _TAIL_X, D_POOL), jnp.float32),
                        pltpu.VMEM((rb + _TAIL_X, D_POOL), jnp.float32),
                        pltpu.VMEM((7, rb + _TAIL_U, D_CONV), jnp.float32),
                        pltpu.VMEM((7, rb + _TAIL_U, D_CONV), jnp.float32),
                        pltpu.VMEM((HEADS, DH, DH), jnp.float32),
                        pltpu.VMEM((HEADS, rb, rb), jnp.float32)],
        compiler_params=_params(),
    )(z, dyc, cs, sn, st_in, ut_in, cv_in, yp_in, wbd, pscale, cdw, cvec, wpw, gn)


def _me():
    return lax.axis_index("x"), lax.axis_index("y"), lax.axis_index("c")


def _flip(me, mask):
    return tuple(1 - m if f else m for m, f in zip(me, mask))


def _push(name, aliased, inputs, fresh, remote):
    n_al, n_in, n_out, n_rem = len(aliased), len(inputs), len(fresh), len(remote)

    def body(*refs):
        ins = refs[n_al:n_al + n_in]
        al = refs[n_al + n_in:2 * n_al + n_in]
        outs = refs[2 * n_al + n_in:2 * n_al + n_in + n_out]
        send_sems, recv_sems = refs[2 * n_al + n_in + n_out:]
        me = _me()
        copies = []
        for k, (mask, src_fn, dst_fn) in enumerate(remote):
            cp = pltpu.make_async_remote_copy(
                src_ref=src_fn(al, ins, outs, me), dst_ref=dst_fn(al, ins, outs, me),
                send_sem=send_sems.at[k], recv_sem=recv_sems.at[k],
                device_id=_flip(me, mask), device_id_type=MESH)
            cp.start()
            copies.append(cp)
        for cp in copies:
            cp.wait()

    return pl.pallas_call(
        body, name=name,
        in_specs=[_ANY] * (n_al + n_in), out_specs=[_ANY] * (n_al + n_out),
        out_shape=[jax.ShapeDtypeStruct(a.shape, a.dtype) for a in aliased] + list(fresh),
        input_output_aliases={i: i for i in range(n_al)},
        scratch_shapes=[pltpu.SemaphoreType.DMA((n_rem,)), pltpu.SemaphoreType.DMA((n_rem,))],
    )(*aliased, *inputs)


_HBM = pl.BlockSpec(memory_space=pltpu.HBM)
_SEM = pl.BlockSpec(memory_space=pltpu.SEMAPHORE)
_EFFECT = pltpu.SideEffectType.DATAFLOW_SIDE_EFFECTING


def _push_start(name, bufs, remote):
    n, n_rem = len(bufs), len(remote)

    def body(*refs):
        ins = refs[:n]
        send_sems, recv_sems = refs[n], refs[n + 1]
        token = refs[2 * n + 2]
        me = _me()
        for k, (mask, src_fn, dst_fn) in enumerate(remote):
            pltpu.make_async_remote_copy(
                src_ref=src_fn(ins, me), dst_ref=dst_fn(ins, me),
                send_sem=send_sems.at[k], recv_sem=recv_sems.at[k],
                device_id=_flip(me, mask), device_id_type=MESH).start()
        token[...] = jnp.zeros_like(token)

    out = pl.pallas_call(
        body, name=name,
        out_shape=(pltpu.SemaphoreType.DMA((n_rem,)), pltpu.SemaphoreType.DMA((n_rem,)),
                   *[pltpu.HBM(b.shape, b.dtype) for b in bufs], jax.ShapeDtypeStruct((8, 128), jnp.float32)),
        in_specs=[_HBM] * n,
        out_specs=(_SEM, _SEM, *[_HBM] * n, pl.BlockSpec(memory_space=pltpu.VMEM)),
        input_output_aliases={i: i + 2 for i in range(n)},
        compiler_params=pltpu.CompilerParams(has_side_effects=_EFFECT),
    )(*[pltpu.with_memory_space_constraint(b, pltpu.HBM) for b in bufs])
    return out[0], out[1], list(out[2:2 + n]), out[2 + n]


def _push_wait(name, send_sems, recv_sems, bufs, after, remote):
    n = len(bufs)

    def body(*refs):
        ins = refs[:n]
        s_sems, r_sems = refs[n], refs[n + 1]
        me = _me()
        for k, (mask, src_fn, dst_fn) in enumerate(remote):
            cp = pltpu.make_async_remote_copy(
                src_ref=src_fn(ins, me), dst_ref=dst_fn(ins, me),
                send_sem=s_sems.at[k], recv_sem=r_sems.at[k],
                device_id=_flip(me, mask), device_id_type=MESH)
            cp.wait_send()
            cp.wait_recv()

    out = pl.pallas_call(
        body, name=name,
        out_shape=tuple(pltpu.HBM(b.shape, b.dtype) for b in bufs),
        in_specs=[_HBM] * n + [_SEM, _SEM, _ANY], out_specs=tuple([_HBM] * n),
        input_output_aliases={i: i for i in range(n)},
        compiler_params=pltpu.CompilerParams(has_side_effects=_EFFECT),
    )(*bufs, send_sems, recv_sems, after)
    return list(out)


_ICI_MASKS = ((0, 1, 0), (1, 0, 0), (1, 1, 0))
_D2D_MASK = (0, 0, 1)
_ALL_MASKS = tuple((a, b, c) for a in (0, 1) for b in (0, 1) for c in (0, 1))[1:]


def _chip(me):
    return 2 * me[0] + me[1]


def _half(me, rows):
    return pl.ds(me[2] * (rows // 2), rows // 2)


def _other_half(me, rows):
    return pl.ds((1 - me[2]) * (rows // 2), rows // 2)


def _sum_block(rh):
    return next((b for b in (768, 640, 512, 128) if rh % b == 0), rh)


def _own_slot(mine):
    chip = _chip(_me())
    return lax.dynamic_update_slice(lax.empty((N_SHARD,) + mine.shape, mine.dtype), mine[None],
                                    (chip,) + (0,) * mine.ndim)


def _gather_ici_plan(rows, with_small):
    remote = []
    for mask in _ICI_MASKS:
        for b, r in enumerate(rows):
            mine = lambda bufs, me, b=b, r=r: bufs[b].at[_chip(me), _half(me, r)]
            remote.append((mask, mine, mine))
        if with_small:
            mine_small = lambda bufs, me: bufs[len(rows)].at[_chip(me)]
            remote.append((mask, mine_small, mine_small))
    return remote


def _gather_d2d(name, ws):
    remote = []
    for b, w in enumerate(ws):
        for j in range(1, N_SHARD):
            theirs = lambda al, ins, outs, me, j=j, b=b, r=w.shape[1]: al[b].at[(_chip(me) + j) % N_SHARD, _half(me, r)]
            remote.append((_D2D_MASK, theirs, theirs))
    return _push(name, list(ws), [], [], remote)


def _sum_pair(name, g, recv):
    _, r, _ = g.shape
    rb = _sum_block(r // 2)
    nb = r // 2 // rb
    c = lax.axis_index("c").astype(jnp.int32).reshape(1)

    def body(c_ref, g_ref, r_ref, o_ref):
        o_ref[...] = (g_ref[...].astype(jnp.float32) + r_ref[...].astype(jnp.float32)).astype(o_ref.dtype)

    return pl.pallas_call(
        body, name=name,
        grid_spec=pltpu.PrefetchScalarGridSpec(
            num_scalar_prefetch=1, grid=(N_SHARD, nb),
            in_specs=[pl.BlockSpec((None, rb, D), lambda s, i, c_ref: (s, c_ref[0] * nb + i, 0)),
                      pl.BlockSpec((None, rb, D), lambda s, i, c_ref: (s, i, 0))],
            out_specs=pl.BlockSpec((None, rb, D), lambda s, i, c_ref: (s, i, 0))),
        out_shape=jax.ShapeDtypeStruct((N_SHARD, r // 2, D), g.dtype),
        compiler_params=_params(("arbitrary", "arbitrary")),
    )(c, g, recv)


def _sum_chips(name, p, recv):
    _, rh, _ = p.shape
    rb = _sum_block(rh)
    nb = rh // rb
    s = jnp.stack([2 * lax.axis_index("x") + lax.axis_index("y"), lax.axis_index("c")]).astype(jnp.int32)

    def body(s_ref, p_ref, r_ref, o_ref):
        acc = p_ref[...].astype(jnp.float32)
        for j in range(3):
            acc = acc + r_ref[j].astype(jnp.float32)
        o_ref[...] = acc

    return pl.pallas_call(
        body, name=name,
        grid_spec=pltpu.PrefetchScalarGridSpec(
            num_scalar_prefetch=1, grid=(nb,),
            in_specs=[pl.BlockSpec((None, rb, D), lambda i, s_ref: (s_ref[0], i, 0)),
                      pl.BlockSpec((3, rb, D), lambda i, s_ref: (0, i, 0))],
            out_specs=pl.BlockSpec((rb, D), lambda i, s_ref: (s_ref[1] * nb + i, 0))),
        out_shape=jax.ShapeDtypeStruct((2 * rh, D), jnp.float32),
        compiler_params=_params(),
    )(s, p, recv)


def _rs_ici_plan():
    remote = []
    for j, mask in enumerate(_ICI_MASKS):
        remote.append((mask,
                       lambda bufs, me, mask=mask: bufs[0].at[_chip(_flip(me, mask))],
                       lambda bufs, me, j=j: bufs[1].at[j]))
    return remote


def _rs_pair(tag, g):
    _, r, _ = g.shape
    remote = [(_D2D_MASK,
               lambda al, ins, outs, me, s=s: ins[0].at[s, _other_half(me, r)],
               lambda al, ins, outs, me, s=s: outs[0].at[s]) for s in range(N_SHARD)]
    (recv,) = _push("rs_d2d_" + tag, [], [g], [jax.ShapeDtypeStruct((N_SHARD, r // 2, D), g.dtype)], remote)
    return _sum_pair("rs_sum_pair_" + tag, g, recv)


def _rs_start(tag, g, rs_plan):
    p = _rs_pair(tag, g)
    landing = lax.empty((3, p.shape[1], D), p.dtype)
    send, recv, flying, token = _push_start("rs_ici_" + tag + "_start", [p, landing], rs_plan)
    return (tag, send, recv, flying), token[0, 0]


def _rs_end(started, after, rs_plan):
    tag, send, recv, flying = started
    p, recv3 = _push_wait("rs_ici_" + tag + "_wait", send, recv, flying, after, rs_plan)
    mine = _sum_chips("rs_sum_chips_" + tag, p, recv3)
    r = mine.shape[0]
    half = lambda al, ins, outs, me: al[0].at[_half(me, r)]
    (both,) = _push("rs_share_" + tag, [mine], [], [], [(_D2D_MASK, half, half)])
    return both


def _all_reduce_small(v):
    s = v.shape[0]
    me = _me()
    every = lax.dynamic_update_slice(lax.empty((8, s, D), jnp.float32), v[None], (4 * me[0] + 2 * me[1] + me[2], 0, 0))
    slot = lambda al, ins, outs, me: al[0].at[4 * me[0] + 2 * me[1] + me[2]]
    (every,) = _push("small_all", [every], [], [], [(mask, slot, slot) for mask in _ALL_MASKS])

    def body(e_ref, o_ref):
        acc = e_ref[0]
        for j in range(1, 8):
            acc = acc + e_ref[j]
        o_ref[...] = acc

    return pl.pallas_call(
        body, name="small_sum", grid=(1,),
        in_specs=[pl.BlockSpec((8, s, D), lambda i: (0, 0, 0))],
        out_specs=pl.BlockSpec((s, D), lambda i: (0, 0)),
        out_shape=jax.ShapeDtypeStruct((s, D), jnp.float32),
        compiler_params=_params(),
    )(every)


def _adamw(name, w, g, m, v):
    r, c = w.shape
    rb = next(b for b in (256, 344, 128, 64, 32, 16, 8, r) if r % b == 0)

    def body(w_ref, g_ref, m_ref, v_ref, d_ref, mo_ref, vo_ref):
        g = g_ref[...]
        m = ADAM_B1 * m_ref[...] + (1.0 - ADAM_B1) * g
        v = ADAM_B2 * v_ref[...] + (1.0 - ADAM_B2) * (g * g)
        m_hat = m / (1.0 - ADAM_B1 ** ADAM_STEP)
        v_hat = v / (1.0 - ADAM_B2 ** ADAM_STEP)
        d_ref[...] = -ADAM_LR * (m_hat / (jnp.sqrt(v_hat) + ADAM_EPS) + ADAM_WD * w_ref[...])
        mo_ref[...] = m
        vo_ref[...] = v

    spec = pl.BlockSpec((rb, c), lambda i: (i, 0))
    return pl.pallas_call(
        body, name=name, grid=(r // rb,),
        in_specs=[spec] * 4, out_specs=[spec] * 3,
        out_shape=[jax.ShapeDtypeStruct((r, c), jnp.float32)] * 3,
        compiler_params=_params(),
    )(w, g, m, v)


_BIG = ("ffn1_w13", "ffn2_w13", "ffn1_w2", "ffn2_w2", "w_in", "w_out", "conv_pw")
_BIG_SHARD = {"ffn1_w13": (D, 1376), "ffn2_w13": (D, 1376), "ffn1_w2": (688, D), "ffn2_w2": (688, D),
              "w_in": (D, 704), "w_out": (256, D), "conv_pw": (64, 256)}


def _pack_rows(parts):
    flat = jnp.concatenate([p.reshape(-1) for p in parts])
    pad = (-flat.shape[0]) % (8 * D)
    if pad:
        flat = jnp.concatenate([flat, jnp.zeros((pad,), flat.dtype)])
    return flat.reshape(-1, D)


def _unpack_rows(buf, shapes):
    flat = buf.reshape(-1)
    out, off = [], 0
    for shp in shapes:
        n = math.prod(shp)
        out.append(flat[off:off + n].reshape(shp))
        off += n
    return out


def _pack_shard(parts):
    zeros = lambda n: jnp.zeros((n, D), parts["w_out"].dtype)
    a = jnp.concatenate([parts["ffn1_w13"].T, parts["ffn1_w2"], zeros(A_ROWS - OFF_WIN)], axis=0)
    b = jnp.concatenate([parts["ffn2_w13"].T, parts["ffn2_w2"], parts["w_in"].T, parts["w_out"],
                         parts["conv_pw"].reshape(PW_ROWS, D), zeros(B_ROWS - OFF_PW - PW_ROWS)], axis=0)
    return a, b


def _unpack_shard(a, b):
    return {"ffn1_w13": a[OFF_W13:OFF_W13 + W13_ROWS].T, "ffn1_w2": a[OFF_W2:OFF_W2 + W2_ROWS],
            "ffn2_w13": b[OFF_W13:OFF_W13 + W13_ROWS].T, "ffn2_w2": b[OFF_W2:OFF_W2 + W2_ROWS],
            "w_in": b[OFF_WIN:OFF_WIN + WIN_ROWS].T, "w_out": b[OFF_WOUT:OFF_WOUT + WOUT_ROWS],
            "conv_pw": b[OFF_PW:OFF_PW + PW_ROWS].reshape(64, 256)}


def kernel(x, meta, ln_in_g, ln_in_b, ffn1_w13, ffn1_w2, w_in, pool_w, pool_scale, conv_dw, conv_db, conv_ln_g, conv_ln_b, conv_pw, ret_gn_g, w_out, ffn2_w13, ffn2_w2, ln_g, ln_b, loss_target, m_meta, m_ln_in_g, m_ln_in_b, m_ffn1_w13, m_ffn1_w2, m_w_in, m_pool_w, m_pool_scale, m_conv_dw, m_conv_db, m_conv_ln_g, m_conv_ln_b, m_conv_pw, m_ret_gn_g, m_w_out, m_ffn2_w13, m_ffn2_w2, m_ln_g, m_ln_b, v_meta, v_ln_in_g, v_ln_in_b, v_ffn1_w13, v_ffn1_w2, v_w_in, v_pool_w, v_pool_scale, v_conv_dw, v_conv_db, v_conv_ln_g, v_conv_ln_b, v_conv_pw, v_ret_gn_g, v_w_out, v_ffn2_w13, v_ffn2_w2, v_ln_g, v_ln_b):
    f32 = jnp.float32
    seq = x.shape[1]
    t = seq + ROW0
    me = _me()
    chip = _chip(me)
    big_w = {"ffn1_w13": ffn1_w13, "ffn2_w13": ffn2_w13, "ffn1_w2": ffn1_w2, "ffn2_w2": ffn2_w2,
             "w_in": w_in, "w_out": w_out, "conv_pw": conv_pw}

    wa, wb = [], []
    for l in range(DEPTH):
        a, b = _pack_shard({n: big_w[n][l].astype(_WIRE) for n in _BIG})
        wa.append(_own_slot(a))
        wb.append(_own_slot(b))
    small_shapes = [(N_META, 256), (DEPTH, CONV_W, 64), (DEPTH, 3, 256), (DEPTH, 3, 256)]
    small_all = _own_slot(_pack_rows([meta, conv_dw, ln_g, ln_b]))
    wrap = lambda f: (lambda al, ins, outs, me: f(al, me))
    (small_all,) = _push("gather_small", [small_all], [], [],
                         [(m, wrap(s), wrap(d)) for m, s, d in _gather_ici_plan([], True)])
    plan_a0 = _gather_ici_plan([A_ROWS], False)
    plan_b0 = _gather_ici_plan([B_ROWS], False)
    plan_l1 = _gather_ici_plan([A_ROWS, B_ROWS], False)
    a0_send, a0_recv, a0_flying, a0_token = _push_start("gather_ici_a0_start", [wa[0]], plan_a0)

    sm = [_unpack_rows(small_all[s], small_shapes) for s in range(N_SHARD)]
    meta_f = jnp.concatenate([sm[s][0] for s in range(N_SHARD)], axis=1) + a0_token[0, 0]
    cdw_f = jnp.concatenate([sm[s][1] for s in range(N_SHARD)], axis=2)
    lng_f = jnp.concatenate([sm[s][2] for s in range(N_SHARD)], axis=2)
    lnb_f = jnp.concatenate([sm[s][3] for s in range(N_SHARD)], axis=2)

    def mix_params(l):
        wbd = jnp.zeros((D_POOL, D_POOL), f32)
        for g in range(4):
            wbd = wbd.at[64 * g:64 * (g + 1), 64 * g:64 * (g + 1)].set(pool_w[l, g])
        cdw = jnp.pad(cdw_f[l], ((0, 1), (0, 0)))
        cvec = jnp.pad(jnp.stack([conv_db[l], conv_ln_g[l], conv_ln_b[l]]), ((0, 5), (0, 0)))
        wpw = wb[l][:, OFF_PW:OFF_PW + PW_ROWS].reshape(D_CONV, D_CONV)
        return (wbd.astype(_MM), pool_scale[l][None], cdw, cvec, wpw, ret_gn_g[l][None])

    gb_of = lambda l, i: jnp.stack([lng_f[l, i], lnb_f[l, i]])
    gb_in = jnp.stack([ln_in_g, ln_in_b])

    pos = jnp.arange(t, dtype=f32) - PAD
    inv_freq = ROPE_BASE ** (-jnp.arange(0, DH, 2, dtype=f32) / DH)
    ang = pos[:, None] * inv_freq[None, :]
    cs = jnp.concatenate([jnp.cos(ang), jnp.cos(ang)], axis=1)
    sn = jnp.concatenate([-jnp.sin(ang), jnp.sin(ang)], axis=1)

    raw = jnp.concatenate([jnp.zeros((PAD, D), f32), meta_f, x[0]], axis=0)
    target = jnp.concatenate([jnp.zeros((ROW0, D), f32), loss_target[0]], axis=0)
    xh, rstd = _ln_in_fwd(raw)
    (wa[0],) = _push_wait("gather_ici_a0_wait", a0_send, a0_recv, a0_flying, xh, plan_a0)
    (wa[0],) = _gather_d2d("gather_d2d_a0", [wa[0]])
    b0_send, b0_recv, b0_flying, b0_token = _push_start("gather_ici_b0_start", [wb[0]], plan_b0)
    cur = (xh, rstd, gb_in + b0_token[0, 0])
    saved = []
    for l in range(DEPTH):
        if l == 1:
            wa[1], wb[1] = _push_wait("gather_ici_l1_wait", l1_send, l1_recv, l1_flying, cur[0], plan_l1)
            wa[1], wb[1] = _gather_d2d("gather_d2d_l1", [wa[1], wb[1]])
        a0 = cur
        xh1, r1, au1, hb1 = _ffn_fwd(f"ffn1_fwd_{l}", a0[0], a0[2], wa[l])
        a1 = (xh1, r1, gb_of(l, 0))
        if l == 0:
            (wb[0],) = _push_wait("gather_ici_b0_wait", b0_send, b0_recv, b0_flying, xh1, plan_b0)
            (wb[0],) = _gather_d2d("gather_d2d_b0", [wb[0]])
            l1_send, l1_recv, l1_flying, l1_token = _push_start("gather_ici_l1_start", [wa[1], wb[1]], plan_l1)
            a1 = (xh1, r1, a1[2] + l1_token[0, 0])
        z = _mix_in_fwd(f"mix_in_fwd_{l}", a1[0], a1[2], wb[l])
        mp = mix_params(l)
        ycat, st_in, ut_in, cv_in, yp_in = _mix_core_fwd(f"mix_core_fwd_{l}", z, cs, sn, *mp)
        xt_in = (cv_in, yp_in)
        xh2, r2 = _mix_out_fwd(f"mix_out_fwd_{l}", a1[0], a1[2], ycat, wb[l])
        a2 = (xh2, r2, gb_of(l, 1))
        xh3, r3, au2, hb2 = _ffn_fwd(f"ffn2_fwd_{l}", a2[0], a2[2], wb[l])
        a3 = (xh3, r3, gb_of(l, 2))
        saved.append((a0, a1, a2, a3, z, ycat, st_in, ut_in, xt_in, mp, au1, hb1, au2, hb2))
        cur = a3

    dy, loss_part = _loss_fwd_bwd(cur[0], cur[2], target)
    loss = lax.psum(loss_part[0, 0], ("x", "y", "c"))

    g_ln_g = [[None] * 3 for _ in range(DEPTH)]
    g_ln_b = [[None] * 3 for _ in range(DEPTH)]
    g_small = [dict() for _ in range(DEPTH)]
    slot = lambda j: j

    def ffn_grads(gbuf, tag, l, hb, hid, dau, dffn):
        gbuf = _dw_into(f"dw13_{tag}_{l}", gbuf, dau, hb, FF_SLOT, [(0, W13_ROWS, slot, OFF_W13)])
        return _dw_into(f"dw2_{tag}_{l}", gbuf, hid, dffn, FF_SLOT,
                        [(0, W2_ROWS, lambda j: 2 * j, OFF_W2), (W2_ROWS, W2_ROWS, lambda j: 2 * j + 1, OFF_W2)])

    rs_plan = _rs_ici_plan()
    token = jnp.zeros((), f32)
    started = {}
    for l in reversed(range(DEPTH)):
        a0, a1, a2, a3, z, ycat, st_in, ut_in, xt_in, mp, au1, hb1, au2, hb2 = saved[l]
        g_a = lax.empty((N_SHARD, A_ROWS, D), _WIRE)
        g_b = lax.empty((N_SHARD, B_ROWS, D), _WIRE)
        dh, hid, dau, dffn, dgb = _ffn_bwd(f"ffn2_bwd_{l}", dy, a3[0], a3[1], a3[2] + token, au2, wb[l])
        g_ln_g[l][2], g_ln_b[l][2] = dgb[0], dgb[1]
        g_b = ffn_grads(g_b, "ffn2", l, hb2, hid, dau, dffn)
        dh_res, dycat, dsb, dgb = _mix_out_bwd(f"mix_out_bwd_{l}", dh, a2[0], a2[1], a2[2], wb[l])
        g_ln_g[l][1], g_ln_b[l][1] = dgb[0], dgb[1]
        g_b = _dw_into(f"dw_out_{l}", g_b, ycat, dsb, D,
                       [(WOUT_ROWS * s, WOUT_ROWS, lambda j, s=s: s, OFF_WOUT) for s in range(N_SHARD)])
        dz, dwbd, dwpw, dcdw, dsm = _mix_core_bwd(f"mix_core_bwd_{l}", z, dycat, cs, sn, st_in, ut_in, *xt_in, *mp)
        pw = jnp.concatenate([dwpw.astype(_WIRE).reshape(N_SHARD, PW_ROWS, D),
                              jnp.zeros((N_SHARD, B_ROWS - OFF_PW - PW_ROWS, D), _WIRE)], axis=1)
        g_b = lax.dynamic_update_slice(g_b, pw, (0, OFF_PW, 0))
        g_small[l] = dict(
            pool_w=jnp.stack([dwbd[64 * g:64 * (g + 1), 64 * g:64 * (g + 1)] for g in range(4)]),
            pool_scale=dsm[1, :256], conv_db=dsm[2, :256], conv_ln_g=dsm[3, :256], conv_ln_b=dsm[4, :256],
            ret_gn_g=dsm[0], conv_dw=dcdw[:CONV_W])
        dh, hb = _mix_in_bwd(f"mix_in_bwd_{l}", dh_res, dz, a1[0], a1[2], wb[l])
        g_b = _dw_into(f"dw_in_{l}", g_b, dz, hb, D_IN,
                       [(WIN_ROWS * s, WIN_ROWS, lambda j, s=s: s, OFF_WIN) for s in range(N_SHARD)])
        started["b", l], token = _rs_start(f"b{l}", g_b, rs_plan)
        dh, hid, dau, dffn, dgb = _ffn_bwd(f"ffn1_bwd_{l}", dh, a1[0], a1[1], a1[2] + token, au1, wa[l])
        g_ln_g[l][0], g_ln_b[l][0] = dgb[0], dgb[1]
        g_a = ffn_grads(g_a, "ffn1", l, hb1, hid, dau, dffn)
        g_a = lax.dynamic_update_slice(g_a, jnp.zeros((N_SHARD, A_ROWS - OFF_WIN, D), _WIRE), (0, OFF_WIN, 0))
        dy = dh
        started["a", l], token = _rs_start(f"a{l}", g_a, rs_plan)
    d_raw, dgb_in = _ln_in_bwd(dy, saved[0][0][0], saved[0][0][1], gb_in + token)
    grad_x = d_raw[ROW0:][None]

    gsum, after = {}, d_raw
    for key in (("b", 1), ("a", 1), ("b", 0), ("a", 0)):
        gsum[key] = after = _rs_end(started[key], after, rs_plan)
    g_big = [_unpack_shard(gsum["a", l], gsum["b", l]) for l in range(DEPTH)]
    grads = {n: jnp.stack([g_big[l][n] for l in range(DEPTH)]) for n in _BIG}

    small_parts = [
        d_raw[PAD:ROW0],
        jnp.stack([g_small[l]["conv_dw"] for l in range(DEPTH)]),
        jnp.stack([jnp.stack(g_ln_g[l]) for l in range(DEPTH)]),
        jnp.stack([jnp.stack(g_ln_b[l]) for l in range(DEPTH)]),
        dgb_in[0], dgb_in[1],
        jnp.stack([g_small[l]["pool_w"] for l in range(DEPTH)]),
        jnp.stack([g_small[l]["pool_scale"] for l in range(DEPTH)]),
        jnp.stack([g_small[l]["conv_db"] for l in range(DEPTH)]),
        jnp.stack([g_small[l]["conv_ln_g"] for l in range(DEPTH)]),
        jnp.stack([g_small[l]["conv_ln_b"] for l in range(DEPTH)]),
        jnp.stack([g_small[l]["ret_gn_g"] for l in range(DEPTH)]),
    ]
    red = _unpack_rows(_all_reduce_small(_pack_rows(small_parts)), [p.shape for p in small_parts])
    grads["meta"] = lax.dynamic_slice_in_dim(red[0], 256 * chip, 256, axis=1)
    grads["conv_dw"] = lax.dynamic_slice_in_dim(red[1], 64 * chip, 64, axis=2)
    grads["ln_g"] = lax.dynamic_slice_in_dim(red[2], 256 * chip, 256, axis=2)
    grads["ln_b"] = lax.dynamic_slice_in_dim(red[3], 256 * chip, 256, axis=2)
    for n, v in zip(("ln_in_g", "ln_in_b", "pool_w", "pool_scale", "conv_db", "conv_ln_g", "conv_ln_b", "ret_gn_g"),
                    red[4:]):
        grads[n] = v

    names = ['meta', 'ln_in_g', 'ln_in_b', 'ffn1_w13', 'ffn1_w2', 'w_in', 'pool_w', 'pool_scale', 'conv_dw',
             'conv_db', 'conv_ln_g', 'conv_ln_b', 'conv_pw', 'ret_gn_g', 'w_out', 'ffn2_w13', 'ffn2_w2', 'ln_g', 'ln_b']
    ws = dict(meta=meta, ln_in_g=ln_in_g, ln_in_b=ln_in_b, ffn1_w13=ffn1_w13, ffn1_w2=ffn1_w2, w_in=w_in,
              pool_w=pool_w, pool_scale=pool_scale, conv_dw=conv_dw, conv_db=conv_db, conv_ln_g=conv_ln_g,
              conv_ln_b=conv_ln_b, conv_pw=conv_pw, ret_gn_g=ret_gn_g, w_out=w_out, ffn2_w13=ffn2_w13,
              ffn2_w2=ffn2_w2, ln_g=ln_g, ln_b=ln_b)
    ms = dict(meta=m_meta, ln_in_g=m_ln_in_g, ln_in_b=m_ln_in_b, ffn1_w13=m_ffn1_w13, ffn1_w2=m_ffn1_w2,
              w_in=m_w_in, pool_w=m_pool_w, pool_scale=m_pool_scale, conv_dw=m_conv_dw, conv_db=m_conv_db,
              conv_ln_g=m_conv_ln_g, conv_ln_b=m_conv_ln_b, conv_pw=m_conv_pw, ret_gn_g=m_ret_gn_g,
              w_out=m_w_out, ffn2_w13=m_ffn2_w13, ffn2_w2=m_ffn2_w2, ln_g=m_ln_g, ln_b=m_ln_b)
    vs = dict(meta=v_meta, ln_in_g=v_ln_in_g, ln_in_b=v_ln_in_b, ffn1_w13=v_ffn1_w13, ffn1_w2=v_ffn1_w2,
              w_in=v_w_in, pool_w=v_pool_w, pool_scale=v_pool_scale, conv_dw=v_conv_dw, conv_db=v_conv_db,
              conv_ln_g=v_conv_ln_g, conv_ln_b=v_conv_ln_b, conv_pw=v_conv_pw, ret_gn_g=v_ret_gn_g,
              w_out=v_w_out, ffn2_w13=v_ffn2_w13, ffn2_w2=v_ffn2_w2, ln_g=v_ln_g, ln_b=v_ln_b)
    delta, new_m, new_v = {}, {}, {}
    for n in _BIG:
        shp = ws[n].shape
        two = lambda a: a.reshape(-1, shp[-1])
        d_, m_, v_ = _adamw("adamw_" + n, two(ws[n]), two(grads[n]), two(ms[n]), two(vs[n]))
        delta[n], new_m[n], new_v[n] = d_.reshape(shp), m_.reshape(shp), v_.reshape(shp)
    small_names = [n for n in names if n not in _BIG]
    pk = lambda d: _pack_rows([d[n] for n in small_names])
    d_, m_, v_ = _adamw("adamw_small", pk(ws), pk(grads), pk(ms), pk(vs))
    shapes = [ws[n].shape for n in small_names]
    for n, a, b, c in zip(small_names, _unpack_rows(d_, shapes), _unpack_rows(m_, shapes), _unpack_rows(v_, shapes)):
        delta[n], new_m[n], new_v[n] = a, b, c

    return (loss, grad_x, *[grads[n] for n in names], *[delta[n] for n in names],
            *[new_m[n] for n in names], *[new_v[n] for n in names])
```

```python
import functools
import math

import jax
import jax.numpy as jnp
from jax import lax
from jax.experimental import pallas as pl
from jax.experimental.pallas import tpu as pltpu

D = 1024
DEPTH = 2
N_META = 16
PAD = 112
ROW0 = PAD + N_META
D_POOL = 256
D_CONV = 256
D_RET = 512
HEADS = 4
DH = 128
CONV_W = 31
D_FF = 2752
FF_SLOT = 1408
D_FFP = 2 * FF_SLOT
D_IN = 2816
N_SHARD = 4
ALPHA = (2.0 * DEPTH) ** 0.25
LN_EPS = 1e-5
ROPE_BASE = 10000.0
LOG_GAMMA = tuple(math.log(1.0 - 2.0 ** (-5.0 - h)) for h in range(HEADS))
ADAM_LR, ADAM_B1, ADAM_B2, ADAM_EPS, ADAM_WD, ADAM_STEP = 0.001, 0.9, 0.999, 1e-08, 0.01, 10

_MM = jnp.bfloat16
_WIRE = jnp.bfloat16
_VMEM_LIMIT = 56 * 1024 * 1024
_FFN_ROWS = 640

MESH = pl.DeviceIdType.MESH
_ANY = pl.BlockSpec(memory_space=pl.ANY)

W13_ROWS = 1376
W2_ROWS = 688
WIN_ROWS = 704
WOUT_ROWS = 256
PW_ROWS = 16
OFF_W13 = 0
OFF_W2 = W13_ROWS
OFF_WIN = W13_ROWS + W2_ROWS
OFF_WOUT = OFF_WIN + WIN_ROWS
OFF_PW = OFF_WOUT + WOUT_ROWS
A_ROWS = 2080
B_ROWS = 3072
W2_SLOT_OFF = (0, W2_ROWS, FF_SLOT, FF_SLOT + W2_ROWS)


def _row_copies(w_ref, off, n, dst_of, sems, k0):
    return [pltpu.make_async_copy(w_ref.at[s, pl.ds(off, n)], dst_of(s), sems.at[k0 + s]) for s in range(N_SHARD)]


def _ffn_weight_copies(w_ref, w13, w2, sems):
    return (_row_copies(w_ref, OFF_W13, W13_ROWS, lambda s: w13.at[s, pl.ds(0, W13_ROWS)], sems, 0),
            _row_copies(w_ref, OFF_W2, W2_ROWS, lambda s: w2.at[pl.ds(W2_SLOT_OFF[s], W2_ROWS)], sems, 4))


def _start_ffn_weights(copies, w13, w2):
    for cp in copies[0] + copies[1]:
        cp.start()
    zpad = jnp.zeros((FF_SLOT - W13_ROWS, D), w13.dtype)
    for s in range(N_SHARD):
        w13[s, W13_ROWS:FF_SLOT, :] = zpad
    w2[W13_ROWS:FF_SLOT, :] = zpad
    w2[FF_SLOT + W13_ROWS:D_FFP, :] = zpad


def _wait_at_first_step(copies):
    @pl.when(pl.program_id(0) == 0)
    def _():
        for cp in copies:
            cp.wait()


def _load_rows(w_ref, off, n, dst, sems):
    cps = _row_copies(w_ref, off, n, lambda s: dst.at[pl.ds(s * n, n)], sems, 0)
    for cp in cps:
        cp.start()
    for cp in cps:
        cp.wait()


def _dot(a, b):
    return jnp.dot(a, b, preferred_element_type=jnp.float32)


def _dot_nt(a, b):
    return lax.dot_general(a, b, (((1,), (1,)), ((), ())), preferred_element_type=jnp.float32)


def _dot_tn(a, b):
    return lax.dot_general(a, b, (((0,), (0,)), ((), ())), preferred_element_type=jnp.float32)


def _params(sem=("arbitrary",)):
    return pltpu.CompilerParams(dimension_semantics=sem, vmem_limit_bytes=_VMEM_LIMIT)


def _row_block(t, cap=640):
    for rb in (640, 320, 128):
        if rb <= cap and t % rb == 0 and (t > 1024 or rb == 128):
            return rb
    raise ValueError(t)


def _rows(rb, n):
    return pl.BlockSpec((rb, n), lambda i: (i, 0))


def _full(shape):
    nd = len(shape)
    return pl.BlockSpec(tuple(shape), lambda i: (0,) * nd, pipeline_mode=pl.Buffered(1))


def _acc(shape):
    nd = len(shape)
    return pl.BlockSpec(tuple(shape), lambda i: (0,) * nd)


def _sigmoid(x):
    return 1.0 / (1.0 + jnp.exp(-x))


def _ln_fwd(s):
    mu = jnp.mean(s, axis=-1, keepdims=True)
    xc = s - mu
    var = jnp.mean(xc * xc, axis=-1, keepdims=True)
    rstd = lax.rsqrt(var + LN_EPS)
    return xc * rstd, rstd


def _ln_bwd(dxh, xh, rstd):
    m1 = jnp.mean(dxh, axis=-1, keepdims=True)
    m2 = jnp.mean(dxh * xh, axis=-1, keepdims=True)
    return rstd * (dxh - m1 - xh * m2)


def _ln_in_fwd(raw):
    t = raw.shape[0]
    rb = _row_block(t)

    def body(raw_ref, xh_ref, rstd_ref):
        xh, rstd = _ln_fwd(raw_ref[...])
        xh_ref[...] = xh
        rstd_ref[...] = rstd

    return pl.pallas_call(
        body, name="ln_in_fwd", grid=(t // rb,),
        in_specs=[_rows(rb, D)],
        out_specs=[_rows(rb, D), _rows(rb, 1)],
        out_shape=[jax.ShapeDtypeStruct((t, D), jnp.float32), jax.ShapeDtypeStruct((t, 1), jnp.float32)],
        compiler_params=_params(),
    )(raw)


def _ffn_fwd(name, xh, gb, wfull):
    t = xh.shape[0]
    rb = _row_block(t, _FFN_ROWS)

    def body(xh_ref, gb_ref, w_ref, out_ref, rstd_ref, au_ref, hb_ref, w13, w2, sems):
        copies = _ffn_weight_copies(w_ref, w13, w2, sems)

        @pl.when(pl.program_id(0) == 0)
        def _():
            _start_ffn_weights(copies, w13, w2)

        h = xh_ref[...] * gb_ref[0:1, :] + gb_ref[1:2, :]
        hb = h.astype(_MM)
        hb_ref[...] = hb
        _wait_at_first_step(copies[0])
        acc = jnp.zeros((rb, D), jnp.float32)
        for j in range(2):
            lo = j * FF_SLOT
            a = _dot_nt(hb, w13[j])
            u = _dot_nt(hb, w13[2 + j])
            au_ref[:, lo:lo + FF_SLOT] = a.astype(_MM)
            au_ref[:, D_FFP + lo:D_FFP + lo + FF_SLOT] = u.astype(_MM)
            hid = (a * _sigmoid(a) * u).astype(_MM)
            if j == 0:
                _wait_at_first_step(copies[1])
            acc = acc + _dot(hid, w2[lo:lo + FF_SLOT, :])
        xo, rstd = _ln_fwd(ALPHA * h + 0.5 * acc)
        out_ref[...] = xo
        rstd_ref[...] = rstd

    return pl.pallas_call(
        body, name=name, grid=(t // rb,),
        in_specs=[_rows(rb, D), _full((2, D)), _ANY],
        out_specs=[_rows(rb, D), _rows(rb, 1), _rows(rb, 2 * D_FFP), _rows(rb, D)],
        out_shape=[jax.ShapeDtypeStruct((t, D), jnp.float32), jax.ShapeDtypeStruct((t, 1), jnp.float32),
                   jax.ShapeDtypeStruct((t, 2 * D_FFP), _MM), jax.ShapeDtypeStruct((t, D), _MM)],
        scratch_shapes=[pltpu.VMEM((N_SHARD, FF_SLOT, D), _MM), pltpu.VMEM((D_FFP, D), _MM),
                        pltpu.SemaphoreType.DMA((8,))],
        compiler_params=_params(),
    )(xh, gb, wfull)


def _ffn_bwd(name, dy, xo, rstd, gb_out, au, wfull):
    t = xo.shape[0]
    rb = _row_block(t, 320)

    def body(dy_ref, xo_ref, rstd_ref, gbo_ref, au_ref, w_ref,
             dh_ref, hid_ref, dau_ref, dffn_ref, dgb_ref, w13, w2, sems):
        i = pl.program_id(0)
        copies = _ffn_weight_copies(w_ref, w13, w2, sems)

        @pl.when(i == 0)
        def _():
            dgb_ref[...] = jnp.zeros_like(dgb_ref)
            _start_ffn_weights(copies, w13, w2)

        dy = dy_ref[...]
        xo = xo_ref[...]
        dgb_ref[0:1, :] += jnp.sum(dy * xo, axis=0, keepdims=True)
        dgb_ref[1:2, :] += jnp.sum(dy, axis=0, keepdims=True)
        ds = _ln_bwd(dy * gbo_ref[0:1, :], xo, rstd_ref[...])
        dffn = (0.5 * ds).astype(_MM)
        dffn_ref[...] = dffn
        dh = ALPHA * ds
        _wait_at_first_step(copies[1])
        for j in range(2):
            lo = j * FF_SLOT
            a = au_ref[:, lo:lo + FF_SLOT].astype(jnp.float32)
            u = au_ref[:, D_FFP + lo:D_FFP + lo + FF_SLOT].astype(jnp.float32)
            sg = _sigmoid(a)
            si = a * sg
            hid_ref[:, lo:lo + FF_SLOT] = (si * u).astype(_MM)
            dhid = _dot_nt(dffn, w2[lo:lo + FF_SLOT, :])
            da = (dhid * u * (sg * (1.0 + a * (1.0 - sg)))).astype(_MM)
            du = (dhid * si).astype(_MM)
            dau_ref[:, lo:lo + FF_SLOT] = da
            dau_ref[:, D_FFP + lo:D_FFP + lo + FF_SLOT] = du
            if j == 0:
                _wait_at_first_step(copies[0])
            dh = dh + _dot(da, w13[j]) + _dot(du, w13[2 + j])
        dh_ref[...] = dh

    return pl.pallas_call(
        body, name=name, grid=(t // rb,),
        in_specs=[_rows(rb, D), _rows(rb, D), _rows(rb, 1), _full((2, D)), _rows(rb, 2 * D_FFP), _ANY],
        out_specs=[_rows(rb, D), _rows(rb, D_FFP), _rows(rb, 2 * D_FFP), _rows(rb, D), _acc((8, D))],
        out_shape=[jax.ShapeDtypeStruct((t, D), jnp.float32),
                   jax.ShapeDtypeStruct((t, D_FFP), _MM), jax.ShapeDtypeStruct((t, 2 * D_FFP), _MM),
                   jax.ShapeDtypeStruct((t, D), _MM), jax.ShapeDtypeStruct((8, D), jnp.float32)],
        scratch_shapes=[pltpu.VMEM((N_SHARD, FF_SLOT, D), _MM), pltpu.VMEM((D_FFP, D), _MM),
                        pltpu.SemaphoreType.DMA((8,))],
        compiler_params=_params(),
    )(dy, xo, rstd, gb_out, au, wfull)


def _mix_in_fwd(name, xh, gb, wfull):
    t = xh.shape[0]
    rb = _row_block(t)

    def body(xh_ref, gb_ref, w_ref, z_ref, wt, sems):
        @pl.when(pl.program_id(0) == 0)
        def _():
            _load_rows(w_ref, OFF_WIN, WIN_ROWS, wt, sems)

        h = xh_ref[...] * gb_ref[0:1, :] + gb_ref[1:2, :]
        z = _dot_nt(h.astype(_MM), wt[...])
        row = pl.program_id(0) * rb + lax.broadcasted_iota(jnp.int32, (rb, 1), 0)
        z_ref[...] = jnp.where(row >= PAD, z, 0.0)

    return pl.pallas_call(
        body, name=name, grid=(t // rb,),
        in_specs=[_rows(rb, D), _full((2, D)), _ANY],
        out_specs=_rows(rb, D_IN),
        out_shape=jax.ShapeDtypeStruct((t, D_IN), jnp.float32),
        scratch_shapes=[pltpu.VMEM((D_IN, D), _MM), pltpu.SemaphoreType.DMA((4,))],
        compiler_params=_params(),
    )(xh, gb, wfull)


def _mix_in_bwd(name, dh_res, dz, xh, gb, wfull):
    t = xh.shape[0]
    rb = _row_block(t)

    def body(dhr_ref, dz_ref, xh_ref, gb_ref, w_ref, dh_ref, hb_ref, wt, sems):
        @pl.when(pl.program_id(0) == 0)
        def _():
            _load_rows(w_ref, OFF_WIN, WIN_ROWS, wt, sems)

        dh_ref[...] = dhr_ref[...] + _dot(dz_ref[...], wt[...])
        hb_ref[...] = (xh_ref[...] * gb_ref[0:1, :] + gb_ref[1:2, :]).astype(_MM)

    return pl.pallas_call(
        body, name=name, grid=(t // rb,),
        in_specs=[_rows(rb, D), _rows(rb, D_IN), _rows(rb, D), _full((2, D)), _ANY],
        out_specs=[_rows(rb, D), _rows(rb, D)],
        out_shape=[jax.ShapeDtypeStruct((t, D), jnp.float32), jax.ShapeDtypeStruct((t, D), _MM)],
        scratch_shapes=[pltpu.VMEM((D_IN, D), _MM), pltpu.SemaphoreType.DMA((4,))],
        compiler_params=_params(),
    )(dh_res, dz, xh, gb, wfull)


def _mix_out_fwd(name, xh, gb, ycat, wfull):
    t = xh.shape[0]
    rb = _row_block(t)

    def body(xh_ref, gb_ref, y_ref, w_ref, out_ref, rstd_ref, wo, sems):
        @pl.when(pl.program_id(0) == 0)
        def _():
            _load_rows(w_ref, OFF_WOUT, WOUT_ROWS, wo, sems)

        h = xh_ref[...] * gb_ref[0:1, :] + gb_ref[1:2, :]
        xo, rstd = _ln_fwd(ALPHA * h + _dot(y_ref[...], wo[...]))
        out_ref[...] = xo
        rstd_ref[...] = rstd

    return pl.pallas_call(
        body, name=name, grid=(t // rb,),
        in_specs=[_rows(rb, D), _full((2, D)), _rows(rb, D), _ANY],
        out_specs=[_rows(rb, D), _rows(rb, 1)],
        out_shape=[jax.ShapeDtypeStruct((t, D), jnp.float32), jax.ShapeDtypeStruct((t, 1), jnp.float32)],
        scratch_shapes=[pltpu.VMEM((D, D), _MM), pltpu.SemaphoreType.DMA((4,))],
        compiler_params=_params(),
    )(xh, gb, ycat, wfull)


def _mix_out_bwd(name, dy, xo, rstd, gb_out, wfull):
    t = xo.shape[0]
    rb = _row_block(t)

    def body(dy_ref, xo_ref, rstd_ref, gbo_ref, w_ref, dhr_ref, dyc_ref, dsb_ref, dgb_ref, wo, sems):
        @pl.when(pl.program_id(0) == 0)
        def _():
            dgb_ref[...] = jnp.zeros_like(dgb_ref)
            _load_rows(w_ref, OFF_WOUT, WOUT_ROWS, wo, sems)

        dy = dy_ref[...]
        xo = xo_ref[...]
        dgb_ref[0:1, :] += jnp.sum(dy * xo, axis=0, keepdims=True)
        dgb_ref[1:2, :] += jnp.sum(dy, axis=0, keepdims=True)
        ds = _ln_bwd(dy * gbo_ref[0:1, :], xo, rstd_ref[...])
        dsb = ds.astype(_MM)
        dsb_ref[...] = dsb
        dhr_ref[...] = ALPHA * ds
        dyc_ref[...] = _dot_nt(dsb, wo[...])

    return pl.pallas_call(
        body, name=name, grid=(t // rb,),
        in_specs=[_rows(rb, D), _rows(rb, D), _rows(rb, 1), _full((2, D)), _ANY],
        out_specs=[_rows(rb, D), _rows(rb, D), _rows(rb, D), _acc((8, D))],
        out_shape=[jax.ShapeDtypeStruct((t, D), jnp.float32), jax.ShapeDtypeStruct((t, D), jnp.float32),
                   jax.ShapeDtypeStruct((t, D), _MM), jax.ShapeDtypeStruct((8, D), jnp.float32)],
        scratch_shapes=[pltpu.VMEM((D, D), _MM), pltpu.SemaphoreType.DMA((4,))],
        compiler_params=_params(),
    )(dy, xo, rstd, gb_out, wfull)


def _loss_fwd_bwd(xh, gb, target):
    t = xh.shape[0]
    rb = _row_block(t)

    def body(xh_ref, gb_ref, tg_ref, dy_ref, loss_ref):
        @pl.when(pl.program_id(0) == 0)
        def _():
            loss_ref[...] = jnp.zeros_like(loss_ref)

        y = xh_ref[...] * gb_ref[0:1, :] + gb_ref[1:2, :]
        row = pl.program_id(0) * rb + lax.broadcasted_iota(jnp.int32, (rb, 1), 0)
        err = jnp.where(row >= ROW0, y - tg_ref[...], 0.0)
        dy_ref[...] = err * (1.0 / D)
        per_row = jnp.mean(err * err, axis=-1, keepdims=True)
        loss_ref[...] += 0.5 * jnp.sum(per_row, axis=0, keepdims=True)

    return pl.pallas_call(
        body, name="loss", grid=(t // rb,),
        in_specs=[_rows(rb, D), _full((2, D)), _rows(rb, D)],
        out_specs=[_rows(rb, D), _acc((1, 1))],
        out_shape=[jax.ShapeDtypeStruct((t, D), jnp.float32), jax.ShapeDtypeStruct((1, 1), jnp.float32)],
        compiler_params=_params(),
    )(xh, gb, target)


def _ln_in_bwd(dy, xh, rstd, gb):
    t = xh.shape[0]
    rb = _row_block(t)

    def body(dy_ref, xh_ref, rstd_ref, gb_ref, dx_ref, dgb_ref):
        @pl.when(pl.program_id(0) == 0)
        def _():
            dgb_ref[...] = jnp.zeros_like(dgb_ref)

        dy = dy_ref[...]
        xh = xh_ref[...]
        dgb_ref[0:1, :] += jnp.sum(dy * xh, axis=0, keepdims=True)
        dgb_ref[1:2, :] += jnp.sum(dy, axis=0, keepdims=True)
        dx_ref[...] = _ln_bwd(dy * gb_ref[0:1, :], xh, rstd_ref[...])

    return pl.pallas_call(
        body, name="ln_in_bwd", grid=(t // rb,),
        in_specs=[_rows(rb, D), _rows(rb, D), _rows(rb, 1), _full((2, D))],
        out_specs=[_rows(rb, D), _acc((8, D))],
        out_shape=[jax.ShapeDtypeStruct((t, D), jnp.float32), jax.ShapeDtypeStruct((8, D), jnp.float32)],
        compiler_params=_params(),
    )(dy, xh, rstd, gb)


def _dw_rows(t, cols):
    for tt in (2080, 1664, 640, 128):
        vmem = 2 * tt * (cols + D) * 2 + cols * D * 6
        if t % tt == 0 and (t > 1024 or tt == 128) and vmem <= 44 * 1024 * 1024:
            return tt
    raise ValueError((t, cols))


def _dw_into(name, gpack, x, y, cols, pieces):
    t, k = x.shape
    tt = _dw_rows(t, cols)
    nt = t // tt

    def body(x_ref, y_ref, g_in, g_out, acc, stage, sems):
        j = pl.program_id(0)
        s = pl.program_id(1)

        @pl.when(s == 0)
        def _():
            acc[...] = jnp.zeros_like(acc)

        acc[...] += _dot_tn(x_ref[...], y_ref[...])

        @pl.when(s == nt - 1)
        def _():
            stage[...] = acc[...].astype(stage.dtype)
            cps = []
            for q, (lo, n, chip_of, off) in enumerate(pieces):
                cp = pltpu.make_async_copy(stage.at[pl.ds(lo, n)], g_out.at[chip_of(j), pl.ds(off, n)],
                                           sems.at[q])
                cp.start()
                cps.append(cp)
            for cp in cps:
                cp.wait()

    return pl.pallas_call(
        body, name=name, grid=(k // cols, nt),
        in_specs=[pl.BlockSpec((tt, cols), lambda j, s: (s, j)), pl.BlockSpec((tt, D), lambda j, s: (s, 0)), _ANY],
        out_specs=_ANY,
        out_shape=jax.ShapeDtypeStruct(gpack.shape, gpack.dtype),
        input_output_aliases={2: 0},
        scratch_shapes=[pltpu.VMEM((cols, D), jnp.float32), pltpu.VMEM((cols, D), gpack.dtype),
                        pltpu.SemaphoreType.DMA((len(pieces),))],
        compiler_params=_params(("arbitrary", "arbitrary")),
    )(x, y, gpack)


_TAIL_U = 32
_TAIL_X = 32
_MIX_ROWS = 320


def _decay_mask(rb, h):
    ii = lax.broadcasted_iota(jnp.int32, (rb, rb), 0)
    jj = lax.broadcasted_iota(jnp.int32, (rb, rb), 1)
    dist = jnp.abs(ii - jj).astype(jnp.float32)
    vis = (jj >> 6) <= (ii >> 6)
    return jnp.where(vis, jnp.exp(LOG_GAMMA[h] * dist), 0.0)


def _row_decays(rb, h):
    r = lax.broadcasted_iota(jnp.int32, (rb, DH), 0).astype(jnp.float32)
    return jnp.exp(LOG_GAMMA[h] * (r + 1.0)), jnp.exp(LOG_GAMMA[h] * (rb - 1.0 - r))


def _rope(x, cs, sn):
    return x * cs + pltpu.roll(x, DH // 2, 1) * sn


def _rope_t(dx, cs, sn):
    return dx * cs + pltpu.roll(dx * sn, DH // 2, 1)


def _pool_count(blk, rb):
    row = blk * rb + lax.broadcasted_iota(jnp.int32, (rb, D_POOL), 0) - PAD
    lane = lax.broadcasted_iota(jnp.int32, (rb, D_POOL), 1)
    win = jnp.left_shift(2, lane >> 6)
    return jnp.clip(row + 1, 1, win).astype(jnp.float32)


def _pool_select(p2, p4, p8, p16):
    lane = lax.broadcasted_iota(jnp.int32, p2.shape, 1)
    return jnp.where(lane < 64, p2, jnp.where(lane < 128, p4, jnp.where(lane < 192, p8, p16)))


def _trailing_windows(ext, p2, p4, p8, rb):
    n = _TAIL_X + rb
    p2[8:n, :] = ext[8:n, :] + ext[pl.ds(7, n - 8), :]
    p4[16:n, :] = p2[16:n, :] + p2[pl.ds(14, n - 16), :]
    p8[24:n, :] = p4[24:n, :] + p4[pl.ds(20, n - 24), :]
    lo = _TAIL_X
    p16 = p8[lo:n, :] + p8[lo - 8:n - 8, :]
    return _pool_select(p2[lo:n, :], p4[lo:n, :], p8[lo:n, :], p16)


def _leading_windows(ext, p2, p4, p8, rb):
    p2[0:rb + 24, :] = ext[0:rb + 24, :] + ext[pl.ds(1, rb + 24), :]
    p4[0:rb + 16, :] = p2[0:rb + 16, :] + p2[pl.ds(2, rb + 16), :]
    p8[0:rb + 8, :] = p4[0:rb + 8, :] + p4[pl.ds(4, rb + 8), :]
    p16 = p8[0:rb, :] + p8[8:rb + 8, :]
    return _pool_select(p2[0:rb, :], p4[0:rb, :], p8[0:rb, :], p16)


def _shifted_copies(ext, copies, first, n, sign):
    for b in range(1, 8):
        copies[b - 1, first:first + n, :] = ext[pl.ds(first - sign * b, n), :]


def _tap(ext, copies, k, start, rows, sign):
    a, b = divmod(k, 8)
    src = ext if b == 0 else copies.at[b - 1]
    return src[pl.ds(start - sign * 8 * a, rows), :]


def _sub_rows(rb):
    return 128 if rb % 128 == 0 else 64


def _mix_core_fwd(name, z, cs, sn, wbd, pscale, cdw, cvec, wpw, gn):
    t = z.shape[0]
    rb = _row_block(t, _MIX_ROWS)
    nblk = t // rb
    sr = _sub_rows(rb)

    def body(z_ref, cs_ref, sn_ref, wbd_ref, ps_ref, cdw_ref, cvec_ref, wpw_ref, gn_ref,
             y_ref, st_ref, ut_ref, cv_ref, yp_ref,
             uext, xext, cv, p2, p4, p8, ucopies, state, wmask):
        i = pl.program_id(0)

        @pl.when(i == 0)
        def _():
            state[...] = jnp.zeros_like(state)
            uext[0:_TAIL_U, :] = jnp.zeros((_TAIL_U, D_CONV), jnp.float32)
            xext[0:_TAIL_X, :] = jnp.zeros((_TAIL_X, D_POOL), jnp.float32)
            for h in range(HEADS):
                wmask[h] = _decay_mask(rb, h)

        st_ref[0] = state[...]
        ut_ref[0] = uext[0:_TAIL_U, :]

        xp = z_ref[:, 0:256]
        uext[_TAIL_U:_TAIL_U + rb, :] = z_ref[:, 256:512] * _sigmoid(z_ref[:, 512:768])
        xext[_TAIL_X:_TAIL_X + rb, :] = xp

        _shifted_copies(uext, ucopies, 8, rb + _TAIL_U - 8, 1)
        for r in range(0, rb, sr):
            acc = jnp.zeros((sr, D_CONV), jnp.float32)
            for k in range(CONV_W):
                acc = acc + _tap(uext, ucopies, k, _TAIL_U + r, sr, 1) * cdw_ref[CONV_W - 1 - k:CONV_W - k, :]
            cv[r:r + sr, :] = acc

        win = _trailing_windows(xext, p2, p4, p8, rb)
        ypb = (win / _pool_count(i, rb) - xp).astype(_MM)
        yp_ref[...] = ypb
        y_ref[:, 0:256] = (_dot(ypb, wbd_ref[...]) * ps_ref[...]).astype(_MM)
        cv_ref[...] = cv[...]
        cn, _ = _ln_fwd(cv[...] + cvec_ref[0:1, :])
        ln = cn * cvec_ref[1:2, :] + cvec_ref[2:3, :]
        sw = ln * _sigmoid(ln)
        y_ref[:, 256:512] = _dot(sw.astype(_MM), wpw_ref[...]).astype(_MM)
        csv = cs_ref[...]
        snv = sn_ref[...]
        for h in range(HEADS):
            q = _rope(z_ref[:, 768 + h * DH:768 + (h + 1) * DH], csv, snv)
            k = _rope(z_ref[:, 1280 + h * DH:1280 + (h + 1) * DH], csv, snv) * (DH ** -0.5)
            vb = z_ref[:, 1792 + h * DH:1792 + (h + 1) * DH].astype(_MM)
            g = z_ref[:, 2304 + h * DH:2304 + (h + 1) * DH]
            a, b = _row_decays(rb, h)
            s = _dot_nt(q.astype(_MM), k.astype(_MM)) * wmask[h]
            o = _dot(s.astype(_MM), vb) + _dot((q * a).astype(_MM), state[h].astype(_MM))
            state[h] = math.exp(LOG_GAMMA[h] * rb) * state[h] + _dot_tn((k * b).astype(_MM), vb)
            on, _ = _ln_fwd(o)
            y_ref[:, 512 + h * DH:512 + (h + 1) * DH] = (
                g * _sigmoid(g) * on * gn_ref[:, h * DH:(h + 1) * DH]).astype(_MM)

        uext[0:_TAIL_U, :] = uext[rb:rb + _TAIL_U, :]
        xext[0:_TAIL_X, :] = xext[rb:rb + _TAIL_X, :]

    return pl.pallas_call(
        body, name=name, grid=(nblk,),
        in_specs=[_rows(rb, D_IN), _rows(rb, DH), _rows(rb, DH), _full((256, 256)), _full((1, 256)),
                  _full((32, 256)), _full((8, 256)), _full((256, 256)), _full((1, D_RET))],
        out_specs=[_rows(rb, D),
                   pl.BlockSpec((1, HEADS, DH, DH), lambda i: (i, 0, 0, 0)),
                   pl.BlockSpec((1, _TAIL_U, D_CONV), lambda i: (i, 0, 0)),
                   _rows(rb, D_CONV), _rows(rb, D_POOL)],
        out_shape=[jax.ShapeDtypeStruct((t, D), _MM),
                   jax.ShapeDtypeStruct((nblk, HEADS, DH, DH), jnp.float32),
                   jax.ShapeDtypeStruct((nblk, _TAIL_U, D_CONV), jnp.float32),
                   jax.ShapeDtypeStruct((t, D_CONV), jnp.float32),
                   jax.ShapeDtypeStruct((t, D_POOL), _MM)],
        scratch_shapes=[pltpu.VMEM((rb + _TAIL_U, D_CONV), jnp.float32),
                        pltpu.VMEM((rb + _TAIL_X, D_POOL), jnp.float32),
                        pltpu.VMEM((rb, D_CONV), jnp.float32),
                        pltpu.VMEM((rb + _TAIL_X, D_POOL), jnp.float32),
                        pltpu.VMEM((rb + _TAIL_X, D_POOL), jnp.float32),
                        pltpu.VMEM((rb + _TAIL_X, D_POOL), jnp.float32),
                        pltpu.VMEM((7, rb + _TAIL_U, D_CONV), jnp.float32),
                        pltpu.VMEM((HEADS, DH, DH), jnp.float32),
                        pltpu.VMEM((HEADS, rb, rb), jnp.float32)],
        compiler_params=_params(),
    )(z, cs, sn, wbd, pscale, cdw, cvec, wpw, gn)


def _mix_core_bwd(name, z, dyc, cs, sn, st_in, ut_in, cv_in, yp_in, wbd, pscale, cdw, cvec, wpw, gn):
    t = z.shape[0]
    rb = _row_block(t, _MIX_ROWS)
    nblk = t // rb
    sr = _sub_rows(rb)
    rev = lambda i: nblk - 1 - i

    def body(z_ref, dy_ref, cs_ref, sn_ref, st_ref, ut_ref, cv_ref, yp_ref,
             wbd_ref, ps_ref, cdw_ref, cvec_ref, wpw_ref, gn_ref,
             dz_ref, dwbd_ref, dwpw_ref, dcdw_ref, dsm_ref,
             uext, cv, dcvext, eext, p2, p4, p8, ucopies, dcopies, dstate, wmask):
        i = pl.program_id(0)
        blk = nblk - 1 - i

        @pl.when(i == 0)
        def _():
            dstate[...] = jnp.zeros_like(dstate)
            dcvext[rb:rb + _TAIL_U, :] = jnp.zeros((_TAIL_U, D_CONV), jnp.float32)
            eext[rb:rb + _TAIL_X, :] = jnp.zeros((_TAIL_X, D_POOL), jnp.float32)
            dwbd_ref[...] = jnp.zeros_like(dwbd_ref)
            dwpw_ref[...] = jnp.zeros_like(dwpw_ref)
            dcdw_ref[...] = jnp.zeros_like(dcdw_ref)
            dsm_ref[...] = jnp.zeros_like(dsm_ref)
            for h in range(HEADS):
                wmask[h] = _decay_mask(rb, h)

        row = blk * rb + lax.broadcasted_iota(jnp.int32, (rb, 1), 0)
        live = row >= PAD

        ca = z_ref[:, 256:512]
        sg_c = _sigmoid(z_ref[:, 512:768])
        uext[0:_TAIL_U, :] = ut_ref[0]
        uext[_TAIL_U:_TAIL_U + rb, :] = ca * sg_c

        cnt = _pool_count(blk, rb)
        ypb = yp_ref[...]
        dyp = dy_ref[:, 0:256]
        pm = _dot(ypb, wbd_ref[...])
        dsm_ref[1:2, 0:256] += jnp.sum(dyp * pm, axis=0, keepdims=True)
        dpm = (dyp * ps_ref[...]).astype(_MM)
        dwbd_ref[...] += _dot_tn(ypb, dpm)
        dypre = _dot_nt(dpm, wbd_ref[...])
        eext[0:rb, :] = dypre / cnt
        win = _leading_windows(eext, p2, p4, p8, rb)
        dz_ref[:, 0:256] = jnp.where(live, win - dypre, 0.0).astype(_MM)

        cn, rstd_c = _ln_fwd(cv_ref[...] + cvec_ref[0:1, :])
        ln = cn * cvec_ref[1:2, :] + cvec_ref[2:3, :]
        sg_l = _sigmoid(ln)
        swb = (ln * sg_l).astype(_MM)
        dycb = dy_ref[:, 256:512].astype(_MM)
        dwpw_ref[...] += _dot_tn(swb, dycb)
        dln = _dot_nt(dycb, wpw_ref[...]) * (sg_l * (1.0 + ln * (1.0 - sg_l)))
        dsm_ref[3:4, 0:256] += jnp.sum(dln * cn, axis=0, keepdims=True)
        dsm_ref[4:5, 0:256] += jnp.sum(dln, axis=0, keepdims=True)
        dcv = _ln_bwd(dln * cvec_ref[1:2, :], cn, rstd_c)
        dsm_ref[2:3, 0:256] += jnp.sum(dcv, axis=0, keepdims=True)
        dcvext[0:rb, :] = dcv
        _shifted_copies(uext, ucopies, 8, rb + _TAIL_U - 8, 1)
        _shifted_copies(dcvext, dcopies, 0, rb + _TAIL_U - 8, -1)
        for k in range(CONV_W):
            prod = dcv * _tap(uext, ucopies, k, _TAIL_U, rb, 1)
            dcdw_ref[CONV_W - 1 - k:CONV_W - k, :] += jnp.sum(prod, axis=0, keepdims=True)
        for r in range(0, rb, sr):
            acc = jnp.zeros((sr, D_CONV), jnp.float32)
            for k in range(CONV_W):
                acc = acc + _tap(dcvext, dcopies, k, r, sr, -1) * cdw_ref[CONV_W - 1 - k:CONV_W - k, :]
            cv[r:r + sr, :] = acc
        du = cv[...]
        dz_ref[:, 256:512] = jnp.where(live, du * sg_c, 0.0).astype(_MM)
        dz_ref[:, 512:768] = jnp.where(live, du * ca * sg_c * (1.0 - sg_c), 0.0).astype(_MM)

        csv = cs_ref[...]
        snv = sn_ref[...]
        for h in range(HEADS):
            q = _rope(z_ref[:, 768 + h * DH:768 + (h + 1) * DH], csv, snv)
            k = _rope(z_ref[:, 1280 + h * DH:1280 + (h + 1) * DH], csv, snv) * (DH ** -0.5)
            vb = z_ref[:, 1792 + h * DH:1792 + (h + 1) * DH].astype(_MM)
            g = z_ref[:, 2304 + h * DH:2304 + (h + 1) * DH]
            a, b = _row_decays(rb, h)
            qb = q.astype(_MM)
            kb = k.astype(_MM)
            qab = (q * a).astype(_MM)
            kbb = (k * b).astype(_MM)
            stb = st_ref[0, h].astype(_MM)
            sb = (_dot_nt(qb, kb) * wmask[h]).astype(_MM)
            o = _dot(sb, vb) + _dot(qab, stb)
            on, rstd_o = _ln_fwd(o)
            gnv = gn_ref[:, h * DH:(h + 1) * DH]
            sg_g = _sigmoid(g)
            si_g = g * sg_g
            dyr = dy_ref[:, 512 + h * DH:512 + (h + 1) * DH]
            dsm_ref[0:1, h * DH:(h + 1) * DH] += jnp.sum(dyr * on * si_g, axis=0, keepdims=True)
            dgate = dyr * on * gnv * (sg_g * (1.0 + g * (1.0 - sg_g)))
            dob = _ln_bwd(dyr * gnv * si_g, on, rstd_o).astype(_MM)
            dstb = dstate[h].astype(_MM)
            dsb = (_dot_nt(dob, vb) * wmask[h]).astype(_MM)
            dq = _dot(dsb, kb) + _dot_nt(dob, stb) * a
            dk = _dot_tn(dsb, qb) + _dot_nt(vb, dstb) * b
            dv = _dot_tn(sb, dob) + _dot(kbb, dstb)
            dstate[h] = math.exp(LOG_GAMMA[h] * rb) * dstate[h] + _dot_tn(qab, dob)
            dz_ref[:, 768 + h * DH:768 + (h + 1) * DH] = jnp.where(live, _rope_t(dq, csv, snv), 0.0).astype(_MM)
            dz_ref[:, 1280 + h * DH:1280 + (h + 1) * DH] = jnp.where(
                live, _rope_t(dk * (DH ** -0.5), csv, snv), 0.0).astype(_MM)
            dz_ref[:, 1792 + h * DH:1792 + (h + 1) * DH] = jnp.where(live, dv, 0.0).astype(_MM)
            dz_ref[:, 2304 + h * DH:2304 + (h + 1) * DH] = jnp.where(live, dgate, 0.0).astype(_MM)

        dcvext[rb:rb + _TAIL_U, :] = dcvext[0:_TAIL_U, :]
        eext[rb:rb + _TAIL_X, :] = eext[0:_TAIL_X, :]

    rrows = lambda n: pl.BlockSpec((rb, n), lambda i: (rev(i), 0))
    return pl.pallas_call(
        body, name=name, grid=(nblk,),
        in_specs=[rrows(D_IN), rrows(D), rrows(DH), rrows(DH),
                  pl.BlockSpec((1, HEADS, DH, DH), lambda i: (rev(i), 0, 0, 0)),
                  pl.BlockSpec((1, _TAIL_U, D_CONV), lambda i: (rev(i), 0, 0)),
                  rrows(D_CONV), rrows(D_POOL),
                  _full((256, 256)), _full((1, 256)), _full((32, 256)), _full((8, 256)), _full((256, 256)),
                  _full((1, D_RET))],
        out_specs=[rrows(D_IN), _acc((256, 256)), _acc((256, 256)), _acc((32, 256)), _acc((8, 512))],
        out_shape=[jax.ShapeDtypeStruct((t, D_IN), _MM),
                   jax.ShapeDtypeStruct((256, 256), jnp.float32), jax.ShapeDtypeStruct((256, 256), jnp.float32),
                   jax.ShapeDtypeStruct((32, 256), jnp.float32), jax.ShapeDtypeStruct((8, 512), jnp.float32)],
        scratch_shapes=[pltpu.VMEM((rb + _TAIL_U, D_CONV), jnp.float32),
                        pltpu.VMEM((rb, D_CONV), jnp.float32),
                        pltpu.VMEM((rb + _TAIL_U, D_CONV), jnp.float32),
                        pltpu.VMEM((rb + _TAIL_X, D_POOL), jnp.float32),
                        pltpu.VMEM((rb + _TAIL_X, D_POOL), jnp.float32),
                        pltpu.VMEM((rb + _TAIL_X, D_POOL), jnp.float32),
                        pltpu.VMEM((rb + _TAIL_X, D_POOL), jnp.float32),
                        pltpu.VMEM((7, rb + _TAIL_U, D_CONV), jnp.float32),
                        pltpu.VMEM((7, rb + _TAIL_U, D_CONV), jnp.float32),
                        pltpu.VMEM((HEADS, DH, DH), jnp.float32),
                        pltpu.VMEM((HEADS, rb, rb), jnp.float32)],
        compiler_params=_params(),
    )(z, dyc, cs, sn, st_in, ut_in, cv_in, yp_in, wbd, pscale, cdw, cvec, wpw, gn)


def _me():
    return lax.axis_index("x"), lax.axis_index("y"), lax.axis_index("c")


def _flip(me, mask):
    return tuple(1 - m if f else m for m, f in zip(me, mask))


def _push(name, aliased, inputs, fresh, remote):
    n_al, n_in, n_out, n_rem = len(aliased), len(inputs), len(fresh), len(remote)

    def body(*refs):
        ins = refs[n_al:n_al + n_in]
        al = refs[n_al + n_in:2 * n_al + n_in]
        outs = refs[2 * n_al + n_in:2 * n_al + n_in + n_out]
        send_sems, recv_sems = refs[2 * n_al + n_in + n_out:]
        me = _me()
        copies = []
        for k, (mask, src_fn, dst_fn) in enumerate(remote):
            cp = pltpu.make_async_remote_copy(
                src_ref=src_fn(al, ins, outs, me), dst_ref=dst_fn(al, ins, outs, me),
                send_sem=send_sems.at[k], recv_sem=recv_sems.at[k],
                device_id=_flip(me, mask), device_id_type=MESH)
            cp.start()
            copies.append(cp)
        for cp in copies:
            cp.wait()

    return pl.pallas_call(
        body, name=name,
        in_specs=[_ANY] * (n_al + n_in), out_specs=[_ANY] * (n_al + n_out),
        out_shape=[jax.ShapeDtypeStruct(a.shape, a.dtype) for a in aliased] + list(fresh),
        input_output_aliases={i: i for i in range(n_al)},
        scratch_shapes=[pltpu.SemaphoreType.DMA((n_rem,)), pltpu.SemaphoreType.DMA((n_rem,))],
    )(*aliased, *inputs)


_HBM = pl.BlockSpec(memory_space=pltpu.HBM)
_SEM = pl.BlockSpec(memory_space=pltpu.SEMAPHORE)
_EFFECT = pltpu.SideEffectType.DATAFLOW_SIDE_EFFECTING


def _push_start(name, bufs, remote):
    n, n_rem = len(bufs), len(remote)

    def body(*refs):
        ins = refs[:n]
        send_sems, recv_sems = refs[n], refs[n + 1]
        token = refs[2 * n + 2]
        me = _me()
        for k, (mask, src_fn, dst_fn) in enumerate(remote):
            pltpu.make_async_remote_copy(
                src_ref=src_fn(ins, me), dst_ref=dst_fn(ins, me),
                send_sem=send_sems.at[k], recv_sem=recv_sems.at[k],
                device_id=_flip(me, mask), device_id_type=MESH).start()
        token[...] = jnp.zeros_like(token)

    out = pl.pallas_call(
        body, name=name,
        out_shape=(pltpu.SemaphoreType.DMA((n_rem,)), pltpu.SemaphoreType.DMA((n_rem,)),
                   *[pltpu.HBM(b.shape, b.dtype) for b in bufs], jax.ShapeDtypeStruct((8, 128), jnp.float32)),
        in_specs=[_HBM] * n,
        out_specs=(_SEM, _SEM, *[_HBM] * n, pl.BlockSpec(memory_space=pltpu.VMEM)),
        input_output_aliases={i: i + 2 for i in range(n)},
        compiler_params=pltpu.CompilerParams(has_side_effects=_EFFECT),
    )(*[pltpu.with_memory_space_constraint(b, pltpu.HBM) for b in bufs])
    return out[0], out[1], list(out[2:2 + n]), out[2 + n]


def _push_wait(name, send_sems, recv_sems, bufs, after, remote):
    n = len(bufs)

    def body(*refs):
        ins = refs[:n]
        s_sems, r_sems = refs[n], refs[n + 1]
        me = _me()
        for k, (mask, src_fn, dst_fn) in enumerate(remote):
            cp = pltpu.make_async_remote_copy(
                src_ref=src_fn(ins, me), dst_ref=dst_fn(ins, me),
                send_sem=s_sems.at[k], recv_sem=r_sems.at[k],
                device_id=_flip(me, mask), device_id_type=MESH)
            cp.wait_send()
            cp.wait_recv()

    out = pl.pallas_call(
        body, name=name,
        out_shape=tuple(pltpu.HBM(b.shape, b.dtype) for b in bufs),
        in_specs=[_HBM] * n + [_SEM, _SEM, _ANY], out_specs=tuple([_HBM] * n),
        input_output_aliases={i: i for i in range(n)},
        compiler_params=pltpu.CompilerParams(has_side_effects=_EFFECT),
    )(*bufs, send_sems, recv_sems, after)
    return list(out)


_ICI_MASKS = ((0, 1, 0), (1, 0, 0), (1, 1, 0))
_D2D_MASK = (0, 0, 1)
_ALL_MASKS = tuple((a, b, c) for a in (0, 1) for b in (0, 1) for c in (0, 1))[1:]


def _chip(me):
    return 2 * me[0] + me[1]


def _half(me, rows):
    return pl.ds(me[2] * (rows // 2), rows // 2)


def _other_half(me, rows):
    return pl.ds((1 - me[2]) * (rows // 2), rows // 2)


def _sum_block(rh):
    return next((b for b in (768, 640, 512, 128) if rh % b == 0), rh)


def _own_slot(mine):
    chip = _chip(_me())
    return lax.dynamic_update_slice(lax.empty((N_SHARD,) + mine.shape, mine.dtype), mine[None],
                                    (chip,) + (0,) * mine.ndim)


def _gather_ici_plan(rows, with_small):
    remote = []
    for mask in _ICI_MASKS:
        for b, r in enumerate(rows):
            mine = lambda bufs, me, b=b, r=r: bufs[b].at[_chip(me), _half(me, r)]
            remote.append((mask, mine, mine))
        if with_small:
            mine_small = lambda bufs, me: bufs[len(rows)].at[_chip(me)]
            remote.append((mask, mine_small, mine_small))
    return remote


def _gather_d2d(name, ws):
    remote = []
    for b, w in enumerate(ws):
        for j in range(1, N_SHARD):
            theirs = lambda al, ins, outs, me, j=j, b=b, r=w.shape[1]: al[b].at[(_chip(me) + j) % N_SHARD, _half(me, r)]
            remote.append((_D2D_MASK, theirs, theirs))
    return _push(name, list(ws), [], [], remote)


def _sum_pair(name, g, recv):
    _, r, _ = g.shape
    rb = _sum_block(r // 2)
    nb = r // 2 // rb
    c = lax.axis_index("c").astype(jnp.int32).reshape(1)

    def body(c_ref, g_ref, r_ref, o_ref):
        o_ref[...] = (g_ref[...].astype(jnp.float32) + r_ref[...].astype(jnp.float32)).astype(o_ref.dtype)

    return pl.pallas_call(
        body, name=name,
        grid_spec=pltpu.PrefetchScalarGridSpec(
            num_scalar_prefetch=1, grid=(N_SHARD, nb),
            in_specs=[pl.BlockSpec((None, rb, D), lambda s, i, c_ref: (s, c_ref[0] * nb + i, 0)),
                      pl.BlockSpec((None, rb, D), lambda s, i, c_ref: (s, i, 0))],
            out_specs=pl.BlockSpec((None, rb, D), lambda s, i, c_ref: (s, i, 0))),
        out_shape=jax.ShapeDtypeStruct((N_SHARD, r // 2, D), g.dtype),
        compiler_params=_params(("arbitrary", "arbitrary")),
    )(c, g, recv)


def _sum_chips(name, p, recv):
    _, rh, _ = p.shape
    rb = _sum_block(rh)
    nb = rh // rb
    s = jnp.stack([2 * lax.axis_index("x") + lax.axis_index("y"), lax.axis_index("c")]).astype(jnp.int32)

    def body(s_ref, p_ref, r_ref, o_ref):
        acc = p_ref[...].astype(jnp.float32)
        for j in range(3):
            acc = acc + r_ref[j].astype(jnp.float32)
        o_ref[...] = acc

    return pl.pallas_call(
        body, name=name,
        grid_spec=pltpu.PrefetchScalarGridSpec(
            num_scalar_prefetch=1, grid=(nb,),
            in_specs=[pl.BlockSpec((None, rb, D), lambda i, s_ref: (s_ref[0], i, 0)),
                      pl.BlockSpec((3, rb, D), lambda i, s_ref: (0, i, 0))],
            out_specs=pl.BlockSpec((rb, D), lambda i, s_ref: (s_ref[1] * nb + i, 0))),
        out_shape=jax.ShapeDtypeStruct((2 * rh, D), jnp.float32),
        compiler_params=_params(),
    )(s, p, recv)


def _rs_ici_plan():
    remote = []
    for j, mask in enumerate(_ICI_MASKS):
        remote.append((mask,
                       lambda bufs, me, mask=mask: bufs[0].at[_chip(_flip(me, mask))],
                       lambda bufs, me, j=j: bufs[1].at[j]))
    return remote


def _rs_pair(tag, g):
    _, r, _ = g.shape
    remote = [(_D2D_MASK,
               lambda al, ins, outs, me, s=s: ins[0].at[s, _other_half(me, r)],
               lambda al, ins, outs, me, s=s: outs[0].at[s]) for s in range(N_SHARD)]
    (recv,) = _push("rs_d2d_" + tag, [], [g], [jax.ShapeDtypeStruct((N_SHARD, r // 2, D), g.dtype)], remote)
    return _sum_pair("rs_sum_pair_" + tag, g, recv)


def _rs_start(tag, g, rs_plan):
    p = _rs_pair(tag, g)
    landing = lax.empty((3, p.shape[1], D), p.dtype)
    send, recv, flying, token = _push_start("rs_ici_" + tag + "_start", [p, landing], rs_plan)
    return (tag, send, recv, flying), token[0, 0]


def _rs_end(started, after, rs_plan):
    tag, send, recv, flying = started
    p, recv3 = _push_wait("rs_ici_" + tag + "_wait", send, recv, flying, after, rs_plan)
    return _sum_chips("rs_sum_chips_" + tag, p, recv3)


def _rs_share(mines):
    remote = []
    for b, m in enumerate(mines):
        half = lambda al, ins, outs, me, b=b, r=m.shape[0]: al[b].at[_half(me, r)]
        remote.append((_D2D_MASK, half, half))
    return _push("rs_share", list(mines), [], [], remote)


def _all_reduce_small(v):
    s = v.shape[0]
    me = _me()
    every = lax.dynamic_update_slice(lax.empty((8, s, D), jnp.float32), v[None], (4 * me[0] + 2 * me[1] + me[2], 0, 0))
    slot = lambda al, ins, outs, me: al[0].at[4 * me[0] + 2 * me[1] + me[2]]
    (every,) = _push("small_all", [every], [], [], [(mask, slot, slot) for mask in _ALL_MASKS])

    def body(e_ref, o_ref):
        acc = e_ref[0]
        for j in range(1, 8):
            acc = acc + e_ref[j]
        o_ref[...] = acc

    return pl.pallas_call(
        body, name="small_sum", grid=(1,),
        in_specs=[pl.BlockSpec((8, s, D), lambda i: (0, 0, 0))],
        out_specs=pl.BlockSpec((s, D), lambda i: (0, 0)),
        out_shape=jax.ShapeDtypeStruct((s, D), jnp.float32),
        compiler_params=_params(),
    )(every)


def _adamw(name, w, g, m, v):
    r, c = w.shape
    rb = next(b for b in (256, 344, 128, 64, 32, 16, 8, r) if r % b == 0)

    def body(w_ref, g_ref, m_ref, v_ref, d_ref, mo_ref, vo_ref):
        g = g_ref[...]
        m = ADAM_B1 * m_ref[...] + (1.0 - ADAM_B1) * g
        v = ADAM_B2 * v_ref[...] + (1.0 - ADAM_B2) * (g * g)
        m_hat = m / (1.0 - ADAM_B1 ** ADAM_STEP)
        v_hat = v / (1.0 - ADAM_B2 ** ADAM_STEP)
        d_ref[...] = -ADAM_LR * (m_hat / (jnp.sqrt(v_hat) + ADAM_EPS) + ADAM_WD * w_ref[...])
        mo_ref[...] = m
        vo_ref[...] = v

    spec = pl.BlockSpec((rb, c), lambda i: (i, 0))
    return pl.pallas_call(
        body, name=name, grid=(r // rb,),
        in_specs=[spec] * 4, out_specs=[spec] * 3,
        out_shape=[jax.ShapeDtypeStruct((r, c), jnp.float32)] * 3,
        compiler_params=_params(),
    )(w, g, m, v)


_BIG = ("ffn1_w13", "ffn2_w13", "ffn1_w2", "ffn2_w2", "w_in", "w_out", "conv_pw")
_BIG_SHARD = {"ffn1_w13": (D, 1376), "ffn2_w13": (D, 1376), "ffn1_w2": (688, D), "ffn2_w2": (688, D),
              "w_in": (D, 704), "w_out": (256, D), "conv_pw": (64, 256)}


def _pack_rows(parts):
    flat = jnp.concatenate([p.reshape(-1) for p in parts])
    pad = (-flat.shape[0]) % (8 * D)
    if pad:
        flat = jnp.concatenate([flat, jnp.zeros((pad,), flat.dtype)])
    return flat.reshape(-1, D)


def _unpack_rows(buf, shapes):
    flat = buf.reshape(-1)
    out, off = [], 0
    for shp in shapes:
        n = math.prod(shp)
        out.append(flat[off:off + n].reshape(shp))
        off += n
    return out


def _pack_shard(parts):
    zeros = lambda n: jnp.zeros((n, D), parts["w_out"].dtype)
    a = jnp.concatenate([parts["ffn1_w13"].T, parts["ffn1_w2"], zeros(A_ROWS - OFF_WIN)], axis=0)
    b = jnp.concatenate([parts["ffn2_w13"].T, parts["ffn2_w2"], parts["w_in"].T, parts["w_out"],
                         parts["conv_pw"].reshape(PW_ROWS, D), zeros(B_ROWS - OFF_PW - PW_ROWS)], axis=0)
    return a, b


def _unpack_shard(a, b):
    return {"ffn1_w13": a[OFF_W13:OFF_W13 + W13_ROWS].T, "ffn1_w2": a[OFF_W2:OFF_W2 + W2_ROWS],
            "ffn2_w13": b[OFF_W13:OFF_W13 + W13_ROWS].T, "ffn2_w2": b[OFF_W2:OFF_W2 + W2_ROWS],
            "w_in": b[OFF_WIN:OFF_WIN + WIN_ROWS].T, "w_out": b[OFF_WOUT:OFF_WOUT + WOUT_ROWS],
            "conv_pw": b[OFF_PW:OFF_PW + PW_ROWS].reshape(64, 256)}


def kernel(x, meta, ln_in_g, ln_in_b, ffn1_w13, ffn1_w2, w_in, pool_w, pool_scale, conv_dw, conv_db, conv_ln_g, conv_ln_b, conv_pw, ret_gn_g, w_out, ffn2_w13, ffn2_w2, ln_g, ln_b, loss_target, m_meta, m_ln_in_g, m_ln_in_b, m_ffn1_w13, m_ffn1_w2, m_w_in, m_pool_w, m_pool_scale, m_conv_dw, m_conv_db, m_conv_ln_g, m_conv_ln_b, m_conv_pw, m_ret_gn_g, m_w_out, m_ffn2_w13, m_ffn2_w2, m_ln_g, m_ln_b, v_meta, v_ln_in_g, v_ln_in_b, v_ffn1_w13, v_ffn1_w2, v_w_in, v_pool_w, v_pool_scale, v_conv_dw, v_conv_db, v_conv_ln_g, v_conv_ln_b, v_conv_pw, v_ret_gn_g, v_w_out, v_ffn2_w13, v_ffn2_w2, v_ln_g, v_ln_b):
    f32 = jnp.float32
    seq = x.shape[1]
    t = seq + ROW0
    me = _me()
    chip = _chip(me)
    big_w = {"ffn1_w13": ffn1_w13, "ffn2_w13": ffn2_w13, "ffn1_w2": ffn1_w2, "ffn2_w2": ffn2_w2,
             "w_in": w_in, "w_out": w_out, "conv_pw": conv_pw}

    wa, wb = [], []
    for l in range(DEPTH):
        a, b = _pack_shard({n: big_w[n][l].astype(_WIRE) for n in _BIG})
        wa.append(_own_slot(a))
        wb.append(_own_slot(b))
    small_shapes = [(N_META, 256), (DEPTH, CONV_W, 64), (DEPTH, 3, 256), (DEPTH, 3, 256)]
    small_all = _own_slot(_pack_rows([meta, conv_dw, ln_g, ln_b]))
    wrap = lambda f: (lambda al, ins, outs, me: f(al, me))
    (small_all,) = _push("gather_small", [small_all], [], [],
                         [(m, wrap(s), wrap(d)) for m, s, d in _gather_ici_plan([], True)])
    plan_a0 = _gather_ici_plan([A_ROWS], False)
    plan_b0 = _gather_ici_plan([B_ROWS], False)
    plan_l1 = _gather_ici_plan([A_ROWS, B_ROWS], False)
    a0_send, a0_recv, a0_flying, a0_token = _push_start("gather_ici_a0_start", [wa[0]], plan_a0)

    sm = [_unpack_rows(small_all[s], small_shapes) for s in range(N_SHARD)]
    meta_f = jnp.concatenate([sm[s][0] for s in range(N_SHARD)], axis=1) + a0_token[0, 0]
    cdw_f = jnp.concatenate([sm[s][1] for s in range(N_SHARD)], axis=2)
    lng_f = jnp.concatenate([sm[s][2] for s in range(N_SHARD)], axis=2)
    lnb_f = jnp.concatenate([sm[s][3] for s in range(N_SHARD)], axis=2)

    def mix_params(l):
        wbd = jnp.zeros((D_POOL, D_POOL), f32)
        for g in range(4):
            wbd = wbd.at[64 * g:64 * (g + 1), 64 * g:64 * (g + 1)].set(pool_w[l, g])
        cdw = jnp.pad(cdw_f[l], ((0, 1), (0, 0)))
        cvec = jnp.pad(jnp.stack([conv_db[l], conv_ln_g[l], conv_ln_b[l]]), ((0, 5), (0, 0)))
        wpw = wb[l][:, OFF_PW:OFF_PW + PW_ROWS].reshape(D_CONV, D_CONV)
        return (wbd.astype(_MM), pool_scale[l][None], cdw, cvec, wpw, ret_gn_g[l][None])

    gb_of = lambda l, i: jnp.stack([lng_f[l, i], lnb_f[l, i]])
    gb_in = jnp.stack([ln_in_g, ln_in_b])

    pos = jnp.arange(t, dtype=f32) - PAD
    inv_freq = ROPE_BASE ** (-jnp.arange(0, DH, 2, dtype=f32) / DH)
    ang = pos[:, None] * inv_freq[None, :]
    cs = jnp.concatenate([jnp.cos(ang), jnp.cos(ang)], axis=1)
    sn = jnp.concatenate([-jnp.sin(ang), jnp.sin(ang)], axis=1)

    raw = jnp.concatenate([jnp.zeros((PAD, D), f32), meta_f, x[0]], axis=0)
    target = jnp.concatenate([jnp.zeros((ROW0, D), f32), loss_target[0]], axis=0)
    xh, rstd = _ln_in_fwd(raw)
    (wa[0],) = _push_wait("gather_ici_a0_wait", a0_send, a0_recv, a0_flying, xh, plan_a0)
    (wa[0],) = _gather_d2d("gather_d2d_a0", [wa[0]])
    b0_send, b0_recv, b0_flying, b0_token = _push_start("gather_ici_b0_start", [wb[0]], plan_b0)
    cur = (xh, rstd, gb_in + b0_token[0, 0])
    saved = []
    for l in range(DEPTH):
        if l == 1:
            wa[1], wb[1] = _push_wait("gather_ici_l1_wait", l1_send, l1_recv, l1_flying, cur[0], plan_l1)
            wa[1], wb[1] = _gather_d2d("gather_d2d_l1", [wa[1], wb[1]])
        a0 = cur
        xh1, r1, au1, hb1 = _ffn_fwd(f"ffn1_fwd_{l}", a0[0], a0[2], wa[l])
        a1 = (xh1, r1, gb_of(l, 0))
        if l == 0:
            (wb[0],) = _push_wait("gather_ici_b0_wait", b0_send, b0_recv, b0_flying, xh1, plan_b0)
            (wb[0],) = _gather_d2d("gather_d2d_b0", [wb[0]])
            l1_send, l1_recv, l1_flying, l1_token = _push_start("gather_ici_l1_start", [wa[1], wb[1]], plan_l1)
            a1 = (xh1, r1, a1[2] + l1_token[0, 0])
        z = _mix_in_fwd(f"mix_in_fwd_{l}", a1[0], a1[2], wb[l])
        mp = mix_params(l)
        ycat, st_in, ut_in, cv_in, yp_in = _mix_core_fwd(f"mix_core_fwd_{l}", z, cs, sn, *mp)
        xt_in = (cv_in, yp_in)
        xh2, r2 = _mix_out_fwd(f"mix_out_fwd_{l}", a1[0], a1[2], ycat, wb[l])
        a2 = (xh2, r2, gb_of(l, 1))
        xh3, r3, au2, hb2 = _ffn_fwd(f"ffn2_fwd_{l}", a2[0], a2[2], wb[l])
        a3 = (xh3, r3, gb_of(l, 2))
        saved.append((a0, a1, a2, a3, z, ycat, st_in, ut_in, xt_in, mp, au1, hb1, au2, hb2))
        cur = a3

    dy, loss_part = _loss_fwd_bwd(cur[0], cur[2], target)
    loss = lax.psum(loss_part[0, 0], ("x", "y", "c"))

    g_ln_g = [[None] * 3 for _ in range(DEPTH)]
    g_ln_b = [[None] * 3 for _ in range(DEPTH)]
    g_small = [dict() for _ in range(DEPTH)]
    slot = lambda j: j

    def ffn_grads(gbuf, tag, l, hb, hid, dau, dffn):
        gbuf = _dw_into(f"dw13_{tag}_{l}", gbuf, dau, hb, FF_SLOT, [(0, W13_ROWS, slot, OFF_W13)])
        return _dw_into(f"dw2_{tag}_{l}", gbuf, hid, dffn, FF_SLOT,
                        [(0, W2_ROWS, lambda j: 2 * j, OFF_W2), (W2_ROWS, W2_ROWS, lambda j: 2 * j + 1, OFF_W2)])

    rs_plan = _rs_ici_plan()
    token = jnp.zeros((), f32)
    started = {}
    for l in reversed(range(DEPTH)):
        a0, a1, a2, a3, z, ycat, st_in, ut_in, xt_in, mp, au1, hb1, au2, hb2 = saved[l]
        g_a = lax.empty((N_SHARD, A_ROWS, D), _WIRE)
        g_b = lax.empty((N_SHARD, B_ROWS, D), _WIRE)
        dh, hid, dau, dffn, dgb = _ffn_bwd(f"ffn2_bwd_{l}", dy, a3[0], a3[1], a3[2] + token, au2, wb[l])
        g_ln_g[l][2], g_ln_b[l][2] = dgb[0], dgb[1]
        g_b = ffn_grads(g_b, "ffn2", l, hb2, hid, dau, dffn)
        dh_res, dycat, dsb, dgb = _mix_out_bwd(f"mix_out_bwd_{l}", dh, a2[0], a2[1], a2[2], wb[l])
        g_ln_g[l][1], g_ln_b[l][1] = dgb[0], dgb[1]
        g_b = _dw_into(f"dw_out_{l}", g_b, ycat, dsb, D,
                       [(WOUT_ROWS * s, WOUT_ROWS, lambda j, s=s: s, OFF_WOUT) for s in range(N_SHARD)])
        dz, dwbd, dwpw, dcdw, dsm = _mix_core_bwd(f"mix_core_bwd_{l}", z, dycat, cs, sn, st_in, ut_in, *xt_in, *mp)
        pw = jnp.concatenate([dwpw.astype(_WIRE).reshape(N_SHARD, PW_ROWS, D),
                              jnp.zeros((N_SHARD, B_ROWS - OFF_PW - PW_ROWS, D), _WIRE)], axis=1)
        g_b = lax.dynamic_update_slice(g_b, pw, (0, OFF_PW, 0))
        g_small[l] = dict(
            pool_w=jnp.stack([dwbd[64 * g:64 * (g + 1), 64 * g:64 * (g + 1)] for g in range(4)]),
            pool_scale=dsm[1, :256], conv_db=dsm[2, :256], conv_ln_g=dsm[3, :256], conv_ln_b=dsm[4, :256],
            ret_gn_g=dsm[0], conv_dw=dcdw[:CONV_W])
        dh, hb = _mix_in_bwd(f"mix_in_bwd_{l}", dh_res, dz, a1[0], a1[2], wb[l])
        g_b = _dw_into(f"dw_in_{l}", g_b, dz, hb, D_IN,
                       [(WIN_ROWS * s, WIN_ROWS, lambda j, s=s: s, OFF_WIN) for s in range(N_SHARD)])
        started["b", l], token = _rs_start(f"b{l}", g_b, rs_plan)
        dh, hid, dau, dffn, dgb = _ffn_bwd(f"ffn1_bwd_{l}", dh, a1[0], a1[1], a1[2] + token, au1, wa[l])
        g_ln_g[l][0], g_ln_b[l][0] = dgb[0], dgb[1]
        g_a = ffn_grads(g_a, "ffn1", l, hb1, hid, dau, dffn)
        g_a = lax.dynamic_update_slice(g_a, jnp.zeros((N_SHARD, A_ROWS - OFF_WIN, D), _WIRE), (0, OFF_WIN, 0))
        dy = dh
        started["a", l], token = _rs_start(f"a{l}", g_a, rs_plan)
    d_raw, dgb_in = _ln_in_bwd(dy, saved[0][0][0], saved[0][0][1], gb_in + token)
    grad_x = d_raw[ROW0:][None]

    keys, mines, after = (("b", 1), ("a", 1), ("b", 0), ("a", 0)), [], d_raw
    for key in keys:
        after = _rs_end(started[key], after, rs_plan)
        mines.append(after)
    gsum = dict(zip(keys, _rs_share(mines)))
    g_big = [_unpack_shard(gsum["a", l], gsum["b", l]) for l in range(DEPTH)]
    grads = {n: jnp.stack([g_big[l][n] for l in range(DEPTH)]) for n in _BIG}

    small_parts = [
        d_raw[PAD:ROW0],
        jnp.stack([g_small[l]["conv_dw"] for l in range(DEPTH)]),
        jnp.stack([jnp.stack(g_ln_g[l]) for l in range(DEPTH)]),
        jnp.stack([jnp.stack(g_ln_b[l]) for l in range(DEPTH)]),
        dgb_in[0], dgb_in[1],
        jnp.stack([g_small[l]["pool_w"] for l in range(DEPTH)]),
        jnp.stack([g_small[l]["pool_scale"] for l in range(DEPTH)]),
        jnp.stack([g_small[l]["conv_db"] for l in range(DEPTH)]),
        jnp.stack([g_small[l]["conv_ln_g"] for l in range(DEPTH)]),
        jnp.stack([g_small[l]["conv_ln_b"] for l in range(DEPTH)]),
        jnp.stack([g_small[l]["ret_gn_g"] for l in range(DEPTH)]),
    ]
    red = _unpack_rows(_all_reduce_small(_pack_rows(small_parts)), [p.shape for p in small_parts])
    grads["meta"] = lax.dynamic_slice_in_dim(red[0], 256 * chip, 256, axis=1)
    grads["conv_dw"] = lax.dynamic_slice_in_dim(red[1], 64 * chip, 64, axis=2)
    grads["ln_g"] = lax.dynamic_slice_in_dim(red[2], 256 * chip, 256, axis=2)
    grads["ln_b"] = lax.dynamic_slice_in_dim(red[3], 256 * chip, 256, axis=2)
    for n, v in zip(("ln_in_g", "ln_in_b", "pool_w", "pool_scale", "conv_db", "conv_ln_g", "conv_ln_b", "ret_gn_g"),
                    red[4:]):
        grads[n] = v

    names = ['meta', 'ln_in_g', 'ln_in_b', 'ffn1_w13', 'ffn1_w2', 'w_in', 'pool_w', 'pool_scale', 'conv_dw',
             'conv_db', 'conv_ln_g', 'conv_ln_b', 'conv_pw', 'ret_gn_g', 'w_out', 'ffn2_w13', 'ffn2_w2', 'ln_g', 'ln_b']
    ws = dict(meta=meta, ln_in_g=ln_in_g, ln_in_b=ln_in_b, ffn1_w13=ffn1_w13, ffn1_w2=ffn1_w2, w_in=w_in,
              pool_w=pool_w, pool_scale=pool_scale, conv_dw=conv_dw, conv_db=conv_db, conv_ln_g=conv_ln_g,
              conv_ln_b=conv_ln_b, conv_pw=conv_pw, ret_gn_g=ret_gn_g, w_out=w_out, ffn2_w13=ffn2_w13,
              ffn2_w2=ffn2_w2, ln_g=ln_g, ln_b=ln_b)
    ms = dict(meta=m_meta, ln_in_g=m_ln_in_g, ln_in_b=m_ln_in_b, ffn1_w13=m_ffn1_w13, ffn1_w2=m_ffn1_w2,
              w_in=m_w_in, pool_w=m_pool_w, pool_scale=m_pool_scale, conv_dw=m_conv_dw, conv_db=m_conv_db,
              conv_ln_g=m_conv_ln_g, conv_ln_b=m_conv_ln_b, conv_pw=m_conv_pw, ret_gn_g=m_ret_gn_g,
              w_out=m_w_out, ffn2_w13=m_ffn2_w13, ffn2_w2=m_ffn2_w2, ln_g=m_ln_g, ln_b=m_ln_b)
    vs = dict(meta=v_meta, ln_in_g=v_ln_in_g, ln_in_b=v_ln_in_b, ffn1_w13=v_ffn1_w13, ffn1_w2=v_ffn1_w2,
              w_in=v_w_in, pool_w=v_pool_w, pool_scale=v_pool_scale, conv_dw=v_conv_dw, conv_db=v_conv_db,
              conv_ln_g=v_conv_ln_g, conv_ln_b=v_conv_ln_b, conv_pw=v_conv_pw, ret_gn_g=v_ret_gn_g,
              w_out=v_w_out, ffn2_w13=v_ffn2_w13, ffn2_w2=v_ffn2_w2, ln_g=v_ln_g, ln_b=v_ln_b)
    delta, new_m, new_v = {}, {}, {}
    for n in _BIG:
        shp = ws[n].shape
        two = lambda a: a.reshape(-1, shp[-1])
        d_, m_, v_ = _adamw("adamw_" + n, two(ws[n]), two(grads[n]), two(ms[n]), two(vs[n]))
        delta[n], new_m[n], new_v[n] = d_.reshape(shp), m_.reshape(shp), v_.reshape(shp)
    small_names = [n for n in names if n not in _BIG]
    pk = lambda d: _pack_rows([d[n] for n in small_names])
    d_, m_, v_ = _adamw("adamw_small", pk(ws), pk(grads), pk(ms), pk(vs))
    shapes = [ws[n].shape for n in small_names]
    for n, a, b, c in zip(small_names, _unpack_rows(d_, shapes), _unpack_rows(m_, shapes), _unpack_rows(v_, shapes)):
        delta[n], new_m[n], new_v[n] = a, b, c

    return (loss, grad_x, *[grads[n] for n in names], *[delta[n] for n in names],
            *[new_m[n] for n in names], *[new_v[n] for n in names])
```

```python
import functools
import math

import jax
import jax.numpy as jnp
from jax import lax
from jax.experimental import pallas as pl
from jax.experimental.pallas import tpu as pltpu

D = 1024
DEPTH = 2
N_META = 16
PAD = 112
ROW0 = PAD + N_META
D_POOL = 256
D_CONV = 256
D_RET = 512
HEADS = 4
DH = 128
CONV_W = 31
D_FF = 2752
FF_SLOT = 1408
D_FFP = 2 * FF_SLOT
D_IN = 2816
N_SHARD = 4
ALPHA = (2.0 * DEPTH) ** 0.25
LN_EPS = 1e-5
ROPE_BASE = 10000.0
LOG_GAMMA = tuple(math.log(1.0 - 2.0 ** (-5.0 - h)) for h in range(HEADS))
ADAM_LR, ADAM_B1, ADAM_B2, ADAM_EPS, ADAM_WD, ADAM_STEP = 0.001, 0.9, 0.999, 1e-08, 0.01, 10

_MM = jnp.bfloat16
_WIRE = jnp.bfloat16
_VMEM_LIMIT = 56 * 1024 * 1024
_FFN_ROWS = 640

MESH = pl.DeviceIdType.MESH
_ANY = pl.BlockSpec(memory_space=pl.ANY)

W13_ROWS = 1376
W2_ROWS = 688
WIN_ROWS = 704
WOUT_ROWS = 256
PW_ROWS = 16
OFF_W13 = 0
OFF_W2 = W13_ROWS
OFF_WIN = W13_ROWS + W2_ROWS
OFF_WOUT = OFF_WIN + WIN_ROWS
OFF_PW = OFF_WOUT + WOUT_ROWS
A_ROWS = 2080
B_ROWS = 3072
W2_SLOT_OFF = (0, W2_ROWS, FF_SLOT, FF_SLOT + W2_ROWS)


def _row_copies(w_ref, off, n, dst_of, sems, k0):
    return [pltpu.make_async_copy(w_ref.at[s, pl.ds(off, n)], dst_of(s), sems.at[k0 + s]) for s in range(N_SHARD)]


def _load_ffn_weights(w_ref, w13, w2, sems):
    cps = _row_copies(w_ref, OFF_W13, W13_ROWS, lambda s: w13.at[s, pl.ds(0, W13_ROWS)], sems, 0)
    cps += _row_copies(w_ref, OFF_W2, W2_ROWS, lambda s: w2.at[pl.ds(W2_SLOT_OFF[s], W2_ROWS)], sems, 4)
    for cp in cps:
        cp.start()
    zpad = jnp.zeros((FF_SLOT - W13_ROWS, D), w13.dtype)
    for s in range(N_SHARD):
        w13[s, W13_ROWS:FF_SLOT, :] = zpad
    w2[W13_ROWS:FF_SLOT, :] = zpad
    w2[FF_SLOT + W13_ROWS:D_FFP, :] = zpad
    for cp in cps:
        cp.wait()


def _load_rows(w_ref, off, n, dst, sems):
    cps = _row_copies(w_ref, off, n, lambda s: dst.at[pl.ds(s * n, n)], sems, 0)
    for cp in cps:
        cp.start()
    for cp in cps:
        cp.wait()


def _dot(a, b):
    return jnp.dot(a, b, preferred_element_type=jnp.float32)


def _dot_nt(a, b):
    return lax.dot_general(a, b, (((1,), (1,)), ((), ())), preferred_element_type=jnp.float32)


def _dot_tn(a, b):
    return lax.dot_general(a, b, (((0,), (0,)), ((), ())), preferred_element_type=jnp.float32)


def _params(sem=("arbitrary",)):
    return pltpu.CompilerParams(dimension_semantics=sem, vmem_limit_bytes=_VMEM_LIMIT)


def _row_block(t, cap=640):
    for rb in (640, 320, 128):
        if rb <= cap and t % rb == 0 and (t > 1024 or rb == 128):
            return rb
    raise ValueError(t)


def _rows(rb, n):
    return pl.BlockSpec((rb, n), lambda i: (i, 0))


def _full(shape):
    nd = len(shape)
    return pl.BlockSpec(tuple(shape), lambda i: (0,) * nd, pipeline_mode=pl.Buffered(1))


def _acc(shape):
    nd = len(shape)
    return pl.BlockSpec(tuple(shape), lambda i: (0,) * nd)


def _sigmoid(x):
    return 1.0 / (1.0 + jnp.exp(-x))


def _ln_fwd(s):
    mu = jnp.mean(s, axis=-1, keepdims=True)
    xc = s - mu
    var = jnp.mean(xc * xc, axis=-1, keepdims=True)
    rstd = lax.rsqrt(var + LN_EPS)
    return xc * rstd, rstd


def _ln_bwd(dxh, xh, rstd):
    m1 = jnp.mean(dxh, axis=-1, keepdims=True)
    m2 = jnp.mean(dxh * xh, axis=-1, keepdims=True)
    return rstd * (dxh - m1 - xh * m2)


def _frames(n):
    return pl.BlockSpec((ROW0, n), lambda i: (jnp.maximum(i - 1, 0), 0))


def _ln_in_fwd(head, x2d):
    t = x2d.shape[0] + ROW0

    def body(head_ref, x_ref, xh_ref, rstd_ref):
        raw = jnp.where(pl.program_id(0) == 0, head_ref[...], x_ref[...])
        xh, rstd = _ln_fwd(raw)
        xh_ref[...] = xh
        rstd_ref[...] = rstd

    return pl.pallas_call(
        body, name="ln_in_fwd", grid=(t // ROW0,),
        in_specs=[_full((ROW0, D)), _frames(D)],
        out_specs=[_rows(ROW0, D), _rows(ROW0, 1)],
        out_shape=[jax.ShapeDtypeStruct((t, D), jnp.float32), jax.ShapeDtypeStruct((t, 1), jnp.float32)],
        compiler_params=_params(),
    )(head, x2d)


def _ffn_fwd(name, xh, gb, wfull):
    t = xh.shape[0]
    rb = _row_block(t, _FFN_ROWS)

    def body(xh_ref, gb_ref, w_ref, out_ref, rstd_ref, au_ref, hb_ref, w13, w2, sems):
        @pl.when(pl.program_id(0) == 0)
        def _():
            _load_ffn_weights(w_ref, w13, w2, sems)

        h = xh_ref[...] * gb_ref[0:1, :] + gb_ref[1:2, :]
        hb = h.astype(_MM)
        hb_ref[...] = hb
        acc = jnp.zeros((rb, D), jnp.float32)
        for j in range(2):
            lo = j * FF_SLOT
            a = _dot_nt(hb, w13[j])
            u = _dot_nt(hb, w13[2 + j])
            au_ref[:, lo:lo + FF_SLOT] = a.astype(_MM)
            au_ref[:, D_FFP + lo:D_FFP + lo + FF_SLOT] = u.astype(_MM)
            hid = (a * _sigmoid(a) * u).astype(_MM)
            acc = acc + _dot(hid, w2[lo:lo + FF_SLOT, :])
        xo, rstd = _ln_fwd(ALPHA * h + 0.5 * acc)
        out_ref[...] = xo
        rstd_ref[...] = rstd

    return pl.pallas_call(
        body, name=name, grid=(t // rb,),
        in_specs=[_rows(rb, D), _full((2, D)), _ANY],
        out_specs=[_rows(rb, D), _rows(rb, 1), _rows(rb, 2 * D_FFP), _rows(rb, D)],
        out_shape=[jax.ShapeDtypeStruct((t, D), jnp.float32), jax.ShapeDtypeStruct((t, 1), jnp.float32),
                   jax.ShapeDtypeStruct((t, 2 * D_FFP), _MM), jax.ShapeDtypeStruct((t, D), _MM)],
        scratch_shapes=[pltpu.VMEM((N_SHARD, FF_SLOT, D), _MM), pltpu.VMEM((D_FFP, D), _MM),
                        pltpu.SemaphoreType.DMA((8,))],
        compiler_params=_params(),
    )(xh, gb, wfull)


def _ffn_bwd(name, dy, xo, rstd, gb_out, au, wfull):
    t = xo.shape[0]
    rb = _row_block(t, 320)

    def body(dy_ref, xo_ref, rstd_ref, gbo_ref, au_ref, w_ref,
             dh_ref, hid_ref, dau_ref, dffn_ref, dgb_ref, w13, w2, sems):
        i = pl.program_id(0)

        @pl.when(i == 0)
        def _():
            dgb_ref[...] = jnp.zeros_like(dgb_ref)
            _load_ffn_weights(w_ref, w13, w2, sems)

        dy = dy_ref[...]
        xo = xo_ref[...]
        dgb_ref[0:1, :] += jnp.sum(dy * xo, axis=0, keepdims=True)
        dgb_ref[1:2, :] += jnp.sum(dy, axis=0, keepdims=True)
        ds = _ln_bwd(dy * gbo_ref[0:1, :], xo, rstd_ref[...])
        dffn = (0.5 * ds).astype(_MM)
        dffn_ref[...] = dffn
        dh = ALPHA * ds
        for j in range(2):
            lo = j * FF_SLOT
            a = au_ref[:, lo:lo + FF_SLOT].astype(jnp.float32)
            u = au_ref[:, D_FFP + lo:D_FFP + lo + FF_SLOT].astype(jnp.float32)
            sg = _sigmoid(a)
            si = a * sg
            hid_ref[:, lo:lo + FF_SLOT] = (si * u).astype(_MM)
            dhid = _dot_nt(dffn, w2[lo:lo + FF_SLOT, :])
            da = (dhid * u * (sg * (1.0 + a * (1.0 - sg)))).astype(_MM)
            du = (dhid * si).astype(_MM)
            dau_ref[:, lo:lo + FF_SLOT] = da
            dau_ref[:, D_FFP + lo:D_FFP + lo + FF_SLOT] = du
            dh = dh + _dot(da, w13[j]) + _dot(du, w13[2 + j])
        dh_ref[...] = dh

    return pl.pallas_call(
        body, name=name, grid=(t // rb,),
        in_specs=[_rows(rb, D), _rows(rb, D), _rows(rb, 1), _full((2, D)), _rows(rb, 2 * D_FFP), _ANY],
        out_specs=[_rows(rb, D), _rows(rb, D_FFP), _rows(rb, 2 * D_FFP), _rows(rb, D), _acc((8, D))],
        out_shape=[jax.ShapeDtypeStruct((t, D), jnp.float32),
                   jax.ShapeDtypeStruct((t, D_FFP), _MM), jax.ShapeDtypeStruct((t, 2 * D_FFP), _MM),
                   jax.ShapeDtypeStruct((t, D), _MM), jax.ShapeDtypeStruct((8, D), jnp.float32)],
        scratch_shapes=[pltpu.VMEM((N_SHARD, FF_SLOT, D), _MM), pltpu.VMEM((D_FFP, D), _MM),
                        pltpu.SemaphoreType.DMA((8,))],
        compiler_params=_params(),
    )(dy, xo, rstd, gb_out, au, wfull)


def _mix_in_fwd(name, xh, gb, wfull):
    t = xh.shape[0]
    rb = _row_block(t)

    def body(xh_ref, gb_ref, w_ref, z_ref, wt, sems):
        @pl.when(pl.program_id(0) == 0)
        def _():
            _load_rows(w_ref, OFF_WIN, WIN_ROWS, wt, sems)

        h = xh_ref[...] * gb_ref[0:1, :] + gb_ref[1:2, :]
        z = _dot_nt(h.astype(_MM), wt[...])
        row = pl.program_id(0) * rb + lax.broadcasted_iota(jnp.int32, (rb, 1), 0)
        z_ref[...] = jnp.where(row >= PAD, z, 0.0)

    return pl.pallas_call(
        body, name=name, grid=(t // rb,),
        in_specs=[_rows(rb, D), _full((2, D)), _ANY],
        out_specs=_rows(rb, D_IN),
        out_shape=jax.ShapeDtypeStruct((t, D_IN), jnp.float32),
        scratch_shapes=[pltpu.VMEM((D_IN, D), _MM), pltpu.SemaphoreType.DMA((4,))],
        compiler_params=_params(),
    )(xh, gb, wfull)


def _mix_in_bwd(name, dh_res, dz, xh, gb, wfull):
    t = xh.shape[0]
    rb = _row_block(t)

    def body(dhr_ref, dz_ref, xh_ref, gb_ref, w_ref, dh_ref, hb_ref, wt, sems):
        @pl.when(pl.program_id(0) == 0)
        def _():
            _load_rows(w_ref, OFF_WIN, WIN_ROWS, wt, sems)

        dh_ref[...] = dhr_ref[...] + _dot(dz_ref[...], wt[...])
        hb_ref[...] = (xh_ref[...] * gb_ref[0:1, :] + gb_ref[1:2, :]).astype(_MM)

    return pl.pallas_call(
        body, name=name, grid=(t // rb,),
        in_specs=[_rows(rb, D), _rows(rb, D_IN), _rows(rb, D), _full((2, D)), _ANY],
        out_specs=[_rows(rb, D), _rows(rb, D)],
        out_shape=[jax.ShapeDtypeStruct((t, D), jnp.float32), jax.ShapeDtypeStruct((t, D), _MM)],
        scratch_shapes=[pltpu.VMEM((D_IN, D), _MM), pltpu.SemaphoreType.DMA((4,))],
        compiler_params=_params(),
    )(dh_res, dz, xh, gb, wfull)


def _mix_out_fwd(name, xh, gb, ycat, wfull):
    t = xh.shape[0]
    rb = _row_block(t)

    def body(xh_ref, gb_ref, y_ref, w_ref, out_ref, rstd_ref, wo, sems):
        @pl.when(pl.program_id(0) == 0)
        def _():
            _load_rows(w_ref, OFF_WOUT, WOUT_ROWS, wo, sems)

        h = xh_ref[...] * gb_ref[0:1, :] + gb_ref[1:2, :]
        xo, rstd = _ln_fwd(ALPHA * h + _dot(y_ref[...], wo[...]))
        out_ref[...] = xo
        rstd_ref[...] = rstd

    return pl.pallas_call(
        body, name=name, grid=(t // rb,),
        in_specs=[_rows(rb, D), _full((2, D)), _rows(rb, D), _ANY],
        out_specs=[_rows(rb, D), _rows(rb, 1)],
        out_shape=[jax.ShapeDtypeStruct((t, D), jnp.float32), jax.ShapeDtypeStruct((t, 1), jnp.float32)],
        scratch_shapes=[pltpu.VMEM((D, D), _MM), pltpu.SemaphoreType.DMA((4,))],
        compiler_params=_params(),
    )(xh, gb, ycat, wfull)


def _mix_out_bwd(name, dy, xo, rstd, gb_out, wfull):
    t = xo.shape[0]
    rb = _row_block(t)

    def body(dy_ref, xo_ref, rstd_ref, gbo_ref, w_ref, dhr_ref, dyc_ref, dsb_ref, dgb_ref, wo, sems):
        @pl.when(pl.program_id(0) == 0)
        def _():
            dgb_ref[...] = jnp.zeros_like(dgb_ref)
            _load_rows(w_ref, OFF_WOUT, WOUT_ROWS, wo, sems)

        dy = dy_ref[...]
        xo = xo_ref[...]
        dgb_ref[0:1, :] += jnp.sum(dy * xo, axis=0, keepdims=True)
        dgb_ref[1:2, :] += jnp.sum(dy, axis=0, keepdims=True)
        ds = _ln_bwd(dy * gbo_ref[0:1, :], xo, rstd_ref[...])
        dsb = ds.astype(_MM)
        dsb_ref[...] = dsb
        dhr_ref[...] = ALPHA * ds
        dyc_ref[...] = _dot_nt(dsb, wo[...])

    return pl.pallas_call(
        body, name=name, grid=(t // rb,),
        in_specs=[_rows(rb, D), _rows(rb, D), _rows(rb, 1), _full((2, D)), _ANY],
        out_specs=[_rows(rb, D), _rows(rb, D), _rows(rb, D), _acc((8, D))],
        out_shape=[jax.ShapeDtypeStruct((t, D), jnp.float32), jax.ShapeDtypeStruct((t, D), jnp.float32),
                   jax.ShapeDtypeStruct((t, D), _MM), jax.ShapeDtypeStruct((8, D), jnp.float32)],
        scratch_shapes=[pltpu.VMEM((D, D), _MM), pltpu.SemaphoreType.DMA((4,))],
        compiler_params=_params(),
    )(dy, xo, rstd, gb_out, wfull)


def _loss_fwd_bwd(xh, gb, target):
    t = xh.shape[0]

    def body(xh_ref, gb_ref, tg_ref, dy_ref, loss_ref):
        @pl.when(pl.program_id(0) == 0)
        def _():
            loss_ref[...] = jnp.zeros_like(loss_ref)

        y = xh_ref[...] * gb_ref[0:1, :] + gb_ref[1:2, :]
        err = jnp.where(pl.program_id(0) > 0, y - tg_ref[...], 0.0)
        dy_ref[...] = err * (1.0 / D)
        per_row = jnp.mean(err * err, axis=-1, keepdims=True)
        loss_ref[...] += 0.5 * jnp.sum(per_row, axis=0, keepdims=True)

    return pl.pallas_call(
        body, name="loss", grid=(t // ROW0,),
        in_specs=[_rows(ROW0, D), _full((2, D)), _frames(D)],
        out_specs=[_rows(ROW0, D), _acc((1, 1))],
        out_shape=[jax.ShapeDtypeStruct((t, D), jnp.float32), jax.ShapeDtypeStruct((1, 1), jnp.float32)],
        compiler_params=_params(),
    )(xh, gb, target)


def _ln_in_bwd(dy, xh, rstd, gb):
    t = xh.shape[0]

    def body(dy_ref, xh_ref, rstd_ref, gb_ref, dx_ref, dhead_ref, dgb_ref):
        i = pl.program_id(0)

        @pl.when(i == 0)
        def _():
            dgb_ref[...] = jnp.zeros_like(dgb_ref)

        dy = dy_ref[...]
        xh = xh_ref[...]
        dgb_ref[0:1, :] += jnp.sum(dy * xh, axis=0, keepdims=True)
        dgb_ref[1:2, :] += jnp.sum(dy, axis=0, keepdims=True)
        dx = _ln_bwd(dy * gb_ref[0:1, :], xh, rstd_ref[...])

        @pl.when(i == 0)
        def _():
            dhead_ref[...] = dx

        @pl.when(i > 0)
        def _():
            dx_ref[...] = dx

    return pl.pallas_call(
        body, name="ln_in_bwd", grid=(t // ROW0,),
        in_specs=[_rows(ROW0, D), _rows(ROW0, D), _rows(ROW0, 1), _full((2, D))],
        out_specs=[_frames(D), _acc((ROW0, D)), _acc((8, D))],
        out_shape=[jax.ShapeDtypeStruct((t - ROW0, D), jnp.float32), jax.ShapeDtypeStruct((ROW0, D), jnp.float32),
                   jax.ShapeDtypeStruct((8, D), jnp.float32)],
        compiler_params=_params(),
    )(dy, xh, rstd, gb)


def _dw_rows(t, cols):
    for tt in (2080, 1664, 640, 128):
        vmem = 2 * tt * (cols + D) * 2 + cols * D * 6
        if t % tt == 0 and (t > 1024 or tt == 128) and vmem <= 44 * 1024 * 1024:
            return tt
    raise ValueError((t, cols))


def _dw_into(name, gpack, x, y, cols, pieces):
    t, k = x.shape
    tt = _dw_rows(t, cols)
    nt = t // tt

    def body(x_ref, y_ref, g_in, g_out, acc, stage, sems):
        j = pl.program_id(0)
        s = pl.program_id(1)

        @pl.when(s == 0)
        def _():
            acc[...] = jnp.zeros_like(acc)

        acc[...] += _dot_tn(x_ref[...], y_ref[...])

        @pl.when(s == nt - 1)
        def _():
            stage[...] = acc[...].astype(stage.dtype)
            cps = []
            for q, (lo, n, chip_of, off) in enumerate(pieces):
                cp = pltpu.make_async_copy(stage.at[pl.ds(lo, n)], g_out.at[chip_of(j), pl.ds(off, n)],
                                           sems.at[q])
                cp.start()
                cps.append(cp)
            for cp in cps:
                cp.wait()

    return pl.pallas_call(
        body, name=name, grid=(k // cols, nt),
        in_specs=[pl.BlockSpec((tt, cols), lambda j, s: (s, j)), pl.BlockSpec((tt, D), lambda j, s: (s, 0)), _ANY],
        out_specs=_ANY,
        out_shape=jax.ShapeDtypeStruct(gpack.shape, gpack.dtype),
        input_output_aliases={2: 0},
        scratch_shapes=[pltpu.VMEM((cols, D), jnp.float32), pltpu.VMEM((cols, D), gpack.dtype),
                        pltpu.SemaphoreType.DMA((len(pieces),))],
        compiler_params=_params(("arbitrary", "arbitrary")),
    )(x, y, gpack)


_TAIL_U = 32
_TAIL_X = 32
_MIX_ROWS = 320


def _decay_mask(rb, h):
    ii = lax.broadcasted_iota(jnp.int32, (rb, rb), 0)
    jj = lax.broadcasted_iota(jnp.int32, (rb, rb), 1)
    dist = jnp.abs(ii - jj).astype(jnp.float32)
    vis = (jj >> 6) <= (ii >> 6)
    return jnp.where(vis, jnp.exp(LOG_GAMMA[h] * dist), 0.0)


def _row_decays(rb, h):
    r = lax.broadcasted_iota(jnp.int32, (rb, DH), 0).astype(jnp.float32)
    return jnp.exp(LOG_GAMMA[h] * (r + 1.0)), jnp.exp(LOG_GAMMA[h] * (rb - 1.0 - r))


def _rope(x, cs, sn):
    return x * cs + pltpu.roll(x, DH // 2, 1) * sn


def _rope_t(dx, cs, sn):
    return dx * cs + pltpu.roll(dx * sn, DH // 2, 1)


def _pool_count(blk, rb):
    row = blk * rb + lax.broadcasted_iota(jnp.int32, (rb, D_POOL), 0) - PAD
    lane = lax.broadcasted_iota(jnp.int32, (rb, D_POOL), 1)
    win = jnp.left_shift(2, lane >> 6)
    return jnp.clip(row + 1, 1, win).astype(jnp.float32)


def _pool_select(p2, p4, p8, p16):
    lane = lax.broadcasted_iota(jnp.int32, p2.shape, 1)
    return jnp.where(lane < 64, p2, jnp.where(lane < 128, p4, jnp.where(lane < 192, p8, p16)))


def _trailing_windows(ext, p2, p4, p8, rb):
    n = _TAIL_X + rb
    p2[8:n, :] = ext[8:n, :] + ext[pl.ds(7, n - 8), :]
    p4[16:n, :] = p2[16:n, :] + p2[pl.ds(14, n - 16), :]
    p8[24:n, :] = p4[24:n, :] + p4[pl.ds(20, n - 24), :]
    lo = _TAIL_X
    p16 = p8[lo:n, :] + p8[lo - 8:n - 8, :]
    return _pool_select(p2[lo:n, :], p4[lo:n, :], p8[lo:n, :], p16)


def _leading_windows(ext, p2, p4, p8, rb):
    p2[0:rb + 24, :] = ext[0:rb + 24, :] + ext[pl.ds(1, rb + 24), :]
    p4[0:rb + 16, :] = p2[0:rb + 16, :] + p2[pl.ds(2, rb + 16), :]
    p8[0:rb + 8, :] = p4[0:rb + 8, :] + p4[pl.ds(4, rb + 8), :]
    p16 = p8[0:rb, :] + p8[8:rb + 8, :]
    return _pool_select(p2[0:rb, :], p4[0:rb, :], p8[0:rb, :], p16)


def _shifted_copies(ext, copies, first, n, sign):
    for b in range(1, 8):
        copies[b - 1, first:first + n, :] = ext[pl.ds(first - sign * b, n), :]


def _tap(ext, copies, k, start, rows, sign):
    a, b = divmod(k, 8)
    src = ext if b == 0 else copies.at[b - 1]
    return src[pl.ds(start - sign * 8 * a, rows), :]


def _sub_rows(rb):
    return 128 if rb % 128 == 0 else 64


def _mix_core_fwd(name, z, cs, sn, wbd, pscale, cdw, cvec, wpw, gn):
    t = z.shape[0]
    rb = _row_block(t, _MIX_ROWS)
    nblk = t // rb
    sr = _sub_rows(rb)

    def body(z_ref, cs_ref, sn_ref, wbd_ref, ps_ref, cdw_ref, cvec_ref, wpw_ref, gn_ref,
             y_ref, st_ref, ut_ref, cv_ref, yp_ref,
             uext, xext, cv, p2, p4, p8, ucopies, state, wmask):
        i = pl.program_id(0)

        @pl.when(i == 0)
        def _():
            state[...] = jnp.zeros_like(state)
            uext[0:_TAIL_U, :] = jnp.zeros((_TAIL_U, D_CONV), jnp.float32)
            xext[0:_TAIL_X, :] = jnp.zeros((_TAIL_X, D_POOL), jnp.float32)
            for h in range(HEADS):
                wmask[h] = _decay_mask(rb, h)

        st_ref[0] = state[...]
        ut_ref[0] = uext[0:_TAIL_U, :]

        xp = z_ref[:, 0:256]
        uext[_TAIL_U:_TAIL_U + rb, :] = z_ref[:, 256:512] * _sigmoid(z_ref[:, 512:768])
        xext[_TAIL_X:_TAIL_X + rb, :] = xp

        _shifted_copies(uext, ucopies, 8, rb + _TAIL_U - 8, 1)
        for r in range(0, rb, sr):
            acc = jnp.zeros((sr, D_CONV), jnp.float32)
            for k in range(CONV_W):
                acc = acc + _tap(uext, ucopies, k, _TAIL_U + r, sr, 1) * cdw_ref[CONV_W - 1 - k:CONV_W - k, :]
            cv[r:r + sr, :] = acc

        win = _trailing_windows(xext, p2, p4, p8, rb)
        ypb = (win / _pool_count(i, rb) - xp).astype(_MM)
        yp_ref[...] = ypb
        y_ref[:, 0:256] = (_dot(ypb, wbd_ref[...]) * ps_ref[...]).astype(_MM)
        cv_ref[...] = cv[...]
        cn, _ = _ln_fwd(cv[...] + cvec_ref[0:1, :])
        ln = cn * cvec_ref[1:2, :] + cvec_ref[2:3, :]
        sw = ln * _sigmoid(ln)
        y_ref[:, 256:512] = _dot(sw.astype(_MM), wpw_ref[...]).astype(_MM)
        csv = cs_ref[...]
        snv = sn_ref[...]
        for h in range(HEADS):
            q = _rope(z_ref[:, 768 + h * DH:768 + (h + 1) * DH], csv, snv)
            k = _rope(z_ref[:, 1280 + h * DH:1280 + (h + 1) * DH], csv, snv) * (DH ** -0.5)
            vb = z_ref[:, 1792 + h * DH:1792 + (h + 1) * DH].astype(_MM)
            g = z_ref[:, 2304 + h * DH:2304 + (h + 1) * DH]
            a, b = _row_decays(rb, h)
            s = _dot_nt(q.astype(_MM), k.astype(_MM)) * wmask[h]
            o = _dot(s.astype(_MM), vb) + _dot((q * a).astype(_MM), state[h].astype(_MM))
            state[h] = math.exp(LOG_GAMMA[h] * rb) * state[h] + _dot_tn((k * b).astype(_MM), vb)
            on, _ = _ln_fwd(o)
            y_ref[:, 512 + h * DH:512 + (h + 1) * DH] = (
                g * _sigmoid(g) * on * gn_ref[:, h * DH:(h + 1) * DH]).astype(_MM)

        uext[0:_TAIL_U, :] = uext[rb:rb + _TAIL_U, :]
        xext[0:_TAIL_X, :] = xext[rb:rb + _TAIL_X, :]

    return pl.pallas_call(
        body, name=name, grid=(nblk,),
        in_specs=[_rows(rb, D_IN), _rows(rb, DH), _rows(rb, DH), _full((256, 256)), _full((1, 256)),
                  _full((32, 256)), _full((8, 256)), _full((256, 256)), _full((1, D_RET))],
        out_specs=[_rows(rb, D),
                   pl.BlockSpec((1, HEADS, DH, DH), lambda i: (i, 0, 0, 0)),
                   pl.BlockSpec((1, _TAIL_U, D_CONV), lambda i: (i, 0, 0)),
                   _rows(rb, D_CONV), _rows(rb, D_POOL)],
        out_shape=[jax.ShapeDtypeStruct((t, D), _MM),
                   jax.ShapeDtypeStruct((nblk, HEADS, DH, DH), jnp.float32),
                   jax.ShapeDtypeStruct((nblk, _TAIL_U, D_CONV), jnp.float32),
                   jax.ShapeDtypeStruct((t, D_CONV), jnp.float32),
                   jax.ShapeDtypeStruct((t, D_POOL), _MM)],
        scratch_shapes=[pltpu.VMEM((rb + _TAIL_U, D_CONV), jnp.float32),
                        pltpu.VMEM((rb + _TAIL_X, D_POOL), jnp.float32),
                        pltpu.VMEM((rb, D_CONV), jnp.float32),
                        pltpu.VMEM((rb + _TAIL_X, D_POOL), jnp.float32),
                        pltpu.VMEM((rb + _TAIL_X, D_POOL), jnp.float32),
                        pltpu.VMEM((rb + _TAIL_X, D_POOL), jnp.float32),
                        pltpu.VMEM((7, rb + _TAIL_U, D_CONV), jnp.float32),
                        pltpu.VMEM((HEADS, DH, DH), jnp.float32),
                        pltpu.VMEM((HEADS, rb, rb), jnp.float32)],
        compiler_params=_params(),
    )(z, cs, sn, wbd, pscale, cdw, cvec, wpw, gn)


def _mix_core_bwd(name, z, dyc, cs, sn, st_in, ut_in, cv_in, yp_in, wbd, pscale, cdw, cvec, wpw, gn):
    t = z.shape[0]
    rb = _row_block(t, _MIX_ROWS)
    nblk = t // rb
    sr = _sub_rows(rb)
    rev = lambda i: nblk - 1 - i

    def body(z_ref, dy_ref, cs_ref, sn_ref, st_ref, ut_ref, cv_ref, yp_ref,
             wbd_ref, ps_ref, cdw_ref, cvec_ref, wpw_ref, gn_ref,
             dz_ref, dwbd_ref, dwpw_ref, dcdw_ref, dsm_ref,
             uext, cv, dcvext, eext, p2, p4, p8, ucopies, dcopies, dstate, wmask):
        i = pl.program_id(0)
        blk = nblk - 1 - i

        @pl.when(i == 0)
        def _():
            dstate[...] = jnp.zeros_like(dstate)
            dcvext[rb:rb + _TAIL_U, :] = jnp.zeros((_TAIL_U, D_CONV), jnp.float32)
            eext[rb:rb + _TAIL_X, :] = jnp.zeros((_TAIL_X, D_POOL), jnp.float32)
            dwbd_ref[...] = jnp.zeros_like(dwbd_ref)
            dwpw_ref[...] = jnp.zeros_like(dwpw_ref)
            dcdw_ref[...] = jnp.zeros_like(dcdw_ref)
            dsm_ref[...] = jnp.zeros_like(dsm_ref)
            for h in range(HEADS):
                wmask[h] = _decay_mask(rb, h)

        row = blk * rb + lax.broadcasted_iota(jnp.int32, (rb, 1), 0)
        live = row >= PAD

        ca = z_ref[:, 256:512]
        sg_c = _sigmoid(z_ref[:, 512:768])
        uext[0:_TAIL_U, :] = ut_ref[0]
        uext[_TAIL_U:_TAIL_U + rb, :] = ca * sg_c

        cnt = _pool_count(blk, rb)
        ypb = yp_ref[...]
        dyp = dy_ref[:, 0:256]
        pm = _dot(ypb, wbd_ref[...])
        dsm_ref[1:2, 0:256] += jnp.sum(dyp * pm, axis=0, keepdims=True)
        dpm = (dyp * ps_ref[...]).astype(_MM)
        dwbd_ref[...] += _dot_tn(ypb, dpm)
        dypre = _dot_nt(dpm, wbd_ref[...])
        eext[0:rb, :] = dypre / cnt
        win = _leading_windows(eext, p2, p4, p8, rb)
        dz_ref[:, 0:256] = jnp.where(live, win - dypre, 0.0).astype(_MM)

        cn, rstd_c = _ln_fwd(cv_ref[...] + cvec_ref[0:1, :])
        ln = cn * cvec_ref[1:2, :] + cvec_ref[2:3, :]
        sg_l = _sigmoid(ln)
        swb = (ln * sg_l).astype(_MM)
        dycb = dy_ref[:, 256:512].astype(_MM)
        dwpw_ref[...] += _dot_tn(swb, dycb)
        dln = _dot_nt(dycb, wpw_ref[...]) * (sg_l * (1.0 + ln * (1.0 - sg_l)))
        dsm_ref[3:4, 0:256] += jnp.sum(dln * cn, axis=0, keepdims=True)
        dsm_ref[4:5, 0:256] += jnp.sum(dln, axis=0, keepdims=True)
        dcv = _ln_bwd(dln * cvec_ref[1:2, :], cn, rstd_c)
        dsm_ref[2:3, 0:256] += jnp.sum(dcv, axis=0, keepdims=True)
        dcvext[0:rb, :] = dcv
        _shifted_copies(uext, ucopies, 8, rb + _TAIL_U - 8, 1)
        _shifted_copies(dcvext, dcopies, 0, rb + _TAIL_U - 8, -1)
        for k in range(CONV_W):
            prod = dcv * _tap(uext, ucopies, k, _TAIL_U, rb, 1)
            dcdw_ref[CONV_W - 1 - k:CONV_W - k, :] += jnp.sum(prod, axis=0, keepdims=True)
        for r in range(0, rb, sr):
            acc = jnp.zeros((sr, D_CONV), jnp.float32)
            for k in range(CONV_W):
                acc = acc + _tap(dcvext, dcopies, k, r, sr, -1) * cdw_ref[CONV_W - 1 - k:CONV_W - k, :]
            cv[r:r + sr, :] = acc
        du = cv[...]
        dz_ref[:, 256:512] = jnp.where(live, du * sg_c, 0.0).astype(_MM)
        dz_ref[:, 512:768] = jnp.where(live, du * ca * sg_c * (1.0 - sg_c), 0.0).astype(_MM)

        csv = cs_ref[...]
        snv = sn_ref[...]
        for h in range(HEADS):
            q = _rope(z_ref[:, 768 + h * DH:768 + (h + 1) * DH], csv, snv)
            k = _rope(z_ref[:, 1280 + h * DH:1280 + (h + 1) * DH], csv, snv) * (DH ** -0.5)
            vb = z_ref[:, 1792 + h * DH:1792 + (h + 1) * DH].astype(_MM)
            g = z_ref[:, 2304 + h * DH:2304 + (h + 1) * DH]
            a, b = _row_decays(rb, h)
            qb = q.astype(_MM)
            kb = k.astype(_MM)
            qab = (q * a).astype(_MM)
            kbb = (k * b).astype(_MM)
            stb = st_ref[0, h].astype(_MM)
            sb = (_dot_nt(qb, kb) * wmask[h]).astype(_MM)
            o = _dot(sb, vb) + _dot(qab, stb)
            on, rstd_o = _ln_fwd(o)
            gnv = gn_ref[:, h * DH:(h + 1) * DH]
            sg_g = _sigmoid(g)
            si_g = g * sg_g
            dyr = dy_ref[:, 512 + h * DH:512 + (h + 1) * DH]
            dsm_ref[0:1, h * DH:(h + 1) * DH] += jnp.sum(dyr * on * si_g, axis=0, keepdims=True)
            dgate = dyr * on * gnv * (sg_g * (1.0 + g * (1.0 - sg_g)))
            dob = _ln_bwd(dyr * gnv * si_g, on, rstd_o).astype(_MM)
            dstb = dstate[h].astype(_MM)
            dsb = (_dot_nt(dob, vb) * wmask[h]).astype(_MM)
            dq = _dot(dsb, kb) + _dot_nt(dob, stb) * a
            dk = _dot_tn(dsb, qb) + _dot_nt(vb, dstb) * b
            dv = _dot_tn(sb, dob) + _dot(kbb, dstb)
            dstate[h] = math.exp(LOG_GAMMA[h] * rb) * dstate[h] + _dot_tn(qab, dob)
            dz_ref[:, 768 + h * DH:768 + (h + 1) * DH] = jnp.where(live, _rope_t(dq, csv, snv), 0.0).astype(_MM)
            dz_ref[:, 1280 + h * DH:1280 + (h + 1) * DH] = jnp.where(
                live, _rope_t(dk * (DH ** -0.5), csv, snv), 0.0).astype(_MM)
            dz_ref[:, 1792 + h * DH:1792 + (h + 1) * DH] = jnp.where(live, dv, 0.0).astype(_MM)
            dz_ref[:, 2304 + h * DH:2304 + (h + 1) * DH] = jnp.where(live, dgate, 0.0).astype(_MM)

        dcvext[rb:rb + _TAIL_U, :] = dcvext[0:_TAIL_U, :]
        eext[rb:rb + _TAIL_X, :] = eext[0:_TAIL_X, :]

    rrows = lambda n: pl.BlockSpec((rb, n), lambda i: (rev(i), 0))
    return pl.pallas_call(
        body, name=name, grid=(nblk,),
        in_specs=[rrows(D_IN), rrows(D), rrows(DH), rrows(DH),
                  pl.BlockSpec((1, HEADS, DH, DH), lambda i: (rev(i), 0, 0, 0)),
                  pl.BlockSpec((1, _TAIL_U, D_CONV), lambda i: (rev(i), 0, 0)),
                  rrows(D_CONV), rrows(D_POOL),
                  _full((256, 256)), _full((1, 256)), _full((32, 256)), _full((8, 256)), _full((256, 256)),
                  _full((1, D_RET))],
        out_specs=[rrows(D_IN), _acc((256, 256)), _acc((256, 256)), _acc((32, 256)), _acc((8, 512))],
        out_shape=[jax.ShapeDtypeStruct((t, D_IN), _MM),
                   jax.ShapeDtypeStruct((256, 256), jnp.float32), jax.ShapeDtypeStruct((256, 256), jnp.float32),
                   jax.ShapeDtypeStruct((32, 256), jnp.float32), jax.ShapeDtypeStruct((8, 512), jnp.float32)],
        scratch_shapes=[pltpu.VMEM((rb + _TAIL_U, D_CONV), jnp.float32),
                        pltpu.VMEM((rb, D_CONV), jnp.float32),
                        pltpu.VMEM((rb + _TAIL_U, D_CONV), jnp.float32),
                        pltpu.VMEM((rb + _TAIL_X, D_POOL), jnp.float32),
                        pltpu.VMEM((rb + _TAIL_X, D_POOL), jnp.float32),
                        pltpu.VMEM((rb + _TAIL_X, D_POOL), jnp.float32),
                        pltpu.VMEM((rb + _TAIL_X, D_POOL), jnp.float32),
                        pltpu.VMEM((7, rb + _TAIL_U, D_CONV), jnp.float32),
                        pltpu.VMEM((7, rb + _TAIL_U, D_CONV), jnp.float32),
                        pltpu.VMEM((HEADS, DH, DH), jnp.float32),
                        pltpu.VMEM((HEADS, rb, rb), jnp.float32)],
        compiler_params=_params(),
    )(z, dyc, cs, sn, st_in, ut_in, cv_in, yp_in, wbd, pscale, cdw, cvec, wpw, gn)


def _me():
    return lax.axis_index("x"), lax.axis_index("y"), lax.axis_index("c")


def _flip(me, mask):
    return tuple(1 - m if f else m for m, f in zip(me, mask))


def _push(name, aliased, inputs, fresh, remote):
    n_al, n_in, n_out, n_rem = len(aliased), len(inputs), len(fresh), len(remote)

    def body(*refs):
        ins = refs[n_al:n_al + n_in]
        al = refs[n_al + n_in:2 * n_al + n_in]
        outs = refs[2 * n_al + n_in:2 * n_al + n_in + n_out]
        send_sems, recv_sems = refs[2 * n_al + n_in + n_out:]
        me = _me()
        copies = []
        for k, (mask, src_fn, dst_fn) in enumerate(remote):
            cp = pltpu.make_async_remote_copy(
                src_ref=src_fn(al, ins, outs, me), dst_ref=dst_fn(al, ins, outs, me),
                send_sem=send_sems.at[k], recv_sem=recv_sems.at[k],
                device_id=_flip(me, mask), device_id_type=MESH)
            cp.start()
            copies.append(cp)
        for cp in copies:
            cp.wait()

    return pl.pallas_call(
        body, name=name,
        in_specs=[_ANY] * (n_al + n_in), out_specs=[_ANY] * (n_al + n_out),
        out_shape=[jax.ShapeDtypeStruct(a.shape, a.dtype) for a in aliased] + list(fresh),
        input_output_aliases={i: i for i in range(n_al)},
        scratch_shapes=[pltpu.SemaphoreType.DMA((n_rem,)), pltpu.SemaphoreType.DMA((n_rem,))],
    )(*aliased, *inputs)


_HBM = pl.BlockSpec(memory_space=pltpu.HBM)
_SEM = pl.BlockSpec(memory_space=pltpu.SEMAPHORE)
_EFFECT = pltpu.SideEffectType.DATAFLOW_SIDE_EFFECTING


def _push_start(name, bufs, remote):
    n, n_rem = len(bufs), len(remote)

    def body(*refs):
        ins = refs[:n]
        send_sems, recv_sems = refs[n], refs[n + 1]
        token = refs[2 * n + 2]
        me = _me()
        for k, (mask, src_fn, dst_fn) in enumerate(remote):
            pltpu.make_async_remote_copy(
                src_ref=src_fn(ins, me), dst_ref=dst_fn(ins, me),
                send_sem=send_sems.at[k], recv_sem=recv_sems.at[k],
                device_id=_flip(me, mask), device_id_type=MESH).start()
        token[...] = jnp.zeros_like(token)

    out = pl.pallas_call(
        body, name=name,
        out_shape=(pltpu.SemaphoreType.DMA((n_rem,)), pltpu.SemaphoreType.DMA((n_rem,)),
                   *[pltpu.HBM(b.shape, b.dtype) for b in bufs], jax.ShapeDtypeStruct((8, 128), jnp.float32)),
        in_specs=[_HBM] * n,
        out_specs=(_SEM, _SEM, *[_HBM] * n, pl.BlockSpec(memory_space=pltpu.VMEM)),
        input_output_aliases={i: i + 2 for i in range(n)},
        compiler_params=pltpu.CompilerParams(has_side_effects=_EFFECT),
    )(*[pltpu.with_memory_space_constraint(b, pltpu.HBM) for b in bufs])
    return out[0], out[1], list(out[2:2 + n]), out[2 + n]


def _push_wait(name, send_sems, recv_sems, bufs, after, remote):
    n = len(bufs)

    def body(*refs):
        ins = refs[:n]
        s_sems, r_sems = refs[n], refs[n + 1]
        me = _me()
        for k, (mask, src_fn, dst_fn) in enumerate(remote):
            cp = pltpu.make_async_remote_copy(
                src_ref=src_fn(ins, me), dst_ref=dst_fn(ins, me),
                send_sem=s_sems.at[k], recv_sem=r_sems.at[k],
                device_id=_flip(me, mask), device_id_type=MESH)
            cp.wait_send()
            cp.wait_recv()

    out = pl.pallas_call(
        body, name=name,
        out_shape=tuple(pltpu.HBM(b.shape, b.dtype) for b in bufs),
        in_specs=[_HBM] * n + [_SEM, _SEM, _ANY], out_specs=tuple([_HBM] * n),
        input_output_aliases={i: i for i in range(n)},
        compiler_params=pltpu.CompilerParams(has_side_effects=_EFFECT),
    )(*bufs, send_sems, recv_sems, after)
    return list(out)


_ICI_MASKS = ((0, 1, 0), (1, 0, 0), (1, 1, 0))
_D2D_MASK = (0, 0, 1)
_ALL_MASKS = tuple((a, b, c) for a in (0, 1) for b in (0, 1) for c in (0, 1))[1:]


def _chip(me):
    return 2 * me[0] + me[1]


def _half(me, rows):
    return pl.ds(me[2] * (rows // 2), rows // 2)


def _other_half(me, rows):
    return pl.ds((1 - me[2]) * (rows // 2), rows // 2)


def _sum_block(rh):
    return next((b for b in (768, 640, 512, 128) if rh % b == 0), rh)


def _own_slot(mine):
    chip = _chip(_me())
    return lax.dynamic_update_slice(lax.empty((N_SHARD,) + mine.shape, mine.dtype), mine[None],
                                    (chip,) + (0,) * mine.ndim)


def _gather_ici_plan(rows, with_small):
    remote = []
    for mask in _ICI_MASKS:
        for b, r in enumerate(rows):
            mine = lambda bufs, me, b=b, r=r: bufs[b].at[_chip(me), _half(me, r)]
            remote.append((mask, mine, mine))
        if with_small:
            mine_small = lambda bufs, me: bufs[len(rows)].at[_chip(me)]
            remote.append((mask, mine_small, mine_small))
    return remote


def _gather_d2d(name, ws):
    remote = []
    for b, w in enumerate(ws):
        for j in range(1, N_SHARD):
            theirs = lambda al, ins, outs, me, j=j, b=b, r=w.shape[1]: al[b].at[(_chip(me) + j) % N_SHARD, _half(me, r)]
            remote.append((_D2D_MASK, theirs, theirs))
    return _push(name, list(ws), [], [], remote)


def _sum_pair(name, g, recv):
    _, r, _ = g.shape
    rb = _sum_block(r // 2)
    nb = r // 2 // rb
    c = lax.axis_index("c").astype(jnp.int32).reshape(1)

    def body(c_ref, g_ref, r_ref, o_ref):
        o_ref[...] = (g_ref[...].astype(jnp.float32) + r_ref[...].astype(jnp.float32)).astype(o_ref.dtype)

    return pl.pallas_call(
        body, name=name,
        grid_spec=pltpu.PrefetchScalarGridSpec(
            num_scalar_prefetch=1, grid=(N_SHARD, nb),
            in_specs=[pl.BlockSpec((None, rb, D), lambda s, i, c_ref: (s, c_ref[0] * nb + i, 0)),
                      pl.BlockSpec((None, rb, D), lambda s, i, c_ref: (s, i, 0))],
            out_specs=pl.BlockSpec((None, rb, D), lambda s, i, c_ref: (s, i, 0))),
        out_shape=jax.ShapeDtypeStruct((N_SHARD, r // 2, D), g.dtype),
        compiler_params=_params(("arbitrary", "arbitrary")),
    )(c, g, recv)


def _sum_chips(name, p, recv):
    _, rh, _ = p.shape
    rb = _sum_block(rh)
    nb = rh // rb
    s = jnp.stack([2 * lax.axis_index("x") + lax.axis_index("y"), lax.axis_index("c")]).astype(jnp.int32)

    def body(s_ref, p_ref, r_ref, o_ref):
        acc = p_ref[...].astype(jnp.float32)
        for j in range(3):
            acc = acc + r_ref[j].astype(jnp.float32)
        o_ref[...] = acc

    return pl.pallas_call(
        body, name=name,
        grid_spec=pltpu.PrefetchScalarGridSpec(
            num_scalar_prefetch=1, grid=(nb,),
            in_specs=[pl.BlockSpec((None, rb, D), lambda i, s_ref: (s_ref[0], i, 0)),
                      pl.BlockSpec((3, rb, D), lambda i, s_ref: (0, i, 0))],
            out_specs=pl.BlockSpec((rb, D), lambda i, s_ref: (s_ref[1] * nb + i, 0))),
        out_shape=jax.ShapeDtypeStruct((2 * rh, D), jnp.float32),
        compiler_params=_params(),
    )(s, p, recv)


def _rs_ici_plan():
    remote = []
    for j, mask in enumerate(_ICI_MASKS):
        remote.append((mask,
                       lambda bufs, me, mask=mask: bufs[0].at[_chip(_flip(me, mask))],
                       lambda bufs, me, j=j: bufs[1].at[j]))
    return remote


def _rs_pair(tag, g):
    _, r, _ = g.shape
    remote = [(_D2D_MASK,
               lambda al, ins, outs, me, s=s: ins[0].at[s, _other_half(me, r)],
               lambda al, ins, outs, me, s=s: outs[0].at[s]) for s in range(N_SHARD)]
    (recv,) = _push("rs_d2d_" + tag, [], [g], [jax.ShapeDtypeStruct((N_SHARD, r // 2, D), g.dtype)], remote)
    return _sum_pair("rs_sum_pair_" + tag, g, recv)


def _rs_start(tag, g, rs_plan):
    p = _rs_pair(tag, g)
    landing = lax.empty((3, p.shape[1], D), p.dtype)
    send, recv, flying, token = _push_start("rs_ici_" + tag + "_start", [p, landing], rs_plan)
    return (tag, send, recv, flying), token[0, 0]


def _rs_end(started, after, rs_plan):
    tag, send, recv, flying = started
    p, recv3 = _push_wait("rs_ici_" + tag + "_wait", send, recv, flying, after, rs_plan)
    return _sum_chips("rs_sum_chips_" + tag, p, recv3)


def _rs_share(mines):
    remote = []
    for b, m in enumerate(mines):
        half = lambda al, ins, outs, me, b=b, r=m.shape[0]: al[b].at[_half(me, r)]
        remote.append((_D2D_MASK, half, half))
    return _push("rs_share", list(mines), [], [], remote)


def _small_plan():
    slot = lambda bufs, me: bufs[0].at[4 * me[0] + 2 * me[1] + me[2]]
    return [(mask, slot, slot) for mask in _ALL_MASKS]


def _small_start(v):
    me = _me()
    every = lax.dynamic_update_slice(lax.empty((8,) + v.shape, jnp.float32), v[None],
                                     (4 * me[0] + 2 * me[1] + me[2], 0, 0))
    send, recv, flying, token = _push_start("small_all_start", [every], _small_plan())
    return (send, recv, flying), token


def _small_end(started, after):
    send, recv, flying = started
    (every,) = _push_wait("small_all_wait", send, recv, flying, after, _small_plan())
    s = every.shape[1]

    def body(e_ref, o_ref):
        acc = e_ref[0]
        for j in range(1, 8):
            acc = acc + e_ref[j]
        o_ref[...] = acc

    return pl.pallas_call(
        body, name="small_sum", grid=(1,),
        in_specs=[pl.BlockSpec((8, s, D), lambda i: (0, 0, 0))],
        out_specs=pl.BlockSpec((s, D), lambda i: (0, 0)),
        out_shape=jax.ShapeDtypeStruct((s, D), jnp.float32),
        compiler_params=_params(),
    )(every)


def _adamw(name, w, g, m, v):
    r, c = w.shape
    rb = next(b for b in (256, 344, 128, 64, 32, 16, 8, r) if r % b == 0)

    def body(w_ref, g_ref, m_ref, v_ref, d_ref, mo_ref, vo_ref):
        g = g_ref[...]
        m = ADAM_B1 * m_ref[...] + (1.0 - ADAM_B1) * g
        v = ADAM_B2 * v_ref[...] + (1.0 - ADAM_B2) * (g * g)
        m_hat = m / (1.0 - ADAM_B1 ** ADAM_STEP)
        v_hat = v / (1.0 - ADAM_B2 ** ADAM_STEP)
        d_ref[...] = -ADAM_LR * (m_hat / (jnp.sqrt(v_hat) + ADAM_EPS) + ADAM_WD * w_ref[...])
        mo_ref[...] = m
        vo_ref[...] = v

    spec = pl.BlockSpec((rb, c), lambda i: (i, 0))
    return pl.pallas_call(
        body, name=name, grid=(r // rb,),
        in_specs=[spec] * 4, out_specs=[spec] * 3,
        out_shape=[jax.ShapeDtypeStruct((r, c), jnp.float32)] * 3,
        compiler_params=_params(),
    )(w, g, m, v)


_BIG = ("ffn1_w13", "ffn2_w13", "ffn1_w2", "ffn2_w2", "w_in", "w_out", "conv_pw")
_BIG_SHARD = {"ffn1_w13": (D, 1376), "ffn2_w13": (D, 1376), "ffn1_w2": (688, D), "ffn2_w2": (688, D),
              "w_in": (D, 704), "w_out": (256, D), "conv_pw": (64, 256)}


def _pack_rows(parts):
    flat = jnp.concatenate([p.reshape(-1) for p in parts])
    pad = (-flat.shape[0]) % (8 * D)
    if pad:
        flat = jnp.concatenate([flat, jnp.zeros((pad,), flat.dtype)])
    return flat.reshape(-1, D)


def _unpack_rows(buf, shapes):
    flat = buf.reshape(-1)
    out, off = [], 0
    for shp in shapes:
        n = math.prod(shp)
        out.append(flat[off:off + n].reshape(shp))
        off += n
    return out


def _pack_shard(parts):
    zeros = lambda n: jnp.zeros((n, D), parts["w_out"].dtype)
    a = jnp.concatenate([parts["ffn1_w13"].T, parts["ffn1_w2"], zeros(A_ROWS - OFF_WIN)], axis=0)
    b = jnp.concatenate([parts["ffn2_w13"].T, parts["ffn2_w2"], parts["w_in"].T, parts["w_out"],
                         parts["conv_pw"].reshape(PW_ROWS, D), zeros(B_ROWS - OFF_PW - PW_ROWS)], axis=0)
    return a, b


def _unpack_shard(a, b):
    return {"ffn1_w13": a[OFF_W13:OFF_W13 + W13_ROWS].T, "ffn1_w2": a[OFF_W2:OFF_W2 + W2_ROWS],
            "ffn2_w13": b[OFF_W13:OFF_W13 + W13_ROWS].T, "ffn2_w2": b[OFF_W2:OFF_W2 + W2_ROWS],
            "w_in": b[OFF_WIN:OFF_WIN + WIN_ROWS].T, "w_out": b[OFF_WOUT:OFF_WOUT + WOUT_ROWS],
            "conv_pw": b[OFF_PW:OFF_PW + PW_ROWS].reshape(64, 256)}


def kernel(x, meta, ln_in_g, ln_in_b, ffn1_w13, ffn1_w2, w_in, pool_w, pool_scale, conv_dw, conv_db, conv_ln_g, conv_ln_b, conv_pw, ret_gn_g, w_out, ffn2_w13, ffn2_w2, ln_g, ln_b, loss_target, m_meta, m_ln_in_g, m_ln_in_b, m_ffn1_w13, m_ffn1_w2, m_w_in, m_pool_w, m_pool_scale, m_conv_dw, m_conv_db, m_conv_ln_g, m_conv_ln_b, m_conv_pw, m_ret_gn_g, m_w_out, m_ffn2_w13, m_ffn2_w2, m_ln_g, m_ln_b, v_meta, v_ln_in_g, v_ln_in_b, v_ffn1_w13, v_ffn1_w2, v_w_in, v_pool_w, v_pool_scale, v_conv_dw, v_conv_db, v_conv_ln_g, v_conv_ln_b, v_conv_pw, v_ret_gn_g, v_w_out, v_ffn2_w13, v_ffn2_w2, v_ln_g, v_ln_b):
    f32 = jnp.float32
    seq = x.shape[1]
    t = seq + ROW0
    me = _me()
    chip = _chip(me)
    big_w = {"ffn1_w13": ffn1_w13, "ffn2_w13": ffn2_w13, "ffn1_w2": ffn1_w2, "ffn2_w2": ffn2_w2,
             "w_in": w_in, "w_out": w_out, "conv_pw": conv_pw}

    wa, wb = [], []
    for l in range(DEPTH):
        a, b = _pack_shard({n: big_w[n][l].astype(_WIRE) for n in _BIG})
        wa.append(_own_slot(a))
        wb.append(_own_slot(b))
    small_shapes = [(N_META, 256), (DEPTH, CONV_W, 64), (DEPTH, 3, 256), (DEPTH, 3, 256)]
    small_all = _own_slot(_pack_rows([meta, conv_dw, ln_g, ln_b]))
    wrap = lambda f: (lambda al, ins, outs, me: f(al, me))
    (small_all,) = _push("gather_small", [small_all], [], [],
                         [(m, wrap(s), wrap(d)) for m, s, d in _gather_ici_plan([], True)])
    plan_a0 = _gather_ici_plan([A_ROWS], False)
    plan_b0 = _gather_ici_plan([B_ROWS], False)
    plan_l1 = _gather_ici_plan([A_ROWS, B_ROWS], False)
    a0_send, a0_recv, a0_flying, a0_token = _push_start("gather_ici_a0_start", [wa[0]], plan_a0)

    sm = [_unpack_rows(small_all[s], small_shapes) for s in range(N_SHARD)]
    meta_f = jnp.concatenate([sm[s][0] for s in range(N_SHARD)], axis=1) + a0_token[0, 0]
    cdw_f = jnp.concatenate([sm[s][1] for s in range(N_SHARD)], axis=2)
    lng_f = jnp.concatenate([sm[s][2] for s in range(N_SHARD)], axis=2)
    lnb_f = jnp.concatenate([sm[s][3] for s in range(N_SHARD)], axis=2)

    def mix_params(l):
        wbd = jnp.zeros((D_POOL, D_POOL), f32)
        for g in range(4):
            wbd = wbd.at[64 * g:64 * (g + 1), 64 * g:64 * (g + 1)].set(pool_w[l, g])
        cdw = jnp.pad(cdw_f[l], ((0, 1), (0, 0)))
        cvec = jnp.pad(jnp.stack([conv_db[l], conv_ln_g[l], conv_ln_b[l]]), ((0, 5), (0, 0)))
        wpw = wb[l][:, OFF_PW:OFF_PW + PW_ROWS].reshape(D_CONV, D_CONV)
        return (wbd.astype(_MM), pool_scale[l][None], cdw, cvec, wpw, ret_gn_g[l][None])

    gb_of = lambda l, i: jnp.stack([lng_f[l, i], lnb_f[l, i]])
    gb_in = jnp.stack([ln_in_g, ln_in_b])

    pos = jnp.arange(t, dtype=f32) - PAD
    inv_freq = ROPE_BASE ** (-jnp.arange(0, DH, 2, dtype=f32) / DH)
    ang = pos[:, None] * inv_freq[None, :]
    cs = jnp.concatenate([jnp.cos(ang), jnp.cos(ang)], axis=1)
    sn = jnp.concatenate([-jnp.sin(ang), jnp.sin(ang)], axis=1)

    xh, rstd = _ln_in_fwd(jnp.concatenate([jnp.zeros((PAD, D), f32), meta_f], axis=0), x[0])
    (wa[0],) = _push_wait("gather_ici_a0_wait", a0_send, a0_recv, a0_flying, xh, plan_a0)
    (wa[0],) = _gather_d2d("gather_d2d_a0", [wa[0]])
    b0_send, b0_recv, b0_flying, b0_token = _push_start("gather_ici_b0_start", [wb[0]], plan_b0)
    cur = (xh, rstd, gb_in + b0_token[0, 0])
    saved = []
    for l in range(DEPTH):
        if l == 1:
            wa[1], wb[1] = _push_wait("gather_ici_l1_wait", l1_send, l1_recv, l1_flying, cur[0], plan_l1)
            wa[1], wb[1] = _gather_d2d("gather_d2d_l1", [wa[1], wb[1]])
        a0 = cur
        xh1, r1, au1, hb1 = _ffn_fwd(f"ffn1_fwd_{l}", a0[0], a0[2], wa[l])
        a1 = (xh1, r1, gb_of(l, 0))
        if l == 0:
            (wb[0],) = _push_wait("gather_ici_b0_wait", b0_send, b0_recv, b0_flying, xh1, plan_b0)
            (wb[0],) = _gather_d2d("gather_d2d_b0", [wb[0]])
            l1_send, l1_recv, l1_flying, l1_token = _push_start("gather_ici_l1_start", [wa[1], wb[1]], plan_l1)
            a1 = (xh1, r1, a1[2] + l1_token[0, 0])
        z = _mix_in_fwd(f"mix_in_fwd_{l}", a1[0], a1[2], wb[l])
        mp = mix_params(l)
        ycat, st_in, ut_in, cv_in, yp_in = _mix_core_fwd(f"mix_core_fwd_{l}", z, cs, sn, *mp)
        xt_in = (cv_in, yp_in)
        xh2, r2 = _mix_out_fwd(f"mix_out_fwd_{l}", a1[0], a1[2], ycat, wb[l])
        a2 = (xh2, r2, gb_of(l, 1))
        xh3, r3, au2, hb2 = _ffn_fwd(f"ffn2_fwd_{l}", a2[0], a2[2], wb[l])
        a3 = (xh3, r3, gb_of(l, 2))
        saved.append((a0, a1, a2, a3, z, ycat, st_in, ut_in, xt_in, mp, au1, hb1, au2, hb2))
        cur = a3

    dy, loss_part = _loss_fwd_bwd(cur[0], cur[2], loss_target[0])
    loss = lax.psum(loss_part[0, 0], ("x", "y", "c"))

    g_ln_g = [[None] * 3 for _ in range(DEPTH)]
    g_ln_b = [[None] * 3 for _ in range(DEPTH)]
    g_small = [dict() for _ in range(DEPTH)]
    slot = lambda j: j

    def ffn_grads(gbuf, tag, l, hb, hid, dau, dffn):
        gbuf = _dw_into(f"dw13_{tag}_{l}", gbuf, dau, hb, FF_SLOT, [(0, W13_ROWS, slot, OFF_W13)])
        return _dw_into(f"dw2_{tag}_{l}", gbuf, hid, dffn, FF_SLOT,
                        [(0, W2_ROWS, lambda j: 2 * j, OFF_W2), (W2_ROWS, W2_ROWS, lambda j: 2 * j + 1, OFF_W2)])

    rs_plan = _rs_ici_plan()
    token = jnp.zeros((), f32)
    started = {}
    for l in reversed(range(DEPTH)):
        a0, a1, a2, a3, z, ycat, st_in, ut_in, xt_in, mp, au1, hb1, au2, hb2 = saved[l]
        g_a = lax.empty((N_SHARD, A_ROWS, D), _WIRE)
        g_b = lax.empty((N_SHARD, B_ROWS, D), _WIRE)
        dh, hid, dau, dffn, dgb = _ffn_bwd(f"ffn2_bwd_{l}", dy, a3[0], a3[1], a3[2] + token, au2, wb[l])
        g_ln_g[l][2], g_ln_b[l][2] = dgb[0], dgb[1]
        g_b = ffn_grads(g_b, "ffn2", l, hb2, hid, dau, dffn)
        dh_res, dycat, dsb, dgb = _mix_out_bwd(f"mix_out_bwd_{l}", dh, a2[0], a2[1], a2[2], wb[l])
        g_ln_g[l][1], g_ln_b[l][1] = dgb[0], dgb[1]
        g_b = _dw_into(f"dw_out_{l}", g_b, ycat, dsb, D,
                       [(WOUT_ROWS * s, WOUT_ROWS, lambda j, s=s: s, OFF_WOUT) for s in range(N_SHARD)])
        dz, dwbd, dwpw, dcdw, dsm = _mix_core_bwd(f"mix_core_bwd_{l}", z, dycat, cs, sn, st_in, ut_in, *xt_in, *mp)
        pw = jnp.concatenate([dwpw.astype(_WIRE).reshape(N_SHARD, PW_ROWS, D),
                              jnp.zeros((N_SHARD, B_ROWS - OFF_PW - PW_ROWS, D), _WIRE)], axis=1)
        g_b = lax.dynamic_update_slice(g_b, pw, (0, OFF_PW, 0))
        g_small[l] = dict(
            pool_w=jnp.stack([dwbd[64 * g:64 * (g + 1), 64 * g:64 * (g + 1)] for g in range(4)]),
            pool_scale=dsm[1, :256], conv_db=dsm[2, :256], conv_ln_g=dsm[3, :256], conv_ln_b=dsm[4, :256],
            ret_gn_g=dsm[0], conv_dw=dcdw[:CONV_W])
        dh, hb = _mix_in_bwd(f"mix_in_bwd_{l}", dh_res, dz, a1[0], a1[2], wb[l])
        g_b = _dw_into(f"dw_in_{l}", g_b, dz, hb, D_IN,
                       [(WIN_ROWS * s, WIN_ROWS, lambda j, s=s: s, OFF_WIN) for s in range(N_SHARD)])
        started["b", l], token = _rs_start(f"b{l}", g_b, rs_plan)
        dh, hid, dau, dffn, dgb = _ffn_bwd(f"ffn1_bwd_{l}", dh, a1[0], a1[1], a1[2] + token, au1, wa[l])
        g_ln_g[l][0], g_ln_b[l][0] = dgb[0], dgb[1]
        g_a = ffn_grads(g_a, "ffn1", l, hb1, hid, dau, dffn)
        g_a = lax.dynamic_update_slice(g_a, jnp.zeros((N_SHARD, A_ROWS - OFF_WIN, D), _WIRE), (0, OFF_WIN, 0))
        dy = dh
        started["a", l], token = _rs_start(f"a{l}", g_a, rs_plan)
    d_frames, d_head, dgb_in = _ln_in_bwd(dy, saved[0][0][0], saved[0][0][1], gb_in + token)
    grad_x = d_frames[None]

    small_parts = [
        d_head[PAD:ROW0],
        jnp.stack([g_small[l]["conv_dw"] for l in range(DEPTH)]),
        jnp.stack([jnp.stack(g_ln_g[l]) for l in range(DEPTH)]),
        jnp.stack([jnp.stack(g_ln_b[l]) for l in range(DEPTH)]),
        dgb_in[0], dgb_in[1],
        jnp.stack([g_small[l]["pool_w"] for l in range(DEPTH)]),
        jnp.stack([g_small[l]["pool_scale"] for l in range(DEPTH)]),
        jnp.stack([g_small[l]["conv_db"] for l in range(DEPTH)]),
        jnp.stack([g_small[l]["conv_ln_g"] for l in range(DEPTH)]),
        jnp.stack([g_small[l]["conv_ln_b"] for l in range(DEPTH)]),
        jnp.stack([g_small[l]["ret_gn_g"] for l in range(DEPTH)]),
    ]
    small_started, small_token = _small_start(_pack_rows(small_parts))

    keys, mines, after = (("b", 1), ("a", 1), ("b", 0), ("a", 0)), [], small_token
    for key in keys:
        after = _rs_end(started[key], after, rs_plan)
        mines.append(after)
    gsum = dict(zip(keys, _rs_share(mines)))
    g_big = [_unpack_shard(gsum["a", l], gsum["b", l]) for l in range(DEPTH)]
    grads = {n: jnp.stack([g_big[l][n] for l in range(DEPTH)]) for n in _BIG}
    red = _unpack_rows(_small_end(small_started, gsum[keys[-1]]), [p.shape for p in small_parts])
    grads["meta"] = lax.dynamic_slice_in_dim(red[0], 256 * chip, 256, axis=1)
    grads["conv_dw"] = lax.dynamic_slice_in_dim(red[1], 64 * chip, 64, axis=2)
    grads["ln_g"] = lax.dynamic_slice_in_dim(red[2], 256 * chip, 256, axis=2)
    grads["ln_b"] = lax.dynamic_slice_in_dim(red[3], 256 * chip, 256, axis=2)
    for n, v in zip(("ln_in_g", "ln_in_b", "pool_w", "pool_scale", "conv_db", "conv_ln_g", "conv_ln_b", "ret_gn_g"),
                    red[4:]):
        grads[n] = v

    names = ['meta', 'ln_in_g', 'ln_in_b', 'ffn1_w13', 'ffn1_w2', 'w_in', 'pool_w', 'pool_scale', 'conv_dw',
             'conv_db', 'conv_ln_g', 'conv_ln_b', 'conv_pw', 'ret_gn_g', 'w_out', 'ffn2_w13', 'ffn2_w2', 'ln_g', 'ln_b']
    ws = dict(meta=meta, ln_in_g=ln_in_g, ln_in_b=ln_in_b, ffn1_w13=ffn1_w13, ffn1_w2=ffn1_w2, w_in=w_in,
              pool_w=pool_w, pool_scale=pool_scale, conv_dw=conv_dw, conv_db=conv_db, conv_ln_g=conv_ln_g,
              conv_ln_b=conv_ln_b, conv_pw=conv_pw, ret_gn_g=ret_gn_g, w_out=w_out, ffn2_w13=ffn2_w13,
              ffn2_w2=ffn2_w2, ln_g=ln_g, ln_b=ln_b)
    ms = dict(meta=m_meta, ln_in_g=m_ln_in_g, ln_in_b=m_ln_in_b, ffn1_w13=m_ffn1_w13, ffn1_w2=m_ffn1_w2,
              w_in=m_w_in, pool_w=m_pool_w, pool_scale=m_pool_scale, conv_dw=m_conv_dw, conv_db=m_conv_db,
              conv_ln_g=m_conv_ln_g, conv_ln_b=m_conv_ln_b, conv_pw=m_conv_pw, ret_gn_g=m_ret_gn_g,
              w_out=m_w_out, ffn2_w13=m_ffn2_w13, ffn2_w2=m_ffn2_w2, ln_g=m_ln_g, ln_b=m_ln_b)
    vs = dict(meta=v_meta, ln_in_g=v_ln_in_g, ln_in_b=v_ln_in_b, ffn1_w13=v_ffn1_w13, ffn1_w2=v_ffn1_w2,
              w_in=v_w_in, pool_w=v_pool_w, pool_scale=v_pool_scale, conv_dw=v_conv_dw, conv_db=v_conv_db,
              conv_ln_g=v_conv_ln_g, conv_ln_b=v_conv_ln_b, conv_pw=v_conv_pw, ret_gn_g=v_ret_gn_g,
              w_out=v_w_out, ffn2_w13=v_ffn2_w13, ffn2_w2=v_ffn2_w2, ln_g=v_ln_g, ln_b=v_ln_b)
    delta, new_m, new_v = {}, {}, {}
    for n in _BIG:
        shp = ws[n].shape
        two = lambda a: a.reshape(-1, shp[-1])
        d_, m_, v_ = _adamw("adamw_" + n, two(ws[n]), two(grads[n]), two(ms[n]), two(vs[n]))
        delta[n], new_m[n], new_v[n] = d_.reshape(shp), m_.reshape(shp), v_.reshape(shp)
    small_names = [n for n in names if n not in _BIG]
    pk = lambda d: _pack_rows([d[n] for n in small_names])
    d_, m_, v_ = _adamw("adamw_small", pk(ws), pk(grads), pk(ms), pk(vs))
    shapes = [ws[n].shape for n in small_names]
    for n, a, b, c in zip(small_names, _unpack_rows(d_, shapes), _unpack_rows(m_, shapes), _unpack_rows(v_, shapes)):
        delta[n], new_m[n], new_v[n] = a, b, c

    return (loss, grad_x, *[grads[n] for n in names], *[delta[n] for n in names],
            *[new_m[n] for n in names], *[new_v[n] for n in names])
```

```python
import functools
import math

import jax
import jax.numpy as jnp
from jax import lax
from jax.experimental import pallas as pl
from jax.experimental.pallas import tpu as pltpu

D = 1024
DEPTH = 2
N_META = 16
PAD = 112
ROW0 = PAD + N_META
D_POOL = 256
D_CONV = 256
D_RET = 512
HEADS = 4
DH = 128
CONV_W = 31
D_FF = 2752
FF_SLOT = 1408
D_FFP = 2 * FF_SLOT
D_IN = 2816
N_SHARD = 4
ALPHA = (2.0 * DEPTH) ** 0.25
LN_EPS = 1e-5
ROPE_BASE = 10000.0
LOG_GAMMA = tuple(math.log(1.0 - 2.0 ** (-5.0 - h)) for h in range(HEADS))
ADAM_LR, ADAM_B1, ADAM_B2, ADAM_EPS, ADAM_WD, ADAM_STEP = 0.001, 0.9, 0.999, 1e-08, 0.01, 10

_MM = jnp.bfloat16
_WIRE = jnp.bfloat16
_VMEM_LIMIT = 56 * 1024 * 1024
_FFN_ROWS = 640

MESH = pl.DeviceIdType.MESH
_ANY = pl.BlockSpec(memory_space=pl.ANY)

W13_ROWS = 1376
W2_ROWS = 688
WIN_ROWS = 704
WOUT_ROWS = 256
PW_ROWS = 16
OFF_W13 = 0
OFF_W2 = W13_ROWS
OFF_WIN = W13_ROWS + W2_ROWS
OFF_WOUT = OFF_WIN + WIN_ROWS
OFF_PW = OFF_WOUT + WOUT_ROWS
A_ROWS = 2080
B_ROWS = 3072
W2_SLOT_OFF = (0, W2_ROWS, FF_SLOT, FF_SLOT + W2_ROWS)


def _row_copies(w_ref, off, n, dst_of, sems, k0):
    return [pltpu.make_async_copy(w_ref.at[s, pl.ds(off, n)], dst_of(s), sems.at[k0 + s]) for s in range(N_SHARD)]


def _load_ffn_weights(w_ref, w13, w2, sems):
    cps = _row_copies(w_ref, OFF_W13, W13_ROWS, lambda s: w13.at[s, pl.ds(0, W13_ROWS)], sems, 0)
    cps += _row_copies(w_ref, OFF_W2, W2_ROWS, lambda s: w2.at[pl.ds(W2_SLOT_OFF[s], W2_ROWS)], sems, 4)
    for cp in cps:
        cp.start()
    zpad = jnp.zeros((FF_SLOT - W13_ROWS, D), w13.dtype)
    for s in range(N_SHARD):
        w13[s, W13_ROWS:FF_SLOT, :] = zpad
    w2[W13_ROWS:FF_SLOT, :] = zpad
    w2[FF_SLOT + W13_ROWS:D_FFP, :] = zpad
    for cp in cps:
        cp.wait()


def _load_rows(w_ref, off, n, dst, sems):
    cps = _row_copies(w_ref, off, n, lambda s: dst.at[pl.ds(s * n, n)], sems, 0)
    for cp in cps:
        cp.start()
    for cp in cps:
        cp.wait()


def _dot(a, b):
    return jnp.dot(a, b, preferred_element_type=jnp.float32)


def _dot_nt(a, b):
    return lax.dot_general(a, b, (((1,), (1,)), ((), ())), preferred_element_type=jnp.float32)


def _dot_tn(a, b):
    return lax.dot_general(a, b, (((0,), (0,)), ((), ())), preferred_element_type=jnp.float32)


def _params(sem=("arbitrary",)):
    return pltpu.CompilerParams(dimension_semantics=sem, vmem_limit_bytes=_VMEM_LIMIT)


def _row_block(t, cap=640):
    for rb in (640, 320, 128):
        if rb <= cap and t % rb == 0 and (t > 1024 or rb == 128):
            return rb
    raise ValueError(t)


def _rows(rb, n):
    return pl.BlockSpec((rb, n), lambda i: (i, 0))


def _full(shape):
    nd = len(shape)
    return pl.BlockSpec(tuple(shape), lambda i: (0,) * nd, pipeline_mode=pl.Buffered(1))


def _acc(shape):
    nd = len(shape)
    return pl.BlockSpec(tuple(shape), lambda i: (0,) * nd)


def _sigmoid(x):
    return 1.0 / (1.0 + jnp.exp(-x))


def _ln_fwd(s):
    mu = jnp.mean(s, axis=-1, keepdims=True)
    xc = s - mu
    var = jnp.mean(xc * xc, axis=-1, keepdims=True)
    rstd = lax.rsqrt(var + LN_EPS)
    return xc * rstd, rstd


def _ln_bwd(dxh, xh, rstd):
    m1 = jnp.mean(dxh, axis=-1, keepdims=True)
    m2 = jnp.mean(dxh * xh, axis=-1, keepdims=True)
    return rstd * (dxh - m1 - xh * m2)


def _frames(n):
    return pl.BlockSpec((ROW0, n), lambda i: (jnp.maximum(i - 1, 0), 0))


def _ln_in_fwd(head, x2d):
    t = x2d.shape[0] + ROW0

    def body(head_ref, x_ref, xh_ref, rstd_ref):
        raw = jnp.where(pl.program_id(0) == 0, head_ref[...], x_ref[...])
        xh, rstd = _ln_fwd(raw)
        xh_ref[...] = xh
        rstd_ref[...] = rstd

    return pl.pallas_call(
        body, name="ln_in_fwd", grid=(t // ROW0,),
        in_specs=[_full((ROW0, D)), _frames(D)],
        out_specs=[_rows(ROW0, D), _rows(ROW0, 1)],
        out_shape=[jax.ShapeDtypeStruct((t, D), jnp.float32), jax.ShapeDtypeStruct((t, 1), jnp.float32)],
        compiler_params=_params(),
    )(head, x2d)


def _ffn_fwd(name, xh, gb, wfull):
    t = xh.shape[0]
    rb = _row_block(t, _FFN_ROWS)

    def body(xh_ref, gb_ref, w_ref, out_ref, rstd_ref, au_ref, hb_ref, w13, w2, sems):
        @pl.when(pl.program_id(0) == 0)
        def _():
            _load_ffn_weights(w_ref, w13, w2, sems)

        h = xh_ref[...] * gb_ref[0:1, :] + gb_ref[1:2, :]
        hb = h.astype(_MM)
        hb_ref[...] = hb
        acc = jnp.zeros((rb, D), jnp.float32)
        for j in range(2):
            lo = j * FF_SLOT
            a = _dot_nt(hb, w13[j])
            u = _dot_nt(hb, w13[2 + j])
            au_ref[:, lo:lo + FF_SLOT] = a.astype(_MM)
            au_ref[:, D_FFP + lo:D_FFP + lo + FF_SLOT] = u.astype(_MM)
            hid = (a * _sigmoid(a) * u).astype(_MM)
            acc = acc + _dot(hid, w2[lo:lo + FF_SLOT, :])
        xo, rstd = _ln_fwd(ALPHA * h + 0.5 * acc)
        out_ref[...] = xo
        rstd_ref[...] = rstd

    return pl.pallas_call(
        body, name=name, grid=(t // rb,),
        in_specs=[_rows(rb, D), _full((2, D)), _ANY],
        out_specs=[_rows(rb, D), _rows(rb, 1), _rows(rb, 2 * D_FFP), _rows(rb, D)],
        out_shape=[jax.ShapeDtypeStruct((t, D), jnp.float32), jax.ShapeDtypeStruct((t, 1), jnp.float32),
                   jax.ShapeDtypeStruct((t, 2 * D_FFP), _MM), jax.ShapeDtypeStruct((t, D), _MM)],
        scratch_shapes=[pltpu.VMEM((N_SHARD, FF_SLOT, D), _MM), pltpu.VMEM((D_FFP, D), _MM),
                        pltpu.SemaphoreType.DMA((8,))],
        compiler_params=_params(),
    )(xh, gb, wfull)


def _ffn_bwd(name, dy, xo, rstd, gb_out, au, wfull):
    t = xo.shape[0]
    rb = _row_block(t, 320)

    def body(dy_ref, xo_ref, rstd_ref, gbo_ref, au_ref, w_ref,
             dh_ref, hid_ref, dau_ref, dffn_ref, dgb_ref, w13, w2, sems):
        i = pl.program_id(0)

        @pl.when(i == 0)
        def _():
            dgb_ref[...] = jnp.zeros_like(dgb_ref)
            _load_ffn_weights(w_ref, w13, w2, sems)

        dy = dy_ref[...]
        xo = xo_ref[...]
        dgb_ref[0:1, :] += jnp.sum(dy * xo, axis=0, keepdims=True)
        dgb_ref[1:2, :] += jnp.sum(dy, axis=0, keepdims=True)
        ds = _ln_bwd(dy * gbo_ref[0:1, :], xo, rstd_ref[...])
        dffn = (0.5 * ds).astype(_MM)
        dffn_ref[...] = dffn
        dh = ALPHA * ds
        for j in range(2):
            lo = j * FF_SLOT
            a = au_ref[:, lo:lo + FF_SLOT].astype(jnp.float32)
            u = au_ref[:, D_FFP + lo:D_FFP + lo + FF_SLOT].astype(jnp.float32)
            sg = _sigmoid(a)
            si = a * sg
            hid_ref[:, lo:lo + FF_SLOT] = (si * u).astype(_MM)
            dhid = _dot_nt(dffn, w2[lo:lo + FF_SLOT, :])
            da = (dhid * u * (sg * (1.0 + a * (1.0 - sg)))).astype(_MM)
            du = (dhid * si).astype(_MM)
            dau_ref[:, lo:lo + FF_SLOT] = da
            dau_ref[:, D_FFP + lo:D_FFP + lo + FF_SLOT] = du
            dh = dh + _dot(da, w13[j]) + _dot(du, w13[2 + j])
        dh_ref[...] = dh

    return pl.pallas_call(
        body, name=name, grid=(t // rb,),
        in_specs=[_rows(rb, D), _rows(rb, D), _rows(rb, 1), _full((2, D)), _rows(rb, 2 * D_FFP), _ANY],
        out_specs=[_rows(rb, D), _rows(rb, D_FFP), _rows(rb, 2 * D_FFP), _rows(rb, D), _acc((8, D))],
        out_shape=[jax.ShapeDtypeStruct((t, D), jnp.float32),
                   jax.ShapeDtypeStruct((t, D_FFP), _MM), jax.ShapeDtypeStruct((t, 2 * D_FFP), _MM),
                   jax.ShapeDtypeStruct((t, D), _MM), jax.ShapeDtypeStruct((8, D), jnp.float32)],
        scratch_shapes=[pltpu.VMEM((N_SHARD, FF_SLOT, D), _MM), pltpu.VMEM((D_FFP, D), _MM),
                        pltpu.SemaphoreType.DMA((8,))],
        compiler_params=_params(),
    )(dy, xo, rstd, gb_out, au, wfull)


def _mix_in_fwd(name, xh, gb, wfull):
    t = xh.shape[0]
    rb = _row_block(t)

    def body(xh_ref, gb_ref, w_ref, z_ref, wt, sems):
        @pl.when(pl.program_id(0) == 0)
        def _():
            _load_rows(w_ref, OFF_WIN, WIN_ROWS, wt, sems)

        h = xh_ref[...] * gb_ref[0:1, :] + gb_ref[1:2, :]
        z = _dot_nt(h.astype(_MM), wt[...])
        row = pl.program_id(0) * rb + lax.broadcasted_iota(jnp.int32, (rb, 1), 0)
        z_ref[...] = jnp.where(row >= PAD, z, 0.0)

    return pl.pallas_call(
        body, name=name, grid=(t // rb,),
        in_specs=[_rows(rb, D), _full((2, D)), _ANY],
        out_specs=_rows(rb, D_IN),
        out_shape=jax.ShapeDtypeStruct((t, D_IN), jnp.float32),
        scratch_shapes=[pltpu.VMEM((D_IN, D), _MM), pltpu.SemaphoreType.DMA((4,))],
        compiler_params=_params(),
    )(xh, gb, wfull)


def _mix_in_bwd(name, dh_res, dz, xh, gb, wfull):
    t = xh.shape[0]
    rb = _row_block(t)

    def body(dhr_ref, dz_ref, xh_ref, gb_ref, w_ref, dh_ref, hb_ref, wt, sems):
        @pl.when(pl.program_id(0) == 0)
        def _():
            _load_rows(w_ref, OFF_WIN, WIN_ROWS, wt, sems)

        dh_ref[...] = dhr_ref[...] + _dot(dz_ref[...], wt[...])
        hb_ref[...] = (xh_ref[...] * gb_ref[0:1, :] + gb_ref[1:2, :]).astype(_MM)

    return pl.pallas_call(
        body, name=name, grid=(t // rb,),
        in_specs=[_rows(rb, D), _rows(rb, D_IN), _rows(rb, D), _full((2, D)), _ANY],
        out_specs=[_rows(rb, D), _rows(rb, D)],
        out_shape=[jax.ShapeDtypeStruct((t, D), jnp.float32), jax.ShapeDtypeStruct((t, D), _MM)],
        scratch_shapes=[pltpu.VMEM((D_IN, D), _MM), pltpu.SemaphoreType.DMA((4,))],
        compiler_params=_params(),
    )(dh_res, dz, xh, gb, wfull)


def _mix_out_fwd(name, xh, gb, ycat, wfull):
    t = xh.shape[0]
    rb = _row_block(t)

    def body(xh_ref, gb_ref, y_ref, w_ref, out_ref, rstd_ref, wo, sems):
        @pl.when(pl.program_id(0) == 0)
        def _():
            _load_rows(w_ref, OFF_WOUT, WOUT_ROWS, wo, sems)

        h = xh_ref[...] * gb_ref[0:1, :] + gb_ref[1:2, :]
        xo, rstd = _ln_fwd(ALPHA * h + _dot(y_ref[...], wo[...]))
        out_ref[...] = xo
        rstd_ref[...] = rstd

    return pl.pallas_call(
        body, name=name, grid=(t // rb,),
        in_specs=[_rows(rb, D), _full((2, D)), _rows(rb, D), _ANY],
        out_specs=[_rows(rb, D), _rows(rb, 1)],
        out_shape=[jax.ShapeDtypeStruct((t, D), jnp.float32), jax.ShapeDtypeStruct((t, 1), jnp.float32)],
        scratch_shapes=[pltpu.VMEM((D, D), _MM), pltpu.SemaphoreType.DMA((4,))],
        compiler_params=_params(),
    )(xh, gb, ycat, wfull)


def _mix_out_bwd(name, dy, xo, rstd, gb_out, wfull):
    t = xo.shape[0]
    rb = _row_block(t)

    def body(dy_ref, xo_ref, rstd_ref, gbo_ref, w_ref, dhr_ref, dyc_ref, dsb_ref, dgb_ref, wo, sems):
        @pl.when(pl.program_id(0) == 0)
        def _():
            dgb_ref[...] = jnp.zeros_like(dgb_ref)
            _load_rows(w_ref, OFF_WOUT, WOUT_ROWS, wo, sems)

        dy = dy_ref[...]
        xo = xo_ref[...]
        dgb_ref[0:1, :] += jnp.sum(dy * xo, axis=0, keepdims=True)
        dgb_ref[1:2, :] += jnp.sum(dy, axis=0, keepdims=True)
        ds = _ln_bwd(dy * gbo_ref[0:1, :], xo, rstd_ref[...])
        dsb = ds.astype(_MM)
        dsb_ref[...] = dsb
        dhr_ref[...] = ALPHA * ds
        dyc_ref[...] = _dot_nt(dsb, wo[...])

    return pl.pallas_call(
        body, name=name, grid=(t // rb,),
        in_specs=[_rows(rb, D), _rows(rb, D), _rows(rb, 1), _full((2, D)), _ANY],
        out_specs=[_rows(rb, D), _rows(rb, D), _rows(rb, D), _acc((8, D))],
        out_shape=[jax.ShapeDtypeStruct((t, D), jnp.float32), jax.ShapeDtypeStruct((t, D), jnp.float32),
                   jax.ShapeDtypeStruct((t, D), _MM), jax.ShapeDtypeStruct((8, D), jnp.float32)],
        scratch_shapes=[pltpu.VMEM((D, D), _MM), pltpu.SemaphoreType.DMA((4,))],
        compiler_params=_params(),
    )(dy, xo, rstd, gb_out, wfull)


def _loss_fwd_bwd(xh, gb, target):
    t = xh.shape[0]

    def body(xh_ref, gb_ref, tg_ref, dy_ref, loss_ref):
        @pl.when(pl.program_id(0) == 0)
        def _():
            loss_ref[...] = jnp.zeros_like(loss_ref)

        y = xh_ref[...] * gb_ref[0:1, :] + gb_ref[1:2, :]
        err = jnp.where(pl.program_id(0) > 0, y - tg_ref[...], 0.0)
        dy_ref[...] = err * (1.0 / D)
        per_row = jnp.mean(err * err, axis=-1, keepdims=True)
        loss_ref[...] += 0.5 * jnp.sum(per_row, axis=0, keepdims=True)

    return pl.pallas_call(
        body, name="loss", grid=(t // ROW0,),
        in_specs=[_rows(ROW0, D), _full((2, D)), _frames(D)],
        out_specs=[_rows(ROW0, D), _acc((1, 1))],
        out_shape=[jax.ShapeDtypeStruct((t, D), jnp.float32), jax.ShapeDtypeStruct((1, 1), jnp.float32)],
        compiler_params=_params(),
    )(xh, gb, target)


def _ln_in_bwd(dy, xh, rstd, gb):
    t = xh.shape[0]

    def body(dy_ref, xh_ref, rstd_ref, gb_ref, dx_ref, dhead_ref, dgb_ref):
        i = pl.program_id(0)

        @pl.when(i == 0)
        def _():
            dgb_ref[...] = jnp.zeros_like(dgb_ref)

        dy = dy_ref[...]
        xh = xh_ref[...]
        dgb_ref[0:1, :] += jnp.sum(dy * xh, axis=0, keepdims=True)
        dgb_ref[1:2, :] += jnp.sum(dy, axis=0, keepdims=True)
        dx = _ln_bwd(dy * gb_ref[0:1, :], xh, rstd_ref[...])

        @pl.when(i == 0)
        def _():
            dhead_ref[...] = dx

        @pl.when(i > 0)
        def _():
            dx_ref[...] = dx

    return pl.pallas_call(
        body, name="ln_in_bwd", grid=(t // ROW0,),
        in_specs=[_rows(ROW0, D), _rows(ROW0, D), _rows(ROW0, 1), _full((2, D))],
        out_specs=[_frames(D), _acc((ROW0, D)), _acc((8, D))],
        out_shape=[jax.ShapeDtypeStruct((t - ROW0, D), jnp.float32), jax.ShapeDtypeStruct((ROW0, D), jnp.float32),
                   jax.ShapeDtypeStruct((8, D), jnp.float32)],
        compiler_params=_params(),
    )(dy, xh, rstd, gb)


def _dw_rows(t, cols):
    for tt in (2080, 1664, 640, 128):
        vmem = 2 * tt * (cols + D) * 2 + cols * D * 6
        if t % tt == 0 and (t > 1024 or tt == 128) and vmem <= 44 * 1024 * 1024:
            return tt
    raise ValueError((t, cols))


def _dw_into(name, gpack, x, y, cols, pieces):
    t, k = x.shape
    tt = _dw_rows(t, cols)
    nt = t // tt

    def body(x_ref, y_ref, g_in, g_out, acc, stage, sems):
        j = pl.program_id(0)
        s = pl.program_id(1)

        @pl.when(s == 0)
        def _():
            acc[...] = jnp.zeros_like(acc)

        acc[...] += _dot_tn(x_ref[...], y_ref[...])

        @pl.when(s == nt - 1)
        def _():
            stage[...] = acc[...].astype(stage.dtype)
            cps = []
            for q, (lo, n, chip_of, off) in enumerate(pieces):
                cp = pltpu.make_async_copy(stage.at[pl.ds(lo, n)], g_out.at[chip_of(j), pl.ds(off, n)],
                                           sems.at[q])
                cp.start()
                cps.append(cp)
            for cp in cps:
                cp.wait()

    return pl.pallas_call(
        body, name=name, grid=(k // cols, nt),
        in_specs=[pl.BlockSpec((tt, cols), lambda j, s: (s, j)), pl.BlockSpec((tt, D), lambda j, s: (s, 0)), _ANY],
        out_specs=_ANY,
        out_shape=jax.ShapeDtypeStruct(gpack.shape, gpack.dtype),
        input_output_aliases={2: 0},
        scratch_shapes=[pltpu.VMEM((cols, D), jnp.float32), pltpu.VMEM((cols, D), gpack.dtype),
                        pltpu.SemaphoreType.DMA((len(pieces),))],
        compiler_params=_params(("arbitrary", "arbitrary")),
    )(x, y, gpack)


_TAIL_U = 32
_TAIL_X = 32
_MIX_ROWS = 320


def _decay_mask(rb, h):
    ii = lax.broadcasted_iota(jnp.int32, (rb, rb), 0)
    jj = lax.broadcasted_iota(jnp.int32, (rb, rb), 1)
    dist = jnp.abs(ii - jj).astype(jnp.float32)
    vis = (jj >> 6) <= (ii >> 6)
    return jnp.where(vis, jnp.exp(LOG_GAMMA[h] * dist), 0.0)


def _row_decays(rb, h):
    r = lax.broadcasted_iota(jnp.int32, (rb, DH), 0).astype(jnp.float32)
    return jnp.exp(LOG_GAMMA[h] * (r + 1.0)), jnp.exp(LOG_GAMMA[h] * (rb - 1.0 - r))


def _rope(x, cs, sn):
    return x * cs + pltpu.roll(x, DH // 2, 1) * sn


def _rope_t(dx, cs, sn):
    return dx * cs + pltpu.roll(dx * sn, DH // 2, 1)


def _pool_count(blk, rb):
    row = blk * rb + lax.broadcasted_iota(jnp.int32, (rb, D_POOL), 0) - PAD
    lane = lax.broadcasted_iota(jnp.int32, (rb, D_POOL), 1)
    win = jnp.left_shift(2, lane >> 6)
    return jnp.clip(row + 1, 1, win).astype(jnp.float32)


def _pool_select(p2, p4, p8, p16):
    lane = lax.broadcasted_iota(jnp.int32, p2.shape, 1)
    return jnp.where(lane < 64, p2, jnp.where(lane < 128, p4, jnp.where(lane < 192, p8, p16)))


def _trailing_windows(ext, p2, p4, p8, rb):
    n = _TAIL_X + rb
    p2[8:n, :] = ext[8:n, :] + ext[pl.ds(7, n - 8), :]
    p4[16:n, :] = p2[16:n, :] + p2[pl.ds(14, n - 16), :]
    p8[24:n, :] = p4[24:n, :] + p4[pl.ds(20, n - 24), :]
    lo = _TAIL_X
    p16 = p8[lo:n, :] + p8[lo - 8:n - 8, :]
    return _pool_select(p2[lo:n, :], p4[lo:n, :], p8[lo:n, :], p16)


def _leading_windows(ext, p2, p4, p8, rb):
    p2[0:rb + 24, :] = ext[0:rb + 24, :] + ext[pl.ds(1, rb + 24), :]
    p4[0:rb + 16, :] = p2[0:rb + 16, :] + p2[pl.ds(2, rb + 16), :]
    p8[0:rb + 8, :] = p4[0:rb + 8, :] + p4[pl.ds(4, rb + 8), :]
    p16 = p8[0:rb, :] + p8[8:rb + 8, :]
    return _pool_select(p2[0:rb, :], p4[0:rb, :], p8[0:rb, :], p16)


def _shifted_copies(ext, copies, first, n, sign):
    for b in range(1, 8):
        copies[b - 1, first:first + n, :] = ext[pl.ds(first - sign * b, n), :]


def _tap(ext, copies, k, start, rows, sign):
    a, b = divmod(k, 8)
    src = ext if b == 0 else copies.at[b - 1]
    return src[pl.ds(start - sign * 8 * a, rows), :]


def _sub_rows(rb):
    return 128 if rb % 128 == 0 else 64


def _mix_core_fwd(name, z, cs, sn, wbd, pscale, cdw, cvec, wpw, gn):
    t = z.shape[0]
    rb = _row_block(t, _MIX_ROWS)
    nblk = t // rb
    sr = _sub_rows(rb)

    def body(z_ref, cs_ref, sn_ref, wbd_ref, ps_ref, cdw_ref, cvec_ref, wpw_ref, gn_ref,
             y_ref, st_ref, ut_ref, cv_ref, yp_ref,
             uext, xext, cv, p2, p4, p8, ucopies, state, wmask):
        i = pl.program_id(0)

        @pl.when(i == 0)
        def _():
            state[...] = jnp.zeros_like(state)
            uext[0:_TAIL_U, :] = jnp.zeros((_TAIL_U, D_CONV), jnp.float32)
            xext[0:_TAIL_X, :] = jnp.zeros((_TAIL_X, D_POOL), jnp.float32)
            for h in range(HEADS):
                wmask[h] = _decay_mask(rb, h)

        st_ref[0] = state[...]
        ut_ref[0] = uext[0:_TAIL_U, :]

        xp = z_ref[:, 0:256]
        uext[_TAIL_U:_TAIL_U + rb, :] = z_ref[:, 256:512] * _sigmoid(z_ref[:, 512:768])
        xext[_TAIL_X:_TAIL_X + rb, :] = xp

        _shifted_copies(uext, ucopies, 8, rb + _TAIL_U - 8, 1)
        for r in range(0, rb, sr):
            acc = jnp.zeros((sr, D_CONV), jnp.float32)
            for k in range(CONV_W):
                acc = acc + _tap(uext, ucopies, k, _TAIL_U + r, sr, 1) * cdw_ref[CONV_W - 1 - k:CONV_W - k, :]
            cv[r:r + sr, :] = acc

        win = _trailing_windows(xext, p2, p4, p8, rb)
        ypb = (win / _pool_count(i, rb) - xp).astype(_MM)
        yp_ref[...] = ypb
        y_ref[:, 0:256] = (_dot(ypb, wbd_ref[...]) * ps_ref[...]).astype(_MM)
        cv_ref[...] = cv[...]
        cn, _ = _ln_fwd(cv[...] + cvec_ref[0:1, :])
        ln = cn * cvec_ref[1:2, :] + cvec_ref[2:3, :]
        sw = ln * _sigmoid(ln)
        y_ref[:, 256:512] = _dot(sw.astype(_MM), wpw_ref[...]).astype(_MM)
        csv = cs_ref[...]
        snv = sn_ref[...]
        for h in range(HEADS):
            q = _rope(z_ref[:, 768 + h * DH:768 + (h + 1) * DH], csv, snv)
            k = _rope(z_ref[:, 1280 + h * DH:1280 + (h + 1) * DH], csv, snv) * (DH ** -0.5)
            vb = z_ref[:, 1792 + h * DH:1792 + (h + 1) * DH].astype(_MM)
            g = z_ref[:, 2304 + h * DH:2304 + (h + 1) * DH]
            a, b = _row_decays(rb, h)
            s = _dot_nt(q.astype(_MM), k.astype(_MM)) * wmask[h]
            o = _dot(s.astype(_MM), vb) + _dot((q * a).astype(_MM), state[h].astype(_MM))
            state[h] = math.exp(LOG_GAMMA[h] * rb) * state[h] + _dot_tn((k * b).astype(_MM), vb)
            on, _ = _ln_fwd(o)
            y_ref[:, 512 + h * DH:512 + (h + 1) * DH] = (
                g * _sigmoid(g) * on * gn_ref[:, h * DH:(h + 1) * DH]).astype(_MM)

        uext[0:_TAIL_U, :] = uext[rb:rb + _TAIL_U, :]
        xext[0:_TAIL_X, :] = xext[rb:rb + _TAIL_X, :]

    return pl.pallas_call(
        body, name=name, grid=(nblk,),
        in_specs=[_rows(rb, D_IN), _rows(rb, DH), _rows(rb, DH), _full((256, 256)), _full((1, 256)),
                  _full((32, 256)), _full((8, 256)), _full((256, 256)), _full((1, D_RET))],
        out_specs=[_rows(rb, D),
                   pl.BlockSpec((1, HEADS, DH, DH), lambda i: (i, 0, 0, 0)),
                   pl.BlockSpec((1, _TAIL_U, D_CONV), lambda i: (i, 0, 0)),
                   _rows(rb, D_CONV), _rows(rb, D_POOL)],
        out_shape=[jax.ShapeDtypeStruct((t, D), _MM),
                   jax.ShapeDtypeStruct((nblk, HEADS, DH, DH), jnp.float32),
                   jax.ShapeDtypeStruct((nblk, _TAIL_U, D_CONV), jnp.float32),
                   jax.ShapeDtypeStruct((t, D_CONV), jnp.float32),
                   jax.ShapeDtypeStruct((t, D_POOL), _MM)],
        scratch_shapes=[pltpu.VMEM((rb + _TAIL_U, D_CONV), jnp.float32),
                        pltpu.VMEM((rb + _TAIL_X, D_POOL), jnp.float32),
                        pltpu.VMEM((rb, D_CONV), jnp.float32),
                        pltpu.VMEM((rb + _TAIL_X, D_POOL), jnp.float32),
                        pltpu.VMEM((rb + _TAIL_X, D_POOL), jnp.float32),
                        pltpu.VMEM((rb + _TAIL_X, D_POOL), jnp.float32),
                        pltpu.VMEM((7, rb + _TAIL_U, D_CONV), jnp.float32),
                        pltpu.VMEM((HEADS, DH, DH), jnp.float32),
                        pltpu.VMEM((HEADS, rb, rb), jnp.float32)],
        compiler_params=_params(),
    )(z, cs, sn, wbd, pscale, cdw, cvec, wpw, gn)


def _mix_core_bwd(name, z, dyc, cs, sn, st_in, ut_in, cv_in, yp_in, wbd, pscale, cdw, cvec, wpw, gn):
    t = z.shape[0]
    rb = _row_block(t, _MIX_ROWS)
    nblk = t // rb
    sr = _sub_rows(rb)
    rev = lambda i: nblk - 1 - i

    def body(z_ref, dy_ref, cs_ref, sn_ref, st_ref, ut_ref, cv_ref, yp_ref,
             wbd_ref, ps_ref, cdw_ref, cvec_ref, wpw_ref, gn_ref,
             dz_ref, dwbd_ref, dwpw_ref, dcdw_ref, dsm_ref,
             uext, cv, dcvext, eext, p2, p4, p8, ucopies, dcopies, dstate, wmask):
        i = pl.program_id(0)
        blk = nblk - 1 - i

        @pl.when(i == 0)
        def _():
            dstate[...] = jnp.zeros_like(dstate)
            dcvext[rb:rb + _TAIL_U, :] = jnp.zeros((_TAIL_U, D_CONV), jnp.float32)
            eext[rb:rb + _TAIL_X, :] = jnp.zeros((_TAIL_X, D_POOL), jnp.float32)
            dwbd_ref[...] = jnp.zeros_like(dwbd_ref)
            dwpw_ref[...] = jnp.zeros_like(dwpw_ref)
            dcdw_ref[...] = jnp.zeros_like(dcdw_ref)
            dsm_ref[...] = jnp.zeros_like(dsm_ref)
            for h in range(HEADS):
                wmask[h] = _decay_mask(rb, h)

        row = blk * rb + lax.broadcasted_iota(jnp.int32, (rb, 1), 0)
        live = row >= PAD

        ca = z_ref[:, 256:512]
        sg_c = _sigmoid(z_ref[:, 512:768])
        uext[0:_TAIL_U, :] = ut_ref[0]
        uext[_TAIL_U:_TAIL_U + rb, :] = ca * sg_c

        cnt = _pool_count(blk, rb)
        ypb = yp_ref[...]
        dyp = dy_ref[:, 0:256]
        pm = _dot(ypb, wbd_ref[...])
        dsm_ref[1:2, 0:256] += jnp.sum(dyp * pm, axis=0, keepdims=True)
        dpm = (dyp * ps_ref[...]).astype(_MM)
        dwbd_ref[...] += _dot_tn(ypb, dpm)
        dypre = _dot_nt(dpm, wbd_ref[...])
        eext[0:rb, :] = dypre / cnt
        win = _leading_windows(eext, p2, p4, p8, rb)
        dz_ref[:, 0:256] = jnp.where(live, win - dypre, 0.0).astype(_MM)

        cn, rstd_c = _ln_fwd(cv_ref[...] + cvec_ref[0:1, :])
        ln = cn * cvec_ref[1:2, :] + cvec_ref[2:3, :]
        sg_l = _sigmoid(ln)
        swb = (ln * sg_l).astype(_MM)
        dycb = dy_ref[:, 256:512].astype(_MM)
        dwpw_ref[...] += _dot_tn(swb, dycb)
        dln = _dot_nt(dycb, wpw_ref[...]) * (sg_l * (1.0 + ln * (1.0 - sg_l)))
        dsm_ref[3:4, 0:256] += jnp.sum(dln * cn, axis=0, keepdims=True)
        dsm_ref[4:5, 0:256] += jnp.sum(dln, axis=0, keepdims=True)
        dcv = _ln_bwd(dln * cvec_ref[1:2, :], cn, rstd_c)
        dsm_ref[2:3, 0:256] += jnp.sum(dcv, axis=0, keepdims=True)
        dcvext[0:rb, :] = dcv
        _shifted_copies(uext, ucopies, 8, rb + _TAIL_U - 8, 1)
        _shifted_copies(dcvext, dcopies, 0, rb + _TAIL_U - 8, -1)
        for k in range(CONV_W):
            prod = dcv * _tap(uext, ucopies, k, _TAIL_U, rb, 1)
            dcdw_ref[CONV_W - 1 - k:CONV_W - k, :] += jnp.sum(prod, axis=0, keepdims=True)
        for r in range(0, rb, sr):
            acc = jnp.zeros((sr, D_CONV), jnp.float32)
            for k in range(CONV_W):
                acc = acc + _tap(dcvext, dcopies, k, r, sr, -1) * cdw_ref[CONV_W - 1 - k:CONV_W - k, :]
            cv[r:r + sr, :] = acc
        du = cv[...]
        dz_ref[:, 256:512] = jnp.where(live, du * sg_c, 0.0).astype(_MM)
        dz_ref[:, 512:768] = jnp.where(live, du * ca * sg_c * (1.0 - sg_c), 0.0).astype(_MM)

        csv = cs_ref[...]
        snv = sn_ref[...]
        for h in range(HEADS):
            q = _rope(z_ref[:, 768 + h * DH:768 + (h + 1) * DH], csv, snv)
            k = _rope(z_ref[:, 1280 + h * DH:1280 + (h + 1) * DH], csv, snv) * (DH ** -0.5)
            vb = z_ref[:, 1792 + h * DH:1792 + (h + 1) * DH].astype(_MM)
            g = z_ref[:, 2304 + h * DH:2304 + (h + 1) * DH]
            a, b = _row_decays(rb, h)
            qb = q.astype(_MM)
            kb = k.astype(_MM)
            qab = (q * a).astype(_MM)
            kbb = (k * b).astype(_MM)
            stb = st_ref[0, h].astype(_MM)
            sb = (_dot_nt(qb, kb) * wmask[h]).astype(_MM)
            o = _dot(sb, vb) + _dot(qab, stb)
            on, rstd_o = _ln_fwd(o)
            gnv = gn_ref[:, h * DH:(h + 1) * DH]
            sg_g = _sigmoid(g)
            si_g = g * sg_g
            dyr = dy_ref[:, 512 + h * DH:512 + (h + 1) * DH]
            dsm_ref[0:1, h * DH:(h + 1) * DH] += jnp.sum(dyr * on * si_g, axis=0, keepdims=True)
            dgate = dyr * on * gnv * (sg_g * (1.0 + g * (1.0 - sg_g)))
            dob = _ln_bwd(dyr * gnv * si_g, on, rstd_o).astype(_MM)
            dstb = dstate[h].astype(_MM)
            dsb = (_dot_nt(dob, vb) * wmask[h]).astype(_MM)
            dq = _dot(dsb, kb) + _dot_nt(dob, stb) * a
            dk = _dot_tn(dsb, qb) + _dot_nt(vb, dstb) * b
            dv = _dot_tn(sb, dob) + _dot(kbb, dstb)
            dstate[h] = math.exp(LOG_GAMMA[h] * rb) * dstate[h] + _dot_tn(qab, dob)
            dz_ref[:, 768 + h * DH:768 + (h + 1) * DH] = jnp.where(live, _rope_t(dq, csv, snv), 0.0).astype(_MM)
            dz_ref[:, 1280 + h * DH:1280 + (h + 1) * DH] = jnp.where(
                live, _rope_t(dk * (DH ** -0.5), csv, snv), 0.0).astype(_MM)
            dz_ref[:, 1792 + h * DH:1792 + (h + 1) * DH] = jnp.where(live, dv, 0.0).astype(_MM)
            dz_ref[:, 2304 + h * DH:2304 + (h + 1) * DH] = jnp.where(live, dgate, 0.0).astype(_MM)

        dcvext[rb:rb + _TAIL_U, :] = dcvext[0:_TAIL_U, :]
        eext[rb:rb + _TAIL_X, :] = eext[0:_TAIL_X, :]

    rrows = lambda n: pl.BlockSpec((rb, n), lambda i: (rev(i), 0))
    return pl.pallas_call(
        body, name=name, grid=(nblk,),
        in_specs=[rrows(D_IN), rrows(D), rrows(DH), rrows(DH),
                  pl.BlockSpec((1, HEADS, DH, DH), lambda i: (rev(i), 0, 0, 0)),
                  pl.BlockSpec((1, _TAIL_U, D_CONV), lambda i: (rev(i), 0, 0)),
                  rrows(D_CONV), rrows(D_POOL),
                  _full((256, 256)), _full((1, 256)), _full((32, 256)), _full((8, 256)), _full((256, 256)),
                  _full((1, D_RET))],
        out_specs=[rrows(D_IN), _acc((256, 256)), _acc((256, 256)), _acc((32, 256)), _acc((8, 512))],
        out_shape=[jax.ShapeDtypeStruct((t, D_IN), _MM),
                   jax.ShapeDtypeStruct((256, 256), jnp.float32), jax.ShapeDtypeStruct((256, 256), jnp.float32),
                   jax.ShapeDtypeStruct((32, 256), jnp.float32), jax.ShapeDtypeStruct((8, 512), jnp.float32)],
        scratch_shapes=[pltpu.VMEM((rb + _TAIL_U, D_CONV), jnp.float32),
                        pltpu.VMEM((rb, D_CONV), jnp.float32),
                        pltpu.VMEM((rb + _TAIL_U, D_CONV), jnp.float32),
                        pltpu.VMEM((rb + _TAIL_X, D_POOL), jnp.float32),
                        pltpu.VMEM((rb + _TAIL_X, D_POOL), jnp.float32),
                        pltpu.VMEM((rb + _TAIL_X, D_POOL), jnp.float32),
                        pltpu.VMEM((rb + _TAIL_X, D_POOL), jnp.float32),
                        pltpu.VMEM((7, rb + _TAIL_U, D_CONV), jnp.float32),
                        pltpu.VMEM((7, rb + _TAIL_U, D_CONV), jnp.float32),
                        pltpu.VMEM((HEADS, DH, DH), jnp.float32),
                        pltpu.VMEM((HEADS, rb, rb), jnp.float32)],
        compiler_params=_params(),
    )(z, dyc, cs, sn, st_in, ut_in, cv_in, yp_in, wbd, pscale, cdw, cvec, wpw, gn)


def _me():
    return lax.axis_index("x"), lax.axis_index("y"), lax.axis_index("c")


def _flip(me, mask):
    return tuple(1 - m if f else m for m, f in zip(me, mask))


def _push(name, aliased, inputs, fresh, remote):
    n_al, n_in, n_out, n_rem = len(aliased), len(inputs), len(fresh), len(remote)

    def body(*refs):
        ins = refs[n_al:n_al + n_in]
        al = refs[n_al + n_in:2 * n_al + n_in]
        outs = refs[2 * n_al + n_in:2 * n_al + n_in + n_out]
        send_sems, recv_sems = refs[2 * n_al + n_in + n_out:]
        me = _me()
        copies = []
        for k, (mask, src_fn, dst_fn) in enumerate(remote):
            cp = pltpu.make_async_remote_copy(
                src_ref=src_fn(al, ins, outs, me), dst_ref=dst_fn(al, ins, outs, me),
                send_sem=send_sems.at[k], recv_sem=recv_sems.at[k],
                device_id=_flip(me, mask), device_id_type=MESH)
            cp.start()
            copies.append(cp)
        for cp in copies:
            cp.wait()

    return pl.pallas_call(
        body, name=name,
        in_specs=[_ANY] * (n_al + n_in), out_specs=[_ANY] * (n_al + n_out),
        out_shape=[jax.ShapeDtypeStruct(a.shape, a.dtype) for a in aliased] + list(fresh),
        input_output_aliases={i: i for i in range(n_al)},
        scratch_shapes=[pltpu.SemaphoreType.DMA((n_rem,)), pltpu.SemaphoreType.DMA((n_rem,))],
    )(*aliased, *inputs)


_HBM = pl.BlockSpec(memory_space=pltpu.HBM)
_SEM = pl.BlockSpec(memory_space=pltpu.SEMAPHORE)
_EFFECT = pltpu.SideEffectType.DATAFLOW_SIDE_EFFECTING


def _push_start(name, bufs, remote):
    n, n_rem = len(bufs), len(remote)

    def body(*refs):
        ins = refs[:n]
        send_sems, recv_sems = refs[n], refs[n + 1]
        token = refs[2 * n + 2]
        me = _me()
        for k, (mask, src_fn, dst_fn) in enumerate(remote):
            pltpu.make_async_remote_copy(
                src_ref=src_fn(ins, me), dst_ref=dst_fn(ins, me),
                send_sem=send_sems.at[k], recv_sem=recv_sems.at[k],
                device_id=_flip(me, mask), device_id_type=MESH).start()
        token[...] = jnp.zeros_like(token)

    out = pl.pallas_call(
        body, name=name,
        out_shape=(pltpu.SemaphoreType.DMA((n_rem,)), pltpu.SemaphoreType.DMA((n_rem,)),
                   *[pltpu.HBM(b.shape, b.dtype) for b in bufs], jax.ShapeDtypeStruct((8, 128), jnp.float32)),
        in_specs=[_HBM] * n,
        out_specs=(_SEM, _SEM, *[_HBM] * n, pl.BlockSpec(memory_space=pltpu.VMEM)),
        input_output_aliases={i: i + 2 for i in range(n)},
        compiler_params=pltpu.CompilerParams(has_side_effects=_EFFECT),
    )(*[pltpu.with_memory_space_constraint(b, pltpu.HBM) for b in bufs])
    return out[0], out[1], list(out[2:2 + n]), out[2 + n]


def _push_wait(name, send_sems, recv_sems, bufs, after, remote):
    n = len(bufs)

    def body(*refs):
        ins = refs[:n]
        s_sems, r_sems = refs[n], refs[n + 1]
        me = _me()
        for k, (mask, src_fn, dst_fn) in enumerate(remote):
            cp = pltpu.make_async_remote_copy(
                src_ref=src_fn(ins, me), dst_ref=dst_fn(ins, me),
                send_sem=s_sems.at[k], recv_sem=r_sems.at[k],
                device_id=_flip(me, mask), device_id_type=MESH)
            cp.wait_send()
            cp.wait_recv()

    out = pl.pallas_call(
        body, name=name,
        out_shape=tuple(pltpu.HBM(b.shape, b.dtype) for b in bufs),
        in_specs=[_HBM] * n + [_SEM, _SEM, _ANY], out_specs=tuple([_HBM] * n),
        input_output_aliases={i: i for i in range(n)},
        compiler_params=pltpu.CompilerParams(has_side_effects=_EFFECT),
    )(*bufs, send_sems, recv_sems, after)
    return list(out)


_ICI_MASKS = ((0, 1, 0), (1, 0, 0), (1, 1, 0))
_D2D_MASK = (0, 0, 1)
_ALL_MASKS = tuple((a, b, c) for a in (0, 1) for b in (0, 1) for c in (0, 1))[1:]


def _chip(me):
    return 2 * me[0] + me[1]


def _half(me, rows):
    return pl.ds(me[2] * (rows // 2), rows // 2)


def _other_half(me, rows):
    return pl.ds((1 - me[2]) * (rows // 2), rows // 2)


def _sum_block(rh):
    return next((b for b in (768, 640, 512, 128) if rh % b == 0), rh)


def _own_slot(mine):
    chip = _chip(_me())
    return lax.dynamic_update_slice(lax.empty((N_SHARD,) + mine.shape, mine.dtype), mine[None],
                                    (chip,) + (0,) * mine.ndim)


def _gather_ici_plan(rows, with_small):
    remote = []
    for mask in _ICI_MASKS:
        for b, r in enumerate(rows):
            mine = lambda bufs, me, b=b, r=r: bufs[b].at[_chip(me), _half(me, r)]
            remote.append((mask, mine, mine))
        if with_small:
            mine_small = lambda bufs, me: bufs[len(rows)].at[_chip(me)]
            remote.append((mask, mine_small, mine_small))
    return remote


def _gather_d2d(name, ws):
    remote = []
    for b, w in enumerate(ws):
        for j in range(1, N_SHARD):
            theirs = lambda al, ins, outs, me, j=j, b=b, r=w.shape[1]: al[b].at[(_chip(me) + j) % N_SHARD, _half(me, r)]
            remote.append((_D2D_MASK, theirs, theirs))
    return _push(name, list(ws), [], [], remote)


def _sum_pair(name, g, recv):
    _, r, _ = g.shape
    rb = _sum_block(r // 2)
    nb = r // 2 // rb
    c = lax.axis_index("c").astype(jnp.int32).reshape(1)

    def body(c_ref, g_ref, r_ref, o_ref):
        o_ref[...] = (g_ref[...].astype(jnp.float32) + r_ref[...].astype(jnp.float32)).astype(o_ref.dtype)

    return pl.pallas_call(
        body, name=name,
        grid_spec=pltpu.PrefetchScalarGridSpec(
            num_scalar_prefetch=1, grid=(N_SHARD, nb),
            in_specs=[pl.BlockSpec((None, rb, D), lambda s, i, c_ref: (s, c_ref[0] * nb + i, 0)),
                      pl.BlockSpec((None, rb, D), lambda s, i, c_ref: (s, i, 0))],
            out_specs=pl.BlockSpec((None, rb, D), lambda s, i, c_ref: (s, i, 0))),
        out_shape=jax.ShapeDtypeStruct((N_SHARD, r // 2, D), g.dtype),
        compiler_params=_params(("arbitrary", "arbitrary")),
    )(c, g, recv)


def _sum_chips(name, p, recv):
    _, rh, _ = p.shape
    rb = _sum_block(rh)
    nb = rh // rb
    s = jnp.stack([2 * lax.axis_index("x") + lax.axis_index("y"), lax.axis_index("c")]).astype(jnp.int32)

    def body(s_ref, p_ref, r_ref, o_ref):
        acc = p_ref[...].astype(jnp.float32)
        for j in range(3):
            acc = acc + r_ref[j].astype(jnp.float32)
        o_ref[...] = acc

    return pl.pallas_call(
        body, name=name,
        grid_spec=pltpu.PrefetchScalarGridSpec(
            num_scalar_prefetch=1, grid=(nb,),
            in_specs=[pl.BlockSpec((None, rb, D), lambda i, s_ref: (s_ref[0], i, 0)),
                      pl.BlockSpec((3, rb, D), lambda i, s_ref: (0, i, 0))],
            out_specs=pl.BlockSpec((rb, D), lambda i, s_ref: (s_ref[1] * nb + i, 0))),
        out_shape=jax.ShapeDtypeStruct((2 * rh, D), jnp.float32),
        compiler_params=_params(),
    )(s, p, recv)


def _rs_ici_plan():
    remote = []
    for j, mask in enumerate(_ICI_MASKS):
        remote.append((mask,
                       lambda bufs, me, mask=mask: bufs[0].at[_chip(_flip(me, mask))],
                       lambda bufs, me, j=j: bufs[1].at[j]))
    return remote


def _rs_pair(tag, g):
    _, r, _ = g.shape
    remote = [(_D2D_MASK,
               lambda al, ins, outs, me, s=s: ins[0].at[s, _other_half(me, r)],
               lambda al, ins, outs, me, s=s: outs[0].at[s]) for s in range(N_SHARD)]
    (recv,) = _push("rs_d2d_" + tag, [], [g], [jax.ShapeDtypeStruct((N_SHARD, r // 2, D), g.dtype)], remote)
    return _sum_pair("rs_sum_pair_" + tag, g, recv)


def _rs_d2d_plan(r):
    return [(_D2D_MASK,
             lambda bufs, me, s=s: bufs[0].at[s, _other_half(me, r)],
             lambda bufs, me, s=s: bufs[1].at[s]) for s in range(N_SHARD)]


def _rs_d2d_start(tag, g):
    r = g.shape[1]
    landing = lax.empty((N_SHARD, r // 2, D), g.dtype)
    send, recv, flying, token = _push_start("rs_d2d_" + tag + "_start", [g, landing], _rs_d2d_plan(r))
    return (tag, r, send, recv, flying), token[0, 0]


def _rs_start_after(exchanged, after, rs_plan):
    tag, r, send, recv, flying = exchanged
    g, landed = _push_wait("rs_d2d_" + tag + "_wait", send, recv, flying, after, _rs_d2d_plan(r))
    p = _sum_pair("rs_sum_pair_" + tag, g, landed)
    landing = lax.empty((3, p.shape[1], D), p.dtype)
    send, recv, flying, token = _push_start("rs_ici_" + tag + "_start", [p, landing], rs_plan)
    return (tag, send, recv, flying), token[0, 0]


def _rs_start(tag, g, rs_plan):
    p = _rs_pair(tag, g)
    landing = lax.empty((3, p.shape[1], D), p.dtype)
    send, recv, flying, token = _push_start("rs_ici_" + tag + "_start", [p, landing], rs_plan)
    return (tag, send, recv, flying), token[0, 0]


def _rs_end(started, after, rs_plan):
    tag, send, recv, flying = started
    p, recv3 = _push_wait("rs_ici_" + tag + "_wait", send, recv, flying, after, rs_plan)
    return _sum_chips("rs_sum_chips_" + tag, p, recv3)


def _rs_share(mines):
    remote = []
    for b, m in enumerate(mines):
        half = lambda al, ins, outs, me, b=b, r=m.shape[0]: al[b].at[_half(me, r)]
        remote.append((_D2D_MASK, half, half))
    return _push("rs_share", list(mines), [], [], remote)


def _small_plan():
    slot = lambda bufs, me: bufs[0].at[4 * me[0] + 2 * me[1] + me[2]]
    return [(mask, slot, slot) for mask in _ALL_MASKS]


def _small_start(v):
    me = _me()
    every = lax.dynamic_update_slice(lax.empty((8,) + v.shape, jnp.float32), v[None],
                                     (4 * me[0] + 2 * me[1] + me[2], 0, 0))
    send, recv, flying, token = _push_start("small_all_start", [every], _small_plan())
    return (send, recv, flying), token


def _small_end(started, after):
    send, recv, flying = started
    (every,) = _push_wait("small_all_wait", send, recv, flying, after, _small_plan())
    s = every.shape[1]

    def body(e_ref, o_ref):
        acc = e_ref[0]
        for j in range(1, 8):
            acc = acc + e_ref[j]
        o_ref[...] = acc

    return pl.pallas_call(
        body, name="small_sum", grid=(1,),
        in_specs=[pl.BlockSpec((8, s, D), lambda i: (0, 0, 0))],
        out_specs=pl.BlockSpec((s, D), lambda i: (0, 0)),
        out_shape=jax.ShapeDtypeStruct((s, D), jnp.float32),
        compiler_params=_params(),
    )(every)


def _adamw(name, w, g, m, v):
    r, c = w.shape
    rb = next(b for b in (256, 344, 128, 64, 32, 16, 8, r) if r % b == 0)

    def body(w_ref, g_ref, m_ref, v_ref, d_ref, mo_ref, vo_ref):
        g = g_ref[...]
        m = ADAM_B1 * m_ref[...] + (1.0 - ADAM_B1) * g
        v = ADAM_B2 * v_ref[...] + (1.0 - ADAM_B2) * (g * g)
        m_hat = m / (1.0 - ADAM_B1 ** ADAM_STEP)
        v_hat = v / (1.0 - ADAM_B2 ** ADAM_STEP)
        d_ref[...] = -ADAM_LR * (m_hat / (jnp.sqrt(v_hat) + ADAM_EPS) + ADAM_WD * w_ref[...])
        mo_ref[...] = m
        vo_ref[...] = v

    spec = pl.BlockSpec((rb, c), lambda i: (i, 0))
    return pl.pallas_call(
        body, name=name, grid=(r // rb,),
        in_specs=[spec] * 4, out_specs=[spec] * 3,
        out_shape=[jax.ShapeDtypeStruct((r, c), jnp.float32)] * 3,
        compiler_params=_params(),
    )(w, g, m, v)


_BIG = ("ffn1_w13", "ffn2_w13", "ffn1_w2", "ffn2_w2", "w_in", "w_out", "conv_pw")
_BIG_SHARD = {"ffn1_w13": (D, 1376), "ffn2_w13": (D, 1376), "ffn1_w2": (688, D), "ffn2_w2": (688, D),
              "w_in": (D, 704), "w_out": (256, D), "conv_pw": (64, 256)}


def _pack_rows(parts):
    flat = jnp.concatenate([p.reshape(-1) for p in parts])
    pad = (-flat.shape[0]) % (8 * D)
    if pad:
        flat = jnp.concatenate([flat, jnp.zeros((pad,), flat.dtype)])
    return flat.reshape(-1, D)


def _unpack_rows(buf, shapes):
    flat = buf.reshape(-1)
    out, off = [], 0
    for shp in shapes:
        n = math.prod(shp)
        out.append(flat[off:off + n].reshape(shp))
        off += n
    return out


def _pack_shard(parts):
    zeros = lambda n: jnp.zeros((n, D), parts["w_out"].dtype)
    a = jnp.concatenate([parts["ffn1_w13"].T, parts["ffn1_w2"], zeros(A_ROWS - OFF_WIN)], axis=0)
    b = jnp.concatenate([parts["ffn2_w13"].T, parts["ffn2_w2"], parts["w_in"].T, parts["w_out"],
                         parts["conv_pw"].reshape(PW_ROWS, D), zeros(B_ROWS - OFF_PW - PW_ROWS)], axis=0)
    return a, b


def _unpack_shard(a, b):
    return {"ffn1_w13": a[OFF_W13:OFF_W13 + W13_ROWS].T, "ffn1_w2": a[OFF_W2:OFF_W2 + W2_ROWS],
            "ffn2_w13": b[OFF_W13:OFF_W13 + W13_ROWS].T, "ffn2_w2": b[OFF_W2:OFF_W2 + W2_ROWS],
            "w_in": b[OFF_WIN:OFF_WIN + WIN_ROWS].T, "w_out": b[OFF_WOUT:OFF_WOUT + WOUT_ROWS],
            "conv_pw": b[OFF_PW:OFF_PW + PW_ROWS].reshape(64, 256)}


def kernel(x, meta, ln_in_g, ln_in_b, ffn1_w13, ffn1_w2, w_in, pool_w, pool_scale, conv_dw, conv_db, conv_ln_g, conv_ln_b, conv_pw, ret_gn_g, w_out, ffn2_w13, ffn2_w2, ln_g, ln_b, loss_target, m_meta, m_ln_in_g, m_ln_in_b, m_ffn1_w13, m_ffn1_w2, m_w_in, m_pool_w, m_pool_scale, m_conv_dw, m_conv_db, m_conv_ln_g, m_conv_ln_b, m_conv_pw, m_ret_gn_g, m_w_out, m_ffn2_w13, m_ffn2_w2, m_ln_g, m_ln_b, v_meta, v_ln_in_g, v_ln_in_b, v_ffn1_w13, v_ffn1_w2, v_w_in, v_pool_w, v_pool_scale, v_conv_dw, v_conv_db, v_conv_ln_g, v_conv_ln_b, v_conv_pw, v_ret_gn_g, v_w_out, v_ffn2_w13, v_ffn2_w2, v_ln_g, v_ln_b):
    f32 = jnp.float32
    seq = x.shape[1]
    t = seq + ROW0
    me = _me()
    chip = _chip(me)
    big_w = {"ffn1_w13": ffn1_w13, "ffn2_w13": ffn2_w13, "ffn1_w2": ffn1_w2, "ffn2_w2": ffn2_w2,
             "w_in": w_in, "w_out": w_out, "conv_pw": conv_pw}

    wa, wb = [], []
    for l in range(DEPTH):
        a, b = _pack_shard({n: big_w[n][l].astype(_WIRE) for n in _BIG})
        wa.append(_own_slot(a))
        wb.append(_own_slot(b))
    small_shapes = [(N_META, 256), (DEPTH, CONV_W, 64), (DEPTH, 3, 256), (DEPTH, 3, 256)]
    small_all = _own_slot(_pack_rows([meta, conv_dw, ln_g, ln_b]))
    wrap = lambda f: (lambda al, ins, outs, me: f(al, me))
    (small_all,) = _push("gather_small", [small_all], [], [],
                         [(m, wrap(s), wrap(d)) for m, s, d in _gather_ici_plan([], True)])
    plan_a0 = _gather_ici_plan([A_ROWS], False)
    plan_b0 = _gather_ici_plan([B_ROWS], False)
    plan_l1 = _gather_ici_plan([A_ROWS, B_ROWS], False)
    a0_send, a0_recv, a0_flying, a0_token = _push_start("gather_ici_a0_start", [wa[0]], plan_a0)

    sm = [_unpack_rows(small_all[s], small_shapes) for s in range(N_SHARD)]
    meta_f = jnp.concatenate([sm[s][0] for s in range(N_SHARD)], axis=1) + a0_token[0, 0]
    cdw_f = jnp.concatenate([sm[s][1] for s in range(N_SHARD)], axis=2)
    lng_f = jnp.concatenate([sm[s][2] for s in range(N_SHARD)], axis=2)
    lnb_f = jnp.concatenate([sm[s][3] for s in range(N_SHARD)], axis=2)

    def mix_params(l):
        wbd = jnp.zeros((D_POOL, D_POOL), f32)
        for g in range(4):
            wbd = wbd.at[64 * g:64 * (g + 1), 64 * g:64 * (g + 1)].set(pool_w[l, g])
        cdw = jnp.pad(cdw_f[l], ((0, 1), (0, 0)))
        cvec = jnp.pad(jnp.stack([conv_db[l], conv_ln_g[l], conv_ln_b[l]]), ((0, 5), (0, 0)))
        wpw = wb[l][:, OFF_PW:OFF_PW + PW_ROWS].reshape(D_CONV, D_CONV)
        return (wbd.astype(_MM), pool_scale[l][None], cdw, cvec, wpw, ret_gn_g[l][None])

    gb_of = lambda l, i: jnp.stack([lng_f[l, i], lnb_f[l, i]])
    gb_in = jnp.stack([ln_in_g, ln_in_b])

    pos = jnp.arange(t, dtype=f32) - PAD
    inv_freq = ROPE_BASE ** (-jnp.arange(0, DH, 2, dtype=f32) / DH)
    ang = pos[:, None] * inv_freq[None, :]
    cs = jnp.concatenate([jnp.cos(ang), jnp.cos(ang)], axis=1)
    sn = jnp.concatenate([-jnp.sin(ang), jnp.sin(ang)], axis=1)

    xh, rstd = _ln_in_fwd(jnp.concatenate([jnp.zeros((PAD, D), f32), meta_f], axis=0), x[0])
    (wa[0],) = _push_wait("gather_ici_a0_wait", a0_send, a0_recv, a0_flying, xh, plan_a0)
    (wa[0],) = _gather_d2d("gather_d2d_a0", [wa[0]])
    b0_send, b0_recv, b0_flying, b0_token = _push_start("gather_ici_b0_start", [wb[0]], plan_b0)
    cur = (xh, rstd, gb_in + b0_token[0, 0])
    saved = []
    for l in range(DEPTH):
        if l == 1:
            wa[1], wb[1] = _push_wait("gather_ici_l1_wait", l1_send, l1_recv, l1_flying, cur[0], plan_l1)
            wa[1], wb[1] = _gather_d2d("gather_d2d_l1", [wa[1], wb[1]])
        a0 = cur
        xh1, r1, au1, hb1 = _ffn_fwd(f"ffn1_fwd_{l}", a0[0], a0[2], wa[l])
        a1 = (xh1, r1, gb_of(l, 0))
        if l == 0:
            (wb[0],) = _push_wait("gather_ici_b0_wait", b0_send, b0_recv, b0_flying, xh1, plan_b0)
            (wb[0],) = _gather_d2d("gather_d2d_b0", [wb[0]])
            l1_send, l1_recv, l1_flying, l1_token = _push_start("gather_ici_l1_start", [wa[1], wb[1]], plan_l1)
            a1 = (xh1, r1, a1[2] + l1_token[0, 0])
        z = _mix_in_fwd(f"mix_in_fwd_{l}", a1[0], a1[2], wb[l])
        mp = mix_params(l)
        ycat, st_in, ut_in, cv_in, yp_in = _mix_core_fwd(f"mix_core_fwd_{l}", z, cs, sn, *mp)
        xt_in = (cv_in, yp_in)
        xh2, r2 = _mix_out_fwd(f"mix_out_fwd_{l}", a1[0], a1[2], ycat, wb[l])
        a2 = (xh2, r2, gb_of(l, 1))
        xh3, r3, au2, hb2 = _ffn_fwd(f"ffn2_fwd_{l}", a2[0], a2[2], wb[l])
        a3 = (xh3, r3, gb_of(l, 2))
        saved.append((a0, a1, a2, a3, z, ycat, st_in, ut_in, xt_in, mp, au1, hb1, au2, hb2))
        cur = a3

    dy, loss_part = _loss_fwd_bwd(cur[0], cur[2], loss_target[0])
    loss = lax.psum(loss_part[0, 0], ("x", "y", "c"))

    g_ln_g = [[None] * 3 for _ in range(DEPTH)]
    g_ln_b = [[None] * 3 for _ in range(DEPTH)]
    g_small = [dict() for _ in range(DEPTH)]
    slot = lambda j: j

    def ffn_grads(gbuf, tag, l, hb, hid, dau, dffn):
        gbuf = _dw_into(f"dw13_{tag}_{l}", gbuf, dau, hb, FF_SLOT, [(0, W13_ROWS, slot, OFF_W13)])
        return _dw_into(f"dw2_{tag}_{l}", gbuf, hid, dffn, FF_SLOT,
                        [(0, W2_ROWS, lambda j: 2 * j, OFF_W2), (W2_ROWS, W2_ROWS, lambda j: 2 * j + 1, OFF_W2)])

    rs_plan = _rs_ici_plan()
    token = jnp.zeros((), f32)
    started, exchanging_a = {}, None
    for l in reversed(range(DEPTH)):
        a0, a1, a2, a3, z, ycat, st_in, ut_in, xt_in, mp, au1, hb1, au2, hb2 = saved[l]
        g_a = lax.empty((N_SHARD, A_ROWS, D), _WIRE)
        g_b = lax.empty((N_SHARD, B_ROWS, D), _WIRE)
        dh, hid, dau, dffn, dgb = _ffn_bwd(f"ffn2_bwd_{l}", dy, a3[0], a3[1], a3[2] + token, au2, wb[l])
        g_ln_g[l][2], g_ln_b[l][2] = dgb[0], dgb[1]
        g_b = ffn_grads(g_b, "ffn2", l, hb2, hid, dau, dffn)
        token = jnp.zeros((), f32)
        if exchanging_a is not None:
            started["a", l + 1], token = _rs_start_after(exchanging_a, dh, rs_plan)
        dh_res, dycat, dsb, dgb = _mix_out_bwd(f"mix_out_bwd_{l}", dh, a2[0], a2[1], a2[2] + token, wb[l])
        g_ln_g[l][1], g_ln_b[l][1] = dgb[0], dgb[1]
        g_b = _dw_into(f"dw_out_{l}", g_b, ycat, dsb, D,
                       [(WOUT_ROWS * s, WOUT_ROWS, lambda j, s=s: s, OFF_WOUT) for s in range(N_SHARD)])
        dz, dwbd, dwpw, dcdw, dsm = _mix_core_bwd(f"mix_core_bwd_{l}", z, dycat, cs, sn, st_in, ut_in, *xt_in, *mp)
        pw = jnp.concatenate([dwpw.astype(_WIRE).reshape(N_SHARD, PW_ROWS, D),
                              jnp.zeros((N_SHARD, B_ROWS - OFF_PW - PW_ROWS, D), _WIRE)], axis=1)
        g_b = lax.dynamic_update_slice(g_b, pw, (0, OFF_PW, 0))
        g_small[l] = dict(
            pool_w=jnp.stack([dwbd[64 * g:64 * (g + 1), 64 * g:64 * (g + 1)] for g in range(4)]),
            pool_scale=dsm[1, :256], conv_db=dsm[2, :256], conv_ln_g=dsm[3, :256], conv_ln_b=dsm[4, :256],
            ret_gn_g=dsm[0], conv_dw=dcdw[:CONV_W])
        dh, hb = _mix_in_bwd(f"mix_in_bwd_{l}", dh_res, dz, a1[0], a1[2], wb[l])
        g_b = _dw_into(f"dw_in_{l}", g_b, dz, hb, D_IN,
                       [(WIN_ROWS * s, WIN_ROWS, lambda j, s=s: s, OFF_WIN) for s in range(N_SHARD)])
        exchanging_b, token = _rs_d2d_start(f"b{l}", g_b)
        dh, hid, dau, dffn, dgb = _ffn_bwd(f"ffn1_bwd_{l}", dh, a1[0], a1[1], a1[2] + token, au1, wa[l])
        g_ln_g[l][0], g_ln_b[l][0] = dgb[0], dgb[1]
        started["b", l], token = _rs_start_after(exchanging_b, dh, rs_plan)
        zero_rows = jnp.zeros((N_SHARD, A_ROWS - OFF_WIN, D), _WIRE) + token.astype(_WIRE)
        g_a = lax.dynamic_update_slice(g_a, zero_rows, (0, OFF_WIN, 0))
        g_a = ffn_grads(g_a, "ffn1", l, hb1, hid, dau, dffn)
        dy = dh
        if l > 0:
            exchanging_a, token = _rs_d2d_start(f"a{l}", g_a)
        else:
            started["a", l], token = _rs_start(f"a{l}", g_a, rs_plan)
    d_frames, d_head, dgb_in = _ln_in_bwd(dy, saved[0][0][0], saved[0][0][1], gb_in + token)
    grad_x = d_frames[None]

    small_parts = [
        d_head[PAD:ROW0],
        jnp.stack([g_small[l]["conv_dw"] for l in range(DEPTH)]),
        jnp.stack([jnp.stack(g_ln_g[l]) for l in range(DEPTH)]),
        jnp.stack([jnp.stack(g_ln_b[l]) for l in range(DEPTH)]),
        dgb_in[0], dgb_in[1],
        jnp.stack([g_small[l]["pool_w"] for l in range(DEPTH)]),
        jnp.stack([g_small[l]["pool_scale"] for l in range(DEPTH)]),
        jnp.stack([g_small[l]["conv_db"] for l in range(DEPTH)]),
        jnp.stack([g_small[l]["conv_ln_g"] for l in range(DEPTH)]),
        jnp.stack([g_small[l]["conv_ln_b"] for l in range(DEPTH)]),
        jnp.stack([g_small[l]["ret_gn_g"] for l in range(DEPTH)]),
    ]
    small_started, small_token = _small_start(_pack_rows(small_parts))

    keys, mines, after = (("b", 1), ("a", 1), ("b", 0), ("a", 0)), [], small_token
    for key in keys:
        after = _rs_end(started[key], after, rs_plan)
        mines.append(after)
    gsum = dict(zip(keys, _rs_share(mines)))
    g_big = [_unpack_shard(gsum["a", l], gsum["b", l]) for l in range(DEPTH)]
    grads = {n: jnp.stack([g_big[l][n] for l in range(DEPTH)]) for n in _BIG}
    red = _unpack_rows(_small_end(small_started, gsum[keys[-1]]), [p.shape for p in small_parts])
    grads["meta"] = lax.dynamic_slice_in_dim(red[0], 256 * chip, 256, axis=1)
    grads["conv_dw"] = lax.dynamic_slice_in_dim(red[1], 64 * chip, 64, axis=2)
    grads["ln_g"] = lax.dynamic_slice_in_dim(red[2], 256 * chip, 256, axis=2)
    grads["ln_b"] = lax.dynamic_slice_in_dim(red[3], 256 * chip, 256, axis=2)
    for n, v in zip(("ln_in_g", "ln_in_b", "pool_w", "pool_scale", "conv_db", "conv_ln_g", "conv_ln_b", "ret_gn_g"),
                    red[4:]):
        grads[n] = v

    names = ['meta', 'ln_in_g', 'ln_in_b', 'ffn1_w13', 'ffn1_w2', 'w_in', 'pool_w', 'pool_scale', 'conv_dw',
             'conv_db', 'conv_ln_g', 'conv_ln_b', 'conv_pw', 'ret_gn_g', 'w_out', 'ffn2_w13', 'ffn2_w2', 'ln_g', 'ln_b']
    ws = dict(meta=meta, ln_in_g=ln_in_g, ln_in_b=ln_in_b, ffn1_w13=ffn1_w13, ffn1_w2=ffn1_w2, w_in=w_in,
              pool_w=pool_w, pool_scale=pool_scale, conv_dw=conv_dw, conv_db=conv_db, conv_ln_g=conv_ln_g,
              conv_ln_b=conv_ln_b, conv_pw=conv_pw, ret_gn_g=ret_gn_g, w_out=w_out, ffn2_w13=ffn2_w13,
              ffn2_w2=ffn2_w2, ln_g=ln_g, ln_b=ln_b)
    ms = dict(meta=m_meta, ln_in_g=m_ln_in_g, ln_in_b=m_ln_in_b, ffn1_w13=m_ffn1_w13, ffn1_w2=m_ffn1_w2,
              w_in=m_w_in, pool_w=m_pool_w, pool_scale=m_pool_scale, conv_dw=m_conv_dw, conv_db=m_conv_db,
              conv_ln_g=m_conv_ln_g, conv_ln_b=m_conv_ln_b, conv_pw=m_conv_pw, ret_gn_g=m_ret_gn_g,
              w_out=m_w_out, ffn2_w13=m_ffn2_w13, ffn2_w2=m_ffn2_w2, ln_g=m_ln_g, ln_b=m_ln_b)
    vs = dict(meta=v_meta, ln_in_g=v_ln_in_g, ln_in_b=v_ln_in_b, ffn1_w13=v_ffn1_w13, ffn1_w2=v_ffn1_w2,
              w_in=v_w_in, pool_w=v_pool_w, pool_scale=v_pool_scale, conv_dw=v_conv_dw, conv_db=v_conv_db,
              conv_ln_g=v_conv_ln_g, conv_ln_b=v_conv_ln_b, conv_pw=v_conv_pw, ret_gn_g=v_ret_gn_g,
              w_out=v_w_out, ffn2_w13=v_ffn2_w13, ffn2_w2=v_ffn2_w2, ln_g=v_ln_g, ln_b=v_ln_b)
    delta, new_m, new_v = {}, {}, {}
    for n in _BIG:
        shp = ws[n].shape
        two = lambda a: a.reshape(-1, shp[-1])
        d_, m_, v_ = _adamw("adamw_" + n, two(ws[n]), two(grads[n]), two(ms[n]), two(vs[n]))
        delta[n], new_m[n], new_v[n] = d_.reshape(shp), m_.reshape(shp), v_.reshape(shp)
    small_names = [n for n in names if n not in _BIG]
    pk = lambda d: _pack_rows([d[n] for n in small_names])
    d_, m_, v_ = _adamw("adamw_small", pk(ws), pk(grads), pk(ms), pk(vs))
    shapes = [ws[n].shape for n in small_names]
    for n, a, b, c in zip(small_names, _unpack_rows(d_, shapes), _unpack_rows(m_, shapes), _unpack_rows(v_, shapes)):
        delta[n], new_m[n], new_v[n] = a, b, c

    return (loss, grad_x, *[grads[n] for n in names], *[delta[n] for n in names],
            *[new_m[n] for n in names], *[new_v[n] for n in names])
```

```python
import math

import jax
import jax.numpy as jnp
from jax import lax
from jax.experimental import pallas as pl
from jax.experimental.pallas import tpu as pltpu

D = 1024
DEPTH = 2
N_META = 16
PAD = 112
ROW0 = PAD + N_META
D_POOL = 256
D_CONV = 256
D_RET = 512
HEADS = 4
DH = 128
CONV_W = 31
FF_SLOT = 1408
D_FFP = 2 * FF_SLOT
D_IN = 2816
N_SHARD = 4
ALPHA = (2.0 * DEPTH) ** 0.25
LN_EPS = 1e-5
ROPE_BASE = 10000.0
LOG_GAMMA = tuple(math.log(1.0 - 2.0 ** (-5.0 - h)) for h in range(HEADS))
ADAM_LR, ADAM_B1, ADAM_B2, ADAM_EPS, ADAM_WD, ADAM_STEP = 0.001, 0.9, 0.999, 1e-08, 0.01, 10

_MM = jnp.bfloat16
_WIRE = jnp.bfloat16
_VMEM_LIMIT = 56 * 1024 * 1024
_FFN_ROWS = 640

MESH = pl.DeviceIdType.MESH
_ANY = pl.BlockSpec(memory_space=pl.ANY)

W13_ROWS = 1376
W2_ROWS = 688
WIN_ROWS = 704
WOUT_ROWS = 256
PW_ROWS = 16
OFF_W13 = 0
OFF_W2 = W13_ROWS
OFF_WIN = W13_ROWS + W2_ROWS
OFF_WOUT = OFF_WIN + WIN_ROWS
OFF_PW = OFF_WOUT + WOUT_ROWS
A_ROWS = 2080
B_ROWS = 3072
W2_SLOT_OFF = (0, W2_ROWS, FF_SLOT, FF_SLOT + W2_ROWS)


def _row_copies(w_ref, off, n, dst_of, sems, k0):
    return [pltpu.make_async_copy(w_ref.at[s, pl.ds(off, n)], dst_of(s), sems.at[k0 + s]) for s in range(N_SHARD)]


def _load_ffn_weights(w_ref, w13, w2, sems):
    cps = _row_copies(w_ref, OFF_W13, W13_ROWS, lambda s: w13.at[s, pl.ds(0, W13_ROWS)], sems, 0)
    cps += _row_copies(w_ref, OFF_W2, W2_ROWS, lambda s: w2.at[pl.ds(W2_SLOT_OFF[s], W2_ROWS)], sems, 4)
    for cp in cps:
        cp.start()
    zpad = jnp.zeros((FF_SLOT - W13_ROWS, D), w13.dtype)
    for s in range(N_SHARD):
        w13[s, W13_ROWS:FF_SLOT, :] = zpad
    w2[W13_ROWS:FF_SLOT, :] = zpad
    w2[FF_SLOT + W13_ROWS:D_FFP, :] = zpad
    for cp in cps:
        cp.wait()


def _load_rows(w_ref, off, n, dst, sems):
    cps = _row_copies(w_ref, off, n, lambda s: dst.at[pl.ds(s * n, n)], sems, 0)
    for cp in cps:
        cp.start()
    for cp in cps:
        cp.wait()


def _dot(a, b):
    return jnp.dot(a, b, preferred_element_type=jnp.float32)


def _dot_nt(a, b):
    return lax.dot_general(a, b, (((1,), (1,)), ((), ())), preferred_element_type=jnp.float32)


def _dot_tn(a, b):
    return lax.dot_general(a, b, (((0,), (0,)), ((), ())), preferred_element_type=jnp.float32)


def _params(sem=("arbitrary",)):
    return pltpu.CompilerParams(dimension_semantics=sem, vmem_limit_bytes=_VMEM_LIMIT)


def _row_block(t, cap=640):
    for rb in (640, 320, 128):
        if rb <= cap and t % rb == 0 and (t > 1024 or rb == 128):
            return rb
    raise ValueError(t)


def _rows(rb, n):
    return pl.BlockSpec((rb, n), lambda i: (i, 0))


def _full(shape):
    nd = len(shape)
    return pl.BlockSpec(tuple(shape), lambda i: (0,) * nd, pipeline_mode=pl.Buffered(1))


def _acc(shape):
    nd = len(shape)
    return pl.BlockSpec(tuple(shape), lambda i: (0,) * nd)


def _sigmoid(x):
    return 1.0 / (1.0 + jnp.exp(-x))


def _ln_fwd(s):
    mu = jnp.mean(s, axis=-1, keepdims=True)
    xc = s - mu
    var = jnp.mean(xc * xc, axis=-1, keepdims=True)
    rstd = lax.rsqrt(var + LN_EPS)
    return xc * rstd, rstd


def _ln_bwd(dxh, xh, rstd):
    m1 = jnp.mean(dxh, axis=-1, keepdims=True)
    m2 = jnp.mean(dxh * xh, axis=-1, keepdims=True)
    return rstd * (dxh - m1 - xh * m2)


def _frames(n):
    return pl.BlockSpec((ROW0, n), lambda i: (jnp.maximum(i - 1, 0), 0))


def _frame_parts(rb, n, nblk):
    per = rb // ROW0
    return [pl.BlockSpec((ROW0, n), lambda i, k=k: (jnp.clip(i * per - 1 + k, 0, nblk - 1), 0)) for k in range(per)]


def _join_parts(head, parts):
    first = jnp.where(pl.program_id(0) == 0, head, parts[0][...])
    return jnp.concatenate([first] + [p[...] for p in parts[1:]], axis=0)


def _ln_in_fwd(head, x2d):
    t = x2d.shape[0] + ROW0
    rb = _row_block(t)
    per = rb // ROW0

    def body(head_ref, *refs):
        xh_ref, rstd_ref = refs[per:]
        xh, rstd = _ln_fwd(_join_parts(head_ref[...], refs[:per]))
        xh_ref[...] = xh
        rstd_ref[...] = rstd

    return pl.pallas_call(
        body, name="ln_in_fwd", grid=(t // rb,),
        in_specs=[_full((ROW0, D))] + _frame_parts(rb, D, x2d.shape[0] // ROW0),
        out_specs=[_rows(rb, D), _rows(rb, 1)],
        out_shape=[jax.ShapeDtypeStruct((t, D), jnp.float32), jax.ShapeDtypeStruct((t, 1), jnp.float32)],
        compiler_params=_params(),
    )(head, *[x2d] * per)


def _ffn_fwd(name, xh, gb, wfull):
    t = xh.shape[0]
    rb = _row_block(t, _FFN_ROWS)

    def body(xh_ref, gb_ref, w_ref, out_ref, rstd_ref, au_ref, hb_ref, w13, w2, sems):
        @pl.when(pl.program_id(0) == 0)
        def _():
            _load_ffn_weights(w_ref, w13, w2, sems)

        h = xh_ref[...] * gb_ref[0:1, :] + gb_ref[1:2, :]
        hb = h.astype(_MM)
        hb_ref[...] = hb
        acc = jnp.zeros((rb, D), jnp.float32)
        for j in range(2):
            lo = j * FF_SLOT
            a = _dot_nt(hb, w13[j])
            u = _dot_nt(hb, w13[2 + j])
            au_ref[:, lo:lo + FF_SLOT] = a.astype(_MM)
            au_ref[:, D_FFP + lo:D_FFP + lo + FF_SLOT] = u.astype(_MM)
            hid = (a * _sigmoid(a) * u).astype(_MM)
            acc = acc + _dot(hid, w2[lo:lo + FF_SLOT, :])
        xo, rstd = _ln_fwd(ALPHA * h + 0.5 * acc)
        out_ref[...] = xo
        rstd_ref[...] = rstd

    return pl.pallas_call(
        body, name=name, grid=(t // rb,),
        in_specs=[_rows(rb, D), _full((2, D)), _ANY],
        out_specs=[_rows(rb, D), _rows(rb, 1), _rows(rb, 2 * D_FFP), _rows(rb, D)],
        out_shape=[jax.ShapeDtypeStruct((t, D), jnp.float32), jax.ShapeDtypeStruct((t, 1), jnp.float32),
                   jax.ShapeDtypeStruct((t, 2 * D_FFP), _MM), jax.ShapeDtypeStruct((t, D), _MM)],
        scratch_shapes=[pltpu.VMEM((N_SHARD, FF_SLOT, D), _MM), pltpu.VMEM((D_FFP, D), _MM),
                        pltpu.SemaphoreType.DMA((8,))],
        compiler_params=_params(),
    )(xh, gb, wfull)


def _ffn_bwd(name, dy, xo, rstd, gb_out, au, wfull):
    t = xo.shape[0]
    rb = _row_block(t, 320)

    def body(dy_ref, xo_ref, rstd_ref, gbo_ref, au_ref, w_ref,
             dh_ref, hid_ref, dau_ref, dffn_ref, dgb_ref, w13, w2, sems):
        i = pl.program_id(0)

        @pl.when(i == 0)
        def _():
            dgb_ref[...] = jnp.zeros_like(dgb_ref)
            _load_ffn_weights(w_ref, w13, w2, sems)

        dy = dy_ref[...]
        xo = xo_ref[...]
        dgb_ref[0:1, :] += jnp.sum(dy * xo, axis=0, keepdims=True)
        dgb_ref[1:2, :] += jnp.sum(dy, axis=0, keepdims=True)
        ds = _ln_bwd(dy * gbo_ref[0:1, :], xo, rstd_ref[...])
        dffn = (0.5 * ds).astype(_MM)
        dffn_ref[...] = dffn
        dh = ALPHA * ds
        for j in range(2):
            lo = j * FF_SLOT
            a = au_ref[:, lo:lo + FF_SLOT].astype(jnp.float32)
            u = au_ref[:, D_FFP + lo:D_FFP + lo + FF_SLOT].astype(jnp.float32)
            sg = _sigmoid(a)
            si = a * sg
            hid_ref[:, lo:lo + FF_SLOT] = (si * u).astype(_MM)
            dhid = _dot_nt(dffn, w2[lo:lo + FF_SLOT, :])
            da = (dhid * u * (sg * (1.0 + a * (1.0 - sg)))).astype(_MM)
            du = (dhid * si).astype(_MM)
            dau_ref[:, lo:lo + FF_SLOT] = da
            dau_ref[:, D_FFP + lo:D_FFP + lo + FF_SLOT] = du
            dh = dh + _dot(da, w13[j]) + _dot(du, w13[2 + j])
        dh_ref[...] = dh

    return pl.pallas_call(
        body, name=name, grid=(t // rb,),
        in_specs=[_rows(rb, D), _rows(rb, D), _rows(rb, 1), _full((2, D)), _rows(rb, 2 * D_FFP), _ANY],
        out_specs=[_rows(rb, D), _rows(rb, D_FFP), _rows(rb, 2 * D_FFP), _rows(rb, D), _acc((8, D))],
        out_shape=[jax.ShapeDtypeStruct((t, D), jnp.float32),
                   jax.ShapeDtypeStruct((t, D_FFP), _MM), jax.ShapeDtypeStruct((t, 2 * D_FFP), _MM),
                   jax.ShapeDtypeStruct((t, D), _MM), jax.ShapeDtypeStruct((8, D), jnp.float32)],
        scratch_shapes=[pltpu.VMEM((N_SHARD, FF_SLOT, D), _MM), pltpu.VMEM((D_FFP, D), _MM),
                        pltpu.SemaphoreType.DMA((8,))],
        compiler_params=_params(),
    )(dy, xo, rstd, gb_out, au, wfull)


def _mix_in_fwd(name, xh, gb, wfull):
    t = xh.shape[0]
    rb = _row_block(t)

    def body(xh_ref, gb_ref, w_ref, z_ref, wt, sems):
        @pl.when(pl.program_id(0) == 0)
        def _():
            _load_rows(w_ref, OFF_WIN, WIN_ROWS, wt, sems)

        h = xh_ref[...] * gb_ref[0:1, :] + gb_ref[1:2, :]
        z = _dot_nt(h.astype(_MM), wt[...])
        row = pl.program_id(0) * rb + lax.broadcasted_iota(jnp.int32, (rb, 1), 0)
        z_ref[...] = jnp.where(row >= PAD, z, 0.0)

    return pl.pallas_call(
        body, name=name, grid=(t // rb,),
        in_specs=[_rows(rb, D), _full((2, D)), _ANY],
        out_specs=_rows(rb, D_IN),
        out_shape=jax.ShapeDtypeStruct((t, D_IN), jnp.float32),
        scratch_shapes=[pltpu.VMEM((D_IN, D), _MM), pltpu.SemaphoreType.DMA((4,))],
        compiler_params=_params(),
    )(xh, gb, wfull)


def _mix_in_bwd(name, dh_res, dz, xh, gb, wfull):
    t = xh.shape[0]
    rb = _row_block(t)

    def body(dhr_ref, dz_ref, xh_ref, gb_ref, w_ref, dh_ref, hb_ref, wt, sems):
        @pl.when(pl.program_id(0) == 0)
        def _():
            _load_rows(w_ref, OFF_WIN, WIN_ROWS, wt, sems)

        dh_ref[...] = dhr_ref[...] + _dot(dz_ref[...], wt[...])
        hb_ref[...] = (xh_ref[...] * gb_ref[0:1, :] + gb_ref[1:2, :]).astype(_MM)

    return pl.pallas_call(
        body, name=name, grid=(t // rb,),
        in_specs=[_rows(rb, D), _rows(rb, D_IN), _rows(rb, D), _full((2, D)), _ANY],
        out_specs=[_rows(rb, D), _rows(rb, D)],
        out_shape=[jax.ShapeDtypeStruct((t, D), jnp.float32), jax.ShapeDtypeStruct((t, D), _MM)],
        scratch_shapes=[pltpu.VMEM((D_IN, D), _MM), pltpu.SemaphoreType.DMA((4,))],
        compiler_params=_params(),
    )(dh_res, dz, xh, gb, wfull)


def _mix_out_fwd(name, xh, gb, ycat, wfull):
    t = xh.shape[0]
    rb = _row_block(t)

    def body(xh_ref, gb_ref, y_ref, w_ref, out_ref, rstd_ref, wo, sems):
        @pl.when(pl.program_id(0) == 0)
        def _():
            _load_rows(w_ref, OFF_WOUT, WOUT_ROWS, wo, sems)

        h = xh_ref[...] * gb_ref[0:1, :] + gb_ref[1:2, :]
        xo, rstd = _ln_fwd(ALPHA * h + _dot(y_ref[...], wo[...]))
        out_ref[...] = xo
        rstd_ref[...] = rstd

    return pl.pallas_call(
        body, name=name, grid=(t // rb,),
        in_specs=[_rows(rb, D), _full((2, D)), _rows(rb, D), _ANY],
        out_specs=[_rows(rb, D), _rows(rb, 1)],
        out_shape=[jax.ShapeDtypeStruct((t, D), jnp.float32), jax.ShapeDtypeStruct((t, 1), jnp.float32)],
        scratch_shapes=[pltpu.VMEM((D, D), _MM), pltpu.SemaphoreType.DMA((4,))],
        compiler_params=_params(),
    )(xh, gb, ycat, wfull)


def _mix_out_bwd(name, dy, xo, rstd, gb_out, wfull):
    t = xo.shape[0]
    rb = _row_block(t)

    def body(dy_ref, xo_ref, rstd_ref, gbo_ref, w_ref, dhr_ref, dyc_ref, dsb_ref, dgb_ref, wo, sems):
        @pl.when(pl.program_id(0) == 0)
        def _():
            dgb_ref[...] = jnp.zeros_like(dgb_ref)
            _load_rows(w_ref, OFF_WOUT, WOUT_ROWS, wo, sems)

        dy = dy_ref[...]
        xo = xo_ref[...]
        dgb_ref[0:1, :] += jnp.sum(dy * xo, axis=0, keepdims=True)
        dgb_ref[1:2, :] += jnp.sum(dy, axis=0, keepdims=True)
        ds = _ln_bwd(dy * gbo_ref[0:1, :], xo, rstd_ref[...])
        dsb = ds.astype(_MM)
        dsb_ref[...] = dsb
        dhr_ref[...] = ALPHA * ds
        dyc_ref[...] = _dot_nt(dsb, wo[...])

    return pl.pallas_call(
        body, name=name, grid=(t // rb,),
        in_specs=[_rows(rb, D), _rows(rb, D), _rows(rb, 1), _full((2, D)), _ANY],
        out_specs=[_rows(rb, D), _rows(rb, D), _rows(rb, D), _acc((8, D))],
        out_shape=[jax.ShapeDtypeStruct((t, D), jnp.float32), jax.ShapeDtypeStruct((t, D), jnp.float32),
                   jax.ShapeDtypeStruct((t, D), _MM), jax.ShapeDtypeStruct((8, D), jnp.float32)],
        scratch_shapes=[pltpu.VMEM((D, D), _MM), pltpu.SemaphoreType.DMA((4,))],
        compiler_params=_params(),
    )(dy, xo, rstd, gb_out, wfull)


def _loss_fwd_bwd(xh, gb, target):
    t = xh.shape[0]
    rb = _row_block(t)
    per = rb // ROW0

    def body(xh_ref, gb_ref, *refs):
        dy_ref, loss_ref = refs[per:]

        @pl.when(pl.program_id(0) == 0)
        def _():
            loss_ref[...] = jnp.zeros_like(loss_ref)

        y = xh_ref[...] * gb_ref[0:1, :] + gb_ref[1:2, :]
        tg = _join_parts(y[0:ROW0], refs[:per])
        err = y - tg
        dy_ref[...] = err * (1.0 / D)
        per_row = jnp.mean(err * err, axis=-1, keepdims=True)
        loss_ref[...] += 0.5 * jnp.sum(per_row, axis=0, keepdims=True)

    return pl.pallas_call(
        body, name="loss", grid=(t // rb,),
        in_specs=[_rows(rb, D), _full((2, D))] + _frame_parts(rb, D, (t - ROW0) // ROW0),
        out_specs=[_rows(rb, D), _acc((1, 1))],
        out_shape=[jax.ShapeDtypeStruct((t, D), jnp.float32), jax.ShapeDtypeStruct((1, 1), jnp.float32)],
        compiler_params=_params(),
    )(xh, gb, *[target] * per)


def _ln_in_bwd(dy, xh, rstd, gb):
    t = xh.shape[0]

    def body(dy_ref, xh_ref, rstd_ref, gb_ref, dx_ref, dhead_ref, dgb_ref):
        i = pl.program_id(0)

        @pl.when(i == 0)
        def _():
            dgb_ref[...] = jnp.zeros_like(dgb_ref)

        dy = dy_ref[...]
        xh = xh_ref[...]
        dgb_ref[0:1, :] += jnp.sum(dy * xh, axis=0, keepdims=True)
        dgb_ref[1:2, :] += jnp.sum(dy, axis=0, keepdims=True)
        dx = _ln_bwd(dy * gb_ref[0:1, :], xh, rstd_ref[...])

        @pl.when(i == 0)
        def _():
            dhead_ref[...] = dx

        @pl.when(i > 0)
        def _():
            dx_ref[...] = dx

    return pl.pallas_call(
        body, name="ln_in_bwd", grid=(t // ROW0,),
        in_specs=[_rows(ROW0, D), _rows(ROW0, D), _rows(ROW0, 1), _full((2, D))],
        out_specs=[_frames(D), _acc((ROW0, D)), _acc((8, D))],
        out_shape=[jax.ShapeDtypeStruct((t - ROW0, D), jnp.float32), jax.ShapeDtypeStruct((ROW0, D), jnp.float32),
                   jax.ShapeDtypeStruct((8, D), jnp.float32)],
        compiler_params=_params(),
    )(dy, xh, rstd, gb)


def _dw_rows(t, cols):
    for tt in (2080, 1664, 640, 128):
        vmem = 2 * tt * (cols + D) * 2 + cols * D * 6
        if t % tt == 0 and (t > 1024 or tt == 128) and vmem <= 44 * 1024 * 1024:
            return tt
    raise ValueError((t, cols))


def _dw_into(name, gpack, x, y, cols, pieces):
    t, k = x.shape
    tt = _dw_rows(t, cols)
    nt = t // tt

    def body(x_ref, y_ref, g_in, g_out, acc, stage, sems):
        j = pl.program_id(0)
        s = pl.program_id(1)

        @pl.when(s == 0)
        def _():
            acc[...] = jnp.zeros_like(acc)

        acc[...] += _dot_tn(x_ref[...], y_ref[...])

        @pl.when(s == nt - 1)
        def _():
            stage[...] = acc[...].astype(stage.dtype)
            cps = []
            for q, (lo, n, chip_of, off) in enumerate(pieces):
                cp = pltpu.make_async_copy(stage.at[pl.ds(lo, n)], g_out.at[chip_of(j), pl.ds(off, n)],
                                           sems.at[q])
                cp.start()
                cps.append(cp)
            for cp in cps:
                cp.wait()

    return pl.pallas_call(
        body, name=name, grid=(k // cols, nt),
        in_specs=[pl.BlockSpec((tt, cols), lambda j, s: (s, j)), pl.BlockSpec((tt, D), lambda j, s: (s, 0)), _ANY],
        out_specs=_ANY,
        out_shape=jax.ShapeDtypeStruct(gpack.shape, gpack.dtype),
        input_output_aliases={2: 0},
        scratch_shapes=[pltpu.VMEM((cols, D), jnp.float32), pltpu.VMEM((cols, D), gpack.dtype),
                        pltpu.SemaphoreType.DMA((len(pieces),))],
        compiler_params=_params(("arbitrary", "arbitrary")),
    )(x, y, gpack)


_TAIL_U = 32
_TAIL_X = 32
_MIX_ROWS = 320


def _decay_mask(rb, h):
    ii = lax.broadcasted_iota(jnp.int32, (rb, rb), 0)
    jj = lax.broadcasted_iota(jnp.int32, (rb, rb), 1)
    dist = jnp.abs(ii - jj).astype(jnp.float32)
    vis = (jj >> 6) <= (ii >> 6)
    return jnp.where(vis, jnp.exp(LOG_GAMMA[h] * dist), 0.0)


def _row_decays(rb, h):
    r = lax.broadcasted_iota(jnp.int32, (rb, DH), 0).astype(jnp.float32)
    return jnp.exp(LOG_GAMMA[h] * (r + 1.0)), jnp.exp(LOG_GAMMA[h] * (rb - 1.0 - r))


def _rope(x, cs, sn):
    return x * cs + pltpu.roll(x, DH // 2, 1) * sn


def _rope_t(dx, cs, sn):
    return dx * cs + pltpu.roll(dx * sn, DH // 2, 1)


def _pool_count(blk, rb):
    row = blk * rb + lax.broadcasted_iota(jnp.int32, (rb, D_POOL), 0) - PAD
    lane = lax.broadcasted_iota(jnp.int32, (rb, D_POOL), 1)
    win = jnp.left_shift(2, lane >> 6)
    return jnp.clip(row + 1, 1, win).astype(jnp.float32)


def _pool_select(p2, p4, p8, p16):
    lane = lax.broadcasted_iota(jnp.int32, p2.shape, 1)
    return jnp.where(lane < 64, p2, jnp.where(lane < 128, p4, jnp.where(lane < 192, p8, p16)))


def _trailing_windows(ext, p2, p4, p8, rb):
    n = _TAIL_X + rb
    p2[8:n, :] = ext[8:n, :] + ext[pl.ds(7, n - 8), :]
    p4[16:n, :] = p2[16:n, :] + p2[pl.ds(14, n - 16), :]
    p8[24:n, :] = p4[24:n, :] + p4[pl.ds(20, n - 24), :]
    lo = _TAIL_X
    p16 = p8[lo:n, :] + p8[lo - 8:n - 8, :]
    return _pool_select(p2[lo:n, :], p4[lo:n, :], p8[lo:n, :], p16)


def _leading_windows(ext, p2, p4, p8, rb):
    p2[0:rb + 24, :] = ext[0:rb + 24, :] + ext[pl.ds(1, rb + 24), :]
    p4[0:rb + 16, :] = p2[0:rb + 16, :] + p2[pl.ds(2, rb + 16), :]
    p8[0:rb + 8, :] = p4[0:rb + 8, :] + p4[pl.ds(4, rb + 8), :]
    p16 = p8[0:rb, :] + p8[8:rb + 8, :]
    return _pool_select(p2[0:rb, :], p4[0:rb, :], p8[0:rb, :], p16)


def _shifted_copies(ext, copies, first, n, sign):
    for b in range(1, 8):
        copies[b - 1, first:first + n, :] = ext[pl.ds(first - sign * b, n), :]


def _tap(ext, copies, k, start, rows, sign):
    a, b = divmod(k, 8)
    src = ext if b == 0 else copies.at[b - 1]
    return src[pl.ds(start - sign * 8 * a, rows), :]


def _sub_rows(rb):
    return 128 if rb % 128 == 0 else 64


def _mix_core_fwd(name, z, cs, sn, wbd, pscale, cdw, cvec, wpw, gn):
    t = z.shape[0]
    rb = _row_block(t, _MIX_ROWS)
    nblk = t // rb
    sr = _sub_rows(rb)

    def body(z_ref, cs_ref, sn_ref, wbd_ref, ps_ref, cdw_ref, cvec_ref, wpw_ref, gn_ref,
             y_ref, st_ref, ut_ref, cv_ref, yp_ref,
             uext, xext, cv, p2, p4, p8, ucopies, state, wmask):
        i = pl.program_id(0)

        @pl.when(i == 0)
        def _():
            state[...] = jnp.zeros_like(state)
            uext[0:_TAIL_U, :] = jnp.zeros((_TAIL_U, D_CONV), jnp.float32)
            xext[0:_TAIL_X, :] = jnp.zeros((_TAIL_X, D_POOL), jnp.float32)
            for h in range(HEADS):
                wmask[h] = _decay_mask(rb, h)

        st_ref[0] = state[...]
        ut_ref[0] = uext[0:_TAIL_U, :]

        xp = z_ref[:, 0:256]
        uext[_TAIL_U:_TAIL_U + rb, :] = z_ref[:, 256:512] * _sigmoid(z_ref[:, 512:768])
        xext[_TAIL_X:_TAIL_X + rb, :] = xp

        _shifted_copies(uext, ucopies, 8, rb + _TAIL_U - 8, 1)
        for r in range(0, rb, sr):
            acc = jnp.zeros((sr, D_CONV), jnp.float32)
            for k in range(CONV_W):
                acc = acc + _tap(uext, ucopies, k, _TAIL_U + r, sr, 1) * cdw_ref[CONV_W - 1 - k:CONV_W - k, :]
            cv[r:r + sr, :] = acc

        win = _trailing_windows(xext, p2, p4, p8, rb)
        ypb = (win / _pool_count(i, rb) - xp).astype(_MM)
        yp_ref[...] = ypb
        y_ref[:, 0:256] = (_dot(ypb, wbd_ref[...]) * ps_ref[...]).astype(_MM)
        cv_ref[...] = cv[...]
        cn, _ = _ln_fwd(cv[...] + cvec_ref[0:1, :])
        ln = cn * cvec_ref[1:2, :] + cvec_ref[2:3, :]
        sw = ln * _sigmoid(ln)
        y_ref[:, 256:512] = _dot(sw.astype(_MM), wpw_ref[...]).astype(_MM)
        csv = cs_ref[...]
        snv = sn_ref[...]
        for h in range(HEADS):
            q = _rope(z_ref[:, 768 + h * DH:768 + (h + 1) * DH], csv, snv)
            k = _rope(z_ref[:, 1280 + h * DH:1280 + (h + 1) * DH], csv, snv) * (DH ** -0.5)
            vb = z_ref[:, 1792 + h * DH:1792 + (h + 1) * DH].astype(_MM)
            g = z_ref[:, 2304 + h * DH:2304 + (h + 1) * DH]
            a, b = _row_decays(rb, h)
            s = _dot_nt(q.astype(_MM), k.astype(_MM)) * wmask[h]
            o = _dot(s.astype(_MM), vb) + _dot((q * a).astype(_MM), state[h].astype(_MM))
            state[h] = math.exp(LOG_GAMMA[h] * rb) * state[h] + _dot_tn((k * b).astype(_MM), vb)
            on, _ = _ln_fwd(o)
            y_ref[:, 512 + h * DH:512 + (h + 1) * DH] = (
                g * _sigmoid(g) * on * gn_ref[:, h * DH:(h + 1) * DH]).astype(_MM)

        uext[0:_TAIL_U, :] = uext[rb:rb + _TAIL_U, :]
        xext[0:_TAIL_X, :] = xext[rb:rb + _TAIL_X, :]

    return pl.pallas_call(
        body, name=name, grid=(nblk,),
        in_specs=[_rows(rb, D_IN), _rows(rb, DH), _rows(rb, DH), _full((256, 256)), _full((1, 256)),
                  _full((32, 256)), _full((8, 256)), _full((256, 256)), _full((1, D_RET))],
        out_specs=[_rows(rb, D),
                   pl.BlockSpec((1, HEADS, DH, DH), lambda i: (i, 0, 0, 0)),
                   pl.BlockSpec((1, _TAIL_U, D_CONV), lambda i: (i, 0, 0)),
                   _rows(rb, D_CONV), _rows(rb, D_POOL)],
        out_shape=[jax.ShapeDtypeStruct((t, D), _MM),
                   jax.ShapeDtypeStruct((nblk, HEADS, DH, DH), jnp.float32),
                   jax.ShapeDtypeStruct((nblk, _TAIL_U, D_CONV), jnp.float32),
                   jax.ShapeDtypeStruct((t, D_CONV), jnp.float32),
                   jax.ShapeDtypeStruct((t, D_POOL), _MM)],
        scratch_shapes=[pltpu.VMEM((rb + _TAIL_U, D_CONV), jnp.float32),
                        pltpu.VMEM((rb + _TAIL_X, D_POOL), jnp.float32),
                        pltpu.VMEM((rb, D_CONV), jnp.float32),
                        pltpu.VMEM((rb + _TAIL_X, D_POOL), jnp.float32),
                        pltpu.VMEM((rb + _TAIL_X, D_POOL), jnp.float32),
                        pltpu.VMEM((rb + _TAIL_X, D_POOL), jnp.float32),
                        pltpu.VMEM((7, rb + _TAIL_U, D_CONV), jnp.float32),
                        pltpu.VMEM((HEADS, DH, DH), jnp.float32),
                        pltpu.VMEM((HEADS, rb, rb), jnp.float32)],
        compiler_params=_params(),
    )(z, cs, sn, wbd, pscale, cdw, cvec, wpw, gn)


def _mix_core_bwd(name, z, dyc, cs, sn, st_in, ut_in, cv_in, yp_in, wbd, pscale, cdw, cvec, wpw, gn):
    t = z.shape[0]
    rb = _row_block(t, _MIX_ROWS)
    nblk = t // rb
    sr = _sub_rows(rb)
    rev = lambda i: nblk - 1 - i

    def body(z_ref, dy_ref, cs_ref, sn_ref, st_ref, ut_ref, cv_ref, yp_ref,
             wbd_ref, ps_ref, cdw_ref, cvec_ref, wpw_ref, gn_ref,
             dz_ref, dwbd_ref, dwpw_ref, dcdw_ref, dsm_ref,
             uext, cv, dcvext, eext, p2, p4, p8, ucopies, dcopies, dstate, wmask):
        i = pl.program_id(0)
        blk = nblk - 1 - i

        @pl.when(i == 0)
        def _():
            dstate[...] = jnp.zeros_like(dstate)
            dcvext[rb:rb + _TAIL_U, :] = jnp.zeros((_TAIL_U, D_CONV), jnp.float32)
            eext[rb:rb + _TAIL_X, :] = jnp.zeros((_TAIL_X, D_POOL), jnp.float32)
            dwbd_ref[...] = jnp.zeros_like(dwbd_ref)
            dwpw_ref[...] = jnp.zeros_like(dwpw_ref)
            dcdw_ref[...] = jnp.zeros_like(dcdw_ref)
            dsm_ref[...] = jnp.zeros_like(dsm_ref)
            for h in range(HEADS):
                wmask[h] = _decay_mask(rb, h)

        row = blk * rb + lax.broadcasted_iota(jnp.int32, (rb, 1), 0)
        live = row >= PAD

        ca = z_ref[:, 256:512]
        sg_c = _sigmoid(z_ref[:, 512:768])
        uext[0:_TAIL_U, :] = ut_ref[0]
        uext[_TAIL_U:_TAIL_U + rb, :] = ca * sg_c

        cnt = _pool_count(blk, rb)
        ypb = yp_ref[...]
        dyp = dy_ref[:, 0:256]
        pm = _dot(ypb, wbd_ref[...])
        dsm_ref[1:2, 0:256] += jnp.sum(dyp * pm, axis=0, keepdims=True)
        dpm = (dyp * ps_ref[...]).astype(_MM)
        dwbd_ref[...] += _dot_tn(ypb, dpm)
        dypre = _dot_nt(dpm, wbd_ref[...])
        eext[0:rb, :] = dypre / cnt
        win = _leading_windows(eext, p2, p4, p8, rb)
        dz_ref[:, 0:256] = jnp.where(live, win - dypre, 0.0).astype(_MM)

        cn, rstd_c = _ln_fwd(cv_ref[...] + cvec_ref[0:1, :])
        ln = cn * cvec_ref[1:2, :] + cvec_ref[2:3, :]
        sg_l = _sigmoid(ln)
        swb = (ln * sg_l).astype(_MM)
        dycb = dy_ref[:, 256:512].astype(_MM)
        dwpw_ref[...] += _dot_tn(swb, dycb)
        dln = _dot_nt(dycb, wpw_ref[...]) * (sg_l * (1.0 + ln * (1.0 - sg_l)))
        dsm_ref[3:4, 0:256] += jnp.sum(dln * cn, axis=0, keepdims=True)
        dsm_ref[4:5, 0:256] += jnp.sum(dln, axis=0, keepdims=True)
        dcv = _ln_bwd(dln * cvec_ref[1:2, :], cn, rstd_c)
        dsm_ref[2:3, 0:256] += jnp.sum(dcv, axis=0, keepdims=True)
        dcvext[0:rb, :] = dcv
        _shifted_copies(uext, ucopies, 8, rb + _TAIL_U - 8, 1)
        _shifted_copies(dcvext, dcopies, 0, rb + _TAIL_U - 8, -1)
        for k in range(CONV_W):
            prod = dcv * _tap(uext, ucopies, k, _TAIL_U, rb, 1)
            dcdw_ref[CONV_W - 1 - k:CONV_W - k, :] += jnp.sum(prod, axis=0, keepdims=True)
        for r in range(0, rb, sr):
            acc = jnp.zeros((sr, D_CONV), jnp.float32)
            for k in range(CONV_W):
                acc = acc + _tap(dcvext, dcopies, k, r, sr, -1) * cdw_ref[CONV_W - 1 - k:CONV_W - k, :]
            cv[r:r + sr, :] = acc
        du = cv[...]
        dz_ref[:, 256:512] = jnp.where(live, du * sg_c, 0.0).astype(_MM)
        dz_ref[:, 512:768] = jnp.where(live, du * ca * sg_c * (1.0 - sg_c), 0.0).astype(_MM)

        csv = cs_ref[...]
        snv = sn_ref[...]
        for h in range(HEADS):
            q = _rope(z_ref[:, 768 + h * DH:768 + (h + 1) * DH], csv, snv)
            k = _rope(z_ref[:, 1280 + h * DH:1280 + (h + 1) * DH], csv, snv) * (DH ** -0.5)
            vb = z_ref[:, 1792 + h * DH:1792 + (h + 1) * DH].astype(_MM)
            g = z_ref[:, 2304 + h * DH:2304 + (h + 1) * DH]
            a, b = _row_decays(rb, h)
            qb = q.astype(_MM)
            kb = k.astype(_MM)
            qab = (q * a).astype(_MM)
            kbb = (k * b).astype(_MM)
            stb = st_ref[0, h].astype(_MM)
            sb = (_dot_nt(qb, kb) * wmask[h]).astype(_MM)
            o = _dot(sb, vb) + _dot(qab, stb)
            on, rstd_o = _ln_fwd(o)
            gnv = gn_ref[:, h * DH:(h + 1) * DH]
            sg_g = _sigmoid(g)
            si_g = g * sg_g
            dyr = dy_ref[:, 512 + h * DH:512 + (h + 1) * DH]
            dsm_ref[0:1, h * DH:(h + 1) * DH] += jnp.sum(dyr * on * si_g, axis=0, keepdims=True)
            dgate = dyr * on * gnv * (sg_g * (1.0 + g * (1.0 - sg_g)))
            dob = _ln_bwd(dyr * gnv * si_g, on, rstd_o).astype(_MM)
            dstb = dstate[h].astype(_MM)
            dsb = (_dot_nt(dob, vb) * wmask[h]).astype(_MM)
            dq = _dot(dsb, kb) + _dot_nt(dob, stb) * a
            dk = _dot_tn(dsb, qb) + _dot_nt(vb, dstb) * b
            dv = _dot_tn(sb, dob) + _dot(kbb, dstb)
            dstate[h] = math.exp(LOG_GAMMA[h] * rb) * dstate[h] + _dot_tn(qab, dob)
            dz_ref[:, 768 + h * DH:768 + (h + 1) * DH] = jnp.where(live, _rope_t(dq, csv, snv), 0.0).astype(_MM)
            dz_ref[:, 1280 + h * DH:1280 + (h + 1) * DH] = jnp.where(
                live, _rope_t(dk * (DH ** -0.5), csv, snv), 0.0).astype(_MM)
            dz_ref[:, 1792 + h * DH:1792 + (h + 1) * DH] = jnp.where(live, dv, 0.0).astype(_MM)
            dz_ref[:, 2304 + h * DH:2304 + (h + 1) * DH] = jnp.where(live, dgate, 0.0).astype(_MM)

        dcvext[rb:rb + _TAIL_U, :] = dcvext[0:_TAIL_U, :]
        eext[rb:rb + _TAIL_X, :] = eext[0:_TAIL_X, :]

    rrows = lambda n: pl.BlockSpec((rb, n), lambda i: (rev(i), 0))
    return pl.pallas_call(
        body, name=name, grid=(nblk,),
        in_specs=[rrows(D_IN), rrows(D), rrows(DH), rrows(DH),
                  pl.BlockSpec((1, HEADS, DH, DH), lambda i: (rev(i), 0, 0, 0)),
                  pl.BlockSpec((1, _TAIL_U, D_CONV), lambda i: (rev(i), 0, 0)),
                  rrows(D_CONV), rrows(D_POOL),
                  _full((256, 256)), _full((1, 256)), _full((32, 256)), _full((8, 256)), _full((256, 256)),
                  _full((1, D_RET))],
        out_specs=[rrows(D_IN), _acc((256, 256)), _acc((256, 256)), _acc((32, 256)), _acc((8, 512))],
        out_shape=[jax.ShapeDtypeStruct((t, D_IN), _MM),
                   jax.ShapeDtypeStruct((256, 256), jnp.float32), jax.ShapeDtypeStruct((256, 256), jnp.float32),
                   jax.ShapeDtypeStruct((32, 256), jnp.float32), jax.ShapeDtypeStruct((8, 512), jnp.float32)],
        scratch_shapes=[pltpu.VMEM((rb + _TAIL_U, D_CONV), jnp.float32),
                        pltpu.VMEM((rb, D_CONV), jnp.float32),
                        pltpu.VMEM((rb + _TAIL_U, D_CONV), jnp.float32),
                        pltpu.VMEM((rb + _TAIL_X, D_POOL), jnp.float32),
                        pltpu.VMEM((rb + _TAIL_X, D_POOL), jnp.float32),
                        pltpu.VMEM((rb + _TAIL_X, D_POOL), jnp.float32),
                        pltpu.VMEM((rb + _TAIL_X, D_POOL), jnp.float32),
                        pltpu.VMEM((7, rb + _TAIL_U, D_CONV), jnp.float32),
                        pltpu.VMEM((7, rb + _TAIL_U, D_CONV), jnp.float32),
                        pltpu.VMEM((HEADS, DH, DH), jnp.float32),
                        pltpu.VMEM((HEADS, rb, rb), jnp.float32)],
        compiler_params=_params(),
    )(z, dyc, cs, sn, st_in, ut_in, cv_in, yp_in, wbd, pscale, cdw, cvec, wpw, gn)


def _me():
    return lax.axis_index("x"), lax.axis_index("y"), lax.axis_index("c")


def _flip(me, mask):
    return tuple(1 - m if f else m for m, f in zip(me, mask))


def _push(name, aliased, inputs, fresh, remote):
    n_al, n_in, n_out, n_rem = len(aliased), len(inputs), len(fresh), len(remote)

    def body(*refs):
        ins = refs[n_al:n_al + n_in]
        al = refs[n_al + n_in:2 * n_al + n_in]
        outs = refs[2 * n_al + n_in:2 * n_al + n_in + n_out]
        send_sems, recv_sems = refs[2 * n_al + n_in + n_out:]
        me = _me()
        copies = []
        for k, (mask, src_fn, dst_fn) in enumerate(remote):
            cp = pltpu.make_async_remote_copy(
                src_ref=src_fn(al, ins, outs, me), dst_ref=dst_fn(al, ins, outs, me),
                send_sem=send_sems.at[k], recv_sem=recv_sems.at[k],
                device_id=_flip(me, mask), device_id_type=MESH)
            cp.start()
            copies.append(cp)
        for cp in copies:
            cp.wait()

    return pl.pallas_call(
        body, name=name,
        in_specs=[_ANY] * (n_al + n_in), out_specs=[_ANY] * (n_al + n_out),
        out_shape=[jax.ShapeDtypeStruct(a.shape, a.dtype) for a in aliased] + list(fresh),
        input_output_aliases={i: i for i in range(n_al)},
        scratch_shapes=[pltpu.SemaphoreType.DMA((n_rem,)), pltpu.SemaphoreType.DMA((n_rem,))],
    )(*aliased, *inputs)


_HBM = pl.BlockSpec(memory_space=pltpu.HBM)
_SEM = pl.BlockSpec(memory_space=pltpu.SEMAPHORE)
_EFFECT = pltpu.SideEffectType.DATAFLOW_SIDE_EFFECTING


def _push_start(name, bufs, remote):
    n, n_rem = len(bufs), len(remote)

    def body(*refs):
        ins = refs[:n]
        send_sems, recv_sems = refs[n], refs[n + 1]
        token = refs[2 * n + 2]
        me = _me()
        for k, (mask, src_fn, dst_fn) in enumerate(remote):
            pltpu.make_async_remote_copy(
                src_ref=src_fn(ins, me), dst_ref=dst_fn(ins, me),
                send_sem=send_sems.at[k], recv_sem=recv_sems.at[k],
                device_id=_flip(me, mask), device_id_type=MESH).start()
        token[...] = jnp.zeros_like(token)

    out = pl.pallas_call(
        body, name=name,
        out_shape=(pltpu.SemaphoreType.DMA((n_rem,)), pltpu.SemaphoreType.DMA((n_rem,)),
                   *[pltpu.HBM(b.shape, b.dtype) for b in bufs], jax.ShapeDtypeStruct((8, 128), jnp.float32)),
        in_specs=[_HBM] * n,
        out_specs=(_SEM, _SEM, *[_HBM] * n, pl.BlockSpec(memory_space=pltpu.VMEM)),
        input_output_aliases={i: i + 2 for i in range(n)},
        compiler_params=pltpu.CompilerParams(has_side_effects=_EFFECT),
    )(*[pltpu.with_memory_space_constraint(b, pltpu.HBM) for b in bufs])
    return out[0], out[1], list(out[2:2 + n]), out[2 + n]


def _push_wait(name, send_sems, recv_sems, bufs, after, remote):
    n = len(bufs)

    def body(*refs):
        ins = refs[:n]
        s_sems, r_sems = refs[n], refs[n + 1]
        me = _me()
        for k, (mask, src_fn, dst_fn) in enumerate(remote):
            cp = pltpu.make_async_remote_copy(
                src_ref=src_fn(ins, me), dst_ref=dst_fn(ins, me),
                send_sem=s_sems.at[k], recv_sem=r_sems.at[k],
                device_id=_flip(me, mask), device_id_type=MESH)
            cp.wait_send()
            cp.wait_recv()

    out = pl.pallas_call(
        body, name=name,
        out_shape=tuple(pltpu.HBM(b.shape, b.dtype) for b in bufs),
        in_specs=[_HBM] * n + [_SEM, _SEM, _ANY], out_specs=tuple([_HBM] * n),
        input_output_aliases={i: i for i in range(n)},
        compiler_params=pltpu.CompilerParams(has_side_effects=_EFFECT),
    )(*bufs, send_sems, recv_sems, after)
    return list(out)


_ICI_MASKS = ((0, 1, 0), (1, 0, 0), (1, 1, 0))
_D2D_MASK = (0, 0, 1)
_ALL_MASKS = tuple((a, b, c) for a in (0, 1) for b in (0, 1) for c in (0, 1))[1:]


def _chip(me):
    return 2 * me[0] + me[1]


def _half(me, rows):
    return pl.ds(me[2] * (rows // 2), rows // 2)


def _other_half(me, rows):
    return pl.ds((1 - me[2]) * (rows // 2), rows // 2)


def _sum_block(rh):
    return next((b for b in (768, 640, 512, 128) if rh % b == 0), rh)


def _own_slot(mine):
    chip = _chip(_me())
    return lax.dynamic_update_slice(lax.empty((N_SHARD,) + mine.shape, mine.dtype), mine[None],
                                    (chip,) + (0,) * mine.ndim)


def _gather_ici_plan(rows, with_small):
    remote = []
    for mask in _ICI_MASKS:
        for b, r in enumerate(rows):
            mine = lambda bufs, me, b=b, r=r: bufs[b].at[_chip(me), _half(me, r)]
            remote.append((mask, mine, mine))
        if with_small:
            mine_small = lambda bufs, me: bufs[len(rows)].at[_chip(me)]
            remote.append((mask, mine_small, mine_small))
    return remote


def _gather_d2d(name, ws):
    remote = []
    for b, w in enumerate(ws):
        for j in range(1, N_SHARD):
            theirs = lambda al, ins, outs, me, j=j, b=b, r=w.shape[1]: al[b].at[(_chip(me) + j) % N_SHARD, _half(me, r)]
            remote.append((_D2D_MASK, theirs, theirs))
    return _push(name, list(ws), [], [], remote)


def _sum_pair(name, g, recv):
    _, r, _ = g.shape
    rb = _sum_block(r // 2)
    nb = r // 2 // rb
    c = lax.axis_index("c").astype(jnp.int32).reshape(1)

    def body(c_ref, g_ref, r_ref, o_ref):
        o_ref[...] = (g_ref[...].astype(jnp.float32) + r_ref[...].astype(jnp.float32)).astype(o_ref.dtype)

    return pl.pallas_call(
        body, name=name,
        grid_spec=pltpu.PrefetchScalarGridSpec(
            num_scalar_prefetch=1, grid=(N_SHARD, nb),
            in_specs=[pl.BlockSpec((None, rb, D), lambda s, i, c_ref: (s, c_ref[0] * nb + i, 0)),
                      pl.BlockSpec((None, rb, D), lambda s, i, c_ref: (s, i, 0))],
            out_specs=pl.BlockSpec((None, rb, D), lambda s, i, c_ref: (s, i, 0))),
        out_shape=jax.ShapeDtypeStruct((N_SHARD, r // 2, D), g.dtype),
        compiler_params=_params(("arbitrary", "arbitrary")),
    )(c, g, recv)


def _sum_chips(name, p, recv):
    _, rh, _ = p.shape
    rb = _sum_block(rh)
    nb = rh // rb
    s = jnp.stack([2 * lax.axis_index("x") + lax.axis_index("y"), lax.axis_index("c")]).astype(jnp.int32)

    def body(s_ref, p_ref, r_ref, o_ref):
        acc = p_ref[...].astype(jnp.float32)
        for j in range(3):
            acc = acc + r_ref[j].astype(jnp.float32)
        o_ref[...] = acc

    return pl.pallas_call(
        body, name=name,
        grid_spec=pltpu.PrefetchScalarGridSpec(
            num_scalar_prefetch=1, grid=(nb,),
            in_specs=[pl.BlockSpec((None, rb, D), lambda i, s_ref: (s_ref[0], i, 0)),
                      pl.BlockSpec((3, rb, D), lambda i, s_ref: (0, i, 0))],
            out_specs=pl.BlockSpec((rb, D), lambda i, s_ref: (s_ref[1] * nb + i, 0))),
        out_shape=jax.ShapeDtypeStruct((2 * rh, D), jnp.float32),
        compiler_params=_params(),
    )(s, p, recv)


def _rs_ici_plan():
    remote = []
    for j, mask in enumerate(_ICI_MASKS):
        remote.append((mask,
                       lambda bufs, me, mask=mask: bufs[0].at[_chip(_flip(me, mask))],
                       lambda bufs, me, j=j: bufs[1].at[j]))
    return remote


def _rs_pair(tag, g):
    _, r, _ = g.shape
    remote = [(_D2D_MASK,
               lambda al, ins, outs, me, s=s: ins[0].at[s, _other_half(me, r)],
               lambda al, ins, outs, me, s=s: outs[0].at[s]) for s in range(N_SHARD)]
    (recv,) = _push("rs_d2d_" + tag, [], [g], [jax.ShapeDtypeStruct((N_SHARD, r // 2, D), g.dtype)], remote)
    return _sum_pair("rs_sum_pair_" + tag, g, recv)


def _rs_d2d_plan(r):
    return [(_D2D_MASK,
             lambda bufs, me, s=s: bufs[0].at[s, _other_half(me, r)],
             lambda bufs, me, s=s: bufs[1].at[s]) for s in range(N_SHARD)]


def _rs_d2d_start(tag, g):
    r = g.shape[1]
    landing = lax.empty((N_SHARD, r // 2, D), g.dtype)
    send, recv, flying, token = _push_start("rs_d2d_" + tag + "_start", [g, landing], _rs_d2d_plan(r))
    return (tag, r, send, recv, flying), token[0, 0]


def _rs_start_after(exchanged, after, rs_plan):
    tag, r, send, recv, flying = exchanged
    g, landed = _push_wait("rs_d2d_" + tag + "_wait", send, recv, flying, after, _rs_d2d_plan(r))
    p = _sum_pair("rs_sum_pair_" + tag, g, landed)
    landing = lax.empty((3, p.shape[1], D), p.dtype)
    send, recv, flying, token = _push_start("rs_ici_" + tag + "_start", [p, landing], rs_plan)
    return (tag, send, recv, flying), token[0, 0]


def _rs_start(tag, g, rs_plan):
    p = _rs_pair(tag, g)
    landing = lax.empty((3, p.shape[1], D), p.dtype)
    send, recv, flying, token = _push_start("rs_ici_" + tag + "_start", [p, landing], rs_plan)
    return (tag, send, recv, flying), token[0, 0]


def _rs_end(started, after, rs_plan):
    tag, send, recv, flying = started
    p, recv3 = _push_wait("rs_ici_" + tag + "_wait", send, recv, flying, after, rs_plan)
    return _sum_chips("rs_sum_chips_" + tag, p, recv3)


def _rs_share(mines):
    remote = []
    for b, m in enumerate(mines):
        half = lambda al, ins, outs, me, b=b, r=m.shape[0]: al[b].at[_half(me, r)]
        remote.append((_D2D_MASK, half, half))
    return _push("rs_share", list(mines), [], [], remote)


def _small_plan():
    slot = lambda bufs, me: bufs[0].at[4 * me[0] + 2 * me[1] + me[2]]
    return [(mask, slot, slot) for mask in _ALL_MASKS]


def _small_start(v):
    me = _me()
    every = lax.dynamic_update_slice(lax.empty((8,) + v.shape, jnp.float32), v[None],
                                     (4 * me[0] + 2 * me[1] + me[2], 0, 0))
    send, recv, flying, token = _push_start("small_all_start", [every], _small_plan())
    return (send, recv, flying), token


def _small_end(started, after):
    send, recv, flying = started
    (every,) = _push_wait("small_all_wait", send, recv, flying, after, _small_plan())
    s = every.shape[1]

    def body(e_ref, o_ref):
        acc = e_ref[0]
        for j in range(1, 8):
            acc = acc + e_ref[j]
        o_ref[...] = acc

    return pl.pallas_call(
        body, name="small_sum", grid=(1,),
        in_specs=[pl.BlockSpec((8, s, D), lambda i: (0, 0, 0))],
        out_specs=pl.BlockSpec((s, D), lambda i: (0, 0)),
        out_shape=jax.ShapeDtypeStruct((s, D), jnp.float32),
        compiler_params=_params(),
    )(every)


def _adamw(name, w, g, m, v):
    r, c = w.shape
    rb = next(b for b in (256, 344, 128, 64, 32, 16, 8, r) if r % b == 0)

    def body(w_ref, g_ref, m_ref, v_ref, d_ref, mo_ref, vo_ref):
        g = g_ref[...]
        m = ADAM_B1 * m_ref[...] + (1.0 - ADAM_B1) * g
        v = ADAM_B2 * v_ref[...] + (1.0 - ADAM_B2) * (g * g)
        m_hat = m / (1.0 - ADAM_B1 ** ADAM_STEP)
        v_hat = v / (1.0 - ADAM_B2 ** ADAM_STEP)
        d_ref[...] = -ADAM_LR * (m_hat / (jnp.sqrt(v_hat) + ADAM_EPS) + ADAM_WD * w_ref[...])
        mo_ref[...] = m
        vo_ref[...] = v

    spec = pl.BlockSpec((rb, c), lambda i: (i, 0))
    return pl.pallas_call(
        body, name=name, grid=(r // rb,),
        in_specs=[spec] * 4, out_specs=[spec] * 3,
        out_shape=[jax.ShapeDtypeStruct((r, c), jnp.float32)] * 3,
        compiler_params=_params(),
    )(w, g, m, v)


_BIG = ("ffn1_w13", "ffn2_w13", "ffn1_w2", "ffn2_w2", "w_in", "w_out", "conv_pw")


def _pack_rows(parts):
    flat = jnp.concatenate([p.reshape(-1) for p in parts])
    pad = (-flat.shape[0]) % (8 * D)
    if pad:
        flat = jnp.concatenate([flat, jnp.zeros((pad,), flat.dtype)])
    return flat.reshape(-1, D)


def _unpack_rows(buf, shapes):
    flat = buf.reshape(-1)
    out, off = [], 0
    for shp in shapes:
        n = math.prod(shp)
        out.append(flat[off:off + n].reshape(shp))
        off += n
    return out


def _pack_shard(parts):
    zeros = lambda n: jnp.zeros((n, D), parts["w_out"].dtype)
    a = jnp.concatenate([parts["ffn1_w13"].T, parts["ffn1_w2"], zeros(A_ROWS - OFF_WIN)], axis=0)
    b = jnp.concatenate([parts["ffn2_w13"].T, parts["ffn2_w2"], parts["w_in"].T, parts["w_out"],
                         parts["conv_pw"].reshape(PW_ROWS, D), zeros(B_ROWS - OFF_PW - PW_ROWS)], axis=0)
    return a, b


def _unpack_shard(a, b):
    return {"ffn1_w13": a[OFF_W13:OFF_W13 + W13_ROWS].T, "ffn1_w2": a[OFF_W2:OFF_W2 + W2_ROWS],
            "ffn2_w13": b[OFF_W13:OFF_W13 + W13_ROWS].T, "ffn2_w2": b[OFF_W2:OFF_W2 + W2_ROWS],
            "w_in": b[OFF_WIN:OFF_WIN + WIN_ROWS].T, "w_out": b[OFF_WOUT:OFF_WOUT + WOUT_ROWS],
            "conv_pw": b[OFF_PW:OFF_PW + PW_ROWS].reshape(64, 256)}


def kernel(x, meta, ln_in_g, ln_in_b, ffn1_w13, ffn1_w2, w_in, pool_w, pool_scale, conv_dw, conv_db, conv_ln_g, conv_ln_b, conv_pw, ret_gn_g, w_out, ffn2_w13, ffn2_w2, ln_g, ln_b, loss_target, m_meta, m_ln_in_g, m_ln_in_b, m_ffn1_w13, m_ffn1_w2, m_w_in, m_pool_w, m_pool_scale, m_conv_dw, m_conv_db, m_conv_ln_g, m_conv_ln_b, m_conv_pw, m_ret_gn_g, m_w_out, m_ffn2_w13, m_ffn2_w2, m_ln_g, m_ln_b, v_meta, v_ln_in_g, v_ln_in_b, v_ffn1_w13, v_ffn1_w2, v_w_in, v_pool_w, v_pool_scale, v_conv_dw, v_conv_db, v_conv_ln_g, v_conv_ln_b, v_conv_pw, v_ret_gn_g, v_w_out, v_ffn2_w13, v_ffn2_w2, v_ln_g, v_ln_b):
    f32 = jnp.float32
    seq = x.shape[1]
    t = seq + ROW0
    me = _me()
    chip = _chip(me)
    big_w = {"ffn1_w13": ffn1_w13, "ffn2_w13": ffn2_w13, "ffn1_w2": ffn1_w2, "ffn2_w2": ffn2_w2,
             "w_in": w_in, "w_out": w_out, "conv_pw": conv_pw}

    wa, wb = [], []
    for l in range(DEPTH):
        a, b = _pack_shard({n: big_w[n][l].astype(_WIRE) for n in _BIG})
        wa.append(_own_slot(a))
        wb.append(_own_slot(b))
    small_shapes = [(N_META, 256), (DEPTH, CONV_W, 64), (DEPTH, 3, 256), (DEPTH, 3, 256)]
    small_all = _own_slot(_pack_rows([meta, conv_dw, ln_g, ln_b]))
    wrap = lambda f: (lambda al, ins, outs, me: f(al, me))
    (small_all,) = _push("gather_small", [small_all], [], [],
                         [(m, wrap(s), wrap(d)) for m, s, d in _gather_ici_plan([], True)])
    plan_a0 = _gather_ici_plan([A_ROWS], False)
    plan_b0 = _gather_ici_plan([B_ROWS], False)
    plan_l1 = _gather_ici_plan([A_ROWS, B_ROWS], False)
    a0_send, a0_recv, a0_flying, a0_token = _push_start("gather_ici_a0_start", [wa[0]], plan_a0)

    sm = [_unpack_rows(small_all[s], small_shapes) for s in range(N_SHARD)]
    meta_f = jnp.concatenate([sm[s][0] for s in range(N_SHARD)], axis=1) + a0_token[0, 0]
    cdw_f = jnp.concatenate([sm[s][1] for s in range(N_SHARD)], axis=2)
    lng_f = jnp.concatenate([sm[s][2] for s in range(N_SHARD)], axis=2)
    lnb_f = jnp.concatenate([sm[s][3] for s in range(N_SHARD)], axis=2)

    def mix_params(l):
        wbd = jnp.zeros((D_POOL, D_POOL), f32)
        for g in range(4):
            wbd = wbd.at[64 * g:64 * (g + 1), 64 * g:64 * (g + 1)].set(pool_w[l, g])
        cdw = jnp.pad(cdw_f[l], ((0, 1), (0, 0)))
        cvec = jnp.pad(jnp.stack([conv_db[l], conv_ln_g[l], conv_ln_b[l]]), ((0, 5), (0, 0)))
        wpw = wb[l][:, OFF_PW:OFF_PW + PW_ROWS].reshape(D_CONV, D_CONV)
        return (wbd.astype(_MM), pool_scale[l][None], cdw, cvec, wpw, ret_gn_g[l][None])

    gb_of = lambda l, i: jnp.stack([lng_f[l, i], lnb_f[l, i]])
    gb_in = jnp.stack([ln_in_g, ln_in_b])

    pos = jnp.arange(t, dtype=f32) - PAD
    inv_freq = ROPE_BASE ** (-jnp.arange(0, DH, 2, dtype=f32) / DH)
    ang = pos[:, None] * inv_freq[None, :]
    cs = jnp.concatenate([jnp.cos(ang), jnp.cos(ang)], axis=1)
    sn = jnp.concatenate([-jnp.sin(ang), jnp.sin(ang)], axis=1)

    xh, rstd = _ln_in_fwd(jnp.concatenate([jnp.zeros((PAD, D), f32), meta_f], axis=0), x[0])
    (wa[0],) = _push_wait("gather_ici_a0_wait", a0_send, a0_recv, a0_flying, xh, plan_a0)
    (wa[0],) = _gather_d2d("gather_d2d_a0", [wa[0]])
    b0_send, b0_recv, b0_flying, b0_token = _push_start("gather_ici_b0_start", [wb[0]], plan_b0)
    cur = (xh, rstd, gb_in + b0_token[0, 0])
    saved = []
    for l in range(DEPTH):
        if l == 1:
            wa[1], wb[1] = _push_wait("gather_ici_l1_wait", l1_send, l1_recv, l1_flying, cur[0], plan_l1)
            wa[1], wb[1] = _gather_d2d("gather_d2d_l1", [wa[1], wb[1]])
        a0 = cur
        xh1, r1, au1, hb1 = _ffn_fwd(f"ffn1_fwd_{l}", a0[0], a0[2], wa[l])
        a1 = (xh1, r1, gb_of(l, 0))
        if l == 0:
            (wb[0],) = _push_wait("gather_ici_b0_wait", b0_send, b0_recv, b0_flying, xh1, plan_b0)
            (wb[0],) = _gather_d2d("gather_d2d_b0", [wb[0]])
            l1_send, l1_recv, l1_flying, l1_token = _push_start("gather_ici_l1_start", [wa[1], wb[1]], plan_l1)
            a1 = (xh1, r1, a1[2] + l1_token[0, 0])
        z = _mix_in_fwd(f"mix_in_fwd_{l}", a1[0], a1[2], wb[l])
        mp = mix_params(l)
        ycat, st_in, ut_in, cv_in, yp_in = _mix_core_fwd(f"mix_core_fwd_{l}", z, cs, sn, *mp)
        mix_kept = (cv_in, yp_in)
        xh2, r2 = _mix_out_fwd(f"mix_out_fwd_{l}", a1[0], a1[2], ycat, wb[l])
        a2 = (xh2, r2, gb_of(l, 1))
        xh3, r3, au2, hb2 = _ffn_fwd(f"ffn2_fwd_{l}", a2[0], a2[2], wb[l])
        a3 = (xh3, r3, gb_of(l, 2))
        saved.append((a0, a1, a2, a3, z, ycat, st_in, ut_in, mix_kept, mp, au1, hb1, au2, hb2))
        cur = a3

    dy, loss_part = _loss_fwd_bwd(cur[0], cur[2], loss_target[0])
    loss = lax.psum(loss_part[0, 0], ("x", "y", "c"))

    g_ln_g = [[None] * 3 for _ in range(DEPTH)]
    g_ln_b = [[None] * 3 for _ in range(DEPTH)]
    g_small = [dict() for _ in range(DEPTH)]
    slot = lambda j: j

    def ffn_grads(gbuf, tag, l, hb, hid, dau, dffn):
        gbuf = _dw_into(f"dw13_{tag}_{l}", gbuf, dau, hb, FF_SLOT, [(0, W13_ROWS, slot, OFF_W13)])
        return _dw_into(f"dw2_{tag}_{l}", gbuf, hid, dffn, FF_SLOT,
                        [(0, W2_ROWS, lambda j: 2 * j, OFF_W2), (W2_ROWS, W2_ROWS, lambda j: 2 * j + 1, OFF_W2)])

    rs_plan = _rs_ici_plan()
    token = jnp.zeros((), f32)
    started, exchanging_a = {}, None
    for l in reversed(range(DEPTH)):
        a0, a1, a2, a3, z, ycat, st_in, ut_in, mix_kept, mp, au1, hb1, au2, hb2 = saved[l]
        g_a = lax.empty((N_SHARD, A_ROWS, D), _WIRE)
        g_b = lax.empty((N_SHARD, B_ROWS, D), _WIRE)
        dh, hid, dau, dffn, dgb = _ffn_bwd(f"ffn2_bwd_{l}", dy, a3[0], a3[1], a3[2] + token, au2, wb[l])
        g_ln_g[l][2], g_ln_b[l][2] = dgb[0], dgb[1]
        g_b = ffn_grads(g_b, "ffn2", l, hb2, hid, dau, dffn)
        token = jnp.zeros((), f32)
        if exchanging_a is not None:
            started["a", l + 1], token = _rs_start_after(exchanging_a, dh, rs_plan)
        dh_res, dycat, dsb, dgb = _mix_out_bwd(f"mix_out_bwd_{l}", dh, a2[0], a2[1], a2[2] + token, wb[l])
        g_ln_g[l][1], g_ln_b[l][1] = dgb[0], dgb[1]
        g_b = _dw_into(f"dw_out_{l}", g_b, ycat, dsb, D,
                       [(WOUT_ROWS * s, WOUT_ROWS, lambda j, s=s: s, OFF_WOUT) for s in range(N_SHARD)])
        dz, dwbd, dwpw, dcdw, dsm = _mix_core_bwd(f"mix_core_bwd_{l}", z, dycat, cs, sn, st_in, ut_in, *mix_kept, *mp)
        pw = jnp.concatenate([dwpw.astype(_WIRE).reshape(N_SHARD, PW_ROWS, D),
                              jnp.zeros((N_SHARD, B_ROWS - OFF_PW - PW_ROWS, D), _WIRE)], axis=1)
        g_b = lax.dynamic_update_slice(g_b, pw, (0, OFF_PW, 0))
        g_small[l] = dict(
            pool_w=jnp.stack([dwbd[64 * g:64 * (g + 1), 64 * g:64 * (g + 1)] for g in range(4)]),
            pool_scale=dsm[1, :256], conv_db=dsm[2, :256], conv_ln_g=dsm[3, :256], conv_ln_b=dsm[4, :256],
            ret_gn_g=dsm[0], conv_dw=dcdw[:CONV_W])
        dh, hb = _mix_in_bwd(f"mix_in_bwd_{l}", dh_res, dz, a1[0], a1[2], wb[l])
        g_b = _dw_into(f"dw_in_{l}", g_b, dz, hb, D_IN,
                       [(WIN_ROWS * s, WIN_ROWS, lambda j, s=s: s, OFF_WIN) for s in range(N_SHARD)])
        exchanging_b, token = _rs_d2d_start(f"b{l}", g_b)
        dh, hid, dau, dffn, dgb = _ffn_bwd(f"ffn1_bwd_{l}", dh, a1[0], a1[1], a1[2] + token, au1, wa[l])
        g_ln_g[l][0], g_ln_b[l][0] = dgb[0], dgb[1]
        started["b", l], token = _rs_start_after(exchanging_b, dh, rs_plan)
        zero_rows = jnp.zeros((N_SHARD, A_ROWS - OFF_WIN, D), _WIRE) + token.astype(_WIRE)
        g_a = lax.dynamic_update_slice(g_a, zero_rows, (0, OFF_WIN, 0))
        g_a = ffn_grads(g_a, "ffn1", l, hb1, hid, dau, dffn)
        dy = dh
        if l > 0:
            exchanging_a, token = _rs_d2d_start(f"a{l}", g_a)
        else:
            started["a", l], token = _rs_start(f"a{l}", g_a, rs_plan)
    d_frames, d_head, dgb_in = _ln_in_bwd(dy, saved[0][0][0], saved[0][0][1], gb_in + token)
    grad_x = d_frames[None]

    small_parts = [
        d_head[PAD:ROW0],
        jnp.stack([g_small[l]["conv_dw"] for l in range(DEPTH)]),
        jnp.stack([jnp.stack(g_ln_g[l]) for l in range(DEPTH)]),
        jnp.stack([jnp.stack(g_ln_b[l]) for l in range(DEPTH)]),
        dgb_in[0], dgb_in[1],
        jnp.stack([g_small[l]["pool_w"] for l in range(DEPTH)]),
        jnp.stack([g_small[l]["pool_scale"] for l in range(DEPTH)]),
        jnp.stack([g_small[l]["conv_db"] for l in range(DEPTH)]),
        jnp.stack([g_small[l]["conv_ln_g"] for l in range(DEPTH)]),
        jnp.stack([g_small[l]["conv_ln_b"] for l in range(DEPTH)]),
        jnp.stack([g_small[l]["ret_gn_g"] for l in range(DEPTH)]),
    ]
    small_started, small_token = _small_start(_pack_rows(small_parts))

    keys, mines, after = (("b", 1), ("a", 1), ("b", 0), ("a", 0)), [], small_token
    for key in keys:
        after = _rs_end(started[key], after, rs_plan)
        mines.append(after)
    gsum = dict(zip(keys, _rs_share(mines)))
    g_big = [_unpack_shard(gsum["a", l], gsum["b", l]) for l in range(DEPTH)]
    grads = {n: jnp.stack([g_big[l][n] for l in range(DEPTH)]) for n in _BIG}
    red = _unpack_rows(_small_end(small_started, gsum[keys[-1]]), [p.shape for p in small_parts])
    grads["meta"] = lax.dynamic_slice_in_dim(red[0], 256 * chip, 256, axis=1)
    grads["conv_dw"] = lax.dynamic_slice_in_dim(red[1], 64 * chip, 64, axis=2)
    grads["ln_g"] = lax.dynamic_slice_in_dim(red[2], 256 * chip, 256, axis=2)
    grads["ln_b"] = lax.dynamic_slice_in_dim(red[3], 256 * chip, 256, axis=2)
    for n, v in zip(("ln_in_g", "ln_in_b", "pool_w", "pool_scale", "conv_db", "conv_ln_g", "conv_ln_b", "ret_gn_g"),
                    red[4:]):
        grads[n] = v

    names = ['meta', 'ln_in_g', 'ln_in_b', 'ffn1_w13', 'ffn1_w2', 'w_in', 'pool_w', 'pool_scale', 'conv_dw',
             'conv_db', 'conv_ln_g', 'conv_ln_b', 'conv_pw', 'ret_gn_g', 'w_out', 'ffn2_w13', 'ffn2_w2', 'ln_g', 'ln_b']
    ws = dict(meta=meta, ln_in_g=ln_in_g, ln_in_b=ln_in_b, ffn1_w13=ffn1_w13, ffn1_w2=ffn1_w2, w_in=w_in,
              pool_w=pool_w, pool_scale=pool_scale, conv_dw=conv_dw, conv_db=conv_db, conv_ln_g=conv_ln_g,
              conv_ln_b=conv_ln_b, conv_pw=conv_pw, ret_gn_g=ret_gn_g, w_out=w_out, ffn2_w13=ffn2_w13,
              ffn2_w2=ffn2_w2, ln_g=ln_g, ln_b=ln_b)
    ms = dict(meta=m_meta, ln_in_g=m_ln_in_g, ln_in_b=m_ln_in_b, ffn1_w13=m_ffn1_w13, ffn1_w2=m_ffn1_w2,
              w_in=m_w_in, pool_w=m_pool_w, pool_scale=m_pool_scale, conv_dw=m_conv_dw, conv_db=m_conv_db,
              conv_ln_g=m_conv_ln_g, conv_ln_b=m_conv_ln_b, conv_pw=m_conv_pw, ret_gn_g=m_ret_gn_g,
              w_out=m_w_out, ffn2_w13=m_ffn2_w13, ffn2_w2=m_ffn2_w2, ln_g=m_ln_g, ln_b=m_ln_b)
    vs = dict(meta=v_meta, ln_in_g=v_ln_in_g, ln_in_b=v_ln_in_b, ffn1_w13=v_ffn1_w13, ffn1_w2=v_ffn1_w2,
              w_in=v_w_in, pool_w=v_pool_w, pool_scale=v_pool_scale, conv_dw=v_conv_dw, conv_db=v_conv_db,
              conv_ln_g=v_conv_ln_g, conv_ln_b=v_conv_ln_b, conv_pw=v_conv_pw, ret_gn_g=v_ret_gn_g,
              w_out=v_w_out, ffn2_w13=v_ffn2_w13, ffn2_w2=v_ffn2_w2, ln_g=v_ln_g, ln_b=v_ln_b)
    delta, new_m, new_v = {}, {}, {}
    for n in _BIG:
        shp = ws[n].shape
        two = lambda a: a.reshape(-1, shp[-1])
        d_, m_, v_ = _adamw("adamw_" + n, two(ws[n]), two(grads[n]), two(ms[n]), two(vs[n]))
        delta[n], new_m[n], new_v[n] = d_.reshape(shp), m_.reshape(shp), v_.reshape(shp)
    small_names = [n for n in names if n not in _BIG]
    pk = lambda d: _pack_rows([d[n] for n in small_names])
    d_, m_, v_ = _adamw("adamw_small", pk(ws), pk(grads), pk(ms), pk(vs))
    shapes = [ws[n].shape for n in small_names]
    for n, a, b, c in zip(small_names, _unpack_rows(d_, shapes), _unpack_rows(m_, shapes), _unpack_rows(v_, shapes)):
        delta[n], new_m[n], new_v[n] = a, b, c

    return (loss, grad_x, *[grads[n] for n in names], *[delta[n] for n in names],
            *[new_m[n] for n in names], *[new_v[n] for n in names])
```

```python
import math

import jax
import jax.numpy as jnp
from jax import lax
from jax.experimental import pallas as pl
from jax.experimental.pallas import tpu as pltpu

D = 1024
DEPTH = 2
N_META = 16
PAD = 112
ROW0 = PAD + N_META
D_POOL = 256
D_CONV = 256
D_RET = 512
HEADS = 4
DH = 128
CONV_W = 31
FF_SLOT = 1408
D_FFP = 2 * FF_SLOT
D_IN = 2816
N_SHARD = 4
ALPHA = (2.0 * DEPTH) ** 0.25
LN_EPS = 1e-5
ROPE_BASE = 10000.0
LOG_GAMMA = tuple(math.log(1.0 - 2.0 ** (-5.0 - h)) for h in range(HEADS))
ADAM_LR, ADAM_B1, ADAM_B2, ADAM_EPS, ADAM_WD, ADAM_STEP = 0.001, 0.9, 0.999, 1e-08, 0.01, 10

_MM = jnp.bfloat16
_WIRE = jnp.bfloat16
_VMEM_LIMIT = 56 * 1024 * 1024
_FFN_ROWS = 640

MESH = pl.DeviceIdType.MESH
_ANY = pl.BlockSpec(memory_space=pl.ANY)

W13_ROWS = 1376
W2_ROWS = 688
WIN_ROWS = 704
WOUT_ROWS = 256
PW_ROWS = 16
OFF_W13 = 0
OFF_W2 = W13_ROWS
OFF_WIN = W13_ROWS + W2_ROWS
OFF_WOUT = OFF_WIN + WIN_ROWS
OFF_PW = OFF_WOUT + WOUT_ROWS
A_ROWS = 2080
B_ROWS = 3072
W2_SLOT_OFF = (0, W2_ROWS, FF_SLOT, FF_SLOT + W2_ROWS)


def _row_copies(w_ref, off, n, dst_of, sems, k0):
    return [pltpu.make_async_copy(w_ref.at[s, pl.ds(off, n)], dst_of(s), sems.at[k0 + s]) for s in range(N_SHARD)]


def _load_ffn_weights(w_ref, w13, w2, sems):
    cps = _row_copies(w_ref, OFF_W13, W13_ROWS, lambda s: w13.at[s, pl.ds(0, W13_ROWS)], sems, 0)
    cps += _row_copies(w_ref, OFF_W2, W2_ROWS, lambda s: w2.at[pl.ds(W2_SLOT_OFF[s], W2_ROWS)], sems, 4)
    for cp in cps:
        cp.start()
    zpad = jnp.zeros((FF_SLOT - W13_ROWS, D), w13.dtype)
    for s in range(N_SHARD):
        w13[s, W13_ROWS:FF_SLOT, :] = zpad
    w2[W13_ROWS:FF_SLOT, :] = zpad
    w2[FF_SLOT + W13_ROWS:D_FFP, :] = zpad
    for cp in cps:
        cp.wait()


def _load_rows(w_ref, off, n, dst, sems):
    cps = _row_copies(w_ref, off, n, lambda s: dst.at[pl.ds(s * n, n)], sems, 0)
    for cp in cps:
        cp.start()
    for cp in cps:
        cp.wait()


def _dot(a, b):
    return jnp.dot(a, b, preferred_element_type=jnp.float32)


def _dot_nt(a, b):
    return lax.dot_general(a, b, (((1,), (1,)), ((), ())), preferred_element_type=jnp.float32)


def _dot_tn(a, b):
    return lax.dot_general(a, b, (((0,), (0,)), ((), ())), preferred_element_type=jnp.float32)


def _params(sem=("arbitrary",)):
    return pltpu.CompilerParams(dimension_semantics=sem, vmem_limit_bytes=_VMEM_LIMIT)


def _row_block(t, cap=640):
    for rb in (640, 320, 128):
        if rb <= cap and t % rb == 0 and (t > 1024 or rb == 128):
            return rb
    raise ValueError(t)


def _rows(rb, n):
    return pl.BlockSpec((rb, n), lambda i: (i, 0))


def _full(shape):
    nd = len(shape)
    return pl.BlockSpec(tuple(shape), lambda i: (0,) * nd, pipeline_mode=pl.Buffered(1))


def _acc(shape):
    nd = len(shape)
    return pl.BlockSpec(tuple(shape), lambda i: (0,) * nd)


def _sigmoid(x):
    return 1.0 / (1.0 + jnp.exp(-x))


def _ln_fwd(s):
    mu = jnp.mean(s, axis=-1, keepdims=True)
    xc = s - mu
    var = jnp.mean(xc * xc, axis=-1, keepdims=True)
    rstd = lax.rsqrt(var + LN_EPS)
    return xc * rstd, rstd


def _ln_bwd(dxh, xh, rstd):
    m1 = jnp.mean(dxh, axis=-1, keepdims=True)
    m2 = jnp.mean(dxh * xh, axis=-1, keepdims=True)
    return rstd * (dxh - m1 - xh * m2)


def _frames(n):
    return pl.BlockSpec((ROW0, n), lambda i: (jnp.maximum(i - 1, 0), 0))


def _frame_parts(rb, n, nblk):
    per = rb // ROW0
    return [pl.BlockSpec((ROW0, n), lambda i, k=k: (jnp.clip(i * per - 1 + k, 0, nblk - 1), 0)) for k in range(per)]


def _join_parts(head, parts):
    first = jnp.where(pl.program_id(0) == 0, head, parts[0][...])
    return jnp.concatenate([first] + [p[...] for p in parts[1:]], axis=0)


def _ln_in_fwd(head, x2d):
    t = x2d.shape[0] + ROW0
    rb = _row_block(t)
    per = rb // ROW0

    def body(head_ref, *refs):
        xh_ref, rstd_ref = refs[per:]
        xh, rstd = _ln_fwd(_join_parts(head_ref[...], refs[:per]))
        xh_ref[...] = xh
        rstd_ref[...] = rstd

    return pl.pallas_call(
        body, name="ln_in_fwd", grid=(t // rb,),
        in_specs=[_full((ROW0, D))] + _frame_parts(rb, D, x2d.shape[0] // ROW0),
        out_specs=[_rows(rb, D), _rows(rb, 1)],
        out_shape=[jax.ShapeDtypeStruct((t, D), jnp.float32), jax.ShapeDtypeStruct((t, 1), jnp.float32)],
        compiler_params=_params(),
    )(head, *[x2d] * per)


def _ffn_fwd(name, xh, gb, wfull):
    t = xh.shape[0]
    rb = _row_block(t, _FFN_ROWS)

    def body(xh_ref, gb_ref, w_ref, out_ref, rstd_ref, au_ref, hb_ref, w13, w2, sems):
        @pl.when(pl.program_id(0) == 0)
        def _():
            _load_ffn_weights(w_ref, w13, w2, sems)

        h = xh_ref[...] * gb_ref[0:1, :] + gb_ref[1:2, :]
        hb = h.astype(_MM)
        hb_ref[...] = hb
        acc = jnp.zeros((rb, D), jnp.float32)
        for j in range(2):
            lo = j * FF_SLOT
            a = _dot_nt(hb, w13[j])
            u = _dot_nt(hb, w13[2 + j])
            au_ref[:, lo:lo + FF_SLOT] = a.astype(_MM)
            au_ref[:, D_FFP + lo:D_FFP + lo + FF_SLOT] = u.astype(_MM)
            hid = (a * _sigmoid(a) * u).astype(_MM)
            acc = acc + _dot(hid, w2[lo:lo + FF_SLOT, :])
        xo, rstd = _ln_fwd(ALPHA * h + 0.5 * acc)
        out_ref[...] = xo
        rstd_ref[...] = rstd

    return pl.pallas_call(
        body, name=name, grid=(t // rb,),
        in_specs=[_rows(rb, D), _full((2, D)), _ANY],
        out_specs=[_rows(rb, D), _rows(rb, 1), _rows(rb, 2 * D_FFP), _rows(rb, D)],
        out_shape=[jax.ShapeDtypeStruct((t, D), jnp.float32), jax.ShapeDtypeStruct((t, 1), jnp.float32),
                   jax.ShapeDtypeStruct((t, 2 * D_FFP), _MM), jax.ShapeDtypeStruct((t, D), _MM)],
        scratch_shapes=[pltpu.VMEM((N_SHARD, FF_SLOT, D), _MM), pltpu.VMEM((D_FFP, D), _MM),
                        pltpu.SemaphoreType.DMA((8,))],
        compiler_params=_params(),
    )(xh, gb, wfull)


def _ffn_bwd(name, dy, xo, rstd, gb_out, au, wfull):
    t = xo.shape[0]
    rb = _row_block(t, 320)

    def body(dy_ref, xo_ref, rstd_ref, gbo_ref, au_ref, w_ref,
             dh_ref, hid_ref, dau_ref, dffn_ref, dgb_ref, w13, w2, sems):
        i = pl.program_id(0)

        @pl.when(i == 0)
        def _():
            dgb_ref[...] = jnp.zeros_like(dgb_ref)
            _load_ffn_weights(w_ref, w13, w2, sems)

        dy = dy_ref[...]
        xo = xo_ref[...]
        dgb_ref[0:1, :] += jnp.sum(dy * xo, axis=0, keepdims=True)
        dgb_ref[1:2, :] += jnp.sum(dy, axis=0, keepdims=True)
        ds = _ln_bwd(dy * gbo_ref[0:1, :], xo, rstd_ref[...])
        dffn = (0.5 * ds).astype(_MM)
        dffn_ref[...] = dffn
        dh = ALPHA * ds
        for j in range(2):
            lo = j * FF_SLOT
            a = au_ref[:, lo:lo + FF_SLOT].astype(jnp.float32)
            u = au_ref[:, D_FFP + lo:D_FFP + lo + FF_SLOT].astype(jnp.float32)
            sg = _sigmoid(a)
            si = a * sg
            hid_ref[:, lo:lo + FF_SLOT] = (si * u).astype(_MM)
            dhid = _dot_nt(dffn, w2[lo:lo + FF_SLOT, :])
            da = (dhid * u * (sg * (1.0 + a * (1.0 - sg)))).astype(_MM)
            du = (dhid * si).astype(_MM)
            dau_ref[:, lo:lo + FF_SLOT] = da
            dau_ref[:, D_FFP + lo:D_FFP + lo + FF_SLOT] = du
            dh = dh + _dot(da, w13[j]) + _dot(du, w13[2 + j])
        dh_ref[...] = dh

    return pl.pallas_call(
        body, name=name, grid=(t // rb,),
        in_specs=[_rows(rb, D), _rows(rb, D), _rows(rb, 1), _full((2, D)), _rows(rb, 2 * D_FFP), _ANY],
        out_specs=[_rows(rb, D), _rows(rb, D_FFP), _rows(rb, 2 * D_FFP), _rows(rb, D), _acc((8, D))],
        out_shape=[jax.ShapeDtypeStruct((t, D), jnp.float32),
                   jax.ShapeDtypeStruct((t, D_FFP), _MM), jax.ShapeDtypeStruct((t, 2 * D_FFP), _MM),
                   jax.ShapeDtypeStruct((t, D), _MM), jax.ShapeDtypeStruct((8, D), jnp.float32)],
        scratch_shapes=[pltpu.VMEM((N_SHARD, FF_SLOT, D), _MM), pltpu.VMEM((D_FFP, D), _MM),
                        pltpu.SemaphoreType.DMA((8,))],
        compiler_params=_params(),
    )(dy, xo, rstd, gb_out, au, wfull)


def _mix_in_fwd(name, xh, gb, wfull):
    t = xh.shape[0]
    rb = _row_block(t)

    def body(xh_ref, gb_ref, w_ref, z_ref, wt, sems):
        @pl.when(pl.program_id(0) == 0)
        def _():
            _load_rows(w_ref, OFF_WIN, WIN_ROWS, wt, sems)

        h = xh_ref[...] * gb_ref[0:1, :] + gb_ref[1:2, :]
        z = _dot_nt(h.astype(_MM), wt[...])
        row = pl.program_id(0) * rb + lax.broadcasted_iota(jnp.int32, (rb, 1), 0)
        z_ref[...] = jnp.where(row >= PAD, z, 0.0)

    return pl.pallas_call(
        body, name=name, grid=(t // rb,),
        in_specs=[_rows(rb, D), _full((2, D)), _ANY],
        out_specs=_rows(rb, D_IN),
        out_shape=jax.ShapeDtypeStruct((t, D_IN), jnp.float32),
        scratch_shapes=[pltpu.VMEM((D_IN, D), _MM), pltpu.SemaphoreType.DMA((4,))],
        compiler_params=_params(),
    )(xh, gb, wfull)


def _mix_in_bwd(name, dh_res, dz, xh, gb, wfull):
    t = xh.shape[0]
    rb = _row_block(t)

    def body(dhr_ref, dz_ref, xh_ref, gb_ref, w_ref, dh_ref, hb_ref, wt, sems):
        @pl.when(pl.program_id(0) == 0)
        def _():
            _load_rows(w_ref, OFF_WIN, WIN_ROWS, wt, sems)

        dh_ref[...] = dhr_ref[...] + _dot(dz_ref[...], wt[...])
        hb_ref[...] = (xh_ref[...] * gb_ref[0:1, :] + gb_ref[1:2, :]).astype(_MM)

    return pl.pallas_call(
        body, name=name, grid=(t // rb,),
        in_specs=[_rows(rb, D), _rows(rb, D_IN), _rows(rb, D), _full((2, D)), _ANY],
        out_specs=[_rows(rb, D), _rows(rb, D)],
        out_shape=[jax.ShapeDtypeStruct((t, D), jnp.float32), jax.ShapeDtypeStruct((t, D), _MM)],
        scratch_shapes=[pltpu.VMEM((D_IN, D), _MM), pltpu.SemaphoreType.DMA((4,))],
        compiler_params=_params(),
    )(dh_res, dz, xh, gb, wfull)


def _mix_out_fwd(name, xh, gb, ycat, wfull):
    t = xh.shape[0]
    rb = _row_block(t)

    def body(xh_ref, gb_ref, y_ref, w_ref, out_ref, rstd_ref, wo, sems):
        @pl.when(pl.program_id(0) == 0)
        def _():
            _load_rows(w_ref, OFF_WOUT, WOUT_ROWS, wo, sems)

        h = xh_ref[...] * gb_ref[0:1, :] + gb_ref[1:2, :]
        xo, rstd = _ln_fwd(ALPHA * h + _dot(y_ref[...], wo[...]))
        out_ref[...] = xo
        rstd_ref[...] = rstd

    return pl.pallas_call(
        body, name=name, grid=(t // rb,),
        in_specs=[_rows(rb, D), _full((2, D)), _rows(rb, D), _ANY],
        out_specs=[_rows(rb, D), _rows(rb, 1)],
        out_shape=[jax.ShapeDtypeStruct((t, D), jnp.float32), jax.ShapeDtypeStruct((t, 1), jnp.float32)],
        scratch_shapes=[pltpu.VMEM((D, D), _MM), pltpu.SemaphoreType.DMA((4,))],
        compiler_params=_params(),
    )(xh, gb, ycat, wfull)


def _mix_out_bwd(name, dy, xo, rstd, gb_out, ycat, wfull, gbuf):
    t = xo.shape[0]
    rb = _row_block(t)
    nblk = t // rb

    def body(dy_ref, xo_ref, rstd_ref, gbo_ref, y_ref, w_ref, g_in, dhr_ref, dyc_ref, dgb_ref, g_out,
             wo, acc, stage, sems, out_sems):
        i = pl.program_id(0)

        @pl.when(i == 0)
        def _():
            dgb_ref[...] = jnp.zeros_like(dgb_ref)
            acc[...] = jnp.zeros_like(acc)
            _load_rows(w_ref, OFF_WOUT, WOUT_ROWS, wo, sems)

        dy = dy_ref[...]
        xo = xo_ref[...]
        dgb_ref[0:1, :] += jnp.sum(dy * xo, axis=0, keepdims=True)
        dgb_ref[1:2, :] += jnp.sum(dy, axis=0, keepdims=True)
        ds = _ln_bwd(dy * gbo_ref[0:1, :], xo, rstd_ref[...])
        dsb = ds.astype(_MM)
        dhr_ref[...] = ALPHA * ds
        dyc_ref[...] = _dot_nt(dsb, wo[...])
        acc[...] += _dot_tn(y_ref[...], dsb)

        @pl.when(i == nblk - 1)
        def _():
            stage[...] = acc[...].astype(stage.dtype)
            cps = [pltpu.make_async_copy(stage.at[pl.ds(WOUT_ROWS * s, WOUT_ROWS)],
                                         g_out.at[s, pl.ds(OFF_WOUT, WOUT_ROWS)], out_sems.at[s])
                   for s in range(N_SHARD)]
            for cp in cps:
                cp.start()
            for cp in cps:
                cp.wait()

    return pl.pallas_call(
        body, name=name, grid=(nblk,),
        in_specs=[_rows(rb, D), _rows(rb, D), _rows(rb, 1), _full((2, D)), _rows(rb, D), _ANY, _ANY],
        out_specs=[_rows(rb, D), _rows(rb, D), _acc((8, D)), _ANY],
        out_shape=[jax.ShapeDtypeStruct((t, D), jnp.float32), jax.ShapeDtypeStruct((t, D), jnp.float32),
                   jax.ShapeDtypeStruct((8, D), jnp.float32), jax.ShapeDtypeStruct(gbuf.shape, gbuf.dtype)],
        input_output_aliases={6: 3},
        scratch_shapes=[pltpu.VMEM((D, D), _MM), pltpu.VMEM((D, D), jnp.float32), pltpu.VMEM((D, D), gbuf.dtype),
                        pltpu.SemaphoreType.DMA((4,)), pltpu.SemaphoreType.DMA((4,))],
        compiler_params=_params(),
    )(dy, xo, rstd, gb_out, ycat, wfull, gbuf)


def _loss_fwd_bwd(xh, gb, target):
    t = xh.shape[0]
    rb = _row_block(t)
    per = rb // ROW0

    def body(xh_ref, gb_ref, *refs):
        dy_ref, loss_ref = refs[per:]

        @pl.when(pl.program_id(0) == 0)
        def _():
            loss_ref[...] = jnp.zeros_like(loss_ref)

        y = xh_ref[...] * gb_ref[0:1, :] + gb_ref[1:2, :]
        tg = _join_parts(y[0:ROW0], refs[:per])
        err = y - tg
        dy_ref[...] = err * (1.0 / D)
        per_row = jnp.mean(err * err, axis=-1, keepdims=True)
        loss_ref[...] += 0.5 * jnp.sum(per_row, axis=0, keepdims=True)

    return pl.pallas_call(
        body, name="loss", grid=(t // rb,),
        in_specs=[_rows(rb, D), _full((2, D))] + _frame_parts(rb, D, (t - ROW0) // ROW0),
        out_specs=[_rows(rb, D), _acc((1, 1))],
        out_shape=[jax.ShapeDtypeStruct((t, D), jnp.float32), jax.ShapeDtypeStruct((1, 1), jnp.float32)],
        compiler_params=_params(),
    )(xh, gb, *[target] * per)


def _ln_in_bwd(dy, xh, rstd, gb):
    t = xh.shape[0]

    def body(dy_ref, xh_ref, rstd_ref, gb_ref, dx_ref, dhead_ref, dgb_ref):
        i = pl.program_id(0)

        @pl.when(i == 0)
        def _():
            dgb_ref[...] = jnp.zeros_like(dgb_ref)

        dy = dy_ref[...]
        xh = xh_ref[...]
        dgb_ref[0:1, :] += jnp.sum(dy * xh, axis=0, keepdims=True)
        dgb_ref[1:2, :] += jnp.sum(dy, axis=0, keepdims=True)
        dx = _ln_bwd(dy * gb_ref[0:1, :], xh, rstd_ref[...])

        @pl.when(i == 0)
        def _():
            dhead_ref[...] = dx

        @pl.when(i > 0)
        def _():
            dx_ref[...] = dx

    return pl.pallas_call(
        body, name="ln_in_bwd", grid=(t // ROW0,),
        in_specs=[_rows(ROW0, D), _rows(ROW0, D), _rows(ROW0, 1), _full((2, D))],
        out_specs=[_frames(D), _acc((ROW0, D)), _acc((8, D))],
        out_shape=[jax.ShapeDtypeStruct((t - ROW0, D), jnp.float32), jax.ShapeDtypeStruct((ROW0, D), jnp.float32),
                   jax.ShapeDtypeStruct((8, D), jnp.float32)],
        compiler_params=_params(),
    )(dy, xh, rstd, gb)


def _dw_rows(t, cols):
    for tt in (2080, 1664, 640, 128):
        vmem = 2 * tt * (cols + D) * 2 + cols * D * 6
        if t % tt == 0 and (t > 1024 or tt == 128) and vmem <= 44 * 1024 * 1024:
            return tt
    raise ValueError((t, cols))


def _dw_into(name, gpack, x, y, cols, pieces):
    t, k = x.shape
    tt = _dw_rows(t, cols)
    nt = t // tt

    def body(x_ref, y_ref, g_in, g_out, acc, stage, sems):
        j = pl.program_id(0)
        s = pl.program_id(1)

        @pl.when(s == 0)
        def _():
            acc[...] = jnp.zeros_like(acc)

        acc[...] += _dot_tn(x_ref[...], y_ref[...])

        @pl.when(s == nt - 1)
        def _():
            stage[...] = acc[...].astype(stage.dtype)
            cps = []
            for q, (lo, n, chip_of, off) in enumerate(pieces):
                cp = pltpu.make_async_copy(stage.at[pl.ds(lo, n)], g_out.at[chip_of(j), pl.ds(off, n)],
                                           sems.at[q])
                cp.start()
                cps.append(cp)
            for cp in cps:
                cp.wait()

    return pl.pallas_call(
        body, name=name, grid=(k // cols, nt),
        in_specs=[pl.BlockSpec((tt, cols), lambda j, s: (s, j)), pl.BlockSpec((tt, D), lambda j, s: (s, 0)), _ANY],
        out_specs=_ANY,
        out_shape=jax.ShapeDtypeStruct(gpack.shape, gpack.dtype),
        input_output_aliases={2: 0},
        scratch_shapes=[pltpu.VMEM((cols, D), jnp.float32), pltpu.VMEM((cols, D), gpack.dtype),
                        pltpu.SemaphoreType.DMA((len(pieces),))],
        compiler_params=_params(("arbitrary", "arbitrary")),
    )(x, y, gpack)


_TAIL_U = 32
_TAIL_X = 32
_MIX_ROWS = 320


def _decay_mask(rb, h):
    ii = lax.broadcasted_iota(jnp.int32, (rb, rb), 0)
    jj = lax.broadcasted_iota(jnp.int32, (rb, rb), 1)
    dist = jnp.abs(ii - jj).astype(jnp.float32)
    vis = (jj >> 6) <= (ii >> 6)
    return jnp.where(vis, jnp.exp(LOG_GAMMA[h] * dist), 0.0)


def _row_decays(rb, h):
    r = lax.broadcasted_iota(jnp.int32, (rb, DH), 0).astype(jnp.float32)
    return jnp.exp(LOG_GAMMA[h] * (r + 1.0)), jnp.exp(LOG_GAMMA[h] * (rb - 1.0 - r))


def _rope(x, cs, sn):
    return x * cs + pltpu.roll(x, DH // 2, 1) * sn


def _rope_t(dx, cs, sn):
    return dx * cs + pltpu.roll(dx * sn, DH // 2, 1)


def _pool_count(blk, rb):
    row = blk * rb + lax.broadcasted_iota(jnp.int32, (rb, D_POOL), 0) - PAD
    lane = lax.broadcasted_iota(jnp.int32, (rb, D_POOL), 1)
    win = jnp.left_shift(2, lane >> 6)
    return jnp.clip(row + 1, 1, win).astype(jnp.float32)


def _pool_select(p2, p4, p8, p16):
    lane = lax.broadcasted_iota(jnp.int32, p2.shape, 1)
    return jnp.where(lane < 64, p2, jnp.where(lane < 128, p4, jnp.where(lane < 192, p8, p16)))


def _trailing_windows(ext, p2, p4, p8, rb):
    n = _TAIL_X + rb
    p2[8:n, :] = ext[8:n, :] + ext[pl.ds(7, n - 8), :]
    p4[16:n, :] = p2[16:n, :] + p2[pl.ds(14, n - 16), :]
    p8[24:n, :] = p4[24:n, :] + p4[pl.ds(20, n - 24), :]
    lo = _TAIL_X
    p16 = p8[lo:n, :] + p8[lo - 8:n - 8, :]
    return _pool_select(p2[lo:n, :], p4[lo:n, :], p8[lo:n, :], p16)


def _leading_windows(ext, p2, p4, p8, rb):
    p2[0:rb + 24, :] = ext[0:rb + 24, :] + ext[pl.ds(1, rb + 24), :]
    p4[0:rb + 16, :] = p2[0:rb + 16, :] + p2[pl.ds(2, rb + 16), :]
    p8[0:rb + 8, :] = p4[0:rb + 8, :] + p4[pl.ds(4, rb + 8), :]
    p16 = p8[0:rb, :] + p8[8:rb + 8, :]
    return _pool_select(p2[0:rb, :], p4[0:rb, :], p8[0:rb, :], p16)


def _shifted_copies(ext, copies, first, n, sign):
    for b in range(1, 8):
        copies[b - 1, first:first + n, :] = ext[pl.ds(first - sign * b, n), :]


def _tap(ext, copies, k, start, rows, sign):
    a, b = divmod(k, 8)
    src = ext if b == 0 else copies.at[b - 1]
    return src[pl.ds(start - sign * 8 * a, rows), :]


def _sub_rows(rb):
    return 128 if rb % 128 == 0 else 64


def _mix_core_fwd(name, z, cs, sn, wbd, pscale, cdw, cvec, wpw, gn):
    t = z.shape[0]
    rb = _row_block(t, _MIX_ROWS)
    nblk = t // rb
    sr = _sub_rows(rb)

    def body(z_ref, cs_ref, sn_ref, wbd_ref, ps_ref, cdw_ref, cvec_ref, wpw_ref, gn_ref,
             y_ref, st_ref, ut_ref, cv_ref, yp_ref,
             uext, xext, cv, p2, p4, p8, ucopies, state, wmask):
        i = pl.program_id(0)

        @pl.when(i == 0)
        def _():
            state[...] = jnp.zeros_like(state)
            uext[0:_TAIL_U, :] = jnp.zeros((_TAIL_U, D_CONV), jnp.float32)
            xext[0:_TAIL_X, :] = jnp.zeros((_TAIL_X, D_POOL), jnp.float32)
            for h in range(HEADS):
                wmask[h] = _decay_mask(rb, h)

        st_ref[0] = state[...]
        ut_ref[0] = uext[0:_TAIL_U, :]

        xp = z_ref[:, 0:256]
        uext[_TAIL_U:_TAIL_U + rb, :] = z_ref[:, 256:512] * _sigmoid(z_ref[:, 512:768])
        xext[_TAIL_X:_TAIL_X + rb, :] = xp

        _shifted_copies(uext, ucopies, 8, rb + _TAIL_U - 8, 1)
        for r in range(0, rb, sr):
            acc = jnp.zeros((sr, D_CONV), jnp.float32)
            for k in range(CONV_W):
                acc = acc + _tap(uext, ucopies, k, _TAIL_U + r, sr, 1) * cdw_ref[CONV_W - 1 - k:CONV_W - k, :]
            cv[r:r + sr, :] = acc

        win = _trailing_windows(xext, p2, p4, p8, rb)
        ypb = (win / _pool_count(i, rb) - xp).astype(_MM)
        yp_ref[...] = ypb
        y_ref[:, 0:256] = (_dot(ypb, wbd_ref[...]) * ps_ref[...]).astype(_MM)
        cv_ref[...] = cv[...]
        cn, _ = _ln_fwd(cv[...] + cvec_ref[0:1, :])
        ln = cn * cvec_ref[1:2, :] + cvec_ref[2:3, :]
        sw = ln * _sigmoid(ln)
        y_ref[:, 256:512] = _dot(sw.astype(_MM), wpw_ref[...]).astype(_MM)
        csv = cs_ref[...]
        snv = sn_ref[...]
        for h in range(HEADS):
            q = _rope(z_ref[:, 768 + h * DH:768 + (h + 1) * DH], csv, snv)
            k = _rope(z_ref[:, 1280 + h * DH:1280 + (h + 1) * DH], csv, snv) * (DH ** -0.5)
            vb = z_ref[:, 1792 + h * DH:1792 + (h + 1) * DH].astype(_MM)
            g = z_ref[:, 2304 + h * DH:2304 + (h + 1) * DH]
            a, b = _row_decays(rb, h)
            s = _dot_nt(q.astype(_MM), k.astype(_MM)) * wmask[h]
            o = _dot(s.astype(_MM), vb) + _dot((q * a).astype(_MM), state[h].astype(_MM))
            state[h] = math.exp(LOG_GAMMA[h] * rb) * state[h] + _dot_tn((k * b).astype(_MM), vb)
            on, _ = _ln_fwd(o)
            y_ref[:, 512 + h * DH:512 + (h + 1) * DH] = (
                g * _sigmoid(g) * on * gn_ref[:, h * DH:(h + 1) * DH]).astype(_MM)

        uext[0:_TAIL_U, :] = uext[rb:rb + _TAIL_U, :]
        xext[0:_TAIL_X, :] = xext[rb:rb + _TAIL_X, :]

    return pl.pallas_call(
        body, name=name, grid=(nblk,),
        in_specs=[_rows(rb, D_IN), _rows(rb, DH), _rows(rb, DH), _full((256, 256)), _full((1, 256)),
                  _full((32, 256)), _full((8, 256)), _full((256, 256)), _full((1, D_RET))],
        out_specs=[_rows(rb, D),
                   pl.BlockSpec((1, HEADS, DH, DH), lambda i: (i, 0, 0, 0)),
                   pl.BlockSpec((1, _TAIL_U, D_CONV), lambda i: (i, 0, 0)),
                   _rows(rb, D_CONV), _rows(rb, D_POOL)],
        out_shape=[jax.ShapeDtypeStruct((t, D), _MM),
                   jax.ShapeDtypeStruct((nblk, HEADS, DH, DH), jnp.float32),
                   jax.ShapeDtypeStruct((nblk, _TAIL_U, D_CONV), jnp.float32),
                   jax.ShapeDtypeStruct((t, D_CONV), jnp.float32),
                   jax.ShapeDtypeStruct((t, D_POOL), _MM)],
        scratch_shapes=[pltpu.VMEM((rb + _TAIL_U, D_CONV), jnp.float32),
                        pltpu.VMEM((rb + _TAIL_X, D_POOL), jnp.float32),
                        pltpu.VMEM((rb, D_CONV), jnp.float32),
                        pltpu.VMEM((rb + _TAIL_X, D_POOL), jnp.float32),
                        pltpu.VMEM((rb + _TAIL_X, D_POOL), jnp.float32),
                        pltpu.VMEM((rb + _TAIL_X, D_POOL), jnp.float32),
                        pltpu.VMEM((7, rb + _TAIL_U, D_CONV), jnp.float32),
                        pltpu.VMEM((HEADS, DH, DH), jnp.float32),
                        pltpu.VMEM((HEADS, rb, rb), jnp.float32)],
        compiler_params=_params(),
    )(z, cs, sn, wbd, pscale, cdw, cvec, wpw, gn)


def _mix_core_bwd(name, z, dyc, cs, sn, st_in, ut_in, cv_in, yp_in, wbd, pscale, cdw, cvec, wpw, gn):
    t = z.shape[0]
    rb = _row_block(t, _MIX_ROWS)
    nblk = t // rb
    sr = _sub_rows(rb)
    rev = lambda i: nblk - 1 - i

    def body(z_ref, dy_ref, cs_ref, sn_ref, st_ref, ut_ref, cv_ref, yp_ref,
             wbd_ref, ps_ref, cdw_ref, cvec_ref, wpw_ref, gn_ref,
             dz_ref, dwbd_ref, dwpw_ref, dcdw_ref, dsm_ref,
             uext, cv, dcvext, eext, p2, p4, p8, ucopies, dcopies, dstate, wmask):
        i = pl.program_id(0)
        blk = nblk - 1 - i

        @pl.when(i == 0)
        def _():
            dstate[...] = jnp.zeros_like(dstate)
            dcvext[rb:rb + _TAIL_U, :] = jnp.zeros((_TAIL_U, D_CONV), jnp.float32)
            eext[rb:rb + _TAIL_X, :] = jnp.zeros((_TAIL_X, D_POOL), jnp.float32)
            dwbd_ref[...] = jnp.zeros_like(dwbd_ref)
            dwpw_ref[...] = jnp.zeros_like(dwpw_ref)
            dcdw_ref[...] = jnp.zeros_like(dcdw_ref)
            dsm_ref[...] = jnp.zeros_like(dsm_ref)
            for h in range(HEADS):
                wmask[h] = _decay_mask(rb, h)

        row = blk * rb + lax.broadcasted_iota(jnp.int32, (rb, 1), 0)
        live = row >= PAD

        ca = z_ref[:, 256:512]
        sg_c = _sigmoid(z_ref[:, 512:768])
        uext[0:_TAIL_U, :] = ut_ref[0]
        uext[_TAIL_U:_TAIL_U + rb, :] = ca * sg_c

        cnt = _pool_count(blk, rb)
        ypb = yp_ref[...]
        dyp = dy_ref[:, 0:256]
        pm = _dot(ypb, wbd_ref[...])
        dsm_ref[1:2, 0:256] += jnp.sum(dyp * pm, axis=0, keepdims=True)
        dpm = (dyp * ps_ref[...]).astype(_MM)
        dwbd_ref[...] += _dot_tn(ypb, dpm)
        dypre = _dot_nt(dpm, wbd_ref[...])
        eext[0:rb, :] = dypre / cnt
        win = _leading_windows(eext, p2, p4, p8, rb)
        dz_ref[:, 0:256] = jnp.where(live, win - dypre, 0.0).astype(_MM)

        cn, rstd_c = _ln_fwd(cv_ref[...] + cvec_ref[0:1, :])
        ln = cn * cvec_ref[1:2, :] + cvec_ref[2:3, :]
        sg_l = _sigmoid(ln)
        swb = (ln * sg_l).astype(_MM)
        dycb = dy_ref[:, 256:512].astype(_MM)
        dwpw_ref[...] += _dot_tn(swb, dycb)
        dln = _dot_nt(dycb, wpw_ref[...]) * (sg_l * (1.0 + ln * (1.0 - sg_l)))
        dsm_ref[3:4, 0:256] += jnp.sum(dln * cn, axis=0, keepdims=True)
        dsm_ref[4:5, 0:256] += jnp.sum(dln, axis=0, keepdims=True)
        dcv = _ln_bwd(dln * cvec_ref[1:2, :], cn, rstd_c)
        dsm_ref[2:3, 0:256] += jnp.sum(dcv, axis=0, keepdims=True)
        dcvext[0:rb, :] = dcv
        _shifted_copies(uext, ucopies, 8, rb + _TAIL_U - 8, 1)
        _shifted_copies(dcvext, dcopies, 0, rb + _TAIL_U - 8, -1)
        for k in range(CONV_W):
            prod = dcv * _tap(uext, ucopies, k, _TAIL_U, rb, 1)
            dcdw_ref[CONV_W - 1 - k:CONV_W - k, :] += jnp.sum(prod, axis=0, keepdims=True)
        for r in range(0, rb, sr):
            acc = jnp.zeros((sr, D_CONV), jnp.float32)
            for k in range(CONV_W):
                acc = acc + _tap(dcvext, dcopies, k, r, sr, -1) * cdw_ref[CONV_W - 1 - k:CONV_W - k, :]
            cv[r:r + sr, :] = acc
        du = cv[...]
        dz_ref[:, 256:512] = jnp.where(live, du * sg_c, 0.0).astype(_MM)
        dz_ref[:, 512:768] = jnp.where(live, du * ca * sg_c * (1.0 - sg_c), 0.0).astype(_MM)

        csv = cs_ref[...]
        snv = sn_ref[...]
        for h in range(HEADS):
            q = _rope(z_ref[:, 768 + h * DH:768 + (h + 1) * DH], csv, snv)
            k = _rope(z_ref[:, 1280 + h * DH:1280 + (h + 1) * DH], csv, snv) * (DH ** -0.5)
            vb = z_ref[:, 1792 + h * DH:1792 + (h + 1) * DH].astype(_MM)
            g = z_ref[:, 2304 + h * DH:2304 + (h + 1) * DH]
            a, b = _row_decays(rb, h)
            qb = q.astype(_MM)
            kb = k.astype(_MM)
            qab = (q * a).astype(_MM)
            kbb = (k * b).astype(_MM)
            stb = st_ref[0, h].astype(_MM)
            sb = (_dot_nt(qb, kb) * wmask[h]).astype(_MM)
            o = _dot(sb, vb) + _dot(qab, stb)
            on, rstd_o = _ln_fwd(o)
            gnv = gn_ref[:, h * DH:(h + 1) * DH]
            sg_g = _sigmoid(g)
            si_g = g * sg_g
            dyr = dy_ref[:, 512 + h * DH:512 + (h + 1) * DH]
            dsm_ref[0:1, h * DH:(h + 1) * DH] += jnp.sum(dyr * on * si_g, axis=0, keepdims=True)
            dgate = dyr * on * gnv * (sg_g * (1.0 + g * (1.0 - sg_g)))
            dob = _ln_bwd(dyr * gnv * si_g, on, rstd_o).astype(_MM)
            dstb = dstate[h].astype(_MM)
            dsb = (_dot_nt(dob, vb) * wmask[h]).astype(_MM)
            dq = _dot(dsb, kb) + _dot_nt(dob, stb) * a
            dk = _dot_tn(dsb, qb) + _dot_nt(vb, dstb) * b
            dv = _dot_tn(sb, dob) + _dot(kbb, dstb)
            dstate[h] = math.exp(LOG_GAMMA[h] * rb) * dstate[h] + _dot_tn(qab, dob)
            dz_ref[:, 768 + h * DH:768 + (h + 1) * DH] = jnp.where(live, _rope_t(dq, csv, snv), 0.0).astype(_MM)
            dz_ref[:, 1280 + h * DH:1280 + (h + 1) * DH] = jnp.where(
                live, _rope_t(dk * (DH ** -0.5), csv, snv), 0.0).astype(_MM)
            dz_ref[:, 1792 + h * DH:1792 + (h + 1) * DH] = jnp.where(live, dv, 0.0).astype(_MM)
            dz_ref[:, 2304 + h * DH:2304 + (h + 1) * DH] = jnp.where(live, dgate, 0.0).astype(_MM)

        dcvext[rb:rb + _TAIL_U, :] = dcvext[0:_TAIL_U, :]
        eext[rb:rb + _TAIL_X, :] = eext[0:_TAIL_X, :]

    rrows = lambda n: pl.BlockSpec((rb, n), lambda i: (rev(i), 0))
    return pl.pallas_call(
        body, name=name, grid=(nblk,),
        in_specs=[rrows(D_IN), rrows(D), rrows(DH), rrows(DH),
                  pl.BlockSpec((1, HEADS, DH, DH), lambda i: (rev(i), 0, 0, 0)),
                  pl.BlockSpec((1, _TAIL_U, D_CONV), lambda i: (rev(i), 0, 0)),
                  rrows(D_CONV), rrows(D_POOL),
                  _full((256, 256)), _full((1, 256)), _full((32, 256)), _full((8, 256)), _full((256, 256)),
                  _full((1, D_RET))],
        out_specs=[rrows(D_IN), _acc((256, 256)), _acc((256, 256)), _acc((32, 256)), _acc((8, 512))],
        out_shape=[jax.ShapeDtypeStruct((t, D_IN), _MM),
                   jax.ShapeDtypeStruct((256, 256), jnp.float32), jax.ShapeDtypeStruct((256, 256), jnp.float32),
                   jax.ShapeDtypeStruct((32, 256), jnp.float32), jax.ShapeDtypeStruct((8, 512), jnp.float32)],
        scratch_shapes=[pltpu.VMEM((rb + _TAIL_U, D_CONV), jnp.float32),
                        pltpu.VMEM((rb, D_CONV), jnp.float32),
                        pltpu.VMEM((rb + _TAIL_U, D_CONV), jnp.float32),
                        pltpu.VMEM((rb + _TAIL_X, D_POOL), jnp.float32),
                        pltpu.VMEM((rb + _TAIL_X, D_POOL), jnp.float32),
                        pltpu.VMEM((rb + _TAIL_X, D_POOL), jnp.float32),
                        pltpu.VMEM((rb + _TAIL_X, D_POOL), jnp.float32),
                        pltpu.VMEM((7, rb + _TAIL_U, D_CONV), jnp.float32),
                        pltpu.VMEM((7, rb + _TAIL_U, D_CONV), jnp.float32),
                        pltpu.VMEM((HEADS, DH, DH), jnp.float32),
                        pltpu.VMEM((HEADS, rb, rb), jnp.float32)],
        compiler_params=_params(),
    )(z, dyc, cs, sn, st_in, ut_in, cv_in, yp_in, wbd, pscale, cdw, cvec, wpw, gn)


def _me():
    return lax.axis_index("x"), lax.axis_index("y"), lax.axis_index("c")


def _flip(me, mask):
    return tuple(1 - m if f else m for m, f in zip(me, mask))


def _push(name, aliased, inputs, fresh, remote):
    n_al, n_in, n_out, n_rem = len(aliased), len(inputs), len(fresh), len(remote)

    def body(*refs):
        ins = refs[n_al:n_al + n_in]
        al = refs[n_al + n_in:2 * n_al + n_in]
        outs = refs[2 * n_al + n_in:2 * n_al + n_in + n_out]
        send_sems, recv_sems = refs[2 * n_al + n_in + n_out:]
        me = _me()
        copies = []
        for k, (mask, src_fn, dst_fn) in enumerate(remote):
            cp = pltpu.make_async_remote_copy(
                src_ref=src_fn(al, ins, outs, me), dst_ref=dst_fn(al, ins, outs, me),
                send_sem=send_sems.at[k], recv_sem=recv_sems.at[k],
                device_id=_flip(me, mask), device_id_type=MESH)
            cp.start()
            copies.append(cp)
        for cp in copies:
            cp.wait()

    return pl.pallas_call(
        body, name=name,
        in_specs=[_ANY] * (n_al + n_in), out_specs=[_ANY] * (n_al + n_out),
        out_shape=[jax.ShapeDtypeStruct(a.shape, a.dtype) for a in aliased] + list(fresh),
        input_output_aliases={i: i for i in range(n_al)},
        scratch_shapes=[pltpu.SemaphoreType.DMA((n_rem,)), pltpu.SemaphoreType.DMA((n_rem,))],
    )(*aliased, *inputs)


_HBM = pl.BlockSpec(memory_space=pltpu.HBM)
_SEM = pl.BlockSpec(memory_space=pltpu.SEMAPHORE)
_EFFECT = pltpu.SideEffectType.DATAFLOW_SIDE_EFFECTING


def _push_start(name, bufs, remote):
    n, n_rem = len(bufs), len(remote)

    def body(*refs):
        ins = refs[:n]
        send_sems, recv_sems = refs[n], refs[n + 1]
        token = refs[2 * n + 2]
        me = _me()
        for k, (mask, src_fn, dst_fn) in enumerate(remote):
            pltpu.make_async_remote_copy(
                src_ref=src_fn(ins, me), dst_ref=dst_fn(ins, me),
                send_sem=send_sems.at[k], recv_sem=recv_sems.at[k],
                device_id=_flip(me, mask), device_id_type=MESH).start()
        token[...] = jnp.zeros_like(token)

    out = pl.pallas_call(
        body, name=name,
        out_shape=(pltpu.SemaphoreType.DMA((n_rem,)), pltpu.SemaphoreType.DMA((n_rem,)),
                   *[pltpu.HBM(b.shape, b.dtype) for b in bufs], jax.ShapeDtypeStruct((8, 128), jnp.float32)),
        in_specs=[_HBM] * n,
        out_specs=(_SEM, _SEM, *[_HBM] * n, pl.BlockSpec(memory_space=pltpu.VMEM)),
        input_output_aliases={i: i + 2 for i in range(n)},
        compiler_params=pltpu.CompilerParams(has_side_effects=_EFFECT),
    )(*[pltpu.with_memory_space_constraint(b, pltpu.HBM) for b in bufs])
    return out[0], out[1], list(out[2:2 + n]), out[2 + n]


def _push_wait(name, send_sems, recv_sems, bufs, after, remote):
    n = len(bufs)

    def body(*refs):
        ins = refs[:n]
        s_sems, r_sems = refs[n], refs[n + 1]
        me = _me()
        for k, (mask, src_fn, dst_fn) in enumerate(remote):
            cp = pltpu.make_async_remote_copy(
                src_ref=src_fn(ins, me), dst_ref=dst_fn(ins, me),
                send_sem=s_sems.at[k], recv_sem=r_sems.at[k],
                device_id=_flip(me, mask), device_id_type=MESH)
            cp.wait_send()
            cp.wait_recv()

    out = pl.pallas_call(
        body, name=name,
        out_shape=tuple(pltpu.HBM(b.shape, b.dtype) for b in bufs),
        in_specs=[_HBM] * n + [_SEM, _SEM, _ANY], out_specs=tuple([_HBM] * n),
        input_output_aliases={i: i for i in range(n)},
        compiler_params=pltpu.CompilerParams(has_side_effects=_EFFECT),
    )(*bufs, send_sems, recv_sems, after)
    return list(out)


_ICI_MASKS = ((0, 1, 0), (1, 0, 0), (1, 1, 0))
_D2D_MASK = (0, 0, 1)
_ALL_MASKS = tuple((a, b, c) for a in (0, 1) for b in (0, 1) for c in (0, 1))[1:]


def _chip(me):
    return 2 * me[0] + me[1]


def _half(me, rows):
    return pl.ds(me[2] * (rows // 2), rows // 2)


def _other_half(me, rows):
    return pl.ds((1 - me[2]) * (rows // 2), rows // 2)


def _sum_block(rh):
    return next((b for b in (768, 640, 512, 128) if rh % b == 0), rh)


def _own_slot(mine):
    chip = _chip(_me())
    return lax.dynamic_update_slice(lax.empty((N_SHARD,) + mine.shape, mine.dtype), mine[None],
                                    (chip,) + (0,) * mine.ndim)


def _gather_ici_plan(rows, with_small):
    remote = []
    for mask in _ICI_MASKS:
        for b, r in enumerate(rows):
            mine = lambda bufs, me, b=b, r=r: bufs[b].at[_chip(me), _half(me, r)]
            remote.append((mask, mine, mine))
        if with_small:
            mine_small = lambda bufs, me: bufs[len(rows)].at[_chip(me)]
            remote.append((mask, mine_small, mine_small))
    return remote


def _gather_d2d(name, ws):
    remote = []
    for b, w in enumerate(ws):
        for j in range(1, N_SHARD):
            theirs = lambda al, ins, outs, me, j=j, b=b, r=w.shape[1]: al[b].at[(_chip(me) + j) % N_SHARD, _half(me, r)]
            remote.append((_D2D_MASK, theirs, theirs))
    return _push(name, list(ws), [], [], remote)


def _sum_pair(name, g, recv):
    _, r, _ = g.shape
    rb = _sum_block(r // 2)
    nb = r // 2 // rb
    c = lax.axis_index("c").astype(jnp.int32).reshape(1)

    def body(c_ref, g_ref, r_ref, o_ref):
        o_ref[...] = (g_ref[...].astype(jnp.float32) + r_ref[...].astype(jnp.float32)).astype(o_ref.dtype)

    return pl.pallas_call(
        body, name=name,
        grid_spec=pltpu.PrefetchScalarGridSpec(
            num_scalar_prefetch=1, grid=(N_SHARD, nb),
            in_specs=[pl.BlockSpec((None, rb, D), lambda s, i, c_ref: (s, c_ref[0] * nb + i, 0)),
                      pl.BlockSpec((None, rb, D), lambda s, i, c_ref: (s, i, 0))],
            out_specs=pl.BlockSpec((None, rb, D), lambda s, i, c_ref: (s, i, 0))),
        out_shape=jax.ShapeDtypeStruct((N_SHARD, r // 2, D), g.dtype),
        compiler_params=_params(("arbitrary", "arbitrary")),
    )(c, g, recv)


def _sum_chips(name, p, recv):
    _, rh, _ = p.shape
    rb = _sum_block(rh)
    nb = rh // rb
    s = jnp.stack([2 * lax.axis_index("x") + lax.axis_index("y"), lax.axis_index("c")]).astype(jnp.int32)

    def body(s_ref, p_ref, r_ref, o_ref):
        acc = p_ref[...].astype(jnp.float32)
        for j in range(3):
            acc = acc + r_ref[j].astype(jnp.float32)
        o_ref[...] = acc

    return pl.pallas_call(
        body, name=name,
        grid_spec=pltpu.PrefetchScalarGridSpec(
            num_scalar_prefetch=1, grid=(nb,),
            in_specs=[pl.BlockSpec((None, rb, D), lambda i, s_ref: (s_ref[0], i, 0)),
                      pl.BlockSpec((3, rb, D), lambda i, s_ref: (0, i, 0))],
            out_specs=pl.BlockSpec((rb, D), lambda i, s_ref: (s_ref[1] * nb + i, 0))),
        out_shape=jax.ShapeDtypeStruct((2 * rh, D), jnp.float32),
        compiler_params=_params(),
    )(s, p, recv)


def _rs_ici_plan():
    remote = []
    for j, mask in enumerate(_ICI_MASKS):
        remote.append((mask,
                       lambda bufs, me, mask=mask: bufs[0].at[_chip(_flip(me, mask))],
                       lambda bufs, me, j=j: bufs[1].at[j]))
    return remote


def _rs_pair(tag, g):
    _, r, _ = g.shape
    remote = [(_D2D_MASK,
               lambda al, ins, outs, me, s=s: ins[0].at[s, _other_half(me, r)],
               lambda al, ins, outs, me, s=s: outs[0].at[s]) for s in range(N_SHARD)]
    (recv,) = _push("rs_d2d_" + tag, [], [g], [jax.ShapeDtypeStruct((N_SHARD, r // 2, D), g.dtype)], remote)
    return _sum_pair("rs_sum_pair_" + tag, g, recv)


def _rs_d2d_plan(r):
    return [(_D2D_MASK,
             lambda bufs, me, s=s: bufs[0].at[s, _other_half(me, r)],
             lambda bufs, me, s=s: bufs[1].at[s]) for s in range(N_SHARD)]


def _rs_d2d_start(tag, g):
    r = g.shape[1]
    landing = lax.empty((N_SHARD, r // 2, D), g.dtype)
    send, recv, flying, token = _push_start("rs_d2d_" + tag + "_start", [g, landing], _rs_d2d_plan(r))
    return (tag, r, send, recv, flying), token[0, 0]


def _rs_start_after(exchanged, after, rs_plan):
    tag, r, send, recv, flying = exchanged
    g, landed = _push_wait("rs_d2d_" + tag + "_wait", send, recv, flying, after, _rs_d2d_plan(r))
    p = _sum_pair("rs_sum_pair_" + tag, g, landed)
    landing = lax.empty((3, p.shape[1], D), p.dtype)
    send, recv, flying, token = _push_start("rs_ici_" + tag + "_start", [p, landing], rs_plan)
    return (tag, send, recv, flying), token[0, 0]


def _rs_start(tag, g, rs_plan):
    p = _rs_pair(tag, g)
    landing = lax.empty((3, p.shape[1], D), p.dtype)
    send, recv, flying, token = _push_start("rs_ici_" + tag + "_start", [p, landing], rs_plan)
    return (tag, send, recv, flying), token[0, 0]


def _rs_end(started, after, rs_plan):
    tag, send, recv, flying = started
    p, recv3 = _push_wait("rs_ici_" + tag + "_wait", send, recv, flying, after, rs_plan)
    return _sum_chips("rs_sum_chips_" + tag, p, recv3)


def _rs_share(mines):
    remote = []
    for b, m in enumerate(mines):
        half = lambda al, ins, outs, me, b=b, r=m.shape[0]: al[b].at[_half(me, r)]
        remote.append((_D2D_MASK, half, half))
    return _push("rs_share", list(mines), [], [], remote)


def _small_plan():
    slot = lambda bufs, me: bufs[0].at[4 * me[0] + 2 * me[1] + me[2]]
    return [(mask, slot, slot) for mask in _ALL_MASKS]


def _small_start(v):
    me = _me()
    every = lax.dynamic_update_slice(lax.empty((8,) + v.shape, jnp.float32), v[None],
                                     (4 * me[0] + 2 * me[1] + me[2], 0, 0))
    send, recv, flying, token = _push_start("small_all_start", [every], _small_plan())
    return (send, recv, flying), token


def _small_end(started, after):
    send, recv, flying = started
    (every,) = _push_wait("small_all_wait", send, recv, flying, after, _small_plan())
    s = every.shape[1]

    def body(e_ref, o_ref):
        acc = e_ref[0]
        for j in range(1, 8):
            acc = acc + e_ref[j]
        o_ref[...] = acc

    return pl.pallas_call(
        body, name="small_sum", grid=(1,),
        in_specs=[pl.BlockSpec((8, s, D), lambda i: (0, 0, 0))],
        out_specs=pl.BlockSpec((s, D), lambda i: (0, 0)),
        out_shape=jax.ShapeDtypeStruct((s, D), jnp.float32),
        compiler_params=_params(),
    )(every)


def _adamw(name, w, g, m, v):
    r, c = w.shape
    rb = next(b for b in (256, 344, 128, 64, 32, 16, 8, r) if r % b == 0)

    def body(w_ref, g_ref, m_ref, v_ref, d_ref, mo_ref, vo_ref):
        g = g_ref[...]
        m = ADAM_B1 * m_ref[...] + (1.0 - ADAM_B1) * g
        v = ADAM_B2 * v_ref[...] + (1.0 - ADAM_B2) * (g * g)
        m_hat = m / (1.0 - ADAM_B1 ** ADAM_STEP)
        v_hat = v / (1.0 - ADAM_B2 ** ADAM_STEP)
        d_ref[...] = -ADAM_LR * (m_hat / (jnp.sqrt(v_hat) + ADAM_EPS) + ADAM_WD * w_ref[...])
        mo_ref[...] = m
        vo_ref[...] = v

    spec = pl.BlockSpec((rb, c), lambda i: (i, 0))
    return pl.pallas_call(
        body, name=name, grid=(r // rb,),
        in_specs=[spec] * 4, out_specs=[spec] * 3,
        out_shape=[jax.ShapeDtypeStruct((r, c), jnp.float32)] * 3,
        compiler_params=_params(),
    )(w, g, m, v)


_BIG = ("ffn1_w13", "ffn2_w13", "ffn1_w2", "ffn2_w2", "w_in", "w_out", "conv_pw")


def _pack_rows(parts):
    flat = jnp.concatenate([p.reshape(-1) for p in parts])
    pad = (-flat.shape[0]) % (8 * D)
    if pad:
        flat = jnp.concatenate([flat, jnp.zeros((pad,), flat.dtype)])
    return flat.reshape(-1, D)


def _unpack_rows(buf, shapes):
    flat = buf.reshape(-1)
    out, off = [], 0
    for shp in shapes:
        n = math.prod(shp)
        out.append(flat[off:off + n].reshape(shp))
        off += n
    return out


def _pack_shard(parts):
    zeros = lambda n: jnp.zeros((n, D), parts["w_out"].dtype)
    a = jnp.concatenate([parts["ffn1_w13"].T, parts["ffn1_w2"], zeros(A_ROWS - OFF_WIN)], axis=0)
    b = jnp.concatenate([parts["ffn2_w13"].T, parts["ffn2_w2"], parts["w_in"].T, parts["w_out"],
                         parts["conv_pw"].reshape(PW_ROWS, D), zeros(B_ROWS - OFF_PW - PW_ROWS)], axis=0)
    return a, b


def _unpack_shard(a, b):
    return {"ffn1_w13": a[OFF_W13:OFF_W13 + W13_ROWS].T, "ffn1_w2": a[OFF_W2:OFF_W2 + W2_ROWS],
            "ffn2_w13": b[OFF_W13:OFF_W13 + W13_ROWS].T, "ffn2_w2": b[OFF_W2:OFF_W2 + W2_ROWS],
            "w_in": b[OFF_WIN:OFF_WIN + WIN_ROWS].T, "w_out": b[OFF_WOUT:OFF_WOUT + WOUT_ROWS],
            "conv_pw": b[OFF_PW:OFF_PW + PW_ROWS].reshape(64, 256)}


def kernel(x, meta, ln_in_g, ln_in_b, ffn1_w13, ffn1_w2, w_in, pool_w, pool_scale, conv_dw, conv_db, conv_ln_g, conv_ln_b, conv_pw, ret_gn_g, w_out, ffn2_w13, ffn2_w2, ln_g, ln_b, loss_target, m_meta, m_ln_in_g, m_ln_in_b, m_ffn1_w13, m_ffn1_w2, m_w_in, m_pool_w, m_pool_scale, m_conv_dw, m_conv_db, m_conv_ln_g, m_conv_ln_b, m_conv_pw, m_ret_gn_g, m_w_out, m_ffn2_w13, m_ffn2_w2, m_ln_g, m_ln_b, v_meta, v_ln_in_g, v_ln_in_b, v_ffn1_w13, v_ffn1_w2, v_w_in, v_pool_w, v_pool_scale, v_conv_dw, v_conv_db, v_conv_ln_g, v_conv_ln_b, v_conv_pw, v_ret_gn_g, v_w_out, v_ffn2_w13, v_ffn2_w2, v_ln_g, v_ln_b):
    f32 = jnp.float32
    seq = x.shape[1]
    t = seq + ROW0
    me = _me()
    chip = _chip(me)
    big_w = {"ffn1_w13": ffn1_w13, "ffn2_w13": ffn2_w13, "ffn1_w2": ffn1_w2, "ffn2_w2": ffn2_w2,
             "w_in": w_in, "w_out": w_out, "conv_pw": conv_pw}

    wa, wb = [], []
    for l in range(DEPTH):
        a, b = _pack_shard({n: big_w[n][l].astype(_WIRE) for n in _BIG})
        wa.append(_own_slot(a))
        wb.append(_own_slot(b))
    small_shapes = [(N_META, 256), (DEPTH, CONV_W, 64), (DEPTH, 3, 256), (DEPTH, 3, 256)]
    small_all = _own_slot(_pack_rows([meta, conv_dw, ln_g, ln_b]))
    wrap = lambda f: (lambda al, ins, outs, me: f(al, me))
    (small_all,) = _push("gather_small", [small_all], [], [],
                         [(m, wrap(s), wrap(d)) for m, s, d in _gather_ici_plan([], True)])
    plan_a0 = _gather_ici_plan([A_ROWS], False)
    plan_b0 = _gather_ici_plan([B_ROWS], False)
    plan_l1 = _gather_ici_plan([A_ROWS, B_ROWS], False)
    a0_send, a0_recv, a0_flying, a0_token = _push_start("gather_ici_a0_start", [wa[0]], plan_a0)

    sm = [_unpack_rows(small_all[s], small_shapes) for s in range(N_SHARD)]
    meta_f = jnp.concatenate([sm[s][0] for s in range(N_SHARD)], axis=1) + a0_token[0, 0]
    cdw_f = jnp.concatenate([sm[s][1] for s in range(N_SHARD)], axis=2)
    lng_f = jnp.concatenate([sm[s][2] for s in range(N_SHARD)], axis=2)
    lnb_f = jnp.concatenate([sm[s][3] for s in range(N_SHARD)], axis=2)

    def mix_params(l):
        wbd = jnp.zeros((D_POOL, D_POOL), f32)
        for g in range(4):
            wbd = wbd.at[64 * g:64 * (g + 1), 64 * g:64 * (g + 1)].set(pool_w[l, g])
        cdw = jnp.pad(cdw_f[l], ((0, 1), (0, 0)))
        cvec = jnp.pad(jnp.stack([conv_db[l], conv_ln_g[l], conv_ln_b[l]]), ((0, 5), (0, 0)))
        wpw = wb[l][:, OFF_PW:OFF_PW + PW_ROWS].reshape(D_CONV, D_CONV)
        return (wbd.astype(_MM), pool_scale[l][None], cdw, cvec, wpw, ret_gn_g[l][None])

    gb_of = lambda l, i: jnp.stack([lng_f[l, i], lnb_f[l, i]])
    gb_in = jnp.stack([ln_in_g, ln_in_b])

    pos = jnp.arange(t, dtype=f32) - PAD
    inv_freq = ROPE_BASE ** (-jnp.arange(0, DH, 2, dtype=f32) / DH)
    ang = pos[:, None] * inv_freq[None, :]
    cs = jnp.concatenate([jnp.cos(ang), jnp.cos(ang)], axis=1)
    sn = jnp.concatenate([-jnp.sin(ang), jnp.sin(ang)], axis=1)

    xh, rstd = _ln_in_fwd(jnp.concatenate([jnp.zeros((PAD, D), f32), meta_f], axis=0), x[0])
    (wa[0],) = _push_wait("gather_ici_a0_wait", a0_send, a0_recv, a0_flying, xh, plan_a0)
    (wa[0],) = _gather_d2d("gather_d2d_a0", [wa[0]])
    b0_send, b0_recv, b0_flying, b0_token = _push_start("gather_ici_b0_start", [wb[0]], plan_b0)
    cur = (xh, rstd, gb_in + b0_token[0, 0])
    saved = []
    for l in range(DEPTH):
        if l == 1:
            wa[1], wb[1] = _push_wait("gather_ici_l1_wait", l1_send, l1_recv, l1_flying, cur[0], plan_l1)
            wa[1], wb[1] = _gather_d2d("gather_d2d_l1", [wa[1], wb[1]])
        a0 = cur
        xh1, r1, au1, hb1 = _ffn_fwd(f"ffn1_fwd_{l}", a0[0], a0[2], wa[l])
        a1 = (xh1, r1, gb_of(l, 0))
        if l == 0:
            (wb[0],) = _push_wait("gather_ici_b0_wait", b0_send, b0_recv, b0_flying, xh1, plan_b0)
            (wb[0],) = _gather_d2d("gather_d2d_b0", [wb[0]])
            l1_send, l1_recv, l1_flying, l1_token = _push_start("gather_ici_l1_start", [wa[1], wb[1]], plan_l1)
            a1 = (xh1, r1, a1[2] + l1_token[0, 0])
        z = _mix_in_fwd(f"mix_in_fwd_{l}", a1[0], a1[2], wb[l])
        mp = mix_params(l)
        ycat, st_in, ut_in, cv_in, yp_in = _mix_core_fwd(f"mix_core_fwd_{l}", z, cs, sn, *mp)
        mix_kept = (cv_in, yp_in)
        xh2, r2 = _mix_out_fwd(f"mix_out_fwd_{l}", a1[0], a1[2], ycat, wb[l])
        a2 = (xh2, r2, gb_of(l, 1))
        xh3, r3, au2, hb2 = _ffn_fwd(f"ffn2_fwd_{l}", a2[0], a2[2], wb[l])
        a3 = (xh3, r3, gb_of(l, 2))
        saved.append((a0, a1, a2, a3, z, ycat, st_in, ut_in, mix_kept, mp, au1, hb1, au2, hb2))
        cur = a3

    dy, loss_part = _loss_fwd_bwd(cur[0], cur[2], loss_target[0])
    loss = lax.psum(loss_part[0, 0], ("x", "y", "c"))

    g_ln_g = [[None] * 3 for _ in range(DEPTH)]
    g_ln_b = [[None] * 3 for _ in range(DEPTH)]
    g_small = [dict() for _ in range(DEPTH)]
    slot = lambda j: j

    def ffn_grads(gbuf, tag, l, hb, hid, dau, dffn):
        gbuf = _dw_into(f"dw13_{tag}_{l}", gbuf, dau, hb, FF_SLOT, [(0, W13_ROWS, slot, OFF_W13)])
        return _dw_into(f"dw2_{tag}_{l}", gbuf, hid, dffn, FF_SLOT,
                        [(0, W2_ROWS, lambda j: 2 * j, OFF_W2), (W2_ROWS, W2_ROWS, lambda j: 2 * j + 1, OFF_W2)])

    rs_plan = _rs_ici_plan()
    token = jnp.zeros((), f32)
    started, exchanging_a = {}, None
    for l in reversed(range(DEPTH)):
        a0, a1, a2, a3, z, ycat, st_in, ut_in, mix_kept, mp, au1, hb1, au2, hb2 = saved[l]
        g_a = lax.empty((N_SHARD, A_ROWS, D), _WIRE)
        g_b = lax.empty((N_SHARD, B_ROWS, D), _WIRE)
        dh, hid, dau, dffn, dgb = _ffn_bwd(f"ffn2_bwd_{l}", dy, a3[0], a3[1], a3[2] + token, au2, wb[l])
        g_ln_g[l][2], g_ln_b[l][2] = dgb[0], dgb[1]
        g_b = ffn_grads(g_b, "ffn2", l, hb2, hid, dau, dffn)
        token = jnp.zeros((), f32)
        if exchanging_a is not None:
            started["a", l + 1], token = _rs_start_after(exchanging_a, dh, rs_plan)
        dh_res, dycat, dgb, g_b = _mix_out_bwd(f"mix_out_bwd_{l}", dh, a2[0], a2[1], a2[2] + token, ycat, wb[l], g_b)
        g_ln_g[l][1], g_ln_b[l][1] = dgb[0], dgb[1]
        dz, dwbd, dwpw, dcdw, dsm = _mix_core_bwd(f"mix_core_bwd_{l}", z, dycat, cs, sn, st_in, ut_in, *mix_kept, *mp)
        pw = jnp.concatenate([dwpw.astype(_WIRE).reshape(N_SHARD, PW_ROWS, D),
                              jnp.zeros((N_SHARD, B_ROWS - OFF_PW - PW_ROWS, D), _WIRE)], axis=1)
        g_b = lax.dynamic_update_slice(g_b, pw, (0, OFF_PW, 0))
        g_small[l] = dict(
            pool_w=jnp.stack([dwbd[64 * g:64 * (g + 1), 64 * g:64 * (g + 1)] for g in range(4)]),
            pool_scale=dsm[1, :256], conv_db=dsm[2, :256], conv_ln_g=dsm[3, :256], conv_ln_b=dsm[4, :256],
            ret_gn_g=dsm[0], conv_dw=dcdw[:CONV_W])
        dh, hb = _mix_in_bwd(f"mix_in_bwd_{l}", dh_res, dz, a1[0], a1[2], wb[l])
        g_b = _dw_into(f"dw_in_{l}", g_b, dz, hb, D_IN,
                       [(WIN_ROWS * s, WIN_ROWS, lambda j, s=s: s, OFF_WIN) for s in range(N_SHARD)])
        exchanging_b, token = _rs_d2d_start(f"b{l}", g_b)
        dh, hid, dau, dffn, dgb = _ffn_bwd(f"ffn1_bwd_{l}", dh, a1[0], a1[1], a1[2] + token, au1, wa[l])
        g_ln_g[l][0], g_ln_b[l][0] = dgb[0], dgb[1]
        started["b", l], token = _rs_start_after(exchanging_b, dh, rs_plan)
        zero_rows = jnp.zeros((N_SHARD, A_ROWS - OFF_WIN, D), _WIRE) + token.astype(_WIRE)
        g_a = lax.dynamic_update_slice(g_a, zero_rows, (0, OFF_WIN, 0))
        g_a = ffn_grads(g_a, "ffn1", l, hb1, hid, dau, dffn)
        dy = dh
        if l > 0:
            exchanging_a, token = _rs_d2d_start(f"a{l}", g_a)
        else:
            started["a", l], token = _rs_start(f"a{l}", g_a, rs_plan)
    d_frames, d_head, dgb_in = _ln_in_bwd(dy, saved[0][0][0], saved[0][0][1], gb_in + token)
    grad_x = d_frames[None]

    small_parts = [
        d_head[PAD:ROW0],
        jnp.stack([g_small[l]["conv_dw"] for l in range(DEPTH)]),
        jnp.stack([jnp.stack(g_ln_g[l]) for l in range(DEPTH)]),
        jnp.stack([jnp.stack(g_ln_b[l]) for l in range(DEPTH)]),
        dgb_in[0], dgb_in[1],
        jnp.stack([g_small[l]["pool_w"] for l in range(DEPTH)]),
        jnp.stack([g_small[l]["pool_scale"] for l in range(DEPTH)]),
        jnp.stack([g_small[l]["conv_db"] for l in range(DEPTH)]),
        jnp.stack([g_small[l]["conv_ln_g"] for l in range(DEPTH)]),
        jnp.stack([g_small[l]["conv_ln_b"] for l in range(DEPTH)]),
        jnp.stack([g_small[l]["ret_gn_g"] for l in range(DEPTH)]),
    ]
    small_started, small_token = _small_start(_pack_rows(small_parts))

    keys, mines, after = (("b", 1), ("a", 1), ("b", 0), ("a", 0)), [], small_token
    for key in keys:
        after = _rs_end(started[key], after, rs_plan)
        mines.append(after)
    gsum = dict(zip(keys, _rs_share(mines)))
    g_big = [_unpack_shard(gsum["a", l], gsum["b", l]) for l in range(DEPTH)]
    grads = {n: jnp.stack([g_big[l][n] for l in range(DEPTH)]) for n in _BIG}
    red = _unpack_rows(_small_end(small_started, gsum[keys[-1]]), [p.shape for p in small_parts])
    grads["meta"] = lax.dynamic_slice_in_dim(red[0], 256 * chip, 256, axis=1)
    grads["conv_dw"] = lax.dynamic_slice_in_dim(red[1], 64 * chip, 64, axis=2)
    grads["ln_g"] = lax.dynamic_slice_in_dim(red[2], 256 * chip, 256, axis=2)
    grads["ln_b"] = lax.dynamic_slice_in_dim(red[3], 256 * chip, 256, axis=2)
    for n, v in zip(("ln_in_g", "ln_in_b", "pool_w", "pool_scale", "conv_db", "conv_ln_g", "conv_ln_b", "ret_gn_g"),
                    red[4:]):
        grads[n] = v

    names = ['meta', 'ln_in_g', 'ln_in_b', 'ffn1_w13', 'ffn1_w2', 'w_in', 'pool_w', 'pool_scale', 'conv_dw',
             'conv_db', 'conv_ln_g', 'conv_ln_b', 'conv_pw', 'ret_gn_g', 'w_out', 'ffn2_w13', 'ffn2_w2', 'ln_g', 'ln_b']
    ws = dict(meta=meta, ln_in_g=ln_in_g, ln_in_b=ln_in_b, ffn1_w13=ffn1_w13, ffn1_w2=ffn1_w2, w_in=w_in,
              pool_w=pool_w, pool_scale=pool_scale, conv_dw=conv_dw, conv_db=conv_db, conv_ln_g=conv_ln_g,
              conv_ln_b=conv_ln_b, conv_pw=conv_pw, ret_gn_g=ret_gn_g, w_out=w_out, ffn2_w13=ffn2_w13,
              ffn2_w2=ffn2_w2, ln_g=ln_g, ln_b=ln_b)
    ms = dict(meta=m_meta, ln_in_g=m_ln_in_g, ln_in_b=m_ln_in_b, ffn1_w13=m_ffn1_w13, ffn1_w2=m_ffn1_w2,
              w_in=m_w_in, pool_w=m_pool_w, pool_scale=m_pool_scale, conv_dw=m_conv_dw, conv_db=m_conv_db,
              conv_ln_g=m_conv_ln_g, conv_ln_b=m_conv_ln_b, conv_pw=m_conv_pw, ret_gn_g=m_ret_gn_g,
              w_out=m_w_out, ffn2_w13=m_ffn2_w13, ffn2_w2=m_ffn2_w2, ln_g=m_ln_g, ln_b=m_ln_b)
    vs = dict(meta=v_meta, ln_in_g=v_ln_in_g, ln_in_b=v_ln_in_b, ffn1_w13=v_ffn1_w13, ffn1_w2=v_ffn1_w2,
              w_in=v_w_in, pool_w=v_pool_w, pool_scale=v_pool_scale, conv_dw=v_conv_dw, conv_db=v_conv_db,
              conv_ln_g=v_conv_ln_g, conv_ln_b=v_conv_ln_b, conv_pw=v_conv_pw, ret_gn_g=v_ret_gn_g,
              w_out=v_w_out, ffn2_w13=v_ffn2_w13, ffn2_w2=v_ffn2_w2, ln_g=v_ln_g, ln_b=v_ln_b)
    delta, new_m, new_v = {}, {}, {}
    for n in _BIG:
        shp = ws[n].shape
        two = lambda a: a.reshape(-1, shp[-1])
        d_, m_, v_ = _adamw("adamw_" + n, two(ws[n]), two(grads[n]), two(ms[n]), two(vs[n]))
        delta[n], new_m[n], new_v[n] = d_.reshape(shp), m_.reshape(shp), v_.reshape(shp)
    small_names = [n for n in names if n not in _BIG]
    pk = lambda d: _pack_rows([d[n] for n in small_names])
    d_, m_, v_ = _adamw("adamw_small", pk(ws), pk(grads), pk(ms), pk(vs))
    shapes = [ws[n].shape for n in small_names]
    for n, a, b, c in zip(small_names, _unpack_rows(d_, shapes), _unpack_rows(m_, shapes), _unpack_rows(v_, shapes)):
        delta[n], new_m[n], new_v[n] = a, b, c

    return (loss, grad_x, *[grads[n] for n in names], *[delta[n] for n in names],
            *[new_m[n] for n in names], *[new_v[n] for n in names])
```
